```python
import jax, jax.numpy as jnp
from jax import lax
import numpy as np

D_MODEL = 1024
BATCH = 16
SEQ = 2048
DEPTH = 1

D_CONV = 1024
CONV_WIDTH = 31
RWKV_HEAD = 64
RWKV_HEADS = D_MODEL // RWKV_HEAD
D_RWKV = RWKV_HEADS * RWKV_HEAD
DECAY_LORA = 64
AAA_LORA = 64
TS_WIDTH = 3 * D_RWKV + DECAY_LORA + AAA_LORA
SPLITS = (D_CONV, 2 * D_CONV, 3 * D_CONV, 3 * D_CONV + TS_WIDTH, 3 * D_CONV + TS_WIDTH + D_RWKV, 3 * D_CONV + TS_WIDTH + D_RWKV + D_MODEL)
D_IN = 3 * D_CONV + TS_WIDTH + D_RWKV + 2 * D_MODEL
RWKV_SPLITS = (D_RWKV, 2 * D_RWKV, 3 * D_RWKV, 3 * D_RWKV + DECAY_LORA)

RMS_EPS = 1e-6
LN_EPS = 1e-5
GN_EPS = 64e-5
L2_EPS = 1e-12

kernel_name = 'hybrid_conformer_conv_rwkv7_adaln'


def _rmsnorm(x, g):
    xf = x.astype(jnp.float32)
    y = xf * lax.rsqrt(jnp.mean(xf * xf, axis=-1, keepdims=True) + RMS_EPS)
    return (y * g.astype(jnp.float32)).astype(x.dtype)


def _layernorm(x, g, b, eps):
    xf = x.astype(jnp.float32)
    mu = jnp.mean(xf, axis=-1, keepdims=True)
    var = jnp.mean(jnp.square(xf - mu), axis=-1, keepdims=True)
    y = (xf - mu) * lax.rsqrt(var + eps) * g.astype(jnp.float32) + b.astype(jnp.float32)
    return y.astype(x.dtype)


def _causal_depthwise_conv(u, w, b):
    y = lax.conv_general_dilated(
        u, w[:, None, :].astype(u.dtype), window_strides=(1,),
        padding=((CONV_WIDTH - 1, 0),), dimension_numbers=('NWC', 'WIO', 'NWC'),
        feature_group_count=u.shape[-1])
    return y + b


def _token_shift(z, mu):
    prev = jnp.pad(z, ((0, 0), (1, 0), (0, 0)))[:, :-1]
    return z + mu * (prev - z)


def _rwkv7_scan(r, decay, k, v, kk, b):
    bsz, _, h, n = r.shape

    def step(state, inp):
        r_t, w_t, k_t, v_t, kk_t, b_t = inp
        sk = jnp.einsum('bhvk,bhk->bhv', state, kk_t)
        state = (state * w_t[:, :, None, :] - sk[..., None] * b_t[:, :, None, :]
                 + v_t[..., None] * k_t[:, :, None, :])
        return state, jnp.einsum('bhvk,bhk->bhv', state, r_t)

    xs = tuple(jnp.moveaxis(t, 1, 0) for t in (r, decay, k, v, kk, b))
    s0 = jnp.zeros((bsz, h, n, n), jnp.float32)
    _, out = lax.scan(step, s0, xs)
    return jnp.moveaxis(out, 0, 1)


def _rwkv7_branch(ts, og, mu, w0, w2, a0, a2, k_k, k_a, r_k, gn_g, gn_b, w_o):
    f32 = jnp.float32
    bsz, seq, _ = ts.shape
    hd = (bsz, seq, RWKV_HEADS, RWKV_HEAD)
    r, k, v, w_low, a_low = jnp.split(_token_shift(ts, mu), RWKV_SPLITS, axis=-1)
    w_pre = (w0 + jnp.tanh(w_low) @ w2).astype(f32)
    decay = jnp.exp(-jnp.exp(-jax.nn.softplus(-w_pre) - 0.5))
    a = jax.nn.sigmoid((a0 + a_low @ a2).astype(f32))
    kk = (k * k_k).astype(f32).reshape(hd)
    kk = kk / jnp.maximum(jnp.sqrt(jnp.sum(kk * kk, axis=-1, keepdims=True)), L2_EPS)
    k = (k.astype(f32) * (1.0 + (a - 1.0) * k_a.astype(f32))).reshape(hd)
    r = r.astype(f32).reshape(hd)
    v = v.astype(f32).reshape(hd)
    a = a.reshape(hd)
    o = _rwkv7_scan(r, decay.reshape(hd), k, v, kk, kk * a)
    o = _layernorm(o, gn_g.reshape(RWKV_HEADS, RWKV_HEAD), gn_b.reshape(RWKV_HEADS, RWKV_HEAD), GN_EPS)
    o = o + jnp.sum(r * k * r_k.astype(f32), axis=-1, keepdims=True) * v
    o = o.reshape(bsz, seq, D_RWKV).astype(og.dtype) * jax.nn.silu(og)
    return o @ w_o


def _conv_branch(val, gate, og, conv_k, conv_b, ln_g, ln_b, w_o):
    u = val * jax.nn.sigmoid(gate)
    u = _causal_depthwise_conv(u, conv_k, conv_b)
    u = _layernorm(u, ln_g, ln_b, LN_EPS)
    u = jax.nn.silu(u) * jax.nn.silu(og)
    return u @ w_o


def _fwd_setup_inputs(seed: int = 0) -> dict:
    key = jax.random.key(seed)
    ks = jax.random.split(key, 24)
    f32 = jnp.float32
    nrm = lambda k, shape, s: jax.random.normal(k, shape, f32) * s
    L = DEPTH
    return {
        'x': nrm(ks[0], (BATCH, SEQ, D_MODEL), 1.0),
        'c': nrm(ks[1], (BATCH, D_MODEL), 1.0),
        'ada_w': nrm(ks[2], (L, D_MODEL, 3 * D_MODEL), 0.5 * D_MODEL ** -0.5),
        'ada_b': nrm(ks[3], (L, 3 * D_MODEL), 0.01),
        'norm_g': 1.0 + nrm(ks[4], (L, D_MODEL), 0.1),
        'w_in': nrm(ks[5], (L, D_MODEL, D_IN), D_MODEL ** -0.5),
        'conv_k': nrm(ks[6], (L, CONV_WIDTH, D_CONV), CONV_WIDTH ** -0.5),
        'conv_b': nrm(ks[7], (L, D_CONV), 0.01),
        'conv_ln_g': 1.0 + nrm(ks[8], (L, D_CONV), 0.1),
        'conv_ln_b': nrm(ks[9], (L, D_CONV), 0.01),
        'w_conv_out': nrm(ks[10], (L, D_CONV, D_MODEL), D_CONV ** -0.5),
        'rwkv_mu': jax.random.uniform(ks[11], (L, TS_WIDTH), f32, 0.0, 1.0),
        'rwkv_w0': jax.random.uniform(ks[12], (L, D_RWKV), f32, -6.0, -1.0),
        'rwkv_w2': nrm(ks[13], (L, DECAY_LORA, D_RWKV), 0.1 * DECAY_LORA ** -0.5),
        'rwkv_a0': nrm(ks[14], (L, D_RWKV), 0.1),
        'rwkv_a2': nrm(ks[15], (L, AAA_LORA, D_RWKV), 0.5 * AAA_LORA ** -0.5),
        'rwkv_k_k': 0.85 + nrm(ks[16], (L, D_RWKV), 0.05),
        'rwkv_k_a': 1.0 + nrm(ks[17], (L, D_RWKV), 0.05),
        'rwkv_r_k': nrm(ks[18], (L, RWKV_HEADS, RWKV_HEAD), 0.1),
        'rwkv_gn_g': 1.0 + nrm(ks[19], (L, D_RWKV), 0.1),
        'rwkv_gn_b': nrm(ks[20], (L, D_RWKV), 0.01),
        'w_rwkv_out': nrm(ks[21], (L, D_RWKV, D_MODEL), D_RWKV ** -0.5),
        'w_out': nrm(ks[22], (L, D_MODEL, D_MODEL), D_MODEL ** -0.5),
        'final_g': 1.0 + nrm(ks[23], (D_MODEL,), 0.1),
    }


def _fwd_reference(x, c, ada_w, ada_b, norm_g, w_in, conv_k, conv_b, conv_ln_g, conv_ln_b, w_conv_out,
              rwkv_mu, rwkv_w0, rwkv_w2, rwkv_a0, rwkv_a2, rwkv_k_k, rwkv_k_a, rwkv_r_k,
              rwkv_gn_g, rwkv_gn_b, w_rwkv_out, w_out, final_g):
    for l in range(DEPTH):
        mod = jnp.einsum('bd,de->be', jax.nn.silu(c), ada_w[l]) + ada_b[l]
        shift, scale, gate = jnp.split(mod, 3, axis=-1)
        h = _rmsnorm(x, norm_g[l]) * (1.0 + scale[:, None, :]) + shift[:, None, :]
        p = jnp.einsum('bsd,de->bse', h, w_in[l])
        c_val, c_gate, c_og, ts, r_og, g_conv, g_rwkv = jnp.split(p, SPLITS, axis=-1)
        y_conv = _conv_branch(c_val, c_gate, c_og, conv_k[l], conv_b[l], conv_ln_g[l], conv_ln_b[l],
                              w_conv_out[l])
        y_rwkv = _rwkv7_branch(ts, r_og, rwkv_mu[l], rwkv_w0[l], rwkv_w2[l], rwkv_a0[l], rwkv_a2[l],
                               rwkv_k_k[l], rwkv_k_a[l], rwkv_r_k[l], rwkv_gn_g[l], rwkv_gn_b[l],
                               w_rwkv_out[l])
        m = jax.nn.sigmoid(g_conv) * y_conv + jax.nn.sigmoid(g_rwkv) * y_rwkv
        out = jnp.einsum('bsd,de->bse', m, w_out[l])
        x = x + gate[:, None, :] * out
    return _rmsnorm(x, final_g)


import jax as _jax
import jax.numpy as _jnp

TWIN_FORMAT = 'train_step'
FWD_PARAMS = ['x', 'c', 'ada_w', 'ada_b', 'norm_g', 'w_in', 'conv_k', 'conv_b', 'conv_ln_g', 'conv_ln_b', 'w_conv_out', 'rwkv_mu', 'rwkv_w0', 'rwkv_w2', 'rwkv_a0', 'rwkv_a2', 'rwkv_k_k', 'rwkv_k_a', 'rwkv_r_k', 'rwkv_gn_g', 'rwkv_gn_b', 'w_rwkv_out', 'w_out', 'final_g']
TWIN_WEIGHTS = ['ada_w', 'ada_b', 'norm_g', 'w_in', 'conv_k', 'conv_b', 'conv_ln_g', 'conv_ln_b', 'w_conv_out', 'rwkv_mu', 'rwkv_w0', 'rwkv_w2', 'rwkv_a0', 'rwkv_a2', 'rwkv_k_k', 'rwkv_k_a', 'rwkv_r_k', 'rwkv_gn_g', 'rwkv_gn_b', 'w_rwkv_out', 'w_out', 'final_g']
TWIN_DIFF_INPUT = 'x'
TWIN_INPUTS = ['x', 'c', 'ada_w', 'ada_b', 'norm_g', 'w_in', 'conv_k', 'conv_b', 'conv_ln_g', 'conv_ln_b', 'w_conv_out', 'rwkv_mu', 'rwkv_w0', 'rwkv_w2', 'rwkv_a0', 'rwkv_a2', 'rwkv_k_k', 'rwkv_k_a', 'rwkv_r_k', 'rwkv_gn_g', 'rwkv_gn_b', 'w_rwkv_out', 'w_out', 'final_g', 'loss_target', 'm_ada_w', 'm_ada_b', 'm_norm_g', 'm_w_in', 'm_conv_k', 'm_conv_b', 'm_conv_ln_g', 'm_conv_ln_b', 'm_w_conv_out', 'm_rwkv_mu', 'm_rwkv_w0', 'm_rwkv_w2', 'm_rwkv_a0', 'm_rwkv_a2', 'm_rwkv_k_k', 'm_rwkv_k_a', 'm_rwkv_r_k', 'm_rwkv_gn_g', 'm_rwkv_gn_b', 'm_w_rwkv_out', 'm_w_out', 'm_final_g', 'v_ada_w', 'v_ada_b', 'v_norm_g', 'v_w_in', 'v_conv_k', 'v_conv_b', 'v_conv_ln_g', 'v_conv_ln_b', 'v_w_conv_out', 'v_rwkv_mu', 'v_rwkv_w0', 'v_rwkv_w2', 'v_rwkv_a0', 'v_rwkv_a2', 'v_rwkv_k_k', 'v_rwkv_k_a', 'v_rwkv_r_k', 'v_rwkv_gn_g', 'v_rwkv_gn_b', 'v_w_rwkv_out', 'v_w_out', 'v_final_g']
TWIN_OUTPUTS = ['loss', 'grad_x', 'grad_ada_w', 'grad_ada_b', 'grad_norm_g', 'grad_w_in', 'grad_conv_k', 'grad_conv_b', 'grad_conv_ln_g', 'grad_conv_ln_b', 'grad_w_conv_out', 'grad_rwkv_mu', 'grad_rwkv_w0', 'grad_rwkv_w2', 'grad_rwkv_a0', 'grad_rwkv_a2', 'grad_rwkv_k_k', 'grad_rwkv_k_a', 'grad_rwkv_r_k', 'grad_rwkv_gn_g', 'grad_rwkv_gn_b', 'grad_w_rwkv_out', 'grad_w_out', 'grad_final_g', 'delta_ada_w', 'delta_ada_b', 'delta_norm_g', 'delta_w_in', 'delta_conv_k', 'delta_conv_b', 'delta_conv_ln_g', 'delta_conv_ln_b', 'delta_w_conv_out', 'delta_rwkv_mu', 'delta_rwkv_w0', 'delta_rwkv_w2', 'delta_rwkv_a0', 'delta_rwkv_a2', 'delta_rwkv_k_k', 'delta_rwkv_k_a', 'delta_rwkv_r_k', 'delta_rwkv_gn_g', 'delta_rwkv_gn_b', 'delta_w_rwkv_out', 'delta_w_out', 'delta_final_g', 'new_m_ada_w', 'new_m_ada_b', 'new_m_norm_g', 'new_m_w_in', 'new_m_conv_k', 'new_m_conv_b', 'new_m_conv_ln_g', 'new_m_conv_ln_b', 'new_m_w_conv_out', 'new_m_rwkv_mu', 'new_m_rwkv_w0', 'new_m_rwkv_w2', 'new_m_rwkv_a0', 'new_m_rwkv_a2', 'new_m_rwkv_k_k', 'new_m_rwkv_k_a', 'new_m_rwkv_r_k', 'new_m_rwkv_gn_g', 'new_m_rwkv_gn_b', 'new_m_w_rwkv_out', 'new_m_w_out', 'new_m_final_g', 'new_v_ada_w', 'new_v_ada_b', 'new_v_norm_g', 'new_v_w_in', 'new_v_conv_k', 'new_v_conv_b', 'new_v_conv_ln_g', 'new_v_conv_ln_b', 'new_v_w_conv_out', 'new_v_rwkv_mu', 'new_v_rwkv_w0', 'new_v_rwkv_w2', 'new_v_rwkv_a0', 'new_v_rwkv_a2', 'new_v_rwkv_k_k', 'new_v_rwkv_k_a', 'new_v_rwkv_r_k', 'new_v_rwkv_gn_g', 'new_v_rwkv_gn_b', 'new_v_w_rwkv_out', 'new_v_w_out', 'new_v_final_g']
TWIN_LEAF_KINDS = {'loss': 'loss', 'grad_x': 'grad_x', 'grad_ada_w': 'grad_w', 'grad_ada_b': 'grad_w', 'grad_norm_g': 'grad_w', 'grad_w_in': 'grad_w', 'grad_conv_k': 'grad_w', 'grad_conv_b': 'grad_w', 'grad_conv_ln_g': 'grad_w', 'grad_conv_ln_b': 'grad_w', 'grad_w_conv_out': 'grad_w', 'grad_rwkv_mu': 'grad_w', 'grad_rwkv_w0': 'grad_w', 'grad_rwkv_w2': 'grad_w', 'grad_rwkv_a0': 'grad_w', 'grad_rwkv_a2': 'grad_w', 'grad_rwkv_k_k': 'grad_w', 'grad_rwkv_k_a': 'grad_w', 'grad_rwkv_r_k': 'grad_w', 'grad_rwkv_gn_g': 'grad_w', 'grad_rwkv_gn_b': 'grad_w', 'grad_w_rwkv_out': 'grad_w', 'grad_w_out': 'grad_w', 'grad_final_g': 'grad_w', 'delta_ada_w': 'delta_w', 'delta_ada_b': 'delta_w', 'delta_norm_g': 'delta_w', 'delta_w_in': 'delta_w', 'delta_conv_k': 'delta_w', 'delta_conv_b': 'delta_w', 'delta_conv_ln_g': 'delta_w', 'delta_conv_ln_b': 'delta_w', 'delta_w_conv_out': 'delta_w', 'delta_rwkv_mu': 'delta_w', 'delta_rwkv_w0': 'delta_w', 'delta_rwkv_w2': 'delta_w', 'delta_rwkv_a0': 'delta_w', 'delta_rwkv_a2': 'delta_w', 'delta_rwkv_k_k': 'delta_w', 'delta_rwkv_k_a': 'delta_w', 'delta_rwkv_r_k': 'delta_w', 'delta_rwkv_gn_g': 'delta_w', 'delta_rwkv_gn_b': 'delta_w', 'delta_w_rwkv_out': 'delta_w', 'delta_w_out': 'delta_w', 'delta_final_g': 'delta_w', 'new_m_ada_w': 'new_m', 'new_m_ada_b': 'new_m', 'new_m_norm_g': 'new_m', 'new_m_w_in': 'new_m', 'new_m_conv_k': 'new_m', 'new_m_conv_b': 'new_m', 'new_m_conv_ln_g': 'new_m', 'new_m_conv_ln_b': 'new_m', 'new_m_w_conv_out': 'new_m', 'new_m_rwkv_mu': 'new_m', 'new_m_rwkv_w0': 'new_m', 'new_m_rwkv_w2': 'new_m', 'new_m_rwkv_a0': 'new_m', 'new_m_rwkv_a2': 'new_m', 'new_m_rwkv_k_k': 'new_m', 'new_m_rwkv_k_a': 'new_m', 'new_m_rwkv_r_k': 'new_m', 'new_m_rwkv_gn_g': 'new_m', 'new_m_rwkv_gn_b': 'new_m', 'new_m_w_rwkv_out': 'new_m', 'new_m_w_out': 'new_m', 'new_m_final_g': 'new_m', 'new_v_ada_w': 'new_v', 'new_v_ada_b': 'new_v', 'new_v_norm_g': 'new_v', 'new_v_w_in': 'new_v', 'new_v_conv_k': 'new_v', 'new_v_conv_b': 'new_v', 'new_v_conv_ln_g': 'new_v', 'new_v_conv_ln_b': 'new_v', 'new_v_w_conv_out': 'new_v', 'new_v_rwkv_mu': 'new_v', 'new_v_rwkv_w0': 'new_v', 'new_v_rwkv_w2': 'new_v', 'new_v_rwkv_a0': 'new_v', 'new_v_rwkv_a2': 'new_v', 'new_v_rwkv_k_k': 'new_v', 'new_v_rwkv_k_a': 'new_v', 'new_v_rwkv_r_k': 'new_v', 'new_v_rwkv_gn_g': 'new_v', 'new_v_rwkv_gn_b': 'new_v', 'new_v_w_rwkv_out': 'new_v', 'new_v_w_out': 'new_v', 'new_v_final_g': 'new_v'}


def _forward(args):
    return _fwd_reference(*[args[k] for k in FWD_PARAMS])


def _output_shape():
    out = _jax.eval_shape(lambda: _forward(_fwd_setup_inputs(0)))
    return out.shape, out.dtype

N_MICROBATCH = 1
ADAM_LR = 0.001
ADAM_B1 = 0.9
ADAM_B2 = 0.999
ADAM_EPS = 1e-08
ADAM_WD = 0.01
ADAM_STEP = 10
PER_EXAMPLE_BATCH_AXIS = {'x': 0, 'c': 0, 'loss_target': 0}
SHARED_INPUTS = []
_WEIGHT_DTYPES = {'ada_w': _jnp.float32, 'ada_b': _jnp.float32, 'norm_g': _jnp.float32, 'w_in': _jnp.float32, 'conv_k': _jnp.float32, 'conv_b': _jnp.float32, 'conv_ln_g': _jnp.float32, 'conv_ln_b': _jnp.float32, 'w_conv_out': _jnp.float32, 'rwkv_mu': _jnp.float32, 'rwkv_w0': _jnp.float32, 'rwkv_w2': _jnp.float32, 'rwkv_a0': _jnp.float32, 'rwkv_a2': _jnp.float32, 'rwkv_k_k': _jnp.float32, 'rwkv_k_a': _jnp.float32, 'rwkv_r_k': _jnp.float32, 'rwkv_gn_g': _jnp.float32, 'rwkv_gn_b': _jnp.float32, 'w_rwkv_out': _jnp.float32, 'w_out': _jnp.float32, 'final_g': _jnp.float32}
MOMENT_SCALE = {'ada_w': 5.494438e-02, 'ada_b': 9.663733e-02, 'norm_g': 4.787507e-02, 'w_in': 1.664883e-02, 'conv_k': 1.296145e-02, 'conv_b': 2.312754e-02, 'conv_ln_g': 1.419128e-02, 'conv_ln_b': 1.293936e-02, 'w_conv_out': 1.231881e-02, 'rwkv_mu': 3.475973e-02, 'rwkv_w0': 9.353394e-03, 'rwkv_w2': 2.007434e-03, 'rwkv_a0': 9.143305e-03, 'rwkv_a2': 9.152677e-03, 'rwkv_k_k': 5.924797e-02, 'rwkv_k_a': 4.297338e-02, 'rwkv_r_k': 6.269525e-02, 'rwkv_gn_g': 2.110079e-02, 'rwkv_gn_b': 2.061069e-02, 'w_rwkv_out': 2.136982e-02, 'w_out': 2.468235e-02, 'final_g': 3.203256e+01}


def _to_microbatches(a, axis):
    t = _jnp.moveaxis(a, axis, 0)
    t = t.reshape((N_MICROBATCH, t.shape[0] // N_MICROBATCH) + t.shape[1:])
    return _jnp.moveaxis(t, 1, axis + 1)


def setup_inputs(seed: int = 0) -> dict:
    inp = _fwd_setup_inputs(seed)
    key = _jax.random.fold_in(_jax.random.key(seed), 7919)
    shape, _ = _output_shape()
    out = dict(inp)
    out["loss_target"] = _jax.random.normal(_jax.random.fold_in(key, 0), shape, _jnp.float32)
    for i, name in enumerate(TWIN_WEIGHTS):
        w = inp[name].astype(_jnp.float32)
        if MOMENT_SCALE is None:
            s = _jnp.sqrt(_jnp.mean(_jnp.square(w)) + 1e-30)
        else:
            s = MOMENT_SCALE[name]
        km, kv = _jax.random.split(_jax.random.fold_in(key, i + 1))
        out[name] = w
        out["m_" + name] = s * _jax.random.normal(km, w.shape, _jnp.float32)
        out["v_" + name] = (s * s) * _jax.random.uniform(kv, w.shape, _jnp.float32, 0.5, 1.5)
    if N_MICROBATCH > 1:
        for name, axis in PER_EXAMPLE_BATCH_AXIS.items():
            out[name] = _to_microbatches(out[name], axis)
    return {'x': out['x'], 'c': out['c'], 'ada_w': out['ada_w'], 'ada_b': out['ada_b'], 'norm_g': out['norm_g'], 'w_in': out['w_in'], 'conv_k': out['conv_k'], 'conv_b': out['conv_b'], 'conv_ln_g': out['conv_ln_g'], 'conv_ln_b': out['conv_ln_b'], 'w_conv_out': out['w_conv_out'], 'rwkv_mu': out['rwkv_mu'], 'rwkv_w0': out['rwkv_w0'], 'rwkv_w2': out['rwkv_w2'], 'rwkv_a0': out['rwkv_a0'], 'rwkv_a2': out['rwkv_a2'], 'rwkv_k_k': out['rwkv_k_k'], 'rwkv_k_a': out['rwkv_k_a'], 'rwkv_r_k': out['rwkv_r_k'], 'rwkv_gn_g': out['rwkv_gn_g'], 'rwkv_gn_b': out['rwkv_gn_b'], 'w_rwkv_out': out['w_rwkv_out'], 'w_out': out['w_out'], 'final_g': out['final_g'], 'loss_target': out['loss_target'], 'm_ada_w': out['m_ada_w'], 'm_ada_b': out['m_ada_b'], 'm_norm_g': out['m_norm_g'], 'm_w_in': out['m_w_in'], 'm_conv_k': out['m_conv_k'], 'm_conv_b': out['m_conv_b'], 'm_conv_ln_g': out['m_conv_ln_g'], 'm_conv_ln_b': out['m_conv_ln_b'], 'm_w_conv_out': out['m_w_conv_out'], 'm_rwkv_mu': out['m_rwkv_mu'], 'm_rwkv_w0': out['m_rwkv_w0'], 'm_rwkv_w2': out['m_rwkv_w2'], 'm_rwkv_a0': out['m_rwkv_a0'], 'm_rwkv_a2': out['m_rwkv_a2'], 'm_rwkv_k_k': out['m_rwkv_k_k'], 'm_rwkv_k_a': out['m_rwkv_k_a'], 'm_rwkv_r_k': out['m_rwkv_r_k'], 'm_rwkv_gn_g': out['m_rwkv_gn_g'], 'm_rwkv_gn_b': out['m_rwkv_gn_b'], 'm_w_rwkv_out': out['m_w_rwkv_out'], 'm_w_out': out['m_w_out'], 'm_final_g': out['m_final_g'], 'v_ada_w': out['v_ada_w'], 'v_ada_b': out['v_ada_b'], 'v_norm_g': out['v_norm_g'], 'v_w_in': out['v_w_in'], 'v_conv_k': out['v_conv_k'], 'v_conv_b': out['v_conv_b'], 'v_conv_ln_g': out['v_conv_ln_g'], 'v_conv_ln_b': out['v_conv_ln_b'], 'v_w_conv_out': out['v_w_conv_out'], 'v_rwkv_mu': out['v_rwkv_mu'], 'v_rwkv_w0': out['v_rwkv_w0'], 'v_rwkv_w2': out['v_rwkv_w2'], 'v_rwkv_a0': out['v_rwkv_a0'], 'v_rwkv_a2': out['v_rwkv_a2'], 'v_rwkv_k_k': out['v_rwkv_k_k'], 'v_rwkv_k_a': out['v_rwkv_k_a'], 'v_rwkv_r_k': out['v_rwkv_r_k'], 'v_rwkv_gn_g': out['v_rwkv_gn_g'], 'v_rwkv_gn_b': out['v_rwkv_gn_b'], 'v_w_rwkv_out': out['v_w_rwkv_out'], 'v_w_out': out['v_w_out'], 'v_final_g': out['v_final_g']}


def _loss(weights, diff, rest, loss_target):
    with _jax.named_scope("forward"):
        args = {**rest, TWIN_DIFF_INPUT: diff, **{k: w.astype(_WEIGHT_DTYPES[k]) for k, w in weights.items()}}
        y = _forward(args)
    with _jax.named_scope("loss_head"):
        err = _jnp.square(y.astype(_jnp.float32) - loss_target)
        return 0.5 * _jnp.sum(_jnp.mean(err, axis=-1)) if err.ndim else 0.5 * err


def _adamw(w, g, m, v):
    m = ADAM_B1 * m + (1.0 - ADAM_B1) * g
    v = ADAM_B2 * v + (1.0 - ADAM_B2) * _jnp.square(g)
    m_hat = m / (1.0 - ADAM_B1 ** ADAM_STEP)
    v_hat = v / (1.0 - ADAM_B2 ** ADAM_STEP)
    delta = -ADAM_LR * (m_hat / (_jnp.sqrt(v_hat) + ADAM_EPS) + ADAM_WD * w)
    return delta, m, v


def reference(x, c, ada_w, ada_b, norm_g, w_in, conv_k, conv_b, conv_ln_g, conv_ln_b, w_conv_out, rwkv_mu, rwkv_w0, rwkv_w2, rwkv_a0, rwkv_a2, rwkv_k_k, rwkv_k_a, rwkv_r_k, rwkv_gn_g, rwkv_gn_b, w_rwkv_out, w_out, final_g, loss_target, m_ada_w, m_ada_b, m_norm_g, m_w_in, m_conv_k, m_conv_b, m_conv_ln_g, m_conv_ln_b, m_w_conv_out, m_rwkv_mu, m_rwkv_w0, m_rwkv_w2, m_rwkv_a0, m_rwkv_a2, m_rwkv_k_k, m_rwkv_k_a, m_rwkv_r_k, m_rwkv_gn_g, m_rwkv_gn_b, m_w_rwkv_out, m_w_out, m_final_g, v_ada_w, v_ada_b, v_norm_g, v_w_in, v_conv_k, v_conv_b, v_conv_ln_g, v_conv_ln_b, v_w_conv_out, v_rwkv_mu, v_rwkv_w0, v_rwkv_w2, v_rwkv_a0, v_rwkv_a2, v_rwkv_k_k, v_rwkv_k_a, v_rwkv_r_k, v_rwkv_gn_g, v_rwkv_gn_b, v_w_rwkv_out, v_w_out, v_final_g):
    given = dict(x=x, c=c, ada_w=ada_w, ada_b=ada_b, norm_g=norm_g, w_in=w_in, conv_k=conv_k, conv_b=conv_b, conv_ln_g=conv_ln_g, conv_ln_b=conv_ln_b, w_conv_out=w_conv_out, rwkv_mu=rwkv_mu, rwkv_w0=rwkv_w0, rwkv_w2=rwkv_w2, rwkv_a0=rwkv_a0, rwkv_a2=rwkv_a2, rwkv_k_k=rwkv_k_k, rwkv_k_a=rwkv_k_a, rwkv_r_k=rwkv_r_k, rwkv_gn_g=rwkv_gn_g, rwkv_gn_b=rwkv_gn_b, w_rwkv_out=w_rwkv_out, w_out=w_out, final_g=final_g, loss_target=loss_target, m_ada_w=m_ada_w, m_ada_b=m_ada_b, m_norm_g=m_norm_g, m_w_in=m_w_in, m_conv_k=m_conv_k, m_conv_b=m_conv_b, m_conv_ln_g=m_conv_ln_g, m_conv_ln_b=m_conv_ln_b, m_w_conv_out=m_w_conv_out, m_rwkv_mu=m_rwkv_mu, m_rwkv_w0=m_rwkv_w0, m_rwkv_w2=m_rwkv_w2, m_rwkv_a0=m_rwkv_a0, m_rwkv_a2=m_rwkv_a2, m_rwkv_k_k=m_rwkv_k_k, m_rwkv_k_a=m_rwkv_k_a, m_rwkv_r_k=m_rwkv_r_k, m_rwkv_gn_g=m_rwkv_gn_g, m_rwkv_gn_b=m_rwkv_gn_b, m_w_rwkv_out=m_w_rwkv_out, m_w_out=m_w_out, m_final_g=m_final_g, v_ada_w=v_ada_w, v_ada_b=v_ada_b, v_norm_g=v_norm_g, v_w_in=v_w_in, v_conv_k=v_conv_k, v_conv_b=v_conv_b, v_conv_ln_g=v_conv_ln_g, v_conv_ln_b=v_conv_ln_b, v_w_conv_out=v_w_conv_out, v_rwkv_mu=v_rwkv_mu, v_rwkv_w0=v_rwkv_w0, v_rwkv_w2=v_rwkv_w2, v_rwkv_a0=v_rwkv_a0, v_rwkv_a2=v_rwkv_a2, v_rwkv_k_k=v_rwkv_k_k, v_rwkv_k_a=v_rwkv_k_a, v_rwkv_r_k=v_rwkv_r_k, v_rwkv_gn_g=v_rwkv_gn_g, v_rwkv_gn_b=v_rwkv_gn_b, v_w_rwkv_out=v_w_rwkv_out, v_w_out=v_w_out, v_final_g=v_final_g)
    weights = {n: given[n] for n in TWIN_WEIGHTS}
    shared = {n: given[n] for n in SHARED_INPUTS}
    per_example = {n: given[n] for n in ['x', 'c']}
    grad_fn = _jax.value_and_grad(_loss, argnums=(0, 1))

    def one_microbatch(ex, loss_target):
        ex = dict(ex)
        diff = ex.pop(TWIN_DIFF_INPUT)
        return grad_fn(weights, diff, {**shared, **ex}, loss_target)

    if N_MICROBATCH == 1:
        loss, (grad_w, grad_x) = one_microbatch(per_example, given["loss_target"])
    else:
        def body(carry, xs):
            loss_sum, grad_sum = carry
            l_k, (gw_k, gx_k) = one_microbatch(xs[0], xs[1])
            with _jax.named_scope("update"):
                return (loss_sum + l_k, _jax.tree.map(_jnp.add, grad_sum, gw_k)), gx_k

        init = (_jnp.zeros((), _jnp.float32), _jax.tree.map(_jnp.zeros_like, weights))
        (loss, grad_w), grad_x = _jax.lax.scan(body, init, (per_example, given["loss_target"]))
    with _jax.named_scope("update"):
        delta_w, new_m, new_v = {}, {}, {}
        for n in TWIN_WEIGHTS:
            delta_w[n], new_m[n], new_v[n] = _adamw(weights[n], grad_w[n], given["m_" + n], given["v_" + n])
    return (loss, grad_x, *[grad_w[n] for n in TWIN_WEIGHTS], *[delta_w[n] for n in TWIN_WEIGHTS],
            *[new_m[n] for n in TWIN_WEIGHTS], *[new_v[n] for n in TWIN_WEIGHTS])
```

```python
import functools

import numpy as np
import jax
import jax.numpy as jnp
from jax import lax
from jax.experimental import pallas as pl
from jax.experimental.pallas import tpu as pltpu

F32 = jnp.float32
BF16 = jnp.bfloat16
HI = lax.Precision.HIGHEST
MESH = pl.DeviceIdType.MESH
ANY = pl.BlockSpec(memory_space=pl.ANY)

D = 1024
NH = 16
HN = 64
LORA = 128
DMAIN = 9 * D
CH = 64
CW = 31
NCHIP = 4
NDEV = 8
VMEM_LIMIT = 56 * 1024 * 1024

RMS_EPS = 1e-6
LN_EPS = 1e-5
GN_EPS = 64e-5
L2_EPS = 1e-12
ADAM_LR = 0.001
ADAM_B1 = 0.9
ADAM_B2 = 0.999
ADAM_EPS = 1e-08
ADAM_WD = 0.01
ADAM_STEP = 10

SMALL = (("ada_b", 3072), ("norm_g", 1024), ("conv_b", 1024), ("conv_ln_g", 1024), ("conv_ln_b", 1024),
         ("rwkv_mu", 3200), ("rwkv_w0", 1024), ("rwkv_a0", 1024), ("rwkv_k_k", 1024), ("rwkv_k_a", 1024),
         ("rwkv_r_k", 1024), ("rwkv_gn_g", 1024), ("rwkv_gn_b", 1024), ("final_g", 1024))
NSMALL = sum(n for _, n in SMALL)

WEIGHTS = ['ada_w', 'ada_b', 'norm_g', 'w_in', 'conv_k', 'conv_b', 'conv_ln_g', 'conv_ln_b', 'w_conv_out', 'rwkv_mu',
           'rwkv_w0', 'rwkv_w2', 'rwkv_a0', 'rwkv_a2', 'rwkv_k_k', 'rwkv_k_a', 'rwkv_r_k', 'rwkv_gn_g', 'rwkv_gn_b',
           'w_rwkv_out', 'w_out', 'final_g']


def _cparams(sem=None, **kw):
    if sem is not None:
        kw["dimension_semantics"] = sem
    return pltpu.CompilerParams(vmem_limit_bytes=VMEM_LIMIT, **kw)


def _dot(a, b, prec=None):
    return jnp.dot(a, b, preferred_element_type=F32, precision=prec)


def _dot_nt(a, b, prec=None):
    return lax.dot_general(a, b, (((1,), (1,)), ((), ())), preferred_element_type=F32, precision=prec)


def _dot_tn(a, b, prec=None):
    return lax.dot_general(a, b, (((0,), (0,)), ((), ())), preferred_element_type=F32, precision=prec)


def _sigmoid(z):
    return 1.0 / (1.0 + jnp.exp(-z))


def _silu(z):
    return z * _sigmoid(z)


def _matmul(a, b, mode, name, tm, tn, tk):
    if mode == "nn":
        (M, K), N = a.shape, b.shape[1]
        a_spec = pl.BlockSpec((tm, tk), lambda j, i, k: (i, k))
        b_spec = pl.BlockSpec((tk, tn), lambda j, i, k: (k, j))
        f = _dot
    elif mode == "nt":
        (M, K), N = a.shape, b.shape[0]
        a_spec = pl.BlockSpec((tm, tk), lambda j, i, k: (i, k))
        b_spec = pl.BlockSpec((tn, tk), lambda j, i, k: (j, k))
        f = _dot_nt
    else:
        (K, M), N = a.shape, b.shape[1]
        a_spec = pl.BlockSpec((tk, tm), lambda j, i, k: (k, i))
        b_spec = pl.BlockSpec((tk, tn), lambda j, i, k: (k, j))
        f = _dot_tn
    assert M % tm == 0 and N % tn == 0 and K % tk == 0, (name, M, N, K)

    def body(a_ref, b_ref, o_ref):
        @pl.when(pl.program_id(2) == 0)
        def _():
            o_ref[...] = jnp.zeros_like(o_ref)
        o_ref[...] += f(a_ref[...], b_ref[...])

    return pl.pallas_call(
        body, name=name, grid=(N // tn, M // tm, K // tk),
        in_specs=[a_spec, b_spec],
        out_specs=pl.BlockSpec((tm, tn), lambda j, i, k: (i, j)),
        out_shape=jax.ShapeDtypeStruct((M, N), F32),
        compiler_params=_cparams(("parallel", "parallel", "arbitrary")),
    )(a, b)


def _rows(fn, name, T, S, tm, rows, bpars, gpars, outs, baccs, gaccs):
    nb = T // S
    tps = S // tm
    n_r, n_b, n_g, n_o, n_ba, n_ga = len(rows), len(bpars), len(gpars), len(outs), len(baccs), len(gaccs)

    def body(*refs):
        r_refs = refs[:n_r]
        b_refs = refs[n_r:n_r + n_b]
        g_refs = refs[n_r + n_b:n_r + n_b + n_g]
        o_refs = refs[n_r + n_b + n_g:n_r + n_b + n_g + n_o]
        ba_refs = refs[n_r + n_b + n_g + n_o:n_r + n_b + n_g + n_o + n_ba]
        ga_refs = refs[n_r + n_b + n_g + n_o + n_ba:]
        i = pl.program_id(0)
        o_vals, ba_vals, ga_vals = fn([r[...] for r in r_refs], [r[...] for r in b_refs], [r[...] for r in g_refs])
        for r, v in zip(o_refs, o_vals):
            r[...] = v.astype(r.dtype)
        if n_ba:
            @pl.when(i % tps == 0)
            def _():
                for r in ba_refs:
                    r[...] = jnp.zeros_like(r)
            for r, v in zip(ba_refs, ba_vals):
                r[...] += v.reshape(r.shape)
        if n_ga:
            @pl.when(i == 0)
            def _():
                for r in ga_refs:
                    r[...] = jnp.zeros_like(r)
            for r, v in zip(ga_refs, ga_vals):
                r[...] += v.reshape(r.shape)

    in_specs = [pl.BlockSpec((tm, w), functools.partial(lambda i, cb: (i, cb), cb=cb)) for _, w, cb in rows]
    in_specs += [pl.BlockSpec((None, 1, p.shape[-1]), lambda i: (i // tps, 0, 0)) for p in bpars]
    in_specs += [pl.BlockSpec(p.shape, lambda i: (0, 0)) for p in gpars]
    out_specs = [pl.BlockSpec((tm, w), lambda i: (i, 0)) for w, _ in outs]
    out_specs += [pl.BlockSpec((None, 1, w), lambda i: (i // tps, 0, 0)) for w in baccs]
    out_specs += [pl.BlockSpec(s, lambda i: (0, 0)) for s in gaccs]
    out_shape = [jax.ShapeDtypeStruct((T, w), dt) for w, dt in outs]
    out_shape += [jax.ShapeDtypeStruct((nb, 1, w), F32) for w in baccs]
    out_shape += [jax.ShapeDtypeStruct(s, F32) for s in gaccs]
    res = pl.pallas_call(
        body, name=name, grid=(T // tm,), in_specs=in_specs, out_specs=out_specs, out_shape=out_shape,
        compiler_params=_cparams(("arbitrary",)),
    )(*[a for a, _, _ in rows], *bpars, *gpars)
    return res[:n_o], res[n_o:n_o + n_ba], res[n_o + n_ba:]


def _gsum(z, G):
    return _dot(z, G, HI)


def _s1(x, g, scale, shift):
    y = x * lax.rsqrt(jnp.mean(x * x, axis=-1, keepdims=True) + RMS_EPS)
    return (y * g) * (1.0 + scale) + shift


def _s3(uc, og, cb, lg, lb):
    u = uc + cb
    mu = jnp.mean(u, axis=-1, keepdims=True)
    d = u - mu
    var = jnp.mean(d * d, axis=-1, keepdims=True)
    y = d * lax.rsqrt(var + LN_EPS) * lg + lb
    return _silu(y) * _silu(og)


def _s4(r0, k0, v0, l0, pr, pk, pv, plo, mu_r, mu_k, mu_v, mu_l, w0, w2p, a0, a2p, k_k, k_a, G):
    r = r0 + mu_r * (pr - r0)
    k = k0 + mu_k * (pk - k0)
    v = v0 + mu_v * (pv - v0)
    lo = l0 + mu_l * (plo - l0)
    w_pre = w0 + _dot(jnp.tanh(lo), w2p, HI)
    lw = -np.float32(np.exp(-0.5)) * _sigmoid(w_pre)
    a = _sigmoid(a0 + _dot(lo, a2p, HI))
    kkr = k * k_k
    ss = _gsum(kkr * kkr, G)
    kk = kkr / jnp.maximum(jnp.sqrt(ss), L2_EPS)
    k2 = k * (1.0 + (a - 1.0) * k_a)
    return r, lw, k2, v, kk, kk * a


def _s5(o, r, k2, v, og, gg, gb, rk, G):
    mu = _gsum(o, G) * (1.0 / HN)
    d = o - mu
    var = _gsum(d * d, G) * (1.0 / HN)
    y = d * lax.rsqrt(var + GN_EPS) * gg + gb
    bonus = _gsum(r * k2 * rk, G)
    return (y + bonus * v) * _silu(og)


def _s6(yc, yr, gc, gr):
    return _sigmoid(gc) * yc + _sigmoid(gr) * yr


def _s7(x, out, tgt, gate, fg):
    x2 = x + gate * out
    y = x2 * lax.rsqrt(jnp.mean(x2 * x2, axis=-1, keepdims=True) + RMS_EPS) * fg
    e = y - tgt
    return 0.5 * jnp.sum(jnp.mean(e * e, axis=-1))


def _chunk(st, r, lw, k, v, kk, b, cm):
    tri, strict, eye = cm[0], cm[1], cm[2]
    cum = _dot(tri, lw, HI)
    ein = jnp.exp(-cum)
    rt = r * jnp.exp(cum)
    kkt = kk * jnp.exp(cum - lw)
    kh = k * ein
    bh = b * ein
    a_kb = strict * _dot_nt(kkt, bh, HI)
    a_kk = strict * _dot_nt(kkt, kh, HI)
    a_rk = tri * _dot_nt(rt, kh, HI)
    a_rb = tri * _dot_nt(rt, bh, HI)
    xi = eye - cm[3] * a_kb
    for lvl in range(1, 6):
        xi = xi - _dot(_dot(xi, cm[3 + lvl] * a_kb, HI), xi, HI)
    u = _dot(xi, _dot_nt(kkt, st, HI) + _dot(a_kk, v, HI), HI)
    o = _dot_nt(rt, st, HI) + _dot(a_rk, v, HI) - _dot(a_rb, u, HI)
    ec = jnp.exp(jnp.sum(lw, axis=0, keepdims=True))
    st2 = st * ec + _dot_tn(v, kh * ec, HI) - _dot_tn(u, bh * ec, HI)
    return o, st2


def _chunk_consts():
    t = np.arange(CH)[:, None]
    s = np.arange(CH)[None, :]
    mats = [(t >= s), (t > s), (t == s)]
    for lvl in range(6):
        sz = 1 << lvl
        mats.append(((t // sz) % 2 == 1) & ((s // sz) == (t // sz) - 1))
    mats.append(np.zeros((CH, CH), bool))
    return np.stack(mats).astype(np.float32)


def _adamw(w, g, m, v):
    m = ADAM_B1 * m + (1.0 - ADAM_B1) * g
    v = ADAM_B2 * v + (1.0 - ADAM_B2) * (g * g)
    m_hat = m / (1.0 - ADAM_B1 ** ADAM_STEP)
    v_hat = v / (1.0 - ADAM_B2 ** ADAM_STEP)
    delta = -ADAM_LR * (m_hat / (jnp.sqrt(v_hat) + ADAM_EPS) + ADAM_WD * w)
    return delta, m, v


CT = 128
RB = 64
WIN = RB + 32


def _conv_fwd(pm, ck, T, S):
    nb = T // S

    def body(val_ref, gate_ref, ck_ref, out_ref, ubuf):
        ubuf[0:32, :] = jnp.zeros((32, CT), F32)
        ubuf[32:, :] = val_ref[...] * _sigmoid(gate_ref[...])

        def blk(rb, carry):
            base = pl.multiple_of(rb * RB, RB)
            win = ubuf[pl.ds(base, WIN), :]
            acc = jnp.zeros((RB, CT), F32)
            for j in range(CW):
                acc = acc + ck_ref[j:j + 1, :] * pltpu.roll(win, (WIN - (2 + j)) % WIN, 0)[0:RB, :]
            out_ref[pl.ds(base, RB), :] = acc
            return carry

        lax.fori_loop(0, S // RB, blk, 0)

    return pl.pallas_call(
        body, name="conv_fwd", grid=(D // CT, nb),
        in_specs=[pl.BlockSpec((S, CT), lambda ct, b: (b, ct)),
                  pl.BlockSpec((S, CT), lambda ct, b: (b, D // CT + ct)),
                  pl.BlockSpec((32, CT), lambda ct, b: (0, ct))],
        out_specs=pl.BlockSpec((S, CT), lambda ct, b: (b, ct)),
        out_shape=jax.ShapeDtypeStruct((T, D), F32),
        scratch_shapes=[pltpu.VMEM((S + 32, CT), F32)],
        compiler_params=_cparams(("parallel", "arbitrary")),
    )(pm, pm, ck)


def _conv_bwd(pm, duc, ck, T, S):
    nb = T // S

    def body(val_ref, gate_ref, duc_ref, ck_ref, dval_ref, dgate_ref, dck_ref, ubuf, dbuf, acc):
        b = pl.program_id(1)
        ubuf[0:32, :] = jnp.zeros((32, CT), F32)
        ubuf[32:, :] = val_ref[...] * _sigmoid(gate_ref[...])
        dbuf[0:S, :] = duc_ref[...]
        dbuf[S:, :] = jnp.zeros((32, CT), F32)
        acc[...] = jnp.zeros_like(acc)

        def blk(rb, carry):
            base = pl.multiple_of(rb * RB, RB)
            uwin = ubuf[pl.ds(base, WIN), :]
            dwin = dbuf[pl.ds(base, WIN), :]
            dblk = dwin[0:RB, :]
            du = jnp.zeros((RB, CT), F32)
            for j in range(CW):
                du = du + ck_ref[j:j + 1, :] * pltpu.roll(dwin, (WIN - (CW - 1 - j)) % WIN, 0)[0:RB, :]
                ush = pltpu.roll(uwin, (WIN - (2 + j)) % WIN, 0)[0:RB, :]
                acc[j] += jnp.sum((dblk * ush).reshape(RB // 8, 8, CT), axis=0)
            val = val_ref[pl.ds(base, RB), :]
            sg = _sigmoid(gate_ref[pl.ds(base, RB), :])
            dval_ref[pl.ds(base, RB), :] = du * sg
            dgate_ref[pl.ds(base, RB), :] = du * val * sg * (1.0 - sg)
            return carry

        lax.fori_loop(0, S // RB, blk, 0)

        @pl.when(b == 0)
        def _():
            dck_ref[...] = jnp.zeros_like(dck_ref)
        for j in range(CW):
            dck_ref[j:j + 1, :] += jnp.sum(acc[j], axis=0, keepdims=True)

    return pl.pallas_call(
        body, name="conv_bwd", grid=(D // CT, nb),
        in_specs=[pl.BlockSpec((S, CT), lambda ct, b: (b, ct)),
                  pl.BlockSpec((S, CT), lambda ct, b: (b, D // CT + ct)),
                  pl.BlockSpec((S, CT), lambda ct, b: (b, ct)),
                  pl.BlockSpec((32, CT), lambda ct, b: (0, ct))],
        out_specs=[pl.BlockSpec((S, CT), lambda ct, b: (b, ct)),
                   pl.BlockSpec((S, CT), lambda ct, b: (b, ct)),
                   pl.BlockSpec((32, CT), lambda ct, b: (0, ct))],
        out_shape=[jax.ShapeDtypeStruct((T, D), F32), jax.ShapeDtypeStruct((T, D), F32),
                   jax.ShapeDtypeStruct((32, D), F32)],
        scratch_shapes=[pltpu.VMEM((S + 32, CT), F32), pltpu.VMEM((S + 32, CT), F32), pltpu.VMEM((32, 8, CT), F32)],
        compiler_params=_cparams(("parallel", "arbitrary")),
    )(pm, pm, duc, ck)


def _scan_fwd(ins, cm, nb, S):
    nc = S // CH
    blk = pl.BlockSpec((None, None, CH, HN), lambda b, h, i: (b, h, i, 0))

    def body(r_ref, lw_ref, k_ref, v_ref, kk_ref, b_ref, cm_ref, o_ref, hs_ref, st):
        @pl.when(pl.program_id(2) == 0)
        def _():
            st[...] = jnp.zeros_like(st)
        s0 = st[...]
        hs_ref[...] = s0
        o, s1 = _chunk(s0, r_ref[...], lw_ref[...], k_ref[...], v_ref[...], kk_ref[...], b_ref[...], cm_ref[...])
        o_ref[...] = o
        st[...] = s1

    return pl.pallas_call(
        body, name="scan_fwd", grid=(nb, NH, nc),
        in_specs=[blk] * 6 + [pl.BlockSpec(cm.shape, lambda b, h, i: (0, 0, 0))],
        out_specs=[blk, pl.BlockSpec((None, None, None, HN, HN), lambda b, h, i: (b, h, i, 0, 0))],
        out_shape=[jax.ShapeDtypeStruct((nb, NH, S, HN), F32), jax.ShapeDtypeStruct((nb, NH, nc, HN, HN), F32)],
        scratch_shapes=[pltpu.VMEM((HN, HN), F32)],
        compiler_params=_cparams(("parallel", "parallel", "arbitrary")),
    )(*ins, cm)


def _scan_bwd(ins, hs, do, cm, nb, S):
    nc = S // CH
    blk = pl.BlockSpec((None, None, CH, HN), lambda b, h, i: (b, h, nc - 1 - i, 0))

    def body(r_ref, lw_ref, k_ref, v_ref, kk_ref, b_ref, hs_ref, do_ref, cm_ref,
             dr_ref, dlw_ref, dk_ref, dv_ref, dkk_ref, db_ref, dst):
        @pl.when(pl.program_id(2) == 0)
        def _():
            dst[...] = jnp.zeros_like(dst)
        cmv = cm_ref[...]
        f = lambda s0, r, lw, k, v, kk, b: _chunk(s0, r, lw, k, v, kk, b, cmv)
        _, vjp = jax.vjp(f, hs_ref[...], r_ref[...], lw_ref[...], k_ref[...], v_ref[...], kk_ref[...], b_ref[...])
        ds0, dr, dlw, dk, dv, dkk, db = vjp((do_ref[...], dst[...]))
        dst[...] = ds0
        dr_ref[...] = dr
        dlw_ref[...] = dlw
        dk_ref[...] = dk
        dv_ref[...] = dv
        dkk_ref[...] = dkk
        db_ref[...] = db

    return pl.pallas_call(
        body, name="scan_bwd", grid=(nb, NH, nc),
        in_specs=[blk] * 6 + [pl.BlockSpec((None, None, None, HN, HN), lambda b, h, i: (b, h, nc - 1 - i, 0, 0)), blk,
                              pl.BlockSpec(cm.shape, lambda b, h, i: (0, 0, 0))],
        out_specs=[blk] * 6,
        out_shape=[jax.ShapeDtypeStruct((nb, NH, S, HN), F32)] * 6,
        scratch_shapes=[pltpu.VMEM((HN, HN), F32)],
        compiler_params=_cparams(("parallel", "parallel", "arbitrary")),
    )(*ins, hs, do, cm)


def _ew(fn, name, ins, n_out, tm):
    R, W = ins[0].shape[-2:]
    tm = min(tm, R)
    assert R % tm == 0

    def body(*refs):
        vals = fn(*[r[...] for r in refs[:len(ins)]])
        for r, v in zip(refs[len(ins):], vals):
            r[...] = v

    def spec(a):
        if a.ndim == 3:
            return pl.BlockSpec((a.shape[0], tm, W), lambda i: (0, i, 0))
        return pl.BlockSpec((tm, W), lambda i: (i, 0))

    return pl.pallas_call(
        body, name=name, grid=(R // tm,), in_specs=[spec(a) for a in ins],
        out_specs=[pl.BlockSpec((tm, W), lambda i: (i, 0))] * n_out,
        out_shape=[jax.ShapeDtypeStruct((R, W), F32)] * n_out,
        compiler_params=_cparams(("parallel",)),
    )(*ins)


def _sum_slots(r):
    s = r[0]
    for j in range(1, r.shape[0]):
        s = s + r[j]
    return s


def _place():
    x, y, c = lax.axis_index("x"), lax.axis_index("y"), lax.axis_index("c")
    return x, y, c


def _flip(v, d):
    return 1 - v if d else v


CHIP_PEERS = ((1, 0), (0, 1), (1, 1))
DEV_PEERS = tuple((dx, dy, dc) for dx in (0, 1) for dy in (0, 1) for dc in (0, 1))[1:]


def _comm_call(name, ins, out_shapes, plan, n_rem):
    n_in = len(ins)

    def body(*refs):
        in_refs, out_refs = refs[:n_in], refs[n_in:n_in + len(out_shapes)]
        send_sems, recv_sems, loc_sems = refs[n_in + len(out_shapes):]
        loc, rem = plan(in_refs, out_refs, _place())
        assert len(rem) == n_rem and len(loc) <= n_in, (name, len(loc), len(rem))
        copies = [pltpu.make_async_copy(s, d, loc_sems.at[i]) for i, (s, d) in enumerate(loc)]
        rcopies = [pltpu.make_async_remote_copy(src_ref=s, dst_ref=d, send_sem=send_sems.at[i], recv_sem=recv_sems.at[i],
                                                device_id=peer, device_id_type=MESH)
                   for i, (s, d, peer) in enumerate(rem)]
        for cp in copies + rcopies:
            cp.start()
        for cp in rcopies:
            cp.wait_send()
        for cp in rcopies:
            cp.wait_recv()
        for cp in copies:
            cp.wait()

    return pl.pallas_call(
        body, name=name, in_specs=[ANY] * n_in, out_specs=[ANY] * len(out_shapes), out_shape=out_shapes,
        scratch_shapes=[pltpu.SemaphoreType.DMA((n_rem,)), pltpu.SemaphoreType.DMA((n_rem,)),
                        pltpu.SemaphoreType.DMA((n_in,))],
        compiler_params=pltpu.CompilerParams(has_side_effects=True),
    )(*ins)


def _gather_plan(in_refs, out_refs, place):
    x, y, c = place
    chip, dev = 2 * x + y, 4 * x + 2 * y + c
    loc = [(in_refs[0], out_refs[0].at[dev])] + [(s, d.at[chip]) for s, d in zip(in_refs[1:], out_refs[1:])]
    rem = [(in_refs[0], out_refs[0].at[dev], (_flip(x, dx), _flip(y, dy), _flip(c, dc))) for dx, dy, dc in DEV_PEERS]
    for s, d in zip(in_refs[1:], out_refs[1:]):
        rem += [(s, d.at[chip], (_flip(x, dx), _flip(y, dy), c)) for dx, dy in CHIP_PEERS]
    return loc, rem


def _scatter_plan(n_all, in_refs, out_refs, place):
    x, y, c = place
    chip, dev = 2 * x + y, 4 * x + 2 * y + c
    loc, rem = [], []
    for s, d in zip(in_refs[:n_all], out_refs[:n_all]):
        loc.append((s.at[dev], d.at[dev]))
        for dx, dy, dc in DEV_PEERS:
            px, py, pc = _flip(x, dx), _flip(y, dy), _flip(c, dc)
            rem.append((s.at[4 * px + 2 * py + pc], d.at[dev], (px, py, pc)))
    for s, d in zip(in_refs[n_all:], out_refs[n_all:]):
        loc.append((s.at[chip], d.at[chip]))
        for dx, dy in CHIP_PEERS:
            px, py = _flip(x, dx), _flip(y, dy)
            rem.append((s.at[2 * px + py], d.at[chip], (px, py, c)))
    return loc, rem


def _sibling_plan(in_refs, out_refs, place):
    x, y, c = place
    return [], [(s, d, (x, y, 1 - c)) for s, d in zip(in_refs, out_refs)]


def _bshape(a, nb):
    return a.reshape(nb, 1, a.shape[-1])


def _to_heads(a, nb, S):
    return a.reshape(nb, S, NH, HN).transpose(0, 2, 1, 3)


def _from_heads(a, nb, S):
    return a.transpose(0, 2, 1, 3).reshape(nb * S, D)


def _shift_down(a, nb, S):
    a3 = a.reshape(nb, S, a.shape[-1])
    return jnp.pad(a3[:, :-1], ((0, 0), (1, 0), (0, 0))).reshape(a.shape)


def _shift_up(a, nb, S):
    a3 = a.reshape(nb, S, a.shape[-1])
    return jnp.pad(a3[:, 1:], ((0, 0), (0, 1), (0, 0))).reshape(a.shape)


def _local_step(x2d, tgt, mod, wmain, wlora, wco, wro, wo, ck, w2, a2, small, nb, S):
    T = nb * S
    shift, scale, gate = (_bshape(mod[:, i * D:(i + 1) * D], nb) for i in range(3))
    G = jnp.asarray(np.kron(np.eye(NH, dtype=np.float32), np.ones((HN, HN), np.float32)))
    cm = jnp.asarray(_chunk_consts())
    ckp = jnp.pad(ck, ((0, 1), (0, 0)))
    zpad = jnp.zeros((64, D), F32)
    w2p = jnp.concatenate([w2, zpad], axis=0)
    a2p = jnp.concatenate([zpad, a2], axis=0)
    mu = small["rwkv_mu"]
    mu_r, mu_k, mu_v, mu_l = mu[:, 0:D], mu[:, D:2 * D], mu[:, 2 * D:3 * D], mu[:, 3 * D:]
    g4 = [mu_r, mu_k, mu_v, mu_l, small["rwkv_w0"], w2p, small["rwkv_a0"], a2p, small["rwkv_k_k"], small["rwkv_k_a"], G]
    g5 = [small["rwkv_gn_g"], small["rwkv_gn_b"], small["rwkv_r_k"], G]
    g3 = [small["conv_b"], small["conv_ln_g"], small["conv_ln_b"]]

    (h,), _, _ = _rows(lambda r, b, g: ([_s1(r[0], g[0], b[0], b[1])], [], []), "pre_fwd", T, S, 256,
                       [(x2d, D, 0)], [scale, shift], [small["norm_g"]], [(D, BF16)], [], [])
    pm = _matmul(h, wmain, "nn", "proj_main", 512, 1024, D)
    plo = _matmul(h, wlora, "nn", "proj_lora", 512, LORA, D)
    uc = _conv_fwd(pm, ckp, T, S)
    (uo,), _, _ = _rows(lambda r, b, g: ([_s3(r[0], r[1], *g)], [], []), "conv_post_fwd", T, S, 256,
                        [(uc, D, 0), (pm, D, 2)], [], g3, [(D, BF16)], [], [])
    yc = _matmul(uo, wco, "nn", "conv_out", 512, 1024, D)
    prkv = _shift_down(pm[:, 3 * D:6 * D], nb, S)
    plop = _shift_down(plo, nb, S)
    rows4 = [(pm, D, 3), (pm, D, 4), (pm, D, 5), (plo, LORA, 0), (prkv, D, 0), (prkv, D, 1), (prkv, D, 2), (plop, LORA, 0)]
    sc_in, _, _ = _rows(lambda r, b, g: (list(_s4(*r, *g)), [], []), "rwkv_pre_fwd", T, S, 128,
                        rows4, [], g4, [(D, F32)] * 6, [], [])
    sc_hm = [_to_heads(a, nb, S) for a in sc_in]
    o_hm, hs = _scan_fwd(sc_hm, cm, nb, S)
    o = _from_heads(o_hm, nb, S)
    rows5 = [(o, D, 0), (sc_in[0], D, 0), (sc_in[2], D, 0), (sc_in[3], D, 0), (pm, D, 6)]
    (o2,), _, _ = _rows(lambda r, b, g: ([_s5(*r, *g)], [], []), "rwkv_post_fwd", T, S, 128,
                        rows5, [], g5, [(D, BF16)], [], [])
    yr = _matmul(o2, wro, "nn", "rwkv_out", 512, 1024, D)
    rows6 = [(yc, D, 0), (yr, D, 0), (pm, D, 7), (pm, D, 8)]
    (m,), _, _ = _rows(lambda r, b, g: ([_s6(*r)], [], []), "merge_fwd", T, S, 256, rows6, [], [], [(D, BF16)], [], [])
    out = _matmul(m, wo, "nn", "out_proj", 512, 1024, D)

    def head(r, b, g):
        loss, (dx, dout, dgate, dfg) = jax.value_and_grad(_s7, argnums=(0, 1, 3, 4))(r[0], r[1], r[2], b[0], g[0])
        return [dx, dout], [dgate], [dfg, jnp.full((1, 128), loss, F32)]

    (dx_res, dout), (dgate,), (d_final_g, loss_v) = _rows(
        head, "head", T, S, 256, [(x2d, D, 0), (out, D, 0), (tgt, D, 0)], [gate], [small["final_g"]],
        [(D, F32), (D, BF16)], [D], [(1, D), (1, 128)])

    d_wo = _matmul(m, dout, "tn", "d_w_out", 512, 1024, 512)
    dm = _matmul(dout, wo, "nt", "d_merge", 512, 1024, D)

    def merge_bwd(r, b, g):
        _, vjp = jax.vjp(_s6, *r[:4])
        dyc, dyr, dgc, dgr = vjp(r[4])
        return [dyc, dyr, dgc, dgr], [], []

    (dyc, dyr, dgc, dgr), _, _ = _rows(merge_bwd, "merge_bwd", T, S, 256, rows6 + [(dm, D, 0)], [], [],
                                       [(D, BF16), (D, BF16), (D, BF16), (D, BF16)], [], [])
    d_wco = _matmul(uo, dyc, "tn", "d_w_conv_out", 512, 1024, 512)
    d_wro = _matmul(o2, dyr, "tn", "d_w_rwkv_out", 512, 1024, 512)
    duo = _matmul(dyc, wco, "nt", "d_conv_act", 512, 1024, D)
    do2 = _matmul(dyr, wro, "nt", "d_rwkv_act", 512, 1024, D)

    def conv_post_bwd(r, b, g):
        _, vjp = jax.vjp(_s3, r[0], r[1], *g)
        duc, dog, dcb, dlg, dlb = vjp(r[2])
        return [duc, dog], [], [dcb, dlg, dlb]

    (duc, dcog), _, (d_cb, d_lg, d_lb) = _rows(conv_post_bwd, "conv_post_bwd", T, S, 256,
                                               [(uc, D, 0), (pm, D, 2), (duo, D, 0)], [], g3,
                                               [(D, F32), (D, BF16)], [], [(1, D)] * 3)
    dval, dgt, d_ckp = _conv_bwd(pm, duc, ckp, T, S)

    def rwkv_post_bwd(r, b, g):
        _, vjp = jax.vjp(lambda *z: _s5(*z, g[3]), *r[:5], *g[:3])
        res = vjp(r[5])
        return list(res[:5]), [], list(res[5:8])

    (do, dr_b, dk_b, dv_b, drog), _, (d_gg, d_gb, d_rk) = _rows(
        rwkv_post_bwd, "rwkv_post_bwd", T, S, 128, rows5 + [(do2, D, 0)], [], g5,
        [(D, F32)] * 4 + [(D, BF16)], [], [(1, D)] * 3)
    dsc_hm = _scan_bwd(sc_hm, hs, _to_heads(do, nb, S), cm, nb, S)
    dsc = [_from_heads(a, nb, S) for a in dsc_hm]

    def rwkv_pre_bwd(r, b, g):
        _, vjp = jax.vjp(lambda *z: _s4(*z, g[10]), *r[:8], *g[:10])
        ct = (r[8] + r[14], r[9], r[10] + r[15], r[11] + r[16], r[12], r[13])
        res = vjp(ct)
        return list(res[:8]), [], list(res[8:18])

    rows4b = rows4 + [(a, D, 0) for a in dsc] + [(dr_b, D, 0), (dk_b, D, 0), (dv_b, D, 0)]
    gshapes = [(1, D), (1, D), (1, D), (1, LORA), (1, D), (LORA, D), (1, D), (LORA, D), (1, D), (1, D)]
    dts, _, gts = _rows(rwkv_pre_bwd, "rwkv_pre_bwd", T, S, 128, rows4b, [], g4,
                        [(D, F32)] * 3 + [(LORA, F32)] + [(D, F32)] * 3 + [(LORA, F32)], [], gshapes)
    dr0, dk0, dv0, dl0, dpr, dpk, dpv, dpl = dts
    d_mu_r, d_mu_k, d_mu_v, d_mu_l, d_w0, d_w2p, d_a0, d_a2p, d_kk, d_ka = gts
    sh = [_shift_up(a, nb, S) for a in (dpr, dpk, dpv, dpl)]

    def assemble(r, b, g):
        main = jnp.concatenate([r[0], r[1], r[2], r[3] + r[4], r[5] + r[6], r[7] + r[8], r[9], r[10], r[11]], axis=1)
        return [main, r[12] + r[13]], [], []

    rows_a = [(dval, D, 0), (dgt, D, 0), (dcog, D, 0), (dr0, D, 0), (sh[0], D, 0), (dk0, D, 0), (sh[1], D, 0),
              (dv0, D, 0), (sh[2], D, 0), (drog, D, 0), (dgc, D, 0), (dgr, D, 0), (dl0, LORA, 0), (sh[3], LORA, 0)]
    (dpm, dplo), _, _ = _rows(assemble, "assemble_dp", T, S, 128, rows_a, [], [], [(DMAIN, BF16), (LORA, BF16)], [], [])
    d_wmain = _matmul(h, dpm, "tn", "d_w_main", 512, 1024, 512)
    d_wlora = _matmul(h, dplo, "tn", "d_w_lora", 512, LORA, 512)
    dh_m = _matmul(dpm, wmain, "nt", "d_h_main", 512, 1024, 1024)
    dh_l = _matmul(dplo, wlora, "nt", "d_h_lora", 512, 1024, LORA)

    def pre_bwd(r, b, g):
        _, vjp = jax.vjp(_s1, r[0], g[0], b[0], b[1])
        dx, dg, dscale, dshift = vjp(r[1] + r[2])
        return [dx + r[3]], [dscale, dshift], [dg]

    (gx,), (dscale, dshift), (d_ng,) = _rows(pre_bwd, "pre_bwd", T, S, 256,
                                             [(x2d, D, 0), (dh_m, D, 0), (dh_l, D, 0), (dx_res, D, 0)],
                                             [scale, shift], [small["norm_g"]], [(D, F32)], [D, D], [(1, D)])
    dmod = jnp.concatenate([dshift, dscale, dgate], axis=-1).reshape(nb, 3 * D)
    d_small = {"norm_g": d_ng, "conv_b": d_cb, "conv_ln_g": d_lg, "conv_ln_b": d_lb,
               "rwkv_mu": jnp.concatenate([d_mu_r, d_mu_k, d_mu_v, d_mu_l], axis=1),
               "rwkv_w0": d_w0, "rwkv_a0": d_a0, "rwkv_k_k": d_kk, "rwkv_k_a": d_ka, "rwkv_r_k": d_rk,
               "rwkv_gn_g": d_gg, "rwkv_gn_b": d_gb, "final_g": d_final_g}
    big = {"wmain": d_wmain, "wlora": d_wlora, "wco": d_wco, "wro": d_wro, "wo": d_wo,
           "ck": d_ckp[:CW], "w2": d_w2p[:64], "a2": d_a2p[64:]}
    return loss_v[0, 0], gx, dmod, big, d_small


def _step(a):
    nb, S, _ = a["x"].shape
    T = nb * S
    x_i, y_i, c_i = _place()
    chip = 2 * x_i + y_i
    w_in = a["w_in"][0]
    WS = w_in.shape[1]
    small_w = {n: a[n].reshape(1, sz) for n, sz in SMALL}

    g_ins = [a["c"], w_in.astype(BF16), a["w_conv_out"][0].astype(BF16), a["w_rwkv_out"][0].astype(BF16),
             a["w_out"][0].astype(BF16), a["conv_k"][0], a["rwkv_w2"][0], a["rwkv_a2"][0]]
    g_out = [jax.ShapeDtypeStruct((NDEV,) + g_ins[0].shape, F32)]
    g_out += [jax.ShapeDtypeStruct((NCHIP,) + t.shape, t.dtype) for t in g_ins[1:]]
    c_all, win_g, wco_g, wro_g, wo_g, ck_g, w2_g, a2_g = _comm_call("gather_weights", g_ins, g_out, _gather_plan, 7 + 3 * (len(g_ins) - 1))
    c_all = c_all.reshape(NDEV * nb, D)
    win_full = jnp.concatenate([win_g[j] for j in range(NCHIP)], axis=1)
    wmain = jnp.concatenate([win_full[:, :6 * D], win_full[:, 6 * D + LORA:]], axis=1)
    wlora = win_full[:, 6 * D:6 * D + LORA]
    wco, wro, wo = (t.reshape(D, D) for t in (wco_g, wro_g, wo_g))
    ck = jnp.concatenate([ck_g[j] for j in range(NCHIP)], axis=1)
    w2 = jnp.concatenate([w2_g[j] for j in range(NCHIP)], axis=1)
    a2 = jnp.concatenate([a2_g[j] for j in range(NCHIP)], axis=1)

    ada_w = a["ada_w"][0]
    MW = ada_w.shape[1]
    ada_b_loc = lax.dynamic_slice(a["ada_b"], (0, chip * MW), (1, MW))

    def mod_body(c_ref, w_ref, b_ref, o_ref):
        o_ref[...] = _dot(_silu(c_ref[...]), w_ref[...], HI) + b_ref[...]

    modp = pl.pallas_call(mod_body, name="ada_mod", out_shape=jax.ShapeDtypeStruct((NDEV * nb, MW), F32),
                          compiler_params=_cparams())(c_all, ada_w, ada_b_loc)
    (mod_g,) = _comm_call("scatter_mod", [modp.reshape(NDEV, nb, MW)],
                          [jax.ShapeDtypeStruct((NDEV, nb, MW), F32)],
                          functools.partial(_scatter_plan, 1), 7)
    mod = mod_g.reshape(NCHIP, 2, nb, MW)
    mod = mod[:, 0].transpose(1, 0, 2).reshape(nb, NCHIP * MW)

    loss_p, gx, dmod, big, d_small = _local_step(
        a["x"].reshape(T, D), a["loss_target"].reshape(T, D), mod, wmain, wlora, wco, wro, wo, ck, w2, a2,
        small_w, nb, S)
    loss = lax.psum(loss_p, ("x", "y", "c"))

    d_small["ada_b"] = _colsum(dmod)
    small_vec = jnp.concatenate([d_small[n] for n, _ in SMALL], axis=1)
    dmod_s = dmod.reshape(nb, NCHIP, MW).transpose(1, 0, 2)
    dmod_s = jnp.repeat(dmod_s, 2, axis=0)
    small_s = jnp.broadcast_to(small_vec[None], (NDEV, 1, NSMALL))
    d_win = jnp.concatenate([big["wmain"][:, :6 * D], big["wlora"], big["wmain"][:, 6 * D:]], axis=1)
    win_s = jnp.stack([d_win[:, j * WS:(j + 1) * WS] for j in range(NCHIP)])
    RW = D // NCHIP
    row_s = [big[n].reshape(NCHIP, RW, D) for n in ("wco", "wro", "wo")]
    col_s = [jnp.stack([big[n][:, j * RW:(j + 1) * RW] for j in range(NCHIP)]) for n in ("ck", "w2", "a2")]
    s_ins = [dmod_s, small_s, win_s] + row_s + col_s
    s_out = [jax.ShapeDtypeStruct(t.shape, F32) for t in s_ins]
    got = _comm_call("scatter_grads", s_ins, s_out, functools.partial(_scatter_plan, 2), 14 + 3 * (len(s_ins) - 2))
    dmod_all, small_all = got[0].reshape(NDEV * nb, MW), got[1].reshape(NDEV, NSMALL)
    parts = []
    for i, t in enumerate(got[2:]):
        (p,) = _ew(lambda r: [_sum_slots(r)], "chip_sum_%d" % i, [t], 1, 128)
        parts.append(p)
    sib = _comm_call("swap_partials", parts, [jax.ShapeDtypeStruct(p.shape, F32) for p in parts], _sibling_plan, len(parts))

    outs = {}

    def upd(name, w, g_parts, tm=128):
        shp = a[name].shape
        w2d = w.reshape(g_parts[0].shape[-2:])
        m2d, v2d = a["m_" + name].reshape(w2d.shape), a["v_" + name].reshape(w2d.shape)

        def fn(w_, m_, v_, *gp):
            g = gp[0] if gp[0].ndim == 2 else _sum_slots(gp[0])
            for e in gp[1:]:
                g = g + e
            return [g, *_adamw(w_, g, m_, v_)]

        res = _ew(fn, "adamw_" + name, [w2d, m2d, v2d, *g_parts], 4, tm)
        outs[name] = [r.reshape(shp) for r in res]

    for name, p, s in zip(("w_in", "w_conv_out", "w_rwkv_out", "w_out", "conv_k", "rwkv_w2", "rwkv_a2"), parts, sib):
        upd(name, a[name][0], [p, s])

    def adaw_body(c_ref, dm_ref, w_ref, m_ref, v_ref, g_ref, d_ref, m2_ref, v2_ref):
        g = _dot_tn(_silu(c_ref[...]), dm_ref[...], HI)
        g_ref[...] = g
        d_ref[...], m2_ref[...], v2_ref[...] = _adamw(w_ref[...], g, m_ref[...], v_ref[...])

    res = pl.pallas_call(adaw_body, name="adamw_ada_w", out_shape=[jax.ShapeDtypeStruct((D, MW), F32)] * 4,
                         compiler_params=_cparams())(c_all, dmod_all, ada_w, a["m_ada_w"][0], a["v_ada_w"][0])
    outs["ada_w"] = [r.reshape(a["ada_w"].shape) for r in res]

    wv, mv, vv = (jnp.concatenate([a[p + n].reshape(1, sz) for n, sz in SMALL], axis=1) for p in ("", "m_", "v_"))
    def small_fn(w_, m_, v_, gs):
        g = _sum_slots(gs)
        return [g, *_adamw(w_, g, m_, v_)]

    res = _ew(small_fn, "adamw_small", [wv, mv, vv, small_all.reshape(NDEV, 1, NSMALL)], 4, 8)
    off = 0
    for n, sz in SMALL:
        outs[n] = [r[:, off:off + sz].reshape(a[n].shape) for r in res]
        off += sz

    return (loss, gx.reshape(nb, S, D), *[outs[n][0] for n in WEIGHTS], *[outs[n][1] for n in WEIGHTS],
            *[outs[n][2] for n in WEIGHTS], *[outs[n][3] for n in WEIGHTS])


def _colsum(dmod):
    def body(d_ref, o_ref):
        o_ref[...] = jnp.sum(d_ref[...], axis=0, keepdims=True)
    return pl.pallas_call(body, name="ada_b_rowsum", out_shape=jax.ShapeDtypeStruct((1, dmod.shape[1]), F32),
                          compiler_params=_cparams())(dmod)


def kernel(x, c, ada_w, ada_b, norm_g, w_in, conv_k, conv_b, conv_ln_g, conv_ln_b, w_conv_out, rwkv_mu, rwkv_w0, rwkv_w2, rwkv_a0, rwkv_a2, rwkv_k_k, rwkv_k_a, rwkv_r_k, rwkv_gn_g, rwkv_gn_b, w_rwkv_out, w_out, final_g, loss_target, m_ada_w, m_ada_b, m_norm_g, m_w_in, m_conv_k, m_conv_b, m_conv_ln_g, m_conv_ln_b, m_w_conv_out, m_rwkv_mu, m_rwkv_w0, m_rwkv_w2, m_rwkv_a0, m_rwkv_a2, m_rwkv_k_k, m_rwkv_k_a, m_rwkv_r_k, m_rwkv_gn_g, m_rwkv_gn_b, m_w_rwkv_out, m_w_out, m_final_g, v_ada_w, v_ada_b, v_norm_g, v_w_in, v_conv_k, v_conv_b, v_conv_ln_g, v_conv_ln_b, v_w_conv_out, v_rwkv_mu, v_rwkv_w0, v_rwkv_w2, v_rwkv_a0, v_rwkv_a2, v_rwkv_k_k, v_rwkv_k_a, v_rwkv_r_k, v_rwkv_gn_g, v_rwkv_gn_b, v_w_rwkv_out, v_w_out, v_final_g):
    return _step(dict(locals()))
```

```python
import functools

import numpy as np
import jax
import jax.numpy as jnp
from jax import lax
from jax.experimental import pallas as pl
from jax.experimental.pallas import tpu as pltpu

F32 = jnp.float32
BF16 = jnp.bfloat16
HI = lax.Precision.HIGHEST
MESH = pl.DeviceIdType.MESH
ANY = pl.BlockSpec(memory_space=pl.ANY)

D = 1024
NH = 16
HN = 64
LORA = 128
DMAIN = 9 * D
CH = 64
CW = 31
NCHIP = 4
NDEV = 8
VMEM_LIMIT = 56 * 1024 * 1024

RMS_EPS = 1e-6
LN_EPS = 1e-5
GN_EPS = 64e-5
L2_EPS = 1e-12
ADAM_LR = 0.001
ADAM_B1 = 0.9
ADAM_B2 = 0.999
ADAM_EPS = 1e-08
ADAM_WD = 0.01
ADAM_STEP = 10

SMALL = (("ada_b", 3072), ("norm_g", 1024), ("conv_b", 1024), ("conv_ln_g", 1024), ("conv_ln_b", 1024),
         ("rwkv_mu", 3200), ("rwkv_w0", 1024), ("rwkv_a0", 1024), ("rwkv_k_k", 1024), ("rwkv_k_a", 1024),
         ("rwkv_r_k", 1024), ("rwkv_gn_g", 1024), ("rwkv_gn_b", 1024), ("final_g", 1024))
NSMALL = sum(n for _, n in SMALL)

WEIGHTS = ['ada_w', 'ada_b', 'norm_g', 'w_in', 'conv_k', 'conv_b', 'conv_ln_g', 'conv_ln_b', 'w_conv_out', 'rwkv_mu',
           'rwkv_w0', 'rwkv_w2', 'rwkv_a0', 'rwkv_a2', 'rwkv_k_k', 'rwkv_k_a', 'rwkv_r_k', 'rwkv_gn_g', 'rwkv_gn_b',
           'w_rwkv_out', 'w_out', 'final_g']


def _cparams(sem=None, **kw):
    if sem is not None:
        kw["dimension_semantics"] = sem
    return pltpu.CompilerParams(vmem_limit_bytes=VMEM_LIMIT, **kw)


def _dot(a, b, prec=None):
    return jnp.dot(a, b, preferred_element_type=F32, precision=prec)


def _dot_nt(a, b, prec=None):
    return lax.dot_general(a, b, (((1,), (1,)), ((), ())), preferred_element_type=F32, precision=prec)


def _dot_tn(a, b, prec=None):
    return lax.dot_general(a, b, (((0,), (0,)), ((), ())), preferred_element_type=F32, precision=prec)


def _pdot(f, a, b, p):
    if p == "hi":
        return f(a, b, HI)
    ah, bh = a.astype(BF16), b.astype(BF16)
    if p == "bf":
        return f(ah, bh)
    al, bl = (a - ah.astype(F32)).astype(BF16), (b - bh.astype(F32)).astype(BF16)
    return f(ah, bh) + (f(ah, bl) + f(al, bh))


P_SCORE = "b3"
P_INV = "bf"
P_APPLY = "bf"


def _sigmoid(z):
    return 1.0 / (1.0 + jnp.exp(-z))


def _silu(z):
    return z * _sigmoid(z)


def _matmul(a, b, mode, name, tm, tn, tk):
    if mode == "nn":
        (M, K), N = a.shape, b.shape[1]
        a_spec = pl.BlockSpec((tm, tk), lambda j, i, k: (i, k))
        b_spec = pl.BlockSpec((tk, tn), lambda j, i, k: (k, j))
        f = _dot
    elif mode == "nt":
        (M, K), N = a.shape, b.shape[0]
        a_spec = pl.BlockSpec((tm, tk), lambda j, i, k: (i, k))
        b_spec = pl.BlockSpec((tn, tk), lambda j, i, k: (j, k))
        f = _dot_nt
    else:
        (K, M), N = a.shape, b.shape[1]
        a_spec = pl.BlockSpec((tk, tm), lambda j, i, k: (k, i))
        b_spec = pl.BlockSpec((tk, tn), lambda j, i, k: (k, j))
        f = _dot_tn
    assert M % tm == 0 and N % tn == 0 and K % tk == 0, (name, M, N, K)

    def body(a_ref, b_ref, o_ref):
        @pl.when(pl.program_id(2) == 0)
        def _():
            o_ref[...] = jnp.zeros_like(o_ref)
        o_ref[...] += f(a_ref[...], b_ref[...])

    return pl.pallas_call(
        body, name=name, grid=(N // tn, M // tm, K // tk),
        in_specs=[a_spec, b_spec],
        out_specs=pl.BlockSpec((tm, tn), lambda j, i, k: (i, j)),
        out_shape=jax.ShapeDtypeStruct((M, N), F32),
        compiler_params=_cparams(("parallel", "parallel", "arbitrary")),
    )(a, b)


def _rows(fn, name, T, S, tm, rows, bpars, gpars, outs, baccs, gaccs):
    nb = T // S
    tps = S // tm
    n_r, n_b, n_g, n_o, n_ba, n_ga = len(rows), len(bpars), len(gpars), len(outs), len(baccs), len(gaccs)

    def body(*refs):
        r_refs = refs[:n_r]
        b_refs = refs[n_r:n_r + n_b]
        g_refs = refs[n_r + n_b:n_r + n_b + n_g]
        o_refs = refs[n_r + n_b + n_g:n_r + n_b + n_g + n_o]
        ba_refs = refs[n_r + n_b + n_g + n_o:n_r + n_b + n_g + n_o + n_ba]
        ga_refs = refs[n_r + n_b + n_g + n_o + n_ba:]
        i = pl.program_id(0)
        o_vals, ba_vals, ga_vals = fn([r[...] for r in r_refs], [r[...] for r in b_refs], [r[...] for r in g_refs])
        for r, v in zip(o_refs, o_vals):
            r[...] = v.astype(r.dtype)
        if n_ba:
            @pl.when(i % tps == 0)
            def _():
                for r in ba_refs:
                    r[...] = jnp.zeros_like(r)
            for r, v in zip(ba_refs, ba_vals):
                r[...] += v.reshape(r.shape)
        if n_ga:
            @pl.when(i == 0)
            def _():
                for r in ga_refs:
                    r[...] = jnp.zeros_like(r)
            for r, v in zip(ga_refs, ga_vals):
                r[...] += v.reshape(r.shape)

    in_specs = [pl.BlockSpec((tm, w), functools.partial(lambda i, cb: (i, cb), cb=cb)) for _, w, cb in rows]
    in_specs += [pl.BlockSpec((None, 1, p.shape[-1]), lambda i: (i // tps, 0, 0)) for p in bpars]
    in_specs += [pl.BlockSpec(p.shape, lambda i: (0, 0)) for p in gpars]
    out_specs = [pl.BlockSpec((tm, w), lambda i: (i, 0)) for w, _ in outs]
    out_specs += [pl.BlockSpec((None, 1, w), lambda i: (i // tps, 0, 0)) for w in baccs]
    out_specs += [pl.BlockSpec(s, lambda i: (0, 0)) for s in gaccs]
    out_shape = [jax.ShapeDtypeStruct((T, w), dt) for w, dt in outs]
    out_shape += [jax.ShapeDtypeStruct((nb, 1, w), F32) for w in baccs]
    out_shape += [jax.ShapeDtypeStruct(s, F32) for s in gaccs]
    res = pl.pallas_call(
        body, name=name, grid=(T // tm,), in_specs=in_specs, out_specs=out_specs, out_shape=out_shape,
        compiler_params=_cparams(("arbitrary",)),
    )(*[a for a, _, _ in rows], *bpars, *gpars)
    return res[:n_o], res[n_o:n_o + n_ba], res[n_o + n_ba:]


@jax.custom_vjp
def _gsum(z, G):
    zh = z.astype(BF16)
    zl = (z - zh.astype(F32)).astype(BF16)
    return _dot(zh, G) + _dot(zl, G)


_gsum.defvjp(lambda z, G: (_gsum(z, G), G), lambda G, ct: (_gsum(ct, G), jnp.zeros_like(G)))


def _s1(x, g, scale, shift):
    y = x * lax.rsqrt(jnp.mean(x * x, axis=-1, keepdims=True) + RMS_EPS)
    return (y * g) * (1.0 + scale) + shift


def _s3(uc, og, cb, lg, lb):
    u = uc + cb
    mu = jnp.mean(u, axis=-1, keepdims=True)
    d = u - mu
    var = jnp.mean(d * d, axis=-1, keepdims=True)
    y = d * lax.rsqrt(var + LN_EPS) * lg + lb
    return _silu(y) * _silu(og)


def _s4(r0, k0, v0, l0, pr, pk, pv, plo, mu_r, mu_k, mu_v, mu_l, w0, w2p, a0, a2p, k_k, k_a, G):
    r = r0 + mu_r * (pr - r0)
    k = k0 + mu_k * (pk - k0)
    v = v0 + mu_v * (pv - v0)
    lo = l0 + mu_l * (plo - l0)
    w_pre = w0 + _dot(jnp.tanh(lo), w2p, HI)
    lw = -np.float32(np.exp(-0.5)) * _sigmoid(w_pre)
    a = _sigmoid(a0 + _dot(lo, a2p, HI))
    kkr = k * k_k
    ss = _gsum(kkr * kkr, G)
    kk = kkr / jnp.maximum(jnp.sqrt(ss), L2_EPS)
    k2 = k * (1.0 + (a - 1.0) * k_a)
    return r, lw, k2, v, kk, kk * a


def _s5(o, r, k2, v, og, gg, gb, rk, G):
    mu = _gsum(o, G) * (1.0 / HN)
    d = o - mu
    var = _gsum(d * d, G) * (1.0 / HN)
    y = d * lax.rsqrt(var + GN_EPS) * gg + gb
    bonus = _gsum(r * k2 * rk, G)
    return (y + bonus * v) * _silu(og)


def _s6(yc, yr, gc, gr):
    return _sigmoid(gc) * yc + _sigmoid(gr) * yr


def _s7(x, out, tgt, gate, fg):
    x2 = x + gate * out
    y = x2 * lax.rsqrt(jnp.mean(x2 * x2, axis=-1, keepdims=True) + RMS_EPS) * fg
    e = y - tgt
    return 0.5 * jnp.sum(jnp.mean(e * e, axis=-1))


def _chunk(sts, r, lw, k, v, kk, b, cm):
    cum = _dot(cm[0], lw, HI)
    ein = jnp.exp(-cum)
    rt = r * jnp.exp(cum)
    kkt = kk * jnp.exp(cum - lw)
    kh = k * ein
    bh = b * ein
    ec = jnp.exp(jnp.sum(lw, axis=0, keepdims=True))
    khe = kh * ec
    bhe = bh * ec
    H = range(len(sts))
    tri, strict, eye = cm[0], cm[1], cm[2]
    rt, kkt, kh, bh, v, khe, bhe, ec = ([a[:, j * HN:(j + 1) * HN] for j in H] for a in (rt, kkt, kh, bh, v, khe, bhe, ec))
    a_kb = [strict * _pdot(_dot_nt, kkt[j], bh[j], P_SCORE) for j in H]
    a_kk = [strict * _pdot(_dot_nt, kkt[j], kh[j], P_SCORE) for j in H]
    a_rk = [tri * _pdot(_dot_nt, rt[j], kh[j], P_SCORE) for j in H]
    a_rb = [tri * _pdot(_dot_nt, rt[j], bh[j], P_SCORE) for j in H]
    rhs = [_pdot(_dot_nt, kkt[j], sts[j], P_APPLY) + _pdot(_dot, a_kk[j], v[j], P_APPLY) for j in H]
    o0 = [_pdot(_dot_nt, rt[j], sts[j], P_APPLY) + _pdot(_dot, a_rk[j], v[j], P_APPLY) for j in H]
    xi = [eye - cm[3] * a_kb[j] for j in H]
    for lvl in range(1, 6):
        t = [_pdot(_dot, xi[j], cm[3 + lvl] * a_kb[j], P_INV) for j in H]
        xi = [xi[j] - _pdot(_dot, t[j], xi[j], P_INV) for j in H]
    u = [_pdot(_dot, xi[j], rhs[j], P_APPLY) for j in H]
    o = [o0[j] - _pdot(_dot, a_rb[j], u[j], P_APPLY) for j in H]
    st2 = [sts[j] * ec[j] + _pdot(_dot_tn, v[j], khe[j], P_APPLY) - _pdot(_dot_tn, u[j], bhe[j], P_APPLY) for j in H]
    return jnp.concatenate(o, axis=1), tuple(st2)


def _chunk_consts():
    t = np.arange(CH)[:, None]
    s = np.arange(CH)[None, :]
    mats = [(t >= s), (t > s), (t == s)]
    for lvl in range(6):
        sz = 1 << lvl
        mats.append(((t // sz) % 2 == 1) & ((s // sz) == (t // sz) - 1))
    mats.append(np.zeros((CH, CH), bool))
    return np.stack(mats).astype(np.float32)


def _adamw(w, g, m, v):
    m = ADAM_B1 * m + (1.0 - ADAM_B1) * g
    v = ADAM_B2 * v + (1.0 - ADAM_B2) * (g * g)
    m_hat = m / (1.0 - ADAM_B1 ** ADAM_STEP)
    v_hat = v / (1.0 - ADAM_B2 ** ADAM_STEP)
    delta = -ADAM_LR * (m_hat / (jnp.sqrt(v_hat) + ADAM_EPS) + ADAM_WD * w)
    return delta, m, v


CT = 128
RB = 64
WIN = RB + 32


def _conv_fwd(pm, ck, T, S):
    nb = T // S

    def body(val_ref, gate_ref, ck_ref, out_ref, ubuf):
        ubuf[0:32, :] = jnp.zeros((32, CT), F32)
        ubuf[32:, :] = val_ref[...] * _sigmoid(gate_ref[...])

        def blk(rb, carry):
            base = pl.multiple_of(rb * RB, RB)
            win = ubuf[pl.ds(base, WIN), :]
            acc = jnp.zeros((RB, CT), F32)
            for j in range(CW):
                acc = acc + ck_ref[j:j + 1, :] * pltpu.roll(win, (WIN - (2 + j)) % WIN, 0)[0:RB, :]
            out_ref[pl.ds(base, RB), :] = acc
            return carry

        lax.fori_loop(0, S // RB, blk, 0)

    return pl.pallas_call(
        body, name="conv_fwd", grid=(D // CT, nb),
        in_specs=[pl.BlockSpec((S, CT), lambda ct, b: (b, ct)),
                  pl.BlockSpec((S, CT), lambda ct, b: (b, D // CT + ct)),
                  pl.BlockSpec((32, CT), lambda ct, b: (0, ct))],
        out_specs=pl.BlockSpec((S, CT), lambda ct, b: (b, ct)),
        out_shape=jax.ShapeDtypeStruct((T, D), F32),
        scratch_shapes=[pltpu.VMEM((S + 32, CT), F32)],
        compiler_params=_cparams(("parallel", "arbitrary")),
    )(pm, pm, ck)


def _conv_bwd(pm, duc, ck, T, S):
    nb = T // S

    def body(val_ref, gate_ref, duc_ref, ck_ref, dval_ref, dgate_ref, dck_ref, ubuf, dbuf, acc):
        b = pl.program_id(1)
        ubuf[0:32, :] = jnp.zeros((32, CT), F32)
        ubuf[32:, :] = val_ref[...] * _sigmoid(gate_ref[...])
        dbuf[0:S, :] = duc_ref[...]
        dbuf[S:, :] = jnp.zeros((32, CT), F32)
        acc[...] = jnp.zeros_like(acc)

        def blk(rb, carry):
            base = pl.multiple_of(rb * RB, RB)
            uwin = ubuf[pl.ds(base, WIN), :]
            dwin = dbuf[pl.ds(base, WIN), :]
            dblk = dwin[0:RB, :]
            du = jnp.zeros((RB, CT), F32)
            for j in range(CW):
                du = du + ck_ref[j:j + 1, :] * pltpu.roll(dwin, (WIN - (CW - 1 - j)) % WIN, 0)[0:RB, :]
                ush = pltpu.roll(uwin, (WIN - (2 + j)) % WIN, 0)[0:RB, :]
                acc[j] += jnp.sum((dblk * ush).reshape(RB // 8, 8, CT), axis=0)
            val = val_ref[pl.ds(base, RB), :]
            sg = _sigmoid(gate_ref[pl.ds(base, RB), :])
            dval_ref[pl.ds(base, RB), :] = du * sg
            dgate_ref[pl.ds(base, RB), :] = du * val * sg * (1.0 - sg)
            return carry

        lax.fori_loop(0, S // RB, blk, 0)

        @pl.when(b == 0)
        def _():
            dck_ref[...] = jnp.zeros_like(dck_ref)
        for j in range(CW):
            dck_ref[j:j + 1, :] += jnp.sum(acc[j], axis=0, keepdims=True)

    return pl.pallas_call(
        body, name="conv_bwd", grid=(D // CT, nb),
        in_specs=[pl.BlockSpec((S, CT), lambda ct, b: (b, ct)),
                  pl.BlockSpec((S, CT), lambda ct, b: (b, D // CT + ct)),
                  pl.BlockSpec((S, CT), lambda ct, b: (b, ct)),
                  pl.BlockSpec((32, CT), lambda ct, b: (0, ct))],
        out_specs=[pl.BlockSpec((S, CT), lambda ct, b: (b, ct)),
                   pl.BlockSpec((S, CT), lambda ct, b: (b, ct)),
                   pl.BlockSpec((32, CT), lambda ct, b: (0, ct))],
        out_shape=[jax.ShapeDtypeStruct((T, D), F32), jax.ShapeDtypeStruct((T, D), F32),
                   jax.ShapeDtypeStruct((32, D), F32)],
        scratch_shapes=[pltpu.VMEM((S + 32, CT), F32), pltpu.VMEM((S + 32, CT), F32), pltpu.VMEM((32, 8, CT), F32)],
        compiler_params=_cparams(("parallel", "arbitrary")),
    )(pm, pm, duc, ck)


HB = 16


def _scan_fwd(ins, cm, nb, S):
    nc = S // CH
    blk = pl.BlockSpec((CH, HB * HN), lambda b, g, i: (b * nc + i, g))
    hblk = pl.BlockSpec((None, HB, None, HN, HN), lambda b, g, i: (b, g, i, 0, 0))

    def body(r_ref, lw_ref, k_ref, v_ref, kk_ref, b_ref, cm_ref, o_ref, hs_ref, st):
        @pl.when(pl.program_id(2) == 0)
        def _():
            st[...] = jnp.zeros_like(st)
        s0 = [st[j] for j in range(HB)]
        for j in range(HB):
            hs_ref[j] = s0[j]
        o, s1 = _chunk(s0, r_ref[...], lw_ref[...], k_ref[...], v_ref[...], kk_ref[...], b_ref[...], cm_ref[...])
        o_ref[...] = o
        for j in range(HB):
            st[j] = s1[j]

    return pl.pallas_call(
        body, name="scan_fwd", grid=(nb, NH // HB, nc),
        in_specs=[blk] * 6 + [pl.BlockSpec(cm.shape, lambda b, g, i: (0, 0, 0))],
        out_specs=[blk, hblk],
        out_shape=[jax.ShapeDtypeStruct((nb * S, D), F32), jax.ShapeDtypeStruct((nb, NH, nc, HN, HN), F32)],
        scratch_shapes=[pltpu.VMEM((HB, HN, HN), F32)],
        compiler_params=_cparams(("parallel", "parallel", "arbitrary")),
    )(*ins, cm)


def _scan_bwd(ins, hs, do, cm, nb, S):
    nc = S // CH
    blk = pl.BlockSpec((CH, HB * HN), lambda b, g, i: (b * nc + nc - 1 - i, g))
    hblk = pl.BlockSpec((None, HB, None, HN, HN), lambda b, g, i: (b, g, nc - 1 - i, 0, 0))

    def body(r_ref, lw_ref, k_ref, v_ref, kk_ref, b_ref, hs_ref, do_ref, cm_ref,
             dr_ref, dlw_ref, dk_ref, dv_ref, dkk_ref, db_ref, dst):
        @pl.when(pl.program_id(2) == 0)
        def _():
            dst[...] = jnp.zeros_like(dst)
        cmv = cm_ref[...]
        f = lambda s0, r, lw, k, v, kk, b: _chunk(s0, r, lw, k, v, kk, b, cmv)
        _, vjp = jax.vjp(f, [hs_ref[j] for j in range(HB)], r_ref[...], lw_ref[...], k_ref[...], v_ref[...],
                         kk_ref[...], b_ref[...])
        ds0, dr, dlw, dk, dv, dkk, db = vjp((do_ref[...], tuple(dst[j] for j in range(HB))))
        for j in range(HB):
            dst[j] = ds0[j]
        dr_ref[...] = dr
        dlw_ref[...] = dlw
        dk_ref[...] = dk
        dv_ref[...] = dv
        dkk_ref[...] = dkk
        db_ref[...] = db

    return pl.pallas_call(
        body, name="scan_bwd", grid=(nb, NH // HB, nc),
        in_specs=[blk] * 6 + [hblk, blk, pl.BlockSpec(cm.shape, lambda b, g, i: (0, 0, 0))],
        out_specs=[blk] * 6,
        out_shape=[jax.ShapeDtypeStruct((nb * S, D), F32)] * 6,
        scratch_shapes=[pltpu.VMEM((HB, HN, HN), F32)],
        compiler_params=_cparams(("parallel", "parallel", "arbitrary")),
    )(*ins, hs, do, cm)


def _ew(fn, name, ins, n_out, tm):
    R, W = ins[0].shape[-2:]
    tm = min(tm, R)
    assert R % tm == 0

    def body(*refs):
        vals = fn(*[r[...] for r in refs[:len(ins)]])
        for r, v in zip(refs[len(ins):], vals):
            r[...] = v

    def spec(a):
        if a.ndim == 3:
            return pl.BlockSpec((a.shape[0], tm, W), lambda i: (0, i, 0))
        return pl.BlockSpec((tm, W), lambda i: (i, 0))

    return pl.pallas_call(
        body, name=name, grid=(R // tm,), in_specs=[spec(a) for a in ins],
        out_specs=[pl.BlockSpec((tm, W), lambda i: (i, 0))] * n_out,
        out_shape=[jax.ShapeDtypeStruct((R, W), F32)] * n_out,
        compiler_params=_cparams(("parallel",)),
    )(*ins)


def _sum_slots(r):
    s = r[0]
    for j in range(1, r.shape[0]):
        s = s + r[j]
    return s


def _place():
    x, y, c = lax.axis_index("x"), lax.axis_index("y"), lax.axis_index("c")
    return x, y, c


def _flip(v, d):
    return 1 - v if d else v


CHIP_PEERS = ((1, 0), (0, 1), (1, 1))
DEV_PEERS = tuple((dx, dy, dc) for dx in (0, 1) for dy in (0, 1) for dc in (0, 1))[1:]


def _comm_call(name, ins, out_shapes, plan, n_rem):
    n_in = len(ins)

    def body(*refs):
        in_refs, out_refs = refs[:n_in], refs[n_in:n_in + len(out_shapes)]
        send_sems, recv_sems, loc_sems = refs[n_in + len(out_shapes):]
        loc, rem = plan(in_refs, out_refs, _place())
        assert len(rem) == n_rem and len(loc) <= n_in, (name, len(loc), len(rem))
        copies = [pltpu.make_async_copy(s, d, loc_sems.at[i]) for i, (s, d) in enumerate(loc)]
        rcopies = [pltpu.make_async_remote_copy(src_ref=s, dst_ref=d, send_sem=send_sems.at[i], recv_sem=recv_sems.at[i],
                                                device_id=peer, device_id_type=MESH)
                   for i, (s, d, peer) in enumerate(rem)]
        for cp in copies + rcopies:
            cp.start()
        for cp in rcopies:
            cp.wait_send()
        for cp in rcopies:
            cp.wait_recv()
        for cp in copies:
            cp.wait()

    return pl.pallas_call(
        body, name=name, in_specs=[ANY] * n_in, out_specs=[ANY] * len(out_shapes), out_shape=out_shapes,
        scratch_shapes=[pltpu.SemaphoreType.DMA((n_rem,)), pltpu.SemaphoreType.DMA((n_rem,)),
                        pltpu.SemaphoreType.DMA((n_in,))],
        compiler_params=pltpu.CompilerParams(has_side_effects=True),
    )(*ins)


def _gather_plan(in_refs, out_refs, place):
    x, y, c = place
    chip, dev = 2 * x + y, 4 * x + 2 * y + c
    loc = [(in_refs[0], out_refs[0].at[dev])] + [(s, d.at[chip]) for s, d in zip(in_refs[1:], out_refs[1:])]
    rem = [(in_refs[0], out_refs[0].at[dev], (_flip(x, dx), _flip(y, dy), _flip(c, dc))) for dx, dy, dc in DEV_PEERS]
    for s, d in zip(in_refs[1:], out_refs[1:]):
        rem += [(s, d.at[chip], (_flip(x, dx), _flip(y, dy), c)) for dx, dy in CHIP_PEERS]
    return loc, rem


def _scatter_plan(n_all, in_refs, out_refs, place):
    x, y, c = place
    chip, dev = 2 * x + y, 4 * x + 2 * y + c
    loc, rem = [], []
    for s, d in zip(in_refs[:n_all], out_refs[:n_all]):
        loc.append((s.at[dev], d.at[dev]))
        for dx, dy, dc in DEV_PEERS:
            px, py, pc = _flip(x, dx), _flip(y, dy), _flip(c, dc)
            rem.append((s.at[4 * px + 2 * py + pc], d.at[dev], (px, py, pc)))
    for s, d in zip(in_refs[n_all:], out_refs[n_all:]):
        loc.append((s.at[chip], d.at[chip]))
        for dx, dy in CHIP_PEERS:
            px, py = _flip(x, dx), _flip(y, dy)
            rem.append((s.at[2 * px + py], d.at[chip], (px, py, c)))
    return loc, rem


def _sibling_plan(in_refs, out_refs, place):
    x, y, c = place
    return [], [(s, d, (x, y, 1 - c)) for s, d in zip(in_refs, out_refs)]


def _bshape(a, nb):
    return a.reshape(nb, 1, a.shape[-1])


def _to_heads(a, nb, S):
    return a.reshape(nb, S, NH, HN).transpose(0, 2, 1, 3)


def _from_heads(a, nb, S):
    return a.transpose(0, 2, 1, 3).reshape(nb * S, D)


def _shift_down(a, nb, S):
    a3 = a.reshape(nb, S, a.shape[-1])
    return jnp.pad(a3[:, :-1], ((0, 0), (1, 0), (0, 0))).reshape(a.shape)


def _shift_up(a, nb, S):
    a3 = a.reshape(nb, S, a.shape[-1])
    return jnp.pad(a3[:, 1:], ((0, 0), (0, 1), (0, 0))).reshape(a.shape)


def _local_step(x2d, tgt, mod, wmain, wlora, wco, wro, wo, ck, w2, a2, small, nb, S):
    T = nb * S
    shift, scale, gate = (_bshape(mod[:, i * D:(i + 1) * D], nb) for i in range(3))
    G = jnp.asarray(np.kron(np.eye(NH, dtype=np.float32), np.ones((HN, HN), np.float32)), dtype=BF16)
    cm = jnp.asarray(_chunk_consts())
    ckp = jnp.pad(ck, ((0, 1), (0, 0)))
    zpad = jnp.zeros((64, D), F32)
    w2p = jnp.concatenate([w2, zpad], axis=0)
    a2p = jnp.concatenate([zpad, a2], axis=0)
    mu = small["rwkv_mu"]
    mu_r, mu_k, mu_v, mu_l = mu[:, 0:D], mu[:, D:2 * D], mu[:, 2 * D:3 * D], mu[:, 3 * D:]
    g4 = [mu_r, mu_k, mu_v, mu_l, small["rwkv_w0"], w2p, small["rwkv_a0"], a2p, small["rwkv_k_k"], small["rwkv_k_a"], G]
    g5 = [small["rwkv_gn_g"], small["rwkv_gn_b"], small["rwkv_r_k"], G]
    g3 = [small["conv_b"], small["conv_ln_g"], small["conv_ln_b"]]

    (h,), _, _ = _rows(lambda r, b, g: ([_s1(r[0], g[0], b[0], b[1])], [], []), "pre_fwd", T, S, 256,
                       [(x2d, D, 0)], [scale, shift], [small["norm_g"]], [(D, BF16)], [], [])
    pm = _matmul(h, wmain, "nn", "proj_main", 512, 1024, D)
    plo = _matmul(h, wlora, "nn", "proj_lora", 512, LORA, D)
    uc = _conv_fwd(pm, ckp, T, S)
    (uo,), _, _ = _rows(lambda r, b, g: ([_s3(r[0], r[1], *g)], [], []), "conv_post_fwd", T, S, 256,
                        [(uc, D, 0), (pm, D, 2)], [], g3, [(D, BF16)], [], [])
    yc = _matmul(uo, wco, "nn", "conv_out", 512, 1024, D)
    prkv = _shift_down(pm[:, 3 * D:6 * D], nb, S)
    plop = _shift_down(plo, nb, S)
    rows4 = [(pm, D, 3), (pm, D, 4), (pm, D, 5), (plo, LORA, 0), (prkv, D, 0), (prkv, D, 1), (prkv, D, 2), (plop, LORA, 0)]
    sc_in, _, _ = _rows(lambda r, b, g: (list(_s4(*r, *g)), [], []), "rwkv_pre_fwd", T, S, 128,
                        rows4, [], g4, [(D, F32)] * 6, [], [])
    o, hs = _scan_fwd(sc_in, cm, nb, S)
    rows5 = [(o, D, 0), (sc_in[0], D, 0), (sc_in[2], D, 0), (sc_in[3], D, 0), (pm, D, 6)]
    (o2,), _, _ = _rows(lambda r, b, g: ([_s5(*r, *g)], [], []), "rwkv_post_fwd", T, S, 128,
                        rows5, [], g5, [(D, BF16)], [], [])
    yr = _matmul(o2, wro, "nn", "rwkv_out", 512, 1024, D)
    rows6 = [(yc, D, 0), (yr, D, 0), (pm, D, 7), (pm, D, 8)]
    (m,), _, _ = _rows(lambda r, b, g: ([_s6(*r)], [], []), "merge_fwd", T, S, 256, rows6, [], [], [(D, BF16)], [], [])
    out = _matmul(m, wo, "nn", "out_proj", 512, 1024, D)

    def head(r, b, g):
        loss, (dx, dout, dgate, dfg) = jax.value_and_grad(_s7, argnums=(0, 1, 3, 4))(r[0], r[1], r[2], b[0], g[0])
        return [dx, dout], [dgate], [dfg, jnp.full((1, 128), loss, F32)]

    (dx_res, dout), (dgate,), (d_final_g, loss_v) = _rows(
        head, "head", T, S, 256, [(x2d, D, 0), (out, D, 0), (tgt, D, 0)], [gate], [small["final_g"]],
        [(D, F32), (D, BF16)], [D], [(1, D), (1, 128)])

    d_wo = _matmul(m, dout, "tn", "d_w_out", 512, 1024, 512)
    dm = _matmul(dout, wo, "nt", "d_merge", 512, 1024, D)

    def merge_bwd(r, b, g):
        _, vjp = jax.vjp(_s6, *r[:4])
        dyc, dyr, dgc, dgr = vjp(r[4])
        return [dyc, dyr, dgc, dgr], [], []

    (dyc, dyr, dgc, dgr), _, _ = _rows(merge_bwd, "merge_bwd", T, S, 256, rows6 + [(dm, D, 0)], [], [],
                                       [(D, BF16), (D, BF16), (D, BF16), (D, BF16)], [], [])
    d_wco = _matmul(uo, dyc, "tn", "d_w_conv_out", 512, 1024, 512)
    d_wro = _matmul(o2, dyr, "tn", "d_w_rwkv_out", 512, 1024, 512)
    duo = _matmul(dyc, wco, "nt", "d_conv_act", 512, 1024, D)
    do2 = _matmul(dyr, wro, "nt", "d_rwkv_act", 512, 1024, D)

    def conv_post_bwd(r, b, g):
        _, vjp = jax.vjp(_s3, r[0], r[1], *g)
        duc, dog, dcb, dlg, dlb = vjp(r[2])
        return [duc, dog], [], [dcb, dlg, dlb]

    (duc, dcog), _, (d_cb, d_lg, d_lb) = _rows(conv_post_bwd, "conv_post_bwd", T, S, 256,
                                               [(uc, D, 0), (pm, D, 2), (duo, D, 0)], [], g3,
                                               [(D, F32), (D, BF16)], [], [(1, D)] * 3)
    dval, dgt, d_ckp = _conv_bwd(pm, duc, ckp, T, S)

    def rwkv_post_bwd(r, b, g):
        _, vjp = jax.vjp(lambda *z: _s5(*z, g[3]), *r[:5], *g[:3])
        res = vjp(r[5])
        return list(res[:5]), [], list(res[5:8])

    (do, dr_b, dk_b, dv_b, drog), _, (d_gg, d_gb, d_rk) = _rows(
        rwkv_post_bwd, "rwkv_post_bwd", T, S, 128, rows5 + [(do2, D, 0)], [], g5,
        [(D, F32)] * 4 + [(D, BF16)], [], [(1, D)] * 3)
    dsc = _scan_bwd(sc_in, hs, do, cm, nb, S)

    def rwkv_pre_bwd(r, b, g):
        _, vjp = jax.vjp(lambda *z: _s4(*z, g[10]), *r[:8], *g[:10])
        ct = (r[8] + r[14], r[9], r[10] + r[15], r[11] + r[16], r[12], r[13])
        res = vjp(ct)
        return list(res[:8]), [], list(res[8:18])

    rows4b = rows4 + [(a, D, 0) for a in dsc] + [(dr_b, D, 0), (dk_b, D, 0), (dv_b, D, 0)]
    gshapes = [(1, D), (1, D), (1, D), (1, LORA), (1, D), (LORA, D), (1, D), (LORA, D), (1, D), (1, D)]
    dts, _, gts = _rows(rwkv_pre_bwd, "rwkv_pre_bwd", T, S, 128, rows4b, [], g4,
                        [(D, F32)] * 3 + [(LORA, F32)] + [(D, F32)] * 3 + [(LORA, F32)], [], gshapes)
    dr0, dk0, dv0, dl0, dpr, dpk, dpv, dpl = dts
    d_mu_r, d_mu_k, d_mu_v, d_mu_l, d_w0, d_w2p, d_a0, d_a2p, d_kk, d_ka = gts
    sh = [_shift_up(a, nb, S) for a in (dpr, dpk, dpv, dpl)]

    def assemble(r, b, g):
        main = jnp.concatenate([r[0], r[1], r[2], r[3] + r[4], r[5] + r[6], r[7] + r[8], r[9], r[10], r[11]], axis=1)
        return [main, r[12] + r[13]], [], []

    rows_a = [(dval, D, 0), (dgt, D, 0), (dcog, D, 0), (dr0, D, 0), (sh[0], D, 0), (dk0, D, 0), (sh[1], D, 0),
              (dv0, D, 0), (sh[2], D, 0), (drog, D, 0), (dgc, D, 0), (dgr, D, 0), (dl0, LORA, 0), (sh[3], LORA, 0)]
    (dpm, dplo), _, _ = _rows(assemble, "assemble_dp", T, S, 128, rows_a, [], [], [(DMAIN, BF16), (LORA, BF16)], [], [])
    d_wmain = _matmul(h, dpm, "tn", "d_w_main", 512, 1024, 512)
    d_wlora = _matmul(h, dplo, "tn", "d_w_lora", 512, LORA, 512)
    dh_m = _matmul(dpm, wmain, "nt", "d_h_main", 512, 1024, 1024)
    dh_l = _matmul(dplo, wlora, "nt", "d_h_lora", 512, 1024, LORA)

    def pre_bwd(r, b, g):
        _, vjp = jax.vjp(_s1, r[0], g[0], b[0], b[1])
        dx, dg, dscale, dshift = vjp(r[1] + r[2])
        return [dx + r[3]], [dscale, dshift], [dg]

    (gx,), (dscale, dshift), (d_ng,) = _rows(pre_bwd, "pre_bwd", T, S, 256,
                                             [(x2d, D, 0), (dh_m, D, 0), (dh_l, D, 0), (dx_res, D, 0)],
                                             [scale, shift], [small["norm_g"]], [(D, F32)], [D, D], [(1, D)])
    dmod = jnp.concatenate([dshift, dscale, dgate], axis=-1).reshape(nb, 3 * D)
    d_small = {"norm_g": d_ng, "conv_b": d_cb, "conv_ln_g": d_lg, "conv_ln_b": d_lb,
               "rwkv_mu": jnp.concatenate([d_mu_r, d_mu_k, d_mu_v, d_mu_l], axis=1),
               "rwkv_w0": d_w0, "rwkv_a0": d_a0, "rwkv_k_k": d_kk, "rwkv_k_a": d_ka, "rwkv_r_k": d_rk,
               "rwkv_gn_g": d_gg, "rwkv_gn_b": d_gb, "final_g": d_final_g}
    big = {"wmain": d_wmain, "wlora": d_wlora, "wco": d_wco, "wro": d_wro, "wo": d_wo,
           "ck": d_ckp[:CW], "w2": d_w2p[:64], "a2": d_a2p[64:]}
    return loss_v[0, 0], gx, dmod, big, d_small


def _step(a):
    nb, S, _ = a["x"].shape
    T = nb * S
    x_i, y_i, c_i = _place()
    chip = 2 * x_i + y_i
    w_in = a["w_in"][0]
    WS = w_in.shape[1]
    small_w = {n: a[n].reshape(1, sz) for n, sz in SMALL}

    g_ins = [a["c"], w_in.astype(BF16), a["w_conv_out"][0].astype(BF16), a["w_rwkv_out"][0].astype(BF16),
             a["w_out"][0].astype(BF16), a["conv_k"][0], a["rwkv_w2"][0], a["rwkv_a2"][0]]
    g_out = [jax.ShapeDtypeStruct((NDEV,) + g_ins[0].shape, F32)]
    g_out += [jax.ShapeDtypeStruct((NCHIP,) + t.shape, t.dtype) for t in g_ins[1:]]
    c_all, win_g, wco_g, wro_g, wo_g, ck_g, w2_g, a2_g = _comm_call("gather_weights", g_ins, g_out, _gather_plan, 7 + 3 * (len(g_ins) - 1))
    c_all = c_all.reshape(NDEV * nb, D)
    win_full = jnp.concatenate([win_g[j] for j in range(NCHIP)], axis=1)
    wmain = jnp.concatenate([win_full[:, :6 * D], win_full[:, 6 * D + LORA:]], axis=1)
    wlora = win_full[:, 6 * D:6 * D + LORA]
    wco, wro, wo = (t.reshape(D, D) for t in (wco_g, wro_g, wo_g))
    ck = jnp.concatenate([ck_g[j] for j in range(NCHIP)], axis=1)
    w2 = jnp.concatenate([w2_g[j] for j in range(NCHIP)], axis=1)
    a2 = jnp.concatenate([a2_g[j] for j in range(NCHIP)], axis=1)

    ada_w = a["ada_w"][0]
    MW = ada_w.shape[1]
    ada_b_loc = lax.dynamic_slice(a["ada_b"], (0, chip * MW), (1, MW))

    def mod_body(c_ref, w_ref, b_ref, o_ref):
        o_ref[...] = _dot(_silu(c_ref[...]), w_ref[...], HI) + b_ref[...]

    modp = pl.pallas_call(mod_body, name="ada_mod", out_shape=jax.ShapeDtypeStruct((NDEV * nb, MW), F32),
                          compiler_params=_cparams())(c_all, ada_w, ada_b_loc)
    (mod_g,) = _comm_call("scatter_mod", [modp.reshape(NDEV, nb, MW)],
                          [jax.ShapeDtypeStruct((NDEV, nb, MW), F32)],
                          functools.partial(_scatter_plan, 1), 7)
    mod = mod_g.reshape(NCHIP, 2, nb, MW)
    mod = mod[:, 0].transpose(1, 0, 2).reshape(nb, NCHIP * MW)

    loss_p, gx, dmod, big, d_small = _local_step(
        a["x"].reshape(T, D), a["loss_target"].reshape(T, D), mod, wmain, wlora, wco, wro, wo, ck, w2, a2,
        small_w, nb, S)
    loss = lax.psum(loss_p, ("x", "y", "c"))

    d_small["ada_b"] = _colsum(dmod)
    small_vec = jnp.concatenate([d_small[n] for n, _ in SMALL], axis=1)
    dmod_s = dmod.reshape(nb, NCHIP, MW).transpose(1, 0, 2)
    dmod_s = jnp.repeat(dmod_s, 2, axis=0)
    small_s = jnp.broadcast_to(small_vec[None], (NDEV, 1, NSMALL))
    d_win = jnp.concatenate([big["wmain"][:, :6 * D], big["wlora"], big["wmain"][:, 6 * D:]], axis=1)
    win_s = jnp.stack([d_win[:, j * WS:(j + 1) * WS] for j in range(NCHIP)])
    RW = D // NCHIP
    row_s = [big[n].reshape(NCHIP, RW, D) for n in ("wco", "wro", "wo")]
    col_s = [jnp.stack([big[n][:, j * RW:(j + 1) * RW] for j in range(NCHIP)]) for n in ("ck", "w2", "a2")]
    s_ins = [dmod_s, small_s, win_s] + row_s + col_s
    s_out = [jax.ShapeDtypeStruct(t.shape, F32) for t in s_ins]
    got = _comm_call("scatter_grads", s_ins, s_out, functools.partial(_scatter_plan, 2), 14 + 3 * (len(s_ins) - 2))
    dmod_all, small_all = got[0].reshape(NDEV * nb, MW), got[1].reshape(NDEV, NSMALL)
    parts = []
    for i, t in enumerate(got[2:]):
        (p,) = _ew(lambda r: [_sum_slots(r)], "chip_sum_%d" % i, [t], 1, 128)
        parts.append(p)
    sib = _comm_call("swap_partials", parts, [jax.ShapeDtypeStruct(p.shape, F32) for p in parts], _sibling_plan, len(parts))

    outs = {}

    def upd(name, w, g_parts, tm=128):
        shp = a[name].shape
        w2d = w.reshape(g_parts[0].shape[-2:])
        m2d, v2d = a["m_" + name].reshape(w2d.shape), a["v_" + name].reshape(w2d.shape)

        def fn(w_, m_, v_, *gp):
            g = gp[0] if gp[0].ndim == 2 else _sum_slots(gp[0])
            for e in gp[1:]:
                g = g + e
            return [g, *_adamw(w_, g, m_, v_)]

        res = _ew(fn, "adamw_" + name, [w2d, m2d, v2d, *g_parts], 4, tm)
        outs[name] = [r.reshape(shp) for r in res]

    for name, p, s in zip(("w_in", "w_conv_out", "w_rwkv_out", "w_out", "conv_k", "rwkv_w2", "rwkv_a2"), parts, sib):
        upd(name, a[name][0], [p, s])

    def adaw_body(c_ref, dm_ref, w_ref, m_ref, v_ref, g_ref, d_ref, m2_ref, v2_ref):
        g = _dot_tn(_silu(c_ref[...]), dm_ref[...], HI)
        g_ref[...] = g
        d_ref[...], m2_ref[...], v2_ref[...] = _adamw(w_ref[...], g, m_ref[...], v_ref[...])

    res = pl.pallas_call(adaw_body, name="adamw_ada_w", out_shape=[jax.ShapeDtypeStruct((D, MW), F32)] * 4,
                         compiler_params=_cparams())(c_all, dmod_all, ada_w, a["m_ada_w"][0], a["v_ada_w"][0])
    outs["ada_w"] = [r.reshape(a["ada_w"].shape) for r in res]

    wv, mv, vv = (jnp.concatenate([a[p + n].reshape(1, sz) for n, sz in SMALL], axis=1) for p in ("", "m_", "v_"))
    def small_fn(w_, m_, v_, gs):
        g = _sum_slots(gs)
        return [g, *_adamw(w_, g, m_, v_)]

    res = _ew(small_fn, "adamw_small", [wv, mv, vv, small_all.reshape(NDEV, 1, NSMALL)], 4, 8)
    off = 0
    for n, sz in SMALL:
        outs[n] = [r[:, off:off + sz].reshape(a[n].shape) for r in res]
        off += sz

    return (loss, gx.reshape(nb, S, D), *[outs[n][0] for n in WEIGHTS], *[outs[n][1] for n in WEIGHTS],
            *[outs[n][2] for n in WEIGHTS], *[outs[n][3] for n in WEIGHTS])


def _colsum(dmod):
    def body(d_ref, o_ref):
        o_ref[...] = jnp.sum(d_ref[...], axis=0, keepdims=True)
    return pl.pallas_call(body, name="ada_b_rowsum", out_shape=jax.ShapeDtypeStruct((1, dmod.shape[1]), F32),
                          compiler_params=_cparams())(dmod)


def kernel(x, c, ada_w, ada_b, norm_g, w_in, conv_k, conv_b, conv_ln_g, conv_ln_b, w_conv_out, rwkv_mu, rwkv_w0, rwkv_w2, rwkv_a0, rwkv_a2, rwkv_k_k, rwkv_k_a, rwkv_r_k, rwkv_gn_g, rwkv_gn_b, w_rwkv_out, w_out, final_g, loss_target, m_ada_w, m_ada_b, m_norm_g, m_w_in, m_conv_k, m_conv_b, m_conv_ln_g, m_conv_ln_b, m_w_conv_out, m_rwkv_mu, m_rwkv_w0, m_rwkv_w2, m_rwkv_a0, m_rwkv_a2, m_rwkv_k_k, m_rwkv_k_a, m_rwkv_r_k, m_rwkv_gn_g, m_rwkv_gn_b, m_w_rwkv_out, m_w_out, m_final_g, v_ada_w, v_ada_b, v_norm_g, v_w_in, v_conv_k, v_conv_b, v_conv_ln_g, v_conv_ln_b, v_w_conv_out, v_rwkv_mu, v_rwkv_w0, v_rwkv_w2, v_rwkv_a0, v_rwkv_a2, v_rwkv_k_k, v_rwkv_k_a, v_rwkv_r_k, v_rwkv_gn_g, v_rwkv_gn_b, v_w_rwkv_out, v_w_out, v_final_g):
    return _step(dict(locals()))
```

```python
import functools

import numpy as np
import jax
import jax.numpy as jnp
from jax import lax
from jax.experimental import pallas as pl
from jax.experimental.pallas import tpu as pltpu

F32 = jnp.float32
BF16 = jnp.bfloat16
HI = lax.Precision.HIGHEST
MESH = pl.DeviceIdType.MESH
ANY = pl.BlockSpec(memory_space=pl.ANY)

D = 1024
NH = 16
HN = 64
LORA = 128
DMAIN = 9 * D
CH = 64
CW = 31
NCHIP = 4
NDEV = 8
VMEM_LIMIT = 56 * 1024 * 1024

RMS_EPS = 1e-6
LN_EPS = 1e-5
GN_EPS = 64e-5
L2_EPS = 1e-12
ADAM_LR = 0.001
ADAM_B1 = 0.9
ADAM_B2 = 0.999
ADAM_EPS = 1e-08
ADAM_WD = 0.01
ADAM_STEP = 10

SMALL = (("ada_b", 3072), ("norm_g", 1024), ("conv_b", 1024), ("conv_ln_g", 1024), ("conv_ln_b", 1024),
         ("rwkv_mu", 3200), ("rwkv_w0", 1024), ("rwkv_a0", 1024), ("rwkv_k_k", 1024), ("rwkv_k_a", 1024),
         ("rwkv_r_k", 1024), ("rwkv_gn_g", 1024), ("rwkv_gn_b", 1024), ("final_g", 1024))
NSMALL = sum(n for _, n in SMALL)

WEIGHTS = ['ada_w', 'ada_b', 'norm_g', 'w_in', 'conv_k', 'conv_b', 'conv_ln_g', 'conv_ln_b', 'w_conv_out', 'rwkv_mu',
           'rwkv_w0', 'rwkv_w2', 'rwkv_a0', 'rwkv_a2', 'rwkv_k_k', 'rwkv_k_a', 'rwkv_r_k', 'rwkv_gn_g', 'rwkv_gn_b',
           'w_rwkv_out', 'w_out', 'final_g']


def _cparams(sem=None, **kw):
    if sem is not None:
        kw["dimension_semantics"] = sem
    return pltpu.CompilerParams(vmem_limit_bytes=VMEM_LIMIT, **kw)


def _dot(a, b, prec=None):
    return jnp.dot(a, b, preferred_element_type=F32, precision=prec)


def _dot_nt(a, b, prec=None):
    return lax.dot_general(a, b, (((1,), (1,)), ((), ())), preferred_element_type=F32, precision=prec)


def _dot_tn(a, b, prec=None):
    return lax.dot_general(a, b, (((0,), (0,)), ((), ())), preferred_element_type=F32, precision=prec)


def _pdot(f, a, b, p):
    if p == "hi":
        return f(a, b, HI)
    ah, bh = a.astype(BF16), b.astype(BF16)
    if p == "bf":
        return f(ah, bh)
    al, bl = (a - ah.astype(F32)).astype(BF16), (b - bh.astype(F32)).astype(BF16)
    return f(ah, bh) + (f(ah, bl) + f(al, bh))


P_SCORE = "b3"
P_INV = "bf"
P_APPLY = "bf"


def _sigmoid(z):
    return 1.0 / (1.0 + jnp.exp(-z))


def _silu(z):
    return z * _sigmoid(z)


def _matmul(a, b, mode, name, tm, tn, tk):
    if mode == "nn":
        (M, K), N = a.shape, b.shape[1]
        a_spec = pl.BlockSpec((tm, tk), lambda j, i, k: (i, k))
        b_spec = pl.BlockSpec((tk, tn), lambda j, i, k: (k, j))
        f = _dot
    elif mode == "nt":
        (M, K), N = a.shape, b.shape[0]
        a_spec = pl.BlockSpec((tm, tk), lambda j, i, k: (i, k))
        b_spec = pl.BlockSpec((tn, tk), lambda j, i, k: (j, k))
        f = _dot_nt
    else:
        (K, M), N = a.shape, b.shape[1]
        a_spec = pl.BlockSpec((tk, tm), lambda j, i, k: (k, i))
        b_spec = pl.BlockSpec((tk, tn), lambda j, i, k: (k, j))
        f = _dot_tn
    assert M % tm == 0 and N % tn == 0 and K % tk == 0, (name, M, N, K)

    def body(a_ref, b_ref, o_ref):
        @pl.when(pl.program_id(2) == 0)
        def _():
            o_ref[...] = jnp.zeros_like(o_ref)
        o_ref[...] += f(a_ref[...], b_ref[...])

    return pl.pallas_call(
        body, name=name, grid=(N // tn, M // tm, K // tk),
        in_specs=[a_spec, b_spec],
        out_specs=pl.BlockSpec((tm, tn), lambda j, i, k: (i, j)),
        out_shape=jax.ShapeDtypeStruct((M, N), F32),
        compiler_params=_cparams(("parallel", "parallel", "arbitrary")),
    )(a, b)


def _rows(fn, name, T, S, tm, rows, bpars, gpars, outs, baccs, gaccs):
    nb = T // S
    tps = S // tm
    n_r, n_b, n_g, n_o, n_ba, n_ga = len(rows), len(bpars), len(gpars), len(outs), len(baccs), len(gaccs)

    def body(*refs):
        r_refs = refs[:n_r]
        b_refs = refs[n_r:n_r + n_b]
        g_refs = refs[n_r + n_b:n_r + n_b + n_g]
        o_refs = refs[n_r + n_b + n_g:n_r + n_b + n_g + n_o]
        ba_refs = refs[n_r + n_b + n_g + n_o:n_r + n_b + n_g + n_o + n_ba]
        ga_refs = refs[n_r + n_b + n_g + n_o + n_ba:]
        i = pl.program_id(0)
        o_vals, ba_vals, ga_vals = fn([r[...] for r in r_refs], [r[...] for r in b_refs], [r[...] for r in g_refs])
        for r, v in zip(o_refs, o_vals):
            r[...] = v.astype(r.dtype)
        if n_ba:
            @pl.when(i % tps == 0)
            def _():
                for r in ba_refs:
                    r[...] = jnp.zeros_like(r)
            for r, v in zip(ba_refs, ba_vals):
                r[...] += v.reshape(r.shape)
        if n_ga:
            @pl.when(i == 0)
            def _():
                for r in ga_refs:
                    r[...] = jnp.zeros_like(r)
            for r, v in zip(ga_refs, ga_vals):
                r[...] += v.reshape(r.shape)

    in_specs = [pl.BlockSpec((tm, w), functools.partial(lambda i, cb: (i, cb), cb=cb)) for _, w, cb in rows]
    in_specs += [pl.BlockSpec((None, 1, p.shape[-1]), lambda i: (i // tps, 0, 0)) for p in bpars]
    in_specs += [pl.BlockSpec(p.shape, lambda i: (0, 0)) for p in gpars]
    out_specs = [pl.BlockSpec((tm, w), lambda i: (i, 0)) for w, _ in outs]
    out_specs += [pl.BlockSpec((None, 1, w), lambda i: (i // tps, 0, 0)) for w in baccs]
    out_specs += [pl.BlockSpec(s, lambda i: (0, 0)) for s in gaccs]
    out_shape = [jax.ShapeDtypeStruct((T, w), dt) for w, dt in outs]
    out_shape += [jax.ShapeDtypeStruct((nb, 1, w), F32) for w in baccs]
    out_shape += [jax.ShapeDtypeStruct(s, F32) for s in gaccs]
    res = pl.pallas_call(
        body, name=name, grid=(T // tm,), in_specs=in_specs, out_specs=out_specs, out_shape=out_shape,
        compiler_params=_cparams(("arbitrary",)),
    )(*[a for a, _, _ in rows], *bpars, *gpars)
    return res[:n_o], res[n_o:n_o + n_ba], res[n_o + n_ba:]


@jax.custom_vjp
def _gsum(z, G):
    zh = z.astype(BF16)
    zl = (z - zh.astype(F32)).astype(BF16)
    return _dot(zh, G) + _dot(zl, G)


_gsum.defvjp(lambda z, G: (_gsum(z, G), G), lambda G, ct: (_gsum(ct, G), jnp.zeros_like(G)))


def _s1(x, g, scale, shift):
    y = x * lax.rsqrt(jnp.mean(x * x, axis=-1, keepdims=True) + RMS_EPS)
    return (y * g) * (1.0 + scale) + shift


def _s3(uc, og, cb, lg, lb):
    u = uc + cb
    mu = jnp.mean(u, axis=-1, keepdims=True)
    d = u - mu
    var = jnp.mean(d * d, axis=-1, keepdims=True)
    y = d * lax.rsqrt(var + LN_EPS) * lg + lb
    return _silu(y) * _silu(og)


def _s4(r0, k0, v0, l0, pr, pk, pv, plo, mu_r, mu_k, mu_v, mu_l, w0, w2p, a0, a2p, k_k, k_a, G):
    r = r0 + mu_r * (pr - r0)
    k = k0 + mu_k * (pk - k0)
    v = v0 + mu_v * (pv - v0)
    lo = l0 + mu_l * (plo - l0)
    w_pre = w0 + _dot(jnp.tanh(lo), w2p, HI)
    lw = -np.float32(np.exp(-0.5)) * _sigmoid(w_pre)
    a = _sigmoid(a0 + _dot(lo, a2p, HI))
    kkr = k * k_k
    ss = _gsum(kkr * kkr, G)
    kk = kkr / jnp.maximum(jnp.sqrt(ss), L2_EPS)
    k2 = k * (1.0 + (a - 1.0) * k_a)
    return r, lw, k2, v, kk, kk * a


def _s5(o, r, k2, v, og, gg, gb, rk, G):
    mu = _gsum(o, G) * (1.0 / HN)
    d = o - mu
    var = _gsum(d * d, G) * (1.0 / HN)
    y = d * lax.rsqrt(var + GN_EPS) * gg + gb
    bonus = _gsum(r * k2 * rk, G)
    return (y + bonus * v) * _silu(og)


def _s6(yc, yr, gc, gr):
    return _sigmoid(gc) * yc + _sigmoid(gr) * yr


def _s7(x, out, tgt, gate, fg):
    x2 = x + gate * out
    y = x2 * lax.rsqrt(jnp.mean(x2 * x2, axis=-1, keepdims=True) + RMS_EPS) * fg
    e = y - tgt
    return 0.5 * jnp.sum(jnp.mean(e * e, axis=-1))


def _chunk(sts, r, lw, k, v, kk, b, cm):
    cum = _dot(cm[0], lw, HI)
    ein = jnp.exp(-cum)
    rt = r * jnp.exp(cum)
    kkt = kk * jnp.exp(cum - lw)
    kh = k * ein
    bh = b * ein
    ec = jnp.exp(jnp.sum(lw, axis=0, keepdims=True))
    khe = kh * ec
    bhe = bh * ec
    H = range(len(sts))
    tri, strict, eye = cm[0], cm[1], cm[2]
    rt, kkt, kh, bh, v, khe, bhe, ec = ([a[:, j * HN:(j + 1) * HN] for j in H] for a in (rt, kkt, kh, bh, v, khe, bhe, ec))
    a_kb = [strict * _pdot(_dot_nt, kkt[j], bh[j], P_SCORE) for j in H]
    a_kk = [strict * _pdot(_dot_nt, kkt[j], kh[j], P_SCORE) for j in H]
    a_rk = [tri * _pdot(_dot_nt, rt[j], kh[j], P_SCORE) for j in H]
    a_rb = [tri * _pdot(_dot_nt, rt[j], bh[j], P_SCORE) for j in H]
    rhs = [_pdot(_dot_nt, kkt[j], sts[j], P_APPLY) + _pdot(_dot, a_kk[j], v[j], P_APPLY) for j in H]
    o0 = [_pdot(_dot_nt, rt[j], sts[j], P_APPLY) + _pdot(_dot, a_rk[j], v[j], P_APPLY) for j in H]
    xi = [eye - cm[3] * a_kb[j] for j in H]
    for lvl in range(1, 6):
        t = [_pdot(_dot, xi[j], cm[3 + lvl] * a_kb[j], P_INV) for j in H]
        xi = [xi[j] - _pdot(_dot, t[j], xi[j], P_INV) for j in H]
    u = [_pdot(_dot, xi[j], rhs[j], P_APPLY) for j in H]
    o = [o0[j] - _pdot(_dot, a_rb[j], u[j], P_APPLY) for j in H]
    st2 = [sts[j] * ec[j] + _pdot(_dot_tn, v[j], khe[j], P_APPLY) - _pdot(_dot_tn, u[j], bhe[j], P_APPLY) for j in H]
    return jnp.concatenate(o, axis=1), tuple(st2)


def _chunk_consts():
    t = np.arange(CH)[:, None]
    s = np.arange(CH)[None, :]
    mats = [(t >= s), (t > s), (t == s)]
    for lvl in range(6):
        sz = 1 << lvl
        mats.append(((t // sz) % 2 == 1) & ((s // sz) == (t // sz) - 1))
    mats.append(np.zeros((CH, CH), bool))
    return np.stack(mats).astype(np.float32)


def _adamw(w, g, m, v):
    m = ADAM_B1 * m + (1.0 - ADAM_B1) * g
    v = ADAM_B2 * v + (1.0 - ADAM_B2) * (g * g)
    m_hat = m / (1.0 - ADAM_B1 ** ADAM_STEP)
    v_hat = v / (1.0 - ADAM_B2 ** ADAM_STEP)
    delta = -ADAM_LR * (m_hat / (jnp.sqrt(v_hat) + ADAM_EPS) + ADAM_WD * w)
    return delta, m, v


CT = 128
RB = 64
WIN = RB + 32


def _conv_fwd(pm, ck, T, S):
    nb = T // S

    def body(val_ref, gate_ref, ck_ref, out_ref, ubuf):
        ubuf[0:32, :] = jnp.zeros((32, CT), F32)
        ubuf[32:, :] = val_ref[...] * _sigmoid(gate_ref[...])

        def blk(rb, carry):
            base = pl.multiple_of(rb * RB, RB)
            win = ubuf[pl.ds(base, WIN), :]
            acc = jnp.zeros((RB, CT), F32)
            for j in range(CW):
                acc = acc + ck_ref[j:j + 1, :] * pltpu.roll(win, (WIN - (2 + j)) % WIN, 0)[0:RB, :]
            out_ref[pl.ds(base, RB), :] = acc
            return carry

        lax.fori_loop(0, S // RB, blk, 0)

    return pl.pallas_call(
        body, name="conv_fwd", grid=(D // CT, nb),
        in_specs=[pl.BlockSpec((S, CT), lambda ct, b: (b, ct)),
                  pl.BlockSpec((S, CT), lambda ct, b: (b, D // CT + ct)),
                  pl.BlockSpec((32, CT), lambda ct, b: (0, ct))],
        out_specs=pl.BlockSpec((S, CT), lambda ct, b: (b, ct)),
        out_shape=jax.ShapeDtypeStruct((T, D), F32),
        scratch_shapes=[pltpu.VMEM((S + 32, CT), F32)],
        compiler_params=_cparams(("parallel", "arbitrary")),
    )(pm, pm, ck)


def _conv_bwd(pm, duc, ck, T, S):
    nb = T // S

    def body(val_ref, gate_ref, duc_ref, ck_ref, dval_ref, dgate_ref, dck_ref, ubuf, dbuf, acc):
        b = pl.program_id(1)
        ubuf[0:32, :] = jnp.zeros((32, CT), F32)
        ubuf[32:, :] = val_ref[...] * _sigmoid(gate_ref[...])
        dbuf[0:S, :] = duc_ref[...]
        dbuf[S:, :] = jnp.zeros((32, CT), F32)
        acc[...] = jnp.zeros_like(acc)

        def blk(rb, carry):
            base = pl.multiple_of(rb * RB, RB)
            uwin = ubuf[pl.ds(base, WIN), :]
            dwin = dbuf[pl.ds(base, WIN), :]
            dblk = dwin[0:RB, :]
            du = jnp.zeros((RB, CT), F32)
            for j in range(CW):
                du = du + ck_ref[j:j + 1, :] * pltpu.roll(dwin, (WIN - (CW - 1 - j)) % WIN, 0)[0:RB, :]
                ush = pltpu.roll(uwin, (WIN - (2 + j)) % WIN, 0)[0:RB, :]
                acc[j] += jnp.sum((dblk * ush).reshape(RB // 8, 8, CT), axis=0)
            val = val_ref[pl.ds(base, RB), :]
            sg = _sigmoid(gate_ref[pl.ds(base, RB), :])
            dval_ref[pl.ds(base, RB), :] = du * sg
            dgate_ref[pl.ds(base, RB), :] = du * val * sg * (1.0 - sg)
            return carry

        lax.fori_loop(0, S // RB, blk, 0)

        @pl.when(b == 0)
        def _():
            dck_ref[...] = jnp.zeros_like(dck_ref)
        for j in range(CW):
            dck_ref[j:j + 1, :] += jnp.sum(acc[j], axis=0, keepdims=True)

    return pl.pallas_call(
        body, name="conv_bwd", grid=(D // CT, nb),
        in_specs=[pl.BlockSpec((S, CT), lambda ct, b: (b, ct)),
                  pl.BlockSpec((S, CT), lambda ct, b: (b, D // CT + ct)),
                  pl.BlockSpec((S, CT), lambda ct, b: (b, ct)),
                  pl.BlockSpec((32, CT), lambda ct, b: (0, ct))],
        out_specs=[pl.BlockSpec((S, CT), lambda ct, b: (b, ct)),
                   pl.BlockSpec((S, CT), lambda ct, b: (b, ct)),
                   pl.BlockSpec((32, CT), lambda ct, b: (0, ct))],
        out_shape=[jax.ShapeDtypeStruct((T, D), F32), jax.ShapeDtypeStruct((T, D), F32),
                   jax.ShapeDtypeStruct((32, D), F32)],
        scratch_shapes=[pltpu.VMEM((S + 32, CT), F32), pltpu.VMEM((S + 32, CT), F32), pltpu.VMEM((32, 8, CT), F32)],
        compiler_params=_cparams(("parallel", "arbitrary")),
    )(pm, pm, duc, ck)


HB = 16


def _scan_fwd(ins, cm, nb, S):
    nc = S // CH
    blk = pl.BlockSpec((CH, HB * HN), lambda b, g, i: (b * nc + i, g))
    hblk = pl.BlockSpec((None, HB, None, HN, HN), lambda b, g, i: (b, g, i, 0, 0))

    def body(r_ref, lw_ref, k_ref, v_ref, kk_ref, b_ref, cm_ref, o_ref, hs_ref, st):
        @pl.when(pl.program_id(2) == 0)
        def _():
            st[...] = jnp.zeros_like(st)
        s0 = [st[j] for j in range(HB)]
        for j in range(HB):
            hs_ref[j] = s0[j]
        o, s1 = _chunk(s0, r_ref[...], lw_ref[...], k_ref[...], v_ref[...], kk_ref[...], b_ref[...], cm_ref[...])
        o_ref[...] = o
        for j in range(HB):
            st[j] = s1[j]

    return pl.pallas_call(
        body, name="scan_fwd", grid=(nb, NH // HB, nc),
        in_specs=[blk] * 6 + [pl.BlockSpec(cm.shape, lambda b, g, i: (0, 0, 0))],
        out_specs=[blk, hblk],
        out_shape=[jax.ShapeDtypeStruct((nb * S, D), F32), jax.ShapeDtypeStruct((nb, NH, nc, HN, HN), F32)],
        scratch_shapes=[pltpu.VMEM((HB, HN, HN), F32)],
        compiler_params=_cparams(("parallel", "parallel", "arbitrary")),
    )(*ins, cm)


def _scan_bwd(ins, hs, do, cm, nb, S):
    nc = S // CH
    blk = pl.BlockSpec((CH, HB * HN), lambda b, g, i: (b * nc + nc - 1 - i, g))
    hblk = pl.BlockSpec((None, HB, None, HN, HN), lambda b, g, i: (b, g, nc - 1 - i, 0, 0))

    def body(r_ref, lw_ref, k_ref, v_ref, kk_ref, b_ref, hs_ref, do_ref, cm_ref,
             dr_ref, dlw_ref, dk_ref, dv_ref, dkk_ref, db_ref, dst):
        @pl.when(pl.program_id(2) == 0)
        def _():
            dst[...] = jnp.zeros_like(dst)
        cmv = cm_ref[...]
        f = lambda s0, r, lw, k, v, kk, b: _chunk(s0, r, lw, k, v, kk, b, cmv)
        _, vjp = jax.vjp(f, [hs_ref[j] for j in range(HB)], r_ref[...], lw_ref[...], k_ref[...], v_ref[...],
                         kk_ref[...], b_ref[...])
        ds0, dr, dlw, dk, dv, dkk, db = vjp((do_ref[...], tuple(dst[j] for j in range(HB))))
        for j in range(HB):
            dst[j] = ds0[j]
        dr_ref[...] = dr
        dlw_ref[...] = dlw
        dk_ref[...] = dk
        dv_ref[...] = dv
        dkk_ref[...] = dkk
        db_ref[...] = db

    return pl.pallas_call(
        body, name="scan_bwd", grid=(nb, NH // HB, nc),
        in_specs=[blk] * 6 + [hblk, blk, pl.BlockSpec(cm.shape, lambda b, g, i: (0, 0, 0))],
        out_specs=[blk] * 6,
        out_shape=[jax.ShapeDtypeStruct((nb * S, D), F32)] * 6,
        scratch_shapes=[pltpu.VMEM((HB, HN, HN), F32)],
        compiler_params=_cparams(("parallel", "parallel", "arbitrary")),
    )(*ins, hs, do, cm)


def _ew(fn, name, ins, n_out, tm, out_dtype=F32):
    R, W = ins[0].shape[-2:]
    tm = min(tm, R)
    assert R % tm == 0

    def body(*refs):
        vals = fn(*[r[...] for r in refs[:len(ins)]])
        for r, v in zip(refs[len(ins):], vals):
            r[...] = v.astype(r.dtype)

    def spec(a):
        if a.ndim == 3:
            return pl.BlockSpec((a.shape[0], tm, W), lambda i: (0, i, 0))
        return pl.BlockSpec((tm, W), lambda i: (i, 0))

    return pl.pallas_call(
        body, name=name, grid=(R // tm,), in_specs=[spec(a) for a in ins],
        out_specs=[pl.BlockSpec((tm, W), lambda i: (i, 0))] * n_out,
        out_shape=[jax.ShapeDtypeStruct((R, W), out_dtype)] * n_out,
        compiler_params=_cparams(("parallel",)),
    )(*ins)


def _sum_slots(r):
    s = r[0]
    for j in range(1, r.shape[0]):
        s = s + r[j]
    return s


def _place():
    x, y, c = lax.axis_index("x"), lax.axis_index("y"), lax.axis_index("c")
    return x, y, c


def _flip(v, d):
    return 1 - v if d else v


CHIP_PEERS = ((1, 0), (0, 1), (1, 1))
DEV_PEERS = tuple((dx, dy, dc) for dx in (0, 1) for dy in (0, 1) for dc in (0, 1))[1:]


def _comm_call(name, ins, out_shapes, plan, n_rem, n_fwd=0):
    n_in = len(ins)

    def body(*refs):
        in_refs, out_refs = refs[:n_in], refs[n_in:n_in + len(out_shapes)]
        send_sems, recv_sems, loc_sems = refs[n_in + len(out_shapes):]
        loc, rem, *rest = plan(in_refs, out_refs, _place())
        fwd = rest[0] if rest else []
        assert len(rem) == n_rem and len(fwd) == n_fwd and len(loc) <= 2 * n_in, (name, len(loc), len(rem), len(fwd))

        def remote(i, s, d, peer):
            return pltpu.make_async_remote_copy(src_ref=s, dst_ref=d, send_sem=send_sems.at[i], recv_sem=recv_sems.at[i],
                                                device_id=peer, device_id_type=MESH)

        copies = [pltpu.make_async_copy(s, d, loc_sems.at[i]) for i, (s, d) in enumerate(loc)]
        rcopies = [remote(i, s, d, peer) for i, (s, d, peer) in enumerate(rem)]
        for cp in copies + rcopies:
            cp.start()
        landed = set()
        fcopies = []
        for i, (s, d, peer, k) in enumerate(fwd):
            if k not in landed:
                rcopies[k].wait_recv()
                landed.add(k)
            fcopies.append(remote(n_rem + i, s, d, peer))
            fcopies[-1].start()
        for k, cp in enumerate(rcopies):
            if k not in landed:
                cp.wait_recv()
        for cp in rcopies + fcopies:
            cp.wait_send()
        for cp in fcopies:
            cp.wait_recv()
        for cp in copies:
            cp.wait()

    return pl.pallas_call(
        body, name=name, in_specs=[ANY] * n_in, out_specs=[ANY] * len(out_shapes), out_shape=out_shapes,
        scratch_shapes=[pltpu.SemaphoreType.DMA((n_rem + n_fwd,)), pltpu.SemaphoreType.DMA((n_rem + n_fwd,)),
                        pltpu.SemaphoreType.DMA((2 * n_in,))],
        compiler_params=pltpu.CompilerParams(has_side_effects=True),
    )(*ins)


def _gather_plan(n_big, in_refs, out_refs, place):
    x, y, c = place
    chip, dev = 2 * x + y, 4 * x + 2 * y + c
    sib = (x, y, 1 - c)
    loc = [(in_refs[0], out_refs[0].at[dev])] + [(s, d.at[chip]) for s, d in zip(in_refs[1:], out_refs[1:])]
    rem = [(in_refs[0], out_refs[0].at[dev], (_flip(x, dx), _flip(y, dy), _flip(c, dc))) for dx, dy, dc in DEV_PEERS]
    fwd = []
    for s, d in zip(in_refs[1:1 + n_big], out_refs[1:1 + n_big]):
        for dx, dy in CHIP_PEERS:
            px, py = _flip(x, dx), _flip(y, dy)
            fwd.append((d.at[2 * px + py, c], d.at[2 * px + py, c], sib, len(rem)))
            rem.append((s.at[c], d.at[chip, c], (px, py, c)))
    for s, d in zip(in_refs[1 + n_big:], out_refs[1 + n_big:]):
        rem += [(s, d.at[chip], (_flip(x, dx), _flip(y, dy), c)) for dx, dy in CHIP_PEERS]
    return loc, rem, fwd


def _halve_plan(in_refs, out_refs, place):
    x, y, c = place
    n = len(in_refs)
    loc = [(s.at[c], d) for s, d in zip(in_refs, out_refs[:n])]
    rem = [(s.at[1 - c], d, (x, y, 1 - c)) for s, d in zip(in_refs, out_refs[n:])]
    return loc, rem


def _join_plan(in_refs, out_refs, place):
    x, y, c = place
    loc = [(s, d.at[c]) for s, d in zip(in_refs, out_refs)]
    rem = [(s, d.at[c], (x, y, 1 - c)) for s, d in zip(in_refs, out_refs)]
    return loc, rem


def _scatter_plan(n_all, in_refs, out_refs, place):
    x, y, c = place
    chip, dev = 2 * x + y, 4 * x + 2 * y + c
    loc, rem = [], []
    for s, d in zip(in_refs[:n_all], out_refs[:n_all]):
        loc.append((s.at[dev], d.at[dev]))
        for dx, dy, dc in DEV_PEERS:
            px, py, pc = _flip(x, dx), _flip(y, dy), _flip(c, dc)
            rem.append((s.at[4 * px + 2 * py + pc], d.at[dev], (px, py, pc)))
    for s, d in zip(in_refs[n_all:], out_refs[n_all:]):
        loc.append((s.at[chip], d.at[chip]))
        for dx, dy in CHIP_PEERS:
            px, py = _flip(x, dx), _flip(y, dy)
            rem.append((s.at[2 * px + py], d.at[chip], (px, py, c)))
    return loc, rem


def _bshape(a, nb):
    return a.reshape(nb, 1, a.shape[-1])


def _shift_down(a, nb, S):
    a3 = a.reshape(nb, S, a.shape[-1])
    return jnp.pad(a3[:, :-1], ((0, 0), (1, 0), (0, 0))).reshape(a.shape)


def _shift_up(a, nb, S):
    a3 = a.reshape(nb, S, a.shape[-1])
    return jnp.pad(a3[:, 1:], ((0, 0), (0, 1), (0, 0))).reshape(a.shape)


def _local_step(x2d, tgt, mod, wmain, wlora, wco, wro, wo, ck, w2, a2, small, nb, S):
    T = nb * S
    shift, scale, gate = (_bshape(mod[:, i * D:(i + 1) * D], nb) for i in range(3))
    G = jnp.asarray(np.kron(np.eye(NH, dtype=np.float32), np.ones((HN, HN), np.float32)), dtype=BF16)
    cm = jnp.asarray(_chunk_consts())
    ckp = jnp.pad(ck, ((0, 1), (0, 0)))
    zpad = jnp.zeros((64, D), F32)
    w2p = jnp.concatenate([w2, zpad], axis=0)
    a2p = jnp.concatenate([zpad, a2], axis=0)
    mu = small["rwkv_mu"]
    mu_r, mu_k, mu_v, mu_l = mu[:, 0:D], mu[:, D:2 * D], mu[:, 2 * D:3 * D], mu[:, 3 * D:]
    g4 = [mu_r, mu_k, mu_v, mu_l, small["rwkv_w0"], w2p, small["rwkv_a0"], a2p, small["rwkv_k_k"], small["rwkv_k_a"], G]
    g5 = [small["rwkv_gn_g"], small["rwkv_gn_b"], small["rwkv_r_k"], G]
    g3 = [small["conv_b"], small["conv_ln_g"], small["conv_ln_b"]]

    (h,), _, _ = _rows(lambda r, b, g: ([_s1(r[0], g[0], b[0], b[1])], [], []), "pre_fwd", T, S, 256,
                       [(x2d, D, 0)], [scale, shift], [small["norm_g"]], [(D, BF16)], [], [])
    pm = _matmul(h, wmain, "nn", "proj_main", 512, 1024, D)
    plo = _matmul(h, wlora, "nn", "proj_lora", 512, LORA, D)
    uc = _conv_fwd(pm, ckp, T, S)
    (uo,), _, _ = _rows(lambda r, b, g: ([_s3(r[0], r[1], *g)], [], []), "conv_post_fwd", T, S, 256,
                        [(uc, D, 0), (pm, D, 2)], [], g3, [(D, BF16)], [], [])
    yc = _matmul(uo, wco, "nn", "conv_out", 512, 1024, D)
    prkv = _shift_down(pm[:, 3 * D:6 * D], nb, S)
    plop = _shift_down(plo, nb, S)
    rows4 = [(pm, D, 3), (pm, D, 4), (pm, D, 5), (plo, LORA, 0), (prkv, D, 0), (prkv, D, 1), (prkv, D, 2), (plop, LORA, 0)]
    sc_in, _, _ = _rows(lambda r, b, g: (list(_s4(*r, *g)), [], []), "rwkv_pre_fwd", T, S, 128,
                        rows4, [], g4, [(D, F32)] * 6, [], [])
    o, hs = _scan_fwd(sc_in, cm, nb, S)
    rows5 = [(o, D, 0), (sc_in[0], D, 0), (sc_in[2], D, 0), (sc_in[3], D, 0), (pm, D, 6)]
    (o2,), _, _ = _rows(lambda r, b, g: ([_s5(*r, *g)], [], []), "rwkv_post_fwd", T, S, 128,
                        rows5, [], g5, [(D, BF16)], [], [])
    yr = _matmul(o2, wro, "nn", "rwkv_out", 512, 1024, D)
    rows6 = [(yc, D, 0), (yr, D, 0), (pm, D, 7), (pm, D, 8)]
    (m,), _, _ = _rows(lambda r, b, g: ([_s6(*r)], [], []), "merge_fwd", T, S, 256, rows6, [], [], [(D, BF16)], [], [])
    out = _matmul(m, wo, "nn", "out_proj", 512, 1024, D)

    def head(r, b, g):
        loss, (dx, dout, dgate, dfg) = jax.value_and_grad(_s7, argnums=(0, 1, 3, 4))(r[0], r[1], r[2], b[0], g[0])
        return [dx, dout], [dgate], [dfg, jnp.full((1, 128), loss, F32)]

    (dx_res, dout), (dgate,), (d_final_g, loss_v) = _rows(
        head, "head", T, S, 256, [(x2d, D, 0), (out, D, 0), (tgt, D, 0)], [gate], [small["final_g"]],
        [(D, F32), (D, BF16)], [D], [(1, D), (1, 128)])

    d_wo = _matmul(m, dout, "tn", "d_w_out", 512, 1024, 512)
    dm = _matmul(dout, wo, "nt", "d_merge", 512, 1024, D)

    def merge_bwd(r, b, g):
        _, vjp = jax.vjp(_s6, *r[:4])
        dyc, dyr, dgc, dgr = vjp(r[4])
        return [dyc, dyr, dgc, dgr], [], []

    (dyc, dyr, dgc, dgr), _, _ = _rows(merge_bwd, "merge_bwd", T, S, 256, rows6 + [(dm, D, 0)], [], [],
                                       [(D, BF16), (D, BF16), (D, BF16), (D, BF16)], [], [])
    d_wco = _matmul(uo, dyc, "tn", "d_w_conv_out", 512, 1024, 512)
    d_wro = _matmul(o2, dyr, "tn", "d_w_rwkv_out", 512, 1024, 512)
    duo = _matmul(dyc, wco, "nt", "d_conv_act", 512, 1024, D)
    do2 = _matmul(dyr, wro, "nt", "d_rwkv_act", 512, 1024, D)

    def conv_post_bwd(r, b, g):
        _, vjp = jax.vjp(_s3, r[0], r[1], *g)
        duc, dog, dcb, dlg, dlb = vjp(r[2])
        return [duc, dog], [], [dcb, dlg, dlb]

    (duc, dcog), _, (d_cb, d_lg, d_lb) = _rows(conv_post_bwd, "conv_post_bwd", T, S, 256,
                                               [(uc, D, 0), (pm, D, 2), (duo, D, 0)], [], g3,
                                               [(D, F32), (D, BF16)], [], [(1, D)] * 3)
    dval, dgt, d_ckp = _conv_bwd(pm, duc, ckp, T, S)

    def rwkv_post_bwd(r, b, g):
        _, vjp = jax.vjp(lambda *z: _s5(*z, g[3]), *r[:5], *g[:3])
        res = vjp(r[5])
        return list(res[:5]), [], list(res[5:8])

    (do, dr_b, dk_b, dv_b, drog), _, (d_gg, d_gb, d_rk) = _rows(
        rwkv_post_bwd, "rwkv_post_bwd", T, S, 128, rows5 + [(do2, D, 0)], [], g5,
        [(D, F32)] * 4 + [(D, BF16)], [], [(1, D)] * 3)
    dsc = _scan_bwd(sc_in, hs, do, cm, nb, S)

    def rwkv_pre_bwd(r, b, g):
        _, vjp = jax.vjp(lambda *z: _s4(*z, g[10]), *r[:8], *g[:10])
        ct = (r[8] + r[14], r[9], r[10] + r[15], r[11] + r[16], r[12], r[13])
        res = vjp(ct)
        return list(res[:8]), [], list(res[8:18])

    rows4b = rows4 + [(a, D, 0) for a in dsc] + [(dr_b, D, 0), (dk_b, D, 0), (dv_b, D, 0)]
    gshapes = [(1, D), (1, D), (1, D), (1, LORA), (1, D), (LORA, D), (1, D), (LORA, D), (1, D), (1, D)]
    dts, _, gts = _rows(rwkv_pre_bwd, "rwkv_pre_bwd", T, S, 128, rows4b, [], g4,
                        [(D, F32)] * 3 + [(LORA, F32)] + [(D, F32)] * 3 + [(LORA, F32)], [], gshapes)
    dr0, dk0, dv0, dl0, dpr, dpk, dpv, dpl = dts
    d_mu_r, d_mu_k, d_mu_v, d_mu_l, d_w0, d_w2p, d_a0, d_a2p, d_kk, d_ka = gts
    sh = [_shift_up(a, nb, S) for a in (dpr, dpk, dpv, dpl)]

    def assemble(r, b, g):
        main = jnp.concatenate([r[0], r[1], r[2], r[3] + r[4], r[5] + r[6], r[7] + r[8], r[9], r[10], r[11]], axis=1)
        return [main, r[12] + r[13]], [], []

    rows_a = [(dval, D, 0), (dgt, D, 0), (dcog, D, 0), (dr0, D, 0), (sh[0], D, 0), (dk0, D, 0), (sh[1], D, 0),
              (dv0, D, 0), (sh[2], D, 0), (drog, D, 0), (dgc, D, 0), (dgr, D, 0), (dl0, LORA, 0), (sh[3], LORA, 0)]
    (dpm, dplo), _, _ = _rows(assemble, "assemble_dp", T, S, 128, rows_a, [], [], [(DMAIN, BF16), (LORA, BF16)], [], [])
    d_wmain = _matmul(h, dpm, "tn", "d_w_main", 512, 1024, 512)
    d_wlora = _matmul(h, dplo, "tn", "d_w_lora", 512, LORA, 512)
    dh_m = _matmul(dpm, wmain, "nt", "d_h_main", 512, 1024, 1024)
    dh_l = _matmul(dplo, wlora, "nt", "d_h_lora", 512, 1024, LORA)

    def pre_bwd(r, b, g):
        _, vjp = jax.vjp(_s1, r[0], g[0], b[0], b[1])
        dx, dg, dscale, dshift = vjp(r[1] + r[2])
        return [dx + r[3]], [dscale, dshift], [dg]

    (gx,), (dscale, dshift), (d_ng,) = _rows(pre_bwd, "pre_bwd", T, S, 256,
                                             [(x2d, D, 0), (dh_m, D, 0), (dh_l, D, 0), (dx_res, D, 0)],
                                             [scale, shift], [small["norm_g"]], [(D, F32)], [D, D], [(1, D)])
    dmod = jnp.concatenate([dshift, dscale, dgate], axis=-1).reshape(nb, 3 * D)
    d_small = {"norm_g": d_ng, "conv_b": d_cb, "conv_ln_g": d_lg, "conv_ln_b": d_lb,
               "rwkv_mu": jnp.concatenate([d_mu_r, d_mu_k, d_mu_v, d_mu_l], axis=1),
               "rwkv_w0": d_w0, "rwkv_a0": d_a0, "rwkv_k_k": d_kk, "rwkv_k_a": d_ka, "rwkv_r_k": d_rk,
               "rwkv_gn_g": d_gg, "rwkv_gn_b": d_gb, "final_g": d_final_g}
    big = {"wmain": d_wmain, "wlora": d_wlora, "wco": d_wco, "wro": d_wro, "wo": d_wo,
           "ck": d_ckp[:CW], "w2": d_w2p[:64], "a2": d_a2p[64:]}
    return loss_v[0, 0], gx, dmod, big, d_small


def _step(a):
    nb, S, _ = a["x"].shape
    T = nb * S
    x_i, y_i, c_i = _place()
    chip = 2 * x_i + y_i
    w_in = a["w_in"][0]
    WS = w_in.shape[1]
    small_w = {n: a[n].reshape(1, sz) for n, sz in SMALL}

    def halves(t):
        return t.reshape(2, t.shape[0] // 2, t.shape[1])

    g_ins = [a["c"]] + [halves(a[n][0].astype(BF16)) for n in ("w_in", "w_conv_out", "w_rwkv_out", "w_out")]
    g_ins += [a["conv_k"][0], a["rwkv_w2"][0], a["rwkv_a2"][0]]
    g_out = [jax.ShapeDtypeStruct((NDEV,) + g_ins[0].shape, F32)]
    g_out += [jax.ShapeDtypeStruct((NCHIP,) + t.shape, t.dtype) for t in g_ins[1:]]
    c_all, win_g, wco_g, wro_g, wo_g, ck_g, w2_g, a2_g = _comm_call(
        "gather_weights", g_ins, g_out, functools.partial(_gather_plan, 4), 7 + 3 * 7, 3 * 4)
    c_all = c_all.reshape(NDEV * nb, D)
    win_g = win_g.reshape(NCHIP, D, WS)
    win_full = jnp.concatenate([win_g[j] for j in range(NCHIP)], axis=1)
    wmain = jnp.concatenate([win_full[:, :6 * D], win_full[:, 6 * D + LORA:]], axis=1)
    wlora = win_full[:, 6 * D:6 * D + LORA]
    wco, wro, wo = (t.reshape(D, D) for t in (wco_g, wro_g, wo_g))
    ck = jnp.concatenate([ck_g[j] for j in range(NCHIP)], axis=1)
    w2 = jnp.concatenate([w2_g[j] for j in range(NCHIP)], axis=1)
    a2 = jnp.concatenate([a2_g[j] for j in range(NCHIP)], axis=1)

    ada_w = a["ada_w"][0]
    MW = ada_w.shape[1]
    ada_b_loc = lax.dynamic_slice(a["ada_b"], (0, chip * MW), (1, MW))

    def mod_body(c_ref, w_ref, b_ref, o_ref):
        o_ref[...] = _dot(_silu(c_ref[...]), w_ref[...], HI) + b_ref[...]

    modp = pl.pallas_call(mod_body, name="ada_mod", out_shape=jax.ShapeDtypeStruct((NDEV * nb, MW), F32),
                          compiler_params=_cparams())(c_all, ada_w, ada_b_loc)
    (mod_g,) = _comm_call("scatter_mod", [modp.reshape(NDEV, nb, MW)],
                          [jax.ShapeDtypeStruct((NDEV, nb, MW), F32)],
                          functools.partial(_scatter_plan, 1), 7)
    mod = mod_g.reshape(NCHIP, 2, nb, MW)
    mod = mod[:, 0].transpose(1, 0, 2).reshape(nb, NCHIP * MW)

    loss_p, gx, dmod, big, d_small = _local_step(
        a["x"].reshape(T, D), a["loss_target"].reshape(T, D), mod, wmain, wlora, wco, wro, wo, ck, w2, a2,
        small_w, nb, S)
    loss = lax.psum(loss_p, ("x", "y", "c"))

    d_small["ada_b"] = _colsum(dmod)
    small_vec = jnp.concatenate([d_small[n] for n, _ in SMALL], axis=1)
    dmod_s = dmod.reshape(nb, NCHIP, MW).transpose(1, 0, 2)
    dmod_s = jnp.repeat(dmod_s, 2, axis=0)
    small_s = jnp.broadcast_to(small_vec[None], (NDEV, 1, NSMALL))
    RW = D // NCHIP
    d_win = jnp.concatenate([big["wmain"][:, :6 * D], big["wlora"], big["wmain"][:, 6 * D:]], axis=1)
    hv = [d_win.reshape(2, D // 2, NCHIP * WS)]
    hv += [big[n].reshape(NCHIP, 2, RW // 2, D).transpose(1, 0, 2, 3).reshape(2, NCHIP * RW // 2, D)
           for n in ("wco", "wro", "wo")]
    hv += [big[n].reshape(-1, NCHIP, 2, RW // 2).transpose(2, 1, 0, 3).reshape(2, -1, RW // 2) for n in ("ck", "w2", "a2")]
    h_out = [jax.ShapeDtypeStruct(t.shape[1:], F32) for t in hv] * 2
    kept_got = _comm_call("halve_grads", hv, h_out, _halve_plan, len(hv))
    chip_part = [_ew(lambda p, q: [p + q], "chip_sum_%d" % i, [kept_got[i], kept_got[len(hv) + i]], 1, 128, BF16)[0]
                 for i in range(len(hv))]
    sh_s = [jnp.stack([chip_part[0][:, j * WS:(j + 1) * WS] for j in range(NCHIP)])]
    sh_s += [t.reshape(NCHIP, RW // 2, D) for t in chip_part[1:4]]
    sh_s += [t.reshape(NCHIP, -1, RW // 2) for t in chip_part[4:]]
    s_ins = [dmod_s, small_s] + sh_s
    s_out = [jax.ShapeDtypeStruct(t.shape, t.dtype) for t in s_ins]
    got = _comm_call("scatter_grads", s_ins, s_out, functools.partial(_scatter_plan, 2), 14 + 3 * len(sh_s))
    dmod_all, small_all = got[0].reshape(NDEV * nb, MW), got[1].reshape(NDEV, NSMALL)
    fin = [_ew(lambda r: [_sum_slots(r.astype(F32))], "shard_sum_%d" % i, [t], 1, 128)[0] for i, t in enumerate(got[2:])]
    full = _comm_call("join_halves", fin, [jax.ShapeDtypeStruct((2,) + t.shape, F32) for t in fin], _join_plan, len(fin))
    grads = [full[0].reshape(D, WS)] + [t.reshape(RW, D) for t in full[1:4]]
    grads += [t.transpose(1, 0, 2).reshape(-1, RW) for t in full[4:]]

    outs = {}

    def upd(name, w, g):
        shp = a[name].shape
        w2d = w.reshape(g.shape)
        m2d, v2d = a["m_" + name].reshape(g.shape), a["v_" + name].reshape(g.shape)
        res = _ew(lambda w_, m_, v_, g_: [g_, *_adamw(w_, g_, m_, v_)], "adamw_" + name, [w2d, m2d, v2d, g], 4, 128)
        outs[name] = [r.reshape(shp) for r in res]

    for name, g in zip(("w_in", "w_conv_out", "w_rwkv_out", "w_out", "conv_k", "rwkv_w2", "rwkv_a2"), grads):
        upd(name, a[name][0], g)

    def adaw_body(c_ref, dm_ref, w_ref, m_ref, v_ref, g_ref, d_ref, m2_ref, v2_ref):
        g = _dot_tn(_silu(c_ref[...]), dm_ref[...], HI)
        g_ref[...] = g
        d_ref[...], m2_ref[...], v2_ref[...] = _adamw(w_ref[...], g, m_ref[...], v_ref[...])

    res = pl.pallas_call(adaw_body, name="adamw_ada_w", out_shape=[jax.ShapeDtypeStruct((D, MW), F32)] * 4,
                         compiler_params=_cparams())(c_all, dmod_all, ada_w, a["m_ada_w"][0], a["v_ada_w"][0])
    outs["ada_w"] = [r.reshape(a["ada_w"].shape) for r in res]

    wv, mv, vv = (jnp.concatenate([a[p + n].reshape(1, sz) for n, sz in SMALL], axis=1) for p in ("", "m_", "v_"))
    def small_fn(w_, m_, v_, gs):
        g = _sum_slots(gs)
        return [g, *_adamw(w_, g, m_, v_)]

    res = _ew(small_fn, "adamw_small", [wv, mv, vv, small_all.reshape(NDEV, 1, NSMALL)], 4, 8)
    off = 0
    for n, sz in SMALL:
        outs[n] = [r[:, off:off + sz].reshape(a[n].shape) for r in res]
        off += sz

    return (loss, gx.reshape(nb, S, D), *[outs[n][0] for n in WEIGHTS], *[outs[n][1] for n in WEIGHTS],
            *[outs[n][2] for n in WEIGHTS], *[outs[n][3] for n in WEIGHTS])


def _colsum(dmod):
    def body(d_ref, o_ref):
        o_ref[...] = jnp.sum(d_ref[...], axis=0, keepdims=True)
    return pl.pallas_call(body, name="ada_b_rowsum", out_shape=jax.ShapeDtypeStruct((1, dmod.shape[1]), F32),
                          compiler_params=_cparams())(dmod)


def kernel(x, c, ada_w, ada_b, norm_g, w_in, conv_k, conv_b, conv_ln_g, conv_ln_b, w_conv_out, rwkv_mu, rwkv_w0, rwkv_w2, rwkv_a0, rwkv_a2, rwkv_k_k, rwkv_k_a, rwkv_r_k, rwkv_gn_g, rwkv_gn_b, w_rwkv_out, w_out, final_g, loss_target, m_ada_w, m_ada_b, m_norm_g, m_w_in, m_conv_k, m_conv_b, m_conv_ln_g, m_conv_ln_b, m_w_conv_out, m_rwkv_mu, m_rwkv_w0, m_rwkv_w2, m_rwkv_a0, m_rwkv_a2, m_rwkv_k_k, m_rwkv_k_a, m_rwkv_r_k, m_rwkv_gn_g, m_rwkv_gn_b, m_w_rwkv_out, m_w_out, m_final_g, v_ada_w, v_ada_b, v_norm_g, v_w_in, v_conv_k, v_conv_b, v_conv_ln_g, v_conv_ln_b, v_w_conv_out, v_rwkv_mu, v_rwkv_w0, v_rwkv_w2, v_rwkv_a0, v_rwkv_a2, v_rwkv_k_k, v_rwkv_k_a, v_rwkv_r_k, v_rwkv_gn_g, v_rwkv_gn_b, v_w_rwkv_out, v_w_out, v_final_g):
    return _step(dict(locals()))
```

```python
import functools

import numpy as np
import jax
import jax.numpy as jnp
from jax import lax
from jax.experimental import pallas as pl
from jax.experimental.pallas import tpu as pltpu

F32 = jnp.float32
BF16 = jnp.bfloat16
HI = lax.Precision.HIGHEST
MESH = pl.DeviceIdType.MESH
ANY = pl.BlockSpec(memory_space=pl.ANY)

D = 1024
NH = 16
HN = 64
LORA = 128
DMAIN = 9 * D
CH = 64
CW = 31
NCHIP = 4
NDEV = 8
VMEM_LIMIT = 56 * 1024 * 1024

RMS_EPS = 1e-6
LN_EPS = 1e-5
GN_EPS = 64e-5
L2_EPS = 1e-12
ADAM_LR = 0.001
ADAM_B1 = 0.9
ADAM_B2 = 0.999
ADAM_EPS = 1e-08
ADAM_WD = 0.01
ADAM_STEP = 10

SMALL = (("ada_b", 3072), ("norm_g", 1024), ("conv_b", 1024), ("conv_ln_g", 1024), ("conv_ln_b", 1024),
         ("rwkv_mu", 3200), ("rwkv_w0", 1024), ("rwkv_a0", 1024), ("rwkv_k_k", 1024), ("rwkv_k_a", 1024),
         ("rwkv_r_k", 1024), ("rwkv_gn_g", 1024), ("rwkv_gn_b", 1024), ("final_g", 1024))
NSMALL = sum(n for _, n in SMALL)

WEIGHTS = ['ada_w', 'ada_b', 'norm_g', 'w_in', 'conv_k', 'conv_b', 'conv_ln_g', 'conv_ln_b', 'w_conv_out', 'rwkv_mu',
           'rwkv_w0', 'rwkv_w2', 'rwkv_a0', 'rwkv_a2', 'rwkv_k_k', 'rwkv_k_a', 'rwkv_r_k', 'rwkv_gn_g', 'rwkv_gn_b',
           'w_rwkv_out', 'w_out', 'final_g']


def _cparams(sem=None, **kw):
    if sem is not None:
        kw["dimension_semantics"] = sem
    return pltpu.CompilerParams(vmem_limit_bytes=VMEM_LIMIT, **kw)


def _dot(a, b, prec=None):
    return jnp.dot(a, b, preferred_element_type=F32, precision=prec)


def _dot_nt(a, b, prec=None):
    return lax.dot_general(a, b, (((1,), (1,)), ((), ())), preferred_element_type=F32, precision=prec)


def _dot_tn(a, b, prec=None):
    return lax.dot_general(a, b, (((0,), (0,)), ((), ())), preferred_element_type=F32, precision=prec)


def _pdot(f, a, b, p):
    if p == "hi":
        return f(a, b, HI)
    ah, bh = a.astype(BF16), b.astype(BF16)
    if p == "bf":
        return f(ah, bh)
    al, bl = (a - ah.astype(F32)).astype(BF16), (b - bh.astype(F32)).astype(BF16)
    return f(ah, bh) + (f(ah, bl) + f(al, bh))


P_SCORE = "b3"
P_INV = "bf"
P_APPLY = "bf"


def _sigmoid(z):
    return 1.0 / (1.0 + jnp.exp(-z))


def _silu(z):
    return z * _sigmoid(z)


def _matmul(a, b, mode, name, tm, tn, tk):
    if mode == "nn":
        (M, K), N = a.shape, b.shape[1]
        a_spec = pl.BlockSpec((tm, tk), lambda j, i, k: (i, k))
        b_spec = pl.BlockSpec((tk, tn), lambda j, i, k: (k, j))
        f = _dot
    elif mode == "nt":
        (M, K), N = a.shape, b.shape[0]
        a_spec = pl.BlockSpec((tm, tk), lambda j, i, k: (i, k))
        b_spec = pl.BlockSpec((tn, tk), lambda j, i, k: (j, k))
        f = _dot_nt
    else:
        (K, M), N = a.shape, b.shape[1]
        a_spec = pl.BlockSpec((tk, tm), lambda j, i, k: (k, i))
        b_spec = pl.BlockSpec((tk, tn), lambda j, i, k: (k, j))
        f = _dot_tn
    assert M % tm == 0 and N % tn == 0 and K % tk == 0, (name, M, N, K)

    def body(a_ref, b_ref, o_ref):
        @pl.when(pl.program_id(2) == 0)
        def _():
            o_ref[...] = jnp.zeros_like(o_ref)
        o_ref[...] += f(a_ref[...], b_ref[...])

    return pl.pallas_call(
        body, name=name, grid=(N // tn, M // tm, K // tk),
        in_specs=[a_spec, b_spec],
        out_specs=pl.BlockSpec((tm, tn), lambda j, i, k: (i, j)),
        out_shape=jax.ShapeDtypeStruct((M, N), F32),
        compiler_params=_cparams(("parallel", "parallel", "arbitrary")),
    )(a, b)


def _rows(fn, name, T, S, tm, rows, bpars, gpars, outs, baccs, gaccs):
    nb = T // S
    tps = S // tm
    n_r, n_b, n_g, n_o, n_ba, n_ga = len(rows), len(bpars), len(gpars), len(outs), len(baccs), len(gaccs)

    def body(*refs):
        r_refs = refs[:n_r]
        b_refs = refs[n_r:n_r + n_b]
        g_refs = refs[n_r + n_b:n_r + n_b + n_g]
        o_refs = refs[n_r + n_b + n_g:n_r + n_b + n_g + n_o]
        ba_refs = refs[n_r + n_b + n_g + n_o:n_r + n_b + n_g + n_o + n_ba]
        ga_refs = refs[n_r + n_b + n_g + n_o + n_ba:]
        i = pl.program_id(0)
        o_vals, ba_vals, ga_vals = fn([r[...] for r in r_refs], [r[...] for r in b_refs], [r[...] for r in g_refs])
        for r, v in zip(o_refs, o_vals):
            r[...] = v.astype(r.dtype)
        if n_ba:
            @pl.when(i % tps == 0)
            def _():
                for r in ba_refs:
                    r[...] = jnp.zeros_like(r)
            for r, v in zip(ba_refs, ba_vals):
                r[...] += v.reshape(r.shape)
        if n_ga:
            @pl.when(i == 0)
            def _():
                for r in ga_refs:
                    r[...] = jnp.zeros_like(r)
            for r, v in zip(ga_refs, ga_vals):
                r[...] += v.reshape(r.shape)

    def row_spec(w, cb, kind="tile"):
        if kind == "prev":
            return pl.BlockSpec((8, w), lambda i: (jnp.maximum(i * (tm // 8) - 1, 0), cb))
        if kind == "next":
            return pl.BlockSpec((8, w), lambda i: (jnp.minimum((i + 1) * (tm // 8), T // 8 - 1), cb))
        return pl.BlockSpec((tm, w), lambda i: (i, cb))

    in_specs = [row_spec(*r[1:]) for r in rows]
    in_specs += [pl.BlockSpec((None, 1, p.shape[-1]), lambda i: (i // tps, 0, 0)) for p in bpars]
    in_specs += [pl.BlockSpec(p.shape, lambda i: (0, 0)) for p in gpars]
    out_specs = [pl.BlockSpec((tm, w), lambda i: (i, 0)) for w, _ in outs]
    out_specs += [pl.BlockSpec((None, 1, w), lambda i: (i // tps, 0, 0)) for w in baccs]
    out_specs += [pl.BlockSpec(s, lambda i: (0, 0)) for s in gaccs]
    out_shape = [jax.ShapeDtypeStruct((T, w), dt) for w, dt in outs]
    out_shape += [jax.ShapeDtypeStruct((nb, 1, w), F32) for w in baccs]
    out_shape += [jax.ShapeDtypeStruct(s, F32) for s in gaccs]
    res = pl.pallas_call(
        body, name=name, grid=(T // tm,), in_specs=in_specs, out_specs=out_specs, out_shape=out_shape,
        compiler_params=_cparams(("arbitrary",)),
    )(*[r[0] for r in rows], *bpars, *gpars)
    return res[:n_o], res[n_o:n_o + n_ba], res[n_o + n_ba:]


@jax.custom_vjp
def _gsum(z, G):
    zh = z.astype(BF16)
    zl = (z - zh.astype(F32)).astype(BF16)
    return _dot(zh, G) + _dot(zl, G)


_gsum.defvjp(lambda z, G: (_gsum(z, G), G), lambda G, ct: (_gsum(ct, G), jnp.zeros_like(G)))


def _s1(x, g, scale, shift):
    y = x * lax.rsqrt(jnp.mean(x * x, axis=-1, keepdims=True) + RMS_EPS)
    return (y * g) * (1.0 + scale) + shift


def _s3(uc, og, cb, lg, lb):
    u = uc + cb
    mu = jnp.mean(u, axis=-1, keepdims=True)
    d = u - mu
    var = jnp.mean(d * d, axis=-1, keepdims=True)
    y = d * lax.rsqrt(var + LN_EPS) * lg + lb
    return _silu(y) * _silu(og)


def _s4(r0, k0, v0, l0, pr, pk, pv, plo, mu_r, mu_k, mu_v, mu_l, w0, w2p, a0, a2p, k_k, k_a, G):
    r = r0 + mu_r * (pr - r0)
    k = k0 + mu_k * (pk - k0)
    v = v0 + mu_v * (pv - v0)
    lo = l0 + mu_l * (plo - l0)
    w_pre = w0 + _dot(jnp.tanh(lo), w2p, HI)
    lw = -np.float32(np.exp(-0.5)) * _sigmoid(w_pre)
    a = _sigmoid(a0 + _dot(lo, a2p, HI))
    kkr = k * k_k
    ss = _gsum(kkr * kkr, G)
    kk = kkr / jnp.maximum(jnp.sqrt(ss), L2_EPS)
    k2 = k * (1.0 + (a - 1.0) * k_a)
    return r, lw, k2, v, kk, kk * a


def _s5(o, r, k2, v, og, gg, gb, rk, G):
    mu = _gsum(o, G) * (1.0 / HN)
    d = o - mu
    var = _gsum(d * d, G) * (1.0 / HN)
    y = d * lax.rsqrt(var + GN_EPS) * gg + gb
    bonus = _gsum(r * k2 * rk, G)
    return (y + bonus * v) * _silu(og)


def _s6(yc, yr, gc, gr):
    return _sigmoid(gc) * yc + _sigmoid(gr) * yr


def _s7(x, out, tgt, gate, fg):
    x2 = x + gate * out
    y = x2 * lax.rsqrt(jnp.mean(x2 * x2, axis=-1, keepdims=True) + RMS_EPS) * fg
    e = y - tgt
    return 0.5 * jnp.sum(jnp.mean(e * e, axis=-1))


def _chunk(sts, r, lw, k, v, kk, b, cm):
    cum = _dot(cm[0], lw, HI)
    ein = jnp.exp(-cum)
    rt = r * jnp.exp(cum)
    kkt = kk * jnp.exp(cum - lw)
    kh = k * ein
    bh = b * ein
    ec = jnp.exp(jnp.sum(lw, axis=0, keepdims=True))
    khe = kh * ec
    bhe = bh * ec
    H = range(len(sts))
    tri, strict, eye = cm[0], cm[1], cm[2]
    rt, kkt, kh, bh, v, khe, bhe, ec = ([a[:, j * HN:(j + 1) * HN] for j in H] for a in (rt, kkt, kh, bh, v, khe, bhe, ec))
    a_kb = [strict * _pdot(_dot_nt, kkt[j], bh[j], P_SCORE) for j in H]
    a_kk = [strict * _pdot(_dot_nt, kkt[j], kh[j], P_SCORE) for j in H]
    a_rk = [tri * _pdot(_dot_nt, rt[j], kh[j], P_SCORE) for j in H]
    a_rb = [tri * _pdot(_dot_nt, rt[j], bh[j], P_SCORE) for j in H]
    rhs = [_pdot(_dot_nt, kkt[j], sts[j], P_APPLY) + _pdot(_dot, a_kk[j], v[j], P_APPLY) for j in H]
    o0 = [_pdot(_dot_nt, rt[j], sts[j], P_APPLY) + _pdot(_dot, a_rk[j], v[j], P_APPLY) for j in H]
    xi = [eye - cm[3] * a_kb[j] for j in H]
    for lvl in range(1, 6):
        t = [_pdot(_dot, xi[j], cm[3 + lvl] * a_kb[j], P_INV) for j in H]
        xi = [xi[j] - _pdot(_dot, t[j], xi[j], P_INV) for j in H]
    u = [_pdot(_dot, xi[j], rhs[j], P_APPLY) for j in H]
    o = [o0[j] - _pdot(_dot, a_rb[j], u[j], P_APPLY) for j in H]
    st2 = [sts[j] * ec[j] + _pdot(_dot_tn, v[j], khe[j], P_APPLY) - _pdot(_dot_tn, u[j], bhe[j], P_APPLY) for j in H]
    return jnp.concatenate(o, axis=1), tuple(st2)


def _chunk_consts():
    t = np.arange(CH)[:, None]
    s = np.arange(CH)[None, :]
    mats = [(t >= s), (t > s), (t == s)]
    for lvl in range(6):
        sz = 1 << lvl
        mats.append(((t // sz) % 2 == 1) & ((s // sz) == (t // sz) - 1))
    mats.append(np.zeros((CH, CH), bool))
    return np.stack(mats).astype(np.float32)


def _adamw(w, g, m, v):
    m = ADAM_B1 * m + (1.0 - ADAM_B1) * g
    v = ADAM_B2 * v + (1.0 - ADAM_B2) * (g * g)
    m_hat = m / (1.0 - ADAM_B1 ** ADAM_STEP)
    v_hat = v / (1.0 - ADAM_B2 ** ADAM_STEP)
    delta = -ADAM_LR * (m_hat / (jnp.sqrt(v_hat) + ADAM_EPS) + ADAM_WD * w)
    return delta, m, v


CT = 128
RB = 64
WIN = RB + 32


def _conv_fwd(pm, ck, T, S):
    nb = T // S

    def body(val_ref, gate_ref, ck_ref, out_ref, ubuf):
        ubuf[0:32, :] = jnp.zeros((32, CT), F32)
        ubuf[32:, :] = val_ref[...] * _sigmoid(gate_ref[...])

        def blk(rb, carry):
            base = pl.multiple_of(rb * RB, RB)
            win = ubuf[pl.ds(base, WIN), :]
            acc = jnp.zeros((RB, CT), F32)
            for j in range(CW):
                acc = acc + ck_ref[j:j + 1, :] * pltpu.roll(win, (WIN - (2 + j)) % WIN, 0)[0:RB, :]
            out_ref[pl.ds(base, RB), :] = acc
            return carry

        lax.fori_loop(0, S // RB, blk, 0)

    return pl.pallas_call(
        body, name="conv_fwd", grid=(D // CT, nb),
        in_specs=[pl.BlockSpec((S, CT), lambda ct, b: (b, ct)),
                  pl.BlockSpec((S, CT), lambda ct, b: (b, D // CT + ct)),
                  pl.BlockSpec((32, CT), lambda ct, b: (0, ct))],
        out_specs=pl.BlockSpec((S, CT), lambda ct, b: (b, ct)),
        out_shape=jax.ShapeDtypeStruct((T, D), F32),
        scratch_shapes=[pltpu.VMEM((S + 32, CT), F32)],
        compiler_params=_cparams(("parallel", "arbitrary")),
    )(pm, pm, ck)


def _conv_bwd(pm, duc, ck, T, S):
    nb = T // S

    def body(val_ref, gate_ref, duc_ref, ck_ref, dval_ref, dgate_ref, dck_ref, ubuf, dbuf, acc):
        b = pl.program_id(1)
        ubuf[0:32, :] = jnp.zeros((32, CT), F32)
        ubuf[32:, :] = val_ref[...] * _sigmoid(gate_ref[...])
        dbuf[0:S, :] = duc_ref[...]
        dbuf[S:, :] = jnp.zeros((32, CT), F32)
        acc[...] = jnp.zeros_like(acc)

        def blk(rb, carry):
            base = pl.multiple_of(rb * RB, RB)
            uwin = ubuf[pl.ds(base, WIN), :]
            dwin = dbuf[pl.ds(base, WIN), :]
            dblk = dwin[0:RB, :]
            du = jnp.zeros((RB, CT), F32)
            for j in range(CW):
                du = du + ck_ref[j:j + 1, :] * pltpu.roll(dwin, (WIN - (CW - 1 - j)) % WIN, 0)[0:RB, :]
                ush = pltpu.roll(uwin, (WIN - (2 + j)) % WIN, 0)[0:RB, :]
                acc[j] += jnp.sum((dblk * ush).reshape(RB // 8, 8, CT), axis=0)
            val = val_ref[pl.ds(base, RB), :]
            sg = _sigmoid(gate_ref[pl.ds(base, RB), :])
            dval_ref[pl.ds(base, RB), :] = du * sg
            dgate_ref[pl.ds(base, RB), :] = du * val * sg * (1.0 - sg)
            return carry

        lax.fori_loop(0, S // RB, blk, 0)

        @pl.when(b == 0)
        def _():
            dck_ref[...] = jnp.zeros_like(dck_ref)
        for j in range(CW):
            dck_ref[j:j + 1, :] += jnp.sum(acc[j], axis=0, keepdims=True)

    return pl.pallas_call(
        body, name="conv_bwd", grid=(D // CT, nb),
        in_specs=[pl.BlockSpec((S, CT), lambda ct, b: (b, ct)),
                  pl.BlockSpec((S, CT), lambda ct, b: (b, D // CT + ct)),
                  pl.BlockSpec((S, CT), lambda ct, b: (b, ct)),
                  pl.BlockSpec((32, CT), lambda ct, b: (0, ct))],
        out_specs=[pl.BlockSpec((S, CT), lambda ct, b: (b, ct)),
                   pl.BlockSpec((S, CT), lambda ct, b: (b, ct)),
                   pl.BlockSpec((32, CT), lambda ct, b: (0, ct))],
        out_shape=[jax.ShapeDtypeStruct((T, D), F32), jax.ShapeDtypeStruct((T, D), F32),
                   jax.ShapeDtypeStruct((32, D), F32)],
        scratch_shapes=[pltpu.VMEM((S + 32, CT), F32), pltpu.VMEM((S + 32, CT), F32), pltpu.VMEM((32, 8, CT), F32)],
        compiler_params=_cparams(("parallel", "arbitrary")),
    )(pm, pm, duc, ck)


HB = 16


def _scan_fwd(ins, cm, nb, S):
    nc = S // CH
    blk = pl.BlockSpec((CH, HB * HN), lambda b, g, i: (b * nc + i, g))
    hblk = pl.BlockSpec((None, HB, None, HN, HN), lambda b, g, i: (b, g, i, 0, 0))

    def body(r_ref, lw_ref, k_ref, v_ref, kk_ref, b_ref, cm_ref, o_ref, hs_ref, st):
        @pl.when(pl.program_id(2) == 0)
        def _():
            st[...] = jnp.zeros_like(st)
        s0 = [st[j] for j in range(HB)]
        for j in range(HB):
            hs_ref[j] = s0[j]
        o, s1 = _chunk(s0, r_ref[...], lw_ref[...], k_ref[...], v_ref[...], kk_ref[...], b_ref[...], cm_ref[...])
        o_ref[...] = o
        for j in range(HB):
            st[j] = s1[j]

    return pl.pallas_call(
        body, name="scan_fwd", grid=(nb, NH // HB, nc),
        in_specs=[blk] * 6 + [pl.BlockSpec(cm.shape, lambda b, g, i: (0, 0, 0))],
        out_specs=[blk, hblk],
        out_shape=[jax.ShapeDtypeStruct((nb * S, D), F32), jax.ShapeDtypeStruct((nb, NH, nc, HN, HN), F32)],
        scratch_shapes=[pltpu.VMEM((HB, HN, HN), F32)],
        compiler_params=_cparams(("parallel", "parallel", "arbitrary")),
    )(*ins, cm)


def _scan_bwd(ins, hs, do, cm, nb, S):
    nc = S // CH
    blk = pl.BlockSpec((CH, HB * HN), lambda b, g, i: (b * nc + nc - 1 - i, g))
    hblk = pl.BlockSpec((None, HB, None, HN, HN), lambda b, g, i: (b, g, nc - 1 - i, 0, 0))

    def body(r_ref, lw_ref, k_ref, v_ref, kk_ref, b_ref, hs_ref, do_ref, cm_ref,
             dr_ref, dlw_ref, dk_ref, dv_ref, dkk_ref, db_ref, dst):
        @pl.when(pl.program_id(2) == 0)
        def _():
            dst[...] = jnp.zeros_like(dst)
        cmv = cm_ref[...]
        f = lambda s0, r, lw, k, v, kk, b: _chunk(s0, r, lw, k, v, kk, b, cmv)
        _, vjp = jax.vjp(f, [hs_ref[j] for j in range(HB)], r_ref[...], lw_ref[...], k_ref[...], v_ref[...],
                         kk_ref[...], b_ref[...])
        ds0, dr, dlw, dk, dv, dkk, db = vjp((do_ref[...], tuple(dst[j] for j in range(HB))))
        for j in range(HB):
            dst[j] = ds0[j]
        dr_ref[...] = dr
        dlw_ref[...] = dlw
        dk_ref[...] = dk
        dv_ref[...] = dv
        dkk_ref[...] = dkk
        db_ref[...] = db

    return pl.pallas_call(
        body, name="scan_bwd", grid=(nb, NH // HB, nc),
        in_specs=[blk] * 6 + [hblk, blk, pl.BlockSpec(cm.shape, lambda b, g, i: (0, 0, 0))],
        out_specs=[blk] * 6,
        out_shape=[jax.ShapeDtypeStruct((nb * S, D), F32)] * 6,
        scratch_shapes=[pltpu.VMEM((HB, HN, HN), F32)],
        compiler_params=_cparams(("parallel", "parallel", "arbitrary")),
    )(*ins, hs, do, cm)


def _ew(fn, name, ins, n_out, tm, out_dtype=F32):
    R, W = ins[0].shape[-2:]
    tm = min(tm, R)
    assert R % tm == 0

    def body(*refs):
        vals = fn(*[r[...] for r in refs[:len(ins)]])
        for r, v in zip(refs[len(ins):], vals):
            r[...] = v.astype(r.dtype)

    def spec(a):
        if a.ndim == 3:
            return pl.BlockSpec((a.shape[0], tm, W), lambda i: (0, i, 0))
        return pl.BlockSpec((tm, W), lambda i: (i, 0))

    return pl.pallas_call(
        body, name=name, grid=(R // tm,), in_specs=[spec(a) for a in ins],
        out_specs=[pl.BlockSpec((tm, W), lambda i: (i, 0))] * n_out,
        out_shape=[jax.ShapeDtypeStruct((R, W), out_dtype)] * n_out,
        compiler_params=_cparams(("parallel",)),
    )(*ins)


def _sum_slots(r):
    s = r[0]
    for j in range(1, r.shape[0]):
        s = s + r[j]
    return s


def _place():
    x, y, c = lax.axis_index("x"), lax.axis_index("y"), lax.axis_index("c")
    return x, y, c


def _flip(v, d):
    return 1 - v if d else v


CHIP_PEERS = ((1, 0), (0, 1), (1, 1))
DEV_PEERS = tuple((dx, dy, dc) for dx in (0, 1) for dy in (0, 1) for dc in (0, 1))[1:]


def _comm_call(name, ins, out_shapes, plan, n_rem, n_fwd=0):
    n_in = len(ins)

    def body(*refs):
        in_refs, out_refs = refs[:n_in], refs[n_in:n_in + len(out_shapes)]
        send_sems, recv_sems, loc_sems = refs[n_in + len(out_shapes):]
        loc, rem, *rest = plan(in_refs, out_refs, _place())
        fwd = rest[0] if rest else []
        assert len(rem) == n_rem and len(fwd) == n_fwd and len(loc) <= 2 * n_in, (name, len(loc), len(rem), len(fwd))

        def remote(i, s, d, peer):
            return pltpu.make_async_remote_copy(src_ref=s, dst_ref=d, send_sem=send_sems.at[i], recv_sem=recv_sems.at[i],
                                                device_id=peer, device_id_type=MESH)

        copies = [pltpu.make_async_copy(s, d, loc_sems.at[i]) for i, (s, d) in enumerate(loc)]
        rcopies = [remote(i, s, d, peer) for i, (s, d, peer) in enumerate(rem)]
        for cp in copies + rcopies:
            cp.start()
        landed = set()
        fcopies = []
        for i, (s, d, peer, k) in enumerate(fwd):
            if k not in landed:
                rcopies[k].wait_recv()
                landed.add(k)
            fcopies.append(remote(n_rem + i, s, d, peer))
            fcopies[-1].start()
        for k, cp in enumerate(rcopies):
            if k not in landed:
                cp.wait_recv()
        for cp in rcopies + fcopies:
            cp.wait_send()
        for cp in fcopies:
            cp.wait_recv()
        for cp in copies:
            cp.wait()

    return pl.pallas_call(
        body, name=name, in_specs=[ANY] * n_in, out_specs=[ANY] * len(out_shapes), out_shape=out_shapes,
        scratch_shapes=[pltpu.SemaphoreType.DMA((n_rem + n_fwd,)), pltpu.SemaphoreType.DMA((n_rem + n_fwd,)),
                        pltpu.SemaphoreType.DMA((2 * n_in,))],
        compiler_params=pltpu.CompilerParams(has_side_effects=True),
    )(*ins)


def _gather_plan(n_big, in_refs, out_refs, place):
    x, y, c = place
    chip, dev = 2 * x + y, 4 * x + 2 * y + c
    sib = (x, y, 1 - c)
    loc = [(in_refs[0], out_refs[0].at[dev])] + [(s, d.at[chip]) for s, d in zip(in_refs[1 + n_big:], out_refs[1 + n_big:])]
    rem = [(in_refs[0], out_refs[0].at[dev], (_flip(x, dx), _flip(y, dy), _flip(c, dc))) for dx, dy, dc in DEV_PEERS]
    fwd = []
    for s, d in zip(in_refs[1:1 + n_big], out_refs[1:1 + n_big]):
        for dx, dy in CHIP_PEERS:
            px, py = _flip(x, dx), _flip(y, dy)
            fwd.append((d.at[2 * px + py, c], d.at[2 * px + py, c], sib, len(rem)))
            rem.append((s.at[c], d.at[chip, c], (px, py, c)))
    for s, d in zip(in_refs[1 + n_big:], out_refs[1 + n_big:]):
        rem += [(s, d.at[chip], (_flip(x, dx), _flip(y, dy), c)) for dx, dy in CHIP_PEERS]
    return loc, rem, fwd


def _halve_plan(in_refs, out_refs, place):
    x, y, c = place
    return [], [(s.at[1 - c], d, (x, y, 1 - c)) for s, d in zip(in_refs, out_refs)]


def _join_plan(in_refs, out_refs, place):
    x, y, c = place
    return [], [(s, d, (x, y, 1 - c)) for s, d in zip(in_refs, out_refs)]


def _scatter_plan(n_all, in_refs, out_refs, place):
    x, y, c = place
    chip, dev = 2 * x + y, 4 * x + 2 * y + c
    loc, rem = [], []
    for s, d in zip(in_refs[:n_all], out_refs[:n_all]):
        loc.append((s.at[dev], d.at[dev]))
        for dx, dy, dc in DEV_PEERS:
            px, py, pc = _flip(x, dx), _flip(y, dy), _flip(c, dc)
            rem.append((s.at[4 * px + 2 * py + pc], d.at[dev], (px, py, pc)))
    for s, d in zip(in_refs[n_all:], out_refs[n_all:]):
        for dx, dy in CHIP_PEERS:
            px, py = _flip(x, dx), _flip(y, dy)
            rem.append((s.at[2 * px + py], d.at[chip], (px, py, c)))
    return loc, rem


def _bshape(a, nb):
    return a.reshape(nb, 1, a.shape[-1])


def _with_prev(cur, before, tiles_per_seq):
    first = pl.program_id(0) % tiles_per_seq == 0
    row0 = jnp.where(first, 0.0, before[7:8, :])
    rid = lax.broadcasted_iota(jnp.int32, cur.shape, 0)
    return jnp.where(rid == 0, row0, pltpu.roll(cur, 1, 0))


def _with_next(cur, after, tiles_per_seq):
    last = pl.program_id(0) % tiles_per_seq == tiles_per_seq - 1
    n = cur.shape[0]
    row_n = jnp.where(last, 0.0, after[0:1, :])
    rid = lax.broadcasted_iota(jnp.int32, cur.shape, 0)
    return jnp.where(rid == n - 1, row_n, pltpu.roll(cur, n - 1, 0))


def _local_step(x2d, tgt, mod, wmain, wlora, wco, wro, wo, ck, w2, a2, small, nb, S):
    T = nb * S
    shift, scale, gate = (_bshape(mod[:, i * D:(i + 1) * D], nb) for i in range(3))
    G = jnp.asarray(np.kron(np.eye(NH, dtype=np.float32), np.ones((HN, HN), np.float32)), dtype=BF16)
    cm = jnp.asarray(_chunk_consts())
    ckp = jnp.pad(ck, ((0, 1), (0, 0)))
    zpad = jnp.zeros((64, D), F32)
    w2p = jnp.concatenate([w2, zpad], axis=0)
    a2p = jnp.concatenate([zpad, a2], axis=0)
    mu = small["rwkv_mu"]
    mu_r, mu_k, mu_v, mu_l = mu[:, 0:D], mu[:, D:2 * D], mu[:, 2 * D:3 * D], mu[:, 3 * D:]
    g4 = [mu_r, mu_k, mu_v, mu_l, small["rwkv_w0"], w2p, small["rwkv_a0"], a2p, small["rwkv_k_k"], small["rwkv_k_a"], G]
    g5 = [small["rwkv_gn_g"], small["rwkv_gn_b"], small["rwkv_r_k"], G]
    g3 = [small["conv_b"], small["conv_ln_g"], small["conv_ln_b"]]

    (h,), _, _ = _rows(lambda r, b, g: ([_s1(r[0], g[0], b[0], b[1])], [], []), "pre_fwd", T, S, 256,
                       [(x2d, D, 0)], [scale, shift], [small["norm_g"]], [(D, BF16)], [], [])
    pm = _matmul(h, wmain, "nn", "proj_main", 512, 1024, D)
    plo = _matmul(h, wlora, "nn", "proj_lora", 512, LORA, D)
    uc = _conv_fwd(pm, ckp, T, S)
    (uo,), _, _ = _rows(lambda r, b, g: ([_s3(r[0], r[1], *g)], [], []), "conv_post_fwd", T, S, 256,
                        [(uc, D, 0), (pm, D, 2)], [], g3, [(D, BF16)], [], [])
    yc = _matmul(uo, wco, "nn", "conv_out", 512, 1024, D)
    rows4 = [(pm, D, 3), (pm, D, 4), (pm, D, 5), (plo, LORA, 0),
             (pm, D, 3, "prev"), (pm, D, 4, "prev"), (pm, D, 5, "prev"), (plo, LORA, 0, "prev")]
    tps4 = S // 128

    def shifted4(r):
        return list(r[:4]) + [_with_prev(r[i], r[4 + i], tps4) for i in range(4)]

    sc_in, _, _ = _rows(lambda r, b, g: (list(_s4(*shifted4(r), *g)), [], []), "rwkv_pre_fwd", T, S, 128,
                        rows4, [], g4, [(D, F32)] * 6, [], [])
    o, hs = _scan_fwd(sc_in, cm, nb, S)
    rows5 = [(o, D, 0), (sc_in[0], D, 0), (sc_in[2], D, 0), (sc_in[3], D, 0), (pm, D, 6)]
    (o2,), _, _ = _rows(lambda r, b, g: ([_s5(*r, *g)], [], []), "rwkv_post_fwd", T, S, 128,
                        rows5, [], g5, [(D, BF16)], [], [])
    yr = _matmul(o2, wro, "nn", "rwkv_out", 512, 1024, D)
    rows6 = [(yc, D, 0), (yr, D, 0), (pm, D, 7), (pm, D, 8)]
    (m,), _, _ = _rows(lambda r, b, g: ([_s6(*r)], [], []), "merge_fwd", T, S, 256, rows6, [], [], [(D, BF16)], [], [])
    out = _matmul(m, wo, "nn", "out_proj", 512, 1024, D)

    def head(r, b, g):
        loss, (dx, dout, dgate, dfg) = jax.value_and_grad(_s7, argnums=(0, 1, 3, 4))(r[0], r[1], r[2], b[0], g[0])
        return [dx, dout], [dgate], [dfg, jnp.full((1, 128), loss, F32)]

    (dx_res, dout), (dgate,), (d_final_g, loss_v) = _rows(
        head, "head", T, S, 256, [(x2d, D, 0), (out, D, 0), (tgt, D, 0)], [gate], [small["final_g"]],
        [(D, F32), (D, BF16)], [D], [(1, D), (1, 128)])

    d_wo = _matmul(m, dout, "tn", "d_w_out", 512, 1024, 512)
    dm = _matmul(dout, wo, "nt", "d_merge", 512, 1024, D)

    def merge_bwd(r, b, g):
        _, vjp = jax.vjp(_s6, *r[:4])
        dyc, dyr, dgc, dgr = vjp(r[4])
        return [dyc, dyr, dgc, dgr], [], []

    (dyc, dyr, dgc, dgr), _, _ = _rows(merge_bwd, "merge_bwd", T, S, 256, rows6 + [(dm, D, 0)], [], [],
                                       [(D, BF16), (D, BF16), (D, BF16), (D, BF16)], [], [])
    d_wco = _matmul(uo, dyc, "tn", "d_w_conv_out", 512, 1024, 512)
    d_wro = _matmul(o2, dyr, "tn", "d_w_rwkv_out", 512, 1024, 512)
    duo = _matmul(dyc, wco, "nt", "d_conv_act", 512, 1024, D)
    do2 = _matmul(dyr, wro, "nt", "d_rwkv_act", 512, 1024, D)

    def conv_post_bwd(r, b, g):
        _, vjp = jax.vjp(_s3, r[0], r[1], *g)
        duc, dog, dcb, dlg, dlb = vjp(r[2])
        return [duc, dog], [], [dcb, dlg, dlb]

    (duc, dcog), _, (d_cb, d_lg, d_lb) = _rows(conv_post_bwd, "conv_post_bwd", T, S, 256,
                                               [(uc, D, 0), (pm, D, 2), (duo, D, 0)], [], g3,
                                               [(D, F32), (D, BF16)], [], [(1, D)] * 3)
    dval, dgt, d_ckp = _conv_bwd(pm, duc, ckp, T, S)

    def rwkv_post_bwd(r, b, g):
        _, vjp = jax.vjp(lambda *z: _s5(*z, g[3]), *r[:5], *g[:3])
        res = vjp(r[5])
        return list(res[:5]), [], list(res[5:8])

    (do, dr_b, dk_b, dv_b, drog), _, (d_gg, d_gb, d_rk) = _rows(
        rwkv_post_bwd, "rwkv_post_bwd", T, S, 128, rows5 + [(do2, D, 0)], [], g5,
        [(D, F32)] * 4 + [(D, BF16)], [], [(1, D)] * 3)
    dsc = _scan_bwd(sc_in, hs, do, cm, nb, S)

    def rwkv_pre_bwd(r, b, g):
        _, vjp = jax.vjp(lambda *z: _s4(*z, g[10]), *shifted4(r), *g[:10])
        ct = (r[8] + r[14], r[9], r[10] + r[15], r[11] + r[16], r[12], r[13])
        res = vjp(ct)
        return list(res[:8]), [], list(res[8:18])

    rows4b = rows4 + [(a, D, 0) for a in dsc] + [(dr_b, D, 0), (dk_b, D, 0), (dv_b, D, 0)]
    gshapes = [(1, D), (1, D), (1, D), (1, LORA), (1, D), (LORA, D), (1, D), (LORA, D), (1, D), (1, D)]
    dts, _, gts = _rows(rwkv_pre_bwd, "rwkv_pre_bwd", T, S, 128, rows4b, [], g4,
                        [(D, F32)] * 3 + [(LORA, F32)] + [(D, F32)] * 3 + [(LORA, F32)], [], gshapes)
    dr0, dk0, dv0, dl0, dpr, dpk, dpv, dpl = dts
    d_mu_r, d_mu_k, d_mu_v, d_mu_l, d_w0, d_w2p, d_a0, d_a2p, d_kk, d_ka = gts

    def assemble(r, b, g):
        sh = [_with_next(r[10 + i], r[14 + i], tps4) for i in range(4)]
        main = jnp.concatenate([r[0], r[1], r[2], r[3] + sh[0], r[4] + sh[1], r[5] + sh[2], r[6], r[7], r[8]], axis=1)
        return [main, r[9] + sh[3]], [], []

    rows_a = [(dval, D, 0), (dgt, D, 0), (dcog, D, 0), (dr0, D, 0), (dk0, D, 0), (dv0, D, 0), (drog, D, 0), (dgc, D, 0),
              (dgr, D, 0), (dl0, LORA, 0), (dpr, D, 0), (dpk, D, 0), (dpv, D, 0), (dpl, LORA, 0),
              (dpr, D, 0, "next"), (dpk, D, 0, "next"), (dpv, D, 0, "next"), (dpl, LORA, 0, "next")]
    (dpm, dplo), _, _ = _rows(assemble, "assemble_dp", T, S, 128, rows_a, [], [], [(DMAIN, BF16), (LORA, BF16)], [], [])
    d_wmain = _matmul(h, dpm, "tn", "d_w_main", 512, 1024, 512)
    d_wlora = _matmul(h, dplo, "tn", "d_w_lora", 512, LORA, 512)
    dh_m = _matmul(dpm, wmain, "nt", "d_h_main", 512, 1024, 1024)
    dh_l = _matmul(dplo, wlora, "nt", "d_h_lora", 512, 1024, LORA)

    def pre_bwd(r, b, g):
        _, vjp = jax.vjp(_s1, r[0], g[0], b[0], b[1])
        dx, dg, dscale, dshift = vjp(r[1] + r[2])
        return [dx + r[3]], [dscale, dshift], [dg]

    (gx,), (dscale, dshift), (d_ng,) = _rows(pre_bwd, "pre_bwd", T, S, 256,
                                             [(x2d, D, 0), (dh_m, D, 0), (dh_l, D, 0), (dx_res, D, 0)],
                                             [scale, shift], [small["norm_g"]], [(D, F32)], [D, D], [(1, D)])
    dmod = jnp.concatenate([dshift, dscale, dgate], axis=-1).reshape(nb, 3 * D)
    d_small = {"norm_g": d_ng, "conv_b": d_cb, "conv_ln_g": d_lg, "conv_ln_b": d_lb,
               "rwkv_mu": jnp.concatenate([d_mu_r, d_mu_k, d_mu_v, d_mu_l], axis=1),
               "rwkv_w0": d_w0, "rwkv_a0": d_a0, "rwkv_k_k": d_kk, "rwkv_k_a": d_ka, "rwkv_r_k": d_rk,
               "rwkv_gn_g": d_gg, "rwkv_gn_b": d_gb, "final_g": d_final_g}
    big = {"wmain": d_wmain, "wlora": d_wlora, "wco": d_wco, "wro": d_wro, "wo": d_wo,
           "ck": d_ckp[:CW], "w2": d_w2p[:64], "a2": d_a2p[64:]}
    return loss_v[0, 0], gx, dmod, big, d_small


def _step(a):
    nb, S, _ = a["x"].shape
    T = nb * S
    x_i, y_i, c_i = _place()
    chip = 2 * x_i + y_i
    w_in = a["w_in"][0]
    WS = w_in.shape[1]
    small_w = {n: a[n].reshape(1, sz) for n, sz in SMALL}

    def halves(t):
        return t.reshape(2, t.shape[0] // 2, t.shape[1])

    g_ins = [a["c"]] + [halves(a[n][0].astype(BF16)) for n in ("w_in", "w_conv_out", "w_rwkv_out", "w_out")]
    g_ins += [a["conv_k"][0], a["rwkv_w2"][0], a["rwkv_a2"][0]]
    g_out = [jax.ShapeDtypeStruct((NDEV,) + g_ins[0].shape, F32)]
    g_out += [jax.ShapeDtypeStruct((NCHIP,) + t.shape, t.dtype) for t in g_ins[1:]]
    c_all, win_g, wco_g, wro_g, wo_g, ck_g, w2_g, a2_g = _comm_call(
        "gather_weights", g_ins, g_out, functools.partial(_gather_plan, 4), 7 + 3 * 7, 3 * 4)
    c_all = c_all.reshape(NDEV * nb, D)
    win_g, wco_g, wro_g, wo_g = (lax.dynamic_update_index_in_dim(g, own, chip, 0)
                                 for g, own in zip((win_g, wco_g, wro_g, wo_g), g_ins[1:5]))
    win_g = win_g.reshape(NCHIP, D, WS)
    win_full = jnp.concatenate([win_g[j] for j in range(NCHIP)], axis=1)
    wmain = jnp.concatenate([win_full[:, :6 * D], win_full[:, 6 * D + LORA:]], axis=1)
    wlora = win_full[:, 6 * D:6 * D + LORA]
    wco, wro, wo = (t.reshape(D, D) for t in (wco_g, wro_g, wo_g))
    ck = jnp.concatenate([ck_g[j] for j in range(NCHIP)], axis=1)
    w2 = jnp.concatenate([w2_g[j] for j in range(NCHIP)], axis=1)
    a2 = jnp.concatenate([a2_g[j] for j in range(NCHIP)], axis=1)

    ada_w = a["ada_w"][0]
    MW = ada_w.shape[1]
    ada_b_loc = lax.dynamic_slice(a["ada_b"], (0, chip * MW), (1, MW))

    def mod_body(c_ref, w_ref, b_ref, o_ref):
        o_ref[...] = _dot(_silu(c_ref[...]), w_ref[...], HI) + b_ref[...]

    modp = pl.pallas_call(mod_body, name="ada_mod", out_shape=jax.ShapeDtypeStruct((NDEV * nb, MW), F32),
                          compiler_params=_cparams())(c_all, ada_w, ada_b_loc)
    (mod_g,) = _comm_call("scatter_mod", [modp.reshape(NDEV, nb, MW)],
                          [jax.ShapeDtypeStruct((NDEV, nb, MW), F32)],
                          functools.partial(_scatter_plan, 1), 7)
    mod = mod_g.reshape(NCHIP, 2, nb, MW)
    mod = mod[:, 0].transpose(1, 0, 2).reshape(nb, NCHIP * MW)

    loss_p, gx, dmod, big, d_small = _local_step(
        a["x"].reshape(T, D), a["loss_target"].reshape(T, D), mod, wmain, wlora, wco, wro, wo, ck, w2, a2,
        small_w, nb, S)
    loss = lax.psum(loss_p, ("x", "y", "c"))

    d_small["ada_b"] = _colsum(dmod)
    small_vec = jnp.concatenate([d_small[n] for n, _ in SMALL], axis=1)
    dmod_s = dmod.reshape(nb, NCHIP, MW).transpose(1, 0, 2)
    dmod_s = jnp.repeat(dmod_s, 2, axis=0)
    small_s = jnp.broadcast_to(small_vec[None], (NDEV, 1, NSMALL))
    RW = D // NCHIP
    d_win = jnp.concatenate([big["wmain"][:, :6 * D], big["wlora"], big["wmain"][:, 6 * D:]], axis=1)
    hv = [d_win.reshape(2, D // 2, NCHIP * WS)]
    hv += [big[n].reshape(NCHIP, 2, RW // 2, D).transpose(1, 0, 2, 3).reshape(2, NCHIP * RW // 2, D)
           for n in ("wco", "wro", "wo")]
    hv += [big[n].reshape(-1, NCHIP, 2, RW // 2).transpose(2, 1, 0, 3).reshape(2, -1, RW // 2) for n in ("ck", "w2", "a2")]
    got_h = _comm_call("halve_grads", hv, [jax.ShapeDtypeStruct(t.shape[1:], F32) for t in hv], _halve_plan, len(hv))

    def own_half_plus(both, q):
        return [jnp.where(lax.axis_index("c") == 0, both[0], both[1]) + q]

    chip_part = [_ew(own_half_plus, "chip_sum_%d" % i, [hv[i], got_h[i]], 1, 128, BF16)[0] for i in range(len(hv))]
    sh_s = [jnp.stack([chip_part[0][:, j * WS:(j + 1) * WS] for j in range(NCHIP)])]
    sh_s += [t.reshape(NCHIP, RW // 2, D) for t in chip_part[1:4]]
    sh_s += [t.reshape(NCHIP, -1, RW // 2) for t in chip_part[4:]]
    s_ins = [dmod_s, small_s] + sh_s
    s_out = [jax.ShapeDtypeStruct(t.shape, t.dtype) for t in s_ins]
    got = _comm_call("scatter_grads", s_ins, s_out, functools.partial(_scatter_plan, 2), 14 + 3 * len(sh_s))
    dmod_all, small_all = got[0].reshape(NDEV * nb, MW), got[1].reshape(NDEV, NSMALL)
    def shard_sum(recv, sent):
        chip_i = 2 * lax.axis_index("x") + lax.axis_index("y")
        s = None
        for j in range(NCHIP):
            t = jnp.where(chip_i == j, sent[j], recv[j]).astype(F32)
            s = t if s is None else s + t
        return [s]

    fin = [_ew(shard_sum, "shard_sum_%d" % i, [t, sh_s[i]], 1, 128)[0] for i, t in enumerate(got[2:])]
    oth = _comm_call("join_halves", fin, [jax.ShapeDtypeStruct(t.shape, F32) for t in fin], _join_plan, len(fin))

    outs = {}

    def upd_halves(name, mine, other):
        shp = a[name].shape
        R, W = 2 * mine.shape[0], mine.shape[1]
        tm = 128
        nh = R // 2 // tm

        def body(w_ref, m_ref, v_ref, f_ref, o_ref, g_ref, d_ref, m2_ref, v2_ref):
            g = jnp.where(pl.program_id(0) // nh == lax.axis_index("c"), f_ref[...], o_ref[...])
            g_ref[...] = g
            d_ref[...], m2_ref[...], v2_ref[...] = _adamw(w_ref[...], g, m_ref[...], v_ref[...])

        full = pl.BlockSpec((tm, W), lambda i: (i, 0))
        half = pl.BlockSpec((tm, W), lambda i: (i % nh, 0))
        res = pl.pallas_call(
            body, name="adamw_" + name, grid=(R // tm,), in_specs=[full] * 3 + [half] * 2, out_specs=[full] * 4,
            out_shape=[jax.ShapeDtypeStruct((R, W), F32)] * 4, compiler_params=_cparams(("parallel",)),
        )(*[a[p + name].reshape(R, W) for p in ("", "m_", "v_")], mine, other)
        outs[name] = [r.reshape(shp) for r in res]

    for name, f, o in zip(("w_in", "w_conv_out", "w_rwkv_out", "w_out"), fin, oth):
        upd_halves(name, f, o)

    def upd(name, g):
        shp = a[name].shape
        ins = [a[p + name].reshape(g.shape) for p in ("", "m_", "v_")]
        res = _ew(lambda w_, m_, v_, g_: [g_, *_adamw(w_, g_, m_, v_)], "adamw_" + name, [*ins, g], 4, 128)
        outs[name] = [r.reshape(shp) for r in res]

    for name, f, o in zip(("conv_k", "rwkv_w2", "rwkv_a2"), fin[4:], oth[4:]):
        both = jnp.where(c_i == 0, jnp.stack([f, o]), jnp.stack([o, f]))
        upd(name, both.transpose(1, 0, 2).reshape(-1, RW))

    def adaw_body(c_ref, dm_ref, w_ref, m_ref, v_ref, g_ref, d_ref, m2_ref, v2_ref):
        g = _dot_tn(_silu(c_ref[...]), dm_ref[...], HI)
        g_ref[...] = g
        d_ref[...], m2_ref[...], v2_ref[...] = _adamw(w_ref[...], g, m_ref[...], v_ref[...])

    res = pl.pallas_call(adaw_body, name="adamw_ada_w", out_shape=[jax.ShapeDtypeStruct((D, MW), F32)] * 4,
                         compiler_params=_cparams())(c_all, dmod_all, ada_w, a["m_ada_w"][0], a["v_ada_w"][0])
    outs["ada_w"] = [r.reshape(a["ada_w"].shape) for r in res]

    wv, mv, vv = (jnp.concatenate([a[p + n].reshape(1, sz) for n, sz in SMALL], axis=1) for p in ("", "m_", "v_"))
    def small_fn(w_, m_, v_, gs):
        g = _sum_slots(gs)
        return [g, *_adamw(w_, g, m_, v_)]

    res = _ew(small_fn, "adamw_small", [wv, mv, vv, small_all.reshape(NDEV, 1, NSMALL)], 4, 8)
    off = 0
    for n, sz in SMALL:
        outs[n] = [r[:, off:off + sz].reshape(a[n].shape) for r in res]
        off += sz

    return (loss, gx.reshape(nb, S, D), *[outs[n][0] for n in WEIGHTS], *[outs[n][1] for n in WEIGHTS],
            *[outs[n][2] for n in WEIGHTS], *[outs[n][3] for n in WEIGHTS])


def _colsum(dmod):
    def body(d_ref, o_ref):
        o_ref[...] = jnp.sum(d_ref[...], axis=0, keepdims=True)
    return pl.pallas_call(body, name="ada_b_rowsum", out_shape=jax.ShapeDtypeStruct((1, dmod.shape[1]), F32),
                          compiler_params=_cparams())(dmod)


def kernel(x, c, ada_w, ada_b, norm_g, w_in, conv_k, conv_b, conv_ln_g, conv_ln_b, w_conv_out, rwkv_mu, rwkv_w0, rwkv_w2, rwkv_a0, rwkv_a2, rwkv_k_k, rwkv_k_a, rwkv_r_k, rwkv_gn_g, rwkv_gn_b, w_rwkv_out, w_out, final_g, loss_target, m_ada_w, m_ada_b, m_norm_g, m_w_in, m_conv_k, m_conv_b, m_conv_ln_g, m_conv_ln_b, m_w_conv_out, m_rwkv_mu, m_rwkv_w0, m_rwkv_w2, m_rwkv_a0, m_rwkv_a2, m_rwkv_k_k, m_rwkv_k_a, m_rwkv_r_k, m_rwkv_gn_g, m_rwkv_gn_b, m_w_rwkv_out, m_w_out, m_final_g, v_ada_w, v_ada_b, v_norm_g, v_w_in, v_conv_k, v_conv_b, v_conv_ln_g, v_conv_ln_b, v_w_conv_out, v_rwkv_mu, v_rwkv_w0, v_rwkv_w2, v_rwkv_a0, v_rwkv_a2, v_rwkv_k_k, v_rwkv_k_a, v_rwkv_r_k, v_rwkv_gn_g, v_rwkv_gn_b, v_w_rwkv_out, v_w_out, v_final_g):
    return _step(dict(locals()))
```

```python
import functools

import numpy as np
import jax
import jax.numpy as jnp
from jax import lax
from jax.experimental import pallas as pl
from jax.experimental.pallas import tpu as pltpu

F32 = jnp.float32
BF16 = jnp.bfloat16
HI = lax.Precision.HIGHEST
MESH = pl.DeviceIdType.MESH
ANY = pl.BlockSpec(memory_space=pl.ANY)

D = 1024
NH = 16
HN = 64
LORA = 128
DMAIN = 9 * D
CH = 64
CW = 31
NCHIP = 4
NDEV = 8
VMEM_LIMIT = 56 * 1024 * 1024

RMS_EPS = 1e-6
LN_EPS = 1e-5
GN_EPS = 64e-5
L2_EPS = 1e-12
ADAM_LR = 0.001
ADAM_B1 = 0.9
ADAM_B2 = 0.999
ADAM_EPS = 1e-08
ADAM_WD = 0.01
ADAM_STEP = 10

SMALL = (("ada_b", 3072), ("norm_g", 1024), ("conv_b", 1024), ("conv_ln_g", 1024), ("conv_ln_b", 1024),
         ("rwkv_mu", 3200), ("rwkv_w0", 1024), ("rwkv_a0", 1024), ("rwkv_k_k", 1024), ("rwkv_k_a", 1024),
         ("rwkv_r_k", 1024), ("rwkv_gn_g", 1024), ("rwkv_gn_b", 1024), ("final_g", 1024))
NSMALL = sum(n for _, n in SMALL)

WEIGHTS = ['ada_w', 'ada_b', 'norm_g', 'w_in', 'conv_k', 'conv_b', 'conv_ln_g', 'conv_ln_b', 'w_conv_out', 'rwkv_mu',
           'rwkv_w0', 'rwkv_w2', 'rwkv_a0', 'rwkv_a2', 'rwkv_k_k', 'rwkv_k_a', 'rwkv_r_k', 'rwkv_gn_g', 'rwkv_gn_b',
           'w_rwkv_out', 'w_out', 'final_g']


def _cparams(sem=None, **kw):
    if sem is not None:
        kw["dimension_semantics"] = sem
    return pltpu.CompilerParams(vmem_limit_bytes=VMEM_LIMIT, **kw)


def _dot(a, b, prec=None):
    return jnp.dot(a, b, preferred_element_type=F32, precision=prec)


def _dot_nt(a, b, prec=None):
    return lax.dot_general(a, b, (((1,), (1,)), ((), ())), preferred_element_type=F32, precision=prec)


def _dot_tn(a, b, prec=None):
    return lax.dot_general(a, b, (((0,), (0,)), ((), ())), preferred_element_type=F32, precision=prec)


def _pdot(f, a, b, p):
    if p == "hi":
        return f(a, b, HI)
    ah, bh = a.astype(BF16), b.astype(BF16)
    if p == "bf":
        return f(ah, bh)
    al, bl = (a - ah.astype(F32)).astype(BF16), (b - bh.astype(F32)).astype(BF16)
    return f(ah, bh) + (f(ah, bl) + f(al, bh))


P_SCORE = "b3"
P_INV = "bf"
P_APPLY = "bf"


def _sigmoid(z):
    return 1.0 / (1.0 + jnp.exp(-z))


def _silu(z):
    return z * _sigmoid(z)


def _matmul(a, b, mode, name, tm, tn, tk):
    if mode == "nn":
        (M, K), N = a.shape, b.shape[1]
        a_spec = pl.BlockSpec((tm, tk), lambda j, i, k: (i, k))
        b_spec = pl.BlockSpec((tk, tn), lambda j, i, k: (k, j))
        f = _dot
    elif mode == "nt":
        (M, K), N = a.shape, b.shape[0]
        a_spec = pl.BlockSpec((tm, tk), lambda j, i, k: (i, k))
        b_spec = pl.BlockSpec((tn, tk), lambda j, i, k: (j, k))
        f = _dot_nt
    else:
        (K, M), N = a.shape, b.shape[1]
        a_spec = pl.BlockSpec((tk, tm), lambda j, i, k: (k, i))
        b_spec = pl.BlockSpec((tk, tn), lambda j, i, k: (k, j))
        f = _dot_tn
    assert M % tm == 0 and N % tn == 0 and K % tk == 0, (name, M, N, K)

    def body(a_ref, b_ref, o_ref):
        @pl.when(pl.program_id(2) == 0)
        def _():
            o_ref[...] = jnp.zeros_like(o_ref)
        o_ref[...] += f(a_ref[...], b_ref[...])

    return pl.pallas_call(
        body, name=name, grid=(N // tn, M // tm, K // tk),
        in_specs=[a_spec, b_spec],
        out_specs=pl.BlockSpec((tm, tn), lambda j, i, k: (i, j)),
        out_shape=jax.ShapeDtypeStruct((M, N), F32),
        compiler_params=_cparams(("parallel", "parallel", "arbitrary")),
    )(a, b)


def _rows(fn, name, T, S, tm, rows, bpars, gpars, outs, baccs, gaccs):
    nb = T // S
    tps = S // tm
    n_r, n_b, n_g, n_o, n_ba, n_ga = len(rows), len(bpars), len(gpars), len(outs), len(baccs), len(gaccs)

    def body(*refs):
        r_refs = refs[:n_r]
        b_refs = refs[n_r:n_r + n_b]
        g_refs = refs[n_r + n_b:n_r + n_b + n_g]
        o_refs = refs[n_r + n_b + n_g:n_r + n_b + n_g + n_o]
        ba_refs = refs[n_r + n_b + n_g + n_o:n_r + n_b + n_g + n_o + n_ba]
        ga_refs = refs[n_r + n_b + n_g + n_o + n_ba:]
        i = pl.program_id(0)
        o_vals, ba_vals, ga_vals = fn([r[...] for r in r_refs], [r[...] for r in b_refs], [r[...] for r in g_refs])
        for r, v in zip(o_refs, o_vals):
            r[...] = v.astype(r.dtype)
        if n_ba:
            @pl.when(i % tps == 0)
            def _():
                for r in ba_refs:
                    r[...] = jnp.zeros_like(r)
            for r, v in zip(ba_refs, ba_vals):
                r[...] += v.reshape(r.shape)
        if n_ga:
            @pl.when(i == 0)
            def _():
                for r in ga_refs:
                    r[...] = jnp.zeros_like(r)
            for r, v in zip(ga_refs, ga_vals):
                r[...] += v.reshape(r.shape)

    def row_spec(w, cb, kind="tile"):
        if kind == "prev":
            return pl.BlockSpec((8, w), lambda i: (jnp.maximum(i * (tm // 8) - 1, 0), cb))
        if kind == "next":
            return pl.BlockSpec((8, w), lambda i: (jnp.minimum((i + 1) * (tm // 8), T // 8 - 1), cb))
        return pl.BlockSpec((tm, w), lambda i: (i, cb))

    in_specs = [row_spec(*r[1:]) for r in rows]
    in_specs += [pl.BlockSpec((None, 1, p.shape[-1]), lambda i: (i // tps, 0, 0)) for p in bpars]
    in_specs += [pl.BlockSpec(p.shape, lambda i: (0, 0)) for p in gpars]
    out_specs = [pl.BlockSpec((tm, w), lambda i: (i, 0)) for w, _ in outs]
    out_specs += [pl.BlockSpec((None, 1, w), lambda i: (i // tps, 0, 0)) for w in baccs]
    out_specs += [pl.BlockSpec(s, lambda i: (0, 0)) for s in gaccs]
    out_shape = [jax.ShapeDtypeStruct((T, w), dt) for w, dt in outs]
    out_shape += [jax.ShapeDtypeStruct((nb, 1, w), F32) for w in baccs]
    out_shape += [jax.ShapeDtypeStruct(s, F32) for s in gaccs]
    res = pl.pallas_call(
        body, name=name, grid=(T // tm,), in_specs=in_specs, out_specs=out_specs, out_shape=out_shape,
        compiler_params=_cparams(("arbitrary",)),
    )(*[r[0] for r in rows], *bpars, *gpars)
    return res[:n_o], res[n_o:n_o + n_ba], res[n_o + n_ba:]


@jax.custom_vjp
def _gsum(z, G):
    zh = z.astype(BF16)
    zl = (z - zh.astype(F32)).astype(BF16)
    return _dot(zh, G) + _dot(zl, G)


_gsum.defvjp(lambda z, G: (_gsum(z, G), G), lambda G, ct: (_gsum(ct, G), jnp.zeros_like(G)))


def _s1(x, g, scale, shift):
    y = x * lax.rsqrt(jnp.mean(x * x, axis=-1, keepdims=True) + RMS_EPS)
    return (y * g) * (1.0 + scale) + shift


def _s3(uc, og, cb, lg, lb):
    u = uc + cb
    mu = jnp.mean(u, axis=-1, keepdims=True)
    d = u - mu
    var = jnp.mean(d * d, axis=-1, keepdims=True)
    y = d * lax.rsqrt(var + LN_EPS) * lg + lb
    return _silu(y) * _silu(og)


def _s4(r0, k0, v0, l0, pr, pk, pv, plo, mu_r, mu_k, mu_v, mu_l, w0, w2p, a0, a2p, k_k, k_a, G):
    r = r0 + mu_r * (pr - r0)
    k = k0 + mu_k * (pk - k0)
    v = v0 + mu_v * (pv - v0)
    lo = l0 + mu_l * (plo - l0)
    w_pre = w0 + _dot(jnp.tanh(lo), w2p, HI)
    lw = -np.float32(np.exp(-0.5)) * _sigmoid(w_pre)
    a = _sigmoid(a0 + _dot(lo, a2p, HI))
    kkr = k * k_k
    ss = _gsum(kkr * kkr, G)
    kk = kkr / jnp.maximum(jnp.sqrt(ss), L2_EPS)
    k2 = k * (1.0 + (a - 1.0) * k_a)
    return r, lw, k2, v, kk, kk * a


def _s5(o, r, k2, v, og, gg, gb, rk, G):
    mu = _gsum(o, G) * (1.0 / HN)
    d = o - mu
    var = _gsum(d * d, G) * (1.0 / HN)
    y = d * lax.rsqrt(var + GN_EPS) * gg + gb
    bonus = _gsum(r * k2 * rk, G)
    return (y + bonus * v) * _silu(og)


def _s6(yc, yr, gc, gr):
    return _sigmoid(gc) * yc + _sigmoid(gr) * yr


def _s7(x, out, tgt, gate, fg):
    x2 = x + gate * out
    y = x2 * lax.rsqrt(jnp.mean(x2 * x2, axis=-1, keepdims=True) + RMS_EPS) * fg
    e = y - tgt
    return 0.5 * jnp.sum(jnp.mean(e * e, axis=-1))


def _chunk(sts, r, lw, k, v, kk, b, cm):
    cum = _dot(cm[0], lw, HI)
    ein = jnp.exp(-cum)
    rt = r * jnp.exp(cum)
    kkt = kk * jnp.exp(cum - lw)
    kh = k * ein
    bh = b * ein
    ec = jnp.exp(jnp.sum(lw, axis=0, keepdims=True))
    khe = kh * ec
    bhe = bh * ec
    H = range(len(sts))
    tri, strict, eye = cm[0], cm[1], cm[2]
    rt, kkt, kh, bh, v, khe, bhe, ec = ([a[:, j * HN:(j + 1) * HN] for j in H] for a in (rt, kkt, kh, bh, v, khe, bhe, ec))
    lhs = [jnp.concatenate([kkt[j], rt[j]], axis=0) for j in H]
    rhs_s = [jnp.concatenate([bh[j], kh[j]], axis=0) for j in H]
    lh = [a.astype(BF16).astype(F32) for a in lhs]
    rh = [a.astype(BF16).astype(F32) for a in rhs_s]
    lc = [jnp.concatenate([lh[j], lh[j], lhs[j] - lh[j]], axis=1).astype(BF16) for j in H]
    rc = [jnp.concatenate([rh[j], rhs_s[j] - rh[j], rh[j]], axis=1).astype(BF16) for j in H]
    sc = [_dot_nt(lc[j], rc[j]) for j in H]
    a_kb = [strict * sc[j][:CH, :CH] for j in H]
    a_kk = [strict * sc[j][:CH, CH:] for j in H]
    a_rb = [tri * sc[j][CH:, :CH] for j in H]
    a_rk = [tri * sc[j][CH:, CH:] for j in H]
    ps = [_dot_nt(lhs[j].astype(BF16), sts[j].astype(BF16)) for j in H]
    pv = [_dot(jnp.concatenate([a_kk[j], a_rk[j]], axis=0).astype(BF16), v[j].astype(BF16)) for j in H]
    rhs = [ps[j][:CH] + pv[j][:CH] for j in H]
    o0 = [ps[j][CH:] + pv[j][CH:] for j in H]
    xi = [eye - cm[3] * a_kb[j] for j in H]
    for lvl in range(1, 6):
        t = [_pdot(_dot, xi[j], cm[3 + lvl] * a_kb[j], P_INV) for j in H]
        xi = [xi[j] - _pdot(_dot, t[j], xi[j], P_INV) for j in H]
    u = [_pdot(_dot, xi[j], rhs[j], P_APPLY) for j in H]
    o = [o0[j] - _pdot(_dot, a_rb[j], u[j], P_APPLY) for j in H]
    st2 = [sts[j] * ec[j] + _dot_tn(jnp.concatenate([v[j], u[j]], axis=0).astype(BF16),
                                    jnp.concatenate([khe[j], -bhe[j]], axis=0).astype(BF16)) for j in H]
    return jnp.concatenate(o, axis=1), tuple(st2)


def _chunk_consts():
    t = np.arange(CH)[:, None]
    s = np.arange(CH)[None, :]
    mats = [(t >= s), (t > s), (t == s)]
    for lvl in range(6):
        sz = 1 << lvl
        mats.append(((t // sz) % 2 == 1) & ((s // sz) == (t // sz) - 1))
    mats.append(np.zeros((CH, CH), bool))
    return np.stack(mats).astype(np.float32)


def _adamw(w, g, m, v):
    m = ADAM_B1 * m + (1.0 - ADAM_B1) * g
    v = ADAM_B2 * v + (1.0 - ADAM_B2) * (g * g)
    m_hat = m / (1.0 - ADAM_B1 ** ADAM_STEP)
    v_hat = v / (1.0 - ADAM_B2 ** ADAM_STEP)
    delta = -ADAM_LR * (m_hat / (jnp.sqrt(v_hat) + ADAM_EPS) + ADAM_WD * w)
    return delta, m, v


CT = 128
RB = 64
WIN = RB + 32


def _conv_fwd(pm, ck, T, S):
    nb = T // S

    def body(val_ref, gate_ref, ck_ref, out_ref, ubuf):
        ubuf[0:32, :] = jnp.zeros((32, CT), F32)
        ubuf[32:, :] = val_ref[...] * _sigmoid(gate_ref[...])

        def blk(rb, carry):
            base = pl.multiple_of(rb * RB, RB)
            win = ubuf[pl.ds(base, WIN), :]
            acc = jnp.zeros((RB, CT), F32)
            for j in range(CW):
                acc = acc + ck_ref[j:j + 1, :] * pltpu.roll(win, (WIN - (2 + j)) % WIN, 0)[0:RB, :]
            out_ref[pl.ds(base, RB), :] = acc
            return carry

        lax.fori_loop(0, S // RB, blk, 0)

    return pl.pallas_call(
        body, name="conv_fwd", grid=(D // CT, nb),
        in_specs=[pl.BlockSpec((S, CT), lambda ct, b: (b, ct)),
                  pl.BlockSpec((S, CT), lambda ct, b: (b, D // CT + ct)),
                  pl.BlockSpec((32, CT), lambda ct, b: (0, ct))],
        out_specs=pl.BlockSpec((S, CT), lambda ct, b: (b, ct)),
        out_shape=jax.ShapeDtypeStruct((T, D), F32),
        scratch_shapes=[pltpu.VMEM((S + 32, CT), F32)],
        compiler_params=_cparams(("parallel", "arbitrary")),
    )(pm, pm, ck)


def _conv_bwd(pm, duc, ck, T, S):
    nb = T // S

    def body(val_ref, gate_ref, duc_ref, ck_ref, dval_ref, dgate_ref, dck_ref, ubuf, dbuf, acc):
        b = pl.program_id(1)
        ubuf[0:32, :] = jnp.zeros((32, CT), F32)
        ubuf[32:, :] = val_ref[...] * _sigmoid(gate_ref[...])
        dbuf[0:S, :] = duc_ref[...]
        dbuf[S:, :] = jnp.zeros((32, CT), F32)
        acc[...] = jnp.zeros_like(acc)

        def blk(rb, carry):
            base = pl.multiple_of(rb * RB, RB)
            uwin = ubuf[pl.ds(base, WIN), :]
            dwin = dbuf[pl.ds(base, WIN), :]
            dblk = dwin[0:RB, :]
            du = jnp.zeros((RB, CT), F32)
            for j in range(CW):
                du = du + ck_ref[j:j + 1, :] * pltpu.roll(dwin, (WIN - (CW - 1 - j)) % WIN, 0)[0:RB, :]
                ush = pltpu.roll(uwin, (WIN - (2 + j)) % WIN, 0)[0:RB, :]
                acc[j] += jnp.sum((dblk * ush).reshape(RB // 8, 8, CT), axis=0)
            val = val_ref[pl.ds(base, RB), :]
            sg = _sigmoid(gate_ref[pl.ds(base, RB), :])
            dval_ref[pl.ds(base, RB), :] = du * sg
            dgate_ref[pl.ds(base, RB), :] = du * val * sg * (1.0 - sg)
            return carry

        lax.fori_loop(0, S // RB, blk, 0)

        @pl.when(b == 0)
        def _():
            dck_ref[...] = jnp.zeros_like(dck_ref)
        for j in range(CW):
            dck_ref[j:j + 1, :] += jnp.sum(acc[j], axis=0, keepdims=True)

    return pl.pallas_call(
        body, name="conv_bwd", grid=(D // CT, nb),
        in_specs=[pl.BlockSpec((S, CT), lambda ct, b: (b, ct)),
                  pl.BlockSpec((S, CT), lambda ct, b: (b, D // CT + ct)),
                  pl.BlockSpec((S, CT), lambda ct, b: (b, ct)),
                  pl.BlockSpec((32, CT), lambda ct, b: (0, ct))],
        out_specs=[pl.BlockSpec((S, CT), lambda ct, b: (b, ct)),
                   pl.BlockSpec((S, CT), lambda ct, b: (b, ct)),
                   pl.BlockSpec((32, CT), lambda ct, b: (0, ct))],
        out_shape=[jax.ShapeDtypeStruct((T, D), F32), jax.ShapeDtypeStruct((T, D), F32),
                   jax.ShapeDtypeStruct((32, D), F32)],
        scratch_shapes=[pltpu.VMEM((S + 32, CT), F32), pltpu.VMEM((S + 32, CT), F32), pltpu.VMEM((32, 8, CT), F32)],
        compiler_params=_cparams(("parallel", "arbitrary")),
    )(pm, pm, duc, ck)


HB = 16


def _scan_fwd(ins, cm, nb, S):
    nc = S // CH
    blk = pl.BlockSpec((CH, HB * HN), lambda b, g, i: (b * nc + i, g))
    hblk = pl.BlockSpec((None, HB, None, HN, HN), lambda b, g, i: (b, g, i, 0, 0))

    def body(r_ref, lw_ref, k_ref, v_ref, kk_ref, b_ref, cm_ref, o_ref, hs_ref, st):
        @pl.when(pl.program_id(2) == 0)
        def _():
            st[...] = jnp.zeros_like(st)
        s0 = [st[j] for j in range(HB)]
        for j in range(HB):
            hs_ref[j] = s0[j]
        o, s1 = _chunk(s0, r_ref[...], lw_ref[...], k_ref[...], v_ref[...], kk_ref[...], b_ref[...], cm_ref[...])
        o_ref[...] = o
        for j in range(HB):
            st[j] = s1[j]

    return pl.pallas_call(
        body, name="scan_fwd", grid=(nb, NH // HB, nc),
        in_specs=[blk] * 6 + [pl.BlockSpec(cm.shape, lambda b, g, i: (0, 0, 0))],
        out_specs=[blk, hblk],
        out_shape=[jax.ShapeDtypeStruct((nb * S, D), F32), jax.ShapeDtypeStruct((nb, NH, nc, HN, HN), F32)],
        scratch_shapes=[pltpu.VMEM((HB, HN, HN), F32)],
        compiler_params=_cparams(("parallel", "parallel", "arbitrary")),
    )(*ins, cm)


def _scan_bwd(ins, hs, do, cm, nb, S):
    nc = S // CH
    blk = pl.BlockSpec((CH, HB * HN), lambda b, g, i: (b * nc + nc - 1 - i, g))
    hblk = pl.BlockSpec((None, HB, None, HN, HN), lambda b, g, i: (b, g, nc - 1 - i, 0, 0))

    def body(r_ref, lw_ref, k_ref, v_ref, kk_ref, b_ref, hs_ref, do_ref, cm_ref,
             dr_ref, dlw_ref, dk_ref, dv_ref, dkk_ref, db_ref, dst):
        @pl.when(pl.program_id(2) == 0)
        def _():
            dst[...] = jnp.zeros_like(dst)
        cmv = cm_ref[...]
        f = lambda s0, r, lw, k, v, kk, b: _chunk(s0, r, lw, k, v, kk, b, cmv)
        _, vjp = jax.vjp(f, [hs_ref[j] for j in range(HB)], r_ref[...], lw_ref[...], k_ref[...], v_ref[...],
                         kk_ref[...], b_ref[...])
        ds0, dr, dlw, dk, dv, dkk, db = vjp((do_ref[...], tuple(dst[j] for j in range(HB))))
        for j in range(HB):
            dst[j] = ds0[j]
        dr_ref[...] = dr
        dlw_ref[...] = dlw
        dk_ref[...] = dk
        dv_ref[...] = dv
        dkk_ref[...] = dkk
        db_ref[...] = db

    return pl.pallas_call(
        body, name="scan_bwd", grid=(nb, NH // HB, nc),
        in_specs=[blk] * 6 + [hblk, blk, pl.BlockSpec(cm.shape, lambda b, g, i: (0, 0, 0))],
        out_specs=[blk] * 6,
        out_shape=[jax.ShapeDtypeStruct((nb * S, D), F32)] * 6,
        scratch_shapes=[pltpu.VMEM((HB, HN, HN), F32)],
        compiler_params=_cparams(("parallel", "parallel", "arbitrary")),
    )(*ins, hs, do, cm)


def _ew(fn, name, ins, n_out, tm, out_dtype=F32):
    R, W = ins[0].shape[-2:]
    tm = min(tm, R)
    assert R % tm == 0

    def body(*refs):
        vals = fn(*[r[...] for r in refs[:len(ins)]])
        for r, v in zip(refs[len(ins):], vals):
            r[...] = v.astype(r.dtype)

    def spec(a):
        if a.ndim == 3:
            return pl.BlockSpec((a.shape[0], tm, W), lambda i: (0, i, 0))
        return pl.BlockSpec((tm, W), lambda i: (i, 0))

    return pl.pallas_call(
        body, name=name, grid=(R // tm,), in_specs=[spec(a) for a in ins],
        out_specs=[pl.BlockSpec((tm, W), lambda i: (i, 0))] * n_out,
        out_shape=[jax.ShapeDtypeStruct((R, W), out_dtype)] * n_out,
        compiler_params=_cparams(("parallel",)),
    )(*ins)


def _sum_slots(r):
    s = r[0]
    for j in range(1, r.shape[0]):
        s = s + r[j]
    return s


def _place():
    x, y, c = lax.axis_index("x"), lax.axis_index("y"), lax.axis_index("c")
    return x, y, c


def _flip(v, d):
    return 1 - v if d else v


CHIP_PEERS = ((1, 0), (0, 1), (1, 1))
DEV_PEERS = tuple((dx, dy, dc) for dx in (0, 1) for dy in (0, 1) for dc in (0, 1))[1:]


def _comm_call(name, ins, out_shapes, plan, n_rem, n_fwd=0):
    n_in = len(ins)

    def body(*refs):
        in_refs, out_refs = refs[:n_in], refs[n_in:n_in + len(out_shapes)]
        send_sems, recv_sems, loc_sems = refs[n_in + len(out_shapes):]
        loc, rem, *rest = plan(in_refs, out_refs, _place())
        fwd = rest[0] if rest else []
        assert len(rem) == n_rem and len(fwd) == n_fwd and len(loc) <= 2 * n_in, (name, len(loc), len(rem), len(fwd))

        def remote(i, s, d, peer):
            return pltpu.make_async_remote_copy(src_ref=s, dst_ref=d, send_sem=send_sems.at[i], recv_sem=recv_sems.at[i],
                                                device_id=peer, device_id_type=MESH)

        copies = [pltpu.make_async_copy(s, d, loc_sems.at[i]) for i, (s, d) in enumerate(loc)]
        rcopies = [remote(i, s, d, peer) for i, (s, d, peer) in enumerate(rem)]
        for cp in copies + rcopies:
            cp.start()
        landed = set()
        fcopies = []
        for i, (s, d, peer, k) in enumerate(fwd):
            if k not in landed:
                rcopies[k].wait_recv()
                landed.add(k)
            fcopies.append(remote(n_rem + i, s, d, peer))
            fcopies[-1].start()
        for k, cp in enumerate(rcopies):
            if k not in landed:
                cp.wait_recv()
        for cp in rcopies + fcopies:
            cp.wait_send()
        for cp in fcopies:
            cp.wait_recv()
        for cp in copies:
            cp.wait()

    return pl.pallas_call(
        body, name=name, in_specs=[ANY] * n_in, out_specs=[ANY] * len(out_shapes), out_shape=out_shapes,
        scratch_shapes=[pltpu.SemaphoreType.DMA((n_rem + n_fwd,)), pltpu.SemaphoreType.DMA((n_rem + n_fwd,)),
                        pltpu.SemaphoreType.DMA((2 * n_in,))],
        compiler_params=pltpu.CompilerParams(has_side_effects=True),
    )(*ins)


def _gather_plan(n_big, in_refs, out_refs, place):
    x, y, c = place
    chip, dev = 2 * x + y, 4 * x + 2 * y + c
    sib = (x, y, 1 - c)
    loc = [(in_refs[0], out_refs[0].at[dev])] + [(s, d.at[chip]) for s, d in zip(in_refs[1 + n_big:], out_refs[1 + n_big:])]
    rem = [(in_refs[0], out_refs[0].at[dev], (_flip(x, dx), _flip(y, dy), _flip(c, dc))) for dx, dy, dc in DEV_PEERS]
    fwd = []
    for s, d in zip(in_refs[1:1 + n_big], out_refs[1:1 + n_big]):
        for dx, dy in CHIP_PEERS:
            px, py = _flip(x, dx), _flip(y, dy)
            fwd.append((d.at[2 * px + py, c], d.at[2 * px + py, c], sib, len(rem)))
            rem.append((s.at[c], d.at[chip, c], (px, py, c)))
    for s, d in zip(in_refs[1 + n_big:], out_refs[1 + n_big:]):
        rem += [(s, d.at[chip], (_flip(x, dx), _flip(y, dy), c)) for dx, dy in CHIP_PEERS]
    return loc, rem, fwd


def _halve_plan(in_refs, out_refs, place):
    x, y, c = place
    return [], [(s.at[1 - c], d, (x, y, 1 - c)) for s, d in zip(in_refs, out_refs)]


def _join_plan(in_refs, out_refs, place):
    x, y, c = place
    return [], [(s, d, (x, y, 1 - c)) for s, d in zip(in_refs, out_refs)]


def _scatter_plan(n_all, in_refs, out_refs, place):
    x, y, c = place
    chip, dev = 2 * x + y, 4 * x + 2 * y + c
    loc, rem = [], []
    for s, d in zip(in_refs[:n_all], out_refs[:n_all]):
        loc.append((s.at[dev], d.at[dev]))
        for dx, dy, dc in DEV_PEERS:
            px, py, pc = _flip(x, dx), _flip(y, dy), _flip(c, dc)
            rem.append((s.at[4 * px + 2 * py + pc], d.at[dev], (px, py, pc)))
    for s, d in zip(in_refs[n_all:], out_refs[n_all:]):
        for dx, dy in CHIP_PEERS:
            px, py = _flip(x, dx), _flip(y, dy)
            rem.append((s.at[2 * px + py], d.at[chip], (px, py, c)))
    return loc, rem


def _bshape(a, nb):
    return a.reshape(nb, 1, a.shape[-1])


def _with_prev(cur, before, tiles_per_seq):
    first = pl.program_id(0) % tiles_per_seq == 0
    row0 = jnp.where(first, 0.0, before[7:8, :])
    rid = lax.broadcasted_iota(jnp.int32, cur.shape, 0)
    return jnp.where(rid == 0, row0, pltpu.roll(cur, 1, 0))


def _with_next(cur, after, tiles_per_seq):
    last = pl.program_id(0) % tiles_per_seq == tiles_per_seq - 1
    n = cur.shape[0]
    row_n = jnp.where(last, 0.0, after[0:1, :])
    rid = lax.broadcasted_iota(jnp.int32, cur.shape, 0)
    return jnp.where(rid == n - 1, row_n, pltpu.roll(cur, n - 1, 0))


def _local_step(x2d, tgt, mod, wmain, wlora, wco, wro, wo, ck, w2, a2, small, nb, S):
    T = nb * S
    shift, scale, gate = (_bshape(mod[:, i * D:(i + 1) * D], nb) for i in range(3))
    G = jnp.asarray(np.kron(np.eye(NH, dtype=np.float32), np.ones((HN, HN), np.float32)), dtype=BF16)
    cm = jnp.asarray(_chunk_consts())
    ckp = jnp.pad(ck, ((0, 1), (0, 0)))
    zpad = jnp.zeros((64, D), F32)
    w2p = jnp.concatenate([w2, zpad], axis=0)
    a2p = jnp.concatenate([zpad, a2], axis=0)
    mu = small["rwkv_mu"]
    mu_r, mu_k, mu_v, mu_l = mu[:, 0:D], mu[:, D:2 * D], mu[:, 2 * D:3 * D], mu[:, 3 * D:]
    g4 = [mu_r, mu_k, mu_v, mu_l, small["rwkv_w0"], w2p, small["rwkv_a0"], a2p, small["rwkv_k_k"], small["rwkv_k_a"], G]
    g5 = [small["rwkv_gn_g"], small["rwkv_gn_b"], small["rwkv_r_k"], G]
    g3 = [small["conv_b"], small["conv_ln_g"], small["conv_ln_b"]]

    (h,), _, _ = _rows(lambda r, b, g: ([_s1(r[0], g[0], b[0], b[1])], [], []), "pre_fwd", T, S, 256,
                       [(x2d, D, 0)], [scale, shift], [small["norm_g"]], [(D, BF16)], [], [])
    pm = _matmul(h, wmain, "nn", "proj_main", min(T, 1024), 1024, D)
    plo = _matmul(h, wlora, "nn", "proj_lora", 512, LORA, D)
    uc = _conv_fwd(pm, ckp, T, S)
    (uo,), _, _ = _rows(lambda r, b, g: ([_s3(r[0], r[1], *g)], [], []), "conv_post_fwd", T, S, 256,
                        [(uc, D, 0), (pm, D, 2)], [], g3, [(D, BF16)], [], [])
    yc = _matmul(uo, wco, "nn", "conv_out", 512, 1024, D)
    rows4 = [(pm, D, 3), (pm, D, 4), (pm, D, 5), (plo, LORA, 0),
             (pm, D, 3, "prev"), (pm, D, 4, "prev"), (pm, D, 5, "prev"), (plo, LORA, 0, "prev")]
    tps4 = S // 128

    def shifted4(r):
        return list(r[:4]) + [_with_prev(r[i], r[4 + i], tps4) for i in range(4)]

    sc_in, _, _ = _rows(lambda r, b, g: (list(_s4(*shifted4(r), *g)), [], []), "rwkv_pre_fwd", T, S, 128,
                        rows4, [], g4, [(D, F32)] * 6, [], [])
    o, hs = _scan_fwd(sc_in, cm, nb, S)
    rows5 = [(o, D, 0), (sc_in[0], D, 0), (sc_in[2], D, 0), (sc_in[3], D, 0), (pm, D, 6)]
    (o2,), _, _ = _rows(lambda r, b, g: ([_s5(*r, *g)], [], []), "rwkv_post_fwd", T, S, 128,
                        rows5, [], g5, [(D, BF16)], [], [])
    yr = _matmul(o2, wro, "nn", "rwkv_out", 512, 1024, D)
    rows6 = [(yc, D, 0), (yr, D, 0), (pm, D, 7), (pm, D, 8)]
    (m,), _, _ = _rows(lambda r, b, g: ([_s6(*r)], [], []), "merge_fwd", T, S, 256, rows6, [], [], [(D, BF16)], [], [])
    out = _matmul(m, wo, "nn", "out_proj", 512, 1024, D)

    def head(r, b, g):
        loss, (dx, dout, dgate, dfg) = jax.value_and_grad(_s7, argnums=(0, 1, 3, 4))(r[0], r[1], r[2], b[0], g[0])
        return [dx, dout], [dgate], [dfg, jnp.full((1, 128), loss, F32)]

    (dx_res, dout), (dgate,), (d_final_g, loss_v) = _rows(
        head, "head", T, S, 256, [(x2d, D, 0), (out, D, 0), (tgt, D, 0)], [gate], [small["final_g"]],
        [(D, F32), (D, BF16)], [D], [(1, D), (1, 128)])

    d_wo = _matmul(m, dout, "tn", "d_w_out", 512, 1024, min(T, 2048))
    dm = _matmul(dout, wo, "nt", "d_merge", 512, 1024, D)

    def merge_bwd(r, b, g):
        _, vjp = jax.vjp(_s6, *r[:4])
        dyc, dyr, dgc, dgr = vjp(r[4])
        return [dyc, dyr, dgc, dgr], [], []

    (dyc, dyr, dgc, dgr), _, _ = _rows(merge_bwd, "merge_bwd", T, S, 256, rows6 + [(dm, D, 0)], [], [],
                                       [(D, BF16), (D, BF16), (D, BF16), (D, BF16)], [], [])
    d_wco = _matmul(uo, dyc, "tn", "d_w_conv_out", 512, 1024, min(T, 2048))
    d_wro = _matmul(o2, dyr, "tn", "d_w_rwkv_out", 512, 1024, min(T, 2048))
    duo = _matmul(dyc, wco, "nt", "d_conv_act", 512, 1024, D)
    do2 = _matmul(dyr, wro, "nt", "d_rwkv_act", 512, 1024, D)

    def conv_post_bwd(r, b, g):
        _, vjp = jax.vjp(_s3, r[0], r[1], *g)
        duc, dog, dcb, dlg, dlb = vjp(r[2])
        return [duc, dog], [], [dcb, dlg, dlb]

    (duc, dcog), _, (d_cb, d_lg, d_lb) = _rows(conv_post_bwd, "conv_post_bwd", T, S, 256,
                                               [(uc, D, 0), (pm, D, 2), (duo, D, 0)], [], g3,
                                               [(D, F32), (D, BF16)], [], [(1, D)] * 3)
    dval, dgt, d_ckp = _conv_bwd(pm, duc, ckp, T, S)

    def rwkv_post_bwd(r, b, g):
        _, vjp = jax.vjp(lambda *z: _s5(*z, g[3]), *r[:5], *g[:3])
        res = vjp(r[5])
        return list(res[:5]), [], list(res[5:8])

    (do, dr_b, dk_b, dv_b, drog), _, (d_gg, d_gb, d_rk) = _rows(
        rwkv_post_bwd, "rwkv_post_bwd", T, S, 128, rows5 + [(do2, D, 0)], [], g5,
        [(D, F32)] * 4 + [(D, BF16)], [], [(1, D)] * 3)
    dsc = _scan_bwd(sc_in, hs, do, cm, nb, S)

    def rwkv_pre_bwd(r, b, g):
        _, vjp = jax.vjp(lambda *z: _s4(*z, g[10]), *shifted4(r), *g[:10])
        ct = (r[8] + r[14], r[9], r[10] + r[15], r[11] + r[16], r[12], r[13])
        res = vjp(ct)
        return list(res[:8]), [], list(res[8:18])

    rows4b = rows4 + [(a, D, 0) for a in dsc] + [(dr_b, D, 0), (dk_b, D, 0), (dv_b, D, 0)]
    gshapes = [(1, D), (1, D), (1, D), (1, LORA), (1, D), (LORA, D), (1, D), (LORA, D), (1, D), (1, D)]
    dts, _, gts = _rows(rwkv_pre_bwd, "rwkv_pre_bwd", T, S, 128, rows4b, [], g4,
                        [(D, F32)] * 3 + [(LORA, F32)] + [(D, F32)] * 3 + [(LORA, F32)], [], gshapes)
    dr0, dk0, dv0, dl0, dpr, dpk, dpv, dpl = dts
    d_mu_r, d_mu_k, d_mu_v, d_mu_l, d_w0, d_w2p, d_a0, d_a2p, d_kk, d_ka = gts

    def assemble(r, b, g):
        sh = [_with_next(r[10 + i], r[14 + i], tps4) for i in range(4)]
        main = jnp.concatenate([r[0], r[1], r[2], r[3] + sh[0], r[4] + sh[1], r[5] + sh[2], r[6], r[7], r[8]], axis=1)
        return [main, r[9] + sh[3]], [], []

    rows_a = [(dval, D, 0), (dgt, D, 0), (dcog, D, 0), (dr0, D, 0), (dk0, D, 0), (dv0, D, 0), (drog, D, 0), (dgc, D, 0),
              (dgr, D, 0), (dl0, LORA, 0), (dpr, D, 0), (dpk, D, 0), (dpv, D, 0), (dpl, LORA, 0),
              (dpr, D, 0, "next"), (dpk, D, 0, "next"), (dpv, D, 0, "next"), (dpl, LORA, 0, "next")]
    (dpm, dplo), _, _ = _rows(assemble, "assemble_dp", T, S, 128, rows_a, [], [], [(DMAIN, BF16), (LORA, BF16)], [], [])
    d_wmain = _matmul(h, dpm, "tn", "d_w_main", 512, 1024, min(T, 2048))
    d_wlora = _matmul(h, dplo, "tn", "d_w_lora", 512, LORA, min(T, 2048))
    dh_m = _matmul(dpm, wmain, "nt", "d_h_main", 512, 1024, 3072)
    dh_l = _matmul(dplo, wlora, "nt", "d_h_lora", 512, 1024, LORA)

    def pre_bwd(r, b, g):
        _, vjp = jax.vjp(_s1, r[0], g[0], b[0], b[1])
        dx, dg, dscale, dshift = vjp(r[1] + r[2])
        return [dx + r[3]], [dscale, dshift], [dg]

    (gx,), (dscale, dshift), (d_ng,) = _rows(pre_bwd, "pre_bwd", T, S, 256,
                                             [(x2d, D, 0), (dh_m, D, 0), (dh_l, D, 0), (dx_res, D, 0)],
                                             [scale, shift], [small["norm_g"]], [(D, F32)], [D, D], [(1, D)])
    dmod = jnp.concatenate([dshift, dscale, dgate], axis=-1).reshape(nb, 3 * D)
    d_small = {"norm_g": d_ng, "conv_b": d_cb, "conv_ln_g": d_lg, "conv_ln_b": d_lb,
               "rwkv_mu": jnp.concatenate([d_mu_r, d_mu_k, d_mu_v, d_mu_l], axis=1),
               "rwkv_w0": d_w0, "rwkv_a0": d_a0, "rwkv_k_k": d_kk, "rwkv_k_a": d_ka, "rwkv_r_k": d_rk,
               "rwkv_gn_g": d_gg, "rwkv_gn_b": d_gb, "final_g": d_final_g}
    big = {"wmain": d_wmain, "wlora": d_wlora, "wco": d_wco, "wro": d_wro, "wo": d_wo,
           "ck": d_ckp[:CW], "w2": d_w2p[:64], "a2": d_a2p[64:]}
    return loss_v[0, 0], gx, dmod, big, d_small


def _step(a):
    nb, S, _ = a["x"].shape
    T = nb * S
    x_i, y_i, c_i = _place()
    chip = 2 * x_i + y_i
    w_in = a["w_in"][0]
    WS = w_in.shape[1]
    small_w = {n: a[n].reshape(1, sz) for n, sz in SMALL}

    def halves(t):
        return t.reshape(2, t.shape[0] // 2, t.shape[1])

    g_ins = [a["c"]] + [halves(a[n][0].astype(BF16)) for n in ("w_in", "w_conv_out", "w_rwkv_out", "w_out")]
    g_ins += [a["conv_k"][0], a["rwkv_w2"][0], a["rwkv_a2"][0]]
    g_out = [jax.ShapeDtypeStruct((NDEV,) + g_ins[0].shape, F32)]
    g_out += [jax.ShapeDtypeStruct((NCHIP,) + t.shape, t.dtype) for t in g_ins[1:]]
    c_all, win_g, wco_g, wro_g, wo_g, ck_g, w2_g, a2_g = _comm_call(
        "gather_weights", g_ins, g_out, functools.partial(_gather_plan, 4), 7 + 3 * 7, 3 * 4)
    c_all = c_all.reshape(NDEV * nb, D)
    win_g, wco_g, wro_g, wo_g = (lax.dynamic_update_index_in_dim(g, own, chip, 0)
                                 for g, own in zip((win_g, wco_g, wro_g, wo_g), g_ins[1:5]))
    win_g = win_g.reshape(NCHIP, D, WS)
    j0, o0 = divmod(6 * D, WS)
    j1, o1 = divmod(6 * D + LORA, WS)
    assert j0 == j1, "the lora columns sit inside one shard"
    wmain = jnp.concatenate([win_g[j] for j in range(j0)] + [win_g[j0][:, :o0], win_g[j0][:, o1:]]
                            + [win_g[j] for j in range(j0 + 1, NCHIP)], axis=1)
    wlora = win_g[j0][:, o0:o1]
    wco, wro, wo = (t.reshape(D, D) for t in (wco_g, wro_g, wo_g))
    ck = jnp.concatenate([ck_g[j] for j in range(NCHIP)], axis=1)
    w2 = jnp.concatenate([w2_g[j] for j in range(NCHIP)], axis=1)
    a2 = jnp.concatenate([a2_g[j] for j in range(NCHIP)], axis=1)

    ada_w = a["ada_w"][0]
    MW = ada_w.shape[1]
    ada_b_loc = lax.dynamic_slice(a["ada_b"], (0, chip * MW), (1, MW))

    def mod_body(c_ref, w_ref, b_ref, o_ref):
        o_ref[...] = _dot(_silu(c_ref[...]), w_ref[...], HI) + b_ref[...]

    modp = pl.pallas_call(mod_body, name="ada_mod", out_shape=jax.ShapeDtypeStruct((NDEV * nb, MW), F32),
                          compiler_params=_cparams())(c_all, ada_w, ada_b_loc)
    (mod_g,) = _comm_call("scatter_mod", [modp.reshape(NDEV, nb, MW)],
                          [jax.ShapeDtypeStruct((NDEV, nb, MW), F32)],
                          functools.partial(_scatter_plan, 1), 7)
    mod = mod_g.reshape(NCHIP, 2, nb, MW)
    mod = mod[:, 0].transpose(1, 0, 2).reshape(nb, NCHIP * MW)

    loss_p, gx, dmod, big, d_small = _local_step(
        a["x"].reshape(T, D), a["loss_target"].reshape(T, D), mod, wmain, wlora, wco, wro, wo, ck, w2, a2,
        small_w, nb, S)
    loss = lax.psum(loss_p, ("x", "y", "c"))

    d_small["ada_b"] = _colsum(dmod)
    small_vec = jnp.concatenate([d_small[n] for n, _ in SMALL], axis=1)
    dmod_s = dmod.reshape(nb, NCHIP, MW).transpose(1, 0, 2)
    dmod_s = jnp.repeat(dmod_s, 2, axis=0)
    small_s = jnp.broadcast_to(small_vec[None], (NDEV, 1, NSMALL))
    RW = D // NCHIP
    hv = [big["wmain"].reshape(2, D // 2, DMAIN), big["wlora"].reshape(2, D // 2, LORA)]
    hv += [big[n].reshape(NCHIP, 2, RW // 2, D).transpose(1, 0, 2, 3).reshape(2, NCHIP * RW // 2, D)
           for n in ("wco", "wro", "wo")]
    hv += [big[n].reshape(-1, NCHIP, 2, RW // 2).transpose(2, 1, 0, 3).reshape(2, -1, RW // 2) for n in ("ck", "w2", "a2")]
    got_h = _comm_call("halve_grads", hv, [jax.ShapeDtypeStruct(t.shape[1:], F32) for t in hv], _halve_plan, len(hv))

    def own_half_plus(both, q):
        return [jnp.where(lax.axis_index("c") == 0, both[0], both[1]) + q]

    chip_part = [_ew(own_half_plus, "chip_sum_%d" % i, [hv[i], got_h[i]], 1, 128, BF16)[0] for i in range(len(hv))]
    d_win_h = jnp.concatenate([chip_part[0][:, :6 * D], chip_part[1], chip_part[0][:, 6 * D:]], axis=1)
    sh_s = [jnp.stack([d_win_h[:, j * WS:(j + 1) * WS] for j in range(NCHIP)])]
    sh_s += [t.reshape(NCHIP, RW // 2, D) for t in chip_part[2:5]]
    sh_s += [t.reshape(NCHIP, -1, RW // 2) for t in chip_part[5:]]
    s_ins = [dmod_s, small_s] + sh_s
    s_out = [jax.ShapeDtypeStruct(t.shape, t.dtype) for t in s_ins]
    got = _comm_call("scatter_grads", s_ins, s_out, functools.partial(_scatter_plan, 2), 14 + 3 * len(sh_s))
    dmod_all, small_all = got[0].reshape(NDEV * nb, MW), got[1].reshape(NDEV, NSMALL)
    def shard_sum(recv, sent):
        chip_i = 2 * lax.axis_index("x") + lax.axis_index("y")
        s = None
        for j in range(NCHIP):
            t = jnp.where(chip_i == j, sent[j], recv[j]).astype(F32)
            s = t if s is None else s + t
        return [s]

    fin = [_ew(shard_sum, "shard_sum_%d" % i, [t, sh_s[i]], 1, 128)[0] for i, t in enumerate(got[2:])]
    oth = _comm_call("join_halves", fin, [jax.ShapeDtypeStruct(t.shape, F32) for t in fin], _join_plan, len(fin))

    outs = {}

    def upd_halves(name, mine, other):
        shp = a[name].shape
        R, W = 2 * mine.shape[0], mine.shape[1]
        tm = 128
        nh = R // 2 // tm

        def body(w_ref, m_ref, v_ref, f_ref, o_ref, g_ref, d_ref, m2_ref, v2_ref):
            g = jnp.where(pl.program_id(0) // nh == lax.axis_index("c"), f_ref[...], o_ref[...])
            g_ref[...] = g
            d_ref[...], m2_ref[...], v2_ref[...] = _adamw(w_ref[...], g, m_ref[...], v_ref[...])

        full = pl.BlockSpec((None, tm, W), lambda i: (0, i, 0))
        half = pl.BlockSpec((tm, W), lambda i: (i % nh, 0))
        assert shp == (1, R, W)
        outs[name] = pl.pallas_call(
            body, name="adamw_" + name, grid=(R // tm,), in_specs=[full] * 3 + [half] * 2, out_specs=[full] * 4,
            out_shape=[jax.ShapeDtypeStruct(shp, F32)] * 4, compiler_params=_cparams(("parallel",)),
        )(*[a[p + name] for p in ("", "m_", "v_")], mine, other)

    for name, f, o in zip(("w_in", "w_conv_out", "w_rwkv_out", "w_out"), fin, oth):
        upd_halves(name, f, o)

    def upd(name, g):
        shp = a[name].shape
        ins = [a[p + name].reshape(g.shape) for p in ("", "m_", "v_")]
        res = _ew(lambda w_, m_, v_, g_: [g_, *_adamw(w_, g_, m_, v_)], "adamw_" + name, [*ins, g], 4, 128)
        outs[name] = [r.reshape(shp) for r in res]

    for name, f, o in zip(("conv_k", "rwkv_w2", "rwkv_a2"), fin[4:], oth[4:]):
        both = jnp.where(c_i == 0, jnp.stack([f, o]), jnp.stack([o, f]))
        upd(name, both.transpose(1, 0, 2).reshape(-1, RW))

    def adaw_body(c_ref, dm_ref, w_ref, m_ref, v_ref, g_ref, d_ref, m2_ref, v2_ref):
        g = _dot_tn(_silu(c_ref[...]), dm_ref[...], HI)
        g_ref[...] = g
        d_ref[...], m2_ref[...], v2_ref[...] = _adamw(w_ref[...], g, m_ref[...], v_ref[...])

    res = pl.pallas_call(adaw_body, name="adamw_ada_w", out_shape=[jax.ShapeDtypeStruct((D, MW), F32)] * 4,
                         compiler_params=_cparams())(c_all, dmod_all, ada_w, a["m_ada_w"][0], a["v_ada_w"][0])
    outs["ada_w"] = [r.reshape(a["ada_w"].shape) for r in res]

    wv, mv, vv = (jnp.concatenate([a[p + n].reshape(1, sz) for n, sz in SMALL], axis=1) for p in ("", "m_", "v_"))
    def small_fn(w_, m_, v_, gs):
        g = _sum_slots(gs)
        return [g, *_adamw(w_, g, m_, v_)]

    res = _ew(small_fn, "adamw_small", [wv, mv, vv, small_all.reshape(NDEV, 1, NSMALL)], 4, 8)
    off = 0
    for n, sz in SMALL:
        outs[n] = [r[:, off:off + sz].reshape(a[n].shape) for r in res]
        off += sz

    return (loss, gx.reshape(nb, S, D), *[outs[n][0] for n in WEIGHTS], *[outs[n][1] for n in WEIGHTS],
            *[outs[n][2] for n in WEIGHTS], *[outs[n][3] for n in WEIGHTS])


def _colsum(dmod):
    def body(d_ref, o_ref):
        o_ref[...] = jnp.sum(d_ref[...], axis=0, keepdims=True)
    return pl.pallas_call(body, name="ada_b_rowsum", out_shape=jax.ShapeDtypeStruct((1, dmod.shape[1]), F32),
                          compiler_params=_cparams())(dmod)


def kernel(x, c, ada_w, ada_b, norm_g, w_in, conv_k, conv_b, conv_ln_g, conv_ln_b, w_conv_out, rwkv_mu, rwkv_w0, rwkv_w2, rwkv_a0, rwkv_a2, rwkv_k_k, rwkv_k_a, rwkv_r_k, rwkv_gn_g, rwkv_gn_b, w_rwkv_out, w_out, final_g, loss_target, m_ada_w, m_ada_b, m_norm_g, m_w_in, m_conv_k, m_conv_b, m_conv_ln_g, m_conv_ln_b, m_w_conv_out, m_rwkv_mu, m_rwkv_w0, m_rwkv_w2, m_rwkv_a0, m_rwkv_a2, m_rwkv_k_k, m_rwkv_k_a, m_rwkv_r_k, m_rwkv_gn_g, m_rwkv_gn_b, m_w_rwkv_out, m_w_out, m_final_g, v_ada_w, v_ada_b, v_norm_g, v_w_in, v_conv_k, v_conv_b, v_conv_ln_g, v_conv_ln_b, v_w_conv_out, v_rwkv_mu, v_rwkv_w0, v_rwkv_w2, v_rwkv_a0, v_rwkv_a2, v_rwkv_k_k, v_rwkv_k_a, v_rwkv_r_k, v_rwkv_gn_g, v_rwkv_gn_b, v_w_rwkv_out, v_w_out, v_final_g):
    return _step(dict(locals()))
```

```python
import functools

import numpy as np
import jax
import jax.numpy as jnp
from jax import lax
from jax.experimental import pallas as pl
from jax.experimental.pallas import tpu as pltpu

F32 = jnp.float32
BF16 = jnp.bfloat16
HI = lax.Precision.HIGHEST
MESH = pl.DeviceIdType.MESH
ANY = pl.BlockSpec(memory_space=pl.ANY)

D = 1024
NH = 16
HN = 64
LORA = 128
DMAIN = 9 * D
CH = 64
CW = 31
NCHIP = 4
NDEV = 8
VMEM_LIMIT = 56 * 1024 * 1024

RMS_EPS = 1e-6
LN_EPS = 1e-5
GN_EPS = 64e-5
L2_EPS = 1e-12
ADAM_LR = 0.001
ADAM_B1 = 0.9
ADAM_B2 = 0.999
ADAM_EPS = 1e-08
ADAM_WD = 0.01
ADAM_STEP = 10

SMALL = (("ada_b", 3072), ("norm_g", 1024), ("conv_b", 1024), ("conv_ln_g", 1024), ("conv_ln_b", 1024),
         ("rwkv_mu", 3200), ("rwkv_w0", 1024), ("rwkv_a0", 1024), ("rwkv_k_k", 1024), ("rwkv_k_a", 1024),
         ("rwkv_r_k", 1024), ("rwkv_gn_g", 1024), ("rwkv_gn_b", 1024), ("final_g", 1024))
NSMALL = sum(n for _, n in SMALL)

WEIGHTS = ['ada_w', 'ada_b', 'norm_g', 'w_in', 'conv_k', 'conv_b', 'conv_ln_g', 'conv_ln_b', 'w_conv_out', 'rwkv_mu',
           'rwkv_w0', 'rwkv_w2', 'rwkv_a0', 'rwkv_a2', 'rwkv_k_k', 'rwkv_k_a', 'rwkv_r_k', 'rwkv_gn_g', 'rwkv_gn_b',
           'w_rwkv_out', 'w_out', 'final_g']


def _cparams(sem=None, **kw):
    if sem is not None:
        kw["dimension_semantics"] = sem
    return pltpu.CompilerParams(vmem_limit_bytes=VMEM_LIMIT, **kw)


def _dot(a, b, prec=None):
    return jnp.dot(a, b, preferred_element_type=F32, precision=prec)


def _dot_nt(a, b, prec=None):
    return lax.dot_general(a, b, (((1,), (1,)), ((), ())), preferred_element_type=F32, precision=prec)


def _dot_tn(a, b, prec=None):
    return lax.dot_general(a, b, (((0,), (0,)), ((), ())), preferred_element_type=F32, precision=prec)


def _pdot(f, a, b, p):
    if p == "hi":
        return f(a, b, HI)
    ah, bh = a.astype(BF16), b.astype(BF16)
    if p == "bf":
        return f(ah, bh)
    al, bl = (a - ah.astype(F32)).astype(BF16), (b - bh.astype(F32)).astype(BF16)
    return f(ah, bh) + (f(ah, bl) + f(al, bh))


P_SCORE = "b3"
P_INV = "bf"
P_APPLY = "bf"


def _sigmoid(z):
    return 1.0 / (1.0 + jnp.exp(-z))


def _silu(z):
    return z * _sigmoid(z)


def _matmul(a, b, mode, name, tm, tn, tk):
    if mode == "nn":
        (M, K), N = a.shape, b.shape[1]
        a_spec = pl.BlockSpec((tm, tk), lambda j, i, k: (i, k))
        b_spec = pl.BlockSpec((tk, tn), lambda j, i, k: (k, j))
        f = _dot
    elif mode == "nt":
        (M, K), N = a.shape, b.shape[0]
        a_spec = pl.BlockSpec((tm, tk), lambda j, i, k: (i, k))
        b_spec = pl.BlockSpec((tn, tk), lambda j, i, k: (j, k))
        f = _dot_nt
    else:
        (K, M), N = a.shape, b.shape[1]
        a_spec = pl.BlockSpec((tk, tm), lambda j, i, k: (k, i))
        b_spec = pl.BlockSpec((tk, tn), lambda j, i, k: (k, j))
        f = _dot_tn
    assert M % tm == 0 and N % tn == 0 and K % tk == 0, (name, M, N, K)

    def body(a_ref, b_ref, o_ref):
        @pl.when(pl.program_id(2) == 0)
        def _():
            o_ref[...] = jnp.zeros_like(o_ref)
        o_ref[...] += f(a_ref[...], b_ref[...])

    return pl.pallas_call(
        body, name=name, grid=(N // tn, M // tm, K // tk),
        in_specs=[a_spec, b_spec],
        out_specs=pl.BlockSpec((tm, tn), lambda j, i, k: (i, j)),
        out_shape=jax.ShapeDtypeStruct((M, N), F32),
        compiler_params=_cparams(("parallel", "parallel", "arbitrary")),
    )(a, b)


def _rows(fn, name, T, S, tm, rows, bpars, gpars, outs, baccs, gaccs):
    nb = T // S
    tps = S // tm
    n_r, n_b, n_g, n_o, n_ba, n_ga = len(rows), len(bpars), len(gpars), len(outs), len(baccs), len(gaccs)

    def body(*refs):
        r_refs = refs[:n_r]
        b_refs = refs[n_r:n_r + n_b]
        g_refs = refs[n_r + n_b:n_r + n_b + n_g]
        o_refs = refs[n_r + n_b + n_g:n_r + n_b + n_g + n_o]
        ba_refs = refs[n_r + n_b + n_g + n_o:n_r + n_b + n_g + n_o + n_ba]
        ga_refs = refs[n_r + n_b + n_g + n_o + n_ba:]
        i = pl.program_id(0)
        o_vals, ba_vals, ga_vals = fn([r[...] for r in r_refs], [r[...] for r in b_refs], [r[...] for r in g_refs])
        for r, v in zip(o_refs, o_vals):
            r[...] = v.astype(r.dtype)
        if n_ba:
            @pl.when(i % tps == 0)
            def _():
                for r in ba_refs:
                    r[...] = jnp.zeros_like(r)
            for r, v in zip(ba_refs, ba_vals):
                r[...] += v.reshape(r.shape)
        if n_ga:
            @pl.when(i == 0)
            def _():
                for r in ga_refs:
                    r[...] = jnp.zeros_like(r)
            for r, v in zip(ga_refs, ga_vals):
                r[...] += v.reshape(r.shape)

    def row_spec(w, cb, kind="tile"):
        if kind == "prev":
            return pl.BlockSpec((8, w), lambda i: (jnp.maximum(i * (tm // 8) - 1, 0), cb))
        if kind == "next":
            return pl.BlockSpec((8, w), lambda i: (jnp.minimum((i + 1) * (tm // 8), T // 8 - 1), cb))
        return pl.BlockSpec((tm, w), lambda i: (i, cb))

    in_specs = [row_spec(*r[1:]) for r in rows]
    in_specs += [pl.BlockSpec((None, 1, p.shape[-1]), lambda i: (i // tps, 0, 0)) for p in bpars]
    in_specs += [pl.BlockSpec(p.shape, lambda i: (0, 0)) for p in gpars]
    out_specs = [pl.BlockSpec((tm, w), lambda i: (i, 0)) for w, _ in outs]
    out_specs += [pl.BlockSpec((None, 1, w), lambda i: (i // tps, 0, 0)) for w in baccs]
    out_specs += [pl.BlockSpec(s, lambda i: (0, 0)) for s in gaccs]
    out_shape = [jax.ShapeDtypeStruct((T, w), dt) for w, dt in outs]
    out_shape += [jax.ShapeDtypeStruct((nb, 1, w), F32) for w in baccs]
    out_shape += [jax.ShapeDtypeStruct(s, F32) for s in gaccs]
    res = pl.pallas_call(
        body, name=name, grid=(T // tm,), in_specs=in_specs, out_specs=out_specs, out_shape=out_shape,
        compiler_params=_cparams(("arbitrary",)),
    )(*[r[0] for r in rows], *bpars, *gpars)
    return res[:n_o], res[n_o:n_o + n_ba], res[n_o + n_ba:]


@jax.custom_vjp
def _gsum(z, G):
    zh = z.astype(BF16)
    zl = (z - zh.astype(F32)).astype(BF16)
    r = _dot_nt(zh, G) + _dot_nt(zl, G)
    rh = r.astype(BF16)
    rl = (r - rh.astype(F32)).astype(BF16)
    return _dot(rh, G) + _dot(rl, G)


def _dot3(x, w):
    xh = x.astype(BF16).astype(F32)
    wh = w.astype(BF16).astype(F32)
    xc = jnp.concatenate([xh, xh, x - xh], axis=1).astype(BF16)
    wc = jnp.concatenate([wh, w - wh, wh], axis=0).astype(BF16)
    return _dot(xc, wc)


_gsum.defvjp(lambda z, G: (_gsum(z, G), G), lambda G, ct: (_gsum(ct, G), jnp.zeros_like(G)))


def _s1(x, g, scale, shift):
    y = x * lax.rsqrt(jnp.mean(x * x, axis=-1, keepdims=True) + RMS_EPS)
    return (y * g) * (1.0 + scale) + shift


def _s3(uc, og, cb, lg, lb):
    u = uc + cb
    mu = jnp.mean(u, axis=-1, keepdims=True)
    d = u - mu
    var = jnp.mean(d * d, axis=-1, keepdims=True)
    y = d * lax.rsqrt(var + LN_EPS) * lg + lb
    return _silu(y) * _silu(og)


def _s4(r0, k0, v0, l0, pr, pk, pv, plo, mu_r, mu_k, mu_v, mu_l, w0, w2p, a0, a2p, k_k, k_a, G):
    r = r0 + mu_r * (pr - r0)
    k = k0 + mu_k * (pk - k0)
    v = v0 + mu_v * (pv - v0)
    lo = l0 + mu_l * (plo - l0)
    w_pre = w0 + _dot3(jnp.tanh(lo), w2p)
    lw = -np.float32(np.exp(-0.5)) * _sigmoid(w_pre)
    a = _sigmoid(a0 + _dot3(lo, a2p))
    kkr = k * k_k
    ss = _gsum(kkr * kkr, G)
    kk = kkr / jnp.maximum(jnp.sqrt(ss), L2_EPS)
    k2 = k * (1.0 + (a - 1.0) * k_a)
    return r, lw, k2, v, kk, kk * a


def _s5(o, r, k2, v, og, gg, gb, rk, G):
    mu = _gsum(o, G) * (1.0 / HN)
    d = o - mu
    var = _gsum(d * d, G) * (1.0 / HN)
    y = d * lax.rsqrt(var + GN_EPS) * gg + gb
    bonus = _gsum(r * k2 * rk, G)
    return (y + bonus * v) * _silu(og)


def _s6(yc, yr, gc, gr):
    return _sigmoid(gc) * yc + _sigmoid(gr) * yr


def _s7(x, out, tgt, gate, fg):
    x2 = x + gate * out
    y = x2 * lax.rsqrt(jnp.mean(x2 * x2, axis=-1, keepdims=True) + RMS_EPS) * fg
    e = y - tgt
    return 0.5 * jnp.sum(jnp.mean(e * e, axis=-1))


@jax.custom_vjp
def _solve_all(a_kbs, rhss, cm):
    return _solve_all_fwd(a_kbs, rhss, cm)[0]


def _solve_all_fwd(a_kbs, rhss, cm):
    H = range(len(a_kbs))
    xi = [cm[2] - cm[3] * a_kbs[j] for j in H]
    for lvl in range(1, 6):
        t = [_pdot(_dot, xi[j], cm[3 + lvl] * a_kbs[j], P_INV) for j in H]
        xi = [xi[j] - _pdot(_dot, t[j], xi[j], P_INV) for j in H]
    u = tuple(_pdot(_dot, xi[j], rhss[j], P_APPLY) for j in H)
    return u, (xi, u, cm)


def _solve_all_bwd(res, dus):
    xi, u, cm = res
    H = range(len(u))
    g = tuple(_pdot(_dot_tn, xi[j], dus[j], P_APPLY) for j in H)
    da = tuple(-(cm[1] * _pdot(_dot_nt, g[j], u[j], P_APPLY)) for j in H)
    return da, g, jnp.zeros_like(cm)


_solve_all.defvjp(_solve_all_fwd, _solve_all_bwd)


def _chunk(sts, r, lw, k, v, kk, b, cm):
    cum = _dot(cm[0], lw, HI)
    ein = jnp.exp(-cum)
    rt = r * jnp.exp(cum)
    kkt = kk * jnp.exp(cum - lw)
    kh = k * ein
    bh = b * ein
    ec = jnp.exp(jnp.sum(lw, axis=0, keepdims=True))
    khe = kh * ec
    bhe = bh * ec
    H = range(len(sts))
    tri, strict, eye = cm[0], cm[1], cm[2]
    rt, kkt, kh, bh, v, khe, bhe, ec = ([a[:, j * HN:(j + 1) * HN] for j in H] for a in (rt, kkt, kh, bh, v, khe, bhe, ec))
    lhs = [jnp.concatenate([kkt[j], rt[j]], axis=0) for j in H]
    rhs_s = [jnp.concatenate([bh[j], kh[j]], axis=0) for j in H]
    lh = [a.astype(BF16).astype(F32) for a in lhs]
    rh = [a.astype(BF16).astype(F32) for a in rhs_s]
    lc = [jnp.concatenate([lh[j], lh[j], lhs[j] - lh[j]], axis=1).astype(BF16) for j in H]
    rc = [jnp.concatenate([rh[j], rhs_s[j] - rh[j], rh[j]], axis=1).astype(BF16) for j in H]
    sc = [_dot_nt(lc[j], rc[j]) for j in H]
    a_kb = [strict * sc[j][:CH, :CH] for j in H]
    a_kk = [strict * sc[j][:CH, CH:] for j in H]
    a_rb = [tri * sc[j][CH:, :CH] for j in H]
    a_rk = [tri * sc[j][CH:, CH:] for j in H]
    ps = [_dot_nt(lhs[j].astype(BF16), sts[j].astype(BF16)) for j in H]
    pv = [_dot(jnp.concatenate([a_kk[j], a_rk[j]], axis=0).astype(BF16), v[j].astype(BF16)) for j in H]
    rhs = [ps[j][:CH] + pv[j][:CH] for j in H]
    o0 = [ps[j][CH:] + pv[j][CH:] for j in H]
    u = _solve_all(tuple(a_kb), tuple(rhs), cm)
    o = [o0[j] - _pdot(_dot, a_rb[j], u[j], P_APPLY) for j in H]
    st2 = [sts[j] * ec[j] + _dot_tn(jnp.concatenate([v[j], u[j]], axis=0).astype(BF16),
                                    jnp.concatenate([khe[j], -bhe[j]], axis=0).astype(BF16)) for j in H]
    return jnp.concatenate(o, axis=1), tuple(st2)


def _chunk_consts():
    t = np.arange(CH)[:, None]
    s = np.arange(CH)[None, :]
    mats = [(t >= s), (t > s), (t == s)]
    for lvl in range(6):
        sz = 1 << lvl
        mats.append(((t // sz) % 2 == 1) & ((s // sz) == (t // sz) - 1))
    mats.append(np.zeros((CH, CH), bool))
    return np.stack(mats).astype(np.float32)


def _adamw(w, g, m, v):
    m = ADAM_B1 * m + (1.0 - ADAM_B1) * g
    v = ADAM_B2 * v + (1.0 - ADAM_B2) * (g * g)
    m_hat = m / (1.0 - ADAM_B1 ** ADAM_STEP)
    v_hat = v / (1.0 - ADAM_B2 ** ADAM_STEP)
    delta = -ADAM_LR * (m_hat / (jnp.sqrt(v_hat) + ADAM_EPS) + ADAM_WD * w)
    return delta, m, v


CT = 128
RB = 64
WIN = RB + 32


def _conv_fwd(pm, ck, T, S):
    nb = T // S

    def body(val_ref, gate_ref, ck_ref, out_ref, ubuf):
        ubuf[0:32, :] = jnp.zeros((32, CT), F32)
        ubuf[32:, :] = val_ref[...] * _sigmoid(gate_ref[...])

        def blk(rb, carry):
            base = pl.multiple_of(rb * RB, RB)
            win = ubuf[pl.ds(base, WIN), :]
            acc = jnp.zeros((RB, CT), F32)
            for j in range(CW):
                acc = acc + ck_ref[j:j + 1, :] * pltpu.roll(win, (WIN - (2 + j)) % WIN, 0)[0:RB, :]
            out_ref[pl.ds(base, RB), :] = acc
            return carry

        lax.fori_loop(0, S // RB, blk, 0)

    return pl.pallas_call(
        body, name="conv_fwd", grid=(D // CT, nb),
        in_specs=[pl.BlockSpec((S, CT), lambda ct, b: (b, ct)),
                  pl.BlockSpec((S, CT), lambda ct, b: (b, D // CT + ct)),
                  pl.BlockSpec((32, CT), lambda ct, b: (0, ct))],
        out_specs=pl.BlockSpec((S, CT), lambda ct, b: (b, ct)),
        out_shape=jax.ShapeDtypeStruct((T, D), F32),
        scratch_shapes=[pltpu.VMEM((S + 32, CT), F32)],
        compiler_params=_cparams(("parallel", "arbitrary")),
    )(pm, pm, ck)


def _conv_bwd(pm, duc, ck, T, S):
    nb = T // S

    def body(val_ref, gate_ref, duc_ref, ck_ref, dval_ref, dgate_ref, dck_ref, ubuf, dbuf, acc):
        b = pl.program_id(1)
        ubuf[0:32, :] = jnp.zeros((32, CT), F32)
        ubuf[32:, :] = val_ref[...] * _sigmoid(gate_ref[...])
        dbuf[0:S, :] = duc_ref[...]
        dbuf[S:, :] = jnp.zeros((32, CT), F32)
        acc[...] = jnp.zeros_like(acc)

        def blk(rb, carry):
            base = pl.multiple_of(rb * RB, RB)
            uwin = ubuf[pl.ds(base, WIN), :]
            dwin = dbuf[pl.ds(base, WIN), :]
            dblk = dwin[0:RB, :]
            du = jnp.zeros((RB, CT), F32)
            for j in range(CW):
                du = du + ck_ref[j:j + 1, :] * pltpu.roll(dwin, (WIN - (CW - 1 - j)) % WIN, 0)[0:RB, :]
                ush = pltpu.roll(uwin, (WIN - (2 + j)) % WIN, 0)[0:RB, :]
                acc[j] += jnp.sum((dblk * ush).reshape(RB // 8, 8, CT), axis=0)
            val = val_ref[pl.ds(base, RB), :]
            sg = _sigmoid(gate_ref[pl.ds(base, RB), :])
            dval_ref[pl.ds(base, RB), :] = du * sg
            dgate_ref[pl.ds(base, RB), :] = du * val * sg * (1.0 - sg)
            return carry

        lax.fori_loop(0, S // RB, blk, 0)

        @pl.when(b == 0)
        def _():
            dck_ref[...] = jnp.zeros_like(dck_ref)
        for j in range(CW):
            dck_ref[j:j + 1, :] += jnp.sum(acc[j], axis=0, keepdims=True)

    return pl.pallas_call(
        body, name="conv_bwd", grid=(D // CT, nb),
        in_specs=[pl.BlockSpec((S, CT), lambda ct, b: (b, ct)),
                  pl.BlockSpec((S, CT), lambda ct, b: (b, D // CT + ct)),
                  pl.BlockSpec((S, CT), lambda ct, b: (b, ct)),
                  pl.BlockSpec((32, CT), lambda ct, b: (0, ct))],
        out_specs=[pl.BlockSpec((S, CT), lambda ct, b: (b, ct)),
                   pl.BlockSpec((S, CT), lambda ct, b: (b, ct)),
                   pl.BlockSpec((32, CT), lambda ct, b: (0, ct))],
        out_shape=[jax.ShapeDtypeStruct((T, D), F32), jax.ShapeDtypeStruct((T, D), F32),
                   jax.ShapeDtypeStruct((32, D), F32)],
        scratch_shapes=[pltpu.VMEM((S + 32, CT), F32), pltpu.VMEM((S + 32, CT), F32), pltpu.VMEM((32, 8, CT), F32)],
        compiler_params=_cparams(("parallel", "arbitrary")),
    )(pm, pm, duc, ck)


HB = 16


def _scan_fwd(ins, cm, nb, S):
    nc = S // CH
    blk = pl.BlockSpec((CH, HB * HN), lambda b, g, i: (b * nc + i, g))
    hblk = pl.BlockSpec((None, HB, None, HN, HN), lambda b, g, i: (b, g, i, 0, 0))

    def body(r_ref, lw_ref, k_ref, v_ref, kk_ref, b_ref, cm_ref, o_ref, hs_ref, st):
        @pl.when(pl.program_id(2) == 0)
        def _():
            st[...] = jnp.zeros_like(st)
        s0 = [st[j] for j in range(HB)]
        for j in range(HB):
            hs_ref[j] = s0[j]
        o, s1 = _chunk(s0, r_ref[...], lw_ref[...], k_ref[...], v_ref[...], kk_ref[...], b_ref[...], cm_ref[...])
        o_ref[...] = o
        for j in range(HB):
            st[j] = s1[j]

    return pl.pallas_call(
        body, name="scan_fwd", grid=(nb, NH // HB, nc),
        in_specs=[blk] * 6 + [pl.BlockSpec(cm.shape, lambda b, g, i: (0, 0, 0))],
        out_specs=[blk, hblk],
        out_shape=[jax.ShapeDtypeStruct((nb * S, D), F32), jax.ShapeDtypeStruct((nb, NH, nc, HN, HN), F32)],
        scratch_shapes=[pltpu.VMEM((HB, HN, HN), F32)],
        compiler_params=_cparams(("parallel", "parallel", "arbitrary")),
    )(*ins, cm)


def _scan_bwd(ins, hs, do, cm, nb, S):
    nc = S // CH
    blk = pl.BlockSpec((CH, HB * HN), lambda b, g, i: (b * nc + nc - 1 - i, g))
    hblk = pl.BlockSpec((None, HB, None, HN, HN), lambda b, g, i: (b, g, nc - 1 - i, 0, 0))

    def body(r_ref, lw_ref, k_ref, v_ref, kk_ref, b_ref, hs_ref, do_ref, cm_ref,
             dr_ref, dlw_ref, dk_ref, dv_ref, dkk_ref, db_ref, dst):
        @pl.when(pl.program_id(2) == 0)
        def _():
            dst[...] = jnp.zeros_like(dst)
        cmv = cm_ref[...]
        f = lambda s0, r, lw, k, v, kk, b: _chunk(s0, r, lw, k, v, kk, b, cmv)
        _, vjp = jax.vjp(f, [hs_ref[j] for j in range(HB)], r_ref[...], lw_ref[...], k_ref[...], v_ref[...],
                         kk_ref[...], b_ref[...])
        ds0, dr, dlw, dk, dv, dkk, db = vjp((do_ref[...], tuple(dst[j] for j in range(HB))))
        for j in range(HB):
            dst[j] = ds0[j]
        dr_ref[...] = dr
        dlw_ref[...] = dlw
        dk_ref[...] = dk
        dv_ref[...] = dv
        dkk_ref[...] = dkk
        db_ref[...] = db

    return pl.pallas_call(
        body, name="scan_bwd", grid=(nb, NH // HB, nc),
        in_specs=[blk] * 6 + [hblk, blk, pl.BlockSpec(cm.shape, lambda b, g, i: (0, 0, 0))],
        out_specs=[blk] * 6,
        out_shape=[jax.ShapeDtypeStruct((nb * S, D), F32)] * 6,
        scratch_shapes=[pltpu.VMEM((HB, HN, HN), F32)],
        compiler_params=_cparams(("parallel", "parallel", "arbitrary")),
    )(*ins, hs, do, cm)


def _ew(fn, name, ins, n_out, tm, out_dtype=F32):
    R, W = ins[0].shape[-2:]
    tm = min(tm, R)
    assert R % tm == 0

    def body(*refs):
        vals = fn(*[r[...] for r in refs[:len(ins)]])
        for r, v in zip(refs[len(ins):], vals):
            r[...] = v.astype(r.dtype)

    def spec(a):
        if a.ndim == 3:
            return pl.BlockSpec((a.shape[0], tm, W), lambda i: (0, i, 0))
        return pl.BlockSpec((tm, W), lambda i: (i, 0))

    return pl.pallas_call(
        body, name=name, grid=(R // tm,), in_specs=[spec(a) for a in ins],
        out_specs=[pl.BlockSpec((tm, W), lambda i: (i, 0))] * n_out,
        out_shape=[jax.ShapeDtypeStruct((R, W), out_dtype)] * n_out,
        compiler_params=_cparams(("parallel",)),
    )(*ins)


def _sum_slots(r):
    s = r[0]
    for j in range(1, r.shape[0]):
        s = s + r[j]
    return s


def _place():
    x, y, c = lax.axis_index("x"), lax.axis_index("y"), lax.axis_index("c")
    return x, y, c


def _flip(v, d):
    return 1 - v if d else v


CHIP_PEERS = ((1, 0), (0, 1), (1, 1))
DEV_PEERS = tuple((dx, dy, dc) for dx in (0, 1) for dy in (0, 1) for dc in (0, 1))[1:]


def _comm_call(name, ins, out_shapes, plan, n_rem, n_fwd=0):
    n_in = len(ins)

    def body(*refs):
        in_refs, out_refs = refs[:n_in], refs[n_in:n_in + len(out_shapes)]
        send_sems, recv_sems, loc_sems = refs[n_in + len(out_shapes):]
        loc, rem, *rest = plan(in_refs, out_refs, _place())
        fwd = rest[0] if rest else []
        assert len(rem) == n_rem and len(fwd) == n_fwd and len(loc) <= 2 * n_in, (name, len(loc), len(rem), len(fwd))

        def remote(i, s, d, peer):
            return pltpu.make_async_remote_copy(src_ref=s, dst_ref=d, send_sem=send_sems.at[i], recv_sem=recv_sems.at[i],
                                                device_id=peer, device_id_type=MESH)

        copies = [pltpu.make_async_copy(s, d, loc_sems.at[i]) for i, (s, d) in enumerate(loc)]
        rcopies = [remote(i, s, d, peer) for i, (s, d, peer) in enumerate(rem)]
        for cp in copies + rcopies:
            cp.start()
        landed = set()
        fcopies = []
        for i, (s, d, peer, k) in enumerate(fwd):
            if k not in landed:
                rcopies[k].wait_recv()
                landed.add(k)
            fcopies.append(remote(n_rem + i, s, d, peer))
            fcopies[-1].start()
        for k, cp in enumerate(rcopies):
            if k not in landed:
                cp.wait_recv()
        for cp in rcopies + fcopies:
            cp.wait_send()
        for cp in fcopies:
            cp.wait_recv()
        for cp in copies:
            cp.wait()

    return pl.pallas_call(
        body, name=name, in_specs=[ANY] * n_in, out_specs=[ANY] * len(out_shapes), out_shape=out_shapes,
        scratch_shapes=[pltpu.SemaphoreType.DMA((n_rem + n_fwd,)), pltpu.SemaphoreType.DMA((n_rem + n_fwd,)),
                        pltpu.SemaphoreType.DMA((2 * n_in,))],
        compiler_params=pltpu.CompilerParams(has_side_effects=True),
    )(*ins)


def _gather_plan(n_big, in_refs, out_refs, place):
    x, y, c = place
    chip, dev = 2 * x + y, 4 * x + 2 * y + c
    sib = (x, y, 1 - c)
    loc = [(in_refs[0], out_refs[0].at[dev])] + [(s, d.at[chip]) for s, d in zip(in_refs[1 + n_big:], out_refs[1 + n_big:])]
    rem = [(in_refs[0], out_refs[0].at[dev], (_flip(x, dx), _flip(y, dy), _flip(c, dc))) for dx, dy, dc in DEV_PEERS]
    fwd = []
    for s, d in zip(in_refs[1:1 + n_big], out_refs[1:1 + n_big]):
        for dx, dy in CHIP_PEERS:
            px, py = _flip(x, dx), _flip(y, dy)
            fwd.append((d.at[2 * px + py, c], d.at[2 * px + py, c], sib, len(rem)))
            rem.append((s.at[c], d.at[chip, c], (px, py, c)))
    for s, d in zip(in_refs[1 + n_big:], out_refs[1 + n_big:]):
        rem += [(s, d.at[chip], (_flip(x, dx), _flip(y, dy), c)) for dx, dy in CHIP_PEERS]
    return loc, rem, fwd


def _halve_plan(in_refs, out_refs, place):
    x, y, c = place
    return [], [(s.at[1 - c], d, (x, y, 1 - c)) for s, d in zip(in_refs, out_refs)]


def _join_plan(in_refs, out_refs, place):
    x, y, c = place
    return [], [(s, d, (x, y, 1 - c)) for s, d in zip(in_refs, out_refs)]


def _scatter_plan(n_all, in_refs, out_refs, place):
    x, y, c = place
    chip, dev = 2 * x + y, 4 * x + 2 * y + c
    loc, rem = [], []
    for s, d in zip(in_refs[:n_all], out_refs[:n_all]):
        loc.append((s.at[dev], d.at[dev]))
        for dx, dy, dc in DEV_PEERS:
            px, py, pc = _flip(x, dx), _flip(y, dy), _flip(c, dc)
            rem.append((s.at[4 * px + 2 * py + pc], d.at[dev], (px, py, pc)))
    for s, d in zip(in_refs[n_all:], out_refs[n_all:]):
        for dx, dy in CHIP_PEERS:
            px, py = _flip(x, dx), _flip(y, dy)
            rem.append((s.at[2 * px + py], d.at[chip], (px, py, c)))
    return loc, rem


def _bshape(a, nb):
    return a.reshape(nb, 1, a.shape[-1])


def _with_prev(cur, before, tiles_per_seq):
    first = pl.program_id(0) % tiles_per_seq == 0
    row0 = jnp.where(first, 0.0, before[7:8, :])
    rid = lax.broadcasted_iota(jnp.int32, cur.shape, 0)
    return jnp.where(rid == 0, row0, pltpu.roll(cur, 1, 0))


def _with_next(cur, after, tiles_per_seq):
    last = pl.program_id(0) % tiles_per_seq == tiles_per_seq - 1
    n = cur.shape[0]
    row_n = jnp.where(last, 0.0, after[0:1, :])
    rid = lax.broadcasted_iota(jnp.int32, cur.shape, 0)
    return jnp.where(rid == n - 1, row_n, pltpu.roll(cur, n - 1, 0))


def _local_step(x2d, tgt, mod, wmain, wlora, wco, wro, wo, ck, w2, a2, small, nb, S):
    T = nb * S
    shift, scale, gate = (_bshape(mod[:, i * D:(i + 1) * D], nb) for i in range(3))
    G = jnp.asarray(np.arange(128)[:, None] == np.arange(D)[None, :] // HN, dtype=BF16)
    cm = jnp.asarray(_chunk_consts())
    ckp = jnp.pad(ck, ((0, 1), (0, 0)))
    zpad = jnp.zeros((64, D), F32)
    w2p = jnp.concatenate([w2, zpad], axis=0)
    a2p = jnp.concatenate([zpad, a2], axis=0)
    mu = small["rwkv_mu"]
    mu_r, mu_k, mu_v, mu_l = mu[:, 0:D], mu[:, D:2 * D], mu[:, 2 * D:3 * D], mu[:, 3 * D:]
    g4 = [mu_r, mu_k, mu_v, mu_l, small["rwkv_w0"], w2p, small["rwkv_a0"], a2p, small["rwkv_k_k"], small["rwkv_k_a"], G]
    g5 = [small["rwkv_gn_g"], small["rwkv_gn_b"], small["rwkv_r_k"], G]
    g3 = [small["conv_b"], small["conv_ln_g"], small["conv_ln_b"]]

    (h,), _, _ = _rows(lambda r, b, g: ([_s1(r[0], g[0], b[0], b[1])], [], []), "pre_fwd", T, S, 256,
                       [(x2d, D, 0)], [scale, shift], [small["norm_g"]], [(D, BF16)], [], [])
    pm = _matmul(h, wmain, "nn", "proj_main", min(T, 1024), 1024, D)
    plo = _matmul(h, wlora, "nn", "proj_lora", 512, LORA, D)
    uc = _conv_fwd(pm, ckp, T, S)
    (uo,), _, _ = _rows(lambda r, b, g: ([_s3(r[0], r[1], *g)], [], []), "conv_post_fwd", T, S, 256,
                        [(uc, D, 0), (pm, D, 2)], [], g3, [(D, BF16)], [], [])
    yc = _matmul(uo, wco, "nn", "conv_out", 512, 1024, D)
    rows4 = [(pm, D, 3), (pm, D, 4), (pm, D, 5), (plo, LORA, 0),
             (pm, D, 3, "prev"), (pm, D, 4, "prev"), (pm, D, 5, "prev"), (plo, LORA, 0, "prev")]
    tps4 = S // 128

    def shifted4(r, tps=tps4):
        return list(r[:4]) + [_with_prev(r[i], r[4 + i], tps) for i in range(4)]

    sc_in, _, _ = _rows(lambda r, b, g: (list(_s4(*shifted4(r, S // 256), *g)), [], []), "rwkv_pre_fwd", T, S, 256,
                        rows4, [], g4, [(D, F32)] * 6, [], [])
    o, hs = _scan_fwd(sc_in, cm, nb, S)
    rows5 = [(o, D, 0), (sc_in[0], D, 0), (sc_in[2], D, 0), (sc_in[3], D, 0), (pm, D, 6)]
    (o2,), _, _ = _rows(lambda r, b, g: ([_s5(*r, *g)], [], []), "rwkv_post_fwd", T, S, 256,
                        rows5, [], g5, [(D, BF16)], [], [])
    yr = _matmul(o2, wro, "nn", "rwkv_out", 512, 1024, D)
    rows6 = [(yc, D, 0), (yr, D, 0), (pm, D, 7), (pm, D, 8)]
    (m,), _, _ = _rows(lambda r, b, g: ([_s6(*r)], [], []), "merge_fwd", T, S, 256, rows6, [], [], [(D, BF16)], [], [])
    out = _matmul(m, wo, "nn", "out_proj", 512, 1024, D)

    def head(r, b, g):
        loss, (dx, dout, dgate, dfg) = jax.value_and_grad(_s7, argnums=(0, 1, 3, 4))(r[0], r[1], r[2], b[0], g[0])
        return [dx, dout], [dgate], [dfg, jnp.full((1, 128), loss, F32)]

    (dx_res, dout), (dgate,), (d_final_g, loss_v) = _rows(
        head, "head", T, S, 256, [(x2d, D, 0), (out, D, 0), (tgt, D, 0)], [gate], [small["final_g"]],
        [(D, F32), (D, BF16)], [D], [(1, D), (1, 128)])

    d_wo = _matmul(m, dout, "tn", "d_w_out", 512, 1024, min(T, 2048))
    dm = _matmul(dout, wo, "nt", "d_merge", 512, 1024, D)

    def merge_bwd(r, b, g):
        _, vjp = jax.vjp(_s6, *r[:4])
        dyc, dyr, dgc, dgr = vjp(r[4])
        return [dyc, dyr, dgc, dgr], [], []

    (dyc, dyr, dgc, dgr), _, _ = _rows(merge_bwd, "merge_bwd", T, S, 256, rows6 + [(dm, D, 0)], [], [],
                                       [(D, BF16), (D, BF16), (D, BF16), (D, BF16)], [], [])
    d_wco = _matmul(uo, dyc, "tn", "d_w_conv_out", 512, 1024, min(T, 2048))
    d_wro = _matmul(o2, dyr, "tn", "d_w_rwkv_out", 512, 1024, min(T, 2048))
    duo = _matmul(dyc, wco, "nt", "d_conv_act", 512, 1024, D)
    do2 = _matmul(dyr, wro, "nt", "d_rwkv_act", 512, 1024, D)

    def conv_post_bwd(r, b, g):
        _, vjp = jax.vjp(_s3, r[0], r[1], *g)
        duc, dog, dcb, dlg, dlb = vjp(r[2])
        return [duc, dog], [], [dcb, dlg, dlb]

    (duc, dcog), _, (d_cb, d_lg, d_lb) = _rows(conv_post_bwd, "conv_post_bwd", T, S, 256,
                                               [(uc, D, 0), (pm, D, 2), (duo, D, 0)], [], g3,
                                               [(D, F32), (D, BF16)], [], [(1, D)] * 3)
    dval, dgt, d_ckp = _conv_bwd(pm, duc, ckp, T, S)

    def rwkv_post_bwd(r, b, g):
        _, vjp = jax.vjp(lambda *z: _s5(*z, g[3]), *r[:5], *g[:3])
        res = vjp(r[5])
        return list(res[:5]), [], list(res[5:8])

    (do, dr_b, dk_b, dv_b, drog), _, (d_gg, d_gb, d_rk) = _rows(
        rwkv_post_bwd, "rwkv_post_bwd", T, S, 256, rows5 + [(do2, D, 0)], [], g5,
        [(D, F32)] * 4 + [(D, BF16)], [], [(1, D)] * 3)
    dsc = _scan_bwd(sc_in, hs, do, cm, nb, S)

    def rwkv_pre_bwd(r, b, g):
        _, vjp = jax.vjp(lambda *z: _s4(*z, g[10]), *shifted4(r), *g[:10])
        ct = (r[8] + r[14], r[9], r[10] + r[15], r[11] + r[16], r[12], r[13])
        res = vjp(ct)
        return list(res[:8]), [], list(res[8:18])

    rows4b = rows4 + [(a, D, 0) for a in dsc] + [(dr_b, D, 0), (dk_b, D, 0), (dv_b, D, 0)]
    gshapes = [(1, D), (1, D), (1, D), (1, LORA), (1, D), (LORA, D), (1, D), (LORA, D), (1, D), (1, D)]
    dts, _, gts = _rows(rwkv_pre_bwd, "rwkv_pre_bwd", T, S, 128, rows4b, [], g4,
                        [(D, F32)] * 3 + [(LORA, F32)] + [(D, F32)] * 3 + [(LORA, F32)], [], gshapes)
    dr0, dk0, dv0, dl0, dpr, dpk, dpv, dpl = dts
    d_mu_r, d_mu_k, d_mu_v, d_mu_l, d_w0, d_w2p, d_a0, d_a2p, d_kk, d_ka = gts

    def assemble(r, b, g):
        sh = [_with_next(r[10 + i], r[14 + i], tps4) for i in range(4)]
        main = jnp.concatenate([r[0], r[1], r[2], r[3] + sh[0], r[4] + sh[1], r[5] + sh[2], r[6], r[7], r[8]], axis=1)
        return [main, r[9] + sh[3]], [], []

    rows_a = [(dval, D, 0), (dgt, D, 0), (dcog, D, 0), (dr0, D, 0), (dk0, D, 0), (dv0, D, 0), (drog, D, 0), (dgc, D, 0),
              (dgr, D, 0), (dl0, LORA, 0), (dpr, D, 0), (dpk, D, 0), (dpv, D, 0), (dpl, LORA, 0),
              (dpr, D, 0, "next"), (dpk, D, 0, "next"), (dpv, D, 0, "next"), (dpl, LORA, 0, "next")]
    (dpm, dplo), _, _ = _rows(assemble, "assemble_dp", T, S, 128, rows_a, [], [], [(DMAIN, BF16), (LORA, BF16)], [], [])
    d_wmain = _matmul(h, dpm, "tn", "d_w_main", 512, 1024, min(T, 2048))
    d_wlora = _matmul(h, dplo, "tn", "d_w_lora", 512, LORA, min(T, 2048))
    dh_m = _matmul(dpm, wmain, "nt", "d_h_main", 512, 1024, 3072)
    dh_l = _matmul(dplo, wlora, "nt", "d_h_lora", 512, 1024, LORA)

    def pre_bwd(r, b, g):
        _, vjp = jax.vjp(_s1, r[0], g[0], b[0], b[1])
        dx, dg, dscale, dshift = vjp(r[1] + r[2])
        return [dx + r[3]], [dscale, dshift], [dg]

    (gx,), (dscale, dshift), (d_ng,) = _rows(pre_bwd, "pre_bwd", T, S, 256,
                                             [(x2d, D, 0), (dh_m, D, 0), (dh_l, D, 0), (dx_res, D, 0)],
                                             [scale, shift], [small["norm_g"]], [(D, F32)], [D, D], [(1, D)])
    dmod = jnp.concatenate([dshift, dscale, dgate], axis=-1).reshape(nb, 3 * D)
    d_small = {"norm_g": d_ng, "conv_b": d_cb, "conv_ln_g": d_lg, "conv_ln_b": d_lb,
               "rwkv_mu": jnp.concatenate([d_mu_r, d_mu_k, d_mu_v, d_mu_l], axis=1),
               "rwkv_w0": d_w0, "rwkv_a0": d_a0, "rwkv_k_k": d_kk, "rwkv_k_a": d_ka, "rwkv_r_k": d_rk,
               "rwkv_gn_g": d_gg, "rwkv_gn_b": d_gb, "final_g": d_final_g}
    big = {"wmain": d_wmain, "wlora": d_wlora, "wco": d_wco, "wro": d_wro, "wo": d_wo,
           "ck": d_ckp[:CW], "w2": d_w2p[:64], "a2": d_a2p[64:]}
    return loss_v[0, 0], gx, dmod, big, d_small


def _step(a):
    nb, S, _ = a["x"].shape
    T = nb * S
    x_i, y_i, c_i = _place()
    chip = 2 * x_i + y_i
    w_in = a["w_in"][0]
    WS = w_in.shape[1]
    small_w = {n: a[n].reshape(1, sz) for n, sz in SMALL}

    def halves(t):
        return t.reshape(2, t.shape[0] // 2, t.shape[1])

    g_ins = [a["c"]] + [halves(a[n][0].astype(BF16)) for n in ("w_in", "w_conv_out", "w_rwkv_out", "w_out")]
    g_ins += [a["conv_k"][0], a["rwkv_w2"][0], a["rwkv_a2"][0]]
    g_out = [jax.ShapeDtypeStruct((NDEV,) + g_ins[0].shape, F32)]
    g_out += [jax.ShapeDtypeStruct((NCHIP,) + t.shape, t.dtype) for t in g_ins[1:]]
    c_all, win_g, wco_g, wro_g, wo_g, ck_g, w2_g, a2_g = _comm_call(
        "gather_weights", g_ins, g_out, functools.partial(_gather_plan, 4), 7 + 3 * 7, 3 * 4)
    c_all = c_all.reshape(NDEV * nb, D)
    win_g, wco_g, wro_g, wo_g = (lax.dynamic_update_index_in_dim(g, own, chip, 0)
                                 for g, own in zip((win_g, wco_g, wro_g, wo_g), g_ins[1:5]))
    win_g = win_g.reshape(NCHIP, D, WS)
    j0, o0 = divmod(6 * D, WS)
    j1, o1 = divmod(6 * D + LORA, WS)
    assert j0 == j1, "the lora columns sit inside one shard"
    wmain = jnp.concatenate([win_g[j] for j in range(j0)] + [win_g[j0][:, :o0], win_g[j0][:, o1:]]
                            + [win_g[j] for j in range(j0 + 1, NCHIP)], axis=1)
    wlora = win_g[j0][:, o0:o1]
    wco, wro, wo = (t.reshape(D, D) for t in (wco_g, wro_g, wo_g))
    ck = jnp.concatenate([ck_g[j] for j in range(NCHIP)], axis=1)
    w2 = jnp.concatenate([w2_g[j] for j in range(NCHIP)], axis=1)
    a2 = jnp.concatenate([a2_g[j] for j in range(NCHIP)], axis=1)

    ada_w = a["ada_w"][0]
    MW = ada_w.shape[1]
    ada_b_loc = lax.dynamic_slice(a["ada_b"], (0, chip * MW), (1, MW))

    def mod_body(c_ref, w_ref, b_ref, o_ref):
        o_ref[...] = _dot(_silu(c_ref[...]), w_ref[...], HI) + b_ref[...]

    modp = pl.pallas_call(mod_body, name="ada_mod", out_shape=jax.ShapeDtypeStruct((NDEV * nb, MW), F32),
                          compiler_params=_cparams())(c_all, ada_w, ada_b_loc)
    (mod_g,) = _comm_call("scatter_mod", [modp.reshape(NDEV, nb, MW)],
                          [jax.ShapeDtypeStruct((NDEV, nb, MW), F32)],
                          functools.partial(_scatter_plan, 1), 7)
    mod = mod_g.reshape(NCHIP, 2, nb, MW)
    mod = mod[:, 0].transpose(1, 0, 2).reshape(nb, NCHIP * MW)

    loss_p, gx, dmod, big, d_small = _local_step(
        a["x"].reshape(T, D), a["loss_target"].reshape(T, D), mod, wmain, wlora, wco, wro, wo, ck, w2, a2,
        small_w, nb, S)
    loss = lax.psum(loss_p, ("x", "y", "c"))

    d_small["ada_b"] = _colsum(dmod)
    small_vec = jnp.concatenate([d_small[n] for n, _ in SMALL], axis=1)
    dmod_s = dmod.reshape(nb, NCHIP, MW).transpose(1, 0, 2)
    dmod_s = jnp.repeat(dmod_s, 2, axis=0)
    small_s = jnp.broadcast_to(small_vec[None], (NDEV, 1, NSMALL))
    RW = D // NCHIP
    hv = [big["wmain"].reshape(2, D // 2, DMAIN), big["wlora"].reshape(2, D // 2, LORA)]
    hv += [big[n].reshape(NCHIP, 2, RW // 2, D).transpose(1, 0, 2, 3).reshape(2, NCHIP * RW // 2, D)
           for n in ("wco", "wro", "wo")]
    hv += [big[n].reshape(-1, NCHIP, 2, RW // 2).transpose(2, 1, 0, 3).reshape(2, -1, RW // 2) for n in ("ck", "w2", "a2")]
    got_h = _comm_call("halve_grads", hv, [jax.ShapeDtypeStruct(t.shape[1:], F32) for t in hv], _halve_plan, len(hv))

    def own_half_plus(both, q):
        return [jnp.where(lax.axis_index("c") == 0, both[0], both[1]) + q]

    chip_part = [_ew(own_half_plus, "chip_sum_%d" % i, [hv[i], got_h[i]], 1, 128, BF16)[0] for i in range(len(hv))]
    d_win_h = jnp.concatenate([chip_part[0][:, :6 * D], chip_part[1], chip_part[0][:, 6 * D:]], axis=1)
    sh_s = [jnp.stack([d_win_h[:, j * WS:(j + 1) * WS] for j in range(NCHIP)])]
    sh_s += [t.reshape(NCHIP, RW // 2, D) for t in chip_part[2:5]]
    sh_s += [t.reshape(NCHIP, -1, RW // 2) for t in chip_part[5:]]
    s_ins = [dmod_s, small_s] + sh_s
    s_out = [jax.ShapeDtypeStruct(t.shape, t.dtype) for t in s_ins]
    got = _comm_call("scatter_grads", s_ins, s_out, functools.partial(_scatter_plan, 2), 14 + 3 * len(sh_s))
    dmod_all, small_all = got[0].reshape(NDEV * nb, MW), got[1].reshape(NDEV, NSMALL)
    def shard_sum(recv, sent):
        chip_i = 2 * lax.axis_index("x") + lax.axis_index("y")
        s = None
        for j in range(NCHIP):
            t = jnp.where(chip_i == j, sent[j], recv[j]).astype(F32)
            s = t if s is None else s + t
        return [s]

    fin = [_ew(shard_sum, "shard_sum_%d" % i, [t, sh_s[i]], 1, 128)[0] for i, t in enumerate(got[2:])]
    oth = _comm_call("join_halves", fin, [jax.ShapeDtypeStruct(t.shape, F32) for t in fin], _join_plan, len(fin))

    outs = {}

    def upd_halves(name, mine, other):
        shp = a[name].shape
        R, W = 2 * mine.shape[0], mine.shape[1]
        tm = 128
        nh = R // 2 // tm

        def body(w_ref, m_ref, v_ref, f_ref, o_ref, g_ref, d_ref, m2_ref, v2_ref):
            g = jnp.where(pl.program_id(0) // nh == lax.axis_index("c"), f_ref[...], o_ref[...])
            g_ref[...] = g
            d_ref[...], m2_ref[...], v2_ref[...] = _adamw(w_ref[...], g, m_ref[...], v_ref[...])

        full = pl.BlockSpec((None, tm, W), lambda i: (0, i, 0))
        half = pl.BlockSpec((tm, W), lambda i: (i % nh, 0))
        assert shp == (1, R, W)
        outs[name] = pl.pallas_call(
            body, name="adamw_" + name, grid=(R // tm,), in_specs=[full] * 3 + [half] * 2, out_specs=[full] * 4,
            out_shape=[jax.ShapeDtypeStruct(shp, F32)] * 4, compiler_params=_cparams(("parallel",)),
        )(*[a[p + name] for p in ("", "m_", "v_")], mine, other)

    for name, f, o in zip(("w_in", "w_conv_out", "w_rwkv_out", "w_out"), fin, oth):
        upd_halves(name, f, o)

    def upd(name, g):
        shp = a[name].shape
        ins = [a[p + name].reshape(g.shape) for p in ("", "m_", "v_")]
        res = _ew(lambda w_, m_, v_, g_: [g_, *_adamw(w_, g_, m_, v_)], "adamw_" + name, [*ins, g], 4, 128)
        outs[name] = [r.reshape(shp) for r in res]

    for name, f, o in zip(("conv_k", "rwkv_w2", "rwkv_a2"), fin[4:], oth[4:]):
        both = jnp.where(c_i == 0, jnp.stack([f, o]), jnp.stack([o, f]))
        upd(name, both.transpose(1, 0, 2).reshape(-1, RW))

    def adaw_body(c_ref, dm_ref, w_ref, m_ref, v_ref, g_ref, d_ref, m2_ref, v2_ref):
        g = _dot_tn(_silu(c_ref[...]), dm_ref[...], HI)
        g_ref[...] = g
        d_ref[...], m2_ref[...], v2_ref[...] = _adamw(w_ref[...], g, m_ref[...], v_ref[...])

    res = pl.pallas_call(adaw_body, name="adamw_ada_w", out_shape=[jax.ShapeDtypeStruct((D, MW), F32)] * 4,
                         compiler_params=_cparams())(c_all, dmod_all, ada_w, a["m_ada_w"][0], a["v_ada_w"][0])
    outs["ada_w"] = [r.reshape(a["ada_w"].shape) for r in res]

    wv, mv, vv = (jnp.concatenate([a[p + n].reshape(1, sz) for n, sz in SMALL], axis=1) for p in ("", "m_", "v_"))
    def small_fn(w_, m_, v_, gs):
        g = _sum_slots(gs)
        return [g, *_adamw(w_, g, m_, v_)]

    res = _ew(small_fn, "adamw_small", [wv, mv, vv, small_all.reshape(NDEV, 1, NSMALL)], 4, 8)
    off = 0
    for n, sz in SMALL:
        outs[n] = [r[:, off:off + sz].reshape(a[n].shape) for r in res]
        off += sz

    return (loss, gx.reshape(nb, S, D), *[outs[n][0] for n in WEIGHTS], *[outs[n][1] for n in WEIGHTS],
            *[outs[n][2] for n in WEIGHTS], *[outs[n][3] for n in WEIGHTS])


def _colsum(dmod):
    def body(d_ref, o_ref):
        o_ref[...] = jnp.sum(d_ref[...], axis=0, keepdims=True)
    return pl.pallas_call(body, name="ada_b_rowsum", out_shape=jax.ShapeDtypeStruct((1, dmod.shape[1]), F32),
                          compiler_params=_cparams())(dmod)


def kernel(x, c, ada_w, ada_b, norm_g, w_in, conv_k, conv_b, conv_ln_g, conv_ln_b, w_conv_out, rwkv_mu, rwkv_w0, rwkv_w2, rwkv_a0, rwkv_a2, rwkv_k_k, rwkv_k_a, rwkv_r_k, rwkv_gn_g, rwkv_gn_b, w_rwkv_out, w_out, final_g, loss_target, m_ada_w, m_ada_b, m_norm_g, m_w_in, m_conv_k, m_conv_b, m_conv_ln_g, m_conv_ln_b, m_w_conv_out, m_rwkv_mu, m_rwkv_w0, m_rwkv_w2, m_rwkv_a0, m_rwkv_a2, m_rwkv_k_k, m_rwkv_k_a, m_rwkv_r_k, m_rwkv_gn_g, m_rwkv_gn_b, m_w_rwkv_out, m_w_out, m_final_g, v_ada_w, v_ada_b, v_norm_g, v_w_in, v_conv_k, v_conv_b, v_conv_ln_g, v_conv_ln_b, v_w_conv_out, v_rwkv_mu, v_rwkv_w0, v_rwkv_w2, v_rwkv_a0, v_rwkv_a2, v_rwkv_k_k, v_rwkv_k_a, v_rwkv_r_k, v_rwkv_gn_g, v_rwkv_gn_b, v_w_rwkv_out, v_w_out, v_final_g):
    return _step(dict(locals()))
```

```python
import functools

import numpy as np
import jax
import jax.numpy as jnp
from jax import lax
from jax.experimental import pallas as pl
from jax.experimental.pallas import tpu as pltpu

F32 = jnp.float32
BF16 = jnp.bfloat16
HI = lax.Precision.HIGHEST
MESH = pl.DeviceIdType.MESH
ANY = pl.BlockSpec(memory_space=pl.ANY)

D = 1024
NH = 16
HN = 64
LORA = 128
DMAIN = 9 * D
CH = 64
CW = 31
NCHIP = 4
NDEV = 8
VMEM_LIMIT = 56 * 1024 * 1024

RMS_EPS = 1e-6
LN_EPS = 1e-5
GN_EPS = 64e-5
L2_EPS = 1e-12
ADAM_LR = 0.001
ADAM_B1 = 0.9
ADAM_B2 = 0.999
ADAM_EPS = 1e-08
ADAM_WD = 0.01
ADAM_STEP = 10

SMALL = (("ada_b", 3072), ("norm_g", 1024), ("conv_b", 1024), ("conv_ln_g", 1024), ("conv_ln_b", 1024),
         ("rwkv_mu", 3200), ("rwkv_w0", 1024), ("rwkv_a0", 1024), ("rwkv_k_k", 1024), ("rwkv_k_a", 1024),
         ("rwkv_r_k", 1024), ("rwkv_gn_g", 1024), ("rwkv_gn_b", 1024), ("final_g", 1024))
NSMALL = sum(n for _, n in SMALL)

WEIGHTS = ['ada_w', 'ada_b', 'norm_g', 'w_in', 'conv_k', 'conv_b', 'conv_ln_g', 'conv_ln_b', 'w_conv_out', 'rwkv_mu',
           'rwkv_w0', 'rwkv_w2', 'rwkv_a0', 'rwkv_a2', 'rwkv_k_k', 'rwkv_k_a', 'rwkv_r_k', 'rwkv_gn_g', 'rwkv_gn_b',
           'w_rwkv_out', 'w_out', 'final_g']


def _cparams(sem=None, **kw):
    if sem is not None:
        kw["dimension_semantics"] = sem
    return pltpu.CompilerParams(vmem_limit_bytes=VMEM_LIMIT, **kw)


def _dot(a, b, prec=None):
    return jnp.dot(a, b, preferred_element_type=F32, precision=prec)


def _dot_nt(a, b, prec=None):
    return lax.dot_general(a, b, (((1,), (1,)), ((), ())), preferred_element_type=F32, precision=prec)


def _dot_tn(a, b, prec=None):
    return lax.dot_general(a, b, (((0,), (0,)), ((), ())), preferred_element_type=F32, precision=prec)


def _pdot(f, a, b, p):
    if p == "hi":
        return f(a, b, HI)
    ah, bh = a.astype(BF16), b.astype(BF16)
    if p == "bf":
        return f(ah, bh)
    al, bl = (a - ah.astype(F32)).astype(BF16), (b - bh.astype(F32)).astype(BF16)
    return f(ah, bh) + (f(ah, bl) + f(al, bh))


P_SCORE = "b3"
P_INV = "bf"
P_APPLY = "bf"


def _sigmoid(z):
    return 1.0 / (1.0 + jnp.exp(-z))


def _silu(z):
    return z * _sigmoid(z)


def _matmul(a, b, mode, name, tm, tn, tk, ride=None):
    if mode == "nn":
        (M, K), N = a.shape, b.shape[1]
        a_spec = pl.BlockSpec((tm, tk), lambda j, i, k: (i, k))
        b_spec = pl.BlockSpec((tk, tn), lambda j, i, k: (k, j))
        f = _dot
    elif mode == "nt":
        (M, K), N = a.shape, b.shape[0]
        a_spec = pl.BlockSpec((tm, tk), lambda j, i, k: (i, k))
        b_spec = pl.BlockSpec((tn, tk), lambda j, i, k: (j, k))
        f = _dot_nt
    else:
        (K, M), N = a.shape, b.shape[1]
        a_spec = pl.BlockSpec((tk, tm), lambda j, i, k: (k, i))
        b_spec = pl.BlockSpec((tk, tn), lambda j, i, k: (k, j))
        f = _dot_tn
    assert M % tm == 0 and N % tn == 0 and K % tk == 0, (name, M, N, K)

    grid = (N // tn, M // tm, K // tk)
    o_spec = pl.BlockSpec((tm, tn), lambda j, i, k: (i, j))
    o_shape = jax.ShapeDtypeStruct((M, N), F32)

    def step(a_ref, b_ref, o_ref):
        @pl.when(pl.program_id(2) == 0)
        def _():
            o_ref[...] = jnp.zeros_like(o_ref)
        o_ref[...] += f(a_ref[...], b_ref[...])

    if ride is None:
        return pl.pallas_call(
            step, name=name, grid=grid, in_specs=[a_spec, b_spec], out_specs=o_spec, out_shape=o_shape,
            compiler_params=_cparams(("parallel", "parallel", "arbitrary")),
        )(a, b)

    r_ins, r_shapes, plan, n_rem = ride
    n_ri, n_ro = len(r_ins), len(r_shapes)

    def body(a_ref, b_ref, *rest):
        r_in, o_ref, r_out = rest[:n_ri], rest[n_ri], rest[n_ri + 1:n_ri + 1 + n_ro]
        send_sems, recv_sems = rest[n_ri + 1 + n_ro:]
        loc, rem = plan(r_in, r_out, _place())
        assert not loc and len(rem) == n_rem, (name, len(loc), len(rem))
        copies = [pltpu.make_async_remote_copy(src_ref=s, dst_ref=d, send_sem=send_sems.at[i], recv_sem=recv_sems.at[i],
                                               device_id=peer, device_id_type=MESH) for i, (s, d, peer) in enumerate(rem)]
        pid = [pl.program_id(ax) for ax in range(3)]

        @pl.when((pid[0] == 0) & (pid[1] == 0) & (pid[2] == 0))
        def _():
            for cp in copies:
                cp.start()

        step(a_ref, b_ref, o_ref)

        @pl.when((pid[0] == grid[0] - 1) & (pid[1] == grid[1] - 1) & (pid[2] == grid[2] - 1))
        def _():
            for cp in copies:
                cp.wait_send()
            for cp in copies:
                cp.wait_recv()

    return pl.pallas_call(
        body, name=name, grid=grid, in_specs=[a_spec, b_spec] + [ANY] * n_ri, out_specs=[o_spec] + [ANY] * n_ro,
        out_shape=[o_shape] + list(r_shapes),
        scratch_shapes=[pltpu.SemaphoreType.DMA((n_rem,)), pltpu.SemaphoreType.DMA((n_rem,))],
        compiler_params=_cparams(("arbitrary", "arbitrary", "arbitrary"), has_side_effects=True),
    )(a, b, *r_ins)


def _rows(fn, name, T, S, tm, rows, bpars, gpars, outs, baccs, gaccs):
    nb = T // S
    tps = S // tm
    n_r, n_b, n_g, n_o, n_ba, n_ga = len(rows), len(bpars), len(gpars), len(outs), len(baccs), len(gaccs)

    def body(*refs):
        r_refs = refs[:n_r]
        b_refs = refs[n_r:n_r + n_b]
        g_refs = refs[n_r + n_b:n_r + n_b + n_g]
        o_refs = refs[n_r + n_b + n_g:n_r + n_b + n_g + n_o]
        ba_refs = refs[n_r + n_b + n_g + n_o:n_r + n_b + n_g + n_o + n_ba]
        ga_refs = refs[n_r + n_b + n_g + n_o + n_ba:]
        i = pl.program_id(0)
        o_vals, ba_vals, ga_vals = fn([r[...] for r in r_refs], [r[...] for r in b_refs], [r[...] for r in g_refs])
        for r, v in zip(o_refs, o_vals):
            r[...] = v.astype(r.dtype)
        if n_ba:
            @pl.when(i % tps == 0)
            def _():
                for r in ba_refs:
                    r[...] = jnp.zeros_like(r)
            for r, v in zip(ba_refs, ba_vals):
                r[...] += v.reshape(r.shape)
        if n_ga:
            @pl.when(i == 0)
            def _():
                for r in ga_refs:
                    r[...] = jnp.zeros_like(r)
            for r, v in zip(ga_refs, ga_vals):
                r[...] += v.reshape(r.shape)

    def row_spec(arr, w, cb, kind="tile"):
        hr = 8 * (4 // arr.dtype.itemsize)
        if kind == "prev":
            return pl.BlockSpec((hr, w), lambda i: (jnp.maximum(i * (tm // hr) - 1, 0), cb))
        if kind == "next":
            return pl.BlockSpec((hr, w), lambda i: (jnp.minimum((i + 1) * (tm // hr), T // hr - 1), cb))
        return pl.BlockSpec((tm, w), lambda i: (i, cb))

    in_specs = [row_spec(*r) for r in rows]
    in_specs += [pl.BlockSpec((None, 1, p.shape[-1]), lambda i: (i // tps, 0, 0)) for p in bpars]
    in_specs += [pl.BlockSpec(p.shape, lambda i: (0, 0)) for p in gpars]
    out_specs = [pl.BlockSpec((tm, w), lambda i: (i, 0)) for w, _ in outs]
    out_specs += [pl.BlockSpec((None, 1, w), lambda i: (i // tps, 0, 0)) for w in baccs]
    out_specs += [pl.BlockSpec(s, lambda i: (0, 0)) for s in gaccs]
    out_shape = [jax.ShapeDtypeStruct((T, w), dt) for w, dt in outs]
    out_shape += [jax.ShapeDtypeStruct((nb, 1, w), F32) for w in baccs]
    out_shape += [jax.ShapeDtypeStruct(s, F32) for s in gaccs]
    res = pl.pallas_call(
        body, name=name, grid=(T // tm,), in_specs=in_specs, out_specs=out_specs, out_shape=out_shape,
        compiler_params=_cparams(("arbitrary",)),
    )(*[r[0] for r in rows], *bpars, *gpars)
    return res[:n_o], res[n_o:n_o + n_ba], res[n_o + n_ba:]


@jax.custom_vjp
def _gsum(z, G):
    zh = z.astype(BF16)
    zl = (z - zh.astype(F32)).astype(BF16)
    r = _dot_nt(zh, G) + _dot_nt(zl, G)
    rh = r.astype(BF16)
    rl = (r - rh.astype(F32)).astype(BF16)
    return _dot(rh, G) + _dot(rl, G)


def _dot3(x, w):
    xh = x.astype(BF16).astype(F32)
    wh = w.astype(BF16).astype(F32)
    xc = jnp.concatenate([xh, xh, x - xh], axis=1).astype(BF16)
    wc = jnp.concatenate([wh, w - wh, wh], axis=0).astype(BF16)
    return _dot(xc, wc)


_gsum.defvjp(lambda z, G: (_gsum(z, G), G), lambda G, ct: (_gsum(ct, G), jnp.zeros_like(G)))


def _s1(x, g, scale, shift):
    y = x * lax.rsqrt(jnp.mean(x * x, axis=-1, keepdims=True) + RMS_EPS)
    return (y * g) * (1.0 + scale) + shift


def _s3(uc, og, cb, lg, lb):
    u = uc + cb
    mu = jnp.mean(u, axis=-1, keepdims=True)
    d = u - mu
    var = jnp.mean(d * d, axis=-1, keepdims=True)
    y = d * lax.rsqrt(var + LN_EPS) * lg + lb
    return _silu(y) * _silu(og)


def _s4(r0, k0, v0, l0, pr, pk, pv, plo, mu_r, mu_k, mu_v, mu_l, w0, w2p, a0, a2p, k_k, k_a, G):
    r = r0 + mu_r * (pr - r0)
    k = k0 + mu_k * (pk - k0)
    v = v0 + mu_v * (pv - v0)
    lo = l0 + mu_l * (plo - l0)
    w_pre = w0 + _dot3(jnp.tanh(lo), w2p)
    lw = -np.float32(np.exp(-0.5)) * _sigmoid(w_pre)
    a = _sigmoid(a0 + _dot3(lo, a2p))
    kkr = k * k_k
    ss = _gsum(kkr * kkr, G)
    kk = kkr / jnp.maximum(jnp.sqrt(ss), L2_EPS)
    k2 = k * (1.0 + (a - 1.0) * k_a)
    return r, lw, k2, v, kk, kk * a


def _s5(o, r, k2, v, og, gg, gb, rk, G):
    mu = _gsum(o, G) * (1.0 / HN)
    d = o - mu
    var = _gsum(d * d, G) * (1.0 / HN)
    y = d * lax.rsqrt(var + GN_EPS) * gg + gb
    bonus = _gsum(r * k2 * rk, G)
    return (y + bonus * v) * _silu(og)


def _s6(yc, yr, gc, gr):
    return _sigmoid(gc) * yc + _sigmoid(gr) * yr


def _s7(x, out, tgt, gate, fg):
    x2 = x + gate * out
    y = x2 * lax.rsqrt(jnp.mean(x2 * x2, axis=-1, keepdims=True) + RMS_EPS) * fg
    e = y - tgt
    return 0.5 * jnp.sum(jnp.mean(e * e, axis=-1))


@jax.custom_vjp
def _solve_all(a_kbs, rhss, cm):
    return _solve_all_fwd(a_kbs, rhss, cm)[0]


def _solve_all_fwd(a_kbs, rhss, cm):
    H = range(len(a_kbs))
    xi = [cm[2] - cm[3] * a_kbs[j] for j in H]
    for lvl in range(1, 6):
        t = [_pdot(_dot, xi[j], cm[3 + lvl] * a_kbs[j], P_INV) for j in H]
        xi = [xi[j] - _pdot(_dot, t[j], xi[j], P_INV) for j in H]
    u = tuple(_pdot(_dot, xi[j], rhss[j], P_APPLY) for j in H)
    return u, (xi, u, cm)


def _solve_all_bwd(res, dus):
    xi, u, cm = res
    H = range(len(u))
    g = tuple(_pdot(_dot_tn, xi[j], dus[j], P_APPLY) for j in H)
    da = tuple(-(cm[1] * _pdot(_dot_nt, g[j], u[j], P_APPLY)) for j in H)
    return da, g, jnp.zeros_like(cm)


_solve_all.defvjp(_solve_all_fwd, _solve_all_bwd)


def _chunk(sts, r, lw, k, v, kk, b, cm):
    cum = _dot(cm[0], lw, HI)
    ein = jnp.exp(-cum)
    rt = r * jnp.exp(cum)
    kkt = kk * jnp.exp(cum - lw)
    kh = k * ein
    bh = b * ein
    ec = jnp.exp(jnp.sum(lw, axis=0, keepdims=True))
    khe = kh * ec
    bhe = bh * ec
    H = range(len(sts))
    tri, strict, eye = cm[0], cm[1], cm[2]
    rt, kkt, kh, bh, v, khe, bhe, ec = ([a[:, j * HN:(j + 1) * HN] for j in H] for a in (rt, kkt, kh, bh, v, khe, bhe, ec))
    lhs = [jnp.concatenate([kkt[j], rt[j]], axis=0) for j in H]
    rhs_s = [jnp.concatenate([bh[j], kh[j]], axis=0) for j in H]
    lh = [a.astype(BF16).astype(F32) for a in lhs]
    rh = [a.astype(BF16).astype(F32) for a in rhs_s]
    lc = [jnp.concatenate([lh[j], lh[j], lhs[j] - lh[j]], axis=1).astype(BF16) for j in H]
    rc = [jnp.concatenate([rh[j], rhs_s[j] - rh[j], rh[j]], axis=1).astype(BF16) for j in H]
    sc = [_dot_nt(lc[j], rc[j]) for j in H]
    a_kb = [strict * sc[j][:CH, :CH] for j in H]
    a_kk = [strict * sc[j][:CH, CH:] for j in H]
    a_rb = [tri * sc[j][CH:, :CH] for j in H]
    a_rk = [tri * sc[j][CH:, CH:] for j in H]
    ps = [_dot_nt(lhs[j].astype(BF16), sts[j].astype(BF16)) for j in H]
    pv = [_dot(jnp.concatenate([a_kk[j], a_rk[j]], axis=0).astype(BF16), v[j].astype(BF16)) for j in H]
    rhs = [ps[j][:CH] + pv[j][:CH] for j in H]
    o0 = [ps[j][CH:] + pv[j][CH:] for j in H]
    u = _solve_all(tuple(a_kb), tuple(rhs), cm)
    o = [o0[j] - _pdot(_dot, a_rb[j], u[j], P_APPLY) for j in H]
    st2 = [sts[j] * ec[j] + _dot_tn(jnp.concatenate([v[j], u[j]], axis=0).astype(BF16),
                                    jnp.concatenate([khe[j], -bhe[j]], axis=0).astype(BF16)) for j in H]
    return jnp.concatenate(o, axis=1), tuple(st2)


def _chunk_consts():
    t = np.arange(CH)[:, None]
    s = np.arange(CH)[None, :]
    mats = [(t >= s), (t > s), (t == s)]
    for lvl in range(6):
        sz = 1 << lvl
        mats.append(((t // sz) % 2 == 1) & ((s // sz) == (t // sz) - 1))
    mats.append(np.zeros((CH, CH), bool))
    return np.stack(mats).astype(np.float32)


def _adamw(w, g, m, v):
    m = ADAM_B1 * m + (1.0 - ADAM_B1) * g
    v = ADAM_B2 * v + (1.0 - ADAM_B2) * (g * g)
    m_hat = m / (1.0 - ADAM_B1 ** ADAM_STEP)
    v_hat = v / (1.0 - ADAM_B2 ** ADAM_STEP)
    delta = -ADAM_LR * (m_hat / (jnp.sqrt(v_hat) + ADAM_EPS) + ADAM_WD * w)
    return delta, m, v


CT = 128
RB = 64
WIN = RB + 32


def _conv_fwd(pm, ck, T, S):
    nb = T // S

    def body(val_ref, gate_ref, ck_ref, out_ref, ubuf):
        ubuf[0:32, :] = jnp.zeros((32, CT), F32)
        ubuf[32:, :] = val_ref[...] * _sigmoid(gate_ref[...])

        def blk(rb, carry):
            base = pl.multiple_of(rb * RB, RB)
            win = ubuf[pl.ds(base, WIN), :]
            acc = jnp.zeros((RB, CT), F32)
            for j in range(CW):
                acc = acc + ck_ref[j:j + 1, :] * pltpu.roll(win, (WIN - (2 + j)) % WIN, 0)[0:RB, :]
            out_ref[pl.ds(base, RB), :] = acc
            return carry

        lax.fori_loop(0, S // RB, blk, 0)

    return pl.pallas_call(
        body, name="conv_fwd", grid=(D // CT, nb),
        in_specs=[pl.BlockSpec((S, CT), lambda ct, b: (b, ct)),
                  pl.BlockSpec((S, CT), lambda ct, b: (b, D // CT + ct)),
                  pl.BlockSpec((32, CT), lambda ct, b: (0, ct))],
        out_specs=pl.BlockSpec((S, CT), lambda ct, b: (b, ct)),
        out_shape=jax.ShapeDtypeStruct((T, D), F32),
        scratch_shapes=[pltpu.VMEM((S + 32, CT), F32)],
        compiler_params=_cparams(("parallel", "arbitrary")),
    )(pm, pm, ck)


def _conv_bwd(pm, duc, ck, T, S):
    nb = T // S

    def body(val_ref, gate_ref, duc_ref, ck_ref, dval_ref, dgate_ref, dck_ref, ubuf, dbuf, acc):
        b = pl.program_id(1)
        ubuf[0:32, :] = jnp.zeros((32, CT), F32)
        ubuf[32:, :] = val_ref[...] * _sigmoid(gate_ref[...])
        dbuf[0:S, :] = duc_ref[...]
        dbuf[S:, :] = jnp.zeros((32, CT), F32)
        acc[...] = jnp.zeros_like(acc)

        def blk(rb, carry):
            base = pl.multiple_of(rb * RB, RB)
            uwin = ubuf[pl.ds(base, WIN), :]
            dwin = dbuf[pl.ds(base, WIN), :]
            dblk = dwin[0:RB, :]
            du = jnp.zeros((RB, CT), F32)
            for j in range(CW):
                du = du + ck_ref[j:j + 1, :] * pltpu.roll(dwin, (WIN - (CW - 1 - j)) % WIN, 0)[0:RB, :]
                ush = pltpu.roll(uwin, (WIN - (2 + j)) % WIN, 0)[0:RB, :]
                acc[j] += jnp.sum((dblk * ush).reshape(RB // 8, 8, CT), axis=0)
            val = val_ref[pl.ds(base, RB), :]
            sg = _sigmoid(gate_ref[pl.ds(base, RB), :])
            dval_ref[pl.ds(base, RB), :] = (du * sg).astype(BF16)
            dgate_ref[pl.ds(base, RB), :] = (du * val * sg * (1.0 - sg)).astype(BF16)
            return carry

        lax.fori_loop(0, S // RB, blk, 0)

        @pl.when(b == 0)
        def _():
            dck_ref[...] = jnp.zeros_like(dck_ref)
        for j in range(CW):
            dck_ref[j:j + 1, :] += jnp.sum(acc[j], axis=0, keepdims=True)

    return pl.pallas_call(
        body, name="conv_bwd", grid=(D // CT, nb),
        in_specs=[pl.BlockSpec((S, CT), lambda ct, b: (b, ct)),
                  pl.BlockSpec((S, CT), lambda ct, b: (b, D // CT + ct)),
                  pl.BlockSpec((S, CT), lambda ct, b: (b, ct)),
                  pl.BlockSpec((32, CT), lambda ct, b: (0, ct))],
        out_specs=[pl.BlockSpec((S, CT), lambda ct, b: (b, ct)),
                   pl.BlockSpec((S, CT), lambda ct, b: (b, ct)),
                   pl.BlockSpec((32, CT), lambda ct, b: (0, ct))],
        out_shape=[jax.ShapeDtypeStruct((T, D), BF16), jax.ShapeDtypeStruct((T, D), BF16),
                   jax.ShapeDtypeStruct((32, D), F32)],
        scratch_shapes=[pltpu.VMEM((S + 32, CT), F32), pltpu.VMEM((S + 32, CT), F32), pltpu.VMEM((32, 8, CT), F32)],
        compiler_params=_cparams(("parallel", "arbitrary")),
    )(pm, pm, duc, ck)


HB = 16


def _scan_fwd(ins, cm, nb, S):
    nc = S // CH
    blk = pl.BlockSpec((CH, HB * HN), lambda b, g, i: (b * nc + i, g))
    hblk = pl.BlockSpec((None, HB, None, HN, HN), lambda b, g, i: (b, g, i, 0, 0))

    def body(r_ref, lw_ref, k_ref, v_ref, kk_ref, b_ref, cm_ref, o_ref, hs_ref, st):
        @pl.when(pl.program_id(2) == 0)
        def _():
            st[...] = jnp.zeros_like(st)
        s0 = [st[j] for j in range(HB)]
        for j in range(HB):
            hs_ref[j] = s0[j]
        o, s1 = _chunk(s0, r_ref[...], lw_ref[...], k_ref[...], v_ref[...], kk_ref[...], b_ref[...], cm_ref[...])
        o_ref[...] = o
        for j in range(HB):
            st[j] = s1[j]

    return pl.pallas_call(
        body, name="scan_fwd", grid=(nb, NH // HB, nc),
        in_specs=[blk] * 6 + [pl.BlockSpec(cm.shape, lambda b, g, i: (0, 0, 0))],
        out_specs=[blk, hblk],
        out_shape=[jax.ShapeDtypeStruct((nb * S, D), F32), jax.ShapeDtypeStruct((nb, NH, nc, HN, HN), F32)],
        scratch_shapes=[pltpu.VMEM((HB, HN, HN), F32)],
        compiler_params=_cparams(("parallel", "parallel", "arbitrary")),
    )(*ins, cm)


def _scan_bwd(ins, hs, do, cm, nb, S):
    nc = S // CH
    blk = pl.BlockSpec((CH, HB * HN), lambda b, g, i: (b * nc + nc - 1 - i, g))
    hblk = pl.BlockSpec((None, HB, None, HN, HN), lambda b, g, i: (b, g, nc - 1 - i, 0, 0))

    def body(r_ref, lw_ref, k_ref, v_ref, kk_ref, b_ref, hs_ref, do_ref, cm_ref,
             dr_ref, dlw_ref, dk_ref, dv_ref, dkk_ref, db_ref, dst):
        @pl.when(pl.program_id(2) == 0)
        def _():
            dst[...] = jnp.zeros_like(dst)
        cmv = cm_ref[...]
        f = lambda s0, r, lw, k, v, kk, b: _chunk(s0, r, lw, k, v, kk, b, cmv)
        _, vjp = jax.vjp(f, [hs_ref[j] for j in range(HB)], r_ref[...], lw_ref[...], k_ref[...], v_ref[...],
                         kk_ref[...], b_ref[...])
        ds0, dr, dlw, dk, dv, dkk, db = vjp((do_ref[...], tuple(dst[j] for j in range(HB))))
        for j in range(HB):
            dst[j] = ds0[j]
        dr_ref[...] = dr
        dlw_ref[...] = dlw
        dk_ref[...] = dk
        dv_ref[...] = dv
        dkk_ref[...] = dkk
        db_ref[...] = db

    return pl.pallas_call(
        body, name="scan_bwd", grid=(nb, NH // HB, nc),
        in_specs=[blk] * 6 + [hblk, blk, pl.BlockSpec(cm.shape, lambda b, g, i: (0, 0, 0))],
        out_specs=[blk] * 6,
        out_shape=[jax.ShapeDtypeStruct((nb * S, D), F32)] * 6,
        scratch_shapes=[pltpu.VMEM((HB, HN, HN), F32)],
        compiler_params=_cparams(("parallel", "parallel", "arbitrary")),
    )(*ins, hs, do, cm)


def _ew(fn, name, ins, n_out, tm, out_dtype=F32):
    R, W = ins[0].shape[-2:]
    tm = min(tm, R)
    assert R % tm == 0

    def body(*refs):
        vals = fn(*[r[...] for r in refs[:len(ins)]])
        for r, v in zip(refs[len(ins):], vals):
            r[...] = v.astype(r.dtype)

    def spec(a):
        if a.ndim == 3:
            return pl.BlockSpec((a.shape[0], tm, W), lambda i: (0, i, 0))
        return pl.BlockSpec((tm, W), lambda i: (i, 0))

    return pl.pallas_call(
        body, name=name, grid=(R // tm,), in_specs=[spec(a) for a in ins],
        out_specs=[pl.BlockSpec((tm, W), lambda i: (i, 0))] * n_out,
        out_shape=[jax.ShapeDtypeStruct((R, W), out_dtype)] * n_out,
        compiler_params=_cparams(("parallel",)),
    )(*ins)


def _sum_slots(r):
    s = r[0]
    for j in range(1, r.shape[0]):
        s = s + r[j]
    return s


def _place():
    x, y, c = lax.axis_index("x"), lax.axis_index("y"), lax.axis_index("c")
    return x, y, c


def _flip(v, d):
    return 1 - v if d else v


CHIP_PEERS = ((1, 0), (0, 1), (1, 1))
DEV_PEERS = tuple((dx, dy, dc) for dx in (0, 1) for dy in (0, 1) for dc in (0, 1))[1:]


def _comm_call(name, ins, out_shapes, plan, n_rem, n_fwd=0):
    n_in = len(ins)

    def body(*refs):
        in_refs, out_refs = refs[:n_in], refs[n_in:n_in + len(out_shapes)]
        send_sems, recv_sems, loc_sems = refs[n_in + len(out_shapes):]
        loc, rem, *rest = plan(in_refs, out_refs, _place())
        fwd = rest[0] if rest else []
        assert len(rem) == n_rem and len(fwd) == n_fwd and len(loc) <= 2 * n_in, (name, len(loc), len(rem), len(fwd))

        def remote(i, s, d, peer):
            return pltpu.make_async_remote_copy(src_ref=s, dst_ref=d, send_sem=send_sems.at[i], recv_sem=recv_sems.at[i],
                                                device_id=peer, device_id_type=MESH)

        copies = [pltpu.make_async_copy(s, d, loc_sems.at[i]) for i, (s, d) in enumerate(loc)]
        rcopies = [remote(i, s, d, peer) for i, (s, d, peer) in enumerate(rem)]
        for cp in copies + rcopies:
            cp.start()
        landed = set()
        fcopies = []
        for i, (s, d, peer, k) in enumerate(fwd):
            if k not in landed:
                rcopies[k].wait_recv()
                landed.add(k)
            fcopies.append(remote(n_rem + i, s, d, peer))
            fcopies[-1].start()
        for k, cp in enumerate(rcopies):
            if k not in landed:
                cp.wait_recv()
        for cp in rcopies + fcopies:
            cp.wait_send()
        for cp in fcopies:
            cp.wait_recv()
        for cp in copies:
            cp.wait()

    return pl.pallas_call(
        body, name=name, in_specs=[ANY] * n_in, out_specs=[ANY] * len(out_shapes), out_shape=out_shapes,
        scratch_shapes=[pltpu.SemaphoreType.DMA((n_rem + n_fwd,)), pltpu.SemaphoreType.DMA((n_rem + n_fwd,)),
                        pltpu.SemaphoreType.DMA((2 * n_in,))],
        compiler_params=pltpu.CompilerParams(has_side_effects=True),
    )(*ins)


def _gather_plan(n_big, in_refs, out_refs, place):
    x, y, c = place
    chip, dev = 2 * x + y, 4 * x + 2 * y + c
    sib = (x, y, 1 - c)
    loc = [(in_refs[0], out_refs[0].at[dev])] + [(s, d.at[chip]) for s, d in zip(in_refs[1 + n_big:], out_refs[1 + n_big:])]
    rem = [(in_refs[0], out_refs[0].at[dev], (_flip(x, dx), _flip(y, dy), _flip(c, dc))) for dx, dy, dc in DEV_PEERS]
    fwd = []
    for s, d in zip(in_refs[1:1 + n_big], out_refs[1:1 + n_big]):
        for dx, dy in CHIP_PEERS:
            px, py = _flip(x, dx), _flip(y, dy)
            fwd.append((d.at[2 * px + py, c], d.at[2 * px + py, c], sib, len(rem)))
            rem.append((s.at[c], d.at[chip, c], (px, py, c)))
    for s, d in zip(in_refs[1 + n_big:], out_refs[1 + n_big:]):
        rem += [(s, d.at[chip], (_flip(x, dx), _flip(y, dy), c)) for dx, dy in CHIP_PEERS]
    return loc, rem, fwd


def _halve_plan(in_refs, out_refs, place):
    x, y, c = place
    return [], [(s.at[1 - c], d, (x, y, 1 - c)) for s, d in zip(in_refs, out_refs)]


def _join_plan(in_refs, out_refs, place):
    x, y, c = place
    return [], [(s, d, (x, y, 1 - c)) for s, d in zip(in_refs, out_refs)]


def _scatter_plan(n_all, in_refs, out_refs, place):
    x, y, c = place
    chip, dev = 2 * x + y, 4 * x + 2 * y + c
    loc, rem = [], []
    for s, d in zip(in_refs[:n_all], out_refs[:n_all]):
        loc.append((s.at[dev], d.at[dev]))
        for dx, dy, dc in DEV_PEERS:
            px, py, pc = _flip(x, dx), _flip(y, dy), _flip(c, dc)
            rem.append((s.at[4 * px + 2 * py + pc], d.at[dev], (px, py, pc)))
    for s, d in zip(in_refs[n_all:], out_refs[n_all:]):
        for dx, dy in CHIP_PEERS:
            px, py = _flip(x, dx), _flip(y, dy)
            rem.append((s.at[2 * px + py], d.at[chip], (px, py, c)))
    return loc, rem


def _bshape(a, nb):
    return a.reshape(nb, 1, a.shape[-1])


def _with_prev(cur, before, tiles_per_seq):
    first = pl.program_id(0) % tiles_per_seq == 0
    row0 = jnp.where(first, 0.0, before[before.shape[0] - 1:, :])
    rid = lax.broadcasted_iota(jnp.int32, cur.shape, 0)
    return jnp.where(rid == 0, row0, pltpu.roll(cur, 1, 0))


def _with_next(cur, after, tiles_per_seq):
    last = pl.program_id(0) % tiles_per_seq == tiles_per_seq - 1
    n = cur.shape[0]
    row_n = jnp.where(last, 0.0, after[0:1, :])
    rid = lax.broadcasted_iota(jnp.int32, cur.shape, 0)
    return jnp.where(rid == n - 1, row_n, pltpu.roll(cur, n - 1, 0))


def _local_step(x2d, tgt, mod, wmain, wlora, late_w, ck, w2, a2, small, nb, S):
    T = nb * S
    shift, scale, gate = (_bshape(mod[:, i * D:(i + 1) * D], nb) for i in range(3))
    G = jnp.asarray(np.arange(128)[:, None] == np.arange(D)[None, :] // HN, dtype=BF16)
    cm = jnp.asarray(_chunk_consts())
    ckp = jnp.pad(ck, ((0, 1), (0, 0)))
    zpad = jnp.zeros((64, D), F32)
    w2p = jnp.concatenate([w2, zpad], axis=0)
    a2p = jnp.concatenate([zpad, a2], axis=0)
    mu = small["rwkv_mu"]
    mu_r, mu_k, mu_v, mu_l = mu[:, 0:D], mu[:, D:2 * D], mu[:, 2 * D:3 * D], mu[:, 3 * D:]
    g4 = [mu_r, mu_k, mu_v, mu_l, small["rwkv_w0"], w2p, small["rwkv_a0"], a2p, small["rwkv_k_k"], small["rwkv_k_a"], G]
    g5 = [small["rwkv_gn_g"], small["rwkv_gn_b"], small["rwkv_r_k"], G]
    g3 = [small["conv_b"], small["conv_ln_g"], small["conv_ln_b"]]

    (h,), _, _ = _rows(lambda r, b, g: ([_s1(r[0], g[0], b[0], b[1])], [], []), "pre_fwd", T, S, 256,
                       [(x2d, D, 0)], [scale, shift], [small["norm_g"]], [(D, BF16)], [], [])
    if len(late_w) == 3:
        pm = _matmul(h, wmain, "nn", "proj_main", min(T, 1024), 1024, D)
        wco, wro, wo = late_w
    else:
        pm, *landed = _matmul(h, wmain, "nn", "proj_main", min(T, 1024), 1024, D, ride=late_w[0])
        wco, wro, wo = late_w[1](landed)
    plo = _matmul(h, wlora, "nn", "proj_lora", 512, LORA, D)
    uc = _conv_fwd(pm, ckp, T, S)
    (uo,), _, _ = _rows(lambda r, b, g: ([_s3(r[0], r[1], *g)], [], []), "conv_post_fwd", T, S, 256,
                        [(uc, D, 0), (pm, D, 2)], [], g3, [(D, BF16)], [], [])
    yc = _matmul(uo, wco, "nn", "conv_out", 512, 1024, D)
    rows4 = [(pm, D, 3), (pm, D, 4), (pm, D, 5), (plo, LORA, 0),
             (pm, D, 3, "prev"), (pm, D, 4, "prev"), (pm, D, 5, "prev"), (plo, LORA, 0, "prev")]
    tps4 = S // 128

    def shifted4(r, tps=tps4):
        return list(r[:4]) + [_with_prev(r[i], r[4 + i], tps) for i in range(4)]

    sc_in, _, _ = _rows(lambda r, b, g: (list(_s4(*shifted4(r, S // 256), *g)), [], []), "rwkv_pre_fwd", T, S, 256,
                        rows4, [], g4, [(D, F32)] * 6, [], [])
    o, hs = _scan_fwd(sc_in, cm, nb, S)
    rows5 = [(o, D, 0), (sc_in[0], D, 0), (sc_in[2], D, 0), (sc_in[3], D, 0), (pm, D, 6)]
    (o2,), _, _ = _rows(lambda r, b, g: ([_s5(*r, *g)], [], []), "rwkv_post_fwd", T, S, 256,
                        rows5, [], g5, [(D, BF16)], [], [])
    yr = _matmul(o2, wro, "nn", "rwkv_out", 512, 1024, D)
    rows6 = [(yc, D, 0), (yr, D, 0), (pm, D, 7), (pm, D, 8)]
    (m,), _, _ = _rows(lambda r, b, g: ([_s6(*r)], [], []), "merge_fwd", T, S, 256, rows6, [], [], [(D, BF16)], [], [])
    out = _matmul(m, wo, "nn", "out_proj", 512, 1024, D)

    def head(r, b, g):
        loss, (dx, dout, dgate, dfg) = jax.value_and_grad(_s7, argnums=(0, 1, 3, 4))(r[0], r[1], r[2], b[0], g[0])
        return [dx, dout], [dgate], [dfg, jnp.full((1, 128), loss, F32)]

    (dx_res, dout), (dgate,), (d_final_g, loss_v) = _rows(
        head, "head", T, S, 256, [(x2d, D, 0), (out, D, 0), (tgt, D, 0)], [gate], [small["final_g"]],
        [(D, F32), (D, BF16)], [D], [(1, D), (1, 128)])

    d_wo = _matmul(m, dout, "tn", "d_w_out", 512, 1024, min(T, 2048))
    dm = _matmul(dout, wo, "nt", "d_merge", 512, 1024, D)

    def merge_bwd(r, b, g):
        _, vjp = jax.vjp(_s6, *r[:4])
        dyc, dyr, dgc, dgr = vjp(r[4])
        return [dyc, dyr, dgc, dgr], [], []

    (dyc, dyr, dgc, dgr), _, _ = _rows(merge_bwd, "merge_bwd", T, S, 256, rows6 + [(dm, D, 0)], [], [],
                                       [(D, BF16), (D, BF16), (D, BF16), (D, BF16)], [], [])
    d_wco = _matmul(uo, dyc, "tn", "d_w_conv_out", 512, 1024, min(T, 2048))
    d_wro = _matmul(o2, dyr, "tn", "d_w_rwkv_out", 512, 1024, min(T, 2048))
    duo = _matmul(dyc, wco, "nt", "d_conv_act", 512, 1024, D)
    do2 = _matmul(dyr, wro, "nt", "d_rwkv_act", 512, 1024, D)

    def conv_post_bwd(r, b, g):
        _, vjp = jax.vjp(_s3, r[0], r[1], *g)
        duc, dog, dcb, dlg, dlb = vjp(r[2])
        return [duc, dog], [], [dcb, dlg, dlb]

    (duc, dcog), _, (d_cb, d_lg, d_lb) = _rows(conv_post_bwd, "conv_post_bwd", T, S, 256,
                                               [(uc, D, 0), (pm, D, 2), (duo, D, 0)], [], g3,
                                               [(D, F32), (D, BF16)], [], [(1, D)] * 3)
    dval, dgt, d_ckp = _conv_bwd(pm, duc, ckp, T, S)

    def rwkv_post_bwd(r, b, g):
        _, vjp = jax.vjp(lambda *z: _s5(*z, g[3]), *r[:5], *g[:3])
        res = vjp(r[5])
        return list(res[:5]), [], list(res[5:8])

    (do, dr_b, dk_b, dv_b, drog), _, (d_gg, d_gb, d_rk) = _rows(
        rwkv_post_bwd, "rwkv_post_bwd", T, S, 256, rows5 + [(do2, D, 0)], [], g5,
        [(D, F32)] * 4 + [(D, BF16)], [], [(1, D)] * 3)
    dsc = _scan_bwd(sc_in, hs, do, cm, nb, S)

    def rwkv_pre_bwd(r, b, g):
        _, vjp = jax.vjp(lambda *z: _s4(*z, g[10]), *shifted4(r), *g[:10])
        ct = (r[8] + r[14], r[9], r[10] + r[15], r[11] + r[16], r[12], r[13])
        res = vjp(ct)
        return list(res[:8]), [], list(res[8:18])

    rows4b = rows4 + [(a, D, 0) for a in dsc] + [(dr_b, D, 0), (dk_b, D, 0), (dv_b, D, 0)]
    gshapes = [(1, D), (1, D), (1, D), (1, LORA), (1, D), (LORA, D), (1, D), (LORA, D), (1, D), (1, D)]
    dts, _, gts = _rows(rwkv_pre_bwd, "rwkv_pre_bwd", T, S, 128, rows4b, [], g4,
                        [(D, BF16)] * 3 + [(LORA, BF16)] + [(D, BF16)] * 3 + [(LORA, BF16)], [], gshapes)
    dr0, dk0, dv0, dl0, dpr, dpk, dpv, dpl = dts
    d_mu_r, d_mu_k, d_mu_v, d_mu_l, d_w0, d_w2p, d_a0, d_a2p, d_kk, d_ka = gts

    def assemble(r, b, g):
        r = [z.astype(F32) for z in r]
        sh = [_with_next(r[10 + i], r[14 + i], tps4) for i in range(4)]
        main = jnp.concatenate([r[0], r[1], r[2], r[3] + sh[0], r[4] + sh[1], r[5] + sh[2], r[6], r[7], r[8]], axis=1)
        return [main, r[9] + sh[3]], [], []

    rows_a = [(dval, D, 0), (dgt, D, 0), (dcog, D, 0), (dr0, D, 0), (dk0, D, 0), (dv0, D, 0), (drog, D, 0), (dgc, D, 0),
              (dgr, D, 0), (dl0, LORA, 0), (dpr, D, 0), (dpk, D, 0), (dpv, D, 0), (dpl, LORA, 0),
              (dpr, D, 0, "next"), (dpk, D, 0, "next"), (dpv, D, 0, "next"), (dpl, LORA, 0, "next")]
    (dpm, dplo), _, _ = _rows(assemble, "assemble_dp", T, S, 128, rows_a, [], [], [(DMAIN, BF16), (LORA, BF16)], [], [])
    d_wmain = _matmul(h, dpm, "tn", "d_w_main", 512, 1024, min(T, 2048))
    d_wlora = _matmul(h, dplo, "tn", "d_w_lora", 512, LORA, min(T, 2048))
    RW = D // NCHIP
    hv = [d_wmain.reshape(2, D // 2, DMAIN), d_wlora.reshape(2, D // 2, LORA)]
    hv += [g.reshape(NCHIP, 2, RW // 2, D).transpose(1, 0, 2, 3).reshape(2, NCHIP * RW // 2, D) for g in (d_wco, d_wro, d_wo)]
    hv += [g.reshape(-1, NCHIP, 2, RW // 2).transpose(2, 1, 0, 3).reshape(2, -1, RW // 2)
           for g in (d_ckp[:CW], d_w2p[:64], d_a2p[64:])]
    halve = (hv, [jax.ShapeDtypeStruct(t.shape[1:], F32) for t in hv], _halve_plan, len(hv))
    dh_m, *got_h = _matmul(dpm, wmain, "nt", "d_h_main", 512, 1024, 3072, ride=halve)
    dh_l = _matmul(dplo, wlora, "nt", "d_h_lora", 512, 1024, LORA)

    def pre_bwd(r, b, g):
        _, vjp = jax.vjp(_s1, r[0], g[0], b[0], b[1])
        dx, dg, dscale, dshift = vjp(r[1] + r[2])
        return [dx + r[3]], [dscale, dshift], [dg]

    (gx,), (dscale, dshift), (d_ng,) = _rows(pre_bwd, "pre_bwd", T, S, 256,
                                             [(x2d, D, 0), (dh_m, D, 0), (dh_l, D, 0), (dx_res, D, 0)],
                                             [scale, shift], [small["norm_g"]], [(D, F32)], [D, D], [(1, D)])
    dmod = jnp.concatenate([dshift, dscale, dgate], axis=-1).reshape(nb, 3 * D)
    d_small = {"norm_g": d_ng, "conv_b": d_cb, "conv_ln_g": d_lg, "conv_ln_b": d_lb,
               "rwkv_mu": jnp.concatenate([d_mu_r, d_mu_k, d_mu_v, d_mu_l], axis=1),
               "rwkv_w0": d_w0, "rwkv_a0": d_a0, "rwkv_k_k": d_kk, "rwkv_k_a": d_ka, "rwkv_r_k": d_rk,
               "rwkv_gn_g": d_gg, "rwkv_gn_b": d_gb, "final_g": d_final_g}
    return loss_v[0, 0], gx, dmod, hv, got_h, d_small


def _step(a):
    nb, S, _ = a["x"].shape
    T = nb * S
    x_i, y_i, c_i = _place()
    chip = 2 * x_i + y_i
    w_in = a["w_in"][0]
    WS = w_in.shape[1]
    small_w = {n: a[n].reshape(1, sz) for n, sz in SMALL}

    def halves(t):
        return t.reshape(2, t.shape[0] // 2, t.shape[1])

    g_ins = [a["c"], halves(w_in.astype(BF16)), a["conv_k"][0], a["rwkv_w2"][0], a["rwkv_a2"][0]]
    g_out = [jax.ShapeDtypeStruct((NDEV,) + g_ins[0].shape, F32)]
    g_out += [jax.ShapeDtypeStruct((NCHIP,) + t.shape, t.dtype) for t in g_ins[1:]]
    c_all, win_g, ck_g, w2_g, a2_g = _comm_call(
        "gather_weights", g_ins, g_out, functools.partial(_gather_plan, 1), 7 + 3 * 4, 3)
    c_all = c_all.reshape(NDEV * nb, D)
    win_g = lax.dynamic_update_index_in_dim(win_g, g_ins[1], chip, 0).reshape(NCHIP, D, WS)
    late = [a[n][0].astype(BF16) for n in ("w_conv_out", "w_rwkv_out", "w_out")]

    def late_plan(in_refs, out_refs, place):
        x, y, c = place
        return [], [(s, d.at[2 * x + y], (_flip(x, dx), _flip(y, dy), c))
                    for s, d in zip(in_refs, out_refs) for dx, dy in CHIP_PEERS]

    def late_finish(landed):
        return [lax.dynamic_update_index_in_dim(g, own, chip, 0).reshape(D, D) for g, own in zip(landed, late)]

    late_w = ((late, [jax.ShapeDtypeStruct((NCHIP,) + t.shape, BF16) for t in late], late_plan, 9), late_finish)
    j0, o0 = divmod(6 * D, WS)
    j1, o1 = divmod(6 * D + LORA, WS)
    assert j0 == j1, "the lora columns sit inside one shard"
    wmain = jnp.concatenate([win_g[j] for j in range(j0)] + [win_g[j0][:, :o0], win_g[j0][:, o1:]]
                            + [win_g[j] for j in range(j0 + 1, NCHIP)], axis=1)
    wlora = win_g[j0][:, o0:o1]
    ck = jnp.concatenate([ck_g[j] for j in range(NCHIP)], axis=1)
    w2 = jnp.concatenate([w2_g[j] for j in range(NCHIP)], axis=1)
    a2 = jnp.concatenate([a2_g[j] for j in range(NCHIP)], axis=1)

    ada_w = a["ada_w"][0]
    MW = ada_w.shape[1]
    ada_b_loc = lax.dynamic_slice(a["ada_b"], (0, chip * MW), (1, MW))

    def mod_body(c_ref, w_ref, b_ref, o_ref):
        o_ref[...] = _dot(_silu(c_ref[...]), w_ref[...], HI) + b_ref[...]

    modp = pl.pallas_call(mod_body, name="ada_mod", out_shape=jax.ShapeDtypeStruct((NDEV * nb, MW), F32),
                          compiler_params=_cparams())(c_all, ada_w, ada_b_loc)
    (mod_g,) = _comm_call("scatter_mod", [modp.reshape(NDEV, nb, MW)],
                          [jax.ShapeDtypeStruct((NDEV, nb, MW), F32)],
                          functools.partial(_scatter_plan, 1), 7)
    mod = mod_g.reshape(NCHIP, 2, nb, MW)
    mod = mod[:, 0].transpose(1, 0, 2).reshape(nb, NCHIP * MW)

    loss_p, gx, dmod, hv, got_h, d_small = _local_step(
        a["x"].reshape(T, D), a["loss_target"].reshape(T, D), mod, wmain, wlora, late_w, ck, w2, a2, small_w, nb, S)
    loss = lax.psum(loss_p, ("x", "y", "c"))

    d_small["ada_b"] = _colsum(dmod)
    small_vec = jnp.concatenate([d_small[n] for n, _ in SMALL], axis=1)
    dmod_s = dmod.reshape(nb, NCHIP, MW).transpose(1, 0, 2)
    dmod_s = jnp.repeat(dmod_s, 2, axis=0)
    small_s = jnp.broadcast_to(small_vec[None], (NDEV, 1, NSMALL))
    RW = D // NCHIP

    def own_half_plus(both, q):
        return [jnp.where(lax.axis_index("c") == 0, both[0], both[1]) + q]

    chip_part = [_ew(own_half_plus, "chip_sum_%d" % i, [hv[i], got_h[i]], 1, 128, BF16)[0] for i in range(len(hv))]
    d_win_h = jnp.concatenate([chip_part[0][:, :6 * D], chip_part[1], chip_part[0][:, 6 * D:]], axis=1)
    sh_s = [jnp.stack([d_win_h[:, j * WS:(j + 1) * WS] for j in range(NCHIP)])]
    sh_s += [t.reshape(NCHIP, RW // 2, D) for t in chip_part[2:5]]
    sh_s += [t.reshape(NCHIP, -1, RW // 2) for t in chip_part[5:]]
    s_ins = [dmod_s, small_s] + sh_s
    s_out = [jax.ShapeDtypeStruct(t.shape, t.dtype) for t in s_ins]
    got = _comm_call("scatter_grads", s_ins, s_out, functools.partial(_scatter_plan, 2), 14 + 3 * len(sh_s))
    dmod_all, small_all = got[0].reshape(NDEV * nb, MW), got[1].reshape(NDEV, NSMALL)
    def shard_sum(recv, sent):
        chip_i = 2 * lax.axis_index("x") + lax.axis_index("y")
        s = None
        for j in range(NCHIP):
            t = jnp.where(chip_i == j, sent[j], recv[j]).astype(F32)
            s = t if s is None else s + t
        return [s]

    fin = [_ew(shard_sum, "shard_sum_%d" % i, [t, sh_s[i]], 1, 128)[0] for i, t in enumerate(got[2:])]
    oth = _comm_call("join_halves", fin, [jax.ShapeDtypeStruct(t.shape, F32) for t in fin], _join_plan, len(fin))

    outs = {}

    def upd_halves(name, mine, other):
        shp = a[name].shape
        R, W = 2 * mine.shape[0], mine.shape[1]
        tm = 128
        nh = R // 2 // tm

        def body(w_ref, m_ref, v_ref, f_ref, o_ref, g_ref, d_ref, m2_ref, v2_ref):
            g = jnp.where(pl.program_id(0) // nh == lax.axis_index("c"), f_ref[...], o_ref[...])
            g_ref[...] = g
            d_ref[...], m2_ref[...], v2_ref[...] = _adamw(w_ref[...], g, m_ref[...], v_ref[...])

        full = pl.BlockSpec((None, tm, W), lambda i: (0, i, 0))
        half = pl.BlockSpec((tm, W), lambda i: (i % nh, 0))
        assert shp == (1, R, W)
        outs[name] = pl.pallas_call(
            body, name="adamw_" + name, grid=(R // tm,), in_specs=[full] * 3 + [half] * 2, out_specs=[full] * 4,
            out_shape=[jax.ShapeDtypeStruct(shp, F32)] * 4, compiler_params=_cparams(("parallel",)),
        )(*[a[p + name] for p in ("", "m_", "v_")], mine, other)

    for name, f, o in zip(("w_in", "w_conv_out", "w_rwkv_out", "w_out"), fin, oth):
        upd_halves(name, f, o)

    def upd(name, g):
        shp = a[name].shape
        ins = [a[p + name].reshape(g.shape) for p in ("", "m_", "v_")]
        res = _ew(lambda w_, m_, v_, g_: [g_, *_adamw(w_, g_, m_, v_)], "adamw_" + name, [*ins, g], 4, 128)
        outs[name] = [r.reshape(shp) for r in res]

    for name, f, o in zip(("conv_k", "rwkv_w2", "rwkv_a2"), fin[4:], oth[4:]):
        both = jnp.where(c_i == 0, jnp.stack([f, o]), jnp.stack([o, f]))
        upd(name, both.transpose(1, 0, 2).reshape(-1, RW))

    def adaw_body(c_ref, dm_ref, w_ref, m_ref, v_ref, g_ref, d_ref, m2_ref, v2_ref):
        g = _dot_tn(_silu(c_ref[...]), dm_ref[...], HI)
        g_ref[...] = g
        d_ref[...], m2_ref[...], v2_ref[...] = _adamw(w_ref[...], g, m_ref[...], v_ref[...])

    res = pl.pallas_call(adaw_body, name="adamw_ada_w", out_shape=[jax.ShapeDtypeStruct((D, MW), F32)] * 4,
                         compiler_params=_cparams())(c_all, dmod_all, ada_w, a["m_ada_w"][0], a["v_ada_w"][0])
    outs["ada_w"] = [r.reshape(a["ada_w"].shape) for r in res]

    wv, mv, vv = (jnp.concatenate([a[p + n].reshape(1, sz) for n, sz in SMALL], axis=1) for p in ("", "m_", "v_"))
    def small_fn(w_, m_, v_, gs):
        g = _sum_slots(gs)
        return [g, *_adamw(w_, g, m_, v_)]

    res = _ew(small_fn, "adamw_small", [wv, mv, vv, small_all.reshape(NDEV, 1, NSMALL)], 4, 8)
    off = 0
    for n, sz in SMALL:
        outs[n] = [r[:, off:off + sz].reshape(a[n].shape) for r in res]
        off += sz

    return (loss, gx.reshape(nb, S, D), *[outs[n][0] for n in WEIGHTS], *[outs[n][1] for n in WEIGHTS],
            *[outs[n][2] for n in WEIGHTS], *[outs[n][3] for n in WEIGHTS])


def _colsum(dmod):
    def body(d_ref, o_ref):
        o_ref[...] = jnp.sum(d_ref[...], axis=0, keepdims=True)
    return pl.pallas_call(body, name="ada_b_rowsum", out_shape=jax.ShapeDtypeStruct((1, dmod.shape[1]), F32),
                          compiler_params=_cparams())(dmod)


def kernel(x, c, ada_w, ada_b, norm_g, w_in, conv_k, conv_b, conv_ln_g, conv_ln_b, w_conv_out, rwkv_mu, rwkv_w0, rwkv_w2, rwkv_a0, rwkv_a2, rwkv_k_k, rwkv_k_a, rwkv_r_k, rwkv_gn_g, rwkv_gn_b, w_rwkv_out, w_out, final_g, loss_target, m_ada_w, m_ada_b, m_norm_g, m_w_in, m_conv_k, m_conv_b, m_conv_ln_g, m_conv_ln_b, m_w_conv_out, m_rwkv_mu, m_rwkv_w0, m_rwkv_w2, m_rwkv_a0, m_rwkv_a2, m_rwkv_k_k, m_rwkv_k_a, m_rwkv_r_k, m_rwkv_gn_g, m_rwkv_gn_b, m_w_rwkv_out, m_w_out, m_final_g, v_ada_w, v_ada_b, v_norm_g, v_w_in, v_conv_k, v_conv_b, v_conv_ln_g, v_conv_ln_b, v_w_conv_out, v_rwkv_mu, v_rwkv_w0, v_rwkv_w2, v_rwkv_a0, v_rwkv_a2, v_rwkv_k_k, v_rwkv_k_a, v_rwkv_r_k, v_rwkv_gn_g, v_rwkv_gn_b, v_w_rwkv_out, v_w_out, v_final_g):
    return _step(dict(locals()))
```

```python
import functools

import numpy as np
import jax
import jax.numpy as jnp
from jax import lax
from jax.experimental import pallas as pl
from jax.experimental.pallas import tpu as pltpu

F32 = jnp.float32
BF16 = jnp.bfloat16
HI = lax.Precision.HIGHEST
MESH = pl.DeviceIdType.MESH
ANY = pl.BlockSpec(memory_space=pl.ANY)

D = 1024
NH = 16
HN = 64
LORA = 128
DMAIN = 9 * D
CH = 64
CW = 31
NCHIP = 4
NDEV = 8
VMEM_LIMIT = 56 * 1024 * 1024

RMS_EPS = 1e-6
LN_EPS = 1e-5
GN_EPS = 64e-5
L2_EPS = 1e-12
ADAM_LR = 0.001
ADAM_B1 = 0.9
ADAM_B2 = 0.999
ADAM_EPS = 1e-08
ADAM_WD = 0.01
ADAM_STEP = 10

SMALL = (("ada_b", 3072), ("norm_g", 1024), ("conv_b", 1024), ("conv_ln_g", 1024), ("conv_ln_b", 1024),
         ("rwkv_mu", 3200), ("rwkv_w0", 1024), ("rwkv_a0", 1024), ("rwkv_k_k", 1024), ("rwkv_k_a", 1024),
         ("rwkv_r_k", 1024), ("rwkv_gn_g", 1024), ("rwkv_gn_b", 1024), ("final_g", 1024))
NSMALL = sum(n for _, n in SMALL)

WEIGHTS = ['ada_w', 'ada_b', 'norm_g', 'w_in', 'conv_k', 'conv_b', 'conv_ln_g', 'conv_ln_b', 'w_conv_out', 'rwkv_mu',
           'rwkv_w0', 'rwkv_w2', 'rwkv_a0', 'rwkv_a2', 'rwkv_k_k', 'rwkv_k_a', 'rwkv_r_k', 'rwkv_gn_g', 'rwkv_gn_b',
           'w_rwkv_out', 'w_out', 'final_g']


def _cparams(sem=None, **kw):
    if sem is not None:
        kw["dimension_semantics"] = sem
    return pltpu.CompilerParams(vmem_limit_bytes=VMEM_LIMIT, **kw)


def _dot(a, b, prec=None):
    return jnp.dot(a, b, preferred_element_type=F32, precision=prec)


def _dot_nt(a, b, prec=None):
    return lax.dot_general(a, b, (((1,), (1,)), ((), ())), preferred_element_type=F32, precision=prec)


def _dot_tn(a, b, prec=None):
    return lax.dot_general(a, b, (((0,), (0,)), ((), ())), preferred_element_type=F32, precision=prec)


def _pdot(f, a, b, p):
    if p == "hi":
        return f(a, b, HI)
    ah, bh = a.astype(BF16), b.astype(BF16)
    if p == "bf":
        return f(ah, bh)
    al, bl = (a - ah.astype(F32)).astype(BF16), (b - bh.astype(F32)).astype(BF16)
    return f(ah, bh) + (f(ah, bl) + f(al, bh))


P_SCORE = "b3"
P_INV = "bf"
P_APPLY = "bf"


def _sigmoid(z):
    return 1.0 / (1.0 + jnp.exp(-z))


def _silu(z):
    return z * _sigmoid(z)


def _matmul(a, b, mode, name, tm, tn, tk, ride=None):
    if mode == "nn":
        (M, K), N = a.shape, b.shape[1]
        a_spec = pl.BlockSpec((tm, tk), lambda j, i, k: (i, k))
        b_spec = pl.BlockSpec((tk, tn), lambda j, i, k: (k, j))
        f = _dot
    elif mode == "nt":
        (M, K), N = a.shape, b.shape[0]
        a_spec = pl.BlockSpec((tm, tk), lambda j, i, k: (i, k))
        b_spec = pl.BlockSpec((tn, tk), lambda j, i, k: (j, k))
        f = _dot_nt
    else:
        (K, M), N = a.shape, b.shape[1]
        a_spec = pl.BlockSpec((tk, tm), lambda j, i, k: (k, i))
        b_spec = pl.BlockSpec((tk, tn), lambda j, i, k: (k, j))
        f = _dot_tn
    assert M % tm == 0 and N % tn == 0 and K % tk == 0, (name, M, N, K)

    grid = (N // tn, M // tm, K // tk)
    o_spec = pl.BlockSpec((tm, tn), lambda j, i, k: (i, j))
    o_shape = jax.ShapeDtypeStruct((M, N), F32)

    def step(a_ref, b_ref, o_ref):
        @pl.when(pl.program_id(2) == 0)
        def _():
            o_ref[...] = jnp.zeros_like(o_ref)
        o_ref[...] += f(a_ref[...], b_ref[...])

    if ride is None:
        return pl.pallas_call(
            step, name=name, grid=grid, in_specs=[a_spec, b_spec], out_specs=o_spec, out_shape=o_shape,
            compiler_params=_cparams(("parallel", "parallel", "arbitrary")),
        )(a, b)

    r_ins, r_shapes, plan, n_rem = ride
    n_ri, n_ro = len(r_ins), len(r_shapes)

    def body(a_ref, b_ref, *rest):
        r_in, o_ref, r_out = rest[:n_ri], rest[n_ri], rest[n_ri + 1:n_ri + 1 + n_ro]
        send_sems, recv_sems = rest[n_ri + 1 + n_ro:]
        loc, rem = plan(r_in, r_out, _place())
        assert not loc and len(rem) == n_rem, (name, len(loc), len(rem))
        copies = [pltpu.make_async_remote_copy(src_ref=s, dst_ref=d, send_sem=send_sems.at[i], recv_sem=recv_sems.at[i],
                                               device_id=peer, device_id_type=MESH) for i, (s, d, peer) in enumerate(rem)]
        pid = [pl.program_id(ax) for ax in range(3)]

        @pl.when((pid[0] == 0) & (pid[1] == 0) & (pid[2] == 0))
        def _():
            for cp in copies:
                cp.start()

        step(a_ref, b_ref, o_ref)

        @pl.when((pid[0] == grid[0] - 1) & (pid[1] == grid[1] - 1) & (pid[2] == grid[2] - 1))
        def _():
            for cp in copies:
                cp.wait_send()
            for cp in copies:
                cp.wait_recv()

    return pl.pallas_call(
        body, name=name, grid=grid, in_specs=[a_spec, b_spec] + [ANY] * n_ri, out_specs=[o_spec] + [ANY] * n_ro,
        out_shape=[o_shape] + list(r_shapes),
        scratch_shapes=[pltpu.SemaphoreType.DMA((n_rem,)), pltpu.SemaphoreType.DMA((n_rem,))],
        compiler_params=_cparams(("arbitrary", "arbitrary", "arbitrary"), has_side_effects=True),
    )(a, b, *r_ins)


def _rows(fn, name, T, S, tm, rows, bpars, gpars, outs, baccs, gaccs):
    nb = T // S
    tps = S // tm
    n_r, n_b, n_g, n_o, n_ba, n_ga = len(rows), len(bpars), len(gpars), len(outs), len(baccs), len(gaccs)

    def body(*refs):
        r_refs = refs[:n_r]
        b_refs = refs[n_r:n_r + n_b]
        g_refs = refs[n_r + n_b:n_r + n_b + n_g]
        o_refs = refs[n_r + n_b + n_g:n_r + n_b + n_g + n_o]
        ba_refs = refs[n_r + n_b + n_g + n_o:n_r + n_b + n_g + n_o + n_ba]
        ga_refs = refs[n_r + n_b + n_g + n_o + n_ba:]
        i = pl.program_id(0)
        o_vals, ba_vals, ga_vals = fn([r[...] for r in r_refs], [r[...] for r in b_refs], [r[...] for r in g_refs])
        for r, v in zip(o_refs, o_vals):
            r[...] = v.astype(r.dtype)
        if n_ba:
            @pl.when(i % tps == 0)
            def _():
                for r in ba_refs:
                    r[...] = jnp.zeros_like(r)
            for r, v in zip(ba_refs, ba_vals):
                r[...] += v.reshape(r.shape)
        if n_ga:
            @pl.when(i == 0)
            def _():
                for r in ga_refs:
                    r[...] = jnp.zeros_like(r)
            for r, v in zip(ga_refs, ga_vals):
                r[...] += v.reshape(r.shape)

    def row_spec(arr, w, cb, kind="tile"):
        hr = 8 * (4 // arr.dtype.itemsize)
        if kind == "prev":
            return pl.BlockSpec((hr, w), lambda i: (jnp.maximum(i * (tm // hr) - 1, 0), cb))
        if kind == "next":
            return pl.BlockSpec((hr, w), lambda i: (jnp.minimum((i + 1) * (tm // hr), T // hr - 1), cb))
        return pl.BlockSpec((tm, w), lambda i: (i, cb))

    in_specs = [row_spec(*r) for r in rows]
    in_specs += [pl.BlockSpec((None, 1, p.shape[-1]), lambda i: (i // tps, 0, 0)) for p in bpars]
    in_specs += [pl.BlockSpec(p.shape, lambda i: (0, 0)) for p in gpars]
    out_specs = [pl.BlockSpec((tm, w), lambda i: (i, 0)) for w, _ in outs]
    out_specs += [pl.BlockSpec((None, 1, w), lambda i: (i // tps, 0, 0)) for w in baccs]
    out_specs += [pl.BlockSpec(s, lambda i: (0, 0)) for s in gaccs]
    out_shape = [jax.ShapeDtypeStruct((T, w), dt) for w, dt in outs]
    out_shape += [jax.ShapeDtypeStruct((nb, 1, w), F32) for w in baccs]
    out_shape += [jax.ShapeDtypeStruct(s, F32) for s in gaccs]
    res = pl.pallas_call(
        body, name=name, grid=(T // tm,), in_specs=in_specs, out_specs=out_specs, out_shape=out_shape,
        compiler_params=_cparams(("arbitrary",)),
    )(*[r[0] for r in rows], *bpars, *gpars)
    return res[:n_o], res[n_o:n_o + n_ba], res[n_o + n_ba:]


@jax.custom_vjp
def _gsum(z, G):
    zh = z.astype(BF16)
    zl = (z - zh.astype(F32)).astype(BF16)
    r = _dot_nt(zh, G) + _dot_nt(zl, G)
    rh = r.astype(BF16)
    rl = (r - rh.astype(F32)).astype(BF16)
    return _dot(rh, G) + _dot(rl, G)


def _dot3(x, w):
    xh = x.astype(BF16).astype(F32)
    wh = w.astype(BF16).astype(F32)
    xc = jnp.concatenate([xh, xh, x - xh], axis=1).astype(BF16)
    wc = jnp.concatenate([wh, w - wh, wh], axis=0).astype(BF16)
    return _dot(xc, wc)


_gsum.defvjp(lambda z, G: (_gsum(z, G), G), lambda G, ct: (_gsum(ct, G), jnp.zeros_like(G)))


def _s1(x, g, scale, shift):
    y = x * lax.rsqrt(jnp.mean(x * x, axis=-1, keepdims=True) + RMS_EPS)
    return (y * g) * (1.0 + scale) + shift


def _s3(uc, og, cb, lg, lb):
    u = uc + cb
    mu = jnp.mean(u, axis=-1, keepdims=True)
    d = u - mu
    var = jnp.mean(d * d, axis=-1, keepdims=True)
    y = d * lax.rsqrt(var + LN_EPS) * lg + lb
    return _silu(y) * _silu(og)


def _s4(r0, k0, v0, l0, pr, pk, pv, plo, mu_r, mu_k, mu_v, mu_l, w0, w2p, a0, a2p, k_k, k_a, G):
    r = r0 + mu_r * (pr - r0)
    k = k0 + mu_k * (pk - k0)
    v = v0 + mu_v * (pv - v0)
    lo = l0 + mu_l * (plo - l0)
    w_pre = w0 + _dot3(jnp.tanh(lo), w2p)
    lw = -np.float32(np.exp(-0.5)) * _sigmoid(w_pre)
    a = _sigmoid(a0 + _dot3(lo, a2p))
    kkr = k * k_k
    ss = _gsum(kkr * kkr, G)
    kk = kkr / jnp.maximum(jnp.sqrt(ss), L2_EPS)
    k2 = k * (1.0 + (a - 1.0) * k_a)
    return r, lw, k2, v, kk, kk * a


def _s5(o, r, k2, v, og, gg, gb, rk, G):
    mu = _gsum(o, G) * (1.0 / HN)
    d = o - mu
    var = _gsum(d * d, G) * (1.0 / HN)
    y = d * lax.rsqrt(var + GN_EPS) * gg + gb
    bonus = _gsum(r * k2 * rk, G)
    return (y + bonus * v) * _silu(og)


def _s6(yc, yr, gc, gr):
    return _sigmoid(gc) * yc + _sigmoid(gr) * yr


def _s7(x, out, tgt, gate, fg):
    x2 = x + gate * out
    y = x2 * lax.rsqrt(jnp.mean(x2 * x2, axis=-1, keepdims=True) + RMS_EPS) * fg
    e = y - tgt
    return 0.5 * jnp.sum(jnp.mean(e * e, axis=-1))


@jax.custom_vjp
def _solve_all(a_kbs, rhss, cm):
    return _solve_all_fwd(a_kbs, rhss, cm)[0]


def _solve_all_fwd(a_kbs, rhss, cm):
    H = range(len(a_kbs))
    xi = [cm[2] - cm[3] * a_kbs[j] for j in H]
    for lvl in range(1, 6):
        t = [_pdot(_dot, xi[j], cm[3 + lvl] * a_kbs[j], P_INV) for j in H]
        xi = [xi[j] - _pdot(_dot, t[j], xi[j], P_INV) for j in H]
    u = tuple(_pdot(_dot, xi[j], rhss[j], P_APPLY) for j in H)
    return u, (xi, u, cm)


def _solve_all_bwd(res, dus):
    xi, u, cm = res
    H = range(len(u))
    g = tuple(_pdot(_dot_tn, xi[j], dus[j], P_APPLY) for j in H)
    da = tuple(-(cm[1] * _pdot(_dot_nt, g[j], u[j], P_APPLY)) for j in H)
    return da, g, jnp.zeros_like(cm)


_solve_all.defvjp(_solve_all_fwd, _solve_all_bwd)


def _chunk(sts, r, lw, k, v, kk, b, cm):
    cum = _dot(cm[0], lw, HI)
    ein = jnp.exp(-cum)
    rt = r * jnp.exp(cum)
    kkt = kk * jnp.exp(cum - lw)
    kh = k * ein
    bh = b * ein
    ec = jnp.exp(jnp.sum(lw, axis=0, keepdims=True))
    khe = kh * ec
    bhe = bh * ec
    H = range(len(sts))
    tri, strict, eye = cm[0], cm[1], cm[2]
    rt, kkt, kh, bh, v, khe, bhe, ec = ([a[:, j * HN:(j + 1) * HN] for j in H] for a in (rt, kkt, kh, bh, v, khe, bhe, ec))
    lhs = [jnp.concatenate([kkt[j], rt[j]], axis=0) for j in H]
    rhs_s = [jnp.concatenate([bh[j], kh[j]], axis=0) for j in H]
    lh = [a.astype(BF16).astype(F32) for a in lhs]
    rh = [a.astype(BF16).astype(F32) for a in rhs_s]
    lc = [jnp.concatenate([lh[j], lh[j], lhs[j] - lh[j]], axis=1).astype(BF16) for j in H]
    rc = [jnp.concatenate([rh[j], rhs_s[j] - rh[j], rh[j]], axis=1).astype(BF16) for j in H]
    sc = [_dot_nt(lc[j], rc[j]) for j in H]
    a_kb = [strict * sc[j][:CH, :CH] for j in H]
    a_kk = [strict * sc[j][:CH, CH:] for j in H]
    a_rb = [tri * sc[j][CH:, :CH] for j in H]
    a_rk = [tri * sc[j][CH:, CH:] for j in H]
    ps = [_dot_nt(lhs[j].astype(BF16), sts[j].astype(BF16)) for j in H]
    pv = [_dot(jnp.concatenate([a_kk[j], a_rk[j]], axis=0).astype(BF16), v[j].astype(BF16)) for j in H]
    rhs = [ps[j][:CH] + pv[j][:CH] for j in H]
    o0 = [ps[j][CH:] + pv[j][CH:] for j in H]
    u = _solve_all(tuple(a_kb), tuple(rhs), cm)
    o = [o0[j] - _pdot(_dot, a_rb[j], u[j], P_APPLY) for j in H]
    st2 = [sts[j] * ec[j] + _dot_tn(jnp.concatenate([v[j], u[j]], axis=0).astype(BF16),
                                    jnp.concatenate([khe[j], -bhe[j]], axis=0).astype(BF16)) for j in H]
    return jnp.concatenate(o, axis=1), tuple(st2)


def _chunk_consts():
    t = np.arange(CH)[:, None]
    s = np.arange(CH)[None, :]
    mats = [(t >= s), (t > s), (t == s)]
    for lvl in range(6):
        sz = 1 << lvl
        mats.append(((t // sz) % 2 == 1) & ((s // sz) == (t // sz) - 1))
    mats.append(np.zeros((CH, CH), bool))
    return np.stack(mats).astype(np.float32)


def _adamw(w, g, m, v):
    m = ADAM_B1 * m + (1.0 - ADAM_B1) * g
    v = ADAM_B2 * v + (1.0 - ADAM_B2) * (g * g)
    m_hat = m / (1.0 - ADAM_B1 ** ADAM_STEP)
    v_hat = v / (1.0 - ADAM_B2 ** ADAM_STEP)
    delta = -ADAM_LR * (m_hat / (jnp.sqrt(v_hat) + ADAM_EPS) + ADAM_WD * w)
    return delta, m, v


CT = 128
RB = 64
WIN = RB + 32


def _conv_fwd(pm, ck, T, S):
    nb = T // S

    def body(val_ref, gate_ref, ck_ref, out_ref, ubuf):
        ubuf[0:32, :] = jnp.zeros((32, CT), F32)
        ubuf[32:, :] = val_ref[...] * _sigmoid(gate_ref[...])

        def blk(rb, carry):
            base = pl.multiple_of(rb * RB, RB)
            win = ubuf[pl.ds(base, WIN), :]
            acc = jnp.zeros((RB, CT), F32)
            for j in range(CW):
                acc = acc + ck_ref[j:j + 1, :] * pltpu.roll(win, (WIN - (2 + j)) % WIN, 0)[0:RB, :]
            out_ref[pl.ds(base, RB), :] = acc
            return carry

        lax.fori_loop(0, S // RB, blk, 0)

    return pl.pallas_call(
        body, name="conv_fwd", grid=(D // CT, nb),
        in_specs=[pl.BlockSpec((S, CT), lambda ct, b: (b, ct)),
                  pl.BlockSpec((S, CT), lambda ct, b: (b, D // CT + ct)),
                  pl.BlockSpec((32, CT), lambda ct, b: (0, ct))],
        out_specs=pl.BlockSpec((S, CT), lambda ct, b: (b, ct)),
        out_shape=jax.ShapeDtypeStruct((T, D), F32),
        scratch_shapes=[pltpu.VMEM((S + 32, CT), F32)],
        compiler_params=_cparams(("parallel", "arbitrary")),
    )(pm, pm, ck)


def _conv_bwd(pm, duc, ck, T, S):
    nb = T // S

    def body(val_ref, gate_ref, duc_ref, ck_ref, dval_ref, dgate_ref, dck_ref, ubuf, dbuf, acc):
        b = pl.program_id(1)
        ubuf[0:32, :] = jnp.zeros((32, CT), F32)
        ubuf[32:, :] = val_ref[...] * _sigmoid(gate_ref[...])
        dbuf[0:S, :] = duc_ref[...]
        dbuf[S:, :] = jnp.zeros((32, CT), F32)
        acc[...] = jnp.zeros_like(acc)

        def blk(rb, carry):
            base = pl.multiple_of(rb * RB, RB)
            uwin = ubuf[pl.ds(base, WIN), :]
            dwin = dbuf[pl.ds(base, WIN), :]
            dblk = dwin[0:RB, :]
            du = jnp.zeros((RB, CT), F32)
            for j in range(CW):
                du = du + ck_ref[j:j + 1, :] * pltpu.roll(dwin, (WIN - (CW - 1 - j)) % WIN, 0)[0:RB, :]
                ush = pltpu.roll(uwin, (WIN - (2 + j)) % WIN, 0)[0:RB, :]
                acc[j] += jnp.sum((dblk * ush).reshape(RB // 8, 8, CT), axis=0)
            val = val_ref[pl.ds(base, RB), :]
            sg = _sigmoid(gate_ref[pl.ds(base, RB), :])
            dval_ref[pl.ds(base, RB), :] = (du * sg).astype(BF16)
            dgate_ref[pl.ds(base, RB), :] = (du * val * sg * (1.0 - sg)).astype(BF16)
            return carry

        lax.fori_loop(0, S // RB, blk, 0)

        @pl.when(b == 0)
        def _():
            dck_ref[...] = jnp.zeros_like(dck_ref)
        for j in range(CW):
            dck_ref[j:j + 1, :] += jnp.sum(acc[j], axis=0, keepdims=True)

    return pl.pallas_call(
        body, name="conv_bwd", grid=(D // CT, nb),
        in_specs=[pl.BlockSpec((S, CT), lambda ct, b: (b, ct)),
                  pl.BlockSpec((S, CT), lambda ct, b: (b, D // CT + ct)),
                  pl.BlockSpec((S, CT), lambda ct, b: (b, ct)),
                  pl.BlockSpec((32, CT), lambda ct, b: (0, ct))],
        out_specs=[pl.BlockSpec((S, CT), lambda ct, b: (b, ct)),
                   pl.BlockSpec((S, CT), lambda ct, b: (b, ct)),
                   pl.BlockSpec((32, CT), lambda ct, b: (0, ct))],
        out_shape=[jax.ShapeDtypeStruct((T, D), BF16), jax.ShapeDtypeStruct((T, D), BF16),
                   jax.ShapeDtypeStruct((32, D), F32)],
        scratch_shapes=[pltpu.VMEM((S + 32, CT), F32), pltpu.VMEM((S + 32, CT), F32), pltpu.VMEM((32, 8, CT), F32)],
        compiler_params=_cparams(("parallel", "arbitrary")),
    )(pm, pm, duc, ck)


HB = 16


def _scan_fwd(ins, cm, nb, S):
    nc = S // CH
    blk = pl.BlockSpec((CH, HB * HN), lambda b, g, i: (b * nc + i, g))
    hblk = pl.BlockSpec((None, HB, None, HN, HN), lambda b, g, i: (b, g, i, 0, 0))

    def body(r_ref, lw_ref, k_ref, v_ref, kk_ref, b_ref, cm_ref, o_ref, hs_ref, st):
        @pl.when(pl.program_id(2) == 0)
        def _():
            st[...] = jnp.zeros_like(st)
        s0 = [st[j] for j in range(HB)]
        for j in range(HB):
            hs_ref[j] = s0[j]
        o, s1 = _chunk(s0, r_ref[...], lw_ref[...], k_ref[...], v_ref[...], kk_ref[...], b_ref[...], cm_ref[...])
        o_ref[...] = o
        for j in range(HB):
            st[j] = s1[j]

    return pl.pallas_call(
        body, name="scan_fwd", grid=(nb, NH // HB, nc),
        in_specs=[blk] * 6 + [pl.BlockSpec(cm.shape, lambda b, g, i: (0, 0, 0))],
        out_specs=[blk, hblk],
        out_shape=[jax.ShapeDtypeStruct((nb * S, D), F32), jax.ShapeDtypeStruct((nb, NH, nc, HN, HN), F32)],
        scratch_shapes=[pltpu.VMEM((HB, HN, HN), F32)],
        compiler_params=_cparams(("parallel", "parallel", "arbitrary")),
    )(*ins, cm)


def _scan_bwd(ins, hs, do, cm, nb, S):
    nc = S // CH
    blk = pl.BlockSpec((CH, HB * HN), lambda b, g, i: (b * nc + nc - 1 - i, g))
    hblk = pl.BlockSpec((None, HB, None, HN, HN), lambda b, g, i: (b, g, nc - 1 - i, 0, 0))

    def body(r_ref, lw_ref, k_ref, v_ref, kk_ref, b_ref, hs_ref, do_ref, cm_ref,
             dr_ref, dlw_ref, dk_ref, dv_ref, dkk_ref, db_ref, dst):
        @pl.when(pl.program_id(2) == 0)
        def _():
            dst[...] = jnp.zeros_like(dst)
        cmv = cm_ref[...]
        f = lambda s0, r, lw, k, v, kk, b: _chunk(s0, r, lw, k, v, kk, b, cmv)
        _, vjp = jax.vjp(f, [hs_ref[j] for j in range(HB)], r_ref[...], lw_ref[...], k_ref[...], v_ref[...],
                         kk_ref[...], b_ref[...])
        ds0, dr, dlw, dk, dv, dkk, db = vjp((do_ref[...], tuple(dst[j] for j in range(HB))))
        for j in range(HB):
            dst[j] = ds0[j]
        dr_ref[...] = dr
        dlw_ref[...] = dlw
        dk_ref[...] = dk
        dv_ref[...] = dv
        dkk_ref[...] = dkk
        db_ref[...] = db

    return pl.pallas_call(
        body, name="scan_bwd", grid=(nb, NH // HB, nc),
        in_specs=[blk] * 6 + [hblk, blk, pl.BlockSpec(cm.shape, lambda b, g, i: (0, 0, 0))],
        out_specs=[blk] * 6,
        out_shape=[jax.ShapeDtypeStruct((nb * S, D), F32)] * 6,
        scratch_shapes=[pltpu.VMEM((HB, HN, HN), F32)],
        compiler_params=_cparams(("parallel", "parallel", "arbitrary")),
    )(*ins, hs, do, cm)


def _ew(fn, name, ins, n_out, tm, out_dtype=F32):
    R, W = ins[0].shape[-2:]
    tm = min(tm, R)
    assert R % tm == 0

    def body(*refs):
        vals = fn(*[r[...] for r in refs[:len(ins)]])
        for r, v in zip(refs[len(ins):], vals):
            r[...] = v.astype(r.dtype)

    def spec(a):
        if a.ndim == 3:
            return pl.BlockSpec((a.shape[0], tm, W), lambda i: (0, i, 0))
        return pl.BlockSpec((tm, W), lambda i: (i, 0))

    return pl.pallas_call(
        body, name=name, grid=(R // tm,), in_specs=[spec(a) for a in ins],
        out_specs=[pl.BlockSpec((tm, W), lambda i: (i, 0))] * n_out,
        out_shape=[jax.ShapeDtypeStruct((R, W), out_dtype)] * n_out,
        compiler_params=_cparams(("parallel",)),
    )(*ins)


def _sum_slots(r):
    s = r[0]
    for j in range(1, r.shape[0]):
        s = s + r[j]
    return s


def _place():
    x, y, c = lax.axis_index("x"), lax.axis_index("y"), lax.axis_index("c")
    return x, y, c


def _flip(v, d):
    return 1 - v if d else v


CHIP_PEERS = ((1, 0), (0, 1), (1, 1))
DEV_PEERS = tuple((dx, dy, dc) for dx in (0, 1) for dy in (0, 1) for dc in (0, 1))[1:]


def _comm_call(name, ins, out_shapes, plan, n_rem, n_fwd=0):
    n_in = len(ins)

    def body(*refs):
        in_refs, out_refs = refs[:n_in], refs[n_in:n_in + len(out_shapes)]
        send_sems, recv_sems, loc_sems = refs[n_in + len(out_shapes):]
        loc, rem, *rest = plan(in_refs, out_refs, _place())
        fwd = rest[0] if rest else []
        assert len(rem) == n_rem and len(fwd) == n_fwd and len(loc) <= 2 * n_in, (name, len(loc), len(rem), len(fwd))

        def remote(i, s, d, peer):
            return pltpu.make_async_remote_copy(src_ref=s, dst_ref=d, send_sem=send_sems.at[i], recv_sem=recv_sems.at[i],
                                                device_id=peer, device_id_type=MESH)

        copies = [pltpu.make_async_copy(s, d, loc_sems.at[i]) for i, (s, d) in enumerate(loc)]
        rcopies = [remote(i, s, d, peer) for i, (s, d, peer) in enumerate(rem)]
        for cp in copies + rcopies:
            cp.start()
        landed = set()
        fcopies = []
        for i, (s, d, peer, k) in enumerate(fwd):
            if k not in landed:
                rcopies[k].wait_recv()
                landed.add(k)
            fcopies.append(remote(n_rem + i, s, d, peer))
            fcopies[-1].start()
        for k, cp in enumerate(rcopies):
            if k not in landed:
                cp.wait_recv()
        for cp in rcopies + fcopies:
            cp.wait_send()
        for cp in fcopies:
            cp.wait_recv()
        for cp in copies:
            cp.wait()

    return pl.pallas_call(
        body, name=name, in_specs=[ANY] * n_in, out_specs=[ANY] * len(out_shapes), out_shape=out_shapes,
        scratch_shapes=[pltpu.SemaphoreType.DMA((n_rem + n_fwd,)), pltpu.SemaphoreType.DMA((n_rem + n_fwd,)),
                        pltpu.SemaphoreType.DMA((2 * n_in,))],
        compiler_params=pltpu.CompilerParams(has_side_effects=True),
    )(*ins)


def _gather_plan(n_big, in_refs, out_refs, place):
    x, y, c = place
    chip, dev = 2 * x + y, 4 * x + 2 * y + c
    sib = (x, y, 1 - c)
    loc = [(in_refs[0], out_refs[0].at[dev])] + [(s, d.at[chip]) for s, d in zip(in_refs[1 + n_big:], out_refs[1 + n_big:])]
    rem = [(in_refs[0], out_refs[0].at[dev], (_flip(x, dx), _flip(y, dy), _flip(c, dc))) for dx, dy, dc in DEV_PEERS]
    fwd = []
    for s, d in zip(in_refs[1:1 + n_big], out_refs[1:1 + n_big]):
        for dx, dy in CHIP_PEERS:
            px, py = _flip(x, dx), _flip(y, dy)
            fwd.append((d.at[2 * px + py, c], d.at[2 * px + py, c], sib, len(rem)))
            rem.append((s.at[c], d.at[chip, c], (px, py, c)))
    for s, d in zip(in_refs[1 + n_big:], out_refs[1 + n_big:]):
        rem += [(s, d.at[chip], (_flip(x, dx), _flip(y, dy), c)) for dx, dy in CHIP_PEERS]
    return loc, rem, fwd


def _halve_plan(in_refs, out_refs, place):
    x, y, c = place
    return [], [(s.at[1 - c], d, (x, y, 1 - c)) for s, d in zip(in_refs, out_refs)]


def _join_plan(in_refs, out_refs, place):
    x, y, c = place
    return [], [(s, d, (x, y, 1 - c)) for s, d in zip(in_refs, out_refs)]


def _scatter_plan(n_all, in_refs, out_refs, place):
    x, y, c = place
    chip, dev = 2 * x + y, 4 * x + 2 * y + c
    loc, rem = [], []
    for s, d in zip(in_refs[:n_all], out_refs[:n_all]):
        loc.append((s.at[dev], d.at[dev]))
        for dx, dy, dc in DEV_PEERS:
            px, py, pc = _flip(x, dx), _flip(y, dy), _flip(c, dc)
            rem.append((s.at[4 * px + 2 * py + pc], d.at[dev], (px, py, pc)))
    for s, d in zip(in_refs[n_all:], out_refs[n_all:]):
        for dx, dy in CHIP_PEERS:
            px, py = _flip(x, dx), _flip(y, dy)
            rem.append((s.at[2 * px + py], d.at[chip], (px, py, c)))
    return loc, rem


def _bshape(a, nb):
    return a.reshape(nb, 1, a.shape[-1])


def _with_prev(cur, before, tiles_per_seq):
    first = pl.program_id(0) % tiles_per_seq == 0
    row0 = jnp.where(first, 0.0, before[before.shape[0] - 1:, :])
    rid = lax.broadcasted_iota(jnp.int32, cur.shape, 0)
    return jnp.where(rid == 0, row0, pltpu.roll(cur, 1, 0))


def _with_next(cur, after, tiles_per_seq):
    last = pl.program_id(0) % tiles_per_seq == tiles_per_seq - 1
    n = cur.shape[0]
    row_n = jnp.where(last, 0.0, after[0:1, :])
    rid = lax.broadcasted_iota(jnp.int32, cur.shape, 0)
    return jnp.where(rid == n - 1, row_n, pltpu.roll(cur, n - 1, 0))


def _local_step(x2d, tgt, mod, wmain, wlora, late_w, ck, w2, a2, small, nb, S, grads_hook):
    T = nb * S
    shift, scale, gate = (_bshape(mod[:, i * D:(i + 1) * D], nb) for i in range(3))
    G = jnp.asarray(np.arange(128)[:, None] == np.arange(D)[None, :] // HN, dtype=BF16)
    cm = jnp.asarray(_chunk_consts())
    ckp = jnp.pad(ck, ((0, 1), (0, 0)))
    zpad = jnp.zeros((64, D), F32)
    w2p = jnp.concatenate([w2, zpad], axis=0)
    a2p = jnp.concatenate([zpad, a2], axis=0)
    mu = small["rwkv_mu"]
    mu_r, mu_k, mu_v, mu_l = mu[:, 0:D], mu[:, D:2 * D], mu[:, 2 * D:3 * D], mu[:, 3 * D:]
    g4 = [mu_r, mu_k, mu_v, mu_l, small["rwkv_w0"], w2p, small["rwkv_a0"], a2p, small["rwkv_k_k"], small["rwkv_k_a"], G]
    g5 = [small["rwkv_gn_g"], small["rwkv_gn_b"], small["rwkv_r_k"], G]
    g3 = [small["conv_b"], small["conv_ln_g"], small["conv_ln_b"]]

    (h,), _, _ = _rows(lambda r, b, g: ([_s1(r[0], g[0], b[0], b[1])], [], []), "pre_fwd", T, S, 256,
                       [(x2d, D, 0)], [scale, shift], [small["norm_g"]], [(D, BF16)], [], [])
    if len(late_w) == 3:
        pm = _matmul(h, wmain, "nn", "proj_main", min(T, 1024), 1024, D)
        wco, wro, wo = late_w
    else:
        pm, *landed = _matmul(h, wmain, "nn", "proj_main", min(T, 1024), 1024, D, ride=late_w[0])
        wco, wro, wo = late_w[1](landed)
    plo = _matmul(h, wlora, "nn", "proj_lora", 512, LORA, D)
    uc = _conv_fwd(pm, ckp, T, S)
    (uo,), _, _ = _rows(lambda r, b, g: ([_s3(r[0], r[1], *g)], [], []), "conv_post_fwd", T, S, 256,
                        [(uc, D, 0), (pm, D, 2)], [], g3, [(D, BF16)], [], [])
    yc = _matmul(uo, wco, "nn", "conv_out", 512, 1024, D)
    rows4 = [(pm, D, 3), (pm, D, 4), (pm, D, 5), (plo, LORA, 0),
             (pm, D, 3, "prev"), (pm, D, 4, "prev"), (pm, D, 5, "prev"), (plo, LORA, 0, "prev")]
    tps4 = S // 128

    def shifted4(r, tps=tps4):
        return list(r[:4]) + [_with_prev(r[i], r[4 + i], tps) for i in range(4)]

    sc_in, _, _ = _rows(lambda r, b, g: (list(_s4(*shifted4(r, S // 256), *g)), [], []), "rwkv_pre_fwd", T, S, 256,
                        rows4, [], g4, [(D, F32)] * 6, [], [])
    o, hs = _scan_fwd(sc_in, cm, nb, S)
    rows5 = [(o, D, 0), (sc_in[0], D, 0), (sc_in[2], D, 0), (sc_in[3], D, 0), (pm, D, 6)]
    (o2,), _, _ = _rows(lambda r, b, g: ([_s5(*r, *g)], [], []), "rwkv_post_fwd", T, S, 256,
                        rows5, [], g5, [(D, BF16)], [], [])
    yr = _matmul(o2, wro, "nn", "rwkv_out", 512, 1024, D)
    rows6 = [(yc, D, 0), (yr, D, 0), (pm, D, 7), (pm, D, 8)]
    (m,), _, _ = _rows(lambda r, b, g: ([_s6(*r)], [], []), "merge_fwd", T, S, 256, rows6, [], [], [(D, BF16)], [], [])
    out = _matmul(m, wo, "nn", "out_proj", 512, 1024, D)

    def head(r, b, g):
        loss, (dx, dout, dgate, dfg) = jax.value_and_grad(_s7, argnums=(0, 1, 3, 4))(r[0], r[1], r[2], b[0], g[0])
        return [dx, dout], [dgate], [dfg, jnp.full((1, 128), loss, F32)]

    (dx_res, dout), (dgate,), (d_final_g, loss_v) = _rows(
        head, "head", T, S, 256, [(x2d, D, 0), (out, D, 0), (tgt, D, 0)], [gate], [small["final_g"]],
        [(D, F32), (D, BF16)], [D], [(1, D), (1, 128)])

    d_wo = _matmul(m, dout, "tn", "d_w_out", 512, 1024, min(T, 2048))
    dm = _matmul(dout, wo, "nt", "d_merge", 512, 1024, D)

    def merge_bwd(r, b, g):
        _, vjp = jax.vjp(_s6, *r[:4])
        dyc, dyr, dgc, dgr = vjp(r[4])
        return [dyc, dyr, dgc, dgr], [], []

    (dyc, dyr, dgc, dgr), _, _ = _rows(merge_bwd, "merge_bwd", T, S, 256, rows6 + [(dm, D, 0)], [], [],
                                       [(D, BF16), (D, BF16), (D, BF16), (D, BF16)], [], [])
    d_wco = _matmul(uo, dyc, "tn", "d_w_conv_out", 512, 1024, min(T, 2048))
    d_wro = _matmul(o2, dyr, "tn", "d_w_rwkv_out", 512, 1024, min(T, 2048))
    duo = _matmul(dyc, wco, "nt", "d_conv_act", 512, 1024, D)
    do2 = _matmul(dyr, wro, "nt", "d_rwkv_act", 512, 1024, D)

    def conv_post_bwd(r, b, g):
        _, vjp = jax.vjp(_s3, r[0], r[1], *g)
        duc, dog, dcb, dlg, dlb = vjp(r[2])
        return [duc, dog], [], [dcb, dlg, dlb]

    (duc, dcog), _, (d_cb, d_lg, d_lb) = _rows(conv_post_bwd, "conv_post_bwd", T, S, 256,
                                               [(uc, D, 0), (pm, D, 2), (duo, D, 0)], [], g3,
                                               [(D, F32), (D, BF16)], [], [(1, D)] * 3)
    dval, dgt, d_ckp = _conv_bwd(pm, duc, ckp, T, S)

    def rwkv_post_bwd(r, b, g):
        _, vjp = jax.vjp(lambda *z: _s5(*z, g[3]), *r[:5], *g[:3])
        res = vjp(r[5])
        return list(res[:5]), [], list(res[5:8])

    (do, dr_b, dk_b, dv_b, drog), _, (d_gg, d_gb, d_rk) = _rows(
        rwkv_post_bwd, "rwkv_post_bwd", T, S, 256, rows5 + [(do2, D, 0)], [], g5,
        [(D, F32)] * 4 + [(D, BF16)], [], [(1, D)] * 3)
    dsc = _scan_bwd(sc_in, hs, do, cm, nb, S)

    def rwkv_pre_bwd(r, b, g):
        _, vjp = jax.vjp(lambda *z: _s4(*z, g[10]), *shifted4(r), *g[:10])
        ct = (r[8] + r[14], r[9], r[10] + r[15], r[11] + r[16], r[12], r[13])
        res = vjp(ct)
        return list(res[:8]), [], list(res[8:18])

    rows4b = rows4 + [(a, D, 0) for a in dsc] + [(dr_b, D, 0), (dk_b, D, 0), (dv_b, D, 0)]
    gshapes = [(1, D), (1, D), (1, D), (1, LORA), (1, D), (LORA, D), (1, D), (LORA, D), (1, D), (1, D)]
    dts, _, gts = _rows(rwkv_pre_bwd, "rwkv_pre_bwd", T, S, 128, rows4b, [], g4,
                        [(D, BF16)] * 3 + [(LORA, BF16)] + [(D, BF16)] * 3 + [(LORA, BF16)], [], gshapes)
    dr0, dk0, dv0, dl0, dpr, dpk, dpv, dpl = dts
    d_mu_r, d_mu_k, d_mu_v, d_mu_l, d_w0, d_w2p, d_a0, d_a2p, d_kk, d_ka = gts

    def assemble(r, b, g):
        r = [z.astype(F32) for z in r]
        sh = [_with_next(r[10 + i], r[14 + i], tps4) for i in range(4)]
        main = jnp.concatenate([r[0], r[1], r[2], r[3] + sh[0], r[4] + sh[1], r[5] + sh[2], r[6], r[7], r[8]], axis=1)
        return [main, r[9] + sh[3]], [], []

    rows_a = [(dval, D, 0), (dgt, D, 0), (dcog, D, 0), (dr0, D, 0), (dk0, D, 0), (dv0, D, 0), (drog, D, 0), (dgc, D, 0),
              (dgr, D, 0), (dl0, LORA, 0), (dpr, D, 0), (dpk, D, 0), (dpv, D, 0), (dpl, LORA, 0),
              (dpr, D, 0, "next"), (dpk, D, 0, "next"), (dpv, D, 0, "next"), (dpl, LORA, 0, "next")]
    (dpm, dplo), _, _ = _rows(assemble, "assemble_dp", T, S, 128, rows_a, [], [], [(DMAIN, BF16), (LORA, BF16)], [], [])
    d_wmain = _matmul(h, dpm, "tn", "d_w_main", 512, 1024, min(T, 2048))
    d_wlora = _matmul(h, dplo, "tn", "d_w_lora", 512, LORA, min(T, 2048))
    RW = D // NCHIP
    hv = [d_wmain.reshape(2, D // 2, DMAIN), d_wlora.reshape(2, D // 2, LORA)]
    hv += [g.reshape(NCHIP, 2, RW // 2, D).transpose(1, 0, 2, 3).reshape(2, NCHIP * RW // 2, D) for g in (d_wco, d_wro, d_wo)]
    hv += [g.reshape(-1, NCHIP, 2, RW // 2).transpose(2, 1, 0, 3).reshape(2, -1, RW // 2)
           for g in (d_ckp[:CW], d_w2p[:64], d_a2p[64:])]
    dh_m, *got_big = _matmul(dpm, wmain, "nt", "d_h_main", 512, 1024, 3072, ride=grads_hook(hv))
    dh_l = _matmul(dplo, wlora, "nt", "d_h_lora", 512, 1024, LORA)

    def pre_bwd(r, b, g):
        _, vjp = jax.vjp(_s1, r[0], g[0], b[0], b[1])
        dx, dg, dscale, dshift = vjp(r[1] + r[2])
        return [dx + r[3]], [dscale, dshift], [dg]

    (gx,), (dscale, dshift), (d_ng,) = _rows(pre_bwd, "pre_bwd", T, S, 256,
                                             [(x2d, D, 0), (dh_m, D, 0), (dh_l, D, 0), (dx_res, D, 0)],
                                             [scale, shift], [small["norm_g"]], [(D, F32)], [D, D], [(1, D)])
    dmod = jnp.concatenate([dshift, dscale, dgate], axis=-1).reshape(nb, 3 * D)
    d_small = {"norm_g": d_ng, "conv_b": d_cb, "conv_ln_g": d_lg, "conv_ln_b": d_lb,
               "rwkv_mu": jnp.concatenate([d_mu_r, d_mu_k, d_mu_v, d_mu_l], axis=1),
               "rwkv_w0": d_w0, "rwkv_a0": d_a0, "rwkv_k_k": d_kk, "rwkv_k_a": d_ka, "rwkv_r_k": d_rk,
               "rwkv_gn_g": d_gg, "rwkv_gn_b": d_gb, "final_g": d_final_g}
    return loss_v[0, 0], gx, dmod, got_big, d_small


def _step(a):
    nb, S, _ = a["x"].shape
    T = nb * S
    x_i, y_i, c_i = _place()
    chip = 2 * x_i + y_i
    w_in = a["w_in"][0]
    WS = w_in.shape[1]
    small_w = {n: a[n].reshape(1, sz) for n, sz in SMALL}

    def halves(t):
        return t.reshape(2, t.shape[0] // 2, t.shape[1])

    g_ins = [a["c"], halves(w_in.astype(BF16)), a["conv_k"][0], a["rwkv_w2"][0], a["rwkv_a2"][0]]
    g_out = [jax.ShapeDtypeStruct((NDEV,) + g_ins[0].shape, F32)]
    g_out += [jax.ShapeDtypeStruct((NCHIP,) + t.shape, t.dtype) for t in g_ins[1:]]
    c_all, win_g, ck_g, w2_g, a2_g = _comm_call(
        "gather_weights", g_ins, g_out, functools.partial(_gather_plan, 1), 7 + 3 * 4, 3)
    c_all = c_all.reshape(NDEV * nb, D)
    win_g = lax.dynamic_update_index_in_dim(win_g, g_ins[1], chip, 0).reshape(NCHIP, D, WS)
    late = [a[n][0].astype(BF16) for n in ("w_conv_out", "w_rwkv_out", "w_out")]

    def late_plan(in_refs, out_refs, place):
        x, y, c = place
        return [], [(s, d.at[2 * x + y], (_flip(x, dx), _flip(y, dy), c))
                    for s, d in zip(in_refs, out_refs) for dx, dy in CHIP_PEERS]

    def late_finish(landed):
        return [lax.dynamic_update_index_in_dim(g, own, chip, 0).reshape(D, D) for g, own in zip(landed, late)]

    late_w = ((late, [jax.ShapeDtypeStruct((NCHIP,) + t.shape, BF16) for t in late], late_plan, 9), late_finish)
    j0, o0 = divmod(6 * D, WS)
    j1, o1 = divmod(6 * D + LORA, WS)
    assert j0 == j1, "the lora columns sit inside one shard"
    wmain = jnp.concatenate([win_g[j] for j in range(j0)] + [win_g[j0][:, :o0], win_g[j0][:, o1:]]
                            + [win_g[j] for j in range(j0 + 1, NCHIP)], axis=1)
    wlora = win_g[j0][:, o0:o1]
    ck = jnp.concatenate([ck_g[j] for j in range(NCHIP)], axis=1)
    w2 = jnp.concatenate([w2_g[j] for j in range(NCHIP)], axis=1)
    a2 = jnp.concatenate([a2_g[j] for j in range(NCHIP)], axis=1)

    ada_w = a["ada_w"][0]
    MW = ada_w.shape[1]
    ada_b_loc = lax.dynamic_slice(a["ada_b"], (0, chip * MW), (1, MW))

    def mod_body(c_ref, w_ref, b_ref, o_ref):
        o_ref[...] = _dot(_silu(c_ref[...]), w_ref[...], HI) + b_ref[...]

    modp = pl.pallas_call(mod_body, name="ada_mod", out_shape=jax.ShapeDtypeStruct((NDEV * nb, MW), F32),
                          compiler_params=_cparams())(c_all, ada_w, ada_b_loc)
    (mod_g,) = _comm_call("scatter_mod", [modp.reshape(NDEV, nb, MW)],
                          [jax.ShapeDtypeStruct((NDEV, nb, MW), F32)],
                          functools.partial(_scatter_plan, 1), 7)
    mod = mod_g.reshape(NCHIP, 2, nb, MW)
    mod = mod[:, 0].transpose(1, 0, 2).reshape(nb, NCHIP * MW)

    RW = D // NCHIP
    sh_s = []

    def own_half_plus(both, q):
        return [jnp.where(lax.axis_index("c") == 0, both[0], both[1]) + q]

    def grads_hook(hv):
        got_h = _comm_call("halve_grads", hv, [jax.ShapeDtypeStruct(t.shape[1:], F32) for t in hv], _halve_plan, len(hv))
        chip_part = [_ew(own_half_plus, "chip_sum_%d" % i, [hv[i], got_h[i]], 1, 128, BF16)[0] for i in range(len(hv))]
        d_win_h = jnp.concatenate([chip_part[0][:, :6 * D], chip_part[1], chip_part[0][:, 6 * D:]], axis=1)
        sh_s.append(jnp.stack([d_win_h[:, j * WS:(j + 1) * WS] for j in range(NCHIP)]))
        sh_s.extend(t.reshape(NCHIP, RW // 2, D) for t in chip_part[2:5])
        sh_s.extend(t.reshape(NCHIP, -1, RW // 2) for t in chip_part[5:])
        return (sh_s, [jax.ShapeDtypeStruct(t.shape, t.dtype) for t in sh_s], functools.partial(_scatter_plan, 0),
                3 * len(sh_s))

    loss_p, gx, dmod, got_big, d_small = _local_step(
        a["x"].reshape(T, D), a["loss_target"].reshape(T, D), mod, wmain, wlora, late_w, ck, w2, a2, small_w, nb, S,
        grads_hook)
    loss = lax.psum(loss_p, ("x", "y", "c"))

    d_small["ada_b"] = _colsum(dmod)
    small_vec = jnp.concatenate([d_small[n] for n, _ in SMALL], axis=1)
    dmod_s = dmod.reshape(nb, NCHIP, MW).transpose(1, 0, 2)
    dmod_s = jnp.repeat(dmod_s, 2, axis=0)
    small_s = jnp.broadcast_to(small_vec[None], (NDEV, 1, NSMALL))
    got = _comm_call("scatter_small", [dmod_s, small_s], [jax.ShapeDtypeStruct(t.shape, F32) for t in (dmod_s, small_s)],
                     functools.partial(_scatter_plan, 2), 14)
    dmod_all, small_all = got[0].reshape(NDEV * nb, MW), got[1].reshape(NDEV, NSMALL)

    def shard_sum(recv, sent):
        chip_i = 2 * lax.axis_index("x") + lax.axis_index("y")
        s = None
        for j in range(NCHIP):
            t = jnp.where(chip_i == j, sent[j], recv[j]).astype(F32)
            s = t if s is None else s + t
        return [s]

    fin = [_ew(shard_sum, "shard_sum_%d" % i, [t, sh_s[i]], 1, 128)[0] for i, t in enumerate(got_big)]
    oth = _comm_call("join_halves", fin, [jax.ShapeDtypeStruct(t.shape, F32) for t in fin], _join_plan, len(fin))

    outs = {}

    def upd_halves(name, mine, other):
        shp = a[name].shape
        R, W = 2 * mine.shape[0], mine.shape[1]
        tm = 128
        nh = R // 2 // tm

        def body(w_ref, m_ref, v_ref, f_ref, o_ref, g_ref, d_ref, m2_ref, v2_ref):
            g = jnp.where(pl.program_id(0) // nh == lax.axis_index("c"), f_ref[...], o_ref[...])
            g_ref[...] = g
            d_ref[...], m2_ref[...], v2_ref[...] = _adamw(w_ref[...], g, m_ref[...], v_ref[...])

        full = pl.BlockSpec((None, tm, W), lambda i: (0, i, 0))
        half = pl.BlockSpec((tm, W), lambda i: (i % nh, 0))
        assert shp == (1, R, W)
        outs[name] = pl.pallas_call(
            body, name="adamw_" + name, grid=(R // tm,), in_specs=[full] * 3 + [half] * 2, out_specs=[full] * 4,
            out_shape=[jax.ShapeDtypeStruct(shp, F32)] * 4, compiler_params=_cparams(("parallel",)),
        )(*[a[p + name] for p in ("", "m_", "v_")], mine, other)

    for name, f, o in zip(("w_in", "w_conv_out", "w_rwkv_out", "w_out"), fin, oth):
        upd_halves(name, f, o)

    def upd(name, g):
        shp = a[name].shape
        ins = [a[p + name].reshape(g.shape) for p in ("", "m_", "v_")]
        res = _ew(lambda w_, m_, v_, g_: [g_, *_adamw(w_, g_, m_, v_)], "adamw_" + name, [*ins, g], 4, 128)
        outs[name] = [r.reshape(shp) for r in res]

    for name, f, o in zip(("conv_k", "rwkv_w2", "rwkv_a2"), fin[4:], oth[4:]):
        both = jnp.where(c_i == 0, jnp.stack([f, o]), jnp.stack([o, f]))
        upd(name, both.transpose(1, 0, 2).reshape(-1, RW))

    def adaw_body(c_ref, dm_ref, w_ref, m_ref, v_ref, g_ref, d_ref, m2_ref, v2_ref):
        g = _dot_tn(_silu(c_ref[...]), dm_ref[...], HI)
        g_ref[...] = g
        d_ref[...], m2_ref[...], v2_ref[...] = _adamw(w_ref[...], g, m_ref[...], v_ref[...])

    res = pl.pallas_call(adaw_body, name="adamw_ada_w", out_shape=[jax.ShapeDtypeStruct((D, MW), F32)] * 4,
                         compiler_params=_cparams())(c_all, dmod_all, ada_w, a["m_ada_w"][0], a["v_ada_w"][0])
    outs["ada_w"] = [r.reshape(a["ada_w"].shape) for r in res]

    wv, mv, vv = (jnp.concatenate([a[p + n].reshape(1, sz) for n, sz in SMALL], axis=1) for p in ("", "m_", "v_"))
    def small_fn(w_, m_, v_, gs):
        g = _sum_slots(gs)
        return [g, *_adamw(w_, g, m_, v_)]

    res = _ew(small_fn, "adamw_small", [wv, mv, vv, small_all.reshape(NDEV, 1, NSMALL)], 4, 8)
    off = 0
    for n, sz in SMALL:
        outs[n] = [r[:, off:off + sz].reshape(a[n].shape) for r in res]
        off += sz

    return (loss, gx.reshape(nb, S, D), *[outs[n][0] for n in WEIGHTS], *[outs[n][1] for n in WEIGHTS],
            *[outs[n][2] for n in WEIGHTS], *[outs[n][3] for n in WEIGHTS])


def _colsum(dmod):
    def body(d_ref, o_ref):
        o_ref[...] = jnp.sum(d_ref[...], axis=0, keepdims=True)
    return pl.pallas_call(body, name="ada_b_rowsum", out_shape=jax.ShapeDtypeStruct((1, dmod.shape[1]), F32),
                          compiler_params=_cparams())(dmod)


def kernel(x, c, ada_w, ada_b, norm_g, w_in, conv_k, conv_b, conv_ln_g, conv_ln_b, w_conv_out, rwkv_mu, rwkv_w0, rwkv_w2, rwkv_a0, rwkv_a2, rwkv_k_k, rwkv_k_a, rwkv_r_k, rwkv_gn_g, rwkv_gn_b, w_rwkv_out, w_out, final_g, loss_target, m_ada_w, m_ada_b, m_norm_g, m_w_in, m_conv_k, m_conv_b, m_conv_ln_g, m_conv_ln_b, m_w_conv_out, m_rwkv_mu, m_rwkv_w0, m_rwkv_w2, m_rwkv_a0, m_rwkv_a2, m_rwkv_k_k, m_rwkv_k_a, m_rwkv_r_k, m_rwkv_gn_g, m_rwkv_gn_b, m_w_rwkv_out, m_w_out, m_final_g, v_ada_w, v_ada_b, v_norm_g, v_w_in, v_conv_k, v_conv_b, v_conv_ln_g, v_conv_ln_b, v_w_conv_out, v_rwkv_mu, v_rwkv_w0, v_rwkv_w2, v_rwkv_a0, v_rwkv_a2, v_rwkv_k_k, v_rwkv_k_a, v_rwkv_r_k, v_rwkv_gn_g, v_rwkv_gn_b, v_w_rwkv_out, v_w_out, v_final_g):
    return _step(dict(locals()))
```

```python
import functools

import numpy as np
import jax
import jax.numpy as jnp
from jax import lax
from jax.experimental import pallas as pl
from jax.experimental.pallas import tpu as pltpu

F32 = jnp.float32
BF16 = jnp.bfloat16
HI = lax.Precision.HIGHEST
MESH = pl.DeviceIdType.MESH
ANY = pl.BlockSpec(memory_space=pl.ANY)

D = 1024
NH = 16
HN = 64
LORA = 128
DMAIN = 9 * D
CH = 64
CW = 31
NCHIP = 4
NDEV = 8
VMEM_LIMIT = 56 * 1024 * 1024

RMS_EPS = 1e-6
LN_EPS = 1e-5
GN_EPS = 64e-5
L2_EPS = 1e-12
ADAM_LR = 0.001
ADAM_B1 = 0.9
ADAM_B2 = 0.999
ADAM_EPS = 1e-08
ADAM_WD = 0.01
ADAM_STEP = 10

SMALL = (("ada_b", 3072), ("norm_g", 1024), ("conv_b", 1024), ("conv_ln_g", 1024), ("conv_ln_b", 1024),
         ("rwkv_mu", 3200), ("rwkv_w0", 1024), ("rwkv_a0", 1024), ("rwkv_k_k", 1024), ("rwkv_k_a", 1024),
         ("rwkv_r_k", 1024), ("rwkv_gn_g", 1024), ("rwkv_gn_b", 1024), ("final_g", 1024))
NSMALL = sum(n for _, n in SMALL)

WEIGHTS = ['ada_w', 'ada_b', 'norm_g', 'w_in', 'conv_k', 'conv_b', 'conv_ln_g', 'conv_ln_b', 'w_conv_out', 'rwkv_mu',
           'rwkv_w0', 'rwkv_w2', 'rwkv_a0', 'rwkv_a2', 'rwkv_k_k', 'rwkv_k_a', 'rwkv_r_k', 'rwkv_gn_g', 'rwkv_gn_b',
           'w_rwkv_out', 'w_out', 'final_g']


def _cparams(sem=None, **kw):
    if sem is not None:
        kw["dimension_semantics"] = sem
    return pltpu.CompilerParams(vmem_limit_bytes=VMEM_LIMIT, **kw)


def _dot(a, b, prec=None):
    return jnp.dot(a, b, preferred_element_type=F32, precision=prec)


def _dot_nt(a, b, prec=None):
    return lax.dot_general(a, b, (((1,), (1,)), ((), ())), preferred_element_type=F32, precision=prec)


def _dot_tn(a, b, prec=None):
    return lax.dot_general(a, b, (((0,), (0,)), ((), ())), preferred_element_type=F32, precision=prec)


def _pdot(f, a, b, p):
    if p == "hi":
        return f(a, b, HI)
    ah, bh = a.astype(BF16), b.astype(BF16)
    if p == "bf":
        return f(ah, bh)
    al, bl = (a - ah.astype(F32)).astype(BF16), (b - bh.astype(F32)).astype(BF16)
    return f(ah, bh) + (f(ah, bl) + f(al, bh))


P_SCORE = "b3"
P_INV = "bf"
P_APPLY = "bf"


def _sigmoid(z):
    return 1.0 / (1.0 + jnp.exp(-z))


def _silu(z):
    return z * _sigmoid(z)


def _matmul(a, b, mode, name, tm, tn, tk, ride=None, out_t=False):
    if mode == "nn":
        (M, K), N = a.shape, b.shape[1]
        a_spec = pl.BlockSpec((tm, tk), lambda j, i, k: (i, k))
        b_spec = pl.BlockSpec((tk, tn), lambda j, i, k: (k, j))
        f = _dot
    elif mode == "nt":
        (M, K), N = a.shape, b.shape[0]
        a_spec = pl.BlockSpec((tm, tk), lambda j, i, k: (i, k))
        b_spec = pl.BlockSpec((tn, tk), lambda j, i, k: (j, k))
        f = _dot_nt
    else:
        (K, M), N = a.shape, b.shape[1]
        a_spec = pl.BlockSpec((tk, tm), lambda j, i, k: (k, i))
        b_spec = pl.BlockSpec((tk, tn), lambda j, i, k: (k, j))
        f = _dot_tn
    assert M % tm == 0 and N % tn == 0 and K % tk == 0, (name, M, N, K)

    grid = (N // tn, M // tm, K // tk)
    o_spec = pl.BlockSpec((tm, tn), lambda j, i, k: (i, j))
    o_shape = jax.ShapeDtypeStruct((M, N), F32)

    def step(a_ref, b_ref, o_ref):
        @pl.when(pl.program_id(2) == 0)
        def _():
            o_ref[...] = jnp.zeros_like(o_ref)
        o_ref[...] += f(a_ref[...], b_ref[...])

    if out_t:
        assert ride is None

        def step_t(a_ref, b_ref, o_ref, acc):
            step(a_ref, b_ref, acc)

            @pl.when(pl.program_id(2) == grid[2] - 1)
            def _():
                o_ref[...] = acc[...].T

        return pl.pallas_call(
            step_t, name=name, grid=grid, in_specs=[a_spec, b_spec],
            out_specs=pl.BlockSpec((None, tn, tm), lambda j, i, k: (i, j, 0)),
            out_shape=jax.ShapeDtypeStruct((M // tm, N, tm), F32), scratch_shapes=[pltpu.VMEM((tm, tn), F32)],
            compiler_params=_cparams(("parallel", "parallel", "arbitrary")),
        )(a, b)

    if ride is None:
        return pl.pallas_call(
            step, name=name, grid=grid, in_specs=[a_spec, b_spec], out_specs=o_spec, out_shape=o_shape,
            compiler_params=_cparams(("parallel", "parallel", "arbitrary")),
        )(a, b)

    r_ins, r_shapes, plan, n_rem = ride
    n_ri, n_ro = len(r_ins), len(r_shapes)

    def body(a_ref, b_ref, *rest):
        r_in, o_ref, r_out = rest[:n_ri], rest[n_ri], rest[n_ri + 1:n_ri + 1 + n_ro]
        send_sems, recv_sems = rest[n_ri + 1 + n_ro:]
        loc, rem = plan(r_in, r_out, _place())
        assert not loc and len(rem) == n_rem, (name, len(loc), len(rem))
        copies = [pltpu.make_async_remote_copy(src_ref=s, dst_ref=d, send_sem=send_sems.at[i], recv_sem=recv_sems.at[i],
                                               device_id=peer, device_id_type=MESH) for i, (s, d, peer) in enumerate(rem)]
        pid = [pl.program_id(ax) for ax in range(3)]

        @pl.when((pid[0] == 0) & (pid[1] == 0) & (pid[2] == 0))
        def _():
            for cp in copies:
                cp.start()

        step(a_ref, b_ref, o_ref)

        @pl.when((pid[0] == grid[0] - 1) & (pid[1] == grid[1] - 1) & (pid[2] == grid[2] - 1))
        def _():
            for cp in copies:
                cp.wait_send()
            for cp in copies:
                cp.wait_recv()

    return pl.pallas_call(
        body, name=name, grid=grid, in_specs=[a_spec, b_spec] + [ANY] * n_ri, out_specs=[o_spec] + [ANY] * n_ro,
        out_shape=[o_shape] + list(r_shapes),
        scratch_shapes=[pltpu.SemaphoreType.DMA((n_rem,)), pltpu.SemaphoreType.DMA((n_rem,))],
        compiler_params=_cparams(("arbitrary", "arbitrary", "arbitrary"), has_side_effects=True),
    )(a, b, *r_ins)


def _rows(fn, name, T, S, tm, rows, bpars, gpars, outs, baccs, gaccs):
    nb = T // S
    tps = S // tm
    n_r, n_b, n_g, n_o, n_ba, n_ga = len(rows), len(bpars), len(gpars), len(outs), len(baccs), len(gaccs)

    def body(*refs):
        r_refs = refs[:n_r]
        b_refs = refs[n_r:n_r + n_b]
        g_refs = refs[n_r + n_b:n_r + n_b + n_g]
        o_refs = refs[n_r + n_b + n_g:n_r + n_b + n_g + n_o]
        ba_refs = refs[n_r + n_b + n_g + n_o:n_r + n_b + n_g + n_o + n_ba]
        ga_refs = refs[n_r + n_b + n_g + n_o + n_ba:]
        i = pl.program_id(0)
        o_vals, ba_vals, ga_vals = fn([r[...] for r in r_refs], [r[...] for r in b_refs], [r[...] for r in g_refs])
        for r, v in zip(o_refs, o_vals):
            r[...] = v.astype(r.dtype)
        if n_ba:
            @pl.when(i % tps == 0)
            def _():
                for r in ba_refs:
                    r[...] = jnp.zeros_like(r)
            for r, v in zip(ba_refs, ba_vals):
                r[...] += v.reshape(r.shape)
        if n_ga:
            @pl.when(i == 0)
            def _():
                for r in ga_refs:
                    r[...] = jnp.zeros_like(r)
            for r, v in zip(ga_refs, ga_vals):
                r[...] += v.reshape(r.shape)

    def row_spec(arr, w, cb, kind="tile"):
        hr = 8 * (4 // arr.dtype.itemsize)
        if kind == "prev":
            return pl.BlockSpec((hr, w), lambda i: (jnp.maximum(i * (tm // hr) - 1, 0), cb))
        if kind == "next":
            return pl.BlockSpec((hr, w), lambda i: (jnp.minimum((i + 1) * (tm // hr), T // hr - 1), cb))
        return pl.BlockSpec((tm, w), lambda i: (i, cb))

    in_specs = [row_spec(*r) for r in rows]
    in_specs += [pl.BlockSpec((None, 1, p.shape[-1]), lambda i: (i // tps, 0, 0)) for p in bpars]
    in_specs += [pl.BlockSpec(p.shape, lambda i: (0, 0)) for p in gpars]
    out_specs = [pl.BlockSpec((tm, w), lambda i: (i, 0)) for w, _ in outs]
    out_specs += [pl.BlockSpec((None, 1, w), lambda i: (i // tps, 0, 0)) for w in baccs]
    out_specs += [pl.BlockSpec(s, lambda i: (0, 0)) for s in gaccs]
    out_shape = [jax.ShapeDtypeStruct((T, w), dt) for w, dt in outs]
    out_shape += [jax.ShapeDtypeStruct((nb, 1, w), F32) for w in baccs]
    out_shape += [jax.ShapeDtypeStruct(s, F32) for s in gaccs]
    res = pl.pallas_call(
        body, name=name, grid=(T // tm,), in_specs=in_specs, out_specs=out_specs, out_shape=out_shape,
        compiler_params=_cparams(("arbitrary",)),
    )(*[r[0] for r in rows], *bpars, *gpars)
    return res[:n_o], res[n_o:n_o + n_ba], res[n_o + n_ba:]


@jax.custom_vjp
def _gsum(z, G):
    zh = z.astype(BF16)
    zl = (z - zh.astype(F32)).astype(BF16)
    r = _dot_nt(zh, G) + _dot_nt(zl, G)
    rh = r.astype(BF16)
    rl = (r - rh.astype(F32)).astype(BF16)
    return _dot(rh, G) + _dot(rl, G)


def _dot3(x, w):
    xh = x.astype(BF16).astype(F32)
    wh = w.astype(BF16).astype(F32)
    xc = jnp.concatenate([xh, xh, x - xh], axis=1).astype(BF16)
    wc = jnp.concatenate([wh, w - wh, wh], axis=0).astype(BF16)
    return _dot(xc, wc)


_gsum.defvjp(lambda z, G: (_gsum(z, G), G), lambda G, ct: (_gsum(ct, G), jnp.zeros_like(G)))


def _s1(x, g, scale, shift):
    y = x * lax.rsqrt(jnp.mean(x * x, axis=-1, keepdims=True) + RMS_EPS)
    return (y * g) * (1.0 + scale) + shift


def _s3(uc, og, cb, lg, lb):
    u = uc + cb
    mu = jnp.mean(u, axis=-1, keepdims=True)
    d = u - mu
    var = jnp.mean(d * d, axis=-1, keepdims=True)
    y = d * lax.rsqrt(var + LN_EPS) * lg + lb
    return _silu(y) * _silu(og)


def _s4(r0, k0, v0, l0, pr, pk, pv, plo, mu_r, mu_k, mu_v, mu_l, w0, w2p, a0, a2p, k_k, k_a, G):
    r = r0 + mu_r * (pr - r0)
    k = k0 + mu_k * (pk - k0)
    v = v0 + mu_v * (pv - v0)
    lo = l0 + mu_l * (plo - l0)
    w_pre = w0 + _dot3(jnp.tanh(lo), w2p)
    lw = -np.float32(np.exp(-0.5)) * _sigmoid(w_pre)
    a = _sigmoid(a0 + _dot3(lo, a2p))
    kkr = k * k_k
    ss = _gsum(kkr * kkr, G)
    kk = kkr / jnp.maximum(jnp.sqrt(ss), L2_EPS)
    k2 = k * (1.0 + (a - 1.0) * k_a)
    return r, lw, k2, v, kk, kk * a


def _s5(o, r, k2, v, og, gg, gb, rk, G):
    mu = _gsum(o, G) * (1.0 / HN)
    d = o - mu
    var = _gsum(d * d, G) * (1.0 / HN)
    y = d * lax.rsqrt(var + GN_EPS) * gg + gb
    bonus = _gsum(r * k2 * rk, G)
    return (y + bonus * v) * _silu(og)


def _s6(yc, yr, gc, gr):
    return _sigmoid(gc) * yc + _sigmoid(gr) * yr


def _s7(x, out, tgt, gate, fg):
    x2 = x + gate * out
    y = x2 * lax.rsqrt(jnp.mean(x2 * x2, axis=-1, keepdims=True) + RMS_EPS) * fg
    e = y - tgt
    return 0.5 * jnp.sum(jnp.mean(e * e, axis=-1))


@jax.custom_vjp
def _solve_all(a_kbs, rhss, cm):
    return _solve_all_fwd(a_kbs, rhss, cm)[0]


def _solve_all_fwd(a_kbs, rhss, cm):
    H = range(len(a_kbs))
    xi = [cm[2] - cm[3] * a_kbs[j] for j in H]
    for lvl in range(1, 6):
        t = [_pdot(_dot, xi[j], cm[3 + lvl] * a_kbs[j], P_INV) for j in H]
        xi = [xi[j] - _pdot(_dot, t[j], xi[j], P_INV) for j in H]
    u = tuple(_pdot(_dot, xi[j], rhss[j], P_APPLY) for j in H)
    return u, (xi, u, cm)


def _solve_all_bwd(res, dus):
    xi, u, cm = res
    H = range(len(u))
    g = tuple(_pdot(_dot_tn, xi[j], dus[j], P_APPLY) for j in H)
    da = tuple(-(cm[1] * _pdot(_dot_nt, g[j], u[j], P_APPLY)) for j in H)
    return da, g, jnp.zeros_like(cm)


_solve_all.defvjp(_solve_all_fwd, _solve_all_bwd)


def _chunk(sts, r, lw, k, v, kk, b, cm):
    cum = _dot(cm[0], lw, HI)
    ein = jnp.exp(-cum)
    rt = r * jnp.exp(cum)
    kkt = kk * jnp.exp(cum - lw)
    kh = k * ein
    bh = b * ein
    ec = jnp.exp(jnp.sum(lw, axis=0, keepdims=True))
    khe = kh * ec
    bhe = bh * ec
    H = range(len(sts))
    tri, strict, eye = cm[0], cm[1], cm[2]
    rt, kkt, kh, bh, v, khe, bhe, ec = ([a[:, j * HN:(j + 1) * HN] for j in H] for a in (rt, kkt, kh, bh, v, khe, bhe, ec))
    lhs = [jnp.concatenate([kkt[j], rt[j]], axis=0) for j in H]
    rhs_s = [jnp.concatenate([bh[j], kh[j]], axis=0) for j in H]
    lh = [a.astype(BF16).astype(F32) for a in lhs]
    rh = [a.astype(BF16).astype(F32) for a in rhs_s]
    lc = [jnp.concatenate([lh[j], lh[j], lhs[j] - lh[j]], axis=1).astype(BF16) for j in H]
    rc = [jnp.concatenate([rh[j], rhs_s[j] - rh[j], rh[j]], axis=1).astype(BF16) for j in H]
    sc = [_dot_nt(lc[j], rc[j]) for j in H]
    a_kb = [strict * sc[j][:CH, :CH] for j in H]
    a_kk = [strict * sc[j][:CH, CH:] for j in H]
    a_rb = [tri * sc[j][CH:, :CH] for j in H]
    a_rk = [tri * sc[j][CH:, CH:] for j in H]
    ps = [_dot_nt(lhs[j].astype(BF16), sts[j].astype(BF16)) for j in H]
    pv = [_dot(jnp.concatenate([a_kk[j], a_rk[j]], axis=0).astype(BF16), v[j].astype(BF16)) for j in H]
    rhs = [ps[j][:CH] + pv[j][:CH] for j in H]
    o0 = [ps[j][CH:] + pv[j][CH:] for j in H]
    u = _solve_all(tuple(a_kb), tuple(rhs), cm)
    o = [o0[j] - _pdot(_dot, a_rb[j], u[j], P_APPLY) for j in H]
    st2 = [sts[j] * ec[j] + _dot_tn(jnp.concatenate([v[j], u[j]], axis=0).astype(BF16),
                                    jnp.concatenate([khe[j], -bhe[j]], axis=0).astype(BF16)) for j in H]
    return jnp.concatenate(o, axis=1), tuple(st2)


def _chunk_consts():
    t = np.arange(CH)[:, None]
    s = np.arange(CH)[None, :]
    mats = [(t >= s), (t > s), (t == s)]
    for lvl in range(6):
        sz = 1 << lvl
        mats.append(((t // sz) % 2 == 1) & ((s // sz) == (t // sz) - 1))
    mats.append(np.zeros((CH, CH), bool))
    return np.stack(mats).astype(np.float32)


def _adamw(w, g, m, v):
    m = ADAM_B1 * m + (1.0 - ADAM_B1) * g
    v = ADAM_B2 * v + (1.0 - ADAM_B2) * (g * g)
    m_hat = m / (1.0 - ADAM_B1 ** ADAM_STEP)
    v_hat = v / (1.0 - ADAM_B2 ** ADAM_STEP)
    delta = -ADAM_LR * (m_hat / (jnp.sqrt(v_hat) + ADAM_EPS) + ADAM_WD * w)
    return delta, m, v


CT = 128
RB = 64
WIN = RB + 32


def _conv_fwd(pm, ck, T, S):
    nb = T // S

    def body(val_ref, gate_ref, ck_ref, out_ref, ubuf):
        ubuf[0:32, :] = jnp.zeros((32, CT), F32)
        ubuf[32:, :] = val_ref[...] * _sigmoid(gate_ref[...])

        def blk(rb, carry):
            base = pl.multiple_of(rb * RB, RB)
            win = ubuf[pl.ds(base, WIN), :]
            acc = jnp.zeros((RB, CT), F32)
            for j in range(CW):
                acc = acc + ck_ref[j:j + 1, :] * pltpu.roll(win, (WIN - (2 + j)) % WIN, 0)[0:RB, :]
            out_ref[pl.ds(base, RB), :] = acc
            return carry

        lax.fori_loop(0, S // RB, blk, 0)

    return pl.pallas_call(
        body, name="conv_fwd", grid=(D // CT, nb),
        in_specs=[pl.BlockSpec((S, CT), lambda ct, b: (b, ct)),
                  pl.BlockSpec((S, CT), lambda ct, b: (b, D // CT + ct)),
                  pl.BlockSpec((32, CT), lambda ct, b: (0, ct))],
        out_specs=pl.BlockSpec((S, CT), lambda ct, b: (b, ct)),
        out_shape=jax.ShapeDtypeStruct((T, D), F32),
        scratch_shapes=[pltpu.VMEM((S + 32, CT), F32)],
        compiler_params=_cparams(("parallel", "arbitrary")),
    )(pm, pm, ck)


def _conv_bwd(pm, duc, ck, T, S):
    nb = T // S

    def body(val_ref, gate_ref, duc_ref, ck_ref, dval_ref, dgate_ref, dck_ref, ubuf, dbuf, acc):
        b = pl.program_id(1)
        ubuf[0:32, :] = jnp.zeros((32, CT), F32)
        ubuf[32:, :] = val_ref[...] * _sigmoid(gate_ref[...])
        dbuf[0:S, :] = duc_ref[...]
        dbuf[S:, :] = jnp.zeros((32, CT), F32)
        acc[...] = jnp.zeros_like(acc)

        def blk(rb, carry):
            base = pl.multiple_of(rb * RB, RB)
            uwin = ubuf[pl.ds(base, WIN), :]
            dwin = dbuf[pl.ds(base, WIN), :]
            dblk = dwin[0:RB, :]
            du = jnp.zeros((RB, CT), F32)
            for j in range(CW):
                du = du + ck_ref[j:j + 1, :] * pltpu.roll(dwin, (WIN - (CW - 1 - j)) % WIN, 0)[0:RB, :]
                ush = pltpu.roll(uwin, (WIN - (2 + j)) % WIN, 0)[0:RB, :]
                acc[j] += jnp.sum((dblk * ush).reshape(RB // 8, 8, CT), axis=0)
            val = val_ref[pl.ds(base, RB), :]
            sg = _sigmoid(gate_ref[pl.ds(base, RB), :])
            dval_ref[pl.ds(base, RB), :] = (du * sg).astype(BF16)
            dgate_ref[pl.ds(base, RB), :] = (du * val * sg * (1.0 - sg)).astype(BF16)
            return carry

        lax.fori_loop(0, S // RB, blk, 0)

        @pl.when(b == 0)
        def _():
            dck_ref[...] = jnp.zeros_like(dck_ref)
        for j in range(CW):
            dck_ref[j:j + 1, :] += jnp.sum(acc[j], axis=0, keepdims=True)

    return pl.pallas_call(
        body, name="conv_bwd", grid=(D // CT, nb),
        in_specs=[pl.BlockSpec((S, CT), lambda ct, b: (b, ct)),
                  pl.BlockSpec((S, CT), lambda ct, b: (b, D // CT + ct)),
                  pl.BlockSpec((S, CT), lambda ct, b: (b, ct)),
                  pl.BlockSpec((32, CT), lambda ct, b: (0, ct))],
        out_specs=[pl.BlockSpec((S, CT), lambda ct, b: (b, ct)),
                   pl.BlockSpec((S, CT), lambda ct, b: (b, ct)),
                   pl.BlockSpec((32, CT), lambda ct, b: (0, ct))],
        out_shape=[jax.ShapeDtypeStruct((T, D), BF16), jax.ShapeDtypeStruct((T, D), BF16),
                   jax.ShapeDtypeStruct((32, D), F32)],
        scratch_shapes=[pltpu.VMEM((S + 32, CT), F32), pltpu.VMEM((S + 32, CT), F32), pltpu.VMEM((32, 8, CT), F32)],
        compiler_params=_cparams(("parallel", "arbitrary")),
    )(pm, pm, duc, ck)


HB = 16


def _scan_fwd(ins, cm, nb, S):
    nc = S // CH
    blk = pl.BlockSpec((CH, HB * HN), lambda b, g, i: (b * nc + i, g))
    hblk = pl.BlockSpec((None, HB, None, HN, HN), lambda b, g, i: (b, g, i, 0, 0))

    def body(r_ref, lw_ref, k_ref, v_ref, kk_ref, b_ref, cm_ref, o_ref, hs_ref, st):
        @pl.when(pl.program_id(2) == 0)
        def _():
            st[...] = jnp.zeros_like(st)
        s0 = [st[j] for j in range(HB)]
        for j in range(HB):
            hs_ref[j] = s0[j]
        o, s1 = _chunk(s0, r_ref[...], lw_ref[...], k_ref[...], v_ref[...], kk_ref[...], b_ref[...], cm_ref[...])
        o_ref[...] = o
        for j in range(HB):
            st[j] = s1[j]

    return pl.pallas_call(
        body, name="scan_fwd", grid=(nb, NH // HB, nc),
        in_specs=[blk] * 6 + [pl.BlockSpec(cm.shape, lambda b, g, i: (0, 0, 0))],
        out_specs=[blk, hblk],
        out_shape=[jax.ShapeDtypeStruct((nb * S, D), F32), jax.ShapeDtypeStruct((nb, NH, nc, HN, HN), F32)],
        scratch_shapes=[pltpu.VMEM((HB, HN, HN), F32)],
        compiler_params=_cparams(("parallel", "parallel", "arbitrary")),
    )(*ins, cm)


def _scan_bwd(ins, hs, do, cm, nb, S):
    nc = S // CH
    blk = pl.BlockSpec((CH, HB * HN), lambda b, g, i: (b * nc + nc - 1 - i, g))
    hblk = pl.BlockSpec((None, HB, None, HN, HN), lambda b, g, i: (b, g, nc - 1 - i, 0, 0))

    def body(r_ref, lw_ref, k_ref, v_ref, kk_ref, b_ref, hs_ref, do_ref, cm_ref,
             dr_ref, dlw_ref, dk_ref, dv_ref, dkk_ref, db_ref, dst):
        @pl.when(pl.program_id(2) == 0)
        def _():
            dst[...] = jnp.zeros_like(dst)
        cmv = cm_ref[...]
        f = lambda s0, r, lw, k, v, kk, b: _chunk(s0, r, lw, k, v, kk, b, cmv)
        _, vjp = jax.vjp(f, [hs_ref[j] for j in range(HB)], r_ref[...], lw_ref[...], k_ref[...], v_ref[...],
                         kk_ref[...], b_ref[...])
        ds0, dr, dlw, dk, dv, dkk, db = vjp((do_ref[...], tuple(dst[j] for j in range(HB))))
        for j in range(HB):
            dst[j] = ds0[j]
        dr_ref[...] = dr
        dlw_ref[...] = dlw
        dk_ref[...] = dk
        dv_ref[...] = dv
        dkk_ref[...] = dkk
        db_ref[...] = db

    return pl.pallas_call(
        body, name="scan_bwd", grid=(nb, NH // HB, nc),
        in_specs=[blk] * 6 + [hblk, blk, pl.BlockSpec(cm.shape, lambda b, g, i: (0, 0, 0))],
        out_specs=[blk] * 6,
        out_shape=[jax.ShapeDtypeStruct((nb * S, D), F32)] * 6,
        scratch_shapes=[pltpu.VMEM((HB, HN, HN), F32)],
        compiler_params=_cparams(("parallel", "parallel", "arbitrary")),
    )(*ins, hs, do, cm)


def _ew(fn, name, ins, n_out, tm, out_dtype=F32):
    R, W = ins[0].shape[-2:]
    tm = min(tm, R)
    if R % tm:
        tm = R // 2
    assert R % tm == 0 and (tm % 16 == 0 or tm == R), (name, R, tm)

    def body(*refs):
        vals = fn(*[r[...] for r in refs[:len(ins)]])
        for r, v in zip(refs[len(ins):], vals):
            r[...] = v.astype(r.dtype)

    def spec(a):
        if a.ndim == 3:
            return pl.BlockSpec((a.shape[0], tm, W), lambda i: (0, i, 0))
        return pl.BlockSpec((tm, W), lambda i: (i, 0))

    return pl.pallas_call(
        body, name=name, grid=(R // tm,), in_specs=[spec(a) for a in ins],
        out_specs=[pl.BlockSpec((tm, W), lambda i: (i, 0))] * n_out,
        out_shape=[jax.ShapeDtypeStruct((R, W), out_dtype)] * n_out,
        compiler_params=_cparams(("parallel",)),
    )(*ins)


def _sum_slots(r):
    s = r[0]
    for j in range(1, r.shape[0]):
        s = s + r[j]
    return s


def _place():
    x, y, c = lax.axis_index("x"), lax.axis_index("y"), lax.axis_index("c")
    return x, y, c


def _flip(v, d):
    return 1 - v if d else v


CHIP_PEERS = ((1, 0), (0, 1), (1, 1))
DEV_PEERS = tuple((dx, dy, dc) for dx in (0, 1) for dy in (0, 1) for dc in (0, 1))[1:]


def _comm_call(name, ins, out_shapes, plan, n_rem, n_fwd=0):
    n_in = len(ins)

    def body(*refs):
        in_refs, out_refs = refs[:n_in], refs[n_in:n_in + len(out_shapes)]
        send_sems, recv_sems, loc_sems = refs[n_in + len(out_shapes):]
        loc, rem, *rest = plan(in_refs, out_refs, _place())
        fwd = rest[0] if rest else []
        assert len(rem) == n_rem and len(fwd) == n_fwd and len(loc) <= 2 * n_in, (name, len(loc), len(rem), len(fwd))

        def remote(i, s, d, peer):
            return pltpu.make_async_remote_copy(src_ref=s, dst_ref=d, send_sem=send_sems.at[i], recv_sem=recv_sems.at[i],
                                                device_id=peer, device_id_type=MESH)

        copies = [pltpu.make_async_copy(s, d, loc_sems.at[i]) for i, (s, d) in enumerate(loc)]
        rcopies = [remote(i, s, d, peer) for i, (s, d, peer) in enumerate(rem)]
        for cp in copies + rcopies:
            cp.start()
        landed = set()
        fcopies = []
        for i, (s, d, peer, k) in enumerate(fwd):
            if k not in landed:
                rcopies[k].wait_recv()
                landed.add(k)
            fcopies.append(remote(n_rem + i, s, d, peer))
            fcopies[-1].start()
        for k, cp in enumerate(rcopies):
            if k not in landed:
                cp.wait_recv()
        for cp in rcopies + fcopies:
            cp.wait_send()
        for cp in fcopies:
            cp.wait_recv()
        for cp in copies:
            cp.wait()

    return pl.pallas_call(
        body, name=name, in_specs=[ANY] * n_in, out_specs=[ANY] * len(out_shapes), out_shape=out_shapes,
        scratch_shapes=[pltpu.SemaphoreType.DMA((n_rem + n_fwd,)), pltpu.SemaphoreType.DMA((n_rem + n_fwd,)),
                        pltpu.SemaphoreType.DMA((2 * n_in,))],
        compiler_params=pltpu.CompilerParams(has_side_effects=True),
    )(*ins)


def _gather_plan(n_big, in_refs, out_refs, place):
    x, y, c = place
    chip, dev = 2 * x + y, 4 * x + 2 * y + c
    sib = (x, y, 1 - c)
    loc = [(in_refs[0], out_refs[0].at[dev])] + [(s, d.at[chip]) for s, d in zip(in_refs[1 + n_big:], out_refs[1 + n_big:])]
    rem = [(in_refs[0], out_refs[0].at[dev], (_flip(x, dx), _flip(y, dy), _flip(c, dc))) for dx, dy, dc in DEV_PEERS]
    fwd = []
    for s, d in zip(in_refs[1:1 + n_big], out_refs[1:1 + n_big]):
        for dx, dy in CHIP_PEERS:
            px, py = _flip(x, dx), _flip(y, dy)
            fwd.append((d.at[2 * px + py, c], d.at[2 * px + py, c], sib, len(rem)))
            rem.append((s.at[c], d.at[chip, c], (px, py, c)))
    for s, d in zip(in_refs[1 + n_big:], out_refs[1 + n_big:]):
        rem += [(s, d.at[chip], (_flip(x, dx), _flip(y, dy), c)) for dx, dy in CHIP_PEERS]
    return loc, rem, fwd


def _halve_plan(in_refs, out_refs, place):
    x, y, c = place
    return [], [(s.at[1 - c], d, (x, y, 1 - c)) for s, d in zip(in_refs, out_refs)]


def _join_plan(in_refs, out_refs, place):
    x, y, c = place
    return [], [(s, d, (x, y, 1 - c)) for s, d in zip(in_refs, out_refs)]


def _scatter_plan(n_all, in_refs, out_refs, place):
    x, y, c = place
    chip, dev = 2 * x + y, 4 * x + 2 * y + c
    loc, rem = [], []
    for s, d in zip(in_refs[:n_all], out_refs[:n_all]):
        loc.append((s.at[dev], d.at[dev]))
        for dx, dy, dc in DEV_PEERS:
            px, py, pc = _flip(x, dx), _flip(y, dy), _flip(c, dc)
            rem.append((s.at[4 * px + 2 * py + pc], d.at[dev], (px, py, pc)))
    for s, d in zip(in_refs[n_all:], out_refs[n_all:]):
        for dx, dy in CHIP_PEERS:
            px, py = _flip(x, dx), _flip(y, dy)
            rem.append((s.at[2 * px + py], d.at[chip], (px, py, c)))
    return loc, rem


def _bshape(a, nb):
    return a.reshape(nb, 1, a.shape[-1])


def _with_prev(cur, before, tiles_per_seq):
    first = pl.program_id(0) % tiles_per_seq == 0
    row0 = jnp.where(first, 0.0, before[before.shape[0] - 1:, :])
    rid = lax.broadcasted_iota(jnp.int32, cur.shape, 0)
    return jnp.where(rid == 0, row0, pltpu.roll(cur, 1, 0))


def _with_next(cur, after, tiles_per_seq):
    last = pl.program_id(0) % tiles_per_seq == tiles_per_seq - 1
    n = cur.shape[0]
    row_n = jnp.where(last, 0.0, after[0:1, :])
    rid = lax.broadcasted_iota(jnp.int32, cur.shape, 0)
    return jnp.where(rid == n - 1, row_n, pltpu.roll(cur, n - 1, 0))


def _local_step(x2d, tgt, mod, wmain, wlora, late_w, ck, w2, a2, small, nb, S, grads_hook):
    T = nb * S
    shift, scale, gate = (_bshape(mod[:, i * D:(i + 1) * D], nb) for i in range(3))
    G = jnp.asarray(np.arange(128)[:, None] == np.arange(D)[None, :] // HN, dtype=BF16)
    cm = jnp.asarray(_chunk_consts())
    ckp = jnp.pad(ck, ((0, 1), (0, 0)))
    zpad = jnp.zeros((64, D), F32)
    w2p = jnp.concatenate([w2, zpad], axis=0)
    a2p = jnp.concatenate([zpad, a2], axis=0)
    mu = small["rwkv_mu"]
    mu_r, mu_k, mu_v, mu_l = mu[:, 0:D], mu[:, D:2 * D], mu[:, 2 * D:3 * D], mu[:, 3 * D:]
    g4 = [mu_r, mu_k, mu_v, mu_l, small["rwkv_w0"], w2p, small["rwkv_a0"], a2p, small["rwkv_k_k"], small["rwkv_k_a"], G]
    g5 = [small["rwkv_gn_g"], small["rwkv_gn_b"], small["rwkv_r_k"], G]
    g3 = [small["conv_b"], small["conv_ln_g"], small["conv_ln_b"]]

    (h,), _, _ = _rows(lambda r, b, g: ([_s1(r[0], g[0], b[0], b[1])], [], []), "pre_fwd", T, S, 256,
                       [(x2d, D, 0)], [scale, shift], [small["norm_g"]], [(D, BF16)], [], [])
    if len(late_w) == 3:
        pm = _matmul(h, wmain, "nt", "proj_main", min(T, 1024), 1024, D)
        wco, wro, wo = late_w
    else:
        pm, *landed = _matmul(h, wmain, "nt", "proj_main", min(T, 1024), 1024, D, ride=late_w[0])
        wco, wro, wo = late_w[1](landed)
    plo = _matmul(h, wlora, "nt", "proj_lora", 512, LORA, D)
    uc = _conv_fwd(pm, ckp, T, S)
    (uo,), _, _ = _rows(lambda r, b, g: ([_s3(r[0], r[1], *g)], [], []), "conv_post_fwd", T, S, 256,
                        [(uc, D, 0), (pm, D, 2)], [], g3, [(D, BF16)], [], [])
    yc = _matmul(uo, wco, "nn", "conv_out", 512, 1024, D)
    rows4 = [(pm, D, 3), (pm, D, 4), (pm, D, 5), (plo, LORA, 0),
             (pm, D, 3, "prev"), (pm, D, 4, "prev"), (pm, D, 5, "prev"), (plo, LORA, 0, "prev")]
    tps4 = S // 128

    def shifted4(r, tps=tps4):
        return list(r[:4]) + [_with_prev(r[i], r[4 + i], tps) for i in range(4)]

    sc_in, _, _ = _rows(lambda r, b, g: (list(_s4(*shifted4(r, S // 256), *g)), [], []), "rwkv_pre_fwd", T, S, 256,
                        rows4, [], g4, [(D, F32)] * 6, [], [])
    o, hs = _scan_fwd(sc_in, cm, nb, S)
    rows5 = [(o, D, 0), (sc_in[0], D, 0), (sc_in[2], D, 0), (sc_in[3], D, 0), (pm, D, 6)]
    (o2,), _, _ = _rows(lambda r, b, g: ([_s5(*r, *g)], [], []), "rwkv_post_fwd", T, S, 256,
                        rows5, [], g5, [(D, BF16)], [], [])
    yr = _matmul(o2, wro, "nn", "rwkv_out", 512, 1024, D)
    rows6 = [(yc, D, 0), (yr, D, 0), (pm, D, 7), (pm, D, 8)]
    (m,), _, _ = _rows(lambda r, b, g: ([_s6(*r)], [], []), "merge_fwd", T, S, 256, rows6, [], [], [(D, BF16)], [], [])
    out = _matmul(m, wo, "nn", "out_proj", 512, 1024, D)

    def head(r, b, g):
        loss, (dx, dout, dgate, dfg) = jax.value_and_grad(_s7, argnums=(0, 1, 3, 4))(r[0], r[1], r[2], b[0], g[0])
        return [dx, dout], [dgate], [dfg, jnp.full((1, 128), loss, F32)]

    (dx_res, dout), (dgate,), (d_final_g, loss_v) = _rows(
        head, "head", T, S, 256, [(x2d, D, 0), (out, D, 0), (tgt, D, 0)], [gate], [small["final_g"]],
        [(D, F32), (D, BF16)], [D], [(1, D), (1, 128)])

    d_wo = _matmul(m, dout, "tn", "d_w_out", 512, 1024, min(T, 2048))
    dm = _matmul(dout, wo, "nt", "d_merge", 512, 1024, D)

    def merge_bwd(r, b, g):
        _, vjp = jax.vjp(_s6, *r[:4])
        dyc, dyr, dgc, dgr = vjp(r[4])
        return [dyc, dyr, dgc, dgr], [], []

    (dyc, dyr, dgc, dgr), _, _ = _rows(merge_bwd, "merge_bwd", T, S, 256, rows6 + [(dm, D, 0)], [], [],
                                       [(D, BF16), (D, BF16), (D, BF16), (D, BF16)], [], [])
    d_wco = _matmul(uo, dyc, "tn", "d_w_conv_out", 512, 1024, min(T, 2048))
    d_wro = _matmul(o2, dyr, "tn", "d_w_rwkv_out", 512, 1024, min(T, 2048))
    duo = _matmul(dyc, wco, "nt", "d_conv_act", 512, 1024, D)
    do2 = _matmul(dyr, wro, "nt", "d_rwkv_act", 512, 1024, D)

    def conv_post_bwd(r, b, g):
        _, vjp = jax.vjp(_s3, r[0], r[1], *g)
        duc, dog, dcb, dlg, dlb = vjp(r[2])
        return [duc, dog], [], [dcb, dlg, dlb]

    (duc, dcog), _, (d_cb, d_lg, d_lb) = _rows(conv_post_bwd, "conv_post_bwd", T, S, 256,
                                               [(uc, D, 0), (pm, D, 2), (duo, D, 0)], [], g3,
                                               [(D, F32), (D, BF16)], [], [(1, D)] * 3)
    dval, dgt, d_ckp = _conv_bwd(pm, duc, ckp, T, S)

    def rwkv_post_bwd(r, b, g):
        _, vjp = jax.vjp(lambda *z: _s5(*z, g[3]), *r[:5], *g[:3])
        res = vjp(r[5])
        return list(res[:5]), [], list(res[5:8])

    (do, dr_b, dk_b, dv_b, drog), _, (d_gg, d_gb, d_rk) = _rows(
        rwkv_post_bwd, "rwkv_post_bwd", T, S, 256, rows5 + [(do2, D, 0)], [], g5,
        [(D, F32)] * 4 + [(D, BF16)], [], [(1, D)] * 3)
    dsc = _scan_bwd(sc_in, hs, do, cm, nb, S)

    def rwkv_pre_bwd(r, b, g):
        _, vjp = jax.vjp(lambda *z: _s4(*z, g[10]), *shifted4(r), *g[:10])
        ct = (r[8] + r[14], r[9], r[10] + r[15], r[11] + r[16], r[12], r[13])
        res = vjp(ct)
        return list(res[:8]), [], list(res[8:18])

    rows4b = rows4 + [(a, D, 0) for a in dsc] + [(dr_b, D, 0), (dk_b, D, 0), (dv_b, D, 0)]
    gshapes = [(1, D), (1, D), (1, D), (1, LORA), (1, D), (LORA, D), (1, D), (LORA, D), (1, D), (1, D)]
    dts, _, gts = _rows(rwkv_pre_bwd, "rwkv_pre_bwd", T, S, 128, rows4b, [], g4,
                        [(D, BF16)] * 3 + [(LORA, BF16)] + [(D, BF16)] * 3 + [(LORA, BF16)], [], gshapes)
    dr0, dk0, dv0, dl0, dpr, dpk, dpv, dpl = dts
    d_mu_r, d_mu_k, d_mu_v, d_mu_l, d_w0, d_w2p, d_a0, d_a2p, d_kk, d_ka = gts

    def assemble(r, b, g):
        r = [z.astype(F32) for z in r]
        sh = [_with_next(r[10 + i], r[14 + i], tps4) for i in range(4)]
        main = jnp.concatenate([r[0], r[1], r[2], r[3] + sh[0], r[4] + sh[1], r[5] + sh[2], r[6], r[7], r[8]], axis=1)
        return [main, r[9] + sh[3]], [], []

    rows_a = [(dval, D, 0), (dgt, D, 0), (dcog, D, 0), (dr0, D, 0), (dk0, D, 0), (dv0, D, 0), (drog, D, 0), (dgc, D, 0),
              (dgr, D, 0), (dl0, LORA, 0), (dpr, D, 0), (dpk, D, 0), (dpv, D, 0), (dpl, LORA, 0),
              (dpr, D, 0, "next"), (dpk, D, 0, "next"), (dpv, D, 0, "next"), (dpl, LORA, 0, "next")]
    (dpm, dplo), _, _ = _rows(assemble, "assemble_dp", T, S, 128, rows_a, [], [], [(DMAIN, BF16), (LORA, BF16)], [], [])
    d_wmain = _matmul(h, dpm, "tn", "d_w_main", D // 2, 1024, min(T, 2048), out_t=True)
    d_wlora = _matmul(h, dplo, "tn", "d_w_lora", D // 2, LORA, min(T, 2048), out_t=True)
    RW = D // NCHIP
    hv = [d_wmain, d_wlora]
    hv += [g.reshape(NCHIP, 2, RW // 2, D).transpose(1, 0, 2, 3).reshape(2, NCHIP * RW // 2, D) for g in (d_wco, d_wro, d_wo)]
    hv += [g.reshape(-1, NCHIP, 2, RW // 2).transpose(2, 1, 0, 3).reshape(2, -1, RW // 2)
           for g in (d_ckp[:CW], d_w2p[:64], d_a2p[64:])]
    dh_m, *got_big = _matmul(dpm, wmain, "nn", "d_h_main", 512, 1024, 3072, ride=grads_hook(hv))
    dh_l = _matmul(dplo, wlora, "nn", "d_h_lora", 512, 1024, LORA)

    def pre_bwd(r, b, g):
        _, vjp = jax.vjp(_s1, r[0], g[0], b[0], b[1])
        dx, dg, dscale, dshift = vjp(r[1] + r[2])
        return [dx + r[3]], [dscale, dshift], [dg]

    (gx,), (dscale, dshift), (d_ng,) = _rows(pre_bwd, "pre_bwd", T, S, 256,
                                             [(x2d, D, 0), (dh_m, D, 0), (dh_l, D, 0), (dx_res, D, 0)],
                                             [scale, shift], [small["norm_g"]], [(D, F32)], [D, D], [(1, D)])
    dmod = jnp.concatenate([dshift, dscale, dgate], axis=-1).reshape(nb, 3 * D)
    d_small = {"norm_g": d_ng, "conv_b": d_cb, "conv_ln_g": d_lg, "conv_ln_b": d_lb,
               "rwkv_mu": jnp.concatenate([d_mu_r, d_mu_k, d_mu_v, d_mu_l], axis=1),
               "rwkv_w0": d_w0, "rwkv_a0": d_a0, "rwkv_k_k": d_kk, "rwkv_k_a": d_ka, "rwkv_r_k": d_rk,
               "rwkv_gn_g": d_gg, "rwkv_gn_b": d_gb, "final_g": d_final_g}
    return loss_v[0, 0], gx, dmod, got_big, d_small


def _step(a):
    nb, S, _ = a["x"].shape
    T = nb * S
    x_i, y_i, c_i = _place()
    chip = 2 * x_i + y_i
    w_in_t, m_w_in_t, v_w_in_t = (jnp.transpose(a[p + "w_in"][0]) for p in ("", "m_", "v_"))
    WS = w_in_t.shape[0]
    small_w = {n: a[n].reshape(1, sz) for n, sz in SMALL}

    def halves(t):
        return t.reshape(2, t.shape[0] // 2, t.shape[1])

    g_ins = [a["c"], halves(w_in_t.astype(BF16)), a["conv_k"][0], a["rwkv_w2"][0], a["rwkv_a2"][0]]
    g_out = [jax.ShapeDtypeStruct((NDEV,) + g_ins[0].shape, F32)]
    g_out += [jax.ShapeDtypeStruct((NCHIP,) + t.shape, t.dtype) for t in g_ins[1:]]
    c_all, win_g, ck_g, w2_g, a2_g = _comm_call(
        "gather_weights", g_ins, g_out, functools.partial(_gather_plan, 1), 7 + 3 * 4, 3)
    c_all = c_all.reshape(NDEV * nb, D)
    win_t = lax.dynamic_update_index_in_dim(win_g, g_ins[1], chip, 0).reshape(NCHIP * WS, D)
    late = [a[n][0].astype(BF16) for n in ("w_conv_out", "w_rwkv_out", "w_out")]

    def late_plan(in_refs, out_refs, place):
        x, y, c = place
        return [], [(s, d.at[2 * x + y], (_flip(x, dx), _flip(y, dy), c))
                    for s, d in zip(in_refs, out_refs) for dx, dy in CHIP_PEERS]

    def late_finish(landed):
        return [lax.dynamic_update_index_in_dim(g, own, chip, 0).reshape(D, D) for g, own in zip(landed, late)]

    late_w = ((late, [jax.ShapeDtypeStruct((NCHIP,) + t.shape, BF16) for t in late], late_plan, 9), late_finish)
    wmain = jnp.concatenate([win_t[:6 * D], win_t[6 * D + LORA:]], axis=0)
    wlora = win_t[6 * D:6 * D + LORA]
    ck = jnp.concatenate([ck_g[j] for j in range(NCHIP)], axis=1)
    w2 = jnp.concatenate([w2_g[j] for j in range(NCHIP)], axis=1)
    a2 = jnp.concatenate([a2_g[j] for j in range(NCHIP)], axis=1)

    ada_w = a["ada_w"][0]
    MW = ada_w.shape[1]
    ada_b_loc = lax.dynamic_slice(a["ada_b"], (0, chip * MW), (1, MW))

    def mod_body(c_ref, w_ref, b_ref, o_ref):
        o_ref[...] = _dot(_silu(c_ref[...]), w_ref[...], HI) + b_ref[...]

    modp = pl.pallas_call(mod_body, name="ada_mod", out_shape=jax.ShapeDtypeStruct((NDEV * nb, MW), F32),
                          compiler_params=_cparams())(c_all, ada_w, ada_b_loc)
    (mod_g,) = _comm_call("scatter_mod", [modp.reshape(NDEV, nb, MW)],
                          [jax.ShapeDtypeStruct((NDEV, nb, MW), F32)],
                          functools.partial(_scatter_plan, 1), 7)
    mod = mod_g.reshape(NCHIP, 2, nb, MW)
    mod = mod[:, 0].transpose(1, 0, 2).reshape(nb, NCHIP * MW)

    RW = D // NCHIP
    sh_s = []

    def own_half_plus(both, q):
        return [jnp.where(lax.axis_index("c") == 0, both[0], both[1]) + q]

    def grads_hook(hv):
        got_h = _comm_call("halve_grads", hv, [jax.ShapeDtypeStruct(t.shape[1:], F32) for t in hv], _halve_plan, len(hv))
        chip_part = [_ew(own_half_plus, "chip_sum_%d" % i, [hv[i], got_h[i]], 1, 128, BF16)[0] for i in range(len(hv))]
        d_win_h = jnp.concatenate([chip_part[0][:6 * D], chip_part[1], chip_part[0][6 * D:]], axis=0)
        sh_s.append(d_win_h.reshape(NCHIP, WS, D // 2))
        sh_s.extend(t.reshape(NCHIP, RW // 2, D) for t in chip_part[2:5])
        sh_s.extend(t.reshape(NCHIP, -1, RW // 2) for t in chip_part[5:])
        return (sh_s, [jax.ShapeDtypeStruct(t.shape, t.dtype) for t in sh_s], functools.partial(_scatter_plan, 0),
                3 * len(sh_s))

    loss_p, gx, dmod, got_big, d_small = _local_step(
        a["x"].reshape(T, D), a["loss_target"].reshape(T, D), mod, wmain, wlora, late_w, ck, w2, a2, small_w, nb, S,
        grads_hook)
    loss = lax.psum(loss_p, ("x", "y", "c"))

    d_small["ada_b"] = _colsum(dmod)
    small_vec = jnp.concatenate([d_small[n] for n, _ in SMALL], axis=1)
    dmod_s = dmod.reshape(nb, NCHIP, MW).transpose(1, 0, 2)
    dmod_s = jnp.repeat(dmod_s, 2, axis=0)
    small_s = jnp.broadcast_to(small_vec[None], (NDEV, 1, NSMALL))
    got = _comm_call("scatter_small", [dmod_s, small_s], [jax.ShapeDtypeStruct(t.shape, F32) for t in (dmod_s, small_s)],
                     functools.partial(_scatter_plan, 2), 14)
    dmod_all, small_all = got[0].reshape(NDEV * nb, MW), got[1].reshape(NDEV, NSMALL)

    def shard_sum(recv, sent):
        chip_i = 2 * lax.axis_index("x") + lax.axis_index("y")
        s = None
        for j in range(NCHIP):
            t = jnp.where(chip_i == j, sent[j], recv[j]).astype(F32)
            s = t if s is None else s + t
        return [s]

    fin = [_ew(shard_sum, "shard_sum_%d" % i, [t, sh_s[i]], 1, 128)[0] for i, t in enumerate(got_big)]
    oth = _comm_call("join_halves", fin, [jax.ShapeDtypeStruct(t.shape, F32) for t in fin], _join_plan, len(fin))

    outs = {}

    def upd_halves(name, mine, other):
        shp = a[name].shape
        R, W = 2 * mine.shape[0], mine.shape[1]
        tm = 128
        nh = R // 2 // tm

        def body(w_ref, m_ref, v_ref, f_ref, o_ref, g_ref, d_ref, m2_ref, v2_ref):
            g = jnp.where(pl.program_id(0) // nh == lax.axis_index("c"), f_ref[...], o_ref[...])
            g_ref[...] = g
            d_ref[...], m2_ref[...], v2_ref[...] = _adamw(w_ref[...], g, m_ref[...], v_ref[...])

        full = pl.BlockSpec((None, tm, W), lambda i: (0, i, 0))
        half = pl.BlockSpec((tm, W), lambda i: (i % nh, 0))
        assert shp == (1, R, W)
        outs[name] = pl.pallas_call(
            body, name="adamw_" + name, grid=(R // tm,), in_specs=[full] * 3 + [half] * 2, out_specs=[full] * 4,
            out_shape=[jax.ShapeDtypeStruct(shp, F32)] * 4, compiler_params=_cparams(("parallel",)),
        )(*[a[p + name] for p in ("", "m_", "v_")], mine, other)

    for name, f, o in zip(("w_conv_out", "w_rwkv_out", "w_out"), fin[1:4], oth[1:4]):
        upd_halves(name, f, o)

    tw = WS // 4

    def w_in_body(w_ref, m_ref, v_ref, f_ref, o_ref, g_ref, d_ref, m2_ref, v2_ref):
        first = lax.axis_index("c") == 0
        g = jnp.concatenate([jnp.where(first, f_ref[...], o_ref[...]), jnp.where(first, o_ref[...], f_ref[...])], axis=1)
        g_ref[...] = g
        d_ref[...], m2_ref[...], v2_ref[...] = _adamw(w_ref[...], g, m_ref[...], v_ref[...])

    full = pl.BlockSpec((tw, D), lambda i: (i, 0))
    half = pl.BlockSpec((tw, D // 2), lambda i: (i, 0))
    res = pl.pallas_call(
        w_in_body, name="adamw_w_in", grid=(WS // tw,), in_specs=[full] * 3 + [half] * 2, out_specs=[full] * 4,
        out_shape=[jax.ShapeDtypeStruct((WS, D), F32)] * 4, compiler_params=_cparams(("parallel",)),
    )(w_in_t, m_w_in_t, v_w_in_t, fin[0], oth[0])
    outs["w_in"] = [jnp.transpose(r)[None] for r in res]

    def upd(name, g):
        shp = a[name].shape
        ins = [a[p + name].reshape(g.shape) for p in ("", "m_", "v_")]
        res = _ew(lambda w_, m_, v_, g_: [g_, *_adamw(w_, g_, m_, v_)], "adamw_" + name, [*ins, g], 4, 128)
        outs[name] = [r.reshape(shp) for r in res]

    for name, f, o in zip(("conv_k", "rwkv_w2", "rwkv_a2"), fin[4:], oth[4:]):
        both = jnp.where(c_i == 0, jnp.stack([f, o]), jnp.stack([o, f]))
        upd(name, both.transpose(1, 0, 2).reshape(-1, RW))

    def adaw_body(c_ref, dm_ref, w_ref, m_ref, v_ref, g_ref, d_ref, m2_ref, v2_ref):
        g = _dot_tn(_silu(c_ref[...]), dm_ref[...], HI)
        g_ref[...] = g
        d_ref[...], m2_ref[...], v2_ref[...] = _adamw(w_ref[...], g, m_ref[...], v_ref[...])

    res = pl.pallas_call(adaw_body, name="adamw_ada_w", out_shape=[jax.ShapeDtypeStruct((D, MW), F32)] * 4,
                         compiler_params=_cparams())(c_all, dmod_all, ada_w, a["m_ada_w"][0], a["v_ada_w"][0])
    outs["ada_w"] = [r.reshape(a["ada_w"].shape) for r in res]

    wv, mv, vv = (jnp.concatenate([a[p + n].reshape(1, sz) for n, sz in SMALL], axis=1) for p in ("", "m_", "v_"))
    def small_fn(w_, m_, v_, gs):
        g = _sum_slots(gs)
        return [g, *_adamw(w_, g, m_, v_)]

    res = _ew(small_fn, "adamw_small", [wv, mv, vv, small_all.reshape(NDEV, 1, NSMALL)], 4, 8)
    off = 0
    for n, sz in SMALL:
        outs[n] = [r[:, off:off + sz].reshape(a[n].shape) for r in res]
        off += sz

    return (loss, gx.reshape(nb, S, D), *[outs[n][0] for n in WEIGHTS], *[outs[n][1] for n in WEIGHTS],
            *[outs[n][2] for n in WEIGHTS], *[outs[n][3] for n in WEIGHTS])


def _colsum(dmod):
    def body(d_ref, o_ref):
        o_ref[...] = jnp.sum(d_ref[...], axis=0, keepdims=True)
    return pl.pallas_call(body, name="ada_b_rowsum", out_shape=jax.ShapeDtypeStruct((1, dmod.shape[1]), F32),
                          compiler_params=_cparams())(dmod)


def kernel(x, c, ada_w, ada_b, norm_g, w_in, conv_k, conv_b, conv_ln_g, conv_ln_b, w_conv_out, rwkv_mu, rwkv_w0, rwkv_w2, rwkv_a0, rwkv_a2, rwkv_k_k, rwkv_k_a, rwkv_r_k, rwkv_gn_g, rwkv_gn_b, w_rwkv_out, w_out, final_g, loss_target, m_ada_w, m_ada_b, m_norm_g, m_w_in, m_conv_k, m_conv_b, m_conv_ln_g, m_conv_ln_b, m_w_conv_out, m_rwkv_mu, m_rwkv_w0, m_rwkv_w2, m_rwkv_a0, m_rwkv_a2, m_rwkv_k_k, m_rwkv_k_a, m_rwkv_r_k, m_rwkv_gn_g, m_rwkv_gn_b, m_w_rwkv_out, m_w_out, m_final_g, v_ada_w, v_ada_b, v_norm_g, v_w_in, v_conv_k, v_conv_b, v_conv_ln_g, v_conv_ln_b, v_w_conv_out, v_rwkv_mu, v_rwkv_w0, v_rwkv_w2, v_rwkv_a0, v_rwkv_a2, v_rwkv_k_k, v_rwkv_k_a, v_rwkv_r_k, v_rwkv_gn_g, v_rwkv_gn_b, v_w_rwkv_out, v_w_out, v_final_g):
    return _step(dict(locals()))
```

```python
import functools

import numpy as np
import jax
import jax.numpy as jnp
from jax import lax
from jax.experimental import pallas as pl
from jax.experimental.pallas import tpu as pltpu

F32 = jnp.float32
BF16 = jnp.bfloat16
HI = lax.Precision.HIGHEST
MESH = pl.DeviceIdType.MESH
ANY = pl.BlockSpec(memory_space=pl.ANY)

D = 1024
NH = 16
HN = 64
LORA = 128
DMAIN = 9 * D
CH = 64
CW = 31
NCHIP = 4
NDEV = 8
VMEM_LIMIT = 56 * 1024 * 1024

RMS_EPS = 1e-6
LN_EPS = 1e-5
GN_EPS = 64e-5
L2_EPS = 1e-12
ADAM_LR = 0.001
ADAM_B1 = 0.9
ADAM_B2 = 0.999
ADAM_EPS = 1e-08
ADAM_WD = 0.01
ADAM_STEP = 10

SMALL = (("ada_b", 3072), ("norm_g", 1024), ("conv_b", 1024), ("conv_ln_g", 1024), ("conv_ln_b", 1024),
         ("rwkv_mu", 3200), ("rwkv_w0", 1024), ("rwkv_a0", 1024), ("rwkv_k_k", 1024), ("rwkv_k_a", 1024),
         ("rwkv_r_k", 1024), ("rwkv_gn_g", 1024), ("rwkv_gn_b", 1024), ("final_g", 1024))
NSMALL = sum(n for _, n in SMALL)

WEIGHTS = ['ada_w', 'ada_b', 'norm_g', 'w_in', 'conv_k', 'conv_b', 'conv_ln_g', 'conv_ln_b', 'w_conv_out', 'rwkv_mu',
           'rwkv_w0', 'rwkv_w2', 'rwkv_a0', 'rwkv_a2', 'rwkv_k_k', 'rwkv_k_a', 'rwkv_r_k', 'rwkv_gn_g', 'rwkv_gn_b',
           'w_rwkv_out', 'w_out', 'final_g']


def _cparams(sem=None, **kw):
    if sem is not None:
        kw["dimension_semantics"] = sem
    return pltpu.CompilerParams(vmem_limit_bytes=VMEM_LIMIT, **kw)


def _dot(a, b, prec=None):
    return jnp.dot(a, b, preferred_element_type=F32, precision=prec)


def _dot_nt(a, b, prec=None):
    return lax.dot_general(a, b, (((1,), (1,)), ((), ())), preferred_element_type=F32, precision=prec)


def _dot_tn(a, b, prec=None):
    return lax.dot_general(a, b, (((0,), (0,)), ((), ())), preferred_element_type=F32, precision=prec)


def _pdot(f, a, b, p):
    if p == "hi":
        return f(a, b, HI)
    ah, bh = a.astype(BF16), b.astype(BF16)
    if p == "bf":
        return f(ah, bh)
    al, bl = (a - ah.astype(F32)).astype(BF16), (b - bh.astype(F32)).astype(BF16)
    return f(ah, bh) + (f(ah, bl) + f(al, bh))


P_SCORE = "b3"
P_INV = "bf"
P_APPLY = "bf"


def _sigmoid(z):
    return 1.0 / (1.0 + jnp.exp(-z))


def _silu(z):
    return z * _sigmoid(z)


def _matmul(a, b, mode, name, tm, tn, tk, ride=None, out_t=False):
    if mode == "nn":
        (M, K), N = a.shape, b.shape[1]
        a_spec = pl.BlockSpec((tm, tk), lambda j, i, k: (i, k))
        b_spec = pl.BlockSpec((tk, tn), lambda j, i, k: (k, j))
        f = _dot
    elif mode == "nt":
        (M, K), N = a.shape, b.shape[0]
        a_spec = pl.BlockSpec((tm, tk), lambda j, i, k: (i, k))
        b_spec = pl.BlockSpec((tn, tk), lambda j, i, k: (j, k))
        f = _dot_nt
    else:
        (K, M), N = a.shape, b.shape[1]
        a_spec = pl.BlockSpec((tk, tm), lambda j, i, k: (k, i))
        b_spec = pl.BlockSpec((tk, tn), lambda j, i, k: (k, j))
        f = _dot_tn
    assert M % tm == 0 and N % tn == 0 and K % tk == 0, (name, M, N, K)

    grid = (N // tn, M // tm, K // tk)
    o_spec = pl.BlockSpec((tm, tn), lambda j, i, k: (i, j))
    o_shape = jax.ShapeDtypeStruct((M, N), F32)

    scratch = []
    if out_t:
        o_spec = pl.BlockSpec((None, tn, tm), lambda j, i, k: (i, j, 0))
        o_shape = jax.ShapeDtypeStruct((M // tm, N, tm), F32)
        scratch = [pltpu.VMEM((tm, tn), F32)]

    def step(a_ref, b_ref, o_ref, *acc):
        acc_ref = acc[0] if out_t else o_ref

        @pl.when(pl.program_id(2) == 0)
        def _():
            acc_ref[...] = jnp.zeros_like(acc_ref)
        acc_ref[...] += f(a_ref[...], b_ref[...])
        if out_t:
            @pl.when(pl.program_id(2) == grid[2] - 1)
            def _():
                o_ref[...] = acc_ref[...].T

    if ride is None:
        return pl.pallas_call(
            step, name=name, grid=grid, in_specs=[a_spec, b_spec], out_specs=o_spec, out_shape=o_shape,
            scratch_shapes=scratch, compiler_params=_cparams(("parallel", "parallel", "arbitrary")),
        )(a, b)

    r_ins, r_shapes, plan, n_rem = ride
    n_ri, n_ro = len(r_ins), len(r_shapes)

    def body(a_ref, b_ref, *rest):
        r_in, o_ref, r_out = rest[:n_ri], rest[n_ri], rest[n_ri + 1:n_ri + 1 + n_ro]
        send_sems, recv_sems, *acc = rest[n_ri + 1 + n_ro:]
        loc, rem = plan(r_in, r_out, _place())
        assert not loc and len(rem) == n_rem, (name, len(loc), len(rem))
        copies = [pltpu.make_async_remote_copy(src_ref=s, dst_ref=d, send_sem=send_sems.at[i], recv_sem=recv_sems.at[i],
                                               device_id=peer, device_id_type=MESH) for i, (s, d, peer) in enumerate(rem)]
        pid = [pl.program_id(ax) for ax in range(3)]

        @pl.when((pid[0] == 0) & (pid[1] == 0) & (pid[2] == 0))
        def _():
            for cp in copies:
                cp.start()

        step(a_ref, b_ref, o_ref, *acc)

        @pl.when((pid[0] == grid[0] - 1) & (pid[1] == grid[1] - 1) & (pid[2] == grid[2] - 1))
        def _():
            for cp in copies:
                cp.wait_send()
            for cp in copies:
                cp.wait_recv()

    return pl.pallas_call(
        body, name=name, grid=grid, in_specs=[a_spec, b_spec] + [ANY] * n_ri, out_specs=[o_spec] + [ANY] * n_ro,
        out_shape=[o_shape] + list(r_shapes),
        scratch_shapes=[pltpu.SemaphoreType.DMA((n_rem,)), pltpu.SemaphoreType.DMA((n_rem,))] + scratch,
        compiler_params=_cparams(("arbitrary", "arbitrary", "arbitrary"), has_side_effects=True),
    )(a, b, *r_ins)


def _rows(fn, name, T, S, tm, rows, bpars, gpars, outs, baccs, gaccs):
    nb = T // S
    tps = S // tm
    n_r, n_b, n_g, n_o, n_ba, n_ga = len(rows), len(bpars), len(gpars), len(outs), len(baccs), len(gaccs)

    def body(*refs):
        r_refs = refs[:n_r]
        b_refs = refs[n_r:n_r + n_b]
        g_refs = refs[n_r + n_b:n_r + n_b + n_g]
        o_refs = refs[n_r + n_b + n_g:n_r + n_b + n_g + n_o]
        ba_refs = refs[n_r + n_b + n_g + n_o:n_r + n_b + n_g + n_o + n_ba]
        ga_refs = refs[n_r + n_b + n_g + n_o + n_ba:]
        i = pl.program_id(0)
        o_vals, ba_vals, ga_vals = fn([r[...] for r in r_refs], [r[...] for r in b_refs], [r[...] for r in g_refs])
        for r, v in zip(o_refs, o_vals):
            r[...] = v.astype(r.dtype)
        if n_ba:
            @pl.when(i % tps == 0)
            def _():
                for r in ba_refs:
                    r[...] = jnp.zeros_like(r)
            for r, v in zip(ba_refs, ba_vals):
                r[...] += v.reshape(r.shape)
        if n_ga:
            @pl.when(i == 0)
            def _():
                for r in ga_refs:
                    r[...] = jnp.zeros_like(r)
            for r, v in zip(ga_refs, ga_vals):
                r[...] += v.reshape(r.shape)

    def row_spec(arr, w, cb, kind="tile"):
        hr = 8 * (4 // arr.dtype.itemsize)
        if kind == "prev":
            return pl.BlockSpec((hr, w), lambda i: (jnp.maximum(i * (tm // hr) - 1, 0), cb))
        if kind == "next":
            return pl.BlockSpec((hr, w), lambda i: (jnp.minimum((i + 1) * (tm // hr), T // hr - 1), cb))
        return pl.BlockSpec((tm, w), lambda i: (i, cb))

    in_specs = [row_spec(*r) for r in rows]
    in_specs += [pl.BlockSpec((None, 1, p.shape[-1]), lambda i: (i // tps, 0, 0)) for p in bpars]
    in_specs += [pl.BlockSpec(p.shape, lambda i: (0, 0)) for p in gpars]
    out_specs = [pl.BlockSpec((tm, w), lambda i: (i, 0)) for w, _ in outs]
    out_specs += [pl.BlockSpec((None, 1, w), lambda i: (i // tps, 0, 0)) for w in baccs]
    out_specs += [pl.BlockSpec(s, lambda i: (0, 0)) for s in gaccs]
    out_shape = [jax.ShapeDtypeStruct((T, w), dt) for w, dt in outs]
    out_shape += [jax.ShapeDtypeStruct((nb, 1, w), F32) for w in baccs]
    out_shape += [jax.ShapeDtypeStruct(s, F32) for s in gaccs]
    res = pl.pallas_call(
        body, name=name, grid=(T // tm,), in_specs=in_specs, out_specs=out_specs, out_shape=out_shape,
        compiler_params=_cparams(("arbitrary",)),
    )(*[r[0] for r in rows], *bpars, *gpars)
    return res[:n_o], res[n_o:n_o + n_ba], res[n_o + n_ba:]


@jax.custom_vjp
def _gsum(z, G):
    zh = z.astype(BF16)
    zl = (z - zh.astype(F32)).astype(BF16)
    r = _dot_nt(zh, G) + _dot_nt(zl, G)
    rh = r.astype(BF16)
    rl = (r - rh.astype(F32)).astype(BF16)
    return _dot(rh, G) + _dot(rl, G)


def _dot3(x, w):
    xh = x.astype(BF16).astype(F32)
    wh = w.astype(BF16).astype(F32)
    xc = jnp.concatenate([xh, xh, x - xh], axis=1).astype(BF16)
    wc = jnp.concatenate([wh, w - wh, wh], axis=0).astype(BF16)
    return _dot(xc, wc)


_gsum.defvjp(lambda z, G: (_gsum(z, G), G), lambda G, ct: (_gsum(ct, G), jnp.zeros_like(G)))


def _s1(x, g, scale, shift):
    y = x * lax.rsqrt(jnp.mean(x * x, axis=-1, keepdims=True) + RMS_EPS)
    return (y * g) * (1.0 + scale) + shift


def _s3(uc, og, cb, lg, lb):
    u = uc + cb
    mu = jnp.mean(u, axis=-1, keepdims=True)
    d = u - mu
    var = jnp.mean(d * d, axis=-1, keepdims=True)
    y = d * lax.rsqrt(var + LN_EPS) * lg + lb
    return _silu(y) * _silu(og)


def _s4(r0, k0, v0, l0, pr, pk, pv, plo, mu_r, mu_k, mu_v, mu_l, w0, w2p, a0, a2p, k_k, k_a, G):
    r = r0 + mu_r * (pr - r0)
    k = k0 + mu_k * (pk - k0)
    v = v0 + mu_v * (pv - v0)
    lo = l0 + mu_l * (plo - l0)
    w_pre = w0 + _dot3(jnp.tanh(lo), w2p)
    lw = -np.float32(np.exp(-0.5)) * _sigmoid(w_pre)
    a = _sigmoid(a0 + _dot3(lo, a2p))
    kkr = k * k_k
    ss = _gsum(kkr * kkr, G)
    kk = kkr / jnp.maximum(jnp.sqrt(ss), L2_EPS)
    k2 = k * (1.0 + (a - 1.0) * k_a)
    return r, lw, k2, v, kk, kk * a


def _s5(o, r, k2, v, og, gg, gb, rk, G):
    mu = _gsum(o, G) * (1.0 / HN)
    d = o - mu
    var = _gsum(d * d, G) * (1.0 / HN)
    y = d * lax.rsqrt(var + GN_EPS) * gg + gb
    bonus = _gsum(r * k2 * rk, G)
    return (y + bonus * v) * _silu(og)


def _s6(yc, yr, gc, gr):
    return _sigmoid(gc) * yc + _sigmoid(gr) * yr


def _s7(x, out, tgt, gate, fg):
    x2 = x + gate * out
    y = x2 * lax.rsqrt(jnp.mean(x2 * x2, axis=-1, keepdims=True) + RMS_EPS) * fg
    e = y - tgt
    return 0.5 * jnp.sum(jnp.mean(e * e, axis=-1))


@jax.custom_vjp
def _solve_all(a_kbs, rhss, cm):
    return _solve_all_fwd(a_kbs, rhss, cm)[0]


def _solve_all_fwd(a_kbs, rhss, cm):
    H = range(len(a_kbs))
    xi = [cm[2] - cm[3] * a_kbs[j] for j in H]
    for lvl in range(1, 6):
        t = [_pdot(_dot, xi[j], cm[3 + lvl] * a_kbs[j], P_INV) for j in H]
        xi = [xi[j] - _pdot(_dot, t[j], xi[j], P_INV) for j in H]
    u = tuple(_pdot(_dot, xi[j], rhss[j], P_APPLY) for j in H)
    return u, (xi, u, cm)


def _solve_all_bwd(res, dus):
    xi, u, cm = res
    H = range(len(u))
    g = tuple(_pdot(_dot_tn, xi[j], dus[j], P_APPLY) for j in H)
    da = tuple(-(cm[1] * _pdot(_dot_nt, g[j], u[j], P_APPLY)) for j in H)
    return da, g, jnp.zeros_like(cm)


_solve_all.defvjp(_solve_all_fwd, _solve_all_bwd)


def _chunk(sts, r, lw, k, v, kk, b, cm):
    cum = _dot(cm[0], lw, HI)
    ein = jnp.exp(-cum)
    rt = r * jnp.exp(cum)
    kkt = kk * jnp.exp(cum - lw)
    kh = k * ein
    bh = b * ein
    ec = jnp.exp(jnp.sum(lw, axis=0, keepdims=True))
    khe = kh * ec
    bhe = bh * ec
    H = range(len(sts))
    tri, strict, eye = cm[0], cm[1], cm[2]
    rt, kkt, kh, bh, v, khe, bhe, ec = ([a[:, j * HN:(j + 1) * HN] for j in H] for a in (rt, kkt, kh, bh, v, khe, bhe, ec))
    lhs = [jnp.concatenate([kkt[j], rt[j]], axis=0) for j in H]
    rhs_s = [jnp.concatenate([bh[j], kh[j]], axis=0) for j in H]
    lh = [a.astype(BF16).astype(F32) for a in lhs]
    rh = [a.astype(BF16).astype(F32) for a in rhs_s]
    lc = [jnp.concatenate([lh[j], lh[j], lhs[j] - lh[j]], axis=1).astype(BF16) for j in H]
    rc = [jnp.concatenate([rh[j], rhs_s[j] - rh[j], rh[j]], axis=1).astype(BF16) for j in H]
    sc = [_dot_nt(lc[j], rc[j]) for j in H]
    a_kb = [strict * sc[j][:CH, :CH] for j in H]
    a_kk = [strict * sc[j][:CH, CH:] for j in H]
    a_rb = [tri * sc[j][CH:, :CH] for j in H]
    a_rk = [tri * sc[j][CH:, CH:] for j in H]
    ps = [_dot_nt(lhs[j].astype(BF16), sts[j].astype(BF16)) for j in H]
    pv = [_dot(jnp.concatenate([a_kk[j], a_rk[j]], axis=0).astype(BF16), v[j].astype(BF16)) for j in H]
    rhs = [ps[j][:CH] + pv[j][:CH] for j in H]
    o0 = [ps[j][CH:] + pv[j][CH:] for j in H]
    u = _solve_all(tuple(a_kb), tuple(rhs), cm)
    o = [o0[j] - _pdot(_dot, a_rb[j], u[j], P_APPLY) for j in H]
    st2 = [sts[j] * ec[j] + _dot_tn(jnp.concatenate([v[j], u[j]], axis=0).astype(BF16),
                                    jnp.concatenate([khe[j], -bhe[j]], axis=0).astype(BF16)) for j in H]
    return jnp.concatenate(o, axis=1), tuple(st2)


def _chunk_consts():
    t = np.arange(CH)[:, None]
    s = np.arange(CH)[None, :]
    mats = [(t >= s), (t > s), (t == s)]
    for lvl in range(6):
        sz = 1 << lvl
        mats.append(((t // sz) % 2 == 1) & ((s // sz) == (t // sz) - 1))
    mats.append(np.zeros((CH, CH), bool))
    return np.stack(mats).astype(np.float32)


def _adamw(w, g, m, v):
    m = ADAM_B1 * m + (1.0 - ADAM_B1) * g
    v = ADAM_B2 * v + (1.0 - ADAM_B2) * (g * g)
    m_hat = m / (1.0 - ADAM_B1 ** ADAM_STEP)
    v_hat = v / (1.0 - ADAM_B2 ** ADAM_STEP)
    delta = -ADAM_LR * (m_hat / (jnp.sqrt(v_hat) + ADAM_EPS) + ADAM_WD * w)
    return delta, m, v


CT = 128
RB = 64
WIN = RB + 32


def _conv_fwd(pm, ck, T, S):
    nb = T // S

    def body(val_ref, gate_ref, ck_ref, out_ref, ubuf):
        ubuf[0:32, :] = jnp.zeros((32, CT), F32)
        ubuf[32:, :] = val_ref[...] * _sigmoid(gate_ref[...])

        def blk(rb, carry):
            base = pl.multiple_of(rb * RB, RB)
            win = ubuf[pl.ds(base, WIN), :]
            acc = jnp.zeros((RB, CT), F32)
            for j in range(CW):
                acc = acc + ck_ref[j:j + 1, :] * pltpu.roll(win, (WIN - (2 + j)) % WIN, 0)[0:RB, :]
            out_ref[pl.ds(base, RB), :] = acc
            return carry

        lax.fori_loop(0, S // RB, blk, 0)

    return pl.pallas_call(
        body, name="conv_fwd", grid=(D // CT, nb),
        in_specs=[pl.BlockSpec((S, CT), lambda ct, b: (b, ct)),
                  pl.BlockSpec((S, CT), lambda ct, b: (b, D // CT + ct)),
                  pl.BlockSpec((32, CT), lambda ct, b: (0, ct))],
        out_specs=pl.BlockSpec((S, CT), lambda ct, b: (b, ct)),
        out_shape=jax.ShapeDtypeStruct((T, D), F32),
        scratch_shapes=[pltpu.VMEM((S + 32, CT), F32)],
        compiler_params=_cparams(("parallel", "arbitrary")),
    )(pm, pm, ck)


def _conv_bwd(pm, duc, ck, T, S):
    nb = T // S

    def body(val_ref, gate_ref, duc_ref, ck_ref, dval_ref, dgate_ref, dck_ref, ubuf, dbuf, acc):
        b = pl.program_id(1)
        ubuf[0:32, :] = jnp.zeros((32, CT), F32)
        ubuf[32:, :] = val_ref[...] * _sigmoid(gate_ref[...])
        dbuf[0:S, :] = duc_ref[...]
        dbuf[S:, :] = jnp.zeros((32, CT), F32)
        acc[...] = jnp.zeros_like(acc)

        def blk(rb, carry):
            base = pl.multiple_of(rb * RB, RB)
            uwin = ubuf[pl.ds(base, WIN), :]
            dwin = dbuf[pl.ds(base, WIN), :]
            dblk = dwin[0:RB, :]
            du = jnp.zeros((RB, CT), F32)
            for j in range(CW):
                du = du + ck_ref[j:j + 1, :] * pltpu.roll(dwin, (WIN - (CW - 1 - j)) % WIN, 0)[0:RB, :]
                ush = pltpu.roll(uwin, (WIN - (2 + j)) % WIN, 0)[0:RB, :]
                acc[j] += jnp.sum((dblk * ush).reshape(RB // 8, 8, CT), axis=0)
            val = val_ref[pl.ds(base, RB), :]
            sg = _sigmoid(gate_ref[pl.ds(base, RB), :])
            dval_ref[pl.ds(base, RB), :] = (du * sg).astype(BF16)
            dgate_ref[pl.ds(base, RB), :] = (du * val * sg * (1.0 - sg)).astype(BF16)
            return carry

        lax.fori_loop(0, S // RB, blk, 0)

        @pl.when(b == 0)
        def _():
            dck_ref[...] = jnp.zeros_like(dck_ref)
        for j in range(CW):
            dck_ref[j:j + 1, :] += jnp.sum(acc[j], axis=0, keepdims=True)

    return pl.pallas_call(
        body, name="conv_bwd", grid=(D // CT, nb),
        in_specs=[pl.BlockSpec((S, CT), lambda ct, b: (b, ct)),
                  pl.BlockSpec((S, CT), lambda ct, b: (b, D // CT + ct)),
                  pl.BlockSpec((S, CT), lambda ct, b: (b, ct)),
                  pl.BlockSpec((32, CT), lambda ct, b: (0, ct))],
        out_specs=[pl.BlockSpec((S, CT), lambda ct, b: (b, ct)),
                   pl.BlockSpec((S, CT), lambda ct, b: (b, ct)),
                   pl.BlockSpec((32, CT), lambda ct, b: (0, ct))],
        out_shape=[jax.ShapeDtypeStruct((T, D), BF16), jax.ShapeDtypeStruct((T, D), BF16),
                   jax.ShapeDtypeStruct((32, D), F32)],
        scratch_shapes=[pltpu.VMEM((S + 32, CT), F32), pltpu.VMEM((S + 32, CT), F32), pltpu.VMEM((32, 8, CT), F32)],
        compiler_params=_cparams(("parallel", "arbitrary")),
    )(pm, pm, duc, ck)


HB = 16


def _scan_fwd(ins, cm, nb, S):
    nc = S // CH
    blk = pl.BlockSpec((CH, HB * HN), lambda b, g, i: (b * nc + i, g))
    hblk = pl.BlockSpec((None, HB, None, HN, HN), lambda b, g, i: (b, g, i, 0, 0))

    def body(r_ref, lw_ref, k_ref, v_ref, kk_ref, b_ref, cm_ref, o_ref, hs_ref, st):
        @pl.when(pl.program_id(2) == 0)
        def _():
            st[...] = jnp.zeros_like(st)
        s0 = [st[j] for j in range(HB)]
        for j in range(HB):
            hs_ref[j] = s0[j]
        o, s1 = _chunk(s0, r_ref[...], lw_ref[...], k_ref[...], v_ref[...], kk_ref[...], b_ref[...], cm_ref[...])
        o_ref[...] = o
        for j in range(HB):
            st[j] = s1[j]

    return pl.pallas_call(
        body, name="scan_fwd", grid=(nb, NH // HB, nc),
        in_specs=[blk] * 6 + [pl.BlockSpec(cm.shape, lambda b, g, i: (0, 0, 0))],
        out_specs=[blk, hblk],
        out_shape=[jax.ShapeDtypeStruct((nb * S, D), F32), jax.ShapeDtypeStruct((nb, NH, nc, HN, HN), F32)],
        scratch_shapes=[pltpu.VMEM((HB, HN, HN), F32)],
        compiler_params=_cparams(("parallel", "parallel", "arbitrary")),
    )(*ins, cm)


def _scan_bwd(ins, hs, do, cm, nb, S):
    nc = S // CH
    blk = pl.BlockSpec((CH, HB * HN), lambda b, g, i: (b * nc + nc - 1 - i, g))
    hblk = pl.BlockSpec((None, HB, None, HN, HN), lambda b, g, i: (b, g, nc - 1 - i, 0, 0))

    def body(r_ref, lw_ref, k_ref, v_ref, kk_ref, b_ref, hs_ref, do_ref, cm_ref,
             dr_ref, dlw_ref, dk_ref, dv_ref, dkk_ref, db_ref, dst):
        @pl.when(pl.program_id(2) == 0)
        def _():
            dst[...] = jnp.zeros_like(dst)
        cmv = cm_ref[...]
        f = lambda s0, r, lw, k, v, kk, b: _chunk(s0, r, lw, k, v, kk, b, cmv)
        _, vjp = jax.vjp(f, [hs_ref[j] for j in range(HB)], r_ref[...], lw_ref[...], k_ref[...], v_ref[...],
                         kk_ref[...], b_ref[...])
        ds0, dr, dlw, dk, dv, dkk, db = vjp((do_ref[...], tuple(dst[j] for j in range(HB))))
        for j in range(HB):
            dst[j] = ds0[j]
        dr_ref[...] = dr
        dlw_ref[...] = dlw
        dk_ref[...] = dk
        dv_ref[...] = dv
        dkk_ref[...] = dkk
        db_ref[...] = db

    return pl.pallas_call(
        body, name="scan_bwd", grid=(nb, NH // HB, nc),
        in_specs=[blk] * 6 + [hblk, blk, pl.BlockSpec(cm.shape, lambda b, g, i: (0, 0, 0))],
        out_specs=[blk] * 6,
        out_shape=[jax.ShapeDtypeStruct((nb * S, D), F32)] * 6,
        scratch_shapes=[pltpu.VMEM((HB, HN, HN), F32)],
        compiler_params=_cparams(("parallel", "parallel", "arbitrary")),
    )(*ins, hs, do, cm)


def _ew(fn, name, ins, n_out, tm, out_dtype=F32):
    R, W = ins[0].shape[-2:]
    tm = min(tm, R)
    if R % tm:
        tm = R // 2
    assert R % tm == 0 and (tm % 16 == 0 or tm == R), (name, R, tm)

    def body(*refs):
        vals = fn(*[r[...] for r in refs[:len(ins)]])
        for r, v in zip(refs[len(ins):], vals):
            r[...] = v.astype(r.dtype)

    def spec(a):
        if a.ndim == 3:
            return pl.BlockSpec((a.shape[0], tm, W), lambda i: (0, i, 0))
        return pl.BlockSpec((tm, W), lambda i: (i, 0))

    return pl.pallas_call(
        body, name=name, grid=(R // tm,), in_specs=[spec(a) for a in ins],
        out_specs=[pl.BlockSpec((tm, W), lambda i: (i, 0))] * n_out,
        out_shape=[jax.ShapeDtypeStruct((R, W), out_dtype)] * n_out,
        compiler_params=_cparams(("parallel",)),
    )(*ins)


def _sum_slots(r):
    s = r[0]
    for j in range(1, r.shape[0]):
        s = s + r[j]
    return s


def _place():
    x, y, c = lax.axis_index("x"), lax.axis_index("y"), lax.axis_index("c")
    return x, y, c


def _flip(v, d):
    return 1 - v if d else v


CHIP_PEERS = ((1, 0), (0, 1), (1, 1))
DEV_PEERS = tuple((dx, dy, dc) for dx in (0, 1) for dy in (0, 1) for dc in (0, 1))[1:]


def _comm_call(name, ins, out_shapes, plan, n_rem, n_fwd=0):
    n_in = len(ins)

    def body(*refs):
        in_refs, out_refs = refs[:n_in], refs[n_in:n_in + len(out_shapes)]
        send_sems, recv_sems, loc_sems = refs[n_in + len(out_shapes):]
        loc, rem, *rest = plan(in_refs, out_refs, _place())
        fwd = rest[0] if rest else []
        assert len(rem) == n_rem and len(fwd) == n_fwd and len(loc) <= 2 * n_in, (name, len(loc), len(rem), len(fwd))

        def remote(i, s, d, peer):
            return pltpu.make_async_remote_copy(src_ref=s, dst_ref=d, send_sem=send_sems.at[i], recv_sem=recv_sems.at[i],
                                                device_id=peer, device_id_type=MESH)

        copies = [pltpu.make_async_copy(s, d, loc_sems.at[i]) for i, (s, d) in enumerate(loc)]
        rcopies = [remote(i, s, d, peer) for i, (s, d, peer) in enumerate(rem)]
        for cp in copies + rcopies:
            cp.start()
        landed = set()
        fcopies = []
        for i, (s, d, peer, k) in enumerate(fwd):
            if k not in landed:
                rcopies[k].wait_recv()
                landed.add(k)
            fcopies.append(remote(n_rem + i, s, d, peer))
            fcopies[-1].start()
        for k, cp in enumerate(rcopies):
            if k not in landed:
                cp.wait_recv()
        for cp in rcopies + fcopies:
            cp.wait_send()
        for cp in fcopies:
            cp.wait_recv()
        for cp in copies:
            cp.wait()

    return pl.pallas_call(
        body, name=name, in_specs=[ANY] * n_in, out_specs=[ANY] * len(out_shapes), out_shape=out_shapes,
        scratch_shapes=[pltpu.SemaphoreType.DMA((n_rem + n_fwd,)), pltpu.SemaphoreType.DMA((n_rem + n_fwd,)),
                        pltpu.SemaphoreType.DMA((2 * n_in,))],
        compiler_params=pltpu.CompilerParams(has_side_effects=True),
    )(*ins)


def _gather_plan(n_big, in_refs, out_refs, place):
    x, y, c = place
    chip, dev = 2 * x + y, 4 * x + 2 * y + c
    sib = (x, y, 1 - c)
    loc = [(in_refs[0], out_refs[0].at[dev])] + [(s, d.at[chip]) for s, d in zip(in_refs[1 + n_big:], out_refs[1 + n_big:])]
    rem = [(in_refs[0], out_refs[0].at[dev], (_flip(x, dx), _flip(y, dy), _flip(c, dc))) for dx, dy, dc in DEV_PEERS]
    fwd = []
    for s, d in zip(in_refs[1:1 + n_big], out_refs[1:1 + n_big]):
        for dx, dy in CHIP_PEERS:
            px, py = _flip(x, dx), _flip(y, dy)
            fwd.append((d.at[2 * px + py, c], d.at[2 * px + py, c], sib, len(rem)))
            rem.append((s.at[c], d.at[chip, c], (px, py, c)))
    for s, d in zip(in_refs[1 + n_big:], out_refs[1 + n_big:]):
        rem += [(s, d.at[chip], (_flip(x, dx), _flip(y, dy), c)) for dx, dy in CHIP_PEERS]
    return loc, rem, fwd


def _join_plan(in_refs, out_refs, place):
    x, y, c = place
    return [], [(s, d, (x, y, 1 - c)) for s, d in zip(in_refs, out_refs)]


def _scatter_plan(n_all, in_refs, out_refs, place):
    x, y, c = place
    chip, dev = 2 * x + y, 4 * x + 2 * y + c
    loc, rem = [], []
    for s, d in zip(in_refs[:n_all], out_refs[:n_all]):
        loc.append((s.at[dev], d.at[dev]))
        for dx, dy, dc in DEV_PEERS:
            px, py, pc = _flip(x, dx), _flip(y, dy), _flip(c, dc)
            rem.append((s.at[4 * px + 2 * py + pc], d.at[dev], (px, py, pc)))
    for s, d in zip(in_refs[n_all:], out_refs[n_all:]):
        for dx, dy in CHIP_PEERS:
            px, py = _flip(x, dx), _flip(y, dy)
            rem.append((s.at[2 * px + py], d.at[chip], (px, py, c)))
    return loc, rem


def _bshape(a, nb):
    return a.reshape(nb, 1, a.shape[-1])


def _with_prev(cur, before, tiles_per_seq):
    first = pl.program_id(0) % tiles_per_seq == 0
    row0 = jnp.where(first, 0.0, before[before.shape[0] - 1:, :])
    rid = lax.broadcasted_iota(jnp.int32, cur.shape, 0)
    return jnp.where(rid == 0, row0, pltpu.roll(cur, 1, 0))


def _with_next(cur, after, tiles_per_seq):
    last = pl.program_id(0) % tiles_per_seq == tiles_per_seq - 1
    n = cur.shape[0]
    row_n = jnp.where(last, 0.0, after[0:1, :])
    rid = lax.broadcasted_iota(jnp.int32, cur.shape, 0)
    return jnp.where(rid == n - 1, row_n, pltpu.roll(cur, n - 1, 0))


def _local_step(x2d, tgt, mod, wmain, wlora, late_w, ck, w2, a2, small, nb, S, grads_hook):
    T = nb * S
    shift, scale, gate = (_bshape(mod[:, i * D:(i + 1) * D], nb) for i in range(3))
    G = jnp.asarray(np.arange(128)[:, None] == np.arange(D)[None, :] // HN, dtype=BF16)
    cm = jnp.asarray(_chunk_consts())
    ckp = jnp.pad(ck, ((0, 1), (0, 0)))
    zpad = jnp.zeros((64, D), F32)
    w2p = jnp.concatenate([w2, zpad], axis=0)
    a2p = jnp.concatenate([zpad, a2], axis=0)
    mu = small["rwkv_mu"]
    mu_r, mu_k, mu_v, mu_l = mu[:, 0:D], mu[:, D:2 * D], mu[:, 2 * D:3 * D], mu[:, 3 * D:]
    g4 = [mu_r, mu_k, mu_v, mu_l, small["rwkv_w0"], w2p, small["rwkv_a0"], a2p, small["rwkv_k_k"], small["rwkv_k_a"], G]
    g5 = [small["rwkv_gn_g"], small["rwkv_gn_b"], small["rwkv_r_k"], G]
    g3 = [small["conv_b"], small["conv_ln_g"], small["conv_ln_b"]]

    (h,), _, _ = _rows(lambda r, b, g: ([_s1(r[0], g[0], b[0], b[1])], [], []), "pre_fwd", T, S, 256,
                       [(x2d, D, 0)], [scale, shift], [small["norm_g"]], [(D, BF16)], [], [])
    if len(late_w) == 3:
        pm = _matmul(h, wmain, "nt", "proj_main", min(T, 1024), 1024, D)
        wco, wro, wo = late_w
    else:
        pm, *landed = _matmul(h, wmain, "nt", "proj_main", min(T, 1024), 1024, D, ride=late_w[0])
        wco, wro, wo = late_w[1](landed)
    plo = _matmul(h, wlora, "nt", "proj_lora", 512, LORA, D)
    uc = _conv_fwd(pm, ckp, T, S)
    (uo,), _, _ = _rows(lambda r, b, g: ([_s3(r[0], r[1], *g)], [], []), "conv_post_fwd", T, S, 256,
                        [(uc, D, 0), (pm, D, 2)], [], g3, [(D, BF16)], [], [])
    yc = _matmul(uo, wco, "nn", "conv_out", 512, 1024, D)
    rows4 = [(pm, D, 3), (pm, D, 4), (pm, D, 5), (plo, LORA, 0),
             (pm, D, 3, "prev"), (pm, D, 4, "prev"), (pm, D, 5, "prev"), (plo, LORA, 0, "prev")]
    tps4 = S // 128

    def shifted4(r, tps=tps4):
        return list(r[:4]) + [_with_prev(r[i], r[4 + i], tps) for i in range(4)]

    sc_in, _, _ = _rows(lambda r, b, g: (list(_s4(*shifted4(r, S // 256), *g)), [], []), "rwkv_pre_fwd", T, S, 256,
                        rows4, [], g4, [(D, F32)] * 6, [], [])
    o, hs = _scan_fwd(sc_in, cm, nb, S)
    rows5 = [(o, D, 0), (sc_in[0], D, 0), (sc_in[2], D, 0), (sc_in[3], D, 0), (pm, D, 6)]
    (o2,), _, _ = _rows(lambda r, b, g: ([_s5(*r, *g)], [], []), "rwkv_post_fwd", T, S, 256,
                        rows5, [], g5, [(D, BF16)], [], [])
    yr = _matmul(o2, wro, "nn", "rwkv_out", 512, 1024, D)
    rows6 = [(yc, D, 0), (yr, D, 0), (pm, D, 7), (pm, D, 8)]
    (m,), _, _ = _rows(lambda r, b, g: ([_s6(*r)], [], []), "merge_fwd", T, S, 256, rows6, [], [], [(D, BF16)], [], [])
    out = _matmul(m, wo, "nn", "out_proj", 512, 1024, D)

    def head(r, b, g):
        loss, (dx, dout, dgate, dfg) = jax.value_and_grad(_s7, argnums=(0, 1, 3, 4))(r[0], r[1], r[2], b[0], g[0])
        return [dx, dout], [dgate], [dfg, jnp.full((1, 128), loss, F32)]

    (dx_res, dout), (dgate,), (d_final_g, loss_v) = _rows(
        head, "head", T, S, 256, [(x2d, D, 0), (out, D, 0), (tgt, D, 0)], [gate], [small["final_g"]],
        [(D, F32), (D, BF16)], [D], [(1, D), (1, 128)])

    d_wo = _matmul(m, dout, "tn", "d_w_out", 512, 1024, min(T, 2048))
    dm = _matmul(dout, wo, "nt", "d_merge", 512, 1024, D)

    def merge_bwd(r, b, g):
        _, vjp = jax.vjp(_s6, *r[:4])
        dyc, dyr, dgc, dgr = vjp(r[4])
        return [dyc, dyr, dgc, dgr], [], []

    (dyc, dyr, dgc, dgr), _, _ = _rows(merge_bwd, "merge_bwd", T, S, 256, rows6 + [(dm, D, 0)], [], [],
                                       [(D, BF16), (D, BF16), (D, BF16), (D, BF16)], [], [])
    d_wco = _matmul(uo, dyc, "tn", "d_w_conv_out", 512, 1024, min(T, 2048))
    d_wro = _matmul(o2, dyr, "tn", "d_w_rwkv_out", 512, 1024, min(T, 2048))
    duo = _matmul(dyc, wco, "nt", "d_conv_act", 512, 1024, D)
    do2 = _matmul(dyr, wro, "nt", "d_rwkv_act", 512, 1024, D)

    def conv_post_bwd(r, b, g):
        _, vjp = jax.vjp(_s3, r[0], r[1], *g)
        duc, dog, dcb, dlg, dlb = vjp(r[2])
        return [duc, dog], [], [dcb, dlg, dlb]

    (duc, dcog), _, (d_cb, d_lg, d_lb) = _rows(conv_post_bwd, "conv_post_bwd", T, S, 256,
                                               [(uc, D, 0), (pm, D, 2), (duo, D, 0)], [], g3,
                                               [(D, F32), (D, BF16)], [], [(1, D)] * 3)
    dval, dgt, d_ckp = _conv_bwd(pm, duc, ckp, T, S)

    def rwkv_post_bwd(r, b, g):
        _, vjp = jax.vjp(lambda *z: _s5(*z, g[3]), *r[:5], *g[:3])
        res = vjp(r[5])
        return list(res[:5]), [], list(res[5:8])

    (do, dr_b, dk_b, dv_b, drog), _, (d_gg, d_gb, d_rk) = _rows(
        rwkv_post_bwd, "rwkv_post_bwd", T, S, 256, rows5 + [(do2, D, 0)], [], g5,
        [(D, F32)] * 4 + [(D, BF16)], [], [(1, D)] * 3)
    dsc = _scan_bwd(sc_in, hs, do, cm, nb, S)

    def rwkv_pre_bwd(r, b, g):
        _, vjp = jax.vjp(lambda *z: _s4(*z, g[10]), *shifted4(r), *g[:10])
        ct = (r[8] + r[14], r[9], r[10] + r[15], r[11] + r[16], r[12], r[13])
        res = vjp(ct)
        return list(res[:8]), [], list(res[8:18])

    rows4b = rows4 + [(a, D, 0) for a in dsc] + [(dr_b, D, 0), (dk_b, D, 0), (dv_b, D, 0)]
    gshapes = [(1, D), (1, D), (1, D), (1, LORA), (1, D), (LORA, D), (1, D), (LORA, D), (1, D), (1, D)]
    dts, _, gts = _rows(rwkv_pre_bwd, "rwkv_pre_bwd", T, S, 128, rows4b, [], g4,
                        [(D, BF16)] * 3 + [(LORA, BF16)] + [(D, BF16)] * 3 + [(LORA, BF16)], [], gshapes)
    dr0, dk0, dv0, dl0, dpr, dpk, dpv, dpl = dts
    d_mu_r, d_mu_k, d_mu_v, d_mu_l, d_w0, d_w2p, d_a0, d_a2p, d_kk, d_ka = gts

    def assemble(r, b, g):
        r = [z.astype(F32) for z in r]
        sh = [_with_next(r[10 + i], r[14 + i], tps4) for i in range(4)]
        main = jnp.concatenate([r[0], r[1], r[2], r[3] + sh[0], r[4] + sh[1], r[5] + sh[2], r[6], r[7], r[8]], axis=1)
        return [main, r[9] + sh[3]], [], []

    rows_a = [(dval, D, 0), (dgt, D, 0), (dcog, D, 0), (dr0, D, 0), (dk0, D, 0), (dv0, D, 0), (drog, D, 0), (dgc, D, 0),
              (dgr, D, 0), (dl0, LORA, 0), (dpr, D, 0), (dpk, D, 0), (dpv, D, 0), (dpl, LORA, 0),
              (dpr, D, 0, "next"), (dpk, D, 0, "next"), (dpv, D, 0, "next"), (dpl, LORA, 0, "next")]
    (dpm, dplo), _, _ = _rows(assemble, "assemble_dp", T, S, 128, rows_a, [], [], [(DMAIN, BF16), (LORA, BF16)], [], [])
    RW = D // NCHIP
    c_i = lax.axis_index("c")
    halved = [g.reshape(NCHIP, 2, RW // 2, D).transpose(1, 0, 2, 3).reshape(2, NCHIP * RW // 2, D) for g in (d_wco, d_wro, d_wo)]
    halved += [g.reshape(-1, NCHIP, 2, RW // 2).transpose(2, 1, 0, 3).reshape(2, -1, RW // 2)
               for g in (d_ckp[:CW], d_w2p[:64], d_a2p[64:])]
    h_keep, h_send = (lax.dynamic_slice_in_dim(h, k * (D // 2), D // 2, axis=1) for k in (c_i, 1 - c_i))
    tk = min(T, 2048)
    send = [_matmul(h_send, dpm, "tn", "d_w_main_send", D // 2, 1024, tk, out_t=True)[0],
            _matmul(h_send, dplo, "tn", "d_w_lora_send", D // 2, LORA, tk, out_t=True)[0]]
    send += [lax.dynamic_index_in_dim(g, 1 - c_i, 0, keepdims=False) for g in halved]
    keep = [None, _matmul(h_keep, dplo, "tn", "d_w_lora_keep", D // 2, LORA, tk, out_t=True)[0]]
    keep += [lax.dynamic_index_in_dim(g, c_i, 0, keepdims=False) for g in halved]

    def to_sibling(in_refs, out_refs, place):
        x, y, c = place
        return [], [(s, d, (x, y, 1 - c)) for s, d in zip(in_refs, out_refs)]

    d_w_keep, *got_h = _matmul(h_keep, dpm, "tn", "d_w_main_keep", D // 2, 1024, tk, out_t=True,
                               ride=(send, [jax.ShapeDtypeStruct(t.shape, F32) for t in send], to_sibling, len(send)))
    keep[0] = d_w_keep[0]
    chip_part = [_ew(lambda p, q: [p + q], "chip_sum_%d" % i, [keep[i], got_h[i]], 1, 1024, BF16)[0] for i in range(len(keep))]
    dh_m, *got_big = _matmul(dpm, wmain, "nn", "d_h_main", 512, 1024, 3072, ride=grads_hook(chip_part))
    dh_l = _matmul(dplo, wlora, "nn", "d_h_lora", 512, 1024, LORA)

    def pre_bwd(r, b, g):
        _, vjp = jax.vjp(_s1, r[0], g[0], b[0], b[1])
        dx, dg, dscale, dshift = vjp(r[1] + r[2])
        return [dx + r[3]], [dscale, dshift], [dg]

    (gx,), (dscale, dshift), (d_ng,) = _rows(pre_bwd, "pre_bwd", T, S, 256,
                                             [(x2d, D, 0), (dh_m, D, 0), (dh_l, D, 0), (dx_res, D, 0)],
                                             [scale, shift], [small["norm_g"]], [(D, F32)], [D, D], [(1, D)])
    dmod = jnp.concatenate([dshift, dscale, dgate], axis=-1).reshape(nb, 3 * D)
    d_small = {"norm_g": d_ng, "conv_b": d_cb, "conv_ln_g": d_lg, "conv_ln_b": d_lb,
               "rwkv_mu": jnp.concatenate([d_mu_r, d_mu_k, d_mu_v, d_mu_l], axis=1),
               "rwkv_w0": d_w0, "rwkv_a0": d_a0, "rwkv_k_k": d_kk, "rwkv_k_a": d_ka, "rwkv_r_k": d_rk,
               "rwkv_gn_g": d_gg, "rwkv_gn_b": d_gb, "final_g": d_final_g}
    return loss_v[0, 0], gx, dmod, got_big, d_small


def _step(a):
    nb, S, _ = a["x"].shape
    T = nb * S
    x_i, y_i, c_i = _place()
    chip = 2 * x_i + y_i
    w_in_t, m_w_in_t, v_w_in_t = (jnp.transpose(a[p + "w_in"][0]) for p in ("", "m_", "v_"))
    WS = w_in_t.shape[0]
    small_w = {n: a[n].reshape(1, sz) for n, sz in SMALL}

    def halves(t):
        return t.reshape(2, t.shape[0] // 2, t.shape[1])

    g_ins = [a["c"], halves(w_in_t.astype(BF16)), a["conv_k"][0], a["rwkv_w2"][0], a["rwkv_a2"][0]]
    g_out = [jax.ShapeDtypeStruct((NDEV,) + g_ins[0].shape, F32)]
    g_out += [jax.ShapeDtypeStruct((NCHIP,) + t.shape, t.dtype) for t in g_ins[1:]]
    c_all, win_g, ck_g, w2_g, a2_g = _comm_call(
        "gather_weights", g_ins, g_out, functools.partial(_gather_plan, 1), 7 + 3 * 4, 3)
    c_all = c_all.reshape(NDEV * nb, D)
    win_t = lax.dynamic_update_index_in_dim(win_g, g_ins[1], chip, 0).reshape(NCHIP * WS, D)
    late = [a[n][0].astype(BF16) for n in ("w_conv_out", "w_rwkv_out", "w_out")]

    def late_plan(in_refs, out_refs, place):
        x, y, c = place
        return [], [(s, d.at[2 * x + y], (_flip(x, dx), _flip(y, dy), c))
                    for s, d in zip(in_refs, out_refs) for dx, dy in CHIP_PEERS]

    def late_finish(landed):
        return [lax.dynamic_update_index_in_dim(g, own, chip, 0).reshape(D, D) for g, own in zip(landed, late)]

    late_w = ((late, [jax.ShapeDtypeStruct((NCHIP,) + t.shape, BF16) for t in late], late_plan, 9), late_finish)
    wmain = jnp.concatenate([win_t[:6 * D], win_t[6 * D + LORA:]], axis=0)
    wlora = win_t[6 * D:6 * D + LORA]
    ck = jnp.concatenate([ck_g[j] for j in range(NCHIP)], axis=1)
    w2 = jnp.concatenate([w2_g[j] for j in range(NCHIP)], axis=1)
    a2 = jnp.concatenate([a2_g[j] for j in range(NCHIP)], axis=1)

    ada_w = a["ada_w"][0]
    MW = ada_w.shape[1]
    ada_b_loc = lax.dynamic_slice(a["ada_b"], (0, chip * MW), (1, MW))

    def mod_body(c_ref, w_ref, b_ref, o_ref):
        o_ref[...] = _dot(_silu(c_ref[...]), w_ref[...], HI) + b_ref[...]

    modp = pl.pallas_call(mod_body, name="ada_mod", out_shape=jax.ShapeDtypeStruct((NDEV * nb, MW), F32),
                          compiler_params=_cparams())(c_all, ada_w, ada_b_loc)
    (mod_g,) = _comm_call("scatter_mod", [modp.reshape(NDEV, nb, MW)],
                          [jax.ShapeDtypeStruct((NDEV, nb, MW), F32)],
                          functools.partial(_scatter_plan, 1), 7)
    mod = mod_g.reshape(NCHIP, 2, nb, MW)
    mod = mod[:, 0].transpose(1, 0, 2).reshape(nb, NCHIP * MW)

    RW = D // NCHIP
    sh_s = []

    def grads_hook(chip_part):
        d_win_h = jnp.concatenate([chip_part[0][:6 * D], chip_part[1], chip_part[0][6 * D:]], axis=0)
        sh_s.append(d_win_h.reshape(NCHIP, WS, D // 2))
        sh_s.extend(t.reshape(NCHIP, RW // 2, D) for t in chip_part[2:5])
        sh_s.extend(t.reshape(NCHIP, -1, RW // 2) for t in chip_part[5:])
        return (sh_s, [jax.ShapeDtypeStruct(t.shape, t.dtype) for t in sh_s], functools.partial(_scatter_plan, 0),
                3 * len(sh_s))

    loss_p, gx, dmod, got_big, d_small = _local_step(
        a["x"].reshape(T, D), a["loss_target"].reshape(T, D), mod, wmain, wlora, late_w, ck, w2, a2, small_w, nb, S,
        grads_hook)
    loss = lax.psum(loss_p, ("x", "y", "c"))

    d_small["ada_b"] = _colsum(dmod)
    small_vec = jnp.concatenate([d_small[n] for n, _ in SMALL], axis=1)
    dmod_s = dmod.reshape(nb, NCHIP, MW).transpose(1, 0, 2)
    dmod_s = jnp.repeat(dmod_s, 2, axis=0)
    small_s = jnp.broadcast_to(small_vec[None], (NDEV, 1, NSMALL))
    got = _comm_call("scatter_small", [dmod_s, small_s], [jax.ShapeDtypeStruct(t.shape, F32) for t in (dmod_s, small_s)],
                     functools.partial(_scatter_plan, 2), 14)
    dmod_all, small_all = got[0].reshape(NDEV * nb, MW), got[1].reshape(NDEV, NSMALL)

    def shard_sum(recv, sent):
        chip_i = 2 * lax.axis_index("x") + lax.axis_index("y")
        s = None
        for j in range(NCHIP):
            t = jnp.where(chip_i == j, sent[j], recv[j]).astype(F32)
            s = t if s is None else s + t
        return [s]

    fin = [_ew(shard_sum, "shard_sum_%d" % i, [t, sh_s[i]], 1, 128)[0] for i, t in enumerate(got_big)]
    oth = _comm_call("join_halves", fin, [jax.ShapeDtypeStruct(t.shape, F32) for t in fin], _join_plan, len(fin))

    outs = {}

    def upd_halves(name, mine, other):
        shp = a[name].shape
        R, W = 2 * mine.shape[0], mine.shape[1]
        tm = 128
        nh = R // 2 // tm

        def body(w_ref, m_ref, v_ref, f_ref, o_ref, g_ref, d_ref, m2_ref, v2_ref):
            g = jnp.where(pl.program_id(0) // nh == lax.axis_index("c"), f_ref[...], o_ref[...])
            g_ref[...] = g
            d_ref[...], m2_ref[...], v2_ref[...] = _adamw(w_ref[...], g, m_ref[...], v_ref[...])

        full = pl.BlockSpec((None, tm, W), lambda i: (0, i, 0))
        half = pl.BlockSpec((tm, W), lambda i: (i % nh, 0))
        assert shp == (1, R, W)
        outs[name] = pl.pallas_call(
            body, name="adamw_" + name, grid=(R // tm,), in_specs=[full] * 3 + [half] * 2, out_specs=[full] * 4,
            out_shape=[jax.ShapeDtypeStruct(shp, F32)] * 4, compiler_params=_cparams(("parallel",)),
        )(*[a[p + name] for p in ("", "m_", "v_")], mine, other)

    for name, f, o in zip(("w_conv_out", "w_rwkv_out", "w_out"), fin[1:4], oth[1:4]):
        upd_halves(name, f, o)

    tw = WS // 4

    def w_in_body(w_ref, m_ref, v_ref, f_ref, o_ref, g_ref, d_ref, m2_ref, v2_ref):
        first = lax.axis_index("c") == 0
        g = jnp.concatenate([jnp.where(first, f_ref[...], o_ref[...]), jnp.where(first, o_ref[...], f_ref[...])], axis=1)
        g_ref[...] = g
        d_ref[...], m2_ref[...], v2_ref[...] = _adamw(w_ref[...], g, m_ref[...], v_ref[...])

    full = pl.BlockSpec((tw, D), lambda i: (i, 0))
    half = pl.BlockSpec((tw, D // 2), lambda i: (i, 0))
    res = pl.pallas_call(
        w_in_body, name="adamw_w_in", grid=(WS // tw,), in_specs=[full] * 3 + [half] * 2, out_specs=[full] * 4,
        out_shape=[jax.ShapeDtypeStruct((WS, D), F32)] * 4, compiler_params=_cparams(("parallel",)),
    )(w_in_t, m_w_in_t, v_w_in_t, fin[0], oth[0])
    outs["w_in"] = [jnp.transpose(r)[None] for r in res]

    def upd(name, g):
        shp = a[name].shape
        ins = [a[p + name].reshape(g.shape) for p in ("", "m_", "v_")]
        res = _ew(lambda w_, m_, v_, g_: [g_, *_adamw(w_, g_, m_, v_)], "adamw_" + name, [*ins, g], 4, 128)
        outs[name] = [r.reshape(shp) for r in res]

    for name, f, o in zip(("conv_k", "rwkv_w2", "rwkv_a2"), fin[4:], oth[4:]):
        both = jnp.where(c_i == 0, jnp.stack([f, o]), jnp.stack([o, f]))
        upd(name, both.transpose(1, 0, 2).reshape(-1, RW))

    def adaw_body(c_ref, dm_ref, w_ref, m_ref, v_ref, g_ref, d_ref, m2_ref, v2_ref):
        g = _dot_tn(_silu(c_ref[...]), dm_ref[...], HI)
        g_ref[...] = g
        d_ref[...], m2_ref[...], v2_ref[...] = _adamw(w_ref[...], g, m_ref[...], v_ref[...])

    res = pl.pallas_call(adaw_body, name="adamw_ada_w", out_shape=[jax.ShapeDtypeStruct((D, MW), F32)] * 4,
                         compiler_params=_cparams())(c_all, dmod_all, ada_w, a["m_ada_w"][0], a["v_ada_w"][0])
    outs["ada_w"] = [r.reshape(a["ada_w"].shape) for r in res]

    wv, mv, vv = (jnp.concatenate([a[p + n].reshape(1, sz) for n, sz in SMALL], axis=1) for p in ("", "m_", "v_"))
    def small_fn(w_, m_, v_, gs):
        g = _sum_slots(gs)
        return [g, *_adamw(w_, g, m_, v_)]

    res = _ew(small_fn, "adamw_small", [wv, mv, vv, small_all.reshape(NDEV, 1, NSMALL)], 4, 8)
    off = 0
    for n, sz in SMALL:
        outs[n] = [r[:, off:off + sz].reshape(a[n].shape) for r in res]
        off += sz

    return (loss, gx.reshape(nb, S, D), *[outs[n][0] for n in WEIGHTS], *[outs[n][1] for n in WEIGHTS],
            *[outs[n][2] for n in WEIGHTS], *[outs[n][3] for n in WEIGHTS])


def _colsum(dmod):
    def body(d_ref, o_ref):
        o_ref[...] = jnp.sum(d_ref[...], axis=0, keepdims=True)
    return pl.pallas_call(body, name="ada_b_rowsum", out_shape=jax.ShapeDtypeStruct((1, dmod.shape[1]), F32),
                          compiler_params=_cparams())(dmod)


def kernel(x, c, ada_w, ada_b, norm_g, w_in, conv_k, conv_b, conv_ln_g, conv_ln_b, w_conv_out, rwkv_mu, rwkv_w0, rwkv_w2, rwkv_a0, rwkv_a2, rwkv_k_k, rwkv_k_a, rwkv_r_k, rwkv_gn_g, rwkv_gn_b, w_rwkv_out, w_out, final_g, loss_target, m_ada_w, m_ada_b, m_norm_g, m_w_in, m_conv_k, m_conv_b, m_conv_ln_g, m_conv_ln_b, m_w_conv_out, m_rwkv_mu, m_rwkv_w0, m_rwkv_w2, m_rwkv_a0, m_rwkv_a2, m_rwkv_k_k, m_rwkv_k_a, m_rwkv_r_k, m_rwkv_gn_g, m_rwkv_gn_b, m_w_rwkv_out, m_w_out, m_final_g, v_ada_w, v_ada_b, v_norm_g, v_w_in, v_conv_k, v_conv_b, v_conv_ln_g, v_conv_ln_b, v_w_conv_out, v_rwkv_mu, v_rwkv_w0, v_rwkv_w2, v_rwkv_a0, v_rwkv_a2, v_rwkv_k_k, v_rwkv_k_a, v_rwkv_r_k, v_rwkv_gn_g, v_rwkv_gn_b, v_w_rwkv_out, v_w_out, v_final_g):
    return _step(dict(locals()))
```

```python
import functools

import numpy as np
import jax
import jax.numpy as jnp
from jax import lax
from jax.experimental import pallas as pl
from jax.experimental.pallas import tpu as pltpu

F32 = jnp.float32
BF16 = jnp.bfloat16
HI = lax.Precision.HIGHEST
MESH = pl.DeviceIdType.MESH
ANY = pl.BlockSpec(memory_space=pl.ANY)

D = 1024
NH = 16
HN = 64
LORA = 128
DMAIN = 9 * D
CH = 64
CW = 31
NCHIP = 4
NDEV = 8
VMEM_LIMIT = 56 * 1024 * 1024

RMS_EPS = 1e-6
LN_EPS = 1e-5
GN_EPS = 64e-5
L2_EPS = 1e-12
ADAM_LR = 0.001
ADAM_B1 = 0.9
ADAM_B2 = 0.999
ADAM_EPS = 1e-08
ADAM_WD = 0.01
ADAM_STEP = 10

SMALL = (("ada_b", 3072), ("norm_g", 1024), ("conv_b", 1024), ("conv_ln_g", 1024), ("conv_ln_b", 1024),
         ("rwkv_mu", 3200), ("rwkv_w0", 1024), ("rwkv_a0", 1024), ("rwkv_k_k", 1024), ("rwkv_k_a", 1024),
         ("rwkv_r_k", 1024), ("rwkv_gn_g", 1024), ("rwkv_gn_b", 1024), ("final_g", 1024))
NSMALL = sum(n for _, n in SMALL)

WEIGHTS = ['ada_w', 'ada_b', 'norm_g', 'w_in', 'conv_k', 'conv_b', 'conv_ln_g', 'conv_ln_b', 'w_conv_out', 'rwkv_mu',
           'rwkv_w0', 'rwkv_w2', 'rwkv_a0', 'rwkv_a2', 'rwkv_k_k', 'rwkv_k_a', 'rwkv_r_k', 'rwkv_gn_g', 'rwkv_gn_b',
           'w_rwkv_out', 'w_out', 'final_g']


def _cparams(sem=None, **kw):
    if sem is not None:
        kw["dimension_semantics"] = sem
    return pltpu.CompilerParams(vmem_limit_bytes=VMEM_LIMIT, **kw)


def _dot(a, b, prec=None):
    return jnp.dot(a, b, preferred_element_type=F32, precision=prec)


def _dot_nt(a, b, prec=None):
    return lax.dot_general(a, b, (((1,), (1,)), ((), ())), preferred_element_type=F32, precision=prec)


def _dot_tn(a, b, prec=None):
    return lax.dot_general(a, b, (((0,), (0,)), ((), ())), preferred_element_type=F32, precision=prec)


def _pdot(f, a, b, p):
    if p == "hi":
        return f(a, b, HI)
    ah, bh = a.astype(BF16), b.astype(BF16)
    if p == "bf":
        return f(ah, bh)
    al, bl = (a - ah.astype(F32)).astype(BF16), (b - bh.astype(F32)).astype(BF16)
    return f(ah, bh) + (f(ah, bl) + f(al, bh))


P_SCORE = "b3"
P_INV = "bf"
P_APPLY = "bf"


def _sigmoid(z):
    return 1.0 / (1.0 + jnp.exp(-z))


def _silu(z):
    return z * _sigmoid(z)


def _matmul(a, b, mode, name, tm, tn, tk, ride=None, out_t=False, b_rows=None):
    if mode == "nn":
        (M, K), N = a.shape, b.shape[1]
        a_spec = pl.BlockSpec((tm, tk), lambda j, i, k: (i, k))
        b_spec = pl.BlockSpec((tk, tn), lambda j, i, k: (k, j))
        if b_rows is not None:
            assert b_rows[0] == K
            b_spec = pl.BlockSpec((pl.Element(tk), pl.Element(tn)), lambda j, i, k: (b_rows[1](k, tk), j * tn))
        f = _dot
    elif mode == "nt":
        (M, K), N = a.shape, b.shape[0]
        a_spec = pl.BlockSpec((tm, tk), lambda j, i, k: (i, k))
        b_spec = pl.BlockSpec((tn, tk), lambda j, i, k: (j, k))
        if b_rows is not None:
            N = b_rows[0]
            b_spec = pl.BlockSpec((pl.Element(tn), pl.Element(tk)), lambda j, i, k: (b_rows[1](j, tn), k * tk))
        f = _dot_nt
    else:
        (K, M), N = a.shape, b.shape[1]
        a_spec = pl.BlockSpec((tk, tm), lambda j, i, k: (k, i))
        b_spec = pl.BlockSpec((tk, tn), lambda j, i, k: (k, j))
        f = _dot_tn
    assert M % tm == 0 and N % tn == 0 and K % tk == 0, (name, M, N, K)

    grid = (N // tn, M // tm, K // tk)
    o_spec = pl.BlockSpec((tm, tn), lambda j, i, k: (i, j))
    o_shape = jax.ShapeDtypeStruct((M, N), F32)

    scratch = []
    if out_t:
        o_spec = pl.BlockSpec((None, tn, tm), lambda j, i, k: (i, j, 0))
        o_shape = jax.ShapeDtypeStruct((M // tm, N, tm), F32)
        scratch = [pltpu.VMEM((tm, tn), F32)]

    def step(a_ref, b_ref, o_ref, *acc):
        acc_ref = acc[0] if out_t else o_ref

        @pl.when(pl.program_id(2) == 0)
        def _():
            acc_ref[...] = jnp.zeros_like(acc_ref)
        acc_ref[...] += f(a_ref[...], b_ref[...])
        if out_t:
            @pl.when(pl.program_id(2) == grid[2] - 1)
            def _():
                o_ref[...] = acc_ref[...].T

    if ride is None:
        return pl.pallas_call(
            step, name=name, grid=grid, in_specs=[a_spec, b_spec], out_specs=o_spec, out_shape=o_shape,
            scratch_shapes=scratch, compiler_params=_cparams(("parallel", "parallel", "arbitrary")),
        )(a, b)

    r_ins, r_shapes, plan, n_rem = ride
    n_ri, n_ro = len(r_ins), len(r_shapes)

    def body(a_ref, b_ref, *rest):
        r_in, o_ref, r_out = rest[:n_ri], rest[n_ri], rest[n_ri + 1:n_ri + 1 + n_ro]
        send_sems, recv_sems, *acc = rest[n_ri + 1 + n_ro:]
        loc, rem = plan(r_in, r_out, _place())
        assert not loc and len(rem) == n_rem, (name, len(loc), len(rem))
        copies = [pltpu.make_async_remote_copy(src_ref=s, dst_ref=d, send_sem=send_sems.at[i], recv_sem=recv_sems.at[i],
                                               device_id=peer, device_id_type=MESH) for i, (s, d, peer) in enumerate(rem)]
        pid = [pl.program_id(ax) for ax in range(3)]

        @pl.when((pid[0] == 0) & (pid[1] == 0) & (pid[2] == 0))
        def _():
            for cp in copies:
                cp.start()

        step(a_ref, b_ref, o_ref, *acc)

        @pl.when((pid[0] == grid[0] - 1) & (pid[1] == grid[1] - 1) & (pid[2] == grid[2] - 1))
        def _():
            for cp in copies:
                cp.wait_send()
            for cp in copies:
                cp.wait_recv()

    return pl.pallas_call(
        body, name=name, grid=grid, in_specs=[a_spec, b_spec] + [ANY] * n_ri, out_specs=[o_spec] + [ANY] * n_ro,
        out_shape=[o_shape] + list(r_shapes),
        scratch_shapes=[pltpu.SemaphoreType.DMA((n_rem,)), pltpu.SemaphoreType.DMA((n_rem,))] + scratch,
        compiler_params=_cparams(("arbitrary", "arbitrary", "arbitrary"), has_side_effects=True),
    )(a, b, *r_ins)


def _rows(fn, name, T, S, tm, rows, bpars, gpars, outs, baccs, gaccs):
    nb = T // S
    tps = S // tm
    n_r, n_b, n_g, n_o, n_ba, n_ga = len(rows), len(bpars), len(gpars), len(outs), len(baccs), len(gaccs)

    def body(*refs):
        r_refs = refs[:n_r]
        b_refs = refs[n_r:n_r + n_b]
        g_refs = refs[n_r + n_b:n_r + n_b + n_g]
        o_refs = refs[n_r + n_b + n_g:n_r + n_b + n_g + n_o]
        ba_refs = refs[n_r + n_b + n_g + n_o:n_r + n_b + n_g + n_o + n_ba]
        ga_refs = refs[n_r + n_b + n_g + n_o + n_ba:]
        i = pl.program_id(0)
        o_vals, ba_vals, ga_vals = fn([r[...] for r in r_refs], [r[...] for r in b_refs], [r[...] for r in g_refs])
        for r, v in zip(o_refs, o_vals):
            r[...] = v.astype(r.dtype)
        if n_ba:
            @pl.when(i % tps == 0)
            def _():
                for r in ba_refs:
                    r[...] = jnp.zeros_like(r)
            for r, v in zip(ba_refs, ba_vals):
                r[...] += v.reshape(r.shape)
        if n_ga:
            @pl.when(i == 0)
            def _():
                for r in ga_refs:
                    r[...] = jnp.zeros_like(r)
            for r, v in zip(ga_refs, ga_vals):
                r[...] += v.reshape(r.shape)

    def row_spec(arr, w, cb, kind="tile"):
        hr = 8 * (4 // arr.dtype.itemsize)
        if kind == "prev":
            return pl.BlockSpec((hr, w), lambda i: (jnp.maximum(i * (tm // hr) - 1, 0), cb))
        if kind == "next":
            return pl.BlockSpec((hr, w), lambda i: (jnp.minimum((i + 1) * (tm // hr), T // hr - 1), cb))
        return pl.BlockSpec((tm, w), lambda i: (i, cb))

    in_specs = [row_spec(*r) for r in rows]
    in_specs += [pl.BlockSpec((None, 1, p.shape[-1]), lambda i: (i // tps, 0, 0)) for p in bpars]
    in_specs += [pl.BlockSpec(p.shape, lambda i: (0, 0)) for p in gpars]
    out_specs = [pl.BlockSpec((tm, w), lambda i: (i, 0)) for w, _ in outs]
    out_specs += [pl.BlockSpec((None, 1, w), lambda i: (i // tps, 0, 0)) for w in baccs]
    out_specs += [pl.BlockSpec(s, lambda i: (0, 0)) for s in gaccs]
    out_shape = [jax.ShapeDtypeStruct((T, w), dt) for w, dt in outs]
    out_shape += [jax.ShapeDtypeStruct((nb, 1, w), F32) for w in baccs]
    out_shape += [jax.ShapeDtypeStruct(s, F32) for s in gaccs]
    res = pl.pallas_call(
        body, name=name, grid=(T // tm,), in_specs=in_specs, out_specs=out_specs, out_shape=out_shape,
        compiler_params=_cparams(("arbitrary",)),
    )(*[r[0] for r in rows], *bpars, *gpars)
    return res[:n_o], res[n_o:n_o + n_ba], res[n_o + n_ba:]


@jax.custom_vjp
def _gsum(z, G):
    zh = z.astype(BF16)
    zl = (z - zh.astype(F32)).astype(BF16)
    r = _dot_nt(zh, G) + _dot_nt(zl, G)
    rh = r.astype(BF16)
    rl = (r - rh.astype(F32)).astype(BF16)
    return _dot(rh, G) + _dot(rl, G)


def _dot3(x, w):
    xh = x.astype(BF16).astype(F32)
    wh = w.astype(BF16).astype(F32)
    xc = jnp.concatenate([xh, xh, x - xh], axis=1).astype(BF16)
    wc = jnp.concatenate([wh, w - wh, wh], axis=0).astype(BF16)
    return _dot(xc, wc)


_gsum.defvjp(lambda z, G: (_gsum(z, G), G), lambda G, ct: (_gsum(ct, G), jnp.zeros_like(G)))


def _s1(x, g, scale, shift):
    y = x * lax.rsqrt(jnp.mean(x * x, axis=-1, keepdims=True) + RMS_EPS)
    return (y * g) * (1.0 + scale) + shift


def _s3(uc, og, cb, lg, lb):
    u = uc + cb
    mu = jnp.mean(u, axis=-1, keepdims=True)
    d = u - mu
    var = jnp.mean(d * d, axis=-1, keepdims=True)
    y = d * lax.rsqrt(var + LN_EPS) * lg + lb
    return _silu(y) * _silu(og)


def _s4(r0, k0, v0, l0, pr, pk, pv, plo, mu_r, mu_k, mu_v, mu_l, w0, w2p, a0, a2p, k_k, k_a, G):
    r = r0 + mu_r * (pr - r0)
    k = k0 + mu_k * (pk - k0)
    v = v0 + mu_v * (pv - v0)
    lo = l0 + mu_l * (plo - l0)
    w_pre = w0 + _dot3(jnp.tanh(lo), w2p)
    lw = -np.float32(np.exp(-0.5)) * _sigmoid(w_pre)
    a = _sigmoid(a0 + _dot3(lo, a2p))
    kkr = k * k_k
    ss = _gsum(kkr * kkr, G)
    kk = kkr / jnp.maximum(jnp.sqrt(ss), L2_EPS)
    k2 = k * (1.0 + (a - 1.0) * k_a)
    return r, lw, k2, v, kk, kk * a


def _s5(o, r, k2, v, og, gg, gb, rk, G):
    mu = _gsum(o, G) * (1.0 / HN)
    d = o - mu
    var = _gsum(d * d, G) * (1.0 / HN)
    y = d * lax.rsqrt(var + GN_EPS) * gg + gb
    bonus = _gsum(r * k2 * rk, G)
    return (y + bonus * v) * _silu(og)


def _s6(yc, yr, gc, gr):
    return _sigmoid(gc) * yc + _sigmoid(gr) * yr


def _s7(x, out, tgt, gate, fg):
    x2 = x + gate * out
    y = x2 * lax.rsqrt(jnp.mean(x2 * x2, axis=-1, keepdims=True) + RMS_EPS) * fg
    e = y - tgt
    return 0.5 * jnp.sum(jnp.mean(e * e, axis=-1))


@jax.custom_vjp
def _solve_all(a_kbs, rhss, cm):
    return _solve_all_fwd(a_kbs, rhss, cm)[0]


def _solve_all_fwd(a_kbs, rhss, cm):
    H = range(len(a_kbs))
    xi = [cm[2] - cm[3] * a_kbs[j] for j in H]
    for lvl in range(1, 6):
        t = [_pdot(_dot, xi[j], cm[3 + lvl] * a_kbs[j], P_INV) for j in H]
        xi = [xi[j] - _pdot(_dot, t[j], xi[j], P_INV) for j in H]
    u = tuple(_pdot(_dot, xi[j], rhss[j], P_APPLY) for j in H)
    return u, (xi, u, cm)


def _solve_all_bwd(res, dus):
    xi, u, cm = res
    H = range(len(u))
    g = tuple(_pdot(_dot_tn, xi[j], dus[j], P_APPLY) for j in H)
    da = tuple(-(cm[1] * _pdot(_dot_nt, g[j], u[j], P_APPLY)) for j in H)
    return da, g, jnp.zeros_like(cm)


_solve_all.defvjp(_solve_all_fwd, _solve_all_bwd)


def _chunk(sts, r, lw, k, v, kk, b, cm):
    cum = _dot(cm[0], lw, HI)
    ein = jnp.exp(-cum)
    rt = r * jnp.exp(cum)
    kkt = kk * jnp.exp(cum - lw)
    kh = k * ein
    bh = b * ein
    ec = jnp.exp(jnp.sum(lw, axis=0, keepdims=True))
    khe = kh * ec
    bhe = bh * ec
    H = range(len(sts))
    tri, strict, eye = cm[0], cm[1], cm[2]
    rt, kkt, kh, bh, v, khe, bhe, ec = ([a[:, j * HN:(j + 1) * HN] for j in H] for a in (rt, kkt, kh, bh, v, khe, bhe, ec))
    lhs = [jnp.concatenate([kkt[j], rt[j]], axis=0) for j in H]
    rhs_s = [jnp.concatenate([bh[j], kh[j]], axis=0) for j in H]
    lh = [a.astype(BF16).astype(F32) for a in lhs]
    rh = [a.astype(BF16).astype(F32) for a in rhs_s]
    lc = [jnp.concatenate([lh[j], lh[j], lhs[j] - lh[j]], axis=1).astype(BF16) for j in H]
    rc = [jnp.concatenate([rh[j], rhs_s[j] - rh[j], rh[j]], axis=1).astype(BF16) for j in H]
    sc = [_dot_nt(lc[j], rc[j]) for j in H]
    a_kb = [strict * sc[j][:CH, :CH] for j in H]
    a_kk = [strict * sc[j][:CH, CH:] for j in H]
    a_rb = [tri * sc[j][CH:, :CH] for j in H]
    a_rk = [tri * sc[j][CH:, CH:] for j in H]
    ps = [_dot_nt(lhs[j].astype(BF16), sts[j].astype(BF16)) for j in H]
    pv = [_dot(jnp.concatenate([a_kk[j], a_rk[j]], axis=0).astype(BF16), v[j].astype(BF16)) for j in H]
    rhs = [ps[j][:CH] + pv[j][:CH] for j in H]
    o0 = [ps[j][CH:] + pv[j][CH:] for j in H]
    u = _solve_all(tuple(a_kb), tuple(rhs), cm)
    o = [o0[j] - _pdot(_dot, a_rb[j], u[j], P_APPLY) for j in H]
    st2 = [sts[j] * ec[j] + _dot_tn(jnp.concatenate([v[j], u[j]], axis=0).astype(BF16),
                                    jnp.concatenate([khe[j], -bhe[j]], axis=0).astype(BF16)) for j in H]
    return jnp.concatenate(o, axis=1), tuple(st2)


def _chunk_consts():
    t = np.arange(CH)[:, None]
    s = np.arange(CH)[None, :]
    mats = [(t >= s), (t > s), (t == s)]
    for lvl in range(6):
        sz = 1 << lvl
        mats.append(((t // sz) % 2 == 1) & ((s // sz) == (t // sz) - 1))
    mats.append(np.zeros((CH, CH), bool))
    return np.stack(mats).astype(np.float32)


def _adamw(w, g, m, v):
    m = ADAM_B1 * m + (1.0 - ADAM_B1) * g
    v = ADAM_B2 * v + (1.0 - ADAM_B2) * (g * g)
    m_hat = m / (1.0 - ADAM_B1 ** ADAM_STEP)
    v_hat = v / (1.0 - ADAM_B2 ** ADAM_STEP)
    delta = -ADAM_LR * (m_hat / (jnp.sqrt(v_hat) + ADAM_EPS) + ADAM_WD * w)
    return delta, m, v


CT = 128
RB = 64
WIN = RB + 32


def _conv_fwd(pm, ck, T, S):
    nb = T // S

    def body(val_ref, gate_ref, ck_ref, out_ref, ubuf):
        ubuf[0:32, :] = jnp.zeros((32, CT), F32)
        ubuf[32:, :] = val_ref[...] * _sigmoid(gate_ref[...])

        def blk(rb, carry):
            base = pl.multiple_of(rb * RB, RB)
            win = ubuf[pl.ds(base, WIN), :]
            acc = jnp.zeros((RB, CT), F32)
            for j in range(CW):
                acc = acc + ck_ref[j:j + 1, :] * pltpu.roll(win, (WIN - (2 + j)) % WIN, 0)[0:RB, :]
            out_ref[pl.ds(base, RB), :] = acc
            return carry

        lax.fori_loop(0, S // RB, blk, 0)

    return pl.pallas_call(
        body, name="conv_fwd", grid=(D // CT, nb),
        in_specs=[pl.BlockSpec((S, CT), lambda ct, b: (b, ct)),
                  pl.BlockSpec((S, CT), lambda ct, b: (b, D // CT + ct)),
                  pl.BlockSpec((32, CT), lambda ct, b: (0, ct))],
        out_specs=pl.BlockSpec((S, CT), lambda ct, b: (b, ct)),
        out_shape=jax.ShapeDtypeStruct((T, D), F32),
        scratch_shapes=[pltpu.VMEM((S + 32, CT), F32)],
        compiler_params=_cparams(("parallel", "arbitrary")),
    )(pm, pm, ck)


def _conv_bwd(pm, duc, ck, T, S):
    nb = T // S

    def body(val_ref, gate_ref, duc_ref, ck_ref, dval_ref, dgate_ref, dck_ref, ubuf, dbuf, acc):
        b = pl.program_id(1)
        ubuf[0:32, :] = jnp.zeros((32, CT), F32)
        ubuf[32:, :] = val_ref[...] * _sigmoid(gate_ref[...])
        dbuf[0:S, :] = duc_ref[...]
        dbuf[S:, :] = jnp.zeros((32, CT), F32)
        acc[...] = jnp.zeros_like(acc)

        def blk(rb, carry):
            base = pl.multiple_of(rb * RB, RB)
            uwin = ubuf[pl.ds(base, WIN), :]
            dwin = dbuf[pl.ds(base, WIN), :]
            dblk = dwin[0:RB, :]
            du = jnp.zeros((RB, CT), F32)
            for j in range(CW):
                du = du + ck_ref[j:j + 1, :] * pltpu.roll(dwin, (WIN - (CW - 1 - j)) % WIN, 0)[0:RB, :]
                ush = pltpu.roll(uwin, (WIN - (2 + j)) % WIN, 0)[0:RB, :]
                acc[j] += jnp.sum((dblk * ush).reshape(RB // 8, 8, CT), axis=0)
            val = val_ref[pl.ds(base, RB), :]
            sg = _sigmoid(gate_ref[pl.ds(base, RB), :])
            dval_ref[pl.ds(base, RB), :] = (du * sg).astype(BF16)
            dgate_ref[pl.ds(base, RB), :] = (du * val * sg * (1.0 - sg)).astype(BF16)
            return carry

        lax.fori_loop(0, S // RB, blk, 0)

        @pl.when(b == 0)
        def _():
            dck_ref[...] = jnp.zeros_like(dck_ref)
        for j in range(CW):
            dck_ref[j:j + 1, :] += jnp.sum(acc[j], axis=0, keepdims=True)

    return pl.pallas_call(
        body, name="conv_bwd", grid=(D // CT, nb),
        in_specs=[pl.BlockSpec((S, CT), lambda ct, b: (b, ct)),
                  pl.BlockSpec((S, CT), lambda ct, b: (b, D // CT + ct)),
                  pl.BlockSpec((S, CT), lambda ct, b: (b, ct)),
                  pl.BlockSpec((32, CT), lambda ct, b: (0, ct))],
        out_specs=[pl.BlockSpec((S, CT), lambda ct, b: (b, ct)),
                   pl.BlockSpec((S, CT), lambda ct, b: (b, ct)),
                   pl.BlockSpec((32, CT), lambda ct, b: (0, ct))],
        out_shape=[jax.ShapeDtypeStruct((T, D), BF16), jax.ShapeDtypeStruct((T, D), BF16),
                   jax.ShapeDtypeStruct((32, D), F32)],
        scratch_shapes=[pltpu.VMEM((S + 32, CT), F32), pltpu.VMEM((S + 32, CT), F32), pltpu.VMEM((32, 8, CT), F32)],
        compiler_params=_cparams(("parallel", "arbitrary")),
    )(pm, pm, duc, ck)


HB = 16


def _scan_fwd(ins, cm, nb, S):
    nc = S // CH
    blk = pl.BlockSpec((CH, HB * HN), lambda b, g, i: (b * nc + i, g))
    hblk = pl.BlockSpec((None, HB, None, HN, HN), lambda b, g, i: (b, g, i, 0, 0))

    def body(r_ref, lw_ref, k_ref, v_ref, kk_ref, b_ref, cm_ref, o_ref, hs_ref, st):
        @pl.when(pl.program_id(2) == 0)
        def _():
            st[...] = jnp.zeros_like(st)
        s0 = [st[j] for j in range(HB)]
        for j in range(HB):
            hs_ref[j] = s0[j]
        o, s1 = _chunk(s0, r_ref[...], lw_ref[...], k_ref[...], v_ref[...], kk_ref[...], b_ref[...], cm_ref[...])
        o_ref[...] = o
        for j in range(HB):
            st[j] = s1[j]

    return pl.pallas_call(
        body, name="scan_fwd", grid=(nb, NH // HB, nc),
        in_specs=[blk] * 6 + [pl.BlockSpec(cm.shape, lambda b, g, i: (0, 0, 0))],
        out_specs=[blk, hblk],
        out_shape=[jax.ShapeDtypeStruct((nb * S, D), F32), jax.ShapeDtypeStruct((nb, NH, nc, HN, HN), F32)],
        scratch_shapes=[pltpu.VMEM((HB, HN, HN), F32)],
        compiler_params=_cparams(("parallel", "parallel", "arbitrary")),
    )(*ins, cm)


def _scan_bwd(ins, hs, do, cm, nb, S):
    nc = S // CH
    blk = pl.BlockSpec((CH, HB * HN), lambda b, g, i: (b * nc + nc - 1 - i, g))
    hblk = pl.BlockSpec((None, HB, None, HN, HN), lambda b, g, i: (b, g, nc - 1 - i, 0, 0))

    def body(r_ref, lw_ref, k_ref, v_ref, kk_ref, b_ref, hs_ref, do_ref, cm_ref,
             dr_ref, dlw_ref, dk_ref, dv_ref, dkk_ref, db_ref, dst):
        @pl.when(pl.program_id(2) == 0)
        def _():
            dst[...] = jnp.zeros_like(dst)
        cmv = cm_ref[...]
        f = lambda s0, r, lw, k, v, kk, b: _chunk(s0, r, lw, k, v, kk, b, cmv)
        _, vjp = jax.vjp(f, [hs_ref[j] for j in range(HB)], r_ref[...], lw_ref[...], k_ref[...], v_ref[...],
                         kk_ref[...], b_ref[...])
        ds0, dr, dlw, dk, dv, dkk, db = vjp((do_ref[...], tuple(dst[j] for j in range(HB))))
        for j in range(HB):
            dst[j] = ds0[j]
        dr_ref[...] = dr
        dlw_ref[...] = dlw
        dk_ref[...] = dk
        dv_ref[...] = dv
        dkk_ref[...] = dkk
        db_ref[...] = db

    return pl.pallas_call(
        body, name="scan_bwd", grid=(nb, NH // HB, nc),
        in_specs=[blk] * 6 + [hblk, blk, pl.BlockSpec(cm.shape, lambda b, g, i: (0, 0, 0))],
        out_specs=[blk] * 6,
        out_shape=[jax.ShapeDtypeStruct((nb * S, D), F32)] * 6,
        scratch_shapes=[pltpu.VMEM((HB, HN, HN), F32)],
        compiler_params=_cparams(("parallel", "parallel", "arbitrary")),
    )(*ins, hs, do, cm)


def _ew(fn, name, ins, n_out, tm, out_dtype=F32):
    R, W = ins[0].shape[-2:]
    tm = min(tm, R)
    if R % tm:
        tm = R // 2
    assert R % tm == 0 and (tm % 16 == 0 or tm == R), (name, R, tm)

    def body(*refs):
        vals = fn(*[r[...] for r in refs[:len(ins)]])
        for r, v in zip(refs[len(ins):], vals):
            r[...] = v.astype(r.dtype)

    def spec(a):
        if a.ndim == 3:
            return pl.BlockSpec((a.shape[0], tm, W), lambda i: (0, i, 0))
        return pl.BlockSpec((tm, W), lambda i: (i, 0))

    return pl.pallas_call(
        body, name=name, grid=(R // tm,), in_specs=[spec(a) for a in ins],
        out_specs=[pl.BlockSpec((tm, W), lambda i: (i, 0))] * n_out,
        out_shape=[jax.ShapeDtypeStruct((R, W), out_dtype)] * n_out,
        compiler_params=_cparams(("parallel",)),
    )(*ins)


def _sum_slots(r):
    s = r[0]
    for j in range(1, r.shape[0]):
        s = s + r[j]
    return s


def _place():
    x, y, c = lax.axis_index("x"), lax.axis_index("y"), lax.axis_index("c")
    return x, y, c


def _flip(v, d):
    return 1 - v if d else v


CHIP_PEERS = ((1, 0), (0, 1), (1, 1))
DEV_PEERS = tuple((dx, dy, dc) for dx in (0, 1) for dy in (0, 1) for dc in (0, 1))[1:]


def _comm_call(name, ins, out_shapes, plan, n_rem, n_fwd=0):
    n_in = len(ins)

    def body(*refs):
        in_refs, out_refs = refs[:n_in], refs[n_in:n_in + len(out_shapes)]
        send_sems, recv_sems, loc_sems = refs[n_in + len(out_shapes):]
        loc, rem, *rest = plan(in_refs, out_refs, _place())
        fwd = rest[0] if rest else []
        assert len(rem) == n_rem and len(fwd) == n_fwd and len(loc) <= 2 * n_in, (name, len(loc), len(rem), len(fwd))

        def remote(i, s, d, peer):
            return pltpu.make_async_remote_copy(src_ref=s, dst_ref=d, send_sem=send_sems.at[i], recv_sem=recv_sems.at[i],
                                                device_id=peer, device_id_type=MESH)

        copies = [pltpu.make_async_copy(s, d, loc_sems.at[i]) for i, (s, d) in enumerate(loc)]
        rcopies = [remote(i, s, d, peer) for i, (s, d, peer) in enumerate(rem)]
        for cp in copies + rcopies:
            cp.start()
        landed = set()
        fcopies = []
        for i, (s, d, peer, k) in enumerate(fwd):
            if k not in landed:
                rcopies[k].wait_recv()
                landed.add(k)
            fcopies.append(remote(n_rem + i, s, d, peer))
            fcopies[-1].start()
        for k, cp in enumerate(rcopies):
            if k not in landed:
                cp.wait_recv()
        for cp in rcopies + fcopies:
            cp.wait_send()
        for cp in fcopies:
            cp.wait_recv()
        for cp in copies:
            cp.wait()

    return pl.pallas_call(
        body, name=name, in_specs=[ANY] * n_in, out_specs=[ANY] * len(out_shapes), out_shape=out_shapes,
        scratch_shapes=[pltpu.SemaphoreType.DMA((n_rem + n_fwd,)), pltpu.SemaphoreType.DMA((n_rem + n_fwd,)),
                        pltpu.SemaphoreType.DMA((2 * n_in,))],
        compiler_params=pltpu.CompilerParams(has_side_effects=True),
    )(*ins)


def _gather_plan(n_big, in_refs, out_refs, place):
    x, y, c = place
    chip, dev = 2 * x + y, 4 * x + 2 * y + c
    sib = (x, y, 1 - c)
    loc = [(in_refs[0], out_refs[0].at[dev])] + [(s, d.at[chip]) for s, d in zip(in_refs[1 + n_big:], out_refs[1 + n_big:])]
    rem = [(in_refs[0], out_refs[0].at[dev], (_flip(x, dx), _flip(y, dy), _flip(c, dc))) for dx, dy, dc in DEV_PEERS]
    fwd = []
    for s, d in zip(in_refs[1:1 + n_big], out_refs[1:1 + n_big]):
        for dx, dy in CHIP_PEERS:
            px, py = _flip(x, dx), _flip(y, dy)
            fwd.append((d.at[2 * px + py, c], d.at[2 * px + py, c], sib, len(rem)))
            rem.append((s.at[c], d.at[chip, c], (px, py, c)))
    for s, d in zip(in_refs[1 + n_big:], out_refs[1 + n_big:]):
        rem += [(s, d.at[chip], (_flip(x, dx), _flip(y, dy), c)) for dx, dy in CHIP_PEERS]
    return loc, rem, fwd


def _join_plan(in_refs, out_refs, place):
    x, y, c = place
    return [], [(s, d, (x, y, 1 - c)) for s, d in zip(in_refs, out_refs)]


def _scatter_plan(n_all, in_refs, out_refs, place):
    x, y, c = place
    chip, dev = 2 * x + y, 4 * x + 2 * y + c
    loc, rem = [], []
    for s, d in zip(in_refs[:n_all], out_refs[:n_all]):
        loc.append((s.at[dev], d.at[dev]))
        for dx, dy, dc in DEV_PEERS:
            px, py, pc = _flip(x, dx), _flip(y, dy), _flip(c, dc)
            rem.append((s.at[4 * px + 2 * py + pc], d.at[dev], (px, py, pc)))
    for s, d in zip(in_refs[n_all:], out_refs[n_all:]):
        for dx, dy in CHIP_PEERS:
            px, py = _flip(x, dx), _flip(y, dy)
            rem.append((s.at[2 * px + py], d.at[chip], (px, py, c)))
    return loc, rem


def _bshape(a, nb):
    return a.reshape(nb, 1, a.shape[-1])


def _with_prev(cur, before, tiles_per_seq):
    first = pl.program_id(0) % tiles_per_seq == 0
    row0 = jnp.where(first, 0.0, before[before.shape[0] - 1:, :])
    rid = lax.broadcasted_iota(jnp.int32, cur.shape, 0)
    return jnp.where(rid == 0, row0, pltpu.roll(cur, 1, 0))


def _with_next(cur, after, tiles_per_seq):
    last = pl.program_id(0) % tiles_per_seq == tiles_per_seq - 1
    n = cur.shape[0]
    row_n = jnp.where(last, 0.0, after[0:1, :])
    rid = lax.broadcasted_iota(jnp.int32, cur.shape, 0)
    return jnp.where(rid == n - 1, row_n, pltpu.roll(cur, n - 1, 0))


def _local_step(x2d, tgt, mod, wmain, wlora, late_w, ck, w2, a2, small, nb, S, grads_hook):
    T = nb * S
    shift, scale, gate = (_bshape(mod[:, i * D:(i + 1) * D], nb) for i in range(3))
    G = jnp.asarray(np.arange(128)[:, None] == np.arange(D)[None, :] // HN, dtype=BF16)
    cm = jnp.asarray(_chunk_consts())
    ckp = jnp.pad(ck, ((0, 1), (0, 0)))
    zpad = jnp.zeros((64, D), F32)
    w2p = jnp.concatenate([w2, zpad], axis=0)
    a2p = jnp.concatenate([zpad, a2], axis=0)
    mu = small["rwkv_mu"]
    mu_r, mu_k, mu_v, mu_l = mu[:, 0:D], mu[:, D:2 * D], mu[:, 2 * D:3 * D], mu[:, 3 * D:]
    g4 = [mu_r, mu_k, mu_v, mu_l, small["rwkv_w0"], w2p, small["rwkv_a0"], a2p, small["rwkv_k_k"], small["rwkv_k_a"], G]
    g5 = [small["rwkv_gn_g"], small["rwkv_gn_b"], small["rwkv_r_k"], G]
    g3 = [small["conv_b"], small["conv_ln_g"], small["conv_ln_b"]]

    (h,), _, _ = _rows(lambda r, b, g: ([_s1(r[0], g[0], b[0], b[1])], [], []), "pre_fwd", T, S, 256,
                       [(x2d, D, 0)], [scale, shift], [small["norm_g"]], [(D, BF16)], [], [])
    skip = (DMAIN, lambda g, t: pl.multiple_of(g * t + jnp.where(g * t >= 6 * D, LORA, 0), LORA))
    if len(late_w) == 3:
        pm = _matmul(h, wmain, "nt", "proj_main", min(T, 1024), 1024, D, b_rows=skip)
        wco, wro, wo = late_w
    else:
        pm, *landed = _matmul(h, wmain, "nt", "proj_main", min(T, 1024), 1024, D, ride=late_w[0], b_rows=skip)
        wco, wro, wo = late_w[1](landed)
    plo = _matmul(h, wlora, "nt", "proj_lora", 512, LORA, D)
    uc = _conv_fwd(pm, ckp, T, S)
    (uo,), _, _ = _rows(lambda r, b, g: ([_s3(r[0], r[1], *g)], [], []), "conv_post_fwd", T, S, 256,
                        [(uc, D, 0), (pm, D, 2)], [], g3, [(D, BF16)], [], [])
    yc = _matmul(uo, wco, "nn", "conv_out", 512, 1024, D)
    rows4 = [(pm, D, 3), (pm, D, 4), (pm, D, 5), (plo, LORA, 0),
             (pm, D, 3, "prev"), (pm, D, 4, "prev"), (pm, D, 5, "prev"), (plo, LORA, 0, "prev")]
    tps4 = S // 128

    def shifted4(r, tps=tps4):
        return list(r[:4]) + [_with_prev(r[i], r[4 + i], tps) for i in range(4)]

    sc_in, _, _ = _rows(lambda r, b, g: (list(_s4(*shifted4(r, S // 256), *g)), [], []), "rwkv_pre_fwd", T, S, 256,
                        rows4, [], g4, [(D, F32)] * 6, [], [])
    o, hs = _scan_fwd(sc_in, cm, nb, S)
    rows5 = [(o, D, 0), (sc_in[0], D, 0), (sc_in[2], D, 0), (sc_in[3], D, 0), (pm, D, 6)]
    (o2,), _, _ = _rows(lambda r, b, g: ([_s5(*r, *g)], [], []), "rwkv_post_fwd", T, S, 256,
                        rows5, [], g5, [(D, BF16)], [], [])
    yr = _matmul(o2, wro, "nn", "rwkv_out", 512, 1024, D)
    rows6 = [(yc, D, 0), (yr, D, 0), (pm, D, 7), (pm, D, 8)]
    (m,), _, _ = _rows(lambda r, b, g: ([_s6(*r)], [], []), "merge_fwd", T, S, 256, rows6, [], [], [(D, BF16)], [], [])
    out = _matmul(m, wo, "nn", "out_proj", 512, 1024, D)

    def head(r, b, g):
        loss, (dx, dout, dgate, dfg) = jax.value_and_grad(_s7, argnums=(0, 1, 3, 4))(r[0], r[1], r[2], b[0], g[0])
        return [dx, dout], [dgate], [dfg, jnp.full((1, 128), loss, F32)]

    (dx_res, dout), (dgate,), (d_final_g, loss_v) = _rows(
        head, "head", T, S, 256, [(x2d, D, 0), (out, D, 0), (tgt, D, 0)], [gate], [small["final_g"]],
        [(D, F32), (D, BF16)], [D], [(1, D), (1, 128)])

    d_wo = _matmul(m, dout, "tn", "d_w_out", 512, 1024, min(T, 2048))
    dm = _matmul(dout, wo, "nt", "d_merge", 512, 1024, D)

    def merge_bwd(r, b, g):
        _, vjp = jax.vjp(_s6, *r[:4])
        dyc, dyr, dgc, dgr = vjp(r[4])
        return [dyc, dyr, dgc, dgr], [], []

    (dyc, dyr, dgc, dgr), _, _ = _rows(merge_bwd, "merge_bwd", T, S, 256, rows6 + [(dm, D, 0)], [], [],
                                       [(D, BF16), (D, BF16), (D, BF16), (D, BF16)], [], [])
    d_wco = _matmul(uo, dyc, "tn", "d_w_conv_out", 512, 1024, min(T, 2048))
    d_wro = _matmul(o2, dyr, "tn", "d_w_rwkv_out", 512, 1024, min(T, 2048))
    duo = _matmul(dyc, wco, "nt", "d_conv_act", 512, 1024, D)
    do2 = _matmul(dyr, wro, "nt", "d_rwkv_act", 512, 1024, D)

    def conv_post_bwd(r, b, g):
        _, vjp = jax.vjp(_s3, r[0], r[1], *g)
        duc, dog, dcb, dlg, dlb = vjp(r[2])
        return [duc, dog], [], [dcb, dlg, dlb]

    (duc, dcog), _, (d_cb, d_lg, d_lb) = _rows(conv_post_bwd, "conv_post_bwd", T, S, 256,
                                               [(uc, D, 0), (pm, D, 2), (duo, D, 0)], [], g3,
                                               [(D, F32), (D, BF16)], [], [(1, D)] * 3)
    dval, dgt, d_ckp = _conv_bwd(pm, duc, ckp, T, S)

    def rwkv_post_bwd(r, b, g):
        _, vjp = jax.vjp(lambda *z: _s5(*z, g[3]), *r[:5], *g[:3])
        res = vjp(r[5])
        return list(res[:5]), [], list(res[5:8])

    (do, dr_b, dk_b, dv_b, drog), _, (d_gg, d_gb, d_rk) = _rows(
        rwkv_post_bwd, "rwkv_post_bwd", T, S, 256, rows5 + [(do2, D, 0)], [], g5,
        [(D, F32)] * 4 + [(D, BF16)], [], [(1, D)] * 3)
    dsc = _scan_bwd(sc_in, hs, do, cm, nb, S)

    def rwkv_pre_bwd(r, b, g):
        _, vjp = jax.vjp(lambda *z: _s4(*z, g[10]), *shifted4(r), *g[:10])
        ct = (r[8] + r[14], r[9], r[10] + r[15], r[11] + r[16], r[12], r[13])
        res = vjp(ct)
        return list(res[:8]), [], list(res[8:18])

    rows4b = rows4 + [(a, D, 0) for a in dsc] + [(dr_b, D, 0), (dk_b, D, 0), (dv_b, D, 0)]
    gshapes = [(1, D), (1, D), (1, D), (1, LORA), (1, D), (LORA, D), (1, D), (LORA, D), (1, D), (1, D)]
    dts, _, gts = _rows(rwkv_pre_bwd, "rwkv_pre_bwd", T, S, 128, rows4b, [], g4,
                        [(D, BF16)] * 3 + [(LORA, BF16)] + [(D, BF16)] * 3 + [(LORA, BF16)], [], gshapes)
    dr0, dk0, dv0, dl0, dpr, dpk, dpv, dpl = dts
    d_mu_r, d_mu_k, d_mu_v, d_mu_l, d_w0, d_w2p, d_a0, d_a2p, d_kk, d_ka = gts

    def assemble(r, b, g):
        r = [z.astype(F32) for z in r]
        sh = [_with_next(r[10 + i], r[14 + i], tps4) for i in range(4)]
        main = jnp.concatenate([r[0], r[1], r[2], r[3] + sh[0], r[4] + sh[1], r[5] + sh[2], r[6], r[7], r[8]], axis=1)
        return [main, r[9] + sh[3]], [], []

    rows_a = [(dval, D, 0), (dgt, D, 0), (dcog, D, 0), (dr0, D, 0), (dk0, D, 0), (dv0, D, 0), (drog, D, 0), (dgc, D, 0),
              (dgr, D, 0), (dl0, LORA, 0), (dpr, D, 0), (dpk, D, 0), (dpv, D, 0), (dpl, LORA, 0),
              (dpr, D, 0, "next"), (dpk, D, 0, "next"), (dpv, D, 0, "next"), (dpl, LORA, 0, "next")]
    (dpm, dplo), _, _ = _rows(assemble, "assemble_dp", T, S, 128, rows_a, [], [], [(DMAIN, BF16), (LORA, BF16)], [], [])
    RW = D // NCHIP
    c_i = lax.axis_index("c")
    halved = [g.reshape(NCHIP, 2, RW // 2, D).transpose(1, 0, 2, 3).reshape(2, NCHIP * RW // 2, D) for g in (d_wco, d_wro, d_wo)]
    halved += [g.reshape(-1, NCHIP, 2, RW // 2).transpose(2, 1, 0, 3).reshape(2, -1, RW // 2)
               for g in (d_ckp[:CW], d_w2p[:64], d_a2p[64:])]
    h_keep, h_send = (lax.dynamic_slice_in_dim(h, k * (D // 2), D // 2, axis=1) for k in (c_i, 1 - c_i))
    tk = min(T, 2048)
    send = [_matmul(h_send, dpm, "tn", "d_w_main_send", D // 2, 1024, tk, out_t=True)[0],
            _matmul(h_send, dplo, "tn", "d_w_lora_send", D // 2, LORA, tk, out_t=True)[0]]
    send += [lax.dynamic_index_in_dim(g, 1 - c_i, 0, keepdims=False) for g in halved]
    keep = [None, _matmul(h_keep, dplo, "tn", "d_w_lora_keep", D // 2, LORA, tk, out_t=True)[0]]
    keep += [lax.dynamic_index_in_dim(g, c_i, 0, keepdims=False) for g in halved]

    def to_sibling(in_refs, out_refs, place):
        x, y, c = place
        return [], [(s, d, (x, y, 1 - c)) for s, d in zip(in_refs, out_refs)]

    d_w_keep, *got_h = _matmul(h_keep, dpm, "tn", "d_w_main_keep", D // 2, 1024, tk, out_t=True,
                               ride=(send, [jax.ShapeDtypeStruct(t.shape, F32) for t in send], to_sibling, len(send)))
    keep[0] = d_w_keep[0]
    chip_part = [_ew(lambda p, q: [p + q], "chip_sum_%d" % i, [keep[i], got_h[i]], 1, 1024, BF16)[0]
                 for i in range(2, len(keep))]

    def sum_body(p_ref, q_ref, *rest):
        rest[-1][...] = (p_ref[...] + q_ref[...]).astype(BF16)

    blk = pl.BlockSpec((1024, D // 2), lambda i: (i, 0))
    d_win_h = pl.pallas_call(
        sum_body, name="chip_sum_w_in", grid=(DMAIN // 1024,), in_specs=[blk, blk],
        out_specs=pl.BlockSpec((pl.Element(1024), pl.Element(D // 2)), lambda i: (skip[1](i, 1024), 0)),
        out_shape=jax.ShapeDtypeStruct((DMAIN + LORA, D // 2), BF16), compiler_params=_cparams(("parallel",)),
    )(keep[0], got_h[0])
    lora_blk = pl.BlockSpec((LORA, D // 2), lambda i: (0, 0))
    d_win_h = pl.pallas_call(
        sum_body, name="chip_sum_w_lora", grid=(1,), in_specs=[lora_blk, lora_blk, ANY],
        out_specs=pl.BlockSpec((LORA, D // 2), lambda i: (6 * D // LORA, 0)),
        out_shape=jax.ShapeDtypeStruct((DMAIN + LORA, D // 2), BF16), input_output_aliases={2: 0},
        compiler_params=_cparams(("arbitrary",)),
    )(keep[1], got_h[1], d_win_h)
    chip_part = [d_win_h] + chip_part
    dh_m, *got_big = _matmul(dpm, wmain, "nn", "d_h_main", 512, 1024, 3072, ride=grads_hook(chip_part), b_rows=skip)
    dh_l = _matmul(dplo, wlora, "nn", "d_h_lora", 512, 1024, LORA)

    def pre_bwd(r, b, g):
        _, vjp = jax.vjp(_s1, r[0], g[0], b[0], b[1])
        dx, dg, dscale, dshift = vjp(r[1] + r[2])
        return [dx + r[3]], [dscale, dshift], [dg]

    (gx,), (dscale, dshift), (d_ng,) = _rows(pre_bwd, "pre_bwd", T, S, 256,
                                             [(x2d, D, 0), (dh_m, D, 0), (dh_l, D, 0), (dx_res, D, 0)],
                                             [scale, shift], [small["norm_g"]], [(D, F32)], [D, D], [(1, D)])
    dmod = jnp.concatenate([dshift, dscale, dgate], axis=-1).reshape(nb, 3 * D)
    d_small = {"norm_g": d_ng, "conv_b": d_cb, "conv_ln_g": d_lg, "conv_ln_b": d_lb,
               "rwkv_mu": jnp.concatenate([d_mu_r, d_mu_k, d_mu_v, d_mu_l], axis=1),
               "rwkv_w0": d_w0, "rwkv_a0": d_a0, "rwkv_k_k": d_kk, "rwkv_k_a": d_ka, "rwkv_r_k": d_rk,
               "rwkv_gn_g": d_gg, "rwkv_gn_b": d_gb, "final_g": d_final_g}
    return loss_v[0, 0], gx, dmod, got_big, d_small


def _step(a):
    nb, S, _ = a["x"].shape
    T = nb * S
    x_i, y_i, c_i = _place()
    chip = 2 * x_i + y_i
    w_in_t, m_w_in_t, v_w_in_t = (jnp.transpose(a[p + "w_in"][0]) for p in ("", "m_", "v_"))
    WS = w_in_t.shape[0]
    small_w = {n: a[n].reshape(1, sz) for n, sz in SMALL}

    def halves(t):
        return t.reshape(2, t.shape[0] // 2, t.shape[1])

    g_ins = [a["c"], halves(w_in_t.astype(BF16)), a["conv_k"][0], a["rwkv_w2"][0], a["rwkv_a2"][0]]
    g_out = [jax.ShapeDtypeStruct((NDEV,) + g_ins[0].shape, F32)]
    g_out += [jax.ShapeDtypeStruct((NCHIP,) + t.shape, t.dtype) for t in g_ins[1:]]
    c_all, win_g, ck_g, w2_g, a2_g = _comm_call(
        "gather_weights", g_ins, g_out, functools.partial(_gather_plan, 1), 7 + 3 * 4, 3)
    c_all = c_all.reshape(NDEV * nb, D)
    win_t = lax.dynamic_update_index_in_dim(win_g, g_ins[1], chip, 0).reshape(NCHIP * WS, D)
    late = [a[n][0].astype(BF16) for n in ("w_conv_out", "w_rwkv_out", "w_out")]

    def late_plan(in_refs, out_refs, place):
        x, y, c = place
        return [], [(s, d.at[2 * x + y], (_flip(x, dx), _flip(y, dy), c))
                    for s, d in zip(in_refs, out_refs) for dx, dy in CHIP_PEERS]

    def late_finish(landed):
        return [lax.dynamic_update_index_in_dim(g, own, chip, 0).reshape(D, D) for g, own in zip(landed, late)]

    late_w = ((late, [jax.ShapeDtypeStruct((NCHIP,) + t.shape, BF16) for t in late], late_plan, 9), late_finish)
    wmain = win_t
    wlora = win_t[6 * D:6 * D + LORA]
    ck = jnp.concatenate([ck_g[j] for j in range(NCHIP)], axis=1)
    w2 = jnp.concatenate([w2_g[j] for j in range(NCHIP)], axis=1)
    a2 = jnp.concatenate([a2_g[j] for j in range(NCHIP)], axis=1)

    ada_w = a["ada_w"][0]
    MW = ada_w.shape[1]
    ada_b_loc = lax.dynamic_slice(a["ada_b"], (0, chip * MW), (1, MW))

    def mod_body(c_ref, w_ref, b_ref, o_ref):
        o_ref[...] = _dot(_silu(c_ref[...]), w_ref[...], HI) + b_ref[...]

    modp = pl.pallas_call(mod_body, name="ada_mod", out_shape=jax.ShapeDtypeStruct((NDEV * nb, MW), F32),
                          compiler_params=_cparams())(c_all, ada_w, ada_b_loc)
    (mod_g,) = _comm_call("scatter_mod", [modp.reshape(NDEV, nb, MW)],
                          [jax.ShapeDtypeStruct((NDEV, nb, MW), F32)],
                          functools.partial(_scatter_plan, 1), 7)
    mod = mod_g.reshape(NCHIP, 2, nb, MW)
    mod = mod[:, 0].transpose(1, 0, 2).reshape(nb, NCHIP * MW)

    RW = D // NCHIP
    sh_s = []

    def grads_hook(chip_part):
        sh_s.append(chip_part[0].reshape(NCHIP, WS, D // 2))
        sh_s.extend(t.reshape(NCHIP, RW // 2, D) for t in chip_part[1:4])
        sh_s.extend(t.reshape(NCHIP, -1, RW // 2) for t in chip_part[4:])
        return (sh_s, [jax.ShapeDtypeStruct(t.shape, t.dtype) for t in sh_s], functools.partial(_scatter_plan, 0),
                3 * len(sh_s))

    loss_p, gx, dmod, got_big, d_small = _local_step(
        a["x"].reshape(T, D), a["loss_target"].reshape(T, D), mod, wmain, wlora, late_w, ck, w2, a2, small_w, nb, S,
        grads_hook)
    loss = lax.psum(loss_p, ("x", "y", "c"))

    d_small["ada_b"] = _colsum(dmod)
    small_vec = jnp.concatenate([d_small[n] for n, _ in SMALL], axis=1)
    dmod_s = dmod.reshape(nb, NCHIP, MW).transpose(1, 0, 2)
    dmod_s = jnp.repeat(dmod_s, 2, axis=0)
    small_s = jnp.broadcast_to(small_vec[None], (NDEV, 1, NSMALL))
    got = _comm_call("scatter_small", [dmod_s, small_s], [jax.ShapeDtypeStruct(t.shape, F32) for t in (dmod_s, small_s)],
                     functools.partial(_scatter_plan, 2), 14)
    dmod_all, small_all = got[0].reshape(NDEV * nb, MW), got[1].reshape(NDEV, NSMALL)

    def shard_sum(recv, sent):
        chip_i = 2 * lax.axis_index("x") + lax.axis_index("y")
        s = None
        for j in range(NCHIP):
            t = jnp.where(chip_i == j, sent[j], recv[j]).astype(F32)
            s = t if s is None else s + t
        return [s]

    fin = [_ew(shard_sum, "shard_sum_%d" % i, [t, sh_s[i]], 1, 128)[0] for i, t in enumerate(got_big)]
    oth = _comm_call("join_halves", fin, [jax.ShapeDtypeStruct(t.shape, F32) for t in fin], _join_plan, len(fin))

    outs = {}

    def upd_halves(name, mine, other):
        shp = a[name].shape
        R, W = 2 * mine.shape[0], mine.shape[1]
        tm = 128
        nh = R // 2 // tm

        def body(w_ref, m_ref, v_ref, f_ref, o_ref, g_ref, d_ref, m2_ref, v2_ref):
            g = jnp.where(pl.program_id(0) // nh == lax.axis_index("c"), f_ref[...], o_ref[...])
            g_ref[...] = g
            d_ref[...], m2_ref[...], v2_ref[...] = _adamw(w_ref[...], g, m_ref[...], v_ref[...])

        full = pl.BlockSpec((None, tm, W), lambda i: (0, i, 0))
        half = pl.BlockSpec((tm, W), lambda i: (i % nh, 0))
        assert shp == (1, R, W)
        outs[name] = pl.pallas_call(
            body, name="adamw_" + name, grid=(R // tm,), in_specs=[full] * 3 + [half] * 2, out_specs=[full] * 4,
            out_shape=[jax.ShapeDtypeStruct(shp, F32)] * 4, compiler_params=_cparams(("parallel",)),
        )(*[a[p + name] for p in ("", "m_", "v_")], mine, other)

    for name, f, o in zip(("w_conv_out", "w_rwkv_out", "w_out"), fin[1:4], oth[1:4]):
        upd_halves(name, f, o)

    tw = WS // 4

    def w_in_body(w_ref, m_ref, v_ref, f_ref, o_ref, g_ref, d_ref, m2_ref, v2_ref):
        first = lax.axis_index("c") == 0
        g = jnp.concatenate([jnp.where(first, f_ref[...], o_ref[...]), jnp.where(first, o_ref[...], f_ref[...])], axis=1)
        g_ref[...] = g
        d_ref[...], m2_ref[...], v2_ref[...] = _adamw(w_ref[...], g, m_ref[...], v_ref[...])

    full = pl.BlockSpec((tw, D), lambda i: (i, 0))
    half = pl.BlockSpec((tw, D // 2), lambda i: (i, 0))
    res = pl.pallas_call(
        w_in_body, name="adamw_w_in", grid=(WS // tw,), in_specs=[full] * 3 + [half] * 2, out_specs=[full] * 4,
        out_shape=[jax.ShapeDtypeStruct((WS, D), F32)] * 4, compiler_params=_cparams(("parallel",)),
    )(w_in_t, m_w_in_t, v_w_in_t, fin[0], oth[0])
    outs["w_in"] = [jnp.transpose(r)[None] for r in res]

    def upd(name, g):
        shp = a[name].shape
        ins = [a[p + name].reshape(g.shape) for p in ("", "m_", "v_")]
        res = _ew(lambda w_, m_, v_, g_: [g_, *_adamw(w_, g_, m_, v_)], "adamw_" + name, [*ins, g], 4, 128)
        outs[name] = [r.reshape(shp) for r in res]

    for name, f, o in zip(("conv_k", "rwkv_w2", "rwkv_a2"), fin[4:], oth[4:]):
        both = jnp.where(c_i == 0, jnp.stack([f, o]), jnp.stack([o, f]))
        upd(name, both.transpose(1, 0, 2).reshape(-1, RW))

    def adaw_body(c_ref, dm_ref, w_ref, m_ref, v_ref, g_ref, d_ref, m2_ref, v2_ref):
        g = _dot_tn(_silu(c_ref[...]), dm_ref[...], HI)
        g_ref[...] = g
        d_ref[...], m2_ref[...], v2_ref[...] = _adamw(w_ref[...], g, m_ref[...], v_ref[...])

    res = pl.pallas_call(adaw_body, name="adamw_ada_w", out_shape=[jax.ShapeDtypeStruct((D, MW), F32)] * 4,
                         compiler_params=_cparams())(c_all, dmod_all, ada_w, a["m_ada_w"][0], a["v_ada_w"][0])
    outs["ada_w"] = [r.reshape(a["ada_w"].shape) for r in res]

    wv, mv, vv = (jnp.concatenate([a[p + n].reshape(1, sz) for n, sz in SMALL], axis=1) for p in ("", "m_", "v_"))
    def small_fn(w_, m_, v_, gs):
        g = _sum_slots(gs)
        return [g, *_adamw(w_, g, m_, v_)]

    res = _ew(small_fn, "adamw_small", [wv, mv, vv, small_all.reshape(NDEV, 1, NSMALL)], 4, 8)
    off = 0
    for n, sz in SMALL:
        outs[n] = [r[:, off:off + sz].reshape(a[n].shape) for r in res]
        off += sz

    return (loss, gx.reshape(nb, S, D), *[outs[n][0] for n in WEIGHTS], *[outs[n][1] for n in WEIGHTS],
            *[outs[n][2] for n in WEIGHTS], *[outs[n][3] for n in WEIGHTS])


def _colsum(dmod):
    def body(d_ref, o_ref):
        o_ref[...] = jnp.sum(d_ref[...], axis=0, keepdims=True)
    return pl.pallas_call(body, name="ada_b_rowsum", out_shape=jax.ShapeDtypeStruct((1, dmod.shape[1]), F32),
                          compiler_params=_cparams())(dmod)


def kernel(x, c, ada_w, ada_b, norm_g, w_in, conv_k, conv_b, conv_ln_g, conv_ln_b, w_conv_out, rwkv_mu, rwkv_w0, rwkv_w2, rwkv_a0, rwkv_a2, rwkv_k_k, rwkv_k_a, rwkv_r_k, rwkv_gn_g, rwkv_gn_b, w_rwkv_out, w_out, final_g, loss_target, m_ada_w, m_ada_b, m_norm_g, m_w_in, m_conv_k, m_conv_b, m_conv_ln_g, m_conv_ln_b, m_w_conv_out, m_rwkv_mu, m_rwkv_w0, m_rwkv_w2, m_rwkv_a0, m_rwkv_a2, m_rwkv_k_k, m_rwkv_k_a, m_rwkv_r_k, m_rwkv_gn_g, m_rwkv_gn_b, m_w_rwkv_out, m_w_out, m_final_g, v_ada_w, v_ada_b, v_norm_g, v_w_in, v_conv_k, v_conv_b, v_conv_ln_g, v_conv_ln_b, v_w_conv_out, v_rwkv_mu, v_rwkv_w0, v_rwkv_w2, v_rwkv_a0, v_rwkv_a2, v_rwkv_k_k, v_rwkv_k_a, v_rwkv_r_k, v_rwkv_gn_g, v_rwkv_gn_b, v_w_rwkv_out, v_w_out, v_final_g):
    return _step(dict(locals()))
```

```python
import functools

import numpy as np
import jax
import jax.numpy as jnp
from jax import lax
from jax.experimental import pallas as pl
from jax.experimental.pallas import tpu as pltpu

F32 = jnp.float32
BF16 = jnp.bfloat16
HI = lax.Precision.HIGHEST
MESH = pl.DeviceIdType.MESH
ANY = pl.BlockSpec(memory_space=pl.ANY)

D = 1024
NH = 16
HN = 64
LORA = 128
DMAIN = 9 * D
CH = 64
CW = 31
NCHIP = 4
NDEV = 8
VMEM_LIMIT = 56 * 1024 * 1024

RMS_EPS = 1e-6
LN_EPS = 1e-5
GN_EPS = 64e-5
L2_EPS = 1e-12
ADAM_LR = 0.001
ADAM_B1 = 0.9
ADAM_B2 = 0.999
ADAM_EPS = 1e-08
ADAM_WD = 0.01
ADAM_STEP = 10

SMALL = (("ada_b", 3072), ("norm_g", 1024), ("conv_b", 1024), ("conv_ln_g", 1024), ("conv_ln_b", 1024),
         ("rwkv_mu", 3200), ("rwkv_w0", 1024), ("rwkv_a0", 1024), ("rwkv_k_k", 1024), ("rwkv_k_a", 1024),
         ("rwkv_r_k", 1024), ("rwkv_gn_g", 1024), ("rwkv_gn_b", 1024), ("final_g", 1024))
NSMALL = sum(n for _, n in SMALL)

WEIGHTS = ['ada_w', 'ada_b', 'norm_g', 'w_in', 'conv_k', 'conv_b', 'conv_ln_g', 'conv_ln_b', 'w_conv_out', 'rwkv_mu',
           'rwkv_w0', 'rwkv_w2', 'rwkv_a0', 'rwkv_a2', 'rwkv_k_k', 'rwkv_k_a', 'rwkv_r_k', 'rwkv_gn_g', 'rwkv_gn_b',
           'w_rwkv_out', 'w_out', 'final_g']


def _cparams(sem=None, **kw):
    if sem is not None:
        kw["dimension_semantics"] = sem
    return pltpu.CompilerParams(vmem_limit_bytes=VMEM_LIMIT, **kw)


def _dot(a, b, prec=None):
    return jnp.dot(a, b, preferred_element_type=F32, precision=prec)


def _dot_nt(a, b, prec=None):
    return lax.dot_general(a, b, (((1,), (1,)), ((), ())), preferred_element_type=F32, precision=prec)


def _dot_tn(a, b, prec=None):
    return lax.dot_general(a, b, (((0,), (0,)), ((), ())), preferred_element_type=F32, precision=prec)


def _pdot(f, a, b, p):
    if p == "hi":
        return f(a, b, HI)
    ah, bh = a.astype(BF16), b.astype(BF16)
    if p == "bf":
        return f(ah, bh)
    al, bl = (a - ah.astype(F32)).astype(BF16), (b - bh.astype(F32)).astype(BF16)
    return f(ah, bh) + (f(ah, bl) + f(al, bh))


P_SCORE = "b3"
P_INV = "bf"
P_APPLY = "bf"


def _sigmoid(z):
    return 1.0 / (1.0 + jnp.exp(-z))


def _silu(z):
    return z * _sigmoid(z)


def _matmul(a, b, mode, name, tm, tn, tk, ride=None, out_t=False, b_rows=None):
    if mode == "nn":
        (M, K), N = a.shape, b.shape[1]
        a_spec = pl.BlockSpec((tm, tk), lambda j, i, k: (i, k))
        b_spec = pl.BlockSpec((tk, tn), lambda j, i, k: (k, j))
        if b_rows is not None:
            assert b_rows[0] == K
            b_spec = pl.BlockSpec((pl.Element(tk), pl.Element(tn)), lambda j, i, k: (b_rows[1](k, tk), j * tn))
        f = _dot
    elif mode == "nt":
        (M, K), N = a.shape, b.shape[0]
        a_spec = pl.BlockSpec((tm, tk), lambda j, i, k: (i, k))
        b_spec = pl.BlockSpec((tn, tk), lambda j, i, k: (j, k))
        if b_rows is not None:
            N = b_rows[0]
            b_spec = pl.BlockSpec((pl.Element(tn), pl.Element(tk)), lambda j, i, k: (b_rows[1](j, tn), k * tk))
        f = _dot_nt
    else:
        (K, M), N = a.shape, b.shape[1]
        a_spec = pl.BlockSpec((tk, tm), lambda j, i, k: (k, i))
        b_spec = pl.BlockSpec((tk, tn), lambda j, i, k: (k, j))
        f = _dot_tn
    assert M % tm == 0 and N % tn == 0 and K % tk == 0, (name, M, N, K)

    grid = (N // tn, M // tm, K // tk)
    o_spec = pl.BlockSpec((tm, tn), lambda j, i, k: (i, j))
    o_shape = jax.ShapeDtypeStruct((M, N), F32)

    scratch = []
    if out_t:
        o_spec = pl.BlockSpec((None, tn, tm), lambda j, i, k: (i, j, 0))
        o_shape = jax.ShapeDtypeStruct((M // tm, N, tm), F32)
        scratch = [pltpu.VMEM((tm, tn), F32)]

    def step(a_ref, b_ref, o_ref, *acc):
        acc_ref = acc[0] if out_t else o_ref

        @pl.when(pl.program_id(2) == 0)
        def _():
            acc_ref[...] = jnp.zeros_like(acc_ref)
        acc_ref[...] += f(a_ref[...], b_ref[...])
        if out_t:
            @pl.when(pl.program_id(2) == grid[2] - 1)
            def _():
                o_ref[...] = acc_ref[...].T

    if ride is None:
        return pl.pallas_call(
            step, name=name, grid=grid, in_specs=[a_spec, b_spec], out_specs=o_spec, out_shape=o_shape,
            scratch_shapes=scratch, compiler_params=_cparams(("parallel", "parallel", "arbitrary")),
        )(a, b)

    r_ins, r_shapes, plan, n_rem = ride
    n_ri, n_ro = len(r_ins), len(r_shapes)

    def body(a_ref, b_ref, *rest):
        r_in, o_ref, r_out = rest[:n_ri], rest[n_ri], rest[n_ri + 1:n_ri + 1 + n_ro]
        send_sems, recv_sems, *acc = rest[n_ri + 1 + n_ro:]
        loc, rem = plan(r_in, r_out, _place())
        assert not loc and len(rem) == n_rem, (name, len(loc), len(rem))
        copies = [pltpu.make_async_remote_copy(src_ref=s, dst_ref=d, send_sem=send_sems.at[i], recv_sem=recv_sems.at[i],
                                               device_id=peer, device_id_type=MESH) for i, (s, d, peer) in enumerate(rem)]
        pid = [pl.program_id(ax) for ax in range(3)]

        @pl.when((pid[0] == 0) & (pid[1] == 0) & (pid[2] == 0))
        def _():
            for cp in copies:
                cp.start()

        step(a_ref, b_ref, o_ref, *acc)

        @pl.when((pid[0] == grid[0] - 1) & (pid[1] == grid[1] - 1) & (pid[2] == grid[2] - 1))
        def _():
            for cp in copies:
                cp.wait_send()
            for cp in copies:
                cp.wait_recv()

    return pl.pallas_call(
        body, name=name, grid=grid, in_specs=[a_spec, b_spec] + [ANY] * n_ri, out_specs=[o_spec] + [ANY] * n_ro,
        out_shape=[o_shape] + list(r_shapes),
        scratch_shapes=[pltpu.SemaphoreType.DMA((n_rem,)), pltpu.SemaphoreType.DMA((n_rem,))] + scratch,
        compiler_params=_cparams(("arbitrary", "arbitrary", "arbitrary"), has_side_effects=True),
    )(a, b, *r_ins)


def _rows(fn, name, T, S, tm, rows, bpars, gpars, outs, baccs, gaccs):
    nb = T // S
    tps = S // tm
    n_r, n_b, n_g, n_o, n_ba, n_ga = len(rows), len(bpars), len(gpars), len(outs), len(baccs), len(gaccs)

    def body(*refs):
        r_refs = refs[:n_r]
        b_refs = refs[n_r:n_r + n_b]
        g_refs = refs[n_r + n_b:n_r + n_b + n_g]
        o_refs = refs[n_r + n_b + n_g:n_r + n_b + n_g + n_o]
        ba_refs = refs[n_r + n_b + n_g + n_o:n_r + n_b + n_g + n_o + n_ba]
        ga_refs = refs[n_r + n_b + n_g + n_o + n_ba:]
        i = pl.program_id(0)
        o_vals, ba_vals, ga_vals = fn([r[...] for r in r_refs], [r[...] for r in b_refs], [r[...] for r in g_refs])
        for r, v in zip(o_refs, o_vals):
            r[...] = v.astype(r.dtype)
        if n_ba:
            @pl.when(i % tps == 0)
            def _():
                for r in ba_refs:
                    r[...] = jnp.zeros_like(r)
            for r, v in zip(ba_refs, ba_vals):
                r[...] += v.reshape(r.shape)
        if n_ga:
            @pl.when(i == 0)
            def _():
                for r in ga_refs:
                    r[...] = jnp.zeros_like(r)
            for r, v in zip(ga_refs, ga_vals):
                r[...] += v.reshape(r.shape)

    def row_spec(arr, w, cb, kind="tile"):
        hr = 8 * (4 // arr.dtype.itemsize)
        if kind == "prev":
            return pl.BlockSpec((hr, w), lambda i: (jnp.maximum(i * (tm // hr) - 1, 0), cb))
        if kind == "next":
            return pl.BlockSpec((hr, w), lambda i: (jnp.minimum((i + 1) * (tm // hr), T // hr - 1), cb))
        return pl.BlockSpec((tm, w), lambda i: (i, cb))

    in_specs = [row_spec(*r) for r in rows]
    in_specs += [pl.BlockSpec((None, 1, p.shape[-1]), lambda i: (i // tps, 0, 0)) for p in bpars]
    in_specs += [pl.BlockSpec(p.shape, lambda i: (0, 0)) for p in gpars]
    out_specs = [pl.BlockSpec((tm, w), lambda i: (i, 0)) for w, _ in outs]
    out_specs += [pl.BlockSpec((None, 1, w), lambda i: (i // tps, 0, 0)) for w in baccs]
    out_specs += [pl.BlockSpec(s, lambda i: (0, 0)) for s in gaccs]
    out_shape = [jax.ShapeDtypeStruct((T, w), dt) for w, dt in outs]
    out_shape += [jax.ShapeDtypeStruct((nb, 1, w), F32) for w in baccs]
    out_shape += [jax.ShapeDtypeStruct(s, F32) for s in gaccs]
    res = pl.pallas_call(
        body, name=name, grid=(T // tm,), in_specs=in_specs, out_specs=out_specs, out_shape=out_shape,
        compiler_params=_cparams(("arbitrary",)),
    )(*[r[0] for r in rows], *bpars, *gpars)
    return res[:n_o], res[n_o:n_o + n_ba], res[n_o + n_ba:]


@jax.custom_vjp
def _gsum(z, G):
    zh = z.astype(BF16)
    zl = (z - zh.astype(F32)).astype(BF16)
    r = _dot_nt(zh, G) + _dot_nt(zl, G)
    rh = r.astype(BF16)
    rl = (r - rh.astype(F32)).astype(BF16)
    return _dot(rh, G) + _dot(rl, G)


def _dot3(x, w):
    xh = x.astype(BF16).astype(F32)
    wh = w.astype(BF16).astype(F32)
    xc = jnp.concatenate([xh, xh, x - xh], axis=1).astype(BF16)
    wc = jnp.concatenate([wh, w - wh, wh], axis=0).astype(BF16)
    return _dot(xc, wc)


_gsum.defvjp(lambda z, G: (_gsum(z, G), G), lambda G, ct: (_gsum(ct, G), jnp.zeros_like(G)))


def _s1(x, g, scale, shift):
    y = x * lax.rsqrt(jnp.mean(x * x, axis=-1, keepdims=True) + RMS_EPS)
    return (y * g) * (1.0 + scale) + shift


def _s3(uc, og, cb, lg, lb):
    u = uc + cb
    mu = jnp.mean(u, axis=-1, keepdims=True)
    d = u - mu
    var = jnp.mean(d * d, axis=-1, keepdims=True)
    y = d * lax.rsqrt(var + LN_EPS) * lg + lb
    return _silu(y) * _silu(og)


def _s4(r0, k0, v0, l0, pr, pk, pv, plo, mu_r, mu_k, mu_v, mu_l, w0, w2p, a0, a2p, k_k, k_a, G):
    r = r0 + mu_r * (pr - r0)
    k = k0 + mu_k * (pk - k0)
    v = v0 + mu_v * (pv - v0)
    lo = l0 + mu_l * (plo - l0)
    w_pre = w0 + _dot3(jnp.tanh(lo), w2p)
    lw = -np.float32(np.exp(-0.5)) * _sigmoid(w_pre)
    a = _sigmoid(a0 + _dot3(lo, a2p))
    kkr = k * k_k
    ss = _gsum(kkr * kkr, G)
    kk = kkr / jnp.maximum(jnp.sqrt(ss), L2_EPS)
    k2 = k * (1.0 + (a - 1.0) * k_a)
    return r, lw, k2, v, kk, kk * a


def _s5(o, r, k2, v, og, gg, gb, rk, G):
    mu = _gsum(o, G) * (1.0 / HN)
    d = o - mu
    var = _gsum(d * d, G) * (1.0 / HN)
    y = d * lax.rsqrt(var + GN_EPS) * gg + gb
    bonus = _gsum(r * k2 * rk, G)
    return (y + bonus * v) * _silu(og)


def _s6(yc, yr, gc, gr):
    return _sigmoid(gc) * yc + _sigmoid(gr) * yr


def _s7(x, out, tgt, gate, fg):
    x2 = x + gate * out
    y = x2 * lax.rsqrt(jnp.mean(x2 * x2, axis=-1, keepdims=True) + RMS_EPS) * fg
    e = y - tgt
    return 0.5 * jnp.sum(jnp.mean(e * e, axis=-1))


def _solve_all_fwd(a_kbs, rhss, cm):
    H = range(len(a_kbs))
    xi = [cm[2] - cm[3] * a_kbs[j] for j in H]
    for lvl in range(1, 6):
        t = [_pdot(_dot, xi[j], cm[3 + lvl] * a_kbs[j], P_INV) for j in H]
        xi = [xi[j] - _pdot(_dot, t[j], xi[j], P_INV) for j in H]
    u = tuple(_pdot(_dot, xi[j], rhss[j], P_APPLY) for j in H)
    return u, (xi, u, cm)


def _solve_all_bwd(res, dus):
    xi, u, cm = res
    H = range(len(u))
    g = tuple(_pdot(_dot_tn, xi[j], dus[j], P_APPLY) for j in H)
    da = tuple(-(cm[1] * _pdot(_dot_nt, g[j], u[j], P_APPLY)) for j in H)
    return da, g, jnp.zeros_like(cm)


@jax.custom_vjp
def _solve_all(a_kbs, rhss, cm):
    return _solve_all_fwd(a_kbs, rhss, cm)[0]


_solve_all.defvjp(_solve_all_fwd, _solve_all_bwd)


def _chunk(sts, r, lw, k, v, kk, b, cm):
    cum = _dot(cm[0], lw, HI)
    ein = jnp.exp(-cum)
    rt = r * jnp.exp(cum)
    kkt = kk * jnp.exp(cum - lw)
    kh = k * ein
    bh = b * ein
    ec = jnp.exp(jnp.sum(lw, axis=0, keepdims=True))
    khe = kh * ec
    bhe = bh * ec
    H = range(len(sts))
    tri, strict, eye = cm[0], cm[1], cm[2]
    rt, kkt, kh, bh, v, khe, bhe, ec = ([a[:, j * HN:(j + 1) * HN] for j in H] for a in (rt, kkt, kh, bh, v, khe, bhe, ec))
    lhs = [jnp.concatenate([kkt[j], rt[j]], axis=0) for j in H]
    rhs_s = [jnp.concatenate([bh[j], kh[j]], axis=0) for j in H]
    lh = [a.astype(BF16).astype(F32) for a in lhs]
    rh = [a.astype(BF16).astype(F32) for a in rhs_s]
    lc = [jnp.concatenate([lh[j], lh[j], lhs[j] - lh[j]], axis=1).astype(BF16) for j in H]
    rc = [jnp.concatenate([rh[j], rhs_s[j] - rh[j], rh[j]], axis=1).astype(BF16) for j in H]
    sc = [_dot_nt(lc[j], rc[j]) for j in H]
    a_kb = [strict * sc[j][:CH, :CH] for j in H]
    a_kk = [strict * sc[j][:CH, CH:] for j in H]
    a_rb = [tri * sc[j][CH:, :CH] for j in H]
    a_rk = [tri * sc[j][CH:, CH:] for j in H]
    ps = [_dot_nt(lhs[j].astype(BF16), sts[j].astype(BF16)) for j in H]
    pv = [_dot(jnp.concatenate([a_kk[j], a_rk[j]], axis=0).astype(BF16), v[j].astype(BF16)) for j in H]
    rhs = [ps[j][:CH] + pv[j][:CH] for j in H]
    o0 = [ps[j][CH:] + pv[j][CH:] for j in H]
    u = _solve_all(tuple(a_kb), tuple(rhs), cm)
    o = [o0[j] - _pdot(_dot, a_rb[j], u[j], P_APPLY) for j in H]
    st2 = [sts[j] * ec[j] + _dot_tn(jnp.concatenate([v[j], u[j]], axis=0).astype(BF16),
                                    jnp.concatenate([khe[j], -bhe[j]], axis=0).astype(BF16)) for j in H]
    return jnp.concatenate(o, axis=1), tuple(st2)


def _chunk_consts():
    t = np.arange(CH)[:, None]
    s = np.arange(CH)[None, :]
    mats = [(t >= s), (t > s), (t == s)]
    for lvl in range(6):
        sz = 1 << lvl
        mats.append(((t // sz) % 2 == 1) & ((s // sz) == (t // sz) - 1))
    mats.append(np.zeros((CH, CH), bool))
    return np.stack(mats).astype(np.float32)


def _adamw(w, g, m, v):
    m = ADAM_B1 * m + (1.0 - ADAM_B1) * g
    v = ADAM_B2 * v + (1.0 - ADAM_B2) * (g * g)
    m_hat = m / (1.0 - ADAM_B1 ** ADAM_STEP)
    v_hat = v / (1.0 - ADAM_B2 ** ADAM_STEP)
    delta = -ADAM_LR * (m_hat / (jnp.sqrt(v_hat) + ADAM_EPS) + ADAM_WD * w)
    return delta, m, v


CT = 128
RB = 64
WIN = RB + 32


def _conv_fwd(pm, ck, T, S):
    nb = T // S

    def body(val_ref, gate_ref, ck_ref, out_ref, ubuf):
        ubuf[0:32, :] = jnp.zeros((32, CT), F32)
        ubuf[32:, :] = val_ref[...] * _sigmoid(gate_ref[...])

        def blk(rb, carry):
            base = pl.multiple_of(rb * RB, RB)
            win = ubuf[pl.ds(base, WIN), :]
            acc = jnp.zeros((RB, CT), F32)
            for j in range(CW):
                acc = acc + ck_ref[j:j + 1, :] * pltpu.roll(win, (WIN - (2 + j)) % WIN, 0)[0:RB, :]
            out_ref[pl.ds(base, RB), :] = acc
            return carry

        lax.fori_loop(0, S // RB, blk, 0)

    return pl.pallas_call(
        body, name="conv_fwd", grid=(D // CT, nb),
        in_specs=[pl.BlockSpec((S, CT), lambda ct, b: (b, ct)),
                  pl.BlockSpec((S, CT), lambda ct, b: (b, D // CT + ct)),
                  pl.BlockSpec((32, CT), lambda ct, b: (0, ct))],
        out_specs=pl.BlockSpec((S, CT), lambda ct, b: (b, ct)),
        out_shape=jax.ShapeDtypeStruct((T, D), F32),
        scratch_shapes=[pltpu.VMEM((S + 32, CT), F32)],
        compiler_params=_cparams(("parallel", "arbitrary")),
    )(pm, pm, ck)


def _conv_bwd(pm, duc, ck, T, S):
    nb = T // S

    def body(val_ref, gate_ref, duc_ref, ck_ref, dval_ref, dgate_ref, dck_ref, ubuf, dbuf, acc):
        b = pl.program_id(1)
        ubuf[0:32, :] = jnp.zeros((32, CT), F32)
        ubuf[32:, :] = val_ref[...] * _sigmoid(gate_ref[...])
        dbuf[0:S, :] = duc_ref[...]
        dbuf[S:, :] = jnp.zeros((32, CT), F32)
        acc[...] = jnp.zeros_like(acc)

        def blk(rb, carry):
            base = pl.multiple_of(rb * RB, RB)
            uwin = ubuf[pl.ds(base, WIN), :]
            dwin = dbuf[pl.ds(base, WIN), :]
            dblk = dwin[0:RB, :]
            du = jnp.zeros((RB, CT), F32)
            for j in range(CW):
                du = du + ck_ref[j:j + 1, :] * pltpu.roll(dwin, (WIN - (CW - 1 - j)) % WIN, 0)[0:RB, :]
                ush = pltpu.roll(uwin, (WIN - (2 + j)) % WIN, 0)[0:RB, :]
                acc[j] += jnp.sum((dblk * ush).reshape(RB // 8, 8, CT), axis=0)
            val = val_ref[pl.ds(base, RB), :]
            sg = _sigmoid(gate_ref[pl.ds(base, RB), :])
            dval_ref[pl.ds(base, RB), :] = (du * sg).astype(BF16)
            dgate_ref[pl.ds(base, RB), :] = (du * val * sg * (1.0 - sg)).astype(BF16)
            return carry

        lax.fori_loop(0, S // RB, blk, 0)

        @pl.when(b == 0)
        def _():
            dck_ref[...] = jnp.zeros_like(dck_ref)
        for j in range(CW):
            dck_ref[j:j + 1, :] += jnp.sum(acc[j], axis=0, keepdims=True)

    return pl.pallas_call(
        body, name="conv_bwd", grid=(D // CT, nb),
        in_specs=[pl.BlockSpec((S, CT), lambda ct, b: (b, ct)),
                  pl.BlockSpec((S, CT), lambda ct, b: (b, D // CT + ct)),
                  pl.BlockSpec((S, CT), lambda ct, b: (b, ct)),
                  pl.BlockSpec((32, CT), lambda ct, b: (0, ct))],
        out_specs=[pl.BlockSpec((S, CT), lambda ct, b: (b, ct)),
                   pl.BlockSpec((S, CT), lambda ct, b: (b, ct)),
                   pl.BlockSpec((32, CT), lambda ct, b: (0, ct))],
        out_shape=[jax.ShapeDtypeStruct((T, D), BF16), jax.ShapeDtypeStruct((T, D), BF16),
                   jax.ShapeDtypeStruct((32, D), F32)],
        scratch_shapes=[pltpu.VMEM((S + 32, CT), F32), pltpu.VMEM((S + 32, CT), F32), pltpu.VMEM((32, 8, CT), F32)],
        compiler_params=_cparams(("parallel", "arbitrary")),
    )(pm, pm, duc, ck)


def _scan_fwd(ins, cm, nb, S):
    nc = S // CH
    nch = nb * NH
    blk = pl.BlockSpec((nb, CH, D), lambda i: (0, i, 0))
    hblk = pl.BlockSpec((nb, NH, None, HN, HN), lambda i: (0, 0, i, 0, 0))

    def body(r_ref, lw_ref, k_ref, v_ref, kk_ref, b_ref, cm_ref, o_ref, hs_ref, st):
        @pl.when(pl.program_id(0) == 0)
        def _():
            st[...] = jnp.zeros_like(st)
        s0 = [st[j] for j in range(nch)]
        for j in range(nch):
            hs_ref[j // NH, j % NH] = s0[j]
        vals = [jnp.concatenate([ref[q] for q in range(nb)], axis=1) for ref in (r_ref, lw_ref, k_ref, v_ref, kk_ref, b_ref)]
        o, s1 = _chunk(s0, *vals, cm_ref[...])
        for q in range(nb):
            o_ref[q] = o[:, q * D:(q + 1) * D]
        for j in range(nch):
            st[j] = s1[j]

    o, hs = pl.pallas_call(
        body, name="scan_fwd", grid=(nc,),
        in_specs=[blk] * 6 + [pl.BlockSpec(cm.shape, lambda i: (0, 0, 0))],
        out_specs=[blk, hblk],
        out_shape=[jax.ShapeDtypeStruct((nb, S, D), F32), jax.ShapeDtypeStruct((nb, NH, nc, HN, HN), F32)],
        scratch_shapes=[pltpu.VMEM((nch, HN, HN), F32)],
        compiler_params=_cparams(("arbitrary",)),
    )(*[a.reshape(nb, S, D) for a in ins], cm)
    return o.reshape(nb * S, D), hs


def _scan_bwd(ins, hs, do, cm, nb, S):
    nc = S // CH
    nch = nb * NH
    blk = pl.BlockSpec((nb, CH, D), lambda i: (0, nc - 1 - i, 0))
    hblk = pl.BlockSpec((nb, NH, None, HN, HN), lambda i: (0, 0, nc - 1 - i, 0, 0))

    def body(r_ref, lw_ref, k_ref, v_ref, kk_ref, b_ref, hs_ref, do_ref, cm_ref,
             dr_ref, dlw_ref, dk_ref, dv_ref, dkk_ref, db_ref, dst):
        @pl.when(pl.program_id(0) == 0)
        def _():
            dst[...] = jnp.zeros_like(dst)
        cmv = cm_ref[...]
        side = lambda ref: jnp.concatenate([ref[q] for q in range(nb)], axis=1)
        f = lambda s0, r, lw, k, v, kk, b: _chunk(s0, r, lw, k, v, kk, b, cmv)
        _, vjp = jax.vjp(f, [hs_ref[j // NH, j % NH] for j in range(nch)],
                         *[side(ref) for ref in (r_ref, lw_ref, k_ref, v_ref, kk_ref, b_ref)])
        ds0, *grads = vjp((side(do_ref), tuple(dst[j] for j in range(nch))))
        for j in range(nch):
            dst[j] = ds0[j]
        for ref, g in zip((dr_ref, dlw_ref, dk_ref, dv_ref, dkk_ref, db_ref), grads):
            for q in range(nb):
                ref[q] = g[:, q * D:(q + 1) * D]

    outs = pl.pallas_call(
        body, name="scan_bwd", grid=(nc,),
        in_specs=[blk] * 6 + [hblk, blk, pl.BlockSpec(cm.shape, lambda i: (0, 0, 0))],
        out_specs=[blk] * 6,
        out_shape=[jax.ShapeDtypeStruct((nb, S, D), F32)] * 6,
        scratch_shapes=[pltpu.VMEM((nch, HN, HN), F32)],
        compiler_params=_cparams(("arbitrary",)),
    )(*[a.reshape(nb, S, D) for a in ins], hs, do.reshape(nb, S, D), cm)
    return [a.reshape(nb * S, D) for a in outs]


def _ew(fn, name, ins, n_out, tm, out_dtype=F32):
    R, W = ins[0].shape[-2:]
    tm = min(tm, R)
    if R % tm:
        tm = R // 2
    assert R % tm == 0 and (tm % 16 == 0 or tm == R), (name, R, tm)

    def body(*refs):
        vals = fn(*[r[...] for r in refs[:len(ins)]])
        for r, v in zip(refs[len(ins):], vals):
            r[...] = v.astype(r.dtype)

    def spec(a):
        if a.ndim == 3:
            return pl.BlockSpec((a.shape[0], tm, W), lambda i: (0, i, 0))
        return pl.BlockSpec((tm, W), lambda i: (i, 0))

    return pl.pallas_call(
        body, name=name, grid=(R // tm,), in_specs=[spec(a) for a in ins],
        out_specs=[pl.BlockSpec((tm, W), lambda i: (i, 0))] * n_out,
        out_shape=[jax.ShapeDtypeStruct((R, W), out_dtype)] * n_out,
        compiler_params=_cparams(("parallel",)),
    )(*ins)


def _sum_slots(r):
    s = r[0]
    for j in range(1, r.shape[0]):
        s = s + r[j]
    return s


def _place():
    x, y, c = lax.axis_index("x"), lax.axis_index("y"), lax.axis_index("c")
    return x, y, c


def _flip(v, d):
    return 1 - v if d else v


CHIP_PEERS = ((1, 0), (0, 1), (1, 1))
DEV_PEERS = tuple((dx, dy, dc) for dx in (0, 1) for dy in (0, 1) for dc in (0, 1))[1:]


def _comm_call(name, ins, out_shapes, plan, n_rem, n_fwd=0):
    n_in = len(ins)

    def body(*refs):
        in_refs, out_refs = refs[:n_in], refs[n_in:n_in + len(out_shapes)]
        send_sems, recv_sems, loc_sems = refs[n_in + len(out_shapes):]
        loc, rem, *rest = plan(in_refs, out_refs, _place())
        fwd = rest[0] if rest else []
        assert len(rem) == n_rem and len(fwd) == n_fwd and len(loc) <= 2 * n_in, (name, len(loc), len(rem), len(fwd))

        def remote(i, s, d, peer):
            return pltpu.make_async_remote_copy(src_ref=s, dst_ref=d, send_sem=send_sems.at[i], recv_sem=recv_sems.at[i],
                                                device_id=peer, device_id_type=MESH)

        copies = [pltpu.make_async_copy(s, d, loc_sems.at[i]) for i, (s, d) in enumerate(loc)]
        rcopies = [remote(i, s, d, peer) for i, (s, d, peer) in enumerate(rem)]
        for cp in copies + rcopies:
            cp.start()
        landed = set()
        fcopies = []
        for i, (s, d, peer, k) in enumerate(fwd):
            if k not in landed:
                rcopies[k].wait_recv()
                landed.add(k)
            fcopies.append(remote(n_rem + i, s, d, peer))
            fcopies[-1].start()
        for k, cp in enumerate(rcopies):
            if k not in landed:
                cp.wait_recv()
        for cp in rcopies + fcopies:
            cp.wait_send()
        for cp in fcopies:
            cp.wait_recv()
        for cp in copies:
            cp.wait()

    return pl.pallas_call(
        body, name=name, in_specs=[ANY] * n_in, out_specs=[ANY] * len(out_shapes), out_shape=out_shapes,
        scratch_shapes=[pltpu.SemaphoreType.DMA((n_rem + n_fwd,)), pltpu.SemaphoreType.DMA((n_rem + n_fwd,)),
                        pltpu.SemaphoreType.DMA((2 * n_in,))],
        compiler_params=pltpu.CompilerParams(has_side_effects=True),
    )(*ins)


def _gather_plan(n_big, in_refs, out_refs, place):
    x, y, c = place
    chip, dev = 2 * x + y, 4 * x + 2 * y + c
    sib = (x, y, 1 - c)
    loc = [(in_refs[0], out_refs[0].at[dev])] + [(s, d.at[chip]) for s, d in zip(in_refs[1 + n_big:], out_refs[1 + n_big:])]
    rem = [(in_refs[0], out_refs[0].at[dev], (_flip(x, dx), _flip(y, dy), _flip(c, dc))) for dx, dy, dc in DEV_PEERS]
    fwd = []
    for s, d in zip(in_refs[1:1 + n_big], out_refs[1:1 + n_big]):
        for dx, dy in CHIP_PEERS:
            px, py = _flip(x, dx), _flip(y, dy)
            fwd.append((d.at[2 * px + py, c], d.at[2 * px + py, c], sib, len(rem)))
            rem.append((s.at[c], d.at[chip, c], (px, py, c)))
    for s, d in zip(in_refs[1 + n_big:], out_refs[1 + n_big:]):
        rem += [(s, d.at[chip], (_flip(x, dx), _flip(y, dy), c)) for dx, dy in CHIP_PEERS]
    return loc, rem, fwd


def _join_plan(in_refs, out_refs, place):
    x, y, c = place
    return [], [(s, d, (x, y, 1 - c)) for s, d in zip(in_refs, out_refs)]


def _scatter_plan(n_all, in_refs, out_refs, place):
    x, y, c = place
    chip, dev = 2 * x + y, 4 * x + 2 * y + c
    loc, rem = [], []
    for s, d in zip(in_refs[:n_all], out_refs[:n_all]):
        loc.append((s.at[dev], d.at[dev]))
        for dx, dy, dc in DEV_PEERS:
            px, py, pc = _flip(x, dx), _flip(y, dy), _flip(c, dc)
            rem.append((s.at[4 * px + 2 * py + pc], d.at[dev], (px, py, pc)))
    for s, d in zip(in_refs[n_all:], out_refs[n_all:]):
        for dx, dy in CHIP_PEERS:
            px, py = _flip(x, dx), _flip(y, dy)
            rem.append((s.at[2 * px + py], d.at[chip], (px, py, c)))
    return loc, rem


def _bshape(a, nb):
    return a.reshape(nb, 1, a.shape[-1])


def _with_prev(cur, before, tiles_per_seq):
    first = pl.program_id(0) % tiles_per_seq == 0
    row0 = jnp.where(first, 0.0, before[before.shape[0] - 1:, :])
    rid = lax.broadcasted_iota(jnp.int32, cur.shape, 0)
    return jnp.where(rid == 0, row0, pltpu.roll(cur, 1, 0))


def _with_next(cur, after, tiles_per_seq):
    last = pl.program_id(0) % tiles_per_seq == tiles_per_seq - 1
    n = cur.shape[0]
    row_n = jnp.where(last, 0.0, after[0:1, :])
    rid = lax.broadcasted_iota(jnp.int32, cur.shape, 0)
    return jnp.where(rid == n - 1, row_n, pltpu.roll(cur, n - 1, 0))


def _local_step(x2d, tgt, mod, wmain, wlora, late_w, ck, w2, a2, small, nb, S, grads_hook):
    T = nb * S
    shift, scale, gate = (_bshape(mod[:, i * D:(i + 1) * D], nb) for i in range(3))
    G = jnp.asarray(np.arange(128)[:, None] == np.arange(D)[None, :] // HN, dtype=BF16)
    cm = jnp.asarray(_chunk_consts())
    ckp = jnp.pad(ck, ((0, 1), (0, 0)))
    zpad = jnp.zeros((64, D), F32)
    w2p = jnp.concatenate([w2, zpad], axis=0)
    a2p = jnp.concatenate([zpad, a2], axis=0)
    mu = small["rwkv_mu"]
    mu_r, mu_k, mu_v, mu_l = mu[:, 0:D], mu[:, D:2 * D], mu[:, 2 * D:3 * D], mu[:, 3 * D:]
    g4 = [mu_r, mu_k, mu_v, mu_l, small["rwkv_w0"], w2p, small["rwkv_a0"], a2p, small["rwkv_k_k"], small["rwkv_k_a"], G]
    g5 = [small["rwkv_gn_g"], small["rwkv_gn_b"], small["rwkv_r_k"], G]
    g3 = [small["conv_b"], small["conv_ln_g"], small["conv_ln_b"]]

    (h,), _, _ = _rows(lambda r, b, g: ([_s1(r[0], g[0], b[0], b[1])], [], []), "pre_fwd", T, S, 256,
                       [(x2d, D, 0)], [scale, shift], [small["norm_g"]], [(D, BF16)], [], [])
    skip = (DMAIN, lambda g, t: pl.multiple_of(g * t + jnp.where(g * t >= 6 * D, LORA, 0), LORA))
    if len(late_w) == 3:
        pm = _matmul(h, wmain, "nt", "proj_main", min(T, 1024), 1024, D, b_rows=skip)
        wco, wro, wo = late_w
    else:
        pm, *landed = _matmul(h, wmain, "nt", "proj_main", min(T, 1024), 1024, D, ride=late_w[0], b_rows=skip)
        wco, wro, wo = late_w[1](landed)
    plo = _matmul(h, wlora, "nt", "proj_lora", 512, LORA, D)
    uc = _conv_fwd(pm, ckp, T, S)
    (uo,), _, _ = _rows(lambda r, b, g: ([_s3(r[0], r[1], *g)], [], []), "conv_post_fwd", T, S, 256,
                        [(uc, D, 0), (pm, D, 2)], [], g3, [(D, BF16)], [], [])
    yc = _matmul(uo, wco, "nn", "conv_out", 512, 1024, D)
    rows4 = [(pm, D, 3), (pm, D, 4), (pm, D, 5), (plo, LORA, 0),
             (pm, D, 3, "prev"), (pm, D, 4, "prev"), (pm, D, 5, "prev"), (plo, LORA, 0, "prev")]
    tps4 = S // 128

    def shifted4(r, tps=tps4):
        return list(r[:4]) + [_with_prev(r[i], r[4 + i], tps) for i in range(4)]

    sc_in, _, _ = _rows(lambda r, b, g: (list(_s4(*shifted4(r, S // 256), *g)), [], []), "rwkv_pre_fwd", T, S, 256,
                        rows4, [], g4, [(D, F32)] * 6, [], [])
    o, hs = _scan_fwd(sc_in, cm, nb, S)
    rows5 = [(o, D, 0), (sc_in[0], D, 0), (sc_in[2], D, 0), (sc_in[3], D, 0), (pm, D, 6)]
    (o2,), _, _ = _rows(lambda r, b, g: ([_s5(*r, *g)], [], []), "rwkv_post_fwd", T, S, 256,
                        rows5, [], g5, [(D, BF16)], [], [])
    yr = _matmul(o2, wro, "nn", "rwkv_out", 512, 1024, D)
    rows6 = [(yc, D, 0), (yr, D, 0), (pm, D, 7), (pm, D, 8)]
    (m,), _, _ = _rows(lambda r, b, g: ([_s6(*r)], [], []), "merge_fwd", T, S, 256, rows6, [], [], [(D, BF16)], [], [])
    out = _matmul(m, wo, "nn", "out_proj", 512, 1024, D)

    def head(r, b, g):
        loss, (dx, dout, dgate, dfg) = jax.value_and_grad(_s7, argnums=(0, 1, 3, 4))(r[0], r[1], r[2], b[0], g[0])
        return [dx, dout], [dgate], [dfg, jnp.full((1, 128), loss, F32)]

    (dx_res, dout), (dgate,), (d_final_g, loss_v) = _rows(
        head, "head", T, S, 256, [(x2d, D, 0), (out, D, 0), (tgt, D, 0)], [gate], [small["final_g"]],
        [(D, F32), (D, BF16)], [D], [(1, D), (1, 128)])

    d_wo = _matmul(m, dout, "tn", "d_w_out", 512, 1024, min(T, 2048))
    dm = _matmul(dout, wo, "nt", "d_merge", 512, 1024, D)

    def merge_bwd(r, b, g):
        _, vjp = jax.vjp(_s6, *r[:4])
        dyc, dyr, dgc, dgr = vjp(r[4])
        return [dyc, dyr, dgc, dgr], [], []

    (dyc, dyr, dgc, dgr), _, _ = _rows(merge_bwd, "merge_bwd", T, S, 256, rows6 + [(dm, D, 0)], [], [],
                                       [(D, BF16), (D, BF16), (D, BF16), (D, BF16)], [], [])
    d_wco = _matmul(uo, dyc, "tn", "d_w_conv_out", 512, 1024, min(T, 2048))
    d_wro = _matmul(o2, dyr, "tn", "d_w_rwkv_out", 512, 1024, min(T, 2048))
    duo = _matmul(dyc, wco, "nt", "d_conv_act", 512, 1024, D)
    do2 = _matmul(dyr, wro, "nt", "d_rwkv_act", 512, 1024, D)

    def conv_post_bwd(r, b, g):
        _, vjp = jax.vjp(_s3, r[0], r[1], *g)
        duc, dog, dcb, dlg, dlb = vjp(r[2])
        return [duc, dog], [], [dcb, dlg, dlb]

    (duc, dcog), _, (d_cb, d_lg, d_lb) = _rows(conv_post_bwd, "conv_post_bwd", T, S, 256,
                                               [(uc, D, 0), (pm, D, 2), (duo, D, 0)], [], g3,
                                               [(D, F32), (D, BF16)], [], [(1, D)] * 3)
    dval, dgt, d_ckp = _conv_bwd(pm, duc, ckp, T, S)

    def rwkv_post_bwd(r, b, g):
        _, vjp = jax.vjp(lambda *z: _s5(*z, g[3]), *r[:5], *g[:3])
        res = vjp(r[5])
        return list(res[:5]), [], list(res[5:8])

    (do, dr_b, dk_b, dv_b, drog), _, (d_gg, d_gb, d_rk) = _rows(
        rwkv_post_bwd, "rwkv_post_bwd", T, S, 256, rows5 + [(do2, D, 0)], [], g5,
        [(D, F32)] * 4 + [(D, BF16)], [], [(1, D)] * 3)
    dsc = _scan_bwd(sc_in, hs, do, cm, nb, S)

    def rwkv_pre_bwd(r, b, g):
        _, vjp = jax.vjp(lambda *z: _s4(*z, g[10]), *shifted4(r), *g[:10])
        ct = (r[8] + r[14], r[9], r[10] + r[15], r[11] + r[16], r[12], r[13])
        res = vjp(ct)
        return list(res[:8]), [], list(res[8:18])

    rows4b = rows4 + [(a, D, 0) for a in dsc] + [(dr_b, D, 0), (dk_b, D, 0), (dv_b, D, 0)]
    gshapes = [(1, D), (1, D), (1, D), (1, LORA), (1, D), (LORA, D), (1, D), (LORA, D), (1, D), (1, D)]
    dts, _, gts = _rows(rwkv_pre_bwd, "rwkv_pre_bwd", T, S, 128, rows4b, [], g4,
                        [(D, BF16)] * 3 + [(LORA, BF16)] + [(D, BF16)] * 3 + [(LORA, BF16)], [], gshapes)
    dr0, dk0, dv0, dl0, dpr, dpk, dpv, dpl = dts
    d_mu_r, d_mu_k, d_mu_v, d_mu_l, d_w0, d_w2p, d_a0, d_a2p, d_kk, d_ka = gts

    def assemble(r, b, g):
        r = [z.astype(F32) for z in r]
        sh = [_with_next(r[10 + i], r[14 + i], tps4) for i in range(4)]
        main = jnp.concatenate([r[0], r[1], r[2], r[3] + sh[0], r[4] + sh[1], r[5] + sh[2], r[6], r[7], r[8]], axis=1)
        return [main, r[9] + sh[3]], [], []

    rows_a = [(dval, D, 0), (dgt, D, 0), (dcog, D, 0), (dr0, D, 0), (dk0, D, 0), (dv0, D, 0), (drog, D, 0), (dgc, D, 0),
              (dgr, D, 0), (dl0, LORA, 0), (dpr, D, 0), (dpk, D, 0), (dpv, D, 0), (dpl, LORA, 0),
              (dpr, D, 0, "next"), (dpk, D, 0, "next"), (dpv, D, 0, "next"), (dpl, LORA, 0, "next")]
    (dpm, dplo), _, _ = _rows(assemble, "assemble_dp", T, S, 128, rows_a, [], [], [(DMAIN, BF16), (LORA, BF16)], [], [])
    RW = D // NCHIP
    c_i = lax.axis_index("c")
    halved = [g.reshape(NCHIP, 2, RW // 2, D).transpose(1, 0, 2, 3).reshape(2, NCHIP * RW // 2, D) for g in (d_wco, d_wro, d_wo)]
    halved += [g.reshape(-1, NCHIP, 2, RW // 2).transpose(2, 1, 0, 3).reshape(2, -1, RW // 2)
               for g in (d_ckp[:CW], d_w2p[:64], d_a2p[64:])]
    h_keep, h_send = (lax.dynamic_slice_in_dim(h, k * (D // 2), D // 2, axis=1) for k in (c_i, 1 - c_i))
    tk = min(T, 2048)
    send = [_matmul(h_send, dpm, "tn", "d_w_main_send", D // 2, 1024, tk, out_t=True)[0],
            _matmul(h_send, dplo, "tn", "d_w_lora_send", D // 2, LORA, tk, out_t=True)[0]]
    send += [lax.dynamic_index_in_dim(g, 1 - c_i, 0, keepdims=False) for g in halved]
    keep = [None, _matmul(h_keep, dplo, "tn", "d_w_lora_keep", D // 2, LORA, tk, out_t=True)[0]]
    keep += [lax.dynamic_index_in_dim(g, c_i, 0, keepdims=False) for g in halved]

    def to_sibling(in_refs, out_refs, place):
        x, y, c = place
        return [], [(s, d, (x, y, 1 - c)) for s, d in zip(in_refs, out_refs)]

    d_w_keep, *got_h = _matmul(h_keep, dpm, "tn", "d_w_main_keep", D // 2, 1024, tk, out_t=True,
                               ride=(send, [jax.ShapeDtypeStruct(t.shape, F32) for t in send], to_sibling, len(send)))
    keep[0] = d_w_keep[0]
    chip_part = [_ew(lambda p, q: [p + q], "chip_sum_%d" % i, [keep[i], got_h[i]], 1, 1024, BF16)[0]
                 for i in range(2, len(keep))]

    def sum_body(p_ref, q_ref, *rest):
        rest[-1][...] = (p_ref[...] + q_ref[...]).astype(BF16)

    blk = pl.BlockSpec((1024, D // 2), lambda i: (i, 0))
    d_win_h = pl.pallas_call(
        sum_body, name="chip_sum_w_in", grid=(DMAIN // 1024,), in_specs=[blk, blk],
        out_specs=pl.BlockSpec((pl.Element(1024), pl.Element(D // 2)), lambda i: (skip[1](i, 1024), 0)),
        out_shape=jax.ShapeDtypeStruct((DMAIN + LORA, D // 2), BF16), compiler_params=_cparams(("parallel",)),
    )(keep[0], got_h[0])
    lora_blk = pl.BlockSpec((LORA, D // 2), lambda i: (0, 0))
    d_win_h = pl.pallas_call(
        sum_body, name="chip_sum_w_lora", grid=(1,), in_specs=[lora_blk, lora_blk, ANY],
        out_specs=pl.BlockSpec((LORA, D // 2), lambda i: (6 * D // LORA, 0)),
        out_shape=jax.ShapeDtypeStruct((DMAIN + LORA, D // 2), BF16), input_output_aliases={2: 0},
        compiler_params=_cparams(("arbitrary",)),
    )(keep[1], got_h[1], d_win_h)
    chip_part = [d_win_h] + chip_part
    dh_m, *got_big = _matmul(dpm, wmain, "nn", "d_h_main", 512, 1024, 3072, ride=grads_hook(chip_part), b_rows=skip)
    dh_l = _matmul(dplo, wlora, "nn", "d_h_lora", 512, 1024, LORA)

    def pre_bwd(r, b, g):
        _, vjp = jax.vjp(_s1, r[0], g[0], b[0], b[1])
        dx, dg, dscale, dshift = vjp(r[1] + r[2])
        return [dx + r[3]], [dscale, dshift], [dg]

    (gx,), (dscale, dshift), (d_ng,) = _rows(pre_bwd, "pre_bwd", T, S, 256,
                                             [(x2d, D, 0), (dh_m, D, 0), (dh_l, D, 0), (dx_res, D, 0)],
                                             [scale, shift], [small["norm_g"]], [(D, F32)], [D, D], [(1, D)])
    dmod = jnp.concatenate([dshift, dscale, dgate], axis=-1).reshape(nb, 3 * D)
    d_small = {"norm_g": d_ng, "conv_b": d_cb, "conv_ln_g": d_lg, "conv_ln_b": d_lb,
               "rwkv_mu": jnp.concatenate([d_mu_r, d_mu_k, d_mu_v, d_mu_l], axis=1),
               "rwkv_w0": d_w0, "rwkv_a0": d_a0, "rwkv_k_k": d_kk, "rwkv_k_a": d_ka, "rwkv_r_k": d_rk,
               "rwkv_gn_g": d_gg, "rwkv_gn_b": d_gb, "final_g": d_final_g}
    return loss_v[0, 0], gx, dmod, got_big, d_small


def _step(a):
    nb, S, _ = a["x"].shape
    T = nb * S
    x_i, y_i, c_i = _place()
    chip = 2 * x_i + y_i
    w_in_t, m_w_in_t, v_w_in_t = (jnp.transpose(a[p + "w_in"][0]) for p in ("", "m_", "v_"))
    WS = w_in_t.shape[0]
    small_w = {n: a[n].reshape(1, sz) for n, sz in SMALL}

    def halves(t):
        return t.reshape(2, t.shape[0] // 2, t.shape[1])

    g_ins = [a["c"], halves(w_in_t.astype(BF16)), a["conv_k"][0], a["rwkv_w2"][0], a["rwkv_a2"][0]]
    g_out = [jax.ShapeDtypeStruct((NDEV,) + g_ins[0].shape, F32)]
    g_out += [jax.ShapeDtypeStruct((NCHIP,) + t.shape, t.dtype) for t in g_ins[1:]]
    c_all, win_g, ck_g, w2_g, a2_g = _comm_call(
        "gather_weights", g_ins, g_out, functools.partial(_gather_plan, 1), 7 + 3 * 4, 3)
    c_all = c_all.reshape(NDEV * nb, D)
    win_t = lax.dynamic_update_index_in_dim(win_g, g_ins[1], chip, 0).reshape(NCHIP * WS, D)
    late = [a[n][0].astype(BF16) for n in ("w_conv_out", "w_rwkv_out", "w_out")]

    def late_plan(in_refs, out_refs, place):
        x, y, c = place
        return [], [(s, d.at[2 * x + y], (_flip(x, dx), _flip(y, dy), c))
                    for s, d in zip(in_refs, out_refs) for dx, dy in CHIP_PEERS]

    def late_finish(landed):
        return [lax.dynamic_update_index_in_dim(g, own, chip, 0).reshape(D, D) for g, own in zip(landed, late)]

    late_w = ((late, [jax.ShapeDtypeStruct((NCHIP,) + t.shape, BF16) for t in late], late_plan, 9), late_finish)
    wmain = win_t
    wlora = win_t[6 * D:6 * D + LORA]
    ck = jnp.concatenate([ck_g[j] for j in range(NCHIP)], axis=1)
    w2 = jnp.concatenate([w2_g[j] for j in range(NCHIP)], axis=1)
    a2 = jnp.concatenate([a2_g[j] for j in range(NCHIP)], axis=1)

    ada_w = a["ada_w"][0]
    MW = ada_w.shape[1]
    ada_b_loc = lax.dynamic_slice(a["ada_b"], (0, chip * MW), (1, MW))

    def mod_body(c_ref, w_ref, b_ref, o_ref):
        o_ref[...] = _dot(_silu(c_ref[...]), w_ref[...], HI) + b_ref[...]

    modp = pl.pallas_call(mod_body, name="ada_mod", out_shape=jax.ShapeDtypeStruct((NDEV * nb, MW), F32),
                          compiler_params=_cparams())(c_all, ada_w, ada_b_loc)
    (mod_g,) = _comm_call("scatter_mod", [modp.reshape(NDEV, nb, MW)],
                          [jax.ShapeDtypeStruct((NDEV, nb, MW), F32)],
                          functools.partial(_scatter_plan, 1), 7)
    mod = mod_g.reshape(NCHIP, 2, nb, MW)
    mod = mod[:, 0].transpose(1, 0, 2).reshape(nb, NCHIP * MW)

    RW = D // NCHIP
    sh_s = []

    def grads_hook(chip_part):
        sh_s.append(chip_part[0].reshape(NCHIP, WS, D // 2))
        sh_s.extend(t.reshape(NCHIP, RW // 2, D) for t in chip_part[1:4])
        sh_s.extend(t.reshape(NCHIP, -1, RW // 2) for t in chip_part[4:])
        return (sh_s, [jax.ShapeDtypeStruct(t.shape, t.dtype) for t in sh_s], functools.partial(_scatter_plan, 0),
                3 * len(sh_s))

    loss_p, gx, dmod, got_big, d_small = _local_step(
        a["x"].reshape(T, D), a["loss_target"].reshape(T, D), mod, wmain, wlora, late_w, ck, w2, a2, small_w, nb, S,
        grads_hook)
    loss = lax.psum(loss_p, ("x", "y", "c"))

    d_small["ada_b"] = _colsum(dmod)
    small_vec = jnp.concatenate([d_small[n] for n, _ in SMALL], axis=1)
    dmod_s = dmod.reshape(nb, NCHIP, MW).transpose(1, 0, 2)
    dmod_s = jnp.repeat(dmod_s, 2, axis=0)
    small_s = jnp.broadcast_to(small_vec[None], (NDEV, 1, NSMALL))
    got = _comm_call("scatter_small", [dmod_s, small_s], [jax.ShapeDtypeStruct(t.shape, F32) for t in (dmod_s, small_s)],
                     functools.partial(_scatter_plan, 2), 14)
    dmod_all, small_all = got[0].reshape(NDEV * nb, MW), got[1].reshape(NDEV, NSMALL)

    def shard_sum(recv, sent):
        chip_i = 2 * lax.axis_index("x") + lax.axis_index("y")
        s = None
        for j in range(NCHIP):
            t = jnp.where(chip_i == j, sent[j], recv[j]).astype(F32)
            s = t if s is None else s + t
        return [s]

    fin = [_ew(shard_sum, "shard_sum_%d" % i, [t, sh_s[i]], 1, 128)[0] for i, t in enumerate(got_big)]
    oth = _comm_call("join_halves", fin, [jax.ShapeDtypeStruct(t.shape, F32) for t in fin], _join_plan, len(fin))

    outs = {}

    def upd_halves(name, mine, other):
        shp = a[name].shape
        R, W = 2 * mine.shape[0], mine.shape[1]
        tm = 128
        nh = R // 2 // tm

        def body(w_ref, m_ref, v_ref, f_ref, o_ref, g_ref, d_ref, m2_ref, v2_ref):
            g = jnp.where(pl.program_id(0) // nh == lax.axis_index("c"), f_ref[...], o_ref[...])
            g_ref[...] = g
            d_ref[...], m2_ref[...], v2_ref[...] = _adamw(w_ref[...], g, m_ref[...], v_ref[...])

        full = pl.BlockSpec((None, tm, W), lambda i: (0, i, 0))
        half = pl.BlockSpec((tm, W), lambda i: (i % nh, 0))
        assert shp == (1, R, W)
        outs[name] = pl.pallas_call(
            body, name="adamw_" + name, grid=(R // tm,), in_specs=[full] * 3 + [half] * 2, out_specs=[full] * 4,
            out_shape=[jax.ShapeDtypeStruct(shp, F32)] * 4, compiler_params=_cparams(("parallel",)),
        )(*[a[p + name] for p in ("", "m_", "v_")], mine, other)

    for name, f, o in zip(("w_conv_out", "w_rwkv_out", "w_out"), fin[1:4], oth[1:4]):
        upd_halves(name, f, o)

    tw = WS // 4

    def w_in_body(w_ref, m_ref, v_ref, f_ref, o_ref, g_ref, d_ref, m2_ref, v2_ref):
        first = lax.axis_index("c") == 0
        g = jnp.concatenate([jnp.where(first, f_ref[...], o_ref[...]), jnp.where(first, o_ref[...], f_ref[...])], axis=1)
        g_ref[...] = g
        d_ref[...], m2_ref[...], v2_ref[...] = _adamw(w_ref[...], g, m_ref[...], v_ref[...])

    full = pl.BlockSpec((tw, D), lambda i: (i, 0))
    half = pl.BlockSpec((tw, D // 2), lambda i: (i, 0))
    res = pl.pallas_call(
        w_in_body, name="adamw_w_in", grid=(WS // tw,), in_specs=[full] * 3 + [half] * 2, out_specs=[full] * 4,
        out_shape=[jax.ShapeDtypeStruct((WS, D), F32)] * 4, compiler_params=_cparams(("parallel",)),
    )(w_in_t, m_w_in_t, v_w_in_t, fin[0], oth[0])
    outs["w_in"] = [jnp.transpose(r)[None] for r in res]

    def upd(name, g):
        shp = a[name].shape
        ins = [a[p + name].reshape(g.shape) for p in ("", "m_", "v_")]
        res = _ew(lambda w_, m_, v_, g_: [g_, *_adamw(w_, g_, m_, v_)], "adamw_" + name, [*ins, g], 4, 128)
        outs[name] = [r.reshape(shp) for r in res]

    for name, f, o in zip(("conv_k", "rwkv_w2", "rwkv_a2"), fin[4:], oth[4:]):
        both = jnp.where(c_i == 0, jnp.stack([f, o]), jnp.stack([o, f]))
        upd(name, both.transpose(1, 0, 2).reshape(-1, RW))

    def adaw_body(c_ref, dm_ref, w_ref, m_ref, v_ref, g_ref, d_ref, m2_ref, v2_ref):
        g = _dot_tn(_silu(c_ref[...]), dm_ref[...], HI)
        g_ref[...] = g
        d_ref[...], m2_ref[...], v2_ref[...] = _adamw(w_ref[...], g, m_ref[...], v_ref[...])

    res = pl.pallas_call(adaw_body, name="adamw_ada_w", out_shape=[jax.ShapeDtypeStruct((D, MW), F32)] * 4,
                         compiler_params=_cparams())(c_all, dmod_all, ada_w, a["m_ada_w"][0], a["v_ada_w"][0])
    outs["ada_w"] = [r.reshape(a["ada_w"].shape) for r in res]

    wv, mv, vv = (jnp.concatenate([a[p + n].reshape(1, sz) for n, sz in SMALL], axis=1) for p in ("", "m_", "v_"))
    def small_fn(w_, m_, v_, gs):
        g = _sum_slots(gs)
        return [g, *_adamw(w_, g, m_, v_)]

    res = _ew(small_fn, "adamw_small", [wv, mv, vv, small_all.reshape(NDEV, 1, NSMALL)], 4, 8)
    off = 0
    for n, sz in SMALL:
        outs[n] = [r[:, off:off + sz].reshape(a[n].shape) for r in res]
        off += sz

    return (loss, gx.reshape(nb, S, D), *[outs[n][0] for n in WEIGHTS], *[outs[n][1] for n in WEIGHTS],
            *[outs[n][2] for n in WEIGHTS], *[outs[n][3] for n in WEIGHTS])


def _colsum(dmod):
    def body(d_ref, o_ref):
        o_ref[...] = jnp.sum(d_ref[...], axis=0, keepdims=True)
    return pl.pallas_call(body, name="ada_b_rowsum", out_shape=jax.ShapeDtypeStruct((1, dmod.shape[1]), F32),
                          compiler_params=_cparams())(dmod)


def kernel(x, c, ada_w, ada_b, norm_g, w_in, conv_k, conv_b, conv_ln_g, conv_ln_b, w_conv_out, rwkv_mu, rwkv_w0, rwkv_w2, rwkv_a0, rwkv_a2, rwkv_k_k, rwkv_k_a, rwkv_r_k, rwkv_gn_g, rwkv_gn_b, w_rwkv_out, w_out, final_g, loss_target, m_ada_w, m_ada_b, m_norm_g, m_w_in, m_conv_k, m_conv_b, m_conv_ln_g, m_conv_ln_b, m_w_conv_out, m_rwkv_mu, m_rwkv_w0, m_rwkv_w2, m_rwkv_a0, m_rwkv_a2, m_rwkv_k_k, m_rwkv_k_a, m_rwkv_r_k, m_rwkv_gn_g, m_rwkv_gn_b, m_w_rwkv_out, m_w_out, m_final_g, v_ada_w, v_ada_b, v_norm_g, v_w_in, v_conv_k, v_conv_b, v_conv_ln_g, v_conv_ln_b, v_w_conv_out, v_rwkv_mu, v_rwkv_w0, v_rwkv_w2, v_rwkv_a0, v_rwkv_a2, v_rwkv_k_k, v_rwkv_k_a, v_rwkv_r_k, v_rwkv_gn_g, v_rwkv_gn_b, v_w_rwkv_out, v_w_out, v_final_g):
    return _step(dict(locals()))
```

```python
import functools

import numpy as np
import jax
import jax.numpy as jnp
from jax import lax
from jax.experimental import pallas as pl
from jax.experimental.pallas import tpu as pltpu

F32 = jnp.float32
BF16 = jnp.bfloat16
HI = lax.Precision.HIGHEST
MESH = pl.DeviceIdType.MESH
ANY = pl.BlockSpec(memory_space=pl.ANY)

D = 1024
NH = 16
HN = 64
LORA = 128
DMAIN = 9 * D
CH = 64
CW = 31
NCHIP = 4
NDEV = 8
VMEM_LIMIT = 56 * 1024 * 1024

RMS_EPS = 1e-6
LN_EPS = 1e-5
GN_EPS = 64e-5
L2_EPS = 1e-12
ADAM_LR = 0.001
ADAM_B1 = 0.9
ADAM_B2 = 0.999
ADAM_EPS = 1e-08
ADAM_WD = 0.01
ADAM_STEP = 10

SMALL = (("ada_b", 3072), ("norm_g", 1024), ("conv_b", 1024), ("conv_ln_g", 1024), ("conv_ln_b", 1024),
         ("rwkv_mu", 3200), ("rwkv_w0", 1024), ("rwkv_a0", 1024), ("rwkv_k_k", 1024), ("rwkv_k_a", 1024),
         ("rwkv_r_k", 1024), ("rwkv_gn_g", 1024), ("rwkv_gn_b", 1024), ("final_g", 1024))
NSMALL = sum(n for _, n in SMALL)

WEIGHTS = ['ada_w', 'ada_b', 'norm_g', 'w_in', 'conv_k', 'conv_b', 'conv_ln_g', 'conv_ln_b', 'w_conv_out', 'rwkv_mu',
           'rwkv_w0', 'rwkv_w2', 'rwkv_a0', 'rwkv_a2', 'rwkv_k_k', 'rwkv_k_a', 'rwkv_r_k', 'rwkv_gn_g', 'rwkv_gn_b',
           'w_rwkv_out', 'w_out', 'final_g']


def _cparams(sem=None, **kw):
    if sem is not None:
        kw["dimension_semantics"] = sem
    return pltpu.CompilerParams(vmem_limit_bytes=VMEM_LIMIT, **kw)


def _dot(a, b, prec=None):
    return jnp.dot(a, b, preferred_element_type=F32, precision=prec)


def _dot_nt(a, b, prec=None):
    return lax.dot_general(a, b, (((1,), (1,)), ((), ())), preferred_element_type=F32, precision=prec)


def _dot_tn(a, b, prec=None):
    return lax.dot_general(a, b, (((0,), (0,)), ((), ())), preferred_element_type=F32, precision=prec)


def _pdot(f, a, b, p):
    if p == "hi":
        return f(a, b, HI)
    ah, bh = a.astype(BF16), b.astype(BF16)
    if p == "bf":
        return f(ah, bh)
    al, bl = (a - ah.astype(F32)).astype(BF16), (b - bh.astype(F32)).astype(BF16)
    return f(ah, bh) + (f(ah, bl) + f(al, bh))


P_SCORE = "b3"
P_INV = "bf"
P_APPLY = "bf"


def _sigmoid(z):
    return 1.0 / (1.0 + jnp.exp(-z))


def _silu(z):
    return z * _sigmoid(z)


def _matmul(a, b, mode, name, tm, tn, tk, ride=None, out_t=False, b_rows=None):
    if mode == "nn":
        (M, K), N = a.shape, b.shape[1]
        a_spec = pl.BlockSpec((tm, tk), lambda j, i, k: (i, k))
        b_spec = pl.BlockSpec((tk, tn), lambda j, i, k: (k, j))
        if b_rows is not None:
            assert b_rows[0] == K
            b_spec = pl.BlockSpec((pl.Element(tk), pl.Element(tn)), lambda j, i, k: (b_rows[1](k, tk), j * tn))
        f = _dot
    elif mode == "nt":
        (M, K), N = a.shape, b.shape[0]
        a_spec = pl.BlockSpec((tm, tk), lambda j, i, k: (i, k))
        b_spec = pl.BlockSpec((tn, tk), lambda j, i, k: (j, k))
        if b_rows is not None:
            N = b_rows[0]
            b_spec = pl.BlockSpec((pl.Element(tn), pl.Element(tk)), lambda j, i, k: (b_rows[1](j, tn), k * tk))
        f = _dot_nt
    else:
        (K, M), N = a.shape, b.shape[1]
        a_spec = pl.BlockSpec((tk, tm), lambda j, i, k: (k, i))
        b_spec = pl.BlockSpec((tk, tn), lambda j, i, k: (k, j))
        f = _dot_tn
    assert M % tm == 0 and N % tn == 0 and K % tk == 0, (name, M, N, K)

    grid = (N // tn, M // tm, K // tk)
    o_spec = pl.BlockSpec((tm, tn), lambda j, i, k: (i, j))
    o_shape = jax.ShapeDtypeStruct((M, N), F32)

    scratch = []
    if out_t:
        o_spec = pl.BlockSpec((None, tn, tm), lambda j, i, k: (i, j, 0))
        o_shape = jax.ShapeDtypeStruct((M // tm, N, tm), F32)
        scratch = [pltpu.VMEM((tm, tn), F32)]

    def step(a_ref, b_ref, o_ref, *acc):
        acc_ref = acc[0] if out_t else o_ref

        @pl.when(pl.program_id(2) == 0)
        def _():
            acc_ref[...] = jnp.zeros_like(acc_ref)
        acc_ref[...] += f(a_ref[...], b_ref[...])
        if out_t:
            @pl.when(pl.program_id(2) == grid[2] - 1)
            def _():
                o_ref[...] = acc_ref[...].T

    if ride is None:
        return pl.pallas_call(
            step, name=name, grid=grid, in_specs=[a_spec, b_spec], out_specs=o_spec, out_shape=o_shape,
            scratch_shapes=scratch, compiler_params=_cparams(("parallel", "parallel", "arbitrary")),
        )(a, b)

    r_ins, r_shapes, plan, n_rem = ride
    n_ri, n_ro = len(r_ins), len(r_shapes)

    def body(a_ref, b_ref, *rest):
        r_in, o_ref, r_out = rest[:n_ri], rest[n_ri], rest[n_ri + 1:n_ri + 1 + n_ro]
        send_sems, recv_sems, *acc = rest[n_ri + 1 + n_ro:]
        loc, rem = plan(r_in, r_out, _place())
        assert not loc and len(rem) == n_rem, (name, len(loc), len(rem))
        copies = [pltpu.make_async_remote_copy(src_ref=s, dst_ref=d, send_sem=send_sems.at[i], recv_sem=recv_sems.at[i],
                                               device_id=peer, device_id_type=MESH) for i, (s, d, peer) in enumerate(rem)]
        pid = [pl.program_id(ax) for ax in range(3)]

        @pl.when((pid[0] == 0) & (pid[1] == 0) & (pid[2] == 0))
        def _():
            for cp in copies:
                cp.start()

        step(a_ref, b_ref, o_ref, *acc)

        @pl.when((pid[0] == grid[0] - 1) & (pid[1] == grid[1] - 1) & (pid[2] == grid[2] - 1))
        def _():
            for cp in copies:
                cp.wait_send()
            for cp in copies:
                cp.wait_recv()

    return pl.pallas_call(
        body, name=name, grid=grid, in_specs=[a_spec, b_spec] + [ANY] * n_ri, out_specs=[o_spec] + [ANY] * n_ro,
        out_shape=[o_shape] + list(r_shapes),
        scratch_shapes=[pltpu.SemaphoreType.DMA((n_rem,)), pltpu.SemaphoreType.DMA((n_rem,))] + scratch,
        compiler_params=_cparams(("arbitrary", "arbitrary", "arbitrary"), has_side_effects=True),
    )(a, b, *r_ins)


def _rows(fn, name, T, S, tm, rows, bpars, gpars, outs, baccs, gaccs):
    nb = T // S
    tps = S // tm
    n_r, n_b, n_g, n_o, n_ba, n_ga = len(rows), len(bpars), len(gpars), len(outs), len(baccs), len(gaccs)

    def body(*refs):
        r_refs = refs[:n_r]
        b_refs = refs[n_r:n_r + n_b]
        g_refs = refs[n_r + n_b:n_r + n_b + n_g]
        o_refs = refs[n_r + n_b + n_g:n_r + n_b + n_g + n_o]
        ba_refs = refs[n_r + n_b + n_g + n_o:n_r + n_b + n_g + n_o + n_ba]
        ga_refs = refs[n_r + n_b + n_g + n_o + n_ba:]
        i = pl.program_id(0)
        o_vals, ba_vals, ga_vals = fn([r[...] for r in r_refs], [r[...] for r in b_refs], [r[...] for r in g_refs])
        for r, v in zip(o_refs, o_vals):
            r[...] = v.astype(r.dtype)
        if n_ba:
            @pl.when(i % tps == 0)
            def _():
                for r in ba_refs:
                    r[...] = jnp.zeros_like(r)
            for r, v in zip(ba_refs, ba_vals):
                r[...] += v.reshape(r.shape)
        if n_ga:
            @pl.when(i == 0)
            def _():
                for r in ga_refs:
                    r[...] = jnp.zeros_like(r)
            for r, v in zip(ga_refs, ga_vals):
                r[...] += v.reshape(r.shape)

    def row_spec(arr, w, cb, kind="tile"):
        hr = 8 * (4 // arr.dtype.itemsize)
        if kind == "prev":
            return pl.BlockSpec((hr, w), lambda i: (jnp.maximum(i * (tm // hr) - 1, 0), cb))
        if kind == "next":
            return pl.BlockSpec((hr, w), lambda i: (jnp.minimum((i + 1) * (tm // hr), T // hr - 1), cb))
        return pl.BlockSpec((tm, w), lambda i: (i, cb))

    in_specs = [row_spec(*r) for r in rows]
    in_specs += [pl.BlockSpec((None, 1, p.shape[-1]), lambda i: (i // tps, 0, 0)) for p in bpars]
    in_specs += [pl.BlockSpec(p.shape, lambda i: (0, 0)) for p in gpars]
    out_specs = [pl.BlockSpec((tm, w), lambda i: (i, 0)) for w, _ in outs]
    out_specs += [pl.BlockSpec((None, 1, w), lambda i: (i // tps, 0, 0)) for w in baccs]
    out_specs += [pl.BlockSpec(s, lambda i: (0, 0)) for s in gaccs]
    out_shape = [jax.ShapeDtypeStruct((T, w), dt) for w, dt in outs]
    out_shape += [jax.ShapeDtypeStruct((nb, 1, w), F32) for w in baccs]
    out_shape += [jax.ShapeDtypeStruct(s, F32) for s in gaccs]
    res = pl.pallas_call(
        body, name=name, grid=(T // tm,), in_specs=in_specs, out_specs=out_specs, out_shape=out_shape,
        compiler_params=_cparams(("arbitrary",)),
    )(*[r[0] for r in rows], *bpars, *gpars)
    return res[:n_o], res[n_o:n_o + n_ba], res[n_o + n_ba:]


@jax.custom_vjp
def _gsum(z, G):
    zh = z.astype(BF16)
    zl = (z - zh.astype(F32)).astype(BF16)
    r = _dot_nt(zh, G) + _dot_nt(zl, G)
    rh = r.astype(BF16)
    rl = (r - rh.astype(F32)).astype(BF16)
    return _dot(rh, G) + _dot(rl, G)


def _dot3(x, w):
    xh = x.astype(BF16).astype(F32)
    wh = w.astype(BF16).astype(F32)
    xc = jnp.concatenate([xh, xh, x - xh], axis=1).astype(BF16)
    wc = jnp.concatenate([wh, w - wh, wh], axis=0).astype(BF16)
    return _dot(xc, wc)


_gsum.defvjp(lambda z, G: (_gsum(z, G), G), lambda G, ct: (_gsum(ct, G), jnp.zeros_like(G)))


def _s1(x, g, scale, shift):
    y = x * lax.rsqrt(jnp.mean(x * x, axis=-1, keepdims=True) + RMS_EPS)
    return (y * g) * (1.0 + scale) + shift


def _s3(uc, og, cb, lg, lb):
    u = uc + cb
    mu = jnp.mean(u, axis=-1, keepdims=True)
    d = u - mu
    var = jnp.mean(d * d, axis=-1, keepdims=True)
    y = d * lax.rsqrt(var + LN_EPS) * lg + lb
    return _silu(y) * _silu(og)


def _s4(r0, k0, v0, l0, pr, pk, pv, plo, mu_r, mu_k, mu_v, mu_l, w0, w2p, a0, a2p, k_k, k_a, G):
    r = r0 + mu_r * (pr - r0)
    k = k0 + mu_k * (pk - k0)
    v = v0 + mu_v * (pv - v0)
    lo = l0 + mu_l * (plo - l0)
    w_pre = w0 + _dot3(jnp.tanh(lo), w2p)
    lw = -np.float32(np.exp(-0.5)) * _sigmoid(w_pre)
    a = _sigmoid(a0 + _dot3(lo, a2p))
    kkr = k * k_k
    ss = _gsum(kkr * kkr, G)
    kk = kkr / jnp.maximum(jnp.sqrt(ss), L2_EPS)
    k2 = k * (1.0 + (a - 1.0) * k_a)
    return r, lw, k2, v, kk, kk * a


def _s5(o, r, k2, v, og, gg, gb, rk, G):
    mu = _gsum(o, G) * (1.0 / HN)
    d = o - mu
    var = _gsum(d * d, G) * (1.0 / HN)
    y = d * lax.rsqrt(var + GN_EPS) * gg + gb
    bonus = _gsum(r * k2 * rk, G)
    return (y + bonus * v) * _silu(og)


def _s6(yc, yr, gc, gr):
    return _sigmoid(gc) * yc + _sigmoid(gr) * yr


def _s7(x, out, tgt, gate, fg):
    x2 = x + gate * out
    y = x2 * lax.rsqrt(jnp.mean(x2 * x2, axis=-1, keepdims=True) + RMS_EPS) * fg
    e = y - tgt
    return 0.5 * jnp.sum(jnp.mean(e * e, axis=-1))


def _solve_all_fwd(a_kbs, rhss, cm):
    H = range(len(a_kbs))
    xi = [cm[2] - cm[3] * a_kbs[j] for j in H]
    for lvl in range(1, 6):
        t = [_pdot(_dot, xi[j], cm[3 + lvl] * a_kbs[j], P_INV) for j in H]
        xi = [xi[j] - _pdot(_dot, t[j], xi[j], P_INV) for j in H]
    u = tuple(_pdot(_dot, xi[j], rhss[j], P_APPLY) for j in H)
    return u, (xi, u, cm)


def _solve_all_bwd(res, dus):
    xi, u, cm = res
    H = range(len(u))
    g = tuple(_pdot(_dot_tn, xi[j], dus[j], P_APPLY) for j in H)
    da = tuple(-(cm[1] * _pdot(_dot_nt, g[j], u[j], P_APPLY)) for j in H)
    return da, g, jnp.zeros_like(cm)


@jax.custom_vjp
def _solve_all(a_kbs, rhss, cm):
    return _solve_all_fwd(a_kbs, rhss, cm)[0]


_solve_all.defvjp(_solve_all_fwd, _solve_all_bwd)


def _chunk(sts, r, lw, k, v, kk, b, cm):
    cum = _dot(cm[0], lw, HI)
    ein = jnp.exp(-cum)
    rt = r * jnp.exp(cum)
    kkt = kk * jnp.exp(cum - lw)
    kh = k * ein
    bh = b * ein
    ec = jnp.exp(jnp.sum(lw, axis=0, keepdims=True))
    khe = kh * ec
    bhe = bh * ec
    H = range(len(sts))
    tri, strict, eye = cm[0], cm[1], cm[2]
    rt, kkt, kh, bh, v, khe, bhe, ec = ([a[:, j * HN:(j + 1) * HN] for j in H] for a in (rt, kkt, kh, bh, v, khe, bhe, ec))
    lhs = [jnp.concatenate([kkt[j], rt[j]], axis=0) for j in H]
    rhs_s = [jnp.concatenate([bh[j], kh[j]], axis=0) for j in H]
    lh = [a.astype(BF16).astype(F32) for a in lhs]
    rh = [a.astype(BF16).astype(F32) for a in rhs_s]
    lc = [jnp.concatenate([lh[j], lh[j], lhs[j] - lh[j]], axis=1).astype(BF16) for j in H]
    rc = [jnp.concatenate([rh[j], rhs_s[j] - rh[j], rh[j]], axis=1).astype(BF16) for j in H]
    sc = [_dot_nt(lc[j], rc[j]) for j in H]
    a_kb = [strict * sc[j][:CH, :CH] for j in H]
    a_kk = [strict * sc[j][:CH, CH:] for j in H]
    a_rb = [tri * sc[j][CH:, :CH] for j in H]
    a_rk = [tri * sc[j][CH:, CH:] for j in H]
    ps = [_dot_nt(lhs[j].astype(BF16), sts[j].astype(BF16)) for j in H]
    pv = [_dot(jnp.concatenate([a_kk[j], a_rk[j]], axis=0).astype(BF16), v[j].astype(BF16)) for j in H]
    rhs = [ps[j][:CH] + pv[j][:CH] for j in H]
    o0 = [ps[j][CH:] + pv[j][CH:] for j in H]
    u = _solve_all(tuple(a_kb), tuple(rhs), cm)
    o = [o0[j] - _pdot(_dot, a_rb[j], u[j], P_APPLY) for j in H]
    st2 = [sts[j] * ec[j] + _dot_tn(jnp.concatenate([v[j], u[j]], axis=0).astype(BF16),
                                    jnp.concatenate([khe[j], -bhe[j]], axis=0).astype(BF16)) for j in H]
    return jnp.concatenate(o, axis=1), tuple(st2)


def _chunk_consts():
    t = np.arange(CH)[:, None]
    s = np.arange(CH)[None, :]
    mats = [(t >= s), (t > s), (t == s)]
    for lvl in range(6):
        sz = 1 << lvl
        mats.append(((t // sz) % 2 == 1) & ((s // sz) == (t // sz) - 1))
    mats.append(np.zeros((CH, CH), bool))
    return np.stack(mats).astype(np.float32)


def _adamw(w, g, m, v):
    m = ADAM_B1 * m + (1.0 - ADAM_B1) * g
    v = ADAM_B2 * v + (1.0 - ADAM_B2) * (g * g)
    m_hat = m / (1.0 - ADAM_B1 ** ADAM_STEP)
    v_hat = v / (1.0 - ADAM_B2 ** ADAM_STEP)
    delta = -ADAM_LR * (m_hat / (jnp.sqrt(v_hat) + ADAM_EPS) + ADAM_WD * w)
    return delta, m, v


CT = 128
RB = 64
WIN = RB + 32


def _conv_fwd(pm, ck, T, S):
    nb = T // S

    def body(val_ref, gate_ref, ck_ref, out_ref, ubuf):
        ubuf[0:32, :] = jnp.zeros((32, CT), F32)
        ubuf[32:, :] = val_ref[...] * _sigmoid(gate_ref[...])

        def blk(rb, carry):
            base = pl.multiple_of(rb * RB, RB)
            win = ubuf[pl.ds(base, WIN), :]
            acc = jnp.zeros((RB, CT), F32)
            for j in range(CW):
                acc = acc + ck_ref[j:j + 1, :] * pltpu.roll(win, (WIN - (2 + j)) % WIN, 0)[0:RB, :]
            out_ref[pl.ds(base, RB), :] = acc
            return carry

        lax.fori_loop(0, S // RB, blk, 0)

    return pl.pallas_call(
        body, name="conv_fwd", grid=(D // CT, nb),
        in_specs=[pl.BlockSpec((S, CT), lambda ct, b: (b, ct)),
                  pl.BlockSpec((S, CT), lambda ct, b: (b, D // CT + ct)),
                  pl.BlockSpec((32, CT), lambda ct, b: (0, ct))],
        out_specs=pl.BlockSpec((S, CT), lambda ct, b: (b, ct)),
        out_shape=jax.ShapeDtypeStruct((T, D), F32),
        scratch_shapes=[pltpu.VMEM((S + 32, CT), F32)],
        compiler_params=_cparams(("parallel", "arbitrary")),
    )(pm, pm, ck)


def _conv_bwd(pm, duc, ck, T, S):
    nb = T // S

    def body(val_ref, gate_ref, duc_ref, ck_ref, dval_ref, dgate_ref, dck_ref, ubuf, dbuf, acc):
        b = pl.program_id(1)
        ubuf[0:32, :] = jnp.zeros((32, CT), F32)
        ubuf[32:, :] = val_ref[...] * _sigmoid(gate_ref[...])
        dbuf[0:S, :] = duc_ref[...]
        dbuf[S:, :] = jnp.zeros((32, CT), F32)
        acc[...] = jnp.zeros_like(acc)

        def blk(rb, carry):
            base = pl.multiple_of(rb * RB, RB)
            uwin = ubuf[pl.ds(base, WIN), :]
            dwin = dbuf[pl.ds(base, WIN), :]
            dblk = dwin[0:RB, :]
            du = jnp.zeros((RB, CT), F32)
            for j in range(CW):
                du = du + ck_ref[j:j + 1, :] * pltpu.roll(dwin, (WIN - (CW - 1 - j)) % WIN, 0)[0:RB, :]
                ush = pltpu.roll(uwin, (WIN - (2 + j)) % WIN, 0)[0:RB, :]
                acc[j] += jnp.sum((dblk * ush).reshape(RB // 8, 8, CT), axis=0)
            val = val_ref[pl.ds(base, RB), :]
            sg = _sigmoid(gate_ref[pl.ds(base, RB), :])
            dval_ref[pl.ds(base, RB), :] = (du * sg).astype(BF16)
            dgate_ref[pl.ds(base, RB), :] = (du * val * sg * (1.0 - sg)).astype(BF16)
            return carry

        lax.fori_loop(0, S // RB, blk, 0)

        @pl.when(b == 0)
        def _():
            dck_ref[...] = jnp.zeros_like(dck_ref)
        for j in range(CW):
            dck_ref[j:j + 1, :] += jnp.sum(acc[j], axis=0, keepdims=True)

    return pl.pallas_call(
        body, name="conv_bwd", grid=(D // CT, nb),
        in_specs=[pl.BlockSpec((S, CT), lambda ct, b: (b, ct)),
                  pl.BlockSpec((S, CT), lambda ct, b: (b, D // CT + ct)),
                  pl.BlockSpec((S, CT), lambda ct, b: (b, ct)),
                  pl.BlockSpec((32, CT), lambda ct, b: (0, ct))],
        out_specs=[pl.BlockSpec((S, CT), lambda ct, b: (b, ct)),
                   pl.BlockSpec((S, CT), lambda ct, b: (b, ct)),
                   pl.BlockSpec((32, CT), lambda ct, b: (0, ct))],
        out_shape=[jax.ShapeDtypeStruct((T, D), BF16), jax.ShapeDtypeStruct((T, D), BF16),
                   jax.ShapeDtypeStruct((32, D), F32)],
        scratch_shapes=[pltpu.VMEM((S + 32, CT), F32), pltpu.VMEM((S + 32, CT), F32), pltpu.VMEM((32, 8, CT), F32)],
        compiler_params=_cparams(("parallel", "arbitrary")),
    )(pm, pm, duc, ck)


def _scan_fwd(ins, cm, nb, S):
    nc = S // CH
    nch = nb * NH
    blk = pl.BlockSpec((nb, CH, D), lambda i: (0, i, 0))
    hblk = pl.BlockSpec((nb, NH, None, HN, HN), lambda i: (0, 0, i, 0, 0))

    def body(r_ref, lw_ref, k_ref, v_ref, kk_ref, b_ref, cm_ref, o_ref, hs_ref, st):
        @pl.when(pl.program_id(0) == 0)
        def _():
            st[...] = jnp.zeros_like(st)
        s0 = [st[j] for j in range(nch)]
        for j in range(nch):
            hs_ref[j // NH, j % NH] = s0[j]
        vals = [jnp.concatenate([ref[q] for q in range(nb)], axis=1) for ref in (r_ref, lw_ref, k_ref, v_ref, kk_ref, b_ref)]
        o, s1 = _chunk(s0, *vals, cm_ref[...])
        for q in range(nb):
            o_ref[q] = o[:, q * D:(q + 1) * D]
        for j in range(nch):
            st[j] = s1[j]

    o, hs = pl.pallas_call(
        body, name="scan_fwd", grid=(nc,),
        in_specs=[blk] * 6 + [pl.BlockSpec(cm.shape, lambda i: (0, 0, 0))],
        out_specs=[blk, hblk],
        out_shape=[jax.ShapeDtypeStruct((nb, S, D), F32), jax.ShapeDtypeStruct((nb, NH, nc, HN, HN), F32)],
        scratch_shapes=[pltpu.VMEM((nch, HN, HN), F32)],
        compiler_params=_cparams(("arbitrary",)),
    )(*[a.reshape(nb, S, D) for a in ins], cm)
    return o.reshape(nb * S, D), hs


def _scan_bwd(ins, hs, do, cm, nb, S):
    nc = S // CH
    nch = nb * NH
    blk = pl.BlockSpec((nb, CH, D), lambda i: (0, nc - 1 - i, 0))
    hblk = pl.BlockSpec((nb, NH, None, HN, HN), lambda i: (0, 0, nc - 1 - i, 0, 0))

    def body(r_ref, lw_ref, k_ref, v_ref, kk_ref, b_ref, hs_ref, do_ref, cm_ref,
             dr_ref, dlw_ref, dk_ref, dv_ref, dkk_ref, db_ref, dst):
        @pl.when(pl.program_id(0) == 0)
        def _():
            dst[...] = jnp.zeros_like(dst)
        cmv = cm_ref[...]
        side = lambda ref: jnp.concatenate([ref[q] for q in range(nb)], axis=1)
        f = lambda s0, r, lw, k, v, kk, b: _chunk(s0, r, lw, k, v, kk, b, cmv)
        _, vjp = jax.vjp(f, [hs_ref[j // NH, j % NH] for j in range(nch)],
                         *[side(ref) for ref in (r_ref, lw_ref, k_ref, v_ref, kk_ref, b_ref)])
        ds0, *grads = vjp((side(do_ref), tuple(dst[j] for j in range(nch))))
        for j in range(nch):
            dst[j] = ds0[j]
        for ref, g in zip((dr_ref, dlw_ref, dk_ref, dv_ref, dkk_ref, db_ref), grads):
            for q in range(nb):
                ref[q] = g[:, q * D:(q + 1) * D]

    outs = pl.pallas_call(
        body, name="scan_bwd", grid=(nc,),
        in_specs=[blk] * 6 + [hblk, blk, pl.BlockSpec(cm.shape, lambda i: (0, 0, 0))],
        out_specs=[blk] * 6,
        out_shape=[jax.ShapeDtypeStruct((nb, S, D), F32)] * 6,
        scratch_shapes=[pltpu.VMEM((nch, HN, HN), F32)],
        compiler_params=_cparams(("arbitrary",)),
    )(*[a.reshape(nb, S, D) for a in ins], hs, do.reshape(nb, S, D), cm)
    return [a.reshape(nb * S, D) for a in outs]


def _ew(fn, name, ins, n_out, tm, out_dtype=F32):
    R, W = ins[0].shape[-2:]
    tm = min(tm, R)
    if R % tm:
        tm = R // 2
    assert R % tm == 0 and (tm % 16 == 0 or tm == R), (name, R, tm)

    def body(*refs):
        vals = fn(*[r[...] for r in refs[:len(ins)]])
        for r, v in zip(refs[len(ins):], vals):
            r[...] = v.astype(r.dtype)

    def spec(a):
        if a.ndim == 3:
            return pl.BlockSpec((a.shape[0], tm, W), lambda i: (0, i, 0))
        return pl.BlockSpec((tm, W), lambda i: (i, 0))

    return pl.pallas_call(
        body, name=name, grid=(R // tm,), in_specs=[spec(a) for a in ins],
        out_specs=[pl.BlockSpec((tm, W), lambda i: (i, 0))] * n_out,
        out_shape=[jax.ShapeDtypeStruct((R, W), out_dtype)] * n_out,
        compiler_params=_cparams(("parallel",)),
    )(*ins)


def _sum_slots(r):
    s = r[0]
    for j in range(1, r.shape[0]):
        s = s + r[j]
    return s


def _place():
    x, y, c = lax.axis_index("x"), lax.axis_index("y"), lax.axis_index("c")
    return x, y, c


def _flip(v, d):
    return 1 - v if d else v


CHIP_PEERS = ((1, 0), (0, 1), (1, 1))
DEV_PEERS = tuple((dx, dy, dc) for dx in (0, 1) for dy in (0, 1) for dc in (0, 1))[1:]


def _comm_call(name, ins, out_shapes, plan, n_rem, n_fwd=0):
    n_in = len(ins)

    def body(*refs):
        in_refs, out_refs = refs[:n_in], refs[n_in:n_in + len(out_shapes)]
        send_sems, recv_sems, loc_sems = refs[n_in + len(out_shapes):]
        loc, rem, *rest = plan(in_refs, out_refs, _place())
        fwd = rest[0] if rest else []
        assert len(rem) == n_rem and len(fwd) == n_fwd and len(loc) <= 2 * n_in, (name, len(loc), len(rem), len(fwd))

        def remote(i, s, d, peer):
            return pltpu.make_async_remote_copy(src_ref=s, dst_ref=d, send_sem=send_sems.at[i], recv_sem=recv_sems.at[i],
                                                device_id=peer, device_id_type=MESH)

        copies = [pltpu.make_async_copy(s, d, loc_sems.at[i]) for i, (s, d) in enumerate(loc)]
        rcopies = [remote(i, s, d, peer) for i, (s, d, peer) in enumerate(rem)]
        for cp in copies + rcopies:
            cp.start()
        landed = set()
        fcopies = []
        for i, (s, d, peer, k) in enumerate(fwd):
            if k not in landed:
                rcopies[k].wait_recv()
                landed.add(k)
            fcopies.append(remote(n_rem + i, s, d, peer))
            fcopies[-1].start()
        for k, cp in enumerate(rcopies):
            if k not in landed:
                cp.wait_recv()
        for cp in rcopies + fcopies:
            cp.wait_send()
        for cp in fcopies:
            cp.wait_recv()
        for cp in copies:
            cp.wait()

    return pl.pallas_call(
        body, name=name, in_specs=[ANY] * n_in, out_specs=[ANY] * len(out_shapes), out_shape=out_shapes,
        scratch_shapes=[pltpu.SemaphoreType.DMA((n_rem + n_fwd,)), pltpu.SemaphoreType.DMA((n_rem + n_fwd,)),
                        pltpu.SemaphoreType.DMA((2 * n_in,))],
        compiler_params=pltpu.CompilerParams(has_side_effects=True),
    )(*ins)


def _gather_plan(n_big, in_refs, out_refs, place):
    x, y, c = place
    chip, dev = 2 * x + y, 4 * x + 2 * y + c
    sib = (x, y, 1 - c)
    loc = [(in_refs[0], out_refs[0].at[dev])] + [(s, d.at[chip]) for s, d in zip(in_refs[1 + n_big:], out_refs[1 + n_big:])]
    rem = [(in_refs[0], out_refs[0].at[dev], (_flip(x, dx), _flip(y, dy), _flip(c, dc))) for dx, dy, dc in DEV_PEERS]
    fwd = []
    for s, d in zip(in_refs[1:1 + n_big], out_refs[1:1 + n_big]):
        for dx, dy in CHIP_PEERS:
            px, py = _flip(x, dx), _flip(y, dy)
            fwd.append((d.at[2 * px + py, c], d.at[2 * px + py, c], sib, len(rem)))
            rem.append((s.at[c], d.at[chip, c], (px, py, c)))
    for s, d in zip(in_refs[1 + n_big:], out_refs[1 + n_big:]):
        rem += [(s, d.at[chip], (_flip(x, dx), _flip(y, dy), c)) for dx, dy in CHIP_PEERS]
    return loc, rem, fwd


def _join_plan(in_refs, out_refs, place):
    x, y, c = place
    return [], [(s, d, (x, y, 1 - c)) for s, d in zip(in_refs, out_refs)]


def _scatter_plan(n_all, in_refs, out_refs, place):
    x, y, c = place
    chip, dev = 2 * x + y, 4 * x + 2 * y + c
    loc, rem = [], []
    for s, d in zip(in_refs[:n_all], out_refs[:n_all]):
        loc.append((s.at[dev], d.at[dev]))
        for dx, dy, dc in DEV_PEERS:
            px, py, pc = _flip(x, dx), _flip(y, dy), _flip(c, dc)
            rem.append((s.at[4 * px + 2 * py + pc], d.at[dev], (px, py, pc)))
    for s, d in zip(in_refs[n_all:], out_refs[n_all:]):
        for dx, dy in CHIP_PEERS:
            px, py = _flip(x, dx), _flip(y, dy)
            rem.append((s.at[2 * px + py], d.at[chip], (px, py, c)))
    return loc, rem


def _bshape(a, nb):
    return a.reshape(nb, 1, a.shape[-1])


def _with_prev(cur, before, tiles_per_seq):
    first = pl.program_id(0) % tiles_per_seq == 0
    row0 = jnp.where(first, 0.0, before[before.shape[0] - 1:, :])
    rid = lax.broadcasted_iota(jnp.int32, cur.shape, 0)
    return jnp.where(rid == 0, row0, pltpu.roll(cur, 1, 0))


def _with_next(cur, after, tiles_per_seq):
    last = pl.program_id(0) % tiles_per_seq == tiles_per_seq - 1
    n = cur.shape[0]
    row_n = jnp.where(last, 0.0, after[0:1, :])
    rid = lax.broadcasted_iota(jnp.int32, cur.shape, 0)
    return jnp.where(rid == n - 1, row_n, pltpu.roll(cur, n - 1, 0))


def _local_step(x2d, tgt, mod, wmain, wlora, late_w, ck, w2, a2, small, nb, S, grads_hook):
    T = nb * S
    shift, scale, gate = (_bshape(mod[:, i * D:(i + 1) * D], nb) for i in range(3))
    G = jnp.asarray(np.arange(128)[:, None] == np.arange(D)[None, :] // HN, dtype=BF16)
    cm = jnp.asarray(_chunk_consts())
    ckp = jnp.pad(ck, ((0, 1), (0, 0)))
    zpad = jnp.zeros((64, D), F32)
    w2p = jnp.concatenate([w2, zpad], axis=0)
    a2p = jnp.concatenate([zpad, a2], axis=0)
    mu = small["rwkv_mu"]
    mu_r, mu_k, mu_v, mu_l = mu[:, 0:D], mu[:, D:2 * D], mu[:, 2 * D:3 * D], mu[:, 3 * D:]
    g4 = [mu_r, mu_k, mu_v, mu_l, small["rwkv_w0"], w2p, small["rwkv_a0"], a2p, small["rwkv_k_k"], small["rwkv_k_a"], G]
    g5 = [small["rwkv_gn_g"], small["rwkv_gn_b"], small["rwkv_r_k"], G]
    g3 = [small["conv_b"], small["conv_ln_g"], small["conv_ln_b"]]

    (h,), _, _ = _rows(lambda r, b, g: ([_s1(r[0], g[0], b[0], b[1])], [], []), "pre_fwd", T, S, 256,
                       [(x2d, D, 0)], [scale, shift], [small["norm_g"]], [(D, BF16)], [], [])
    skip = (DMAIN, lambda g, t: pl.multiple_of(g * t + jnp.where(g * t >= 6 * D, LORA, 0), LORA))
    if len(late_w) == 3:
        pm = _matmul(h, wmain, "nt", "proj_main", min(T, 1024), 1024, D, b_rows=skip)
        wco, wro, wo = late_w
    else:
        pm, *landed = _matmul(h, wmain, "nt", "proj_main", min(T, 1024), 1024, D, ride=late_w[0], b_rows=skip)
        wco, wro, wo = late_w[1](landed)
    plo = _matmul(h, wlora, "nt", "proj_lora", 512, LORA, D)
    uc = _conv_fwd(pm, ckp, T, S)
    (uo,), _, _ = _rows(lambda r, b, g: ([_s3(r[0], r[1], *g)], [], []), "conv_post_fwd", T, S, 256,
                        [(uc, D, 0), (pm, D, 2)], [], g3, [(D, BF16)], [], [])
    yc = _matmul(uo, wco, "nn", "conv_out", 512, 1024, D)
    rows4 = [(pm, D, 3), (pm, D, 4), (pm, D, 5), (plo, LORA, 0),
             (pm, D, 3, "prev"), (pm, D, 4, "prev"), (pm, D, 5, "prev"), (plo, LORA, 0, "prev")]
    tps4 = S // 128

    def shifted4(r, tps=tps4):
        return list(r[:4]) + [_with_prev(r[i], r[4 + i], tps) for i in range(4)]

    sc_in, _, _ = _rows(lambda r, b, g: (list(_s4(*shifted4(r, S // 256), *g)), [], []), "rwkv_pre_fwd", T, S, 256,
                        rows4, [], g4, [(D, F32)] * 6, [], [])
    o, hs = _scan_fwd(sc_in, cm, nb, S)
    rows5 = [(o, D, 0), (sc_in[0], D, 0), (sc_in[2], D, 0), (sc_in[3], D, 0), (pm, D, 6)]
    (o2,), _, _ = _rows(lambda r, b, g: ([_s5(*r, *g)], [], []), "rwkv_post_fwd", T, S, 256,
                        rows5, [], g5, [(D, BF16)], [], [])
    yr = _matmul(o2, wro, "nn", "rwkv_out", 512, 1024, D)
    rows6 = [(yc, D, 0), (yr, D, 0), (pm, D, 7), (pm, D, 8)]
    (m,), _, _ = _rows(lambda r, b, g: ([_s6(*r)], [], []), "merge_fwd", T, S, 256, rows6, [], [], [(D, BF16)], [], [])
    out = _matmul(m, wo, "nn", "out_proj", 512, 1024, D)

    def head(r, b, g):
        loss, (dx, dout, dgate, dfg) = jax.value_and_grad(_s7, argnums=(0, 1, 3, 4))(r[0], r[1], r[2], b[0], g[0])
        return [dx, dout], [dgate], [dfg, jnp.full((1, 128), loss, F32)]

    (dx_res, dout), (dgate,), (d_final_g, loss_v) = _rows(
        head, "head", T, S, 256, [(x2d, D, 0), (out, D, 0), (tgt, D, 0)], [gate], [small["final_g"]],
        [(D, F32), (D, BF16)], [D], [(1, D), (1, 128)])

    d_wo = _matmul(m, dout, "tn", "d_w_out", 512, 1024, min(T, 2048))
    dm = _matmul(dout, wo, "nt", "d_merge", 512, 1024, D)

    def merge_bwd(r, b, g):
        _, vjp = jax.vjp(_s6, *r[:4])
        dyc, dyr, dgc, dgr = vjp(r[4])
        return [dyc, dyr, dgc, dgr], [], []

    (dyc, dyr, dgc, dgr), _, _ = _rows(merge_bwd, "merge_bwd", T, S, 256, rows6 + [(dm, D, 0)], [], [],
                                       [(D, BF16), (D, BF16), (D, BF16), (D, BF16)], [], [])
    d_wco = _matmul(uo, dyc, "tn", "d_w_conv_out", 512, 1024, min(T, 2048))
    d_wro = _matmul(o2, dyr, "tn", "d_w_rwkv_out", 512, 1024, min(T, 2048))
    duo = _matmul(dyc, wco, "nt", "d_conv_act", 512, 1024, D)
    do2 = _matmul(dyr, wro, "nt", "d_rwkv_act", 512, 1024, D)

    def conv_post_bwd(r, b, g):
        _, vjp = jax.vjp(_s3, r[0], r[1], *g)
        duc, dog, dcb, dlg, dlb = vjp(r[2])
        return [duc, dog], [], [dcb, dlg, dlb]

    (duc, dcog), _, (d_cb, d_lg, d_lb) = _rows(conv_post_bwd, "conv_post_bwd", T, S, 256,
                                               [(uc, D, 0), (pm, D, 2), (duo, D, 0)], [], g3,
                                               [(D, F32), (D, BF16)], [], [(1, D)] * 3)
    dval, dgt, d_ckp = _conv_bwd(pm, duc, ckp, T, S)

    def rwkv_post_bwd(r, b, g):
        _, vjp = jax.vjp(lambda *z: _s5(*z, g[3]), *r[:5], *g[:3])
        res = vjp(r[5])
        return list(res[:5]), [], list(res[5:8])

    (do, dr_b, dk_b, dv_b, drog), _, (d_gg, d_gb, d_rk) = _rows(
        rwkv_post_bwd, "rwkv_post_bwd", T, S, 256, rows5 + [(do2, D, 0)], [], g5,
        [(D, F32)] * 4 + [(D, BF16)], [], [(1, D)] * 3)
    dsc = _scan_bwd(sc_in, hs, do, cm, nb, S)

    def rwkv_pre_bwd(r, b, g):
        _, vjp = jax.vjp(lambda *z: _s4(*z, g[10]), *shifted4(r), *g[:10])
        ct = (r[8] + r[14], r[9], r[10] + r[15], r[11] + r[16], r[12], r[13])
        res = vjp(ct)
        return list(res[:8]), [], list(res[8:18])

    rows4b = rows4 + [(a, D, 0) for a in dsc] + [(dr_b, D, 0), (dk_b, D, 0), (dv_b, D, 0)]
    gshapes = [(1, D), (1, D), (1, D), (1, LORA), (1, D), (LORA, D), (1, D), (LORA, D), (1, D), (1, D)]
    dts, _, gts = _rows(rwkv_pre_bwd, "rwkv_pre_bwd", T, S, 128, rows4b, [], g4,
                        [(D, BF16)] * 3 + [(LORA, BF16)] + [(D, BF16)] * 3 + [(LORA, BF16)], [], gshapes)
    dr0, dk0, dv0, dl0, dpr, dpk, dpv, dpl = dts
    d_mu_r, d_mu_k, d_mu_v, d_mu_l, d_w0, d_w2p, d_a0, d_a2p, d_kk, d_ka = gts

    def assemble(r, b, g):
        r = [z.astype(F32) for z in r]
        sh = [_with_next(r[10 + i], r[14 + i], tps4) for i in range(4)]
        main = jnp.concatenate([r[0], r[1], r[2], r[3] + sh[0], r[4] + sh[1], r[5] + sh[2], r[6], r[7], r[8]], axis=1)
        return [main, r[9] + sh[3]], [], []

    rows_a = [(dval, D, 0), (dgt, D, 0), (dcog, D, 0), (dr0, D, 0), (dk0, D, 0), (dv0, D, 0), (drog, D, 0), (dgc, D, 0),
              (dgr, D, 0), (dl0, LORA, 0), (dpr, D, 0), (dpk, D, 0), (dpv, D, 0), (dpl, LORA, 0),
              (dpr, D, 0, "next"), (dpk, D, 0, "next"), (dpv, D, 0, "next"), (dpl, LORA, 0, "next")]
    (dpm, dplo), _, _ = _rows(assemble, "assemble_dp", T, S, 128, rows_a, [], [], [(DMAIN, BF16), (LORA, BF16)], [], [])
    RW = D // NCHIP
    c_i = lax.axis_index("c")
    halved = [g.reshape(NCHIP, 2, RW // 2, D).transpose(1, 0, 2, 3).reshape(2, NCHIP * RW // 2, D) for g in (d_wco, d_wro, d_wo)]
    halved += [g.reshape(-1, NCHIP, 2, RW // 2).transpose(2, 1, 0, 3).reshape(2, -1, RW // 2)
               for g in (d_ckp[:CW], d_w2p[:64], d_a2p[64:])]
    h_keep, h_send = (lax.dynamic_slice_in_dim(h, k * (D // 2), D // 2, axis=1) for k in (c_i, 1 - c_i))
    tk = T
    send = [_matmul(h_send, dpm, "tn", "d_w_main_send", D // 2, 1024, tk, out_t=True)[0],
            _matmul(h_send, dplo, "tn", "d_w_lora_send", D // 2, LORA, tk, out_t=True)[0]]
    send += [lax.dynamic_index_in_dim(g, 1 - c_i, 0, keepdims=False) for g in halved]
    keep = [None, _matmul(h_keep, dplo, "tn", "d_w_lora_keep", D // 2, LORA, tk, out_t=True)[0]]
    keep += [lax.dynamic_index_in_dim(g, c_i, 0, keepdims=False) for g in halved]

    def to_sibling(in_refs, out_refs, place):
        x, y, c = place
        return [], [(s, d, (x, y, 1 - c)) for s, d in zip(in_refs, out_refs)]

    d_w_keep, *got_h = _matmul(h_keep, dpm, "tn", "d_w_main_keep", D // 2, 1024, tk, out_t=True,
                               ride=(send, [jax.ShapeDtypeStruct(t.shape, F32) for t in send], to_sibling, len(send)))
    keep[0] = d_w_keep[0]
    chip_part = [_ew(lambda p, q: [p + q], "chip_sum_%d" % i, [keep[i], got_h[i]], 1, 1024, BF16)[0]
                 for i in range(2, len(keep))]

    def sum_body(p_ref, q_ref, *rest):
        rest[-1][...] = (p_ref[...] + q_ref[...]).astype(BF16)

    blk = pl.BlockSpec((1024, D // 2), lambda i: (i, 0))
    d_win_h = pl.pallas_call(
        sum_body, name="chip_sum_w_in", grid=(DMAIN // 1024,), in_specs=[blk, blk],
        out_specs=pl.BlockSpec((pl.Element(1024), pl.Element(D // 2)), lambda i: (skip[1](i, 1024), 0)),
        out_shape=jax.ShapeDtypeStruct((DMAIN + LORA, D // 2), BF16), compiler_params=_cparams(("parallel",)),
    )(keep[0], got_h[0])
    lora_blk = pl.BlockSpec((LORA, D // 2), lambda i: (0, 0))
    d_win_h = pl.pallas_call(
        sum_body, name="chip_sum_w_lora", grid=(1,), in_specs=[lora_blk, lora_blk, ANY],
        out_specs=pl.BlockSpec((LORA, D // 2), lambda i: (6 * D // LORA, 0)),
        out_shape=jax.ShapeDtypeStruct((DMAIN + LORA, D // 2), BF16), input_output_aliases={2: 0},
        compiler_params=_cparams(("arbitrary",)),
    )(keep[1], got_h[1], d_win_h)
    chip_part = [d_win_h] + chip_part
    dh_m, *got_big = _matmul(dpm, wmain, "nn", "d_h_main", 512, 1024, 3072, ride=grads_hook(chip_part), b_rows=skip)
    dh_l = _matmul(dplo, wlora, "nn", "d_h_lora", 512, 1024, LORA)

    def pre_bwd(r, b, g):
        _, vjp = jax.vjp(_s1, r[0], g[0], b[0], b[1])
        dx, dg, dscale, dshift = vjp(r[1] + r[2])
        return [dx + r[3]], [dscale, dshift], [dg]

    (gx,), (dscale, dshift), (d_ng,) = _rows(pre_bwd, "pre_bwd", T, S, 256,
                                             [(x2d, D, 0), (dh_m, D, 0), (dh_l, D, 0), (dx_res, D, 0)],
                                             [scale, shift], [small["norm_g"]], [(D, F32)], [D, D], [(1, D)])
    dmod = jnp.concatenate([dshift, dscale, dgate], axis=-1).reshape(nb, 3 * D)
    d_small = {"norm_g": d_ng, "conv_b": d_cb, "conv_ln_g": d_lg, "conv_ln_b": d_lb,
               "rwkv_mu": jnp.concatenate([d_mu_r, d_mu_k, d_mu_v, d_mu_l], axis=1),
               "rwkv_w0": d_w0, "rwkv_a0": d_a0, "rwkv_k_k": d_kk, "rwkv_k_a": d_ka, "rwkv_r_k": d_rk,
               "rwkv_gn_g": d_gg, "rwkv_gn_b": d_gb, "final_g": d_final_g}
    return loss_v[0, 0], gx, dmod, got_big, d_small


def _step(a):
    nb, S, _ = a["x"].shape
    T = nb * S
    x_i, y_i, c_i = _place()
    chip = 2 * x_i + y_i
    w_in_t, m_w_in_t, v_w_in_t = (jnp.transpose(a[p + "w_in"][0]) for p in ("", "m_", "v_"))
    WS = w_in_t.shape[0]
    small_w = {n: a[n].reshape(1, sz) for n, sz in SMALL}

    def halves(t):
        return t.reshape(2, t.shape[0] // 2, t.shape[1])

    g_ins = [a["c"], halves(w_in_t.astype(BF16)), a["conv_k"][0], a["rwkv_w2"][0], a["rwkv_a2"][0]]
    g_out = [jax.ShapeDtypeStruct((NDEV,) + g_ins[0].shape, F32)]
    g_out += [jax.ShapeDtypeStruct((NCHIP,) + t.shape, t.dtype) for t in g_ins[1:]]
    c_all, win_g, ck_g, w2_g, a2_g = _comm_call(
        "gather_weights", g_ins, g_out, functools.partial(_gather_plan, 1), 7 + 3 * 4, 3)
    c_all = c_all.reshape(NDEV * nb, D)
    win_t = lax.dynamic_update_index_in_dim(win_g, g_ins[1], chip, 0).reshape(NCHIP * WS, D)
    late = [a[n][0].astype(BF16) for n in ("w_conv_out", "w_rwkv_out", "w_out")]

    def late_plan(in_refs, out_refs, place):
        x, y, c = place
        return [], [(s, d.at[2 * x + y], (_flip(x, dx), _flip(y, dy), c))
                    for s, d in zip(in_refs, out_refs) for dx, dy in CHIP_PEERS]

    def late_finish(landed):
        return [lax.dynamic_update_index_in_dim(g, own, chip, 0).reshape(D, D) for g, own in zip(landed, late)]

    late_w = ((late, [jax.ShapeDtypeStruct((NCHIP,) + t.shape, BF16) for t in late], late_plan, 9), late_finish)
    wmain = win_t
    wlora = win_t[6 * D:6 * D + LORA]
    ck = jnp.concatenate([ck_g[j] for j in range(NCHIP)], axis=1)
    w2 = jnp.concatenate([w2_g[j] for j in range(NCHIP)], axis=1)
    a2 = jnp.concatenate([a2_g[j] for j in range(NCHIP)], axis=1)

    ada_w = a["ada_w"][0]
    MW = ada_w.shape[1]
    ada_b_loc = lax.dynamic_slice(a["ada_b"], (0, chip * MW), (1, MW))

    def mod_body(c_ref, w_ref, b_ref, o_ref):
        o_ref[...] = _dot(_silu(c_ref[...]), w_ref[...], HI) + b_ref[...]

    modp = pl.pallas_call(mod_body, name="ada_mod", out_shape=jax.ShapeDtypeStruct((NDEV * nb, MW), F32),
                          compiler_params=_cparams())(c_all, ada_w, ada_b_loc)
    (mod_g,) = _comm_call("scatter_mod", [modp.reshape(NDEV, nb, MW)],
                          [jax.ShapeDtypeStruct((NDEV, nb, MW), F32)],
                          functools.partial(_scatter_plan, 1), 7)
    mod = mod_g.reshape(NCHIP, 2, nb, MW)
    mod = mod[:, 0].transpose(1, 0, 2).reshape(nb, NCHIP * MW)

    RW = D // NCHIP
    sh_s = []

    def grads_hook(chip_part):
        sh_s.append(chip_part[0].reshape(NCHIP, WS, D // 2))
        sh_s.extend(t.reshape(NCHIP, RW // 2, D) for t in chip_part[1:4])
        sh_s.extend(t.reshape(NCHIP, -1, RW // 2) for t in chip_part[4:])
        return (sh_s, [jax.ShapeDtypeStruct(t.shape, t.dtype) for t in sh_s], functools.partial(_scatter_plan, 0),
                3 * len(sh_s))

    loss_p, gx, dmod, got_big, d_small = _local_step(
        a["x"].reshape(T, D), a["loss_target"].reshape(T, D), mod, wmain, wlora, late_w, ck, w2, a2, small_w, nb, S,
        grads_hook)
    loss = lax.psum(loss_p, ("x", "y", "c"))

    d_small["ada_b"] = _colsum(dmod)
    small_vec = jnp.concatenate([d_small[n] for n, _ in SMALL], axis=1)
    dmod_s = dmod.reshape(nb, NCHIP, MW).transpose(1, 0, 2)
    dmod_s = jnp.repeat(dmod_s, 2, axis=0)
    small_s = jnp.broadcast_to(small_vec[None], (NDEV, 1, NSMALL))
    got = _comm_call("scatter_small", [dmod_s, small_s], [jax.ShapeDtypeStruct(t.shape, F32) for t in (dmod_s, small_s)],
                     functools.partial(_scatter_plan, 2), 14)
    dmod_all, small_all = got[0].reshape(NDEV * nb, MW), got[1].reshape(NDEV, NSMALL)

    def shard_sum(recv, sent):
        chip_i = 2 * lax.axis_index("x") + lax.axis_index("y")
        s = None
        for j in range(NCHIP):
            t = jnp.where(chip_i == j, sent[j], recv[j]).astype(F32)
            s = t if s is None else s + t
        return [s]

    fin = [_ew(shard_sum, "shard_sum_%d" % i, [t, sh_s[i]], 1, 128)[0] for i, t in enumerate(got_big)]
    oth = _comm_call("join_halves", fin, [jax.ShapeDtypeStruct(t.shape, F32) for t in fin], _join_plan, len(fin))

    outs = {}

    def upd_halves(name, mine, other):
        shp = a[name].shape
        R, W = 2 * mine.shape[0], mine.shape[1]
        tm = 128
        nh = R // 2 // tm

        def body(w_ref, m_ref, v_ref, f_ref, o_ref, g_ref, d_ref, m2_ref, v2_ref):
            g = jnp.where(pl.program_id(0) // nh == lax.axis_index("c"), f_ref[...], o_ref[...])
            g_ref[...] = g
            d_ref[...], m2_ref[...], v2_ref[...] = _adamw(w_ref[...], g, m_ref[...], v_ref[...])

        full = pl.BlockSpec((None, tm, W), lambda i: (0, i, 0))
        half = pl.BlockSpec((tm, W), lambda i: (i % nh, 0))
        assert shp == (1, R, W)
        outs[name] = pl.pallas_call(
            body, name="adamw_" + name, grid=(R // tm,), in_specs=[full] * 3 + [half] * 2, out_specs=[full] * 4,
            out_shape=[jax.ShapeDtypeStruct(shp, F32)] * 4, compiler_params=_cparams(("parallel",)),
        )(*[a[p + name] for p in ("", "m_", "v_")], mine, other)

    for name, f, o in zip(("w_conv_out", "w_rwkv_out", "w_out"), fin[1:4], oth[1:4]):
        upd_halves(name, f, o)

    tw = WS // 4

    def w_in_body(w_ref, m_ref, v_ref, f_ref, o_ref, g_ref, d_ref, m2_ref, v2_ref):
        first = lax.axis_index("c") == 0
        g = jnp.concatenate([jnp.where(first, f_ref[...], o_ref[...]), jnp.where(first, o_ref[...], f_ref[...])], axis=1)
        g_ref[...] = g
        d_ref[...], m2_ref[...], v2_ref[...] = _adamw(w_ref[...], g, m_ref[...], v_ref[...])

    full = pl.BlockSpec((tw, D), lambda i: (i, 0))
    half = pl.BlockSpec((tw, D // 2), lambda i: (i, 0))
    res = pl.pallas_call(
        w_in_body, name="adamw_w_in", grid=(WS // tw,), in_specs=[full] * 3 + [half] * 2, out_specs=[full] * 4,
        out_shape=[jax.ShapeDtypeStruct((WS, D), F32)] * 4, compiler_params=_cparams(("parallel",)),
    )(w_in_t, m_w_in_t, v_w_in_t, fin[0], oth[0])
    outs["w_in"] = [jnp.transpose(r)[None] for r in res]

    def upd(name, g):
        shp = a[name].shape
        ins = [a[p + name].reshape(g.shape) for p in ("", "m_", "v_")]
        res = _ew(lambda w_, m_, v_, g_: [g_, *_adamw(w_, g_, m_, v_)], "adamw_" + name, [*ins, g], 4, 128)
        outs[name] = [r.reshape(shp) for r in res]

    for name, f, o in zip(("conv_k", "rwkv_w2", "rwkv_a2"), fin[4:], oth[4:]):
        both = jnp.where(c_i == 0, jnp.stack([f, o]), jnp.stack([o, f]))
        upd(name, both.transpose(1, 0, 2).reshape(-1, RW))

    def adaw_body(c_ref, dm_ref, w_ref, m_ref, v_ref, g_ref, d_ref, m2_ref, v2_ref):
        g = _dot_tn(_silu(c_ref[...]), dm_ref[...], HI)
        g_ref[...] = g
        d_ref[...], m2_ref[...], v2_ref[...] = _adamw(w_ref[...], g, m_ref[...], v_ref[...])

    res = pl.pallas_call(adaw_body, name="adamw_ada_w", out_shape=[jax.ShapeDtypeStruct((D, MW), F32)] * 4,
                         compiler_params=_cparams())(c_all, dmod_all, ada_w, a["m_ada_w"][0], a["v_ada_w"][0])
    outs["ada_w"] = [r.reshape(a["ada_w"].shape) for r in res]

    wv, mv, vv = (jnp.concatenate([a[p + n].reshape(1, sz) for n, sz in SMALL], axis=1) for p in ("", "m_", "v_"))
    def small_fn(w_, m_, v_, gs):
        g = _sum_slots(gs)
        return [g, *_adamw(w_, g, m_, v_)]

    res = _ew(small_fn, "adamw_small", [wv, mv, vv, small_all.reshape(NDEV, 1, NSMALL)], 4, 8)
    off = 0
    for n, sz in SMALL:
        outs[n] = [r[:, off:off + sz].reshape(a[n].shape) for r in res]
        off += sz

    return (loss, gx.reshape(nb, S, D), *[outs[n][0] for n in WEIGHTS], *[outs[n][1] for n in WEIGHTS],
            *[outs[n][2] for n in WEIGHTS], *[outs[n][3] for n in WEIGHTS])


def _colsum(dmod):
    def body(d_ref, o_ref):
        o_ref[...] = jnp.sum(d_ref[...], axis=0, keepdims=True)
    return pl.pallas_call(body, name="ada_b_rowsum", out_shape=jax.ShapeDtypeStruct((1, dmod.shape[1]), F32),
                          compiler_params=_cparams())(dmod)


def kernel(x, c, ada_w, ada_b, norm_g, w_in, conv_k, conv_b, conv_ln_g, conv_ln_b, w_conv_out, rwkv_mu, rwkv_w0, rwkv_w2, rwkv_a0, rwkv_a2, rwkv_k_k, rwkv_k_a, rwkv_r_k, rwkv_gn_g, rwkv_gn_b, w_rwkv_out, w_out, final_g, loss_target, m_ada_w, m_ada_b, m_norm_g, m_w_in, m_conv_k, m_conv_b, m_conv_ln_g, m_conv_ln_b, m_w_conv_out, m_rwkv_mu, m_rwkv_w0, m_rwkv_w2, m_rwkv_a0, m_rwkv_a2, m_rwkv_k_k, m_rwkv_k_a, m_rwkv_r_k, m_rwkv_gn_g, m_rwkv_gn_b, m_w_rwkv_out, m_w_out, m_final_g, v_ada_w, v_ada_b, v_norm_g, v_w_in, v_conv_k, v_conv_b, v_conv_ln_g, v_conv_ln_b, v_w_conv_out, v_rwkv_mu, v_rwkv_w0, v_rwkv_w2, v_rwkv_a0, v_rwkv_a2, v_rwkv_k_k, v_rwkv_k_a, v_rwkv_r_k, v_rwkv_gn_g, v_rwkv_gn_b, v_w_rwkv_out, v_w_out, v_final_g):
    return _step(dict(locals()))
```

```python
import functools

import numpy as np
import jax
import jax.numpy as jnp
from jax import lax
from jax.experimental import pallas as pl
from jax.experimental.pallas import tpu as pltpu

F32 = jnp.float32
BF16 = jnp.bfloat16
HI = lax.Precision.HIGHEST
MESH = pl.DeviceIdType.MESH
ANY = pl.BlockSpec(memory_space=pl.ANY)

D = 1024
NH = 16
HN = 64
LORA = 128
DMAIN = 9 * D
CH = 64
CW = 31
NCHIP = 4
NDEV = 8
VMEM_LIMIT = 56 * 1024 * 1024

RMS_EPS = 1e-6
LN_EPS = 1e-5
GN_EPS = 64e-5
L2_EPS = 1e-12
ADAM_LR = 0.001
ADAM_B1 = 0.9
ADAM_B2 = 0.999
ADAM_EPS = 1e-08
ADAM_WD = 0.01
ADAM_STEP = 10

SMALL = (("ada_b", 3072), ("norm_g", 1024), ("conv_b", 1024), ("conv_ln_g", 1024), ("conv_ln_b", 1024),
         ("rwkv_mu", 3200), ("rwkv_w0", 1024), ("rwkv_a0", 1024), ("rwkv_k_k", 1024), ("rwkv_k_a", 1024),
         ("rwkv_r_k", 1024), ("rwkv_gn_g", 1024), ("rwkv_gn_b", 1024), ("final_g", 1024))
NSMALL = sum(n for _, n in SMALL)

WEIGHTS = ['ada_w', 'ada_b', 'norm_g', 'w_in', 'conv_k', 'conv_b', 'conv_ln_g', 'conv_ln_b', 'w_conv_out', 'rwkv_mu',
           'rwkv_w0', 'rwkv_w2', 'rwkv_a0', 'rwkv_a2', 'rwkv_k_k', 'rwkv_k_a', 'rwkv_r_k', 'rwkv_gn_g', 'rwkv_gn_b',
           'w_rwkv_out', 'w_out', 'final_g']


def _cparams(sem=None, **kw):
    if sem is not None:
        kw["dimension_semantics"] = sem
    return pltpu.CompilerParams(vmem_limit_bytes=VMEM_LIMIT, **kw)


def _dot(a, b, prec=None):
    return jnp.dot(a, b, preferred_element_type=F32, precision=prec)


def _dot_nt(a, b, prec=None):
    return lax.dot_general(a, b, (((1,), (1,)), ((), ())), preferred_element_type=F32, precision=prec)


def _dot_tn(a, b, prec=None):
    return lax.dot_general(a, b, (((0,), (0,)), ((), ())), preferred_element_type=F32, precision=prec)


def _pdot(f, a, b, p):
    if p == "hi":
        return f(a, b, HI)
    ah, bh = a.astype(BF16), b.astype(BF16)
    if p == "bf":
        return f(ah, bh)
    al, bl = (a - ah.astype(F32)).astype(BF16), (b - bh.astype(F32)).astype(BF16)
    return f(ah, bh) + (f(ah, bl) + f(al, bh))


P_SCORE = "b3"
P_INV = "bf"
P_APPLY = "bf"


def _sigmoid(z):
    return 1.0 / (1.0 + jnp.exp(-z))


def _silu(z):
    return z * _sigmoid(z)


def _matmul(a, b, mode, name, tm, tn, tk, ride=None, out_t=False, b_rows=None):
    if mode == "nn":
        (M, K), N = a.shape, b.shape[1]
        a_spec = pl.BlockSpec((tm, tk), lambda j, i, k: (i, k))
        b_spec = pl.BlockSpec((tk, tn), lambda j, i, k: (k, j))
        if b_rows is not None:
            assert b_rows[0] == K
            b_spec = pl.BlockSpec((pl.Element(tk), pl.Element(tn)), lambda j, i, k: (b_rows[1](k, tk), j * tn))
        f = _dot
    elif mode == "nt":
        (M, K), N = a.shape, b.shape[0]
        a_spec = pl.BlockSpec((tm, tk), lambda j, i, k: (i, k))
        b_spec = pl.BlockSpec((tn, tk), lambda j, i, k: (j, k))
        if b_rows is not None:
            N = b_rows[0]
            b_spec = pl.BlockSpec((pl.Element(tn), pl.Element(tk)), lambda j, i, k: (b_rows[1](j, tn), k * tk))
        f = _dot_nt
    else:
        (K, M), N = a.shape, b.shape[1]
        a_spec = pl.BlockSpec((tk, tm), lambda j, i, k: (k, i))
        b_spec = pl.BlockSpec((tk, tn), lambda j, i, k: (k, j))
        f = _dot_tn
    assert M % tm == 0 and N % tn == 0 and K % tk == 0, (name, M, N, K)

    grid = (N // tn, M // tm, K // tk)
    o_spec = pl.BlockSpec((tm, tn), lambda j, i, k: (i, j))
    o_shape = jax.ShapeDtypeStruct((M, N), F32)

    scratch = []
    if out_t:
        o_spec = pl.BlockSpec((None, tn, tm), lambda j, i, k: (i, j, 0))
        o_shape = jax.ShapeDtypeStruct((M // tm, N, tm), F32)
        scratch = [pltpu.VMEM((tm, tn), F32)]

    def step(a_ref, b_ref, o_ref, *acc):
        acc_ref = acc[0] if out_t else o_ref

        @pl.when(pl.program_id(2) == 0)
        def _():
            acc_ref[...] = jnp.zeros_like(acc_ref)
        acc_ref[...] += f(a_ref[...], b_ref[...])
        if out_t:
            @pl.when(pl.program_id(2) == grid[2] - 1)
            def _():
                o_ref[...] = acc_ref[...].T

    if ride is None:
        return pl.pallas_call(
            step, name=name, grid=grid, in_specs=[a_spec, b_spec], out_specs=o_spec, out_shape=o_shape,
            scratch_shapes=scratch, compiler_params=_cparams(("parallel", "parallel", "arbitrary")),
        )(a, b)

    r_ins, r_shapes, plan, n_rem = ride
    n_ri, n_ro = len(r_ins), len(r_shapes)

    def body(a_ref, b_ref, *rest):
        r_in, o_ref, r_out = rest[:n_ri], rest[n_ri], rest[n_ri + 1:n_ri + 1 + n_ro]
        send_sems, recv_sems, *acc = rest[n_ri + 1 + n_ro:]
        loc, rem = plan(r_in, r_out, _place())
        assert not loc and len(rem) == n_rem, (name, len(loc), len(rem))
        copies = [pltpu.make_async_remote_copy(src_ref=s, dst_ref=d, send_sem=send_sems.at[i], recv_sem=recv_sems.at[i],
                                               device_id=peer, device_id_type=MESH) for i, (s, d, peer) in enumerate(rem)]
        pid = [pl.program_id(ax) for ax in range(3)]

        @pl.when((pid[0] == 0) & (pid[1] == 0) & (pid[2] == 0))
        def _():
            for cp in copies:
                cp.start()

        step(a_ref, b_ref, o_ref, *acc)

        @pl.when((pid[0] == grid[0] - 1) & (pid[1] == grid[1] - 1) & (pid[2] == grid[2] - 1))
        def _():
            for cp in copies:
                cp.wait_send()
            for cp in copies:
                cp.wait_recv()

    return pl.pallas_call(
        body, name=name, grid=grid, in_specs=[a_spec, b_spec] + [ANY] * n_ri, out_specs=[o_spec] + [ANY] * n_ro,
        out_shape=[o_shape] + list(r_shapes),
        scratch_shapes=[pltpu.SemaphoreType.DMA((n_rem,)), pltpu.SemaphoreType.DMA((n_rem,))] + scratch,
        compiler_params=_cparams(("arbitrary", "arbitrary", "arbitrary"), has_side_effects=True),
    )(a, b, *r_ins)


def _rows(fn, name, T, S, tm, rows, bpars, gpars, outs, baccs, gaccs):
    nb = T // S
    tps = S // tm
    n_r, n_b, n_g, n_o, n_ba, n_ga = len(rows), len(bpars), len(gpars), len(outs), len(baccs), len(gaccs)

    def body(*refs):
        r_refs = refs[:n_r]
        b_refs = refs[n_r:n_r + n_b]
        g_refs = refs[n_r + n_b:n_r + n_b + n_g]
        o_refs = refs[n_r + n_b + n_g:n_r + n_b + n_g + n_o]
        ba_refs = refs[n_r + n_b + n_g + n_o:n_r + n_b + n_g + n_o + n_ba]
        ga_refs = refs[n_r + n_b + n_g + n_o + n_ba:]
        i = pl.program_id(0)
        o_vals, ba_vals, ga_vals = fn([r[...] for r in r_refs], [r[...] for r in b_refs], [r[...] for r in g_refs])
        for r, v in zip(o_refs, o_vals):
            r[...] = v.astype(r.dtype)
        if n_ba:
            @pl.when(i % tps == 0)
            def _():
                for r in ba_refs:
                    r[...] = jnp.zeros_like(r)
            for r, v in zip(ba_refs, ba_vals):
                r[...] += v.reshape(r.shape)
        if n_ga:
            @pl.when(i == 0)
            def _():
                for r in ga_refs:
                    r[...] = jnp.zeros_like(r)
            for r, v in zip(ga_refs, ga_vals):
                r[...] += v.reshape(r.shape)

    def row_spec(arr, w, cb, kind="tile"):
        hr = 8 * (4 // arr.dtype.itemsize)
        if kind == "prev":
            return pl.BlockSpec((hr, w), lambda i: (jnp.maximum(i * (tm // hr) - 1, 0), cb))
        if kind == "next":
            return pl.BlockSpec((hr, w), lambda i: (jnp.minimum((i + 1) * (tm // hr), T // hr - 1), cb))
        return pl.BlockSpec((tm, w), lambda i: (i, cb))

    in_specs = [row_spec(*r) for r in rows]
    in_specs += [pl.BlockSpec((None, 1, p.shape[-1]), lambda i: (i // tps, 0, 0)) for p in bpars]
    in_specs += [pl.BlockSpec(p.shape, lambda i: (0, 0)) for p in gpars]
    out_specs = [pl.BlockSpec((tm, w), lambda i: (i, 0)) for w, _ in outs]
    out_specs += [pl.BlockSpec((None, 1, w), lambda i: (i // tps, 0, 0)) for w in baccs]
    out_specs += [pl.BlockSpec(s, lambda i: (0, 0)) for s in gaccs]
    out_shape = [jax.ShapeDtypeStruct((T, w), dt) for w, dt in outs]
    out_shape += [jax.ShapeDtypeStruct((nb, 1, w), F32) for w in baccs]
    out_shape += [jax.ShapeDtypeStruct(s, F32) for s in gaccs]
    res = pl.pallas_call(
        body, name=name, grid=(T // tm,), in_specs=in_specs, out_specs=out_specs, out_shape=out_shape,
        compiler_params=_cparams(("arbitrary",)),
    )(*[r[0] for r in rows], *bpars, *gpars)
    return res[:n_o], res[n_o:n_o + n_ba], res[n_o + n_ba:]


@jax.custom_vjp
def _gsum(z, G):
    zh = z.astype(BF16)
    zl = (z - zh.astype(F32)).astype(BF16)
    r = _dot_nt(zh, G) + _dot_nt(zl, G)
    rh = r.astype(BF16)
    rl = (r - rh.astype(F32)).astype(BF16)
    return _dot(rh, G) + _dot(rl, G)


def _dot3(x, w):
    xh = x.astype(BF16).astype(F32)
    wh = w.astype(BF16).astype(F32)
    xc = jnp.concatenate([xh, xh, x - xh], axis=1).astype(BF16)
    wc = jnp.concatenate([wh, w - wh, wh], axis=0).astype(BF16)
    return _dot(xc, wc)


_gsum.defvjp(lambda z, G: (_gsum(z, G), G), lambda G, ct: (_gsum(ct, G), jnp.zeros_like(G)))


def _s1(x, g, scale, shift):
    y = x * lax.rsqrt(jnp.mean(x * x, axis=-1, keepdims=True) + RMS_EPS)
    return (y * g) * (1.0 + scale) + shift


def _s3(uc, og, cb, lg, lb):
    u = uc + cb
    mu = jnp.mean(u, axis=-1, keepdims=True)
    d = u - mu
    var = jnp.mean(d * d, axis=-1, keepdims=True)
    y = d * lax.rsqrt(var + LN_EPS) * lg + lb
    return _silu(y) * _silu(og)


def _s4(r0, k0, v0, l0, pr, pk, pv, plo, mu_r, mu_k, mu_v, mu_l, w0, w2p, a0, a2p, k_k, k_a, G):
    r = r0 + mu_r * (pr - r0)
    k = k0 + mu_k * (pk - k0)
    v = v0 + mu_v * (pv - v0)
    lo = l0 + mu_l * (plo - l0)
    w_pre = w0 + _dot3(jnp.tanh(lo), w2p)
    lw = -np.float32(np.exp(-0.5)) * _sigmoid(w_pre)
    a = _sigmoid(a0 + _dot3(lo, a2p))
    kkr = k * k_k
    ss = _gsum(kkr * kkr, G)
    kk = kkr / jnp.maximum(jnp.sqrt(ss), L2_EPS)
    k2 = k * (1.0 + (a - 1.0) * k_a)
    return r, lw, k2, v, kk, kk * a


def _s5(o, r, k2, v, og, gg, gb, rk, G):
    mu = _gsum(o, G) * (1.0 / HN)
    d = o - mu
    var = _gsum(d * d, G) * (1.0 / HN)
    y = d * lax.rsqrt(var + GN_EPS) * gg + gb
    bonus = _gsum(r * k2 * rk, G)
    return (y + bonus * v) * _silu(og)


def _s6(yc, yr, gc, gr):
    return _sigmoid(gc) * yc + _sigmoid(gr) * yr


def _s7(x, out, tgt, gate, fg):
    x2 = x + gate * out
    y = x2 * lax.rsqrt(jnp.mean(x2 * x2, axis=-1, keepdims=True) + RMS_EPS) * fg
    e = y - tgt
    return 0.5 * jnp.sum(jnp.mean(e * e, axis=-1))


def _solve_all_fwd(a_kbs, rhss, cm):
    H = range(len(a_kbs))
    xi = [cm[2] - cm[3] * a_kbs[j] for j in H]
    for lvl in range(1, 6):
        t = [_pdot(_dot, xi[j], cm[3 + lvl] * a_kbs[j], P_INV) for j in H]
        xi = [xi[j] - _pdot(_dot, t[j], xi[j], P_INV) for j in H]
    u = tuple(_pdot(_dot, xi[j], rhss[j], P_APPLY) for j in H)
    return u, (xi, u, cm)


def _solve_all_bwd(res, dus):
    xi, u, cm = res
    H = range(len(u))
    g = tuple(_pdot(_dot_tn, xi[j], dus[j], P_APPLY) for j in H)
    da = tuple(-(cm[1] * _pdot(_dot_nt, g[j], u[j], P_APPLY)) for j in H)
    return da, g, jnp.zeros_like(cm)


@jax.custom_vjp
def _solve_all(a_kbs, rhss, cm):
    return _solve_all_fwd(a_kbs, rhss, cm)[0]


_solve_all.defvjp(_solve_all_fwd, _solve_all_bwd)


def _chunk(sts, r, lw, k, v, kk, b, cm):
    cum = _dot(cm[0], lw, HI)
    ein = jnp.exp(-cum)
    rt = r * jnp.exp(cum)
    kkt = kk * jnp.exp(cum - lw)
    kh = k * ein
    bh = b * ein
    ec = jnp.exp(jnp.sum(lw, axis=0, keepdims=True))
    khe = kh * ec
    bhe = bh * ec
    H = range(len(sts))
    tri, strict, eye = cm[0], cm[1], cm[2]
    rt, kkt, kh, bh, v, khe, bhe, ec = ([a[:, j * HN:(j + 1) * HN] for j in H] for a in (rt, kkt, kh, bh, v, khe, bhe, ec))
    lhs = [jnp.concatenate([kkt[j], rt[j]], axis=0) for j in H]
    rhs_s = [jnp.concatenate([bh[j], kh[j]], axis=0) for j in H]
    lh = [a.astype(BF16).astype(F32) for a in lhs]
    rh = [a.astype(BF16).astype(F32) for a in rhs_s]
    lc = [jnp.concatenate([lh[j], lh[j], lhs[j] - lh[j]], axis=1).astype(BF16) for j in H]
    rc = [jnp.concatenate([rh[j], rhs_s[j] - rh[j], rh[j]], axis=1).astype(BF16) for j in H]
    sc = [_dot_nt(lc[j], rc[j]) for j in H]
    a_kb = [strict * sc[j][:CH, :CH] for j in H]
    a_kk = [strict * sc[j][:CH, CH:] for j in H]
    a_rb = [tri * sc[j][CH:, :CH] for j in H]
    a_rk = [tri * sc[j][CH:, CH:] for j in H]
    ps = [_dot_nt(lhs[j].astype(BF16), sts[j].astype(BF16)) for j in H]
    pv = [_dot(jnp.concatenate([a_kk[j], a_rk[j]], axis=0).astype(BF16), v[j].astype(BF16)) for j in H]
    rhs = [ps[j][:CH] + pv[j][:CH] for j in H]
    o0 = [ps[j][CH:] + pv[j][CH:] for j in H]
    u = _solve_all(tuple(a_kb), tuple(rhs), cm)
    o = [o0[j] - _pdot(_dot, a_rb[j], u[j], P_APPLY) for j in H]
    st2 = [sts[j] * ec[j] + _dot_tn(jnp.concatenate([v[j], u[j]], axis=0).astype(BF16),
                                    jnp.concatenate([khe[j], -bhe[j]], axis=0).astype(BF16)) for j in H]
    return jnp.concatenate(o, axis=1), tuple(st2)


def _chunk_consts():
    t = np.arange(CH)[:, None]
    s = np.arange(CH)[None, :]
    mats = [(t >= s), (t > s), (t == s)]
    for lvl in range(6):
        sz = 1 << lvl
        mats.append(((t // sz) % 2 == 1) & ((s // sz) == (t // sz) - 1))
    mats.append(np.zeros((CH, CH), bool))
    return np.stack(mats).astype(np.float32)


def _adamw(w, g, m, v):
    m = ADAM_B1 * m + (1.0 - ADAM_B1) * g
    v = ADAM_B2 * v + (1.0 - ADAM_B2) * (g * g)
    m_hat = m / (1.0 - ADAM_B1 ** ADAM_STEP)
    v_hat = v / (1.0 - ADAM_B2 ** ADAM_STEP)
    delta = -ADAM_LR * (m_hat / (jnp.sqrt(v_hat) + ADAM_EPS) + ADAM_WD * w)
    return delta, m, v


CT = 128
RB = 64
WIN = RB + 32


def _conv_fwd(pm, ck, T, S):
    nb = T // S

    def body(val_ref, gate_ref, ck_ref, out_ref, ubuf):
        ubuf[0:32, :] = jnp.zeros((32, CT), F32)
        ubuf[32:, :] = val_ref[...] * _sigmoid(gate_ref[...])

        def blk(rb, carry):
            base = pl.multiple_of(rb * RB, RB)
            win = ubuf[pl.ds(base, WIN), :]
            acc = jnp.zeros((RB, CT), F32)
            for j in range(CW):
                acc = acc + ck_ref[j:j + 1, :] * pltpu.roll(win, (WIN - (2 + j)) % WIN, 0)[0:RB, :]
            out_ref[pl.ds(base, RB), :] = acc
            return carry

        lax.fori_loop(0, S // RB, blk, 0)

    return pl.pallas_call(
        body, name="conv_fwd", grid=(D // CT, nb),
        in_specs=[pl.BlockSpec((S, CT), lambda ct, b: (b, ct)),
                  pl.BlockSpec((S, CT), lambda ct, b: (b, D // CT + ct)),
                  pl.BlockSpec((32, CT), lambda ct, b: (0, ct))],
        out_specs=pl.BlockSpec((S, CT), lambda ct, b: (b, ct)),
        out_shape=jax.ShapeDtypeStruct((T, D), F32),
        scratch_shapes=[pltpu.VMEM((S + 32, CT), F32)],
        compiler_params=_cparams(("parallel", "arbitrary")),
    )(pm, pm, ck)


def _conv_bwd(pm, duc, ck, T, S):
    nb = T // S

    def body(val_ref, gate_ref, duc_ref, ck_ref, dval_ref, dgate_ref, dck_ref, ubuf, dbuf, acc):
        b = pl.program_id(1)
        ubuf[0:32, :] = jnp.zeros((32, CT), F32)
        ubuf[32:, :] = val_ref[...] * _sigmoid(gate_ref[...])
        dbuf[0:S, :] = duc_ref[...]
        dbuf[S:, :] = jnp.zeros((32, CT), F32)
        acc[...] = jnp.zeros_like(acc)

        def blk(rb, carry):
            base = pl.multiple_of(rb * RB, RB)
            uwin = ubuf[pl.ds(base, WIN), :]
            dwin = dbuf[pl.ds(base, WIN), :]
            dblk = dwin[0:RB, :]
            du = jnp.zeros((RB, CT), F32)
            for j in range(CW):
                du = du + ck_ref[j:j + 1, :] * pltpu.roll(dwin, (WIN - (CW - 1 - j)) % WIN, 0)[0:RB, :]
                ush = pltpu.roll(uwin, (WIN - (2 + j)) % WIN, 0)[0:RB, :]
                acc[j] += jnp.sum((dblk * ush).reshape(RB // 8, 8, CT), axis=0)
            val = val_ref[pl.ds(base, RB), :]
            sg = _sigmoid(gate_ref[pl.ds(base, RB), :])
            dval_ref[pl.ds(base, RB), :] = (du * sg).astype(BF16)
            dgate_ref[pl.ds(base, RB), :] = (du * val * sg * (1.0 - sg)).astype(BF16)
            return carry

        lax.fori_loop(0, S // RB, blk, 0)

        @pl.when(b == 0)
        def _():
            dck_ref[...] = jnp.zeros_like(dck_ref)
        for j in range(CW):
            dck_ref[j:j + 1, :] += jnp.sum(acc[j], axis=0, keepdims=True)

    return pl.pallas_call(
        body, name="conv_bwd", grid=(D // CT, nb),
        in_specs=[pl.BlockSpec((S, CT), lambda ct, b: (b, ct)),
                  pl.BlockSpec((S, CT), lambda ct, b: (b, D // CT + ct)),
                  pl.BlockSpec((S, CT), lambda ct, b: (b, ct)),
                  pl.BlockSpec((32, CT), lambda ct, b: (0, ct))],
        out_specs=[pl.BlockSpec((S, CT), lambda ct, b: (b, ct)),
                   pl.BlockSpec((S, CT), lambda ct, b: (b, ct)),
                   pl.BlockSpec((32, CT), lambda ct, b: (0, ct))],
        out_shape=[jax.ShapeDtypeStruct((T, D), BF16), jax.ShapeDtypeStruct((T, D), BF16),
                   jax.ShapeDtypeStruct((32, D), F32)],
        scratch_shapes=[pltpu.VMEM((S + 32, CT), F32), pltpu.VMEM((S + 32, CT), F32), pltpu.VMEM((32, 8, CT), F32)],
        compiler_params=_cparams(("parallel", "arbitrary")),
    )(pm, pm, duc, ck)


def _scan_fwd(ins, cm, nb, S):
    nc = S // CH
    nch = nb * NH
    blk = pl.BlockSpec((nb, CH, D), lambda i: (0, i, 0))
    hblk = pl.BlockSpec((nb, NH, None, HN, HN), lambda i: (0, 0, i, 0, 0))

    def body(r_ref, lw_ref, k_ref, v_ref, kk_ref, b_ref, cm_ref, o_ref, hs_ref, st):
        @pl.when(pl.program_id(0) == 0)
        def _():
            st[...] = jnp.zeros_like(st)
        s0 = [st[j] for j in range(nch)]
        for j in range(nch):
            hs_ref[j // NH, j % NH] = s0[j]
        vals = [jnp.concatenate([ref[q] for q in range(nb)], axis=1) for ref in (r_ref, lw_ref, k_ref, v_ref, kk_ref, b_ref)]
        o, s1 = _chunk(s0, *vals, cm_ref[...])
        for q in range(nb):
            o_ref[q] = o[:, q * D:(q + 1) * D]
        for j in range(nch):
            st[j] = s1[j]

    o, hs = pl.pallas_call(
        body, name="scan_fwd", grid=(nc,),
        in_specs=[blk] * 6 + [pl.BlockSpec(cm.shape, lambda i: (0, 0, 0))],
        out_specs=[blk, hblk],
        out_shape=[jax.ShapeDtypeStruct((nb, S, D), F32), jax.ShapeDtypeStruct((nb, NH, nc, HN, HN), F32)],
        scratch_shapes=[pltpu.VMEM((nch, HN, HN), F32)],
        compiler_params=_cparams(("arbitrary",)),
    )(*[a.reshape(nb, S, D) for a in ins], cm)
    return o.reshape(nb * S, D), hs


def _scan_bwd(ins, hs, do, cm, nb, S):
    nc = S // CH
    nch = nb * NH
    blk = pl.BlockSpec((nb, CH, D), lambda i: (0, nc - 1 - i, 0))
    hblk = pl.BlockSpec((nb, NH, None, HN, HN), lambda i: (0, 0, nc - 1 - i, 0, 0))

    def body(r_ref, lw_ref, k_ref, v_ref, kk_ref, b_ref, hs_ref, do_ref, cm_ref,
             dr_ref, dlw_ref, dk_ref, dv_ref, dkk_ref, db_ref, dst):
        @pl.when(pl.program_id(0) == 0)
        def _():
            dst[...] = jnp.zeros_like(dst)
        cmv = cm_ref[...]
        side = lambda ref: jnp.concatenate([ref[q] for q in range(nb)], axis=1)
        f = lambda s0, r, lw, k, v, kk, b: _chunk(s0, r, lw, k, v, kk, b, cmv)
        _, vjp = jax.vjp(f, [hs_ref[j // NH, j % NH] for j in range(nch)],
                         *[side(ref) for ref in (r_ref, lw_ref, k_ref, v_ref, kk_ref, b_ref)])
        ds0, *grads = vjp((side(do_ref), tuple(dst[j] for j in range(nch))))
        for j in range(nch):
            dst[j] = ds0[j]
        for ref, g in zip((dr_ref, dlw_ref, dk_ref, dv_ref, dkk_ref, db_ref), grads):
            for q in range(nb):
                ref[q] = g[:, q * D:(q + 1) * D]

    outs = pl.pallas_call(
        body, name="scan_bwd", grid=(nc,),
        in_specs=[blk] * 6 + [hblk, blk, pl.BlockSpec(cm.shape, lambda i: (0, 0, 0))],
        out_specs=[blk] * 6,
        out_shape=[jax.ShapeDtypeStruct((nb, S, D), F32)] * 6,
        scratch_shapes=[pltpu.VMEM((nch, HN, HN), F32)],
        compiler_params=_cparams(("arbitrary",)),
    )(*[a.reshape(nb, S, D) for a in ins], hs, do.reshape(nb, S, D), cm)
    return [a.reshape(nb * S, D) for a in outs]


def _ew(fn, name, ins, n_out, tm, out_dtype=F32):
    R, W = ins[0].shape[-2:]
    tm = min(tm, R)
    if R % tm:
        tm = R // 2
    assert R % tm == 0 and (tm % 16 == 0 or tm == R), (name, R, tm)

    def body(*refs):
        vals = fn(*[r[...] for r in refs[:len(ins)]])
        for r, v in zip(refs[len(ins):], vals):
            r[...] = v.astype(r.dtype)

    def spec(a):
        if a.ndim == 3:
            return pl.BlockSpec((a.shape[0], tm, W), lambda i: (0, i, 0))
        return pl.BlockSpec((tm, W), lambda i: (i, 0))

    return pl.pallas_call(
        body, name=name, grid=(R // tm,), in_specs=[spec(a) for a in ins],
        out_specs=[pl.BlockSpec((tm, W), lambda i: (i, 0))] * n_out,
        out_shape=[jax.ShapeDtypeStruct((R, W), out_dtype)] * n_out,
        compiler_params=_cparams(("parallel",)),
    )(*ins)


def _sum_slots(r):
    s = r[0]
    for j in range(1, r.shape[0]):
        s = s + r[j]
    return s


def _place():
    x, y, c = lax.axis_index("x"), lax.axis_index("y"), lax.axis_index("c")
    return x, y, c


def _flip(v, d):
    return 1 - v if d else v


CHIP_PEERS = ((1, 0), (0, 1), (1, 1))
DEV_PEERS = tuple((dx, dy, dc) for dx in (0, 1) for dy in (0, 1) for dc in (0, 1))[1:]


def _comm_call(name, ins, out_shapes, plan, n_rem, n_fwd=0):
    n_in = len(ins)

    def body(*refs):
        in_refs, out_refs = refs[:n_in], refs[n_in:n_in + len(out_shapes)]
        send_sems, recv_sems, loc_sems = refs[n_in + len(out_shapes):]
        loc, rem, *rest = plan(in_refs, out_refs, _place())
        fwd = rest[0] if rest else []
        assert len(rem) == n_rem and len(fwd) == n_fwd and len(loc) <= 2 * n_in, (name, len(loc), len(rem), len(fwd))

        def remote(i, s, d, peer):
            return pltpu.make_async_remote_copy(src_ref=s, dst_ref=d, send_sem=send_sems.at[i], recv_sem=recv_sems.at[i],
                                                device_id=peer, device_id_type=MESH)

        copies = [pltpu.make_async_copy(s, d, loc_sems.at[i]) for i, (s, d) in enumerate(loc)]
        rcopies = [remote(i, s, d, peer) for i, (s, d, peer) in enumerate(rem)]
        for cp in copies + rcopies:
            cp.start()
        landed = set()
        fcopies = []
        for i, (s, d, peer, k) in enumerate(fwd):
            if k not in landed:
                rcopies[k].wait_recv()
                landed.add(k)
            fcopies.append(remote(n_rem + i, s, d, peer))
            fcopies[-1].start()
        for k, cp in enumerate(rcopies):
            if k not in landed:
                cp.wait_recv()
        for cp in rcopies + fcopies:
            cp.wait_send()
        for cp in fcopies:
            cp.wait_recv()
        for cp in copies:
            cp.wait()

    return pl.pallas_call(
        body, name=name, in_specs=[ANY] * n_in, out_specs=[ANY] * len(out_shapes), out_shape=out_shapes,
        scratch_shapes=[pltpu.SemaphoreType.DMA((n_rem + n_fwd,)), pltpu.SemaphoreType.DMA((n_rem + n_fwd,)),
                        pltpu.SemaphoreType.DMA((2 * n_in,))],
        compiler_params=pltpu.CompilerParams(has_side_effects=True),
    )(*ins)


def _gather_plan(n_big, in_refs, out_refs, place):
    x, y, c = place
    chip, dev = 2 * x + y, 4 * x + 2 * y + c
    sib = (x, y, 1 - c)
    loc = [(in_refs[0], out_refs[0].at[dev])] + [(s, d.at[chip]) for s, d in zip(in_refs[1 + n_big:], out_refs[1 + n_big:])]
    rem = [(in_refs[0], out_refs[0].at[dev], (_flip(x, dx), _flip(y, dy), _flip(c, dc))) for dx, dy, dc in DEV_PEERS]
    fwd = []
    for s, d in zip(in_refs[1:1 + n_big], out_refs[1:1 + n_big]):
        for dx, dy in CHIP_PEERS:
            px, py = _flip(x, dx), _flip(y, dy)
            fwd.append((d.at[2 * px + py, c], d.at[2 * px + py, c], sib, len(rem)))
            rem.append((s.at[c], d.at[chip, c], (px, py, c)))
    for s, d in zip(in_refs[1 + n_big:], out_refs[1 + n_big:]):
        rem += [(s, d.at[chip], (_flip(x, dx), _flip(y, dy), c)) for dx, dy in CHIP_PEERS]
    return loc, rem, fwd


def _join_plan(in_refs, out_refs, place):
    x, y, c = place
    return [], [(s, d, (x, y, 1 - c)) for s, d in zip(in_refs, out_refs)]


def _scatter_plan(n_all, in_refs, out_refs, place):
    x, y, c = place
    chip, dev = 2 * x + y, 4 * x + 2 * y + c
    loc, rem = [], []
    for s, d in zip(in_refs[:n_all], out_refs[:n_all]):
        loc.append((s.at[dev], d.at[dev]))
        for dx, dy, dc in DEV_PEERS:
            px, py, pc = _flip(x, dx), _flip(y, dy), _flip(c, dc)
            rem.append((s.at[4 * px + 2 * py + pc], d.at[dev], (px, py, pc)))
    for s, d in zip(in_refs[n_all:], out_refs[n_all:]):
        for dx, dy in CHIP_PEERS:
            px, py = _flip(x, dx), _flip(y, dy)
            rem.append((s.at[2 * px + py], d.at[chip], (px, py, c)))
    return loc, rem


def _bshape(a, nb):
    return a.reshape(nb, 1, a.shape[-1])


def _with_prev(cur, before, tiles_per_seq):
    first = pl.program_id(0) % tiles_per_seq == 0
    row0 = jnp.where(first, 0.0, before[before.shape[0] - 1:, :])
    rid = lax.broadcasted_iota(jnp.int32, cur.shape, 0)
    return jnp.where(rid == 0, row0, pltpu.roll(cur, 1, 0))


def _with_next(cur, after, tiles_per_seq):
    last = pl.program_id(0) % tiles_per_seq == tiles_per_seq - 1
    n = cur.shape[0]
    row_n = jnp.where(last, 0.0, after[0:1, :])
    rid = lax.broadcasted_iota(jnp.int32, cur.shape, 0)
    return jnp.where(rid == n - 1, row_n, pltpu.roll(cur, n - 1, 0))


def _local_step(x2d, tgt, mod, wmain, wlora, late_w, ck, w2, a2, small, nb, S, grads_hook):
    T = nb * S
    shift, scale, gate = (_bshape(mod[:, i * D:(i + 1) * D], nb) for i in range(3))
    G = jnp.asarray(np.arange(128)[:, None] == np.arange(D)[None, :] // HN, dtype=BF16)
    cm = jnp.asarray(_chunk_consts())
    ckp = jnp.pad(ck, ((0, 1), (0, 0)))
    zpad = jnp.zeros((64, D), F32)
    w2p = jnp.concatenate([w2, zpad], axis=0)
    a2p = jnp.concatenate([zpad, a2], axis=0)
    mu = small["rwkv_mu"]
    mu_r, mu_k, mu_v, mu_l = mu[:, 0:D], mu[:, D:2 * D], mu[:, 2 * D:3 * D], mu[:, 3 * D:]
    g4 = [mu_r, mu_k, mu_v, mu_l, small["rwkv_w0"], w2p, small["rwkv_a0"], a2p, small["rwkv_k_k"], small["rwkv_k_a"], G]
    g5 = [small["rwkv_gn_g"], small["rwkv_gn_b"], small["rwkv_r_k"], G]
    g3 = [small["conv_b"], small["conv_ln_g"], small["conv_ln_b"]]

    (h,), _, _ = _rows(lambda r, b, g: ([_s1(r[0], g[0], b[0], b[1])], [], []), "pre_fwd", T, S, 256,
                       [(x2d, D, 0)], [scale, shift], [small["norm_g"]], [(D, BF16)], [], [])
    skip = (DMAIN, lambda g, t: pl.multiple_of(g * t + jnp.where(g * t >= 6 * D, LORA, 0), LORA))
    if len(late_w) == 3:
        pm = _matmul(h, wmain, "nt", "proj_main", min(T, 1024), 1024, D, b_rows=skip)
        wco, wro, wo = late_w
    else:
        pm, *landed = _matmul(h, wmain, "nt", "proj_main", min(T, 1024), 1024, D, ride=late_w[0], b_rows=skip)
        wco, wro, wo = late_w[1](landed)
    plo = _matmul(h, wlora, "nt", "proj_lora", 512, LORA, D)
    uc = _conv_fwd(pm, ckp, T, S)
    (uo,), _, _ = _rows(lambda r, b, g: ([_s3(r[0], r[1], *g)], [], []), "conv_post_fwd", T, S, 256,
                        [(uc, D, 0), (pm, D, 2)], [], g3, [(D, BF16)], [], [])
    yc = _matmul(uo, wco, "nn", "conv_out", 512, 1024, D)
    rows4 = [(pm, D, 3), (pm, D, 4), (pm, D, 5), (plo, LORA, 0),
             (pm, D, 3, "prev"), (pm, D, 4, "prev"), (pm, D, 5, "prev"), (plo, LORA, 0, "prev")]
    tps4 = S // 128

    def shifted4(r, tps=tps4):
        return list(r[:4]) + [_with_prev(r[i], r[4 + i], tps) for i in range(4)]

    sc_in, _, _ = _rows(lambda r, b, g: (list(_s4(*shifted4(r, S // 256), *g)), [], []), "rwkv_pre_fwd", T, S, 256,
                        rows4, [], g4, [(D, F32)] * 6, [], [])
    o, hs = _scan_fwd(sc_in, cm, nb, S)
    rows5 = [(o, D, 0), (sc_in[0], D, 0), (sc_in[2], D, 0), (sc_in[3], D, 0), (pm, D, 6)]
    (o2,), _, _ = _rows(lambda r, b, g: ([_s5(*r, *g)], [], []), "rwkv_post_fwd", T, S, 256,
                        rows5, [], g5, [(D, BF16)], [], [])
    yr = _matmul(o2, wro, "nn", "rwkv_out", 512, 1024, D)
    rows6 = [(yc, D, 0), (yr, D, 0), (pm, D, 7), (pm, D, 8)]
    (m,), _, _ = _rows(lambda r, b, g: ([_s6(*r)], [], []), "merge_fwd", T, S, 256, rows6, [], [], [(D, BF16)], [], [])
    out = _matmul(m, wo, "nn", "out_proj", 512, 1024, D)

    def head(r, b, g):
        loss, (dx, dout, dgate, dfg) = jax.value_and_grad(_s7, argnums=(0, 1, 3, 4))(r[0], r[1], r[2], b[0], g[0])
        return [dx, dout], [dgate], [dfg, jnp.full((1, 128), loss, F32)]

    (dx_res, dout), (dgate,), (d_final_g, loss_v) = _rows(
        head, "head", T, S, 256, [(x2d, D, 0), (out, D, 0), (tgt, D, 0)], [gate], [small["final_g"]],
        [(D, F32), (D, BF16)], [D], [(1, D), (1, 128)])

    d_wo = _matmul(m, dout, "tn", "d_w_out", 512, 1024, T)
    dm = _matmul(dout, wo, "nt", "d_merge", 512, 1024, D)

    def merge_bwd(r, b, g):
        _, vjp = jax.vjp(_s6, *r[:4])
        dyc, dyr, dgc, dgr = vjp(r[4])
        return [dyc, dyr, dgc, dgr], [], []

    (dyc, dyr, dgc, dgr), _, _ = _rows(merge_bwd, "merge_bwd", T, S, 256, rows6 + [(dm, D, 0)], [], [],
                                       [(D, BF16), (D, BF16), (D, BF16), (D, BF16)], [], [])
    d_wco = _matmul(uo, dyc, "tn", "d_w_conv_out", 512, 1024, T)
    d_wro = _matmul(o2, dyr, "tn", "d_w_rwkv_out", 512, 1024, T)
    duo = _matmul(dyc, wco, "nt", "d_conv_act", 512, 1024, D)
    do2 = _matmul(dyr, wro, "nt", "d_rwkv_act", 512, 1024, D)

    def conv_post_bwd(r, b, g):
        _, vjp = jax.vjp(_s3, r[0], r[1], *g)
        duc, dog, dcb, dlg, dlb = vjp(r[2])
        return [duc, dog], [], [dcb, dlg, dlb]

    (duc, dcog), _, (d_cb, d_lg, d_lb) = _rows(conv_post_bwd, "conv_post_bwd", T, S, 256,
                                               [(uc, D, 0), (pm, D, 2), (duo, D, 0)], [], g3,
                                               [(D, F32), (D, BF16)], [], [(1, D)] * 3)
    dval, dgt, d_ckp = _conv_bwd(pm, duc, ckp, T, S)

    def rwkv_post_bwd(r, b, g):
        _, vjp = jax.vjp(lambda *z: _s5(*z, g[3]), *r[:5], *g[:3])
        res = vjp(r[5])
        return list(res[:5]), [], list(res[5:8])

    (do, dr_b, dk_b, dv_b, drog), _, (d_gg, d_gb, d_rk) = _rows(
        rwkv_post_bwd, "rwkv_post_bwd", T, S, 256, rows5 + [(do2, D, 0)], [], g5,
        [(D, F32)] * 4 + [(D, BF16)], [], [(1, D)] * 3)
    dsc = _scan_bwd(sc_in, hs, do, cm, nb, S)

    def rwkv_pre_bwd(r, b, g):
        _, vjp = jax.vjp(lambda *z: _s4(*z, g[10]), *shifted4(r), *g[:10])
        ct = (r[8] + r[14], r[9], r[10] + r[15], r[11] + r[16], r[12], r[13])
        res = vjp(ct)
        return list(res[:8]), [], list(res[8:18])

    rows4b = rows4 + [(a, D, 0) for a in dsc] + [(dr_b, D, 0), (dk_b, D, 0), (dv_b, D, 0)]
    gshapes = [(1, D), (1, D), (1, D), (1, LORA), (1, D), (LORA, D), (1, D), (LORA, D), (1, D), (1, D)]
    dts, _, gts = _rows(rwkv_pre_bwd, "rwkv_pre_bwd", T, S, 128, rows4b, [], g4,
                        [(D, BF16)] * 3 + [(LORA, BF16)] + [(D, BF16)] * 3 + [(LORA, BF16)], [], gshapes)
    dr0, dk0, dv0, dl0, dpr, dpk, dpv, dpl = dts
    d_mu_r, d_mu_k, d_mu_v, d_mu_l, d_w0, d_w2p, d_a0, d_a2p, d_kk, d_ka = gts

    def assemble(r, b, g):
        r = [z.astype(F32) for z in r]
        sh = [_with_next(r[10 + i], r[14 + i], tps4) for i in range(4)]
        main = jnp.concatenate([r[0], r[1], r[2], r[3] + sh[0], r[4] + sh[1], r[5] + sh[2], r[6], r[7], r[8]], axis=1)
        return [main, r[9] + sh[3]], [], []

    rows_a = [(dval, D, 0), (dgt, D, 0), (dcog, D, 0), (dr0, D, 0), (dk0, D, 0), (dv0, D, 0), (drog, D, 0), (dgc, D, 0),
              (dgr, D, 0), (dl0, LORA, 0), (dpr, D, 0), (dpk, D, 0), (dpv, D, 0), (dpl, LORA, 0),
              (dpr, D, 0, "next"), (dpk, D, 0, "next"), (dpv, D, 0, "next"), (dpl, LORA, 0, "next")]
    (dpm, dplo), _, _ = _rows(assemble, "assemble_dp", T, S, 128, rows_a, [], [], [(DMAIN, BF16), (LORA, BF16)], [], [])
    RW = D // NCHIP
    c_i = lax.axis_index("c")
    halved = [g.reshape(NCHIP, 2, RW // 2, D).transpose(1, 0, 2, 3).reshape(2, NCHIP * RW // 2, D) for g in (d_wco, d_wro, d_wo)]
    halved += [g.reshape(-1, NCHIP, 2, RW // 2).transpose(2, 1, 0, 3).reshape(2, -1, RW // 2)
               for g in (d_ckp[:CW], d_w2p[:64], d_a2p[64:])]
    h_keep, h_send = (lax.dynamic_slice_in_dim(h, k * (D // 2), D // 2, axis=1) for k in (c_i, 1 - c_i))
    tk = T
    send = [_matmul(h_send, dpm, "tn", "d_w_main_send", D // 2, 1024, tk, out_t=True)[0],
            _matmul(h_send, dplo, "tn", "d_w_lora_send", D // 2, LORA, tk, out_t=True)[0]]
    send += [lax.dynamic_index_in_dim(g, 1 - c_i, 0, keepdims=False) for g in halved]
    keep = [None, _matmul(h_keep, dplo, "tn", "d_w_lora_keep", D // 2, LORA, tk, out_t=True)[0]]
    keep += [lax.dynamic_index_in_dim(g, c_i, 0, keepdims=False) for g in halved]

    def to_sibling(in_refs, out_refs, place):
        x, y, c = place
        return [], [(s, d, (x, y, 1 - c)) for s, d in zip(in_refs, out_refs)]

    d_w_keep, *got_h = _matmul(h_keep, dpm, "tn", "d_w_main_keep", D // 2, 1024, tk, out_t=True,
                               ride=(send, [jax.ShapeDtypeStruct(t.shape, F32) for t in send], to_sibling, len(send)))
    keep[0] = d_w_keep[0]
    chip_part = [_ew(lambda p, q: [p + q], "chip_sum_%d" % i, [keep[i], got_h[i]], 1, 1024, BF16)[0]
                 for i in range(2, len(keep))]

    def sum_body(p_ref, q_ref, *rest):
        rest[-1][...] = (p_ref[...] + q_ref[...]).astype(BF16)

    blk = pl.BlockSpec((1024, D // 2), lambda i: (i, 0))
    d_win_h = pl.pallas_call(
        sum_body, name="chip_sum_w_in", grid=(DMAIN // 1024,), in_specs=[blk, blk],
        out_specs=pl.BlockSpec((pl.Element(1024), pl.Element(D // 2)), lambda i: (skip[1](i, 1024), 0)),
        out_shape=jax.ShapeDtypeStruct((DMAIN + LORA, D // 2), BF16), compiler_params=_cparams(("parallel",)),
    )(keep[0], got_h[0])
    lora_blk = pl.BlockSpec((LORA, D // 2), lambda i: (0, 0))
    d_win_h = pl.pallas_call(
        sum_body, name="chip_sum_w_lora", grid=(1,), in_specs=[lora_blk, lora_blk, ANY],
        out_specs=pl.BlockSpec((LORA, D // 2), lambda i: (6 * D // LORA, 0)),
        out_shape=jax.ShapeDtypeStruct((DMAIN + LORA, D // 2), BF16), input_output_aliases={2: 0},
        compiler_params=_cparams(("arbitrary",)),
    )(keep[1], got_h[1], d_win_h)
    chip_part = [d_win_h] + chip_part
    dh_m, *got_big = _matmul(dpm, wmain, "nn", "d_h_main", 512, 1024, 3072, ride=grads_hook(chip_part), b_rows=skip)
    dh_l = _matmul(dplo, wlora, "nn", "d_h_lora", 512, 1024, LORA)

    def pre_bwd(r, b, g):
        _, vjp = jax.vjp(_s1, r[0], g[0], b[0], b[1])
        dx, dg, dscale, dshift = vjp(r[1] + r[2])
        return [dx + r[3]], [dscale, dshift], [dg]

    (gx,), (dscale, dshift), (d_ng,) = _rows(pre_bwd, "pre_bwd", T, S, 256,
                                             [(x2d, D, 0), (dh_m, D, 0), (dh_l, D, 0), (dx_res, D, 0)],
                                             [scale, shift], [small["norm_g"]], [(D, F32)], [D, D], [(1, D)])
    dmod = jnp.concatenate([dshift, dscale, dgate], axis=-1).reshape(nb, 3 * D)
    d_small = {"norm_g": d_ng, "conv_b": d_cb, "conv_ln_g": d_lg, "conv_ln_b": d_lb,
               "rwkv_mu": jnp.concatenate([d_mu_r, d_mu_k, d_mu_v, d_mu_l], axis=1),
               "rwkv_w0": d_w0, "rwkv_a0": d_a0, "rwkv_k_k": d_kk, "rwkv_k_a": d_ka, "rwkv_r_k": d_rk,
               "rwkv_gn_g": d_gg, "rwkv_gn_b": d_gb, "final_g": d_final_g}
    return loss_v[0, 0], gx, dmod, got_big, d_small


def _step(a):
    nb, S, _ = a["x"].shape
    T = nb * S
    x_i, y_i, c_i = _place()
    chip = 2 * x_i + y_i
    w_in_t, m_w_in_t, v_w_in_t = (jnp.transpose(a[p + "w_in"][0]) for p in ("", "m_", "v_"))
    WS = w_in_t.shape[0]
    small_w = {n: a[n].reshape(1, sz) for n, sz in SMALL}

    def halves(t):
        return t.reshape(2, t.shape[0] // 2, t.shape[1])

    g_ins = [a["c"], halves(w_in_t.astype(BF16)), a["conv_k"][0], a["rwkv_w2"][0], a["rwkv_a2"][0]]
    g_out = [jax.ShapeDtypeStruct((NDEV,) + g_ins[0].shape, F32)]
    g_out += [jax.ShapeDtypeStruct((NCHIP,) + t.shape, t.dtype) for t in g_ins[1:]]
    c_all, win_g, ck_g, w2_g, a2_g = _comm_call(
        "gather_weights", g_ins, g_out, functools.partial(_gather_plan, 1), 7 + 3 * 4, 3)
    c_all = c_all.reshape(NDEV * nb, D)
    win_t = lax.dynamic_update_index_in_dim(win_g, g_ins[1], chip, 0).reshape(NCHIP * WS, D)
    late = [a[n][0].astype(BF16) for n in ("w_conv_out", "w_rwkv_out", "w_out")]

    def late_plan(in_refs, out_refs, place):
        x, y, c = place
        return [], [(s, d.at[2 * x + y], (_flip(x, dx), _flip(y, dy), c))
                    for s, d in zip(in_refs, out_refs) for dx, dy in CHIP_PEERS]

    def late_finish(landed):
        return [lax.dynamic_update_index_in_dim(g, own, chip, 0).reshape(D, D) for g, own in zip(landed, late)]

    late_w = ((late, [jax.ShapeDtypeStruct((NCHIP,) + t.shape, BF16) for t in late], late_plan, 9), late_finish)
    wmain = win_t
    wlora = win_t[6 * D:6 * D + LORA]
    ck = jnp.concatenate([ck_g[j] for j in range(NCHIP)], axis=1)
    w2 = jnp.concatenate([w2_g[j] for j in range(NCHIP)], axis=1)
    a2 = jnp.concatenate([a2_g[j] for j in range(NCHIP)], axis=1)

    ada_w = a["ada_w"][0]
    MW = ada_w.shape[1]
    ada_b_loc = lax.dynamic_slice(a["ada_b"], (0, chip * MW), (1, MW))

    def mod_body(c_ref, w_ref, b_ref, o_ref):
        o_ref[...] = _dot(_silu(c_ref[...]), w_ref[...], HI) + b_ref[...]

    modp = pl.pallas_call(mod_body, name="ada_mod", out_shape=jax.ShapeDtypeStruct((NDEV * nb, MW), F32),
                          compiler_params=_cparams())(c_all, ada_w, ada_b_loc)
    (mod_g,) = _comm_call("scatter_mod", [modp.reshape(NDEV, nb, MW)],
                          [jax.ShapeDtypeStruct((NDEV, nb, MW), F32)],
                          functools.partial(_scatter_plan, 1), 7)
    mod = mod_g.reshape(NCHIP, 2, nb, MW)
    mod = mod[:, 0].transpose(1, 0, 2).reshape(nb, NCHIP * MW)

    RW = D // NCHIP
    sh_s = []

    def grads_hook(chip_part):
        sh_s.append(chip_part[0].reshape(NCHIP, WS, D // 2))
        sh_s.extend(t.reshape(NCHIP, RW // 2, D) for t in chip_part[1:4])
        sh_s.extend(t.reshape(NCHIP, -1, RW // 2) for t in chip_part[4:])
        return (sh_s, [jax.ShapeDtypeStruct(t.shape, t.dtype) for t in sh_s], functools.partial(_scatter_plan, 0),
                3 * len(sh_s))

    loss_p, gx, dmod, got_big, d_small = _local_step(
        a["x"].reshape(T, D), a["loss_target"].reshape(T, D), mod, wmain, wlora, late_w, ck, w2, a2, small_w, nb, S,
        grads_hook)
    loss = lax.psum(loss_p, ("x", "y", "c"))

    d_small["ada_b"] = _colsum(dmod)
    small_vec = jnp.concatenate([d_small[n] for n, _ in SMALL], axis=1)
    dmod_s = dmod.reshape(nb, NCHIP, MW).transpose(1, 0, 2)
    dmod_s = jnp.repeat(dmod_s, 2, axis=0)
    small_s = jnp.broadcast_to(small_vec[None], (NDEV, 1, NSMALL))
    got = _comm_call("scatter_small", [dmod_s, small_s], [jax.ShapeDtypeStruct(t.shape, F32) for t in (dmod_s, small_s)],
                     functools.partial(_scatter_plan, 2), 14)
    dmod_all, small_all = got[0].reshape(NDEV * nb, MW), got[1].reshape(NDEV, NSMALL)

    def shard_sum(recv, sent):
        chip_i = 2 * lax.axis_index("x") + lax.axis_index("y")
        s = None
        for j in range(NCHIP):
            t = jnp.where(chip_i == j, sent[j], recv[j]).astype(F32)
            s = t if s is None else s + t
        return [s]

    fin = [_ew(shard_sum, "shard_sum_%d" % i, [t, sh_s[i]], 1, 128)[0] for i, t in enumerate(got_big)]
    oth = _comm_call("join_halves", fin, [jax.ShapeDtypeStruct(t.shape, F32) for t in fin], _join_plan, len(fin))

    outs = {}

    def upd_halves(name, mine, other):
        shp = a[name].shape
        R, W = 2 * mine.shape[0], mine.shape[1]
        tm = 128
        nh = R // 2 // tm

        def body(w_ref, m_ref, v_ref, f_ref, o_ref, g_ref, d_ref, m2_ref, v2_ref):
            g = jnp.where(pl.program_id(0) // nh == lax.axis_index("c"), f_ref[...], o_ref[...])
            g_ref[...] = g
            d_ref[...], m2_ref[...], v2_ref[...] = _adamw(w_ref[...], g, m_ref[...], v_ref[...])

        full = pl.BlockSpec((None, tm, W), lambda i: (0, i, 0))
        half = pl.BlockSpec((tm, W), lambda i: (i % nh, 0))
        assert shp == (1, R, W)
        outs[name] = pl.pallas_call(
            body, name="adamw_" + name, grid=(R // tm,), in_specs=[full] * 3 + [half] * 2, out_specs=[full] * 4,
            out_shape=[jax.ShapeDtypeStruct(shp, F32)] * 4, compiler_params=_cparams(("parallel",)),
        )(*[a[p + name] for p in ("", "m_", "v_")], mine, other)

    for name, f, o in zip(("w_conv_out", "w_rwkv_out", "w_out"), fin[1:4], oth[1:4]):
        upd_halves(name, f, o)

    tw = WS // 4

    def w_in_body(w_ref, m_ref, v_ref, f_ref, o_ref, g_ref, d_ref, m2_ref, v2_ref):
        first = lax.axis_index("c") == 0
        g = jnp.concatenate([jnp.where(first, f_ref[...], o_ref[...]), jnp.where(first, o_ref[...], f_ref[...])], axis=1)
        g_ref[...] = g
        d_ref[...], m2_ref[...], v2_ref[...] = _adamw(w_ref[...], g, m_ref[...], v_ref[...])

    full = pl.BlockSpec((tw, D), lambda i: (i, 0))
    half = pl.BlockSpec((tw, D // 2), lambda i: (i, 0))
    res = pl.pallas_call(
        w_in_body, name="adamw_w_in", grid=(WS // tw,), in_specs=[full] * 3 + [half] * 2, out_specs=[full] * 4,
        out_shape=[jax.ShapeDtypeStruct((WS, D), F32)] * 4, compiler_params=_cparams(("parallel",)),
    )(w_in_t, m_w_in_t, v_w_in_t, fin[0], oth[0])
    outs["w_in"] = [jnp.transpose(r)[None] for r in res]

    def upd(name, g):
        shp = a[name].shape
        ins = [a[p + name].reshape(g.shape) for p in ("", "m_", "v_")]
        res = _ew(lambda w_, m_, v_, g_: [g_, *_adamw(w_, g_, m_, v_)], "adamw_" + name, [*ins, g], 4, 128)
        outs[name] = [r.reshape(shp) for r in res]

    for name, f, o in zip(("conv_k", "rwkv_w2", "rwkv_a2"), fin[4:], oth[4:]):
        both = jnp.where(c_i == 0, jnp.stack([f, o]), jnp.stack([o, f]))
        upd(name, both.transpose(1, 0, 2).reshape(-1, RW))

    def adaw_body(c_ref, dm_ref, w_ref, m_ref, v_ref, g_ref, d_ref, m2_ref, v2_ref):
        g = _dot_tn(_silu(c_ref[...]), dm_ref[...], HI)
        g_ref[...] = g
        d_ref[...], m2_ref[...], v2_ref[...] = _adamw(w_ref[...], g, m_ref[...], v_ref[...])

    res = pl.pallas_call(adaw_body, name="adamw_ada_w", out_shape=[jax.ShapeDtypeStruct((D, MW), F32)] * 4,
                         compiler_params=_cparams())(c_all, dmod_all, ada_w, a["m_ada_w"][0], a["v_ada_w"][0])
    outs["ada_w"] = [r.reshape(a["ada_w"].shape) for r in res]

    def small_body(gs_ref, *refs):
        g_all = _sum_slots(gs_ref[...])
        ins, out_refs = refs[:3 * len(SMALL)], refs[3 * len(SMALL):]
        off = 0
        for i, (_, sz) in enumerate(SMALL):
            w_ref, m_ref, v_ref = ins[3 * i:3 * i + 3]
            g = g_all[:, off:off + sz]
            res4 = (g, *_adamw(w_ref[...], g, m_ref[...], v_ref[...]))
            for r, val in zip(out_refs[4 * i:4 * i + 4], res4):
                r[...] = val
            off += sz

    small_ins = [a[p + n].reshape(1, sz) for n, sz in SMALL for p in ("", "m_", "v_")]
    res = pl.pallas_call(
        small_body, name="adamw_small", compiler_params=_cparams(),
        out_shape=[jax.ShapeDtypeStruct((1, sz), F32) for _, sz in SMALL for _ in range(4)],
    )(small_all.reshape(NDEV, 1, NSMALL), *small_ins)
    for i, (n, _) in enumerate(SMALL):
        outs[n] = [r.reshape(a[n].shape) for r in res[4 * i:4 * i + 4]]

    return (loss, gx.reshape(nb, S, D), *[outs[n][0] for n in WEIGHTS], *[outs[n][1] for n in WEIGHTS],
            *[outs[n][2] for n in WEIGHTS], *[outs[n][3] for n in WEIGHTS])


def _colsum(dmod):
    def body(d_ref, o_ref):
        o_ref[...] = jnp.sum(d_ref[...], axis=0, keepdims=True)
    return pl.pallas_call(body, name="ada_b_rowsum", out_shape=jax.ShapeDtypeStruct((1, dmod.shape[1]), F32),
                          compiler_params=_cparams())(dmod)


def kernel(x, c, ada_w, ada_b, norm_g, w_in, conv_k, conv_b, conv_ln_g, conv_ln_b, w_conv_out, rwkv_mu, rwkv_w0, rwkv_w2, rwkv_a0, rwkv_a2, rwkv_k_k, rwkv_k_a, rwkv_r_k, rwkv_gn_g, rwkv_gn_b, w_rwkv_out, w_out, final_g, loss_target, m_ada_w, m_ada_b, m_norm_g, m_w_in, m_conv_k, m_conv_b, m_conv_ln_g, m_conv_ln_b, m_w_conv_out, m_rwkv_mu, m_rwkv_w0, m_rwkv_w2, m_rwkv_a0, m_rwkv_a2, m_rwkv_k_k, m_rwkv_k_a, m_rwkv_r_k, m_rwkv_gn_g, m_rwkv_gn_b, m_w_rwkv_out, m_w_out, m_final_g, v_ada_w, v_ada_b, v_norm_g, v_w_in, v_conv_k, v_conv_b, v_conv_ln_g, v_conv_ln_b, v_w_conv_out, v_rwkv_mu, v_rwkv_w0, v_rwkv_w2, v_rwkv_a0, v_rwkv_a2, v_rwkv_k_k, v_rwkv_k_a, v_rwkv_r_k, v_rwkv_gn_g, v_rwkv_gn_b, v_w_rwkv_out, v_w_out, v_final_g):
    return _step(dict(locals()))
```

```python
import functools

import numpy as np
import jax
import jax.numpy as jnp
from jax import lax
from jax.experimental import pallas as pl
from jax.experimental.pallas import tpu as pltpu

F32 = jnp.float32
BF16 = jnp.bfloat16
HI = lax.Precision.HIGHEST
MESH = pl.DeviceIdType.MESH
ANY = pl.BlockSpec(memory_space=pl.ANY)

D = 1024
NH = 16
HN = 64
LORA = 128
DMAIN = 9 * D
CH = 64
CW = 31
NCHIP = 4
NDEV = 8
VMEM_LIMIT = 56 * 1024 * 1024

RMS_EPS = 1e-6
LN_EPS = 1e-5
GN_EPS = 64e-5
L2_EPS = 1e-12
ADAM_LR = 0.001
ADAM_B1 = 0.9
ADAM_B2 = 0.999
ADAM_EPS = 1e-08
ADAM_WD = 0.01
ADAM_STEP = 10

SMALL = (("ada_b", 3072), ("norm_g", 1024), ("conv_b", 1024), ("conv_ln_g", 1024), ("conv_ln_b", 1024),
         ("rwkv_mu", 3200), ("rwkv_w0", 1024), ("rwkv_a0", 1024), ("rwkv_k_k", 1024), ("rwkv_k_a", 1024),
         ("rwkv_r_k", 1024), ("rwkv_gn_g", 1024), ("rwkv_gn_b", 1024), ("final_g", 1024))
NSMALL = sum(n for _, n in SMALL)

WEIGHTS = ['ada_w', 'ada_b', 'norm_g', 'w_in', 'conv_k', 'conv_b', 'conv_ln_g', 'conv_ln_b', 'w_conv_out', 'rwkv_mu',
           'rwkv_w0', 'rwkv_w2', 'rwkv_a0', 'rwkv_a2', 'rwkv_k_k', 'rwkv_k_a', 'rwkv_r_k', 'rwkv_gn_g', 'rwkv_gn_b',
           'w_rwkv_out', 'w_out', 'final_g']


def _cparams(sem=None, **kw):
    if sem is not None:
        kw["dimension_semantics"] = sem
    return pltpu.CompilerParams(vmem_limit_bytes=VMEM_LIMIT, **kw)


def _dot(a, b, prec=None):
    return jnp.dot(a, b, preferred_element_type=F32, precision=prec)


def _dot_nt(a, b, prec=None):
    return lax.dot_general(a, b, (((1,), (1,)), ((), ())), preferred_element_type=F32, precision=prec)


def _dot_tn(a, b, prec=None):
    return lax.dot_general(a, b, (((0,), (0,)), ((), ())), preferred_element_type=F32, precision=prec)


def _pdot(f, a, b, p):
    if p == "hi":
        return f(a, b, HI)
    ah, bh = a.astype(BF16), b.astype(BF16)
    if p == "bf":
        return f(ah, bh)
    al, bl = (a - ah.astype(F32)).astype(BF16), (b - bh.astype(F32)).astype(BF16)
    return f(ah, bh) + (f(ah, bl) + f(al, bh))


P_SCORE = "b3"
P_INV = "bf"
P_APPLY = "bf"


def _sigmoid(z):
    return 1.0 / (1.0 + jnp.exp(-z))


def _silu(z):
    return z * _sigmoid(z)


def _matmul(a, b, mode, name, tm, tn, tk, ride=None, out_t=False, b_rows=None):
    if mode == "nn":
        (M, K), N = a.shape, b.shape[1]
        a_spec = pl.BlockSpec((tm, tk), lambda j, i, k: (i, k))
        b_spec = pl.BlockSpec((tk, tn), lambda j, i, k: (k, j))
        if b_rows is not None:
            assert b_rows[0] == K
            b_spec = pl.BlockSpec((pl.Element(tk), pl.Element(tn)), lambda j, i, k: (b_rows[1](k, tk), j * tn))
        f = _dot
    elif mode == "nt":
        (M, K), N = a.shape, b.shape[0]
        a_spec = pl.BlockSpec((tm, tk), lambda j, i, k: (i, k))
        b_spec = pl.BlockSpec((tn, tk), lambda j, i, k: (j, k))
        if b_rows is not None:
            N = b_rows[0]
            b_spec = pl.BlockSpec((pl.Element(tn), pl.Element(tk)), lambda j, i, k: (b_rows[1](j, tn), k * tk))
        f = _dot_nt
    else:
        (K, M), N = a.shape, b.shape[1]
        a_spec = pl.BlockSpec((tk, tm), lambda j, i, k: (k, i))
        b_spec = pl.BlockSpec((tk, tn), lambda j, i, k: (k, j))
        f = _dot_tn
    assert M % tm == 0 and N % tn == 0 and K % tk == 0, (name, M, N, K)

    grid = (N // tn, M // tm, K // tk)
    o_spec = pl.BlockSpec((tm, tn), lambda j, i, k: (i, j))
    o_shape = jax.ShapeDtypeStruct((M, N), F32)

    scratch = []
    if out_t:
        o_spec = pl.BlockSpec((None, tn, tm), lambda j, i, k: (i, j, 0))
        o_shape = jax.ShapeDtypeStruct((M // tm, N, tm), F32)
        scratch = [pltpu.VMEM((tm, tn), F32)]

    def step(a_ref, b_ref, o_ref, *acc):
        acc_ref = acc[0] if out_t else o_ref

        @pl.when(pl.program_id(2) == 0)
        def _():
            acc_ref[...] = jnp.zeros_like(acc_ref)
        acc_ref[...] += f(a_ref[...], b_ref[...])
        if out_t:
            @pl.when(pl.program_id(2) == grid[2] - 1)
            def _():
                o_ref[...] = acc_ref[...].T

    if ride is None:
        return pl.pallas_call(
            step, name=name, grid=grid, in_specs=[a_spec, b_spec], out_specs=o_spec, out_shape=o_shape,
            scratch_shapes=scratch, compiler_params=_cparams(("parallel", "parallel", "arbitrary")),
        )(a, b)

    r_ins, r_shapes, plan, n_rem = ride
    n_ri, n_ro = len(r_ins), len(r_shapes)

    def body(a_ref, b_ref, *rest):
        r_in, o_ref, r_out = rest[:n_ri], rest[n_ri], rest[n_ri + 1:n_ri + 1 + n_ro]
        send_sems, recv_sems, *acc = rest[n_ri + 1 + n_ro:]
        loc, rem = plan(r_in, r_out, _place())
        assert not loc and len(rem) == n_rem, (name, len(loc), len(rem))
        copies = [pltpu.make_async_remote_copy(src_ref=s, dst_ref=d, send_sem=send_sems.at[i], recv_sem=recv_sems.at[i],
                                               device_id=peer, device_id_type=MESH) for i, (s, d, peer) in enumerate(rem)]
        pid = [pl.program_id(ax) for ax in range(3)]

        @pl.when((pid[0] == 0) & (pid[1] == 0) & (pid[2] == 0))
        def _():
            for cp in copies:
                cp.start()

        step(a_ref, b_ref, o_ref, *acc)

        @pl.when((pid[0] == grid[0] - 1) & (pid[1] == grid[1] - 1) & (pid[2] == grid[2] - 1))
        def _():
            for cp in copies:
                cp.wait_send()
            for cp in copies:
                cp.wait_recv()

    return pl.pallas_call(
        body, name=name, grid=grid, in_specs=[a_spec, b_spec] + [ANY] * n_ri, out_specs=[o_spec] + [ANY] * n_ro,
        out_shape=[o_shape] + list(r_shapes),
        scratch_shapes=[pltpu.SemaphoreType.DMA((n_rem,)), pltpu.SemaphoreType.DMA((n_rem,))] + scratch,
        compiler_params=_cparams(("arbitrary", "arbitrary", "arbitrary"), has_side_effects=True),
    )(a, b, *r_ins)


def _rows(fn, name, T, S, tm, rows, bpars, gpars, outs, baccs, gaccs):
    nb = T // S
    tps = S // tm
    n_r, n_b, n_g, n_o, n_ba, n_ga = len(rows), len(bpars), len(gpars), len(outs), len(baccs), len(gaccs)

    def body(*refs):
        r_refs = refs[:n_r]
        b_refs = refs[n_r:n_r + n_b]
        g_refs = refs[n_r + n_b:n_r + n_b + n_g]
        o_refs = refs[n_r + n_b + n_g:n_r + n_b + n_g + n_o]
        ba_refs = refs[n_r + n_b + n_g + n_o:n_r + n_b + n_g + n_o + n_ba]
        ga_refs = refs[n_r + n_b + n_g + n_o + n_ba:]
        i = pl.program_id(0)
        o_vals, ba_vals, ga_vals = fn([r[...] for r in r_refs], [r[...] for r in b_refs], [r[...] for r in g_refs])
        for r, v in zip(o_refs, o_vals):
            r[...] = v.astype(r.dtype)
        if n_ba:
            @pl.when(i % tps == 0)
            def _():
                for r in ba_refs:
                    r[...] = jnp.zeros_like(r)
            for r, v in zip(ba_refs, ba_vals):
                r[...] += v.reshape(r.shape)
        if n_ga:
            @pl.when(i == 0)
            def _():
                for r in ga_refs:
                    r[...] = jnp.zeros_like(r)
            for r, v in zip(ga_refs, ga_vals):
                r[...] += v.reshape(r.shape)

    def row_spec(arr, w, cb, kind="tile"):
        hr = 8 * (4 // arr.dtype.itemsize)
        if kind == "prev":
            return pl.BlockSpec((hr, w), lambda i: (jnp.maximum(i * (tm // hr) - 1, 0), cb))
        if kind == "next":
            return pl.BlockSpec((hr, w), lambda i: (jnp.minimum((i + 1) * (tm // hr), T // hr - 1), cb))
        return pl.BlockSpec((tm, w), lambda i: (i, cb))

    in_specs = [row_spec(*r) for r in rows]
    in_specs += [pl.BlockSpec((None, 1, p.shape[-1]), lambda i: (i // tps, 0, 0)) for p in bpars]
    in_specs += [pl.BlockSpec(p.shape, lambda i: (0, 0)) for p in gpars]
    out_specs = [pl.BlockSpec((tm, w), lambda i: (i, 0)) for w, _ in outs]
    out_specs += [pl.BlockSpec((None, 1, w), lambda i: (i // tps, 0, 0)) for w in baccs]
    out_specs += [pl.BlockSpec(s, lambda i: (0, 0)) for s in gaccs]
    out_shape = [jax.ShapeDtypeStruct((T, w), dt) for w, dt in outs]
    out_shape += [jax.ShapeDtypeStruct((nb, 1, w), F32) for w in baccs]
    out_shape += [jax.ShapeDtypeStruct(s, F32) for s in gaccs]
    res = pl.pallas_call(
        body, name=name, grid=(T // tm,), in_specs=in_specs, out_specs=out_specs, out_shape=out_shape,
        compiler_params=_cparams(("arbitrary",)),
    )(*[r[0] for r in rows], *bpars, *gpars)
    return res[:n_o], res[n_o:n_o + n_ba], res[n_o + n_ba:]


@jax.custom_vjp
def _gsum(z, G):
    zh = z.astype(BF16)
    zl = (z - zh.astype(F32)).astype(BF16)
    r = _dot_nt(zh, G) + _dot_nt(zl, G)
    rh = r.astype(BF16)
    rl = (r - rh.astype(F32)).astype(BF16)
    return _dot(rh, G) + _dot(rl, G)


def _dot3(x, w):
    xh = x.astype(BF16).astype(F32)
    wh = w.astype(BF16).astype(F32)
    xc = jnp.concatenate([xh, xh, x - xh], axis=1).astype(BF16)
    wc = jnp.concatenate([wh, w - wh, wh], axis=0).astype(BF16)
    return _dot(xc, wc)


_gsum.defvjp(lambda z, G: (_gsum(z, G), G), lambda G, ct: (_gsum(ct, G), jnp.zeros_like(G)))


def _s1(x, g, scale, shift):
    y = x * lax.rsqrt(jnp.mean(x * x, axis=-1, keepdims=True) + RMS_EPS)
    return (y * g) * (1.0 + scale) + shift


def _s3(uc, og, cb, lg, lb):
    u = uc + cb
    mu = jnp.mean(u, axis=-1, keepdims=True)
    d = u - mu
    var = jnp.mean(d * d, axis=-1, keepdims=True)
    y = d * lax.rsqrt(var + LN_EPS) * lg + lb
    return _silu(y) * _silu(og)


def _s4(r0, k0, v0, l0, pr, pk, pv, plo, mu_r, mu_k, mu_v, mu_l, w0, w2p, a0, a2p, k_k, k_a, G):
    r = r0 + mu_r * (pr - r0)
    k = k0 + mu_k * (pk - k0)
    v = v0 + mu_v * (pv - v0)
    lo = l0 + mu_l * (plo - l0)
    w_pre = w0 + _dot3(jnp.tanh(lo), w2p)
    lw = -np.float32(np.exp(-0.5)) * _sigmoid(w_pre)
    a = _sigmoid(a0 + _dot3(lo, a2p))
    kkr = k * k_k
    ss = _gsum(kkr * kkr, G)
    kk = kkr / jnp.maximum(jnp.sqrt(ss), L2_EPS)
    k2 = k * (1.0 + (a - 1.0) * k_a)
    return r, lw, k2, v, kk, kk * a


def _s5(o, r, k2, v, og, gg, gb, rk, G):
    mu = _gsum(o, G) * (1.0 / HN)
    d = o - mu
    var = _gsum(d * d, G) * (1.0 / HN)
    y = d * lax.rsqrt(var + GN_EPS) * gg + gb
    bonus = _gsum(r * k2 * rk, G)
    return (y + bonus * v) * _silu(og)


def _s6(yc, yr, gc, gr):
    return _sigmoid(gc) * yc + _sigmoid(gr) * yr


def _s7(x, out, tgt, gate, fg):
    x2 = x + gate * out
    y = x2 * lax.rsqrt(jnp.mean(x2 * x2, axis=-1, keepdims=True) + RMS_EPS) * fg
    e = y - tgt
    return 0.5 * jnp.sum(jnp.mean(e * e, axis=-1))


def _solve_all_fwd(a_kbs, rhss, cm):
    H = range(len(a_kbs))
    xi = [cm[2] - cm[3] * a_kbs[j] for j in H]
    for lvl in range(1, 6):
        t = [_pdot(_dot, xi[j], cm[3 + lvl] * a_kbs[j], P_INV) for j in H]
        xi = [xi[j] - _pdot(_dot, t[j], xi[j], P_INV) for j in H]
    u = tuple(_pdot(_dot, xi[j], rhss[j], P_APPLY) for j in H)
    return u, (xi, u, cm)


def _solve_all_bwd(res, dus):
    xi, u, cm = res
    H = range(len(u))
    g = tuple(_pdot(_dot_tn, xi[j], dus[j], P_APPLY) for j in H)
    da = tuple(-(cm[1] * _pdot(_dot_nt, g[j], u[j], P_APPLY)) for j in H)
    return da, g, jnp.zeros_like(cm)


@jax.custom_vjp
def _solve_all(a_kbs, rhss, cm):
    return _solve_all_fwd(a_kbs, rhss, cm)[0]


_solve_all.defvjp(_solve_all_fwd, _solve_all_bwd)


def _chunk(sts, r, lw, k, v, kk, b, cm):
    cum = _dot(cm[0], lw, HI)
    ein = jnp.exp(-cum)
    rt = r * jnp.exp(cum)
    kkt = kk * jnp.exp(cum - lw)
    kh = k * ein
    bh = b * ein
    ec = jnp.exp(jnp.sum(lw, axis=0, keepdims=True))
    khe = kh * ec
    bhe = bh * ec
    H = range(len(sts))
    tri, strict, eye = cm[0], cm[1], cm[2]
    rt, kkt, kh, bh, v, khe, bhe, ec = ([a[:, j * HN:(j + 1) * HN] for j in H] for a in (rt, kkt, kh, bh, v, khe, bhe, ec))
    lhs = [jnp.concatenate([kkt[j], rt[j]], axis=0) for j in H]
    rhs_s = [jnp.concatenate([bh[j], kh[j]], axis=0) for j in H]
    lh = [a.astype(BF16).astype(F32) for a in lhs]
    rh = [a.astype(BF16).astype(F32) for a in rhs_s]
    lc = [jnp.concatenate([lh[j], lh[j], lhs[j] - lh[j]], axis=1).astype(BF16) for j in H]
    rc = [jnp.concatenate([rh[j], rhs_s[j] - rh[j], rh[j]], axis=1).astype(BF16) for j in H]
    sc = [_dot_nt(lc[j], rc[j]) for j in H]
    a_kb = [strict * sc[j][:CH, :CH] for j in H]
    a_kk = [strict * sc[j][:CH, CH:] for j in H]
    a_rb = [tri * sc[j][CH:, :CH] for j in H]
    a_rk = [tri * sc[j][CH:, CH:] for j in H]
    ps = [_dot_nt(lhs[j].astype(BF16), sts[j].astype(BF16)) for j in H]
    pv = [_dot(jnp.concatenate([a_kk[j], a_rk[j]], axis=0).astype(BF16), v[j].astype(BF16)) for j in H]
    rhs = [ps[j][:CH] + pv[j][:CH] for j in H]
    o0 = [ps[j][CH:] + pv[j][CH:] for j in H]
    u = _solve_all(tuple(a_kb), tuple(rhs), cm)
    o = [o0[j] - _pdot(_dot, a_rb[j], u[j], P_APPLY) for j in H]
    st2 = [sts[j] * ec[j] + _dot_tn(jnp.concatenate([v[j], u[j]], axis=0).astype(BF16),
                                    jnp.concatenate([khe[j], -bhe[j]], axis=0).astype(BF16)) for j in H]
    return jnp.concatenate(o, axis=1), tuple(st2)


def _chunk_consts():
    t = np.arange(CH)[:, None]
    s = np.arange(CH)[None, :]
    mats = [(t >= s), (t > s), (t == s)]
    for lvl in range(6):
        sz = 1 << lvl
        mats.append(((t // sz) % 2 == 1) & ((s // sz) == (t // sz) - 1))
    mats.append(np.zeros((CH, CH), bool))
    return np.stack(mats).astype(np.float32)


def _adamw(w, g, m, v):
    m = ADAM_B1 * m + (1.0 - ADAM_B1) * g
    v = ADAM_B2 * v + (1.0 - ADAM_B2) * (g * g)
    m_hat = m / (1.0 - ADAM_B1 ** ADAM_STEP)
    v_hat = v / (1.0 - ADAM_B2 ** ADAM_STEP)
    delta = -ADAM_LR * (m_hat / (jnp.sqrt(v_hat) + ADAM_EPS) + ADAM_WD * w)
    return delta, m, v


CT = 128
RB = 64
WIN = RB + 32


def _conv_fwd(pm, ck, T, S):
    nb = T // S

    def body(val_ref, gate_ref, ck_ref, out_ref, ubuf):
        ubuf[0:32, :] = jnp.zeros((32, CT), F32)
        ubuf[32:, :] = val_ref[...] * _sigmoid(gate_ref[...])

        def blk(rb, carry):
            base = pl.multiple_of(rb * RB, RB)
            win = ubuf[pl.ds(base, WIN), :]
            acc = jnp.zeros((RB, CT), F32)
            for j in range(CW):
                acc = acc + ck_ref[j:j + 1, :] * pltpu.roll(win, (WIN - (2 + j)) % WIN, 0)[0:RB, :]
            out_ref[pl.ds(base, RB), :] = acc
            return carry

        lax.fori_loop(0, S // RB, blk, 0)

    return pl.pallas_call(
        body, name="conv_fwd", grid=(D // CT, nb),
        in_specs=[pl.BlockSpec((S, CT), lambda ct, b: (b, ct)),
                  pl.BlockSpec((S, CT), lambda ct, b: (b, D // CT + ct)),
                  pl.BlockSpec((32, CT), lambda ct, b: (0, ct))],
        out_specs=pl.BlockSpec((S, CT), lambda ct, b: (b, ct)),
        out_shape=jax.ShapeDtypeStruct((T, D), F32),
        scratch_shapes=[pltpu.VMEM((S + 32, CT), F32)],
        compiler_params=_cparams(("parallel", "arbitrary")),
    )(pm, pm, ck)


def _conv_bwd(pm, duc, ck, T, S):
    nb = T // S

    def body(val_ref, gate_ref, duc_ref, ck_ref, dval_ref, dgate_ref, dck_ref, ubuf, dbuf, acc):
        b = pl.program_id(1)
        ubuf[0:32, :] = jnp.zeros((32, CT), F32)
        ubuf[32:, :] = val_ref[...] * _sigmoid(gate_ref[...])
        dbuf[0:S, :] = duc_ref[...]
        dbuf[S:, :] = jnp.zeros((32, CT), F32)
        acc[...] = jnp.zeros_like(acc)

        def blk(rb, carry):
            base = pl.multiple_of(rb * RB, RB)
            uwin = ubuf[pl.ds(base, WIN), :]
            dwin = dbuf[pl.ds(base, WIN), :]
            dblk = dwin[0:RB, :]
            du = jnp.zeros((RB, CT), F32)
            for j in range(CW):
                du = du + ck_ref[j:j + 1, :] * pltpu.roll(dwin, (WIN - (CW - 1 - j)) % WIN, 0)[0:RB, :]
                ush = pltpu.roll(uwin, (WIN - (2 + j)) % WIN, 0)[0:RB, :]
                acc[j] += jnp.sum((dblk * ush).reshape(RB // 8, 8, CT), axis=0)
            val = val_ref[pl.ds(base, RB), :]
            sg = _sigmoid(gate_ref[pl.ds(base, RB), :])
            dval_ref[pl.ds(base, RB), :] = (du * sg).astype(BF16)
            dgate_ref[pl.ds(base, RB), :] = (du * val * sg * (1.0 - sg)).astype(BF16)
            return carry

        lax.fori_loop(0, S // RB, blk, 0)

        @pl.when(b == 0)
        def _():
            dck_ref[...] = jnp.zeros_like(dck_ref)
        for j in range(CW):
            dck_ref[j:j + 1, :] += jnp.sum(acc[j], axis=0, keepdims=True)

    return pl.pallas_call(
        body, name="conv_bwd", grid=(D // CT, nb),
        in_specs=[pl.BlockSpec((S, CT), lambda ct, b: (b, ct)),
                  pl.BlockSpec((S, CT), lambda ct, b: (b, D // CT + ct)),
                  pl.BlockSpec((S, CT), lambda ct, b: (b, ct)),
                  pl.BlockSpec((32, CT), lambda ct, b: (0, ct))],
        out_specs=[pl.BlockSpec((S, CT), lambda ct, b: (b, ct)),
                   pl.BlockSpec((S, CT), lambda ct, b: (b, ct)),
                   pl.BlockSpec((32, CT), lambda ct, b: (0, ct))],
        out_shape=[jax.ShapeDtypeStruct((T, D), BF16), jax.ShapeDtypeStruct((T, D), BF16),
                   jax.ShapeDtypeStruct((32, D), F32)],
        scratch_shapes=[pltpu.VMEM((S + 32, CT), F32), pltpu.VMEM((S + 32, CT), F32), pltpu.VMEM((32, 8, CT), F32)],
        compiler_params=_cparams(("parallel", "arbitrary")),
    )(pm, pm, duc, ck)


def _scan_fwd(ins, cm, nb, S):
    nc = S // CH
    nch = nb * NH
    blk = pl.BlockSpec((nb, CH, D), lambda i: (0, i, 0))
    hblk = pl.BlockSpec((nb, NH, None, HN, HN), lambda i: (0, 0, i, 0, 0))

    def body(r_ref, lw_ref, k_ref, v_ref, kk_ref, b_ref, cm_ref, o_ref, hs_ref, st):
        @pl.when(pl.program_id(0) == 0)
        def _():
            st[...] = jnp.zeros_like(st)
        s0 = [st[j] for j in range(nch)]
        for j in range(nch):
            hs_ref[j // NH, j % NH] = s0[j]
        vals = [jnp.concatenate([ref[q] for q in range(nb)], axis=1) for ref in (r_ref, lw_ref, k_ref, v_ref, kk_ref, b_ref)]
        o, s1 = _chunk(s0, *vals, cm_ref[...])
        for q in range(nb):
            o_ref[q] = o[:, q * D:(q + 1) * D]
        for j in range(nch):
            st[j] = s1[j]

    o, hs = pl.pallas_call(
        body, name="scan_fwd", grid=(nc,),
        in_specs=[blk] * 6 + [pl.BlockSpec(cm.shape, lambda i: (0, 0, 0))],
        out_specs=[blk, hblk],
        out_shape=[jax.ShapeDtypeStruct((nb, S, D), F32), jax.ShapeDtypeStruct((nb, NH, nc, HN, HN), F32)],
        scratch_shapes=[pltpu.VMEM((nch, HN, HN), F32)],
        compiler_params=_cparams(("arbitrary",)),
    )(*[a.reshape(nb, S, D) for a in ins], cm)
    return o.reshape(nb * S, D), hs


def _scan_bwd(ins, hs, do, cm, nb, S):
    nc = S // CH
    nch = nb * NH
    blk = pl.BlockSpec((nb, CH, D), lambda i: (0, nc - 1 - i, 0))
    hblk = pl.BlockSpec((nb, NH, None, HN, HN), lambda i: (0, 0, nc - 1 - i, 0, 0))

    def body(r_ref, lw_ref, k_ref, v_ref, kk_ref, b_ref, hs_ref, do_ref, cm_ref,
             dr_ref, dlw_ref, dk_ref, dv_ref, dkk_ref, db_ref, dst):
        @pl.when(pl.program_id(0) == 0)
        def _():
            dst[...] = jnp.zeros_like(dst)
        cmv = cm_ref[...]
        side = lambda ref: jnp.concatenate([ref[q] for q in range(nb)], axis=1)
        f = lambda s0, r, lw, k, v, kk, b: _chunk(s0, r, lw, k, v, kk, b, cmv)
        _, vjp = jax.vjp(f, [hs_ref[j // NH, j % NH] for j in range(nch)],
                         *[side(ref) for ref in (r_ref, lw_ref, k_ref, v_ref, kk_ref, b_ref)])
        ds0, *grads = vjp((side(do_ref), tuple(dst[j] for j in range(nch))))
        for j in range(nch):
            dst[j] = ds0[j]
        for ref, g in zip((dr_ref, dlw_ref, dk_ref, dv_ref, dkk_ref, db_ref), grads):
            for q in range(nb):
                ref[q] = g[:, q * D:(q + 1) * D]

    outs = pl.pallas_call(
        body, name="scan_bwd", grid=(nc,),
        in_specs=[blk] * 6 + [hblk, blk, pl.BlockSpec(cm.shape, lambda i: (0, 0, 0))],
        out_specs=[blk] * 6,
        out_shape=[jax.ShapeDtypeStruct((nb, S, D), F32)] * 6,
        scratch_shapes=[pltpu.VMEM((nch, HN, HN), F32)],
        compiler_params=_cparams(("arbitrary",)),
    )(*[a.reshape(nb, S, D) for a in ins], hs, do.reshape(nb, S, D), cm)
    return [a.reshape(nb * S, D) for a in outs]


def _ew(fn, name, ins, n_out, tm, out_dtype=F32):
    R, W = ins[0].shape[-2:]
    tm = min(tm, R)
    if R % tm:
        tm = R // 2
    assert R % tm == 0 and (tm % 16 == 0 or tm == R), (name, R, tm)

    def body(*refs):
        vals = fn(*[r[...] for r in refs[:len(ins)]])
        for r, v in zip(refs[len(ins):], vals):
            r[...] = v.astype(r.dtype)

    def spec(a):
        if a.ndim == 3:
            return pl.BlockSpec((a.shape[0], tm, W), lambda i: (0, i, 0))
        return pl.BlockSpec((tm, W), lambda i: (i, 0))

    return pl.pallas_call(
        body, name=name, grid=(R // tm,), in_specs=[spec(a) for a in ins],
        out_specs=[pl.BlockSpec((tm, W), lambda i: (i, 0))] * n_out,
        out_shape=[jax.ShapeDtypeStruct((R, W), out_dtype)] * n_out,
        compiler_params=_cparams(("parallel",)),
    )(*ins)


def _sum_slots(r):
    s = r[0]
    for j in range(1, r.shape[0]):
        s = s + r[j]
    return s


def _place():
    x, y, c = lax.axis_index("x"), lax.axis_index("y"), lax.axis_index("c")
    return x, y, c


def _flip(v, d):
    return 1 - v if d else v


CHIP_PEERS = ((1, 0), (0, 1), (1, 1))
DEV_PEERS = tuple((dx, dy, dc) for dx in (0, 1) for dy in (0, 1) for dc in (0, 1))[1:]


def _comm_call(name, ins, out_shapes, plan, n_rem, n_fwd=0):
    n_in = len(ins)

    def body(*refs):
        in_refs, out_refs = refs[:n_in], refs[n_in:n_in + len(out_shapes)]
        send_sems, recv_sems, loc_sems = refs[n_in + len(out_shapes):]
        loc, rem, *rest = plan(in_refs, out_refs, _place())
        fwd = rest[0] if rest else []
        assert len(rem) == n_rem and len(fwd) == n_fwd and len(loc) <= 2 * n_in, (name, len(loc), len(rem), len(fwd))

        def remote(i, s, d, peer):
            return pltpu.make_async_remote_copy(src_ref=s, dst_ref=d, send_sem=send_sems.at[i], recv_sem=recv_sems.at[i],
                                                device_id=peer, device_id_type=MESH)

        copies = [pltpu.make_async_copy(s, d, loc_sems.at[i]) for i, (s, d) in enumerate(loc)]
        rcopies = [remote(i, s, d, peer) for i, (s, d, peer) in enumerate(rem)]
        for cp in copies + rcopies:
            cp.start()
        landed = set()
        fcopies = []
        for i, (s, d, peer, k) in enumerate(fwd):
            if k not in landed:
                rcopies[k].wait_recv()
                landed.add(k)
            fcopies.append(remote(n_rem + i, s, d, peer))
            fcopies[-1].start()
        for k, cp in enumerate(rcopies):
            if k not in landed:
                cp.wait_recv()
        for cp in rcopies + fcopies:
            cp.wait_send()
        for cp in fcopies:
            cp.wait_recv()
        for cp in copies:
            cp.wait()

    return pl.pallas_call(
        body, name=name, in_specs=[ANY] * n_in, out_specs=[ANY] * len(out_shapes), out_shape=out_shapes,
        scratch_shapes=[pltpu.SemaphoreType.DMA((n_rem + n_fwd,)), pltpu.SemaphoreType.DMA((n_rem + n_fwd,)),
                        pltpu.SemaphoreType.DMA((2 * n_in,))],
        compiler_params=pltpu.CompilerParams(has_side_effects=True),
    )(*ins)


def _gather_plan(n_big, in_refs, out_refs, place):
    x, y, c = place
    chip, dev = 2 * x + y, 4 * x + 2 * y + c
    sib = (x, y, 1 - c)
    loc = [(in_refs[0], out_refs[0].at[dev])] + [(s, d.at[chip]) for s, d in zip(in_refs[1 + n_big:], out_refs[1 + n_big:])]
    rem = [(in_refs[0], out_refs[0].at[dev], (_flip(x, dx), _flip(y, dy), _flip(c, dc))) for dx, dy, dc in DEV_PEERS]
    fwd = []
    for s, d in zip(in_refs[1:1 + n_big], out_refs[1:1 + n_big]):
        for dx, dy in CHIP_PEERS:
            px, py = _flip(x, dx), _flip(y, dy)
            fwd.append((d.at[2 * px + py, c], d.at[2 * px + py, c], sib, len(rem)))
            rem.append((s.at[c], d.at[chip, c], (px, py, c)))
    for s, d in zip(in_refs[1 + n_big:], out_refs[1 + n_big:]):
        rem += [(s, d.at[chip], (_flip(x, dx), _flip(y, dy), c)) for dx, dy in CHIP_PEERS]
    return loc, rem, fwd


def _join_plan(in_refs, out_refs, place):
    x, y, c = place
    return [], [(s, d, (x, y, 1 - c)) for s, d in zip(in_refs, out_refs)]


def _scatter_plan(n_all, in_refs, out_refs, place):
    x, y, c = place
    chip, dev = 2 * x + y, 4 * x + 2 * y + c
    loc, rem = [], []
    for s, d in zip(in_refs[:n_all], out_refs[:n_all]):
        loc.append((s.at[dev], d.at[dev]))
        for dx, dy, dc in DEV_PEERS:
            px, py, pc = _flip(x, dx), _flip(y, dy), _flip(c, dc)
            rem.append((s.at[4 * px + 2 * py + pc], d.at[dev], (px, py, pc)))
    for s, d in zip(in_refs[n_all:], out_refs[n_all:]):
        for dx, dy in CHIP_PEERS:
            px, py = _flip(x, dx), _flip(y, dy)
            rem.append((s.at[2 * px + py], d.at[chip], (px, py, c)))
    return loc, rem


def _bshape(a, nb):
    return a.reshape(nb, 1, a.shape[-1])


def _with_prev(cur, before, tiles_per_seq):
    first = pl.program_id(0) % tiles_per_seq == 0
    row0 = jnp.where(first, 0.0, before[before.shape[0] - 1:, :])
    rid = lax.broadcasted_iota(jnp.int32, cur.shape, 0)
    return jnp.where(rid == 0, row0, pltpu.roll(cur, 1, 0))


def _with_next(cur, after, tiles_per_seq):
    last = pl.program_id(0) % tiles_per_seq == tiles_per_seq - 1
    n = cur.shape[0]
    row_n = jnp.where(last, 0.0, after[0:1, :])
    rid = lax.broadcasted_iota(jnp.int32, cur.shape, 0)
    return jnp.where(rid == n - 1, row_n, pltpu.roll(cur, n - 1, 0))


def _local_step(x2d, tgt, mod, wmain, wlora, late_w, ck, w2, a2, small, nb, S, grads_hook):
    T = nb * S
    shift, scale, gate = (_bshape(mod[:, i * D:(i + 1) * D], nb) for i in range(3))
    G = jnp.asarray(np.arange(128)[:, None] == np.arange(D)[None, :] // HN, dtype=BF16)
    cm = jnp.asarray(_chunk_consts())
    ckp = jnp.pad(ck, ((0, 1), (0, 0)))
    zpad = jnp.zeros((64, D), F32)
    w2p = jnp.concatenate([w2, zpad], axis=0)
    a2p = jnp.concatenate([zpad, a2], axis=0)
    mu = small["rwkv_mu"]
    mu_r, mu_k, mu_v, mu_l = mu[:, 0:D], mu[:, D:2 * D], mu[:, 2 * D:3 * D], mu[:, 3 * D:]
    g4 = [mu_r, mu_k, mu_v, mu_l, small["rwkv_w0"], w2p, small["rwkv_a0"], a2p, small["rwkv_k_k"], small["rwkv_k_a"], G]
    g5 = [small["rwkv_gn_g"], small["rwkv_gn_b"], small["rwkv_r_k"], G]
    g3 = [small["conv_b"], small["conv_ln_g"], small["conv_ln_b"]]

    (h,), _, _ = _rows(lambda r, b, g: ([_s1(r[0], g[0], b[0], b[1])], [], []), "pre_fwd", T, S, 256,
                       [(x2d, D, 0)], [scale, shift], [small["norm_g"]], [(D, BF16)], [], [])
    skip = (DMAIN, lambda g, t: pl.multiple_of(g * t + jnp.where(g * t >= 6 * D, LORA, 0), LORA))
    if len(late_w) == 3:
        pm = _matmul(h, wmain, "nt", "proj_main", min(T, 1024), 1024, D, b_rows=skip)
        wco, wro, wo = late_w
    else:
        pm, *landed = _matmul(h, wmain, "nt", "proj_main", min(T, 1024), 1024, D, ride=late_w[0], b_rows=skip)
        wco, wro, wo = late_w[1](landed)
    plo = _matmul(h, wlora, "nt", "proj_lora", 512, LORA, D)
    uc = _conv_fwd(pm, ckp, T, S)
    (uo,), _, _ = _rows(lambda r, b, g: ([_s3(r[0], r[1], *g)], [], []), "conv_post_fwd", T, S, 256,
                        [(uc, D, 0), (pm, D, 2)], [], g3, [(D, BF16)], [], [])
    yc = _matmul(uo, wco, "nn", "conv_out", 512, 1024, D)
    rows4 = [(pm, D, 3), (pm, D, 4), (pm, D, 5), (plo, LORA, 0),
             (pm, D, 3, "prev"), (pm, D, 4, "prev"), (pm, D, 5, "prev"), (plo, LORA, 0, "prev")]
    tps4 = S // 128

    def shifted4(r, tps=tps4):
        return list(r[:4]) + [_with_prev(r[i], r[4 + i], tps) for i in range(4)]

    sc_in, _, _ = _rows(lambda r, b, g: (list(_s4(*shifted4(r, S // 256), *g)), [], []), "rwkv_pre_fwd", T, S, 256,
                        rows4, [], g4, [(D, F32)] * 6, [], [])
    o, hs = _scan_fwd(sc_in, cm, nb, S)
    rows5 = [(o, D, 0), (sc_in[0], D, 0), (sc_in[2], D, 0), (sc_in[3], D, 0), (pm, D, 6)]
    (o2,), _, _ = _rows(lambda r, b, g: ([_s5(*r, *g)], [], []), "rwkv_post_fwd", T, S, 256,
                        rows5, [], g5, [(D, BF16)], [], [])
    yr = _matmul(o2, wro, "nn", "rwkv_out", 512, 1024, D)
    rows6 = [(yc, D, 0), (yr, D, 0), (pm, D, 7), (pm, D, 8)]
    (m,), _, _ = _rows(lambda r, b, g: ([_s6(*r)], [], []), "merge_fwd", T, S, 256, rows6, [], [], [(D, BF16)], [], [])
    out = _matmul(m, wo, "nn", "out_proj", 512, 1024, D)

    def head(r, b, g):
        loss, (dx, dout, dgate, dfg) = jax.value_and_grad(_s7, argnums=(0, 1, 3, 4))(r[0], r[1], r[2], b[0], g[0])
        return [dx, dout], [dgate], [dfg, jnp.full((1, 128), loss, F32)]

    (dx_res, dout), (dgate,), (d_final_g, loss_v) = _rows(
        head, "head", T, S, 256, [(x2d, D, 0), (out, D, 0), (tgt, D, 0)], [gate], [small["final_g"]],
        [(D, F32), (D, BF16)], [D], [(1, D), (1, 128)])

    d_wo = _matmul(m, dout, "tn", "d_w_out", 512, 1024, T)
    dm = _matmul(dout, wo, "nt", "d_merge", 512, 1024, D)

    def merge_bwd(r, b, g):
        _, vjp = jax.vjp(_s6, *r[:4])
        dyc, dyr, dgc, dgr = vjp(r[4])
        return [dyc, dyr, dgc, dgr], [], []

    (dyc, dyr, dgc, dgr), _, _ = _rows(merge_bwd, "merge_bwd", T, S, 256, rows6 + [(dm, D, 0)], [], [],
                                       [(D, BF16), (D, BF16), (D, BF16), (D, BF16)], [], [])
    d_wco = _matmul(uo, dyc, "tn", "d_w_conv_out", 512, 1024, T)
    d_wro = _matmul(o2, dyr, "tn", "d_w_rwkv_out", 512, 1024, T)
    duo = _matmul(dyc, wco, "nt", "d_conv_act", 512, 1024, D)
    do2 = _matmul(dyr, wro, "nt", "d_rwkv_act", 512, 1024, D)

    def conv_post_bwd(r, b, g):
        _, vjp = jax.vjp(_s3, r[0], r[1], *g)
        duc, dog, dcb, dlg, dlb = vjp(r[2])
        return [duc, dog], [], [dcb, dlg, dlb]

    (duc, dcog), _, (d_cb, d_lg, d_lb) = _rows(conv_post_bwd, "conv_post_bwd", T, S, 256,
                                               [(uc, D, 0), (pm, D, 2), (duo, D, 0)], [], g3,
                                               [(D, F32), (D, BF16)], [], [(1, D)] * 3)
    dval, dgt, d_ckp = _conv_bwd(pm, duc, ckp, T, S)

    def rwkv_post_bwd(r, b, g):
        _, vjp = jax.vjp(lambda *z: _s5(*z, g[3]), *r[:5], *g[:3])
        res = vjp(r[5])
        return list(res[:5]), [], list(res[5:8])

    (do, dr_b, dk_b, dv_b, drog), _, (d_gg, d_gb, d_rk) = _rows(
        rwkv_post_bwd, "rwkv_post_bwd", T, S, 256, rows5 + [(do2, D, 0)], [], g5,
        [(D, F32)] * 4 + [(D, BF16)], [], [(1, D)] * 3)
    dsc = _scan_bwd(sc_in, hs, do, cm, nb, S)

    def rwkv_pre_bwd(r, b, g):
        _, vjp = jax.vjp(lambda *z: _s4(*z, g[10]), *shifted4(r), *g[:10])
        ct = (r[8] + r[14], r[9], r[10] + r[15], r[11] + r[16], r[12], r[13])
        res = vjp(ct)
        return list(res[:8]), [], list(res[8:18])

    rows4b = rows4 + [(a, D, 0) for a in dsc] + [(dr_b, D, 0), (dk_b, D, 0), (dv_b, D, 0)]
    gshapes = [(1, D), (1, D), (1, D), (1, LORA), (1, D), (LORA, D), (1, D), (LORA, D), (1, D), (1, D)]
    dts, _, gts = _rows(rwkv_pre_bwd, "rwkv_pre_bwd", T, S, 128, rows4b, [], g4,
                        [(D, BF16)] * 3 + [(LORA, BF16)] + [(D, BF16)] * 3 + [(LORA, BF16)], [], gshapes)
    dr0, dk0, dv0, dl0, dpr, dpk, dpv, dpl = dts
    d_mu_r, d_mu_k, d_mu_v, d_mu_l, d_w0, d_w2p, d_a0, d_a2p, d_kk, d_ka = gts

    def assemble(r, b, g):
        r = [z.astype(F32) for z in r]
        sh = [_with_next(r[10 + i], r[14 + i], tps4) for i in range(4)]
        main = jnp.concatenate([r[0], r[1], r[2], r[3] + sh[0], r[4] + sh[1], r[5] + sh[2], r[6], r[7], r[8]], axis=1)
        return [main, r[9] + sh[3]], [], []

    rows_a = [(dval, D, 0), (dgt, D, 0), (dcog, D, 0), (dr0, D, 0), (dk0, D, 0), (dv0, D, 0), (drog, D, 0), (dgc, D, 0),
              (dgr, D, 0), (dl0, LORA, 0), (dpr, D, 0), (dpk, D, 0), (dpv, D, 0), (dpl, LORA, 0),
              (dpr, D, 0, "next"), (dpk, D, 0, "next"), (dpv, D, 0, "next"), (dpl, LORA, 0, "next")]
    (dpm, dplo), _, _ = _rows(assemble, "assemble_dp", T, S, 128, rows_a, [], [], [(DMAIN, BF16), (LORA, BF16)], [], [])
    RW = D // NCHIP
    c_i = lax.axis_index("c")
    halved = [g.reshape(NCHIP, 2, RW // 2, D).transpose(1, 0, 2, 3).reshape(2, NCHIP * RW // 2, D) for g in (d_wco, d_wro, d_wo)]
    halved += [g.reshape(-1, NCHIP, 2, RW // 2).transpose(2, 1, 0, 3).reshape(2, -1, RW // 2)
               for g in (d_ckp[:CW], d_w2p[:64], d_a2p[64:])]
    h_keep, h_send = (lax.dynamic_slice_in_dim(h, k * (D // 2), D // 2, axis=1) for k in (c_i, 1 - c_i))
    tk = T
    send = [_matmul(h_send, dpm, "tn", "d_w_main_send", D // 2, 1024, tk, out_t=True)[0],
            _matmul(h_send, dplo, "tn", "d_w_lora_send", D // 2, LORA, tk, out_t=True)[0]]
    keep = [None, _matmul(h_keep, dplo, "tn", "d_w_lora_keep", D // 2, LORA, tk, out_t=True)[0]]

    def to_sibling(in_refs, out_refs, place):
        x, y, c = place
        return [], [(s if i < 2 else s.at[1 - c], d, (x, y, 1 - c)) for i, (s, d) in enumerate(zip(in_refs, out_refs))]

    got_shapes = [jax.ShapeDtypeStruct(t.shape, F32) for t in send] + [jax.ShapeDtypeStruct(t.shape[1:], F32) for t in halved]
    d_w_keep, *got_h = _matmul(h_keep, dpm, "tn", "d_w_main_keep", D // 2, 1024, tk, out_t=True,
                               ride=(send + halved, got_shapes, to_sibling, len(got_shapes)))
    keep[0] = d_w_keep[0]

    def own_half_plus(both, q):
        return [jnp.where(lax.axis_index("c") == 0, both[0], both[1]) + q]

    chip_part = [_ew(own_half_plus, "chip_sum_%d" % (2 + i), [g, got_h[2 + i]], 1, 1024, BF16)[0]
                 for i, g in enumerate(halved)]

    def sum_body(p_ref, q_ref, *rest):
        rest[-1][...] = (p_ref[...] + q_ref[...]).astype(BF16)

    blk = pl.BlockSpec((1024, D // 2), lambda i: (i, 0))
    d_win_h = pl.pallas_call(
        sum_body, name="chip_sum_w_in", grid=(DMAIN // 1024,), in_specs=[blk, blk],
        out_specs=pl.BlockSpec((pl.Element(1024), pl.Element(D // 2)), lambda i: (skip[1](i, 1024), 0)),
        out_shape=jax.ShapeDtypeStruct((DMAIN + LORA, D // 2), BF16), compiler_params=_cparams(("parallel",)),
    )(keep[0], got_h[0])
    lora_blk = pl.BlockSpec((LORA, D // 2), lambda i: (0, 0))
    d_win_h = pl.pallas_call(
        sum_body, name="chip_sum_w_lora", grid=(1,), in_specs=[lora_blk, lora_blk, ANY],
        out_specs=pl.BlockSpec((LORA, D // 2), lambda i: (6 * D // LORA, 0)),
        out_shape=jax.ShapeDtypeStruct((DMAIN + LORA, D // 2), BF16), input_output_aliases={2: 0},
        compiler_params=_cparams(("arbitrary",)),
    )(keep[1], got_h[1], d_win_h)
    chip_part = [d_win_h] + chip_part
    dh_m, *got_big = _matmul(dpm, wmain, "nn", "d_h_main", 512, 1024, 3072, ride=grads_hook(chip_part), b_rows=skip)
    dh_l = _matmul(dplo, wlora, "nn", "d_h_lora", 512, 1024, LORA)

    def pre_bwd(r, b, g):
        _, vjp = jax.vjp(_s1, r[0], g[0], b[0], b[1])
        dx, dg, dscale, dshift = vjp(r[1] + r[2])
        return [dx + r[3]], [dscale, dshift], [dg]

    (gx,), (dscale, dshift), (d_ng,) = _rows(pre_bwd, "pre_bwd", T, S, 256,
                                             [(x2d, D, 0), (dh_m, D, 0), (dh_l, D, 0), (dx_res, D, 0)],
                                             [scale, shift], [small["norm_g"]], [(D, F32)], [D, D], [(1, D)])
    dmod = jnp.concatenate([dshift, dscale, dgate], axis=-1).reshape(nb, 3 * D)
    d_small = {"norm_g": d_ng, "conv_b": d_cb, "conv_ln_g": d_lg, "conv_ln_b": d_lb,
               "rwkv_mu": jnp.concatenate([d_mu_r, d_mu_k, d_mu_v, d_mu_l], axis=1),
               "rwkv_w0": d_w0, "rwkv_a0": d_a0, "rwkv_k_k": d_kk, "rwkv_k_a": d_ka, "rwkv_r_k": d_rk,
               "rwkv_gn_g": d_gg, "rwkv_gn_b": d_gb, "final_g": d_final_g}
    return loss_v[0, 0], gx, dmod, got_big, d_small


def _step(a):
    nb, S, _ = a["x"].shape
    T = nb * S
    x_i, y_i, c_i = _place()
    chip = 2 * x_i + y_i
    w_in_t, m_w_in_t, v_w_in_t = (jnp.transpose(a[p + "w_in"][0]) for p in ("", "m_", "v_"))
    WS = w_in_t.shape[0]
    small_w = {n: a[n].reshape(1, sz) for n, sz in SMALL}

    def halves(t):
        return t.reshape(2, t.shape[0] // 2, t.shape[1])

    g_ins = [a["c"], halves(w_in_t.astype(BF16)), a["conv_k"][0], a["rwkv_w2"][0], a["rwkv_a2"][0]]
    g_out = [jax.ShapeDtypeStruct((NDEV,) + g_ins[0].shape, F32)]
    g_out += [jax.ShapeDtypeStruct((NCHIP,) + t.shape, t.dtype) for t in g_ins[1:]]
    c_all, win_g, ck_g, w2_g, a2_g = _comm_call(
        "gather_weights", g_ins, g_out, functools.partial(_gather_plan, 1), 7 + 3 * 4, 3)
    c_all = c_all.reshape(NDEV * nb, D)
    win_t = lax.dynamic_update_index_in_dim(win_g, g_ins[1], chip, 0).reshape(NCHIP * WS, D)
    late = [a[n][0].astype(BF16) for n in ("w_conv_out", "w_rwkv_out", "w_out")]

    def late_plan(in_refs, out_refs, place):
        x, y, c = place
        return [], [(s, d.at[2 * x + y], (_flip(x, dx), _flip(y, dy), c))
                    for s, d in zip(in_refs, out_refs) for dx, dy in CHIP_PEERS]

    def late_finish(landed):
        return [lax.dynamic_update_index_in_dim(g, own, chip, 0).reshape(D, D) for g, own in zip(landed, late)]

    late_w = ((late, [jax.ShapeDtypeStruct((NCHIP,) + t.shape, BF16) for t in late], late_plan, 9), late_finish)
    wmain = win_t
    wlora = win_t[6 * D:6 * D + LORA]
    ck = jnp.concatenate([ck_g[j] for j in range(NCHIP)], axis=1)
    w2 = jnp.concatenate([w2_g[j] for j in range(NCHIP)], axis=1)
    a2 = jnp.concatenate([a2_g[j] for j in range(NCHIP)], axis=1)

    ada_w = a["ada_w"][0]
    MW = ada_w.shape[1]
    ada_b_loc = lax.dynamic_slice(a["ada_b"], (0, chip * MW), (1, MW))

    def mod_body(c_ref, w_ref, b_ref, o_ref):
        o_ref[...] = _dot(_silu(c_ref[...]), w_ref[...], HI) + b_ref[...]

    modp = pl.pallas_call(mod_body, name="ada_mod", out_shape=jax.ShapeDtypeStruct((NDEV * nb, MW), F32),
                          compiler_params=_cparams())(c_all, ada_w, ada_b_loc)
    (mod_g,) = _comm_call("scatter_mod", [modp.reshape(NDEV, nb, MW)],
                          [jax.ShapeDtypeStruct((NDEV, nb, MW), F32)],
                          functools.partial(_scatter_plan, 1), 7)
    mod = mod_g.reshape(NCHIP, 2, nb, MW)
    mod = mod[:, 0].transpose(1, 0, 2).reshape(nb, NCHIP * MW)

    RW = D // NCHIP
    sh_s = []

    def grads_hook(chip_part):
        sh_s.append(chip_part[0].reshape(NCHIP, WS, D // 2))
        sh_s.extend(t.reshape(NCHIP, RW // 2, D) for t in chip_part[1:4])
        sh_s.extend(t.reshape(NCHIP, -1, RW // 2) for t in chip_part[4:])
        return (sh_s, [jax.ShapeDtypeStruct(t.shape, t.dtype) for t in sh_s], functools.partial(_scatter_plan, 0),
                3 * len(sh_s))

    loss_p, gx, dmod, got_big, d_small = _local_step(
        a["x"].reshape(T, D), a["loss_target"].reshape(T, D), mod, wmain, wlora, late_w, ck, w2, a2, small_w, nb, S,
        grads_hook)
    loss = lax.psum(loss_p, ("x", "y", "c"))

    d_small["ada_b"] = _colsum(dmod)
    small_vec = jnp.concatenate([d_small[n] for n, _ in SMALL], axis=1)
    dmod_s = dmod.reshape(nb, NCHIP, MW).transpose(1, 0, 2)
    dmod_s = jnp.repeat(dmod_s, 2, axis=0)
    small_s = jnp.broadcast_to(small_vec[None], (NDEV, 1, NSMALL))
    got = _comm_call("scatter_small", [dmod_s, small_s], [jax.ShapeDtypeStruct(t.shape, F32) for t in (dmod_s, small_s)],
                     functools.partial(_scatter_plan, 2), 14)
    dmod_all, small_all = got[0].reshape(NDEV * nb, MW), got[1].reshape(NDEV, NSMALL)

    def shard_sum(recv, sent):
        chip_i = 2 * lax.axis_index("x") + lax.axis_index("y")
        s = None
        for j in range(NCHIP):
            t = jnp.where(chip_i == j, sent[j], recv[j]).astype(F32)
            s = t if s is None else s + t
        return [s]

    fin = [_ew(shard_sum, "shard_sum_%d" % i, [t, sh_s[i]], 1, 128)[0] for i, t in enumerate(got_big)]
    oth = _comm_call("join_halves", fin, [jax.ShapeDtypeStruct(t.shape, F32) for t in fin], _join_plan, len(fin))

    outs = {}

    def upd_halves(name, mine, other):
        shp = a[name].shape
        R, W = 2 * mine.shape[0], mine.shape[1]
        tm = 128
        nh = R // 2 // tm

        def body(w_ref, m_ref, v_ref, f_ref, o_ref, g_ref, d_ref, m2_ref, v2_ref):
            g = jnp.where(pl.program_id(0) // nh == lax.axis_index("c"), f_ref[...], o_ref[...])
            g_ref[...] = g
            d_ref[...], m2_ref[...], v2_ref[...] = _adamw(w_ref[...], g, m_ref[...], v_ref[...])

        full = pl.BlockSpec((None, tm, W), lambda i: (0, i, 0))
        half = pl.BlockSpec((tm, W), lambda i: (i % nh, 0))
        assert shp == (1, R, W)
        outs[name] = pl.pallas_call(
            body, name="adamw_" + name, grid=(R // tm,), in_specs=[full] * 3 + [half] * 2, out_specs=[full] * 4,
            out_shape=[jax.ShapeDtypeStruct(shp, F32)] * 4, compiler_params=_cparams(("parallel",)),
        )(*[a[p + name] for p in ("", "m_", "v_")], mine, other)

    for name, f, o in zip(("w_conv_out", "w_rwkv_out", "w_out"), fin[1:4], oth[1:4]):
        upd_halves(name, f, o)

    def upd_col_halves(name, wmv, mine, other, tw):
        R, W = wmv[0].shape

        def body(w_ref, m_ref, v_ref, f_ref, o_ref, g_ref, d_ref, m2_ref, v2_ref):
            first = lax.axis_index("c") == 0
            g = jnp.concatenate([jnp.where(first, f_ref[...], o_ref[...]), jnp.where(first, o_ref[...], f_ref[...])],
                                axis=1)
            g_ref[...] = g
            d_ref[...], m2_ref[...], v2_ref[...] = _adamw(w_ref[...], g, m_ref[...], v_ref[...])

        full = pl.BlockSpec((tw, W), lambda i: (i, 0))
        half = pl.BlockSpec((tw, W // 2), lambda i: (i, 0))
        return pl.pallas_call(
            body, name="adamw_" + name, grid=(R // tw,), in_specs=[full] * 3 + [half] * 2, out_specs=[full] * 4,
            out_shape=[jax.ShapeDtypeStruct((R, W), F32)] * 4, compiler_params=_cparams(("parallel",)),
        )(*wmv, mine, other)

    res = upd_col_halves("w_in", (w_in_t, m_w_in_t, v_w_in_t), fin[0], oth[0], WS // 4)
    outs["w_in"] = [jnp.transpose(r)[None] for r in res]
    for name, f, o in zip(("conv_k", "rwkv_w2", "rwkv_a2"), fin[4:], oth[4:]):
        res = upd_col_halves(name, [a[p + name][0] for p in ("", "m_", "v_")], f, o, f.shape[0])
        outs[name] = [r[None] for r in res]

    def adaw_body(c_ref, dm_ref, w_ref, m_ref, v_ref, g_ref, d_ref, m2_ref, v2_ref):
        g = _dot_tn(_silu(c_ref[...]), dm_ref[...], HI)
        g_ref[...] = g
        d_ref[...], m2_ref[...], v2_ref[...] = _adamw(w_ref[...], g, m_ref[...], v_ref[...])

    res = pl.pallas_call(adaw_body, name="adamw_ada_w", out_shape=[jax.ShapeDtypeStruct((D, MW), F32)] * 4,
                         compiler_params=_cparams())(c_all, dmod_all, ada_w, a["m_ada_w"][0], a["v_ada_w"][0])
    outs["ada_w"] = [r.reshape(a["ada_w"].shape) for r in res]

    def small_body(gs_ref, *refs):
        g_all = _sum_slots(gs_ref[...])
        ins, out_refs = refs[:3 * len(SMALL)], refs[3 * len(SMALL):]
        off = 0
        for i, (_, sz) in enumerate(SMALL):
            w_ref, m_ref, v_ref = ins[3 * i:3 * i + 3]
            g = g_all[:, off:off + sz]
            res4 = (g, *_adamw(w_ref[...], g, m_ref[...], v_ref[...]))
            for r, val in zip(out_refs[4 * i:4 * i + 4], res4):
                r[...] = val
            off += sz

    small_ins = [a[p + n].reshape(1, sz) for n, sz in SMALL for p in ("", "m_", "v_")]
    res = pl.pallas_call(
        small_body, name="adamw_small", compiler_params=_cparams(),
        out_shape=[jax.ShapeDtypeStruct((1, sz), F32) for _, sz in SMALL for _ in range(4)],
    )(small_all.reshape(NDEV, 1, NSMALL), *small_ins)
    for i, (n, _) in enumerate(SMALL):
        outs[n] = [r.reshape(a[n].shape) for r in res[4 * i:4 * i + 4]]

    return (loss, gx.reshape(nb, S, D), *[outs[n][0] for n in WEIGHTS], *[outs[n][1] for n in WEIGHTS],
            *[outs[n][2] for n in WEIGHTS], *[outs[n][3] for n in WEIGHTS])


def _colsum(dmod):
    def body(d_ref, o_ref):
        o_ref[...] = jnp.sum(d_ref[...], axis=0, keepdims=True)
    return pl.pallas_call(body, name="ada_b_rowsum", out_shape=jax.ShapeDtypeStruct((1, dmod.shape[1]), F32),
                          compiler_params=_cparams())(dmod)


def kernel(x, c, ada_w, ada_b, norm_g, w_in, conv_k, conv_b, conv_ln_g, conv_ln_b, w_conv_out, rwkv_mu, rwkv_w0, rwkv_w2, rwkv_a0, rwkv_a2, rwkv_k_k, rwkv_k_a, rwkv_r_k, rwkv_gn_g, rwkv_gn_b, w_rwkv_out, w_out, final_g, loss_target, m_ada_w, m_ada_b, m_norm_g, m_w_in, m_conv_k, m_conv_b, m_conv_ln_g, m_conv_ln_b, m_w_conv_out, m_rwkv_mu, m_rwkv_w0, m_rwkv_w2, m_rwkv_a0, m_rwkv_a2, m_rwkv_k_k, m_rwkv_k_a, m_rwkv_r_k, m_rwkv_gn_g, m_rwkv_gn_b, m_w_rwkv_out, m_w_out, m_final_g, v_ada_w, v_ada_b, v_norm_g, v_w_in, v_conv_k, v_conv_b, v_conv_ln_g, v_conv_ln_b, v_w_conv_out, v_rwkv_mu, v_rwkv_w0, v_rwkv_w2, v_rwkv_a0, v_rwkv_a2, v_rwkv_k_k, v_rwkv_k_a, v_rwkv_r_k, v_rwkv_gn_g, v_rwkv_gn_b, v_w_rwkv_out, v_w_out, v_final_g):
    return _step(dict(locals()))
```

```python
import functools

import numpy as np
import jax
import jax.numpy as jnp
from jax import lax
from jax.experimental import pallas as pl
from jax.experimental.pallas import tpu as pltpu

F32 = jnp.float32
BF16 = jnp.bfloat16
HI = lax.Precision.HIGHEST
MESH = pl.DeviceIdType.MESH
ANY = pl.BlockSpec(memory_space=pl.ANY)

D = 1024
NH = 16
HN = 64
LORA = 128
DMAIN = 9 * D
CH = 64
CW = 31
NCHIP = 4
NDEV = 8
VMEM_LIMIT = 56 * 1024 * 1024

RMS_EPS = 1e-6
LN_EPS = 1e-5
GN_EPS = 64e-5
L2_EPS = 1e-12
ADAM_LR = 0.001
ADAM_B1 = 0.9
ADAM_B2 = 0.999
ADAM_EPS = 1e-08
ADAM_WD = 0.01
ADAM_STEP = 10

SMALL = (("ada_b", 3072), ("norm_g", 1024), ("conv_b", 1024), ("conv_ln_g", 1024), ("conv_ln_b", 1024),
         ("rwkv_mu", 3200), ("rwkv_w0", 1024), ("rwkv_a0", 1024), ("rwkv_k_k", 1024), ("rwkv_k_a", 1024),
         ("rwkv_r_k", 1024), ("rwkv_gn_g", 1024), ("rwkv_gn_b", 1024), ("final_g", 1024))
NSMALL = sum(n for _, n in SMALL)

WEIGHTS = ['ada_w', 'ada_b', 'norm_g', 'w_in', 'conv_k', 'conv_b', 'conv_ln_g', 'conv_ln_b', 'w_conv_out', 'rwkv_mu',
           'rwkv_w0', 'rwkv_w2', 'rwkv_a0', 'rwkv_a2', 'rwkv_k_k', 'rwkv_k_a', 'rwkv_r_k', 'rwkv_gn_g', 'rwkv_gn_b',
           'w_rwkv_out', 'w_out', 'final_g']


def _cparams(sem=None, **kw):
    if sem is not None:
        kw["dimension_semantics"] = sem
    return pltpu.CompilerParams(vmem_limit_bytes=VMEM_LIMIT, **kw)


def _dot(a, b, prec=None):
    return jnp.dot(a, b, preferred_element_type=F32, precision=prec)


def _dot_nt(a, b, prec=None):
    return lax.dot_general(a, b, (((1,), (1,)), ((), ())), preferred_element_type=F32, precision=prec)


def _dot_tn(a, b, prec=None):
    return lax.dot_general(a, b, (((0,), (0,)), ((), ())), preferred_element_type=F32, precision=prec)


def _pdot(f, a, b, p):
    if p == "hi":
        return f(a, b, HI)
    ah, bh = a.astype(BF16), b.astype(BF16)
    if p == "bf":
        return f(ah, bh)
    al, bl = (a - ah.astype(F32)).astype(BF16), (b - bh.astype(F32)).astype(BF16)
    return f(ah, bh) + (f(ah, bl) + f(al, bh))


P_SCORE = "b3"
P_INV = "bf"
P_APPLY = "bf"


def _sigmoid(z):
    return 1.0 / (1.0 + jnp.exp(-z))


def _silu(z):
    return z * _sigmoid(z)


def _matmul(a, b, mode, name, tm, tn, tk, ride=None, out_t=False, b_rows=None):
    if mode == "nn":
        (M, K), N = a.shape, b.shape[1]
        a_spec = pl.BlockSpec((tm, tk), lambda j, i, k: (i, k))
        b_spec = pl.BlockSpec((tk, tn), lambda j, i, k: (k, j))
        if b_rows is not None:
            assert b_rows[0] == K
            b_spec = pl.BlockSpec((pl.Element(tk), pl.Element(tn)), lambda j, i, k: (b_rows[1](k, tk), j * tn))
        f = _dot
    elif mode == "nt":
        (M, K), N = a.shape, b.shape[0]
        a_spec = pl.BlockSpec((tm, tk), lambda j, i, k: (i, k))
        b_spec = pl.BlockSpec((tn, tk), lambda j, i, k: (j, k))
        if b_rows is not None:
            N = b_rows[0]
            b_spec = pl.BlockSpec((pl.Element(tn), pl.Element(tk)), lambda j, i, k: (b_rows[1](j, tn), k * tk))
        f = _dot_nt
    else:
        (K, M), N = a.shape, b.shape[1]
        a_spec = pl.BlockSpec((tk, tm), lambda j, i, k: (k, i))
        b_spec = pl.BlockSpec((tk, tn), lambda j, i, k: (k, j))
        f = _dot_tn
    assert M % tm == 0 and N % tn == 0 and K % tk == 0, (name, M, N, K)

    grid = (N // tn, M // tm, K // tk)
    o_spec = pl.BlockSpec((tm, tn), lambda j, i, k: (i, j))
    o_shape = jax.ShapeDtypeStruct((M, N), F32)

    scratch = []
    if out_t:
        o_spec = pl.BlockSpec((None, tn, tm), lambda j, i, k: (i, j, 0))
        o_shape = jax.ShapeDtypeStruct((M // tm, N, tm), F32)
        scratch = [pltpu.VMEM((tm, tn), F32)]

    def step(a_ref, b_ref, o_ref, *acc):
        acc_ref = acc[0] if out_t else o_ref

        @pl.when(pl.program_id(2) == 0)
        def _():
            acc_ref[...] = jnp.zeros_like(acc_ref)
        acc_ref[...] += f(a_ref[...], b_ref[...])
        if out_t:
            @pl.when(pl.program_id(2) == grid[2] - 1)
            def _():
                o_ref[...] = acc_ref[...].T

    if ride is None:
        return pl.pallas_call(
            step, name=name, grid=grid, in_specs=[a_spec, b_spec], out_specs=o_spec, out_shape=o_shape,
            scratch_shapes=scratch, compiler_params=_cparams(("parallel", "parallel", "arbitrary")),
        )(a, b)

    r_ins, r_shapes, plan, n_rem = ride
    n_ri, n_ro = len(r_ins), len(r_shapes)

    def body(a_ref, b_ref, *rest):
        r_in, o_ref, r_out = rest[:n_ri], rest[n_ri], rest[n_ri + 1:n_ri + 1 + n_ro]
        send_sems, recv_sems, *acc = rest[n_ri + 1 + n_ro:]
        loc, rem = plan(r_in, r_out, _place())
        assert not loc and len(rem) == n_rem, (name, len(loc), len(rem))
        copies = [pltpu.make_async_remote_copy(src_ref=s, dst_ref=d, send_sem=send_sems.at[i], recv_sem=recv_sems.at[i],
                                               device_id=peer, device_id_type=MESH) for i, (s, d, peer) in enumerate(rem)]
        pid = [pl.program_id(ax) for ax in range(3)]

        @pl.when((pid[0] == 0) & (pid[1] == 0) & (pid[2] == 0))
        def _():
            for cp in copies:
                cp.start()

        step(a_ref, b_ref, o_ref, *acc)

        @pl.when((pid[0] == grid[0] - 1) & (pid[1] == grid[1] - 1) & (pid[2] == grid[2] - 1))
        def _():
            for cp in copies:
                cp.wait_send()
            for cp in copies:
                cp.wait_recv()

    return pl.pallas_call(
        body, name=name, grid=grid, in_specs=[a_spec, b_spec] + [ANY] * n_ri, out_specs=[o_spec] + [ANY] * n_ro,
        out_shape=[o_shape] + list(r_shapes),
        scratch_shapes=[pltpu.SemaphoreType.DMA((n_rem,)), pltpu.SemaphoreType.DMA((n_rem,))] + scratch,
        compiler_params=_cparams(("arbitrary", "arbitrary", "arbitrary"), has_side_effects=True),
    )(a, b, *r_ins)


def _rows(fn, name, T, S, tm, rows, bpars, gpars, outs, baccs, gaccs):
    nb = T // S
    tps = S // tm
    n_r, n_b, n_g, n_o, n_ba, n_ga = len(rows), len(bpars), len(gpars), len(outs), len(baccs), len(gaccs)

    def body(*refs):
        r_refs = refs[:n_r]
        b_refs = refs[n_r:n_r + n_b]
        g_refs = refs[n_r + n_b:n_r + n_b + n_g]
        o_refs = refs[n_r + n_b + n_g:n_r + n_b + n_g + n_o]
        ba_refs = refs[n_r + n_b + n_g + n_o:n_r + n_b + n_g + n_o + n_ba]
        ga_refs = refs[n_r + n_b + n_g + n_o + n_ba:]
        i = pl.program_id(0)
        o_vals, ba_vals, ga_vals = fn([r[...] for r in r_refs], [r[...] for r in b_refs], [r[...] for r in g_refs])
        for r, v in zip(o_refs, o_vals):
            r[...] = v.astype(r.dtype)
        if n_ba:
            @pl.when(i % tps == 0)
            def _():
                for r in ba_refs:
                    r[...] = jnp.zeros_like(r)
            for r, v in zip(ba_refs, ba_vals):
                r[...] += v.reshape(r.shape)
        if n_ga:
            @pl.when(i == 0)
            def _():
                for r in ga_refs:
                    r[...] = jnp.zeros_like(r)
            for r, v in zip(ga_refs, ga_vals):
                r[...] += v.reshape(r.shape)

    def row_spec(arr, w, cb, kind="tile"):
        hr = 8 * (4 // arr.dtype.itemsize)
        if kind == "prev":
            return pl.BlockSpec((hr, w), lambda i: (jnp.maximum(i * (tm // hr) - 1, 0), cb))
        if kind == "next":
            return pl.BlockSpec((hr, w), lambda i: (jnp.minimum((i + 1) * (tm // hr), T // hr - 1), cb))
        return pl.BlockSpec((tm, w), lambda i: (i, cb))

    in_specs = [row_spec(*r) for r in rows]
    in_specs += [pl.BlockSpec((None, 1, p.shape[-1]), lambda i: (i // tps, 0, 0)) for p in bpars]
    in_specs += [pl.BlockSpec(p.shape, lambda i: (0, 0)) for p in gpars]
    out_specs = [pl.BlockSpec((tm, w), lambda i: (i, 0)) for w, _ in outs]
    out_specs += [pl.BlockSpec((None, 1, w), lambda i: (i // tps, 0, 0)) for w in baccs]
    out_specs += [pl.BlockSpec(s, lambda i: (0, 0)) for s in gaccs]
    out_shape = [jax.ShapeDtypeStruct((T, w), dt) for w, dt in outs]
    out_shape += [jax.ShapeDtypeStruct((nb, 1, w), F32) for w in baccs]
    out_shape += [jax.ShapeDtypeStruct(s, F32) for s in gaccs]
    res = pl.pallas_call(
        body, name=name, grid=(T // tm,), in_specs=in_specs, out_specs=out_specs, out_shape=out_shape,
        compiler_params=_cparams(("arbitrary",)),
    )(*[r[0] for r in rows], *bpars, *gpars)
    return res[:n_o], res[n_o:n_o + n_ba], res[n_o + n_ba:]


@jax.custom_vjp
def _gsum(z, G):
    zh = z.astype(BF16)
    zl = (z - zh.astype(F32)).astype(BF16)
    r = _dot_nt(zh, G) + _dot_nt(zl, G)
    rh = r.astype(BF16)
    rl = (r - rh.astype(F32)).astype(BF16)
    return _dot(rh, G) + _dot(rl, G)


def _dot3(x, w):
    xh = x.astype(BF16).astype(F32)
    wh = w.astype(BF16).astype(F32)
    xc = jnp.concatenate([xh, xh, x - xh], axis=1).astype(BF16)
    wc = jnp.concatenate([wh, w - wh, wh], axis=0).astype(BF16)
    return _dot(xc, wc)


_gsum.defvjp(lambda z, G: (_gsum(z, G), G), lambda G, ct: (_gsum(ct, G), jnp.zeros_like(G)))


def _s1(x, g, scale, shift):
    y = x * lax.rsqrt(jnp.mean(x * x, axis=-1, keepdims=True) + RMS_EPS)
    return (y * g) * (1.0 + scale) + shift


def _s3(uc, og, cb, lg, lb):
    u = uc + cb
    mu = jnp.mean(u, axis=-1, keepdims=True)
    d = u - mu
    var = jnp.mean(d * d, axis=-1, keepdims=True)
    y = d * lax.rsqrt(var + LN_EPS) * lg + lb
    return _silu(y) * _silu(og)


def _s4(r0, k0, v0, l0, pr, pk, pv, plo, mu_r, mu_k, mu_v, mu_l, w0, w2p, a0, a2p, k_k, k_a, G):
    r = r0 + mu_r * (pr - r0)
    k = k0 + mu_k * (pk - k0)
    v = v0 + mu_v * (pv - v0)
    lo = l0 + mu_l * (plo - l0)
    w_pre = w0 + _dot3(jnp.tanh(lo), w2p)
    lw = -np.float32(np.exp(-0.5)) * _sigmoid(w_pre)
    a = _sigmoid(a0 + _dot3(lo, a2p))
    kkr = k * k_k
    ss = _gsum(kkr * kkr, G)
    kk = kkr / jnp.maximum(jnp.sqrt(ss), L2_EPS)
    k2 = k * (1.0 + (a - 1.0) * k_a)
    return r, lw, k2, v, kk, kk * a


def _s5(o, r, k2, v, og, gg, gb, rk, G):
    mu = _gsum(o, G) * (1.0 / HN)
    d = o - mu
    var = _gsum(d * d, G) * (1.0 / HN)
    y = d * lax.rsqrt(var + GN_EPS) * gg + gb
    bonus = _gsum(r * k2 * rk, G)
    return (y + bonus * v) * _silu(og)


def _s6(yc, yr, gc, gr):
    return _sigmoid(gc) * yc + _sigmoid(gr) * yr


def _s7(x, out, tgt, gate, fg):
    x2 = x + gate * out
    y = x2 * lax.rsqrt(jnp.mean(x2 * x2, axis=-1, keepdims=True) + RMS_EPS) * fg
    e = y - tgt
    return 0.5 * jnp.sum(jnp.mean(e * e, axis=-1))


def _solve_all_fwd(a_kbs, rhss, cm):
    H = range(len(a_kbs))
    xi = [cm[2] - cm[3] * a_kbs[j] for j in H]
    for lvl in range(1, 6):
        t = [_pdot(_dot, xi[j], cm[3 + lvl] * a_kbs[j], P_INV) for j in H]
        xi = [xi[j] - _pdot(_dot, t[j], xi[j], P_INV) for j in H]
    u = tuple(_pdot(_dot, xi[j], rhss[j], P_APPLY) for j in H)
    return u, (xi, u, cm)


def _solve_all_bwd(res, dus):
    xi, u, cm = res
    H = range(len(u))
    g = tuple(_pdot(_dot_tn, xi[j], dus[j], P_APPLY) for j in H)
    da = tuple(-(cm[1] * _pdot(_dot_nt, g[j], u[j], P_APPLY)) for j in H)
    return da, g, jnp.zeros_like(cm)


@jax.custom_vjp
def _solve_all(a_kbs, rhss, cm):
    return _solve_all_fwd(a_kbs, rhss, cm)[0]


_solve_all.defvjp(_solve_all_fwd, _solve_all_bwd)


def _chunk(sts, r, lw, k, v, kk, b, cm):
    cum = _dot(cm[0], lw, HI)
    ein = jnp.exp(-cum)
    rt = r * jnp.exp(cum)
    kkt = kk * jnp.exp(cum - lw)
    kh = k * ein
    bh = b * ein
    ec = jnp.exp(jnp.sum(lw, axis=0, keepdims=True))
    khe = kh * ec
    bhe = bh * ec
    H = range(len(sts))
    tri, strict, eye = cm[0], cm[1], cm[2]
    rt, kkt, kh, bh, v, khe, bhe, ec = ([a[:, j * HN:(j + 1) * HN] for j in H] for a in (rt, kkt, kh, bh, v, khe, bhe, ec))
    lhs = [jnp.concatenate([kkt[j], rt[j]], axis=0) for j in H]
    rhs_s = [jnp.concatenate([bh[j], kh[j]], axis=0) for j in H]
    lh = [a.astype(BF16).astype(F32) for a in lhs]
    rh = [a.astype(BF16).astype(F32) for a in rhs_s]
    lc = [jnp.concatenate([lh[j], lh[j], lhs[j] - lh[j]], axis=1).astype(BF16) for j in H]
    rc = [jnp.concatenate([rh[j], rhs_s[j] - rh[j], rh[j]], axis=1).astype(BF16) for j in H]
    sc = [_dot_nt(lc[j], rc[j]) for j in H]
    a_kb = [strict * sc[j][:CH, :CH] for j in H]
    a_kk = [strict * sc[j][:CH, CH:] for j in H]
    a_rb = [tri * sc[j][CH:, :CH] for j in H]
    a_rk = [tri * sc[j][CH:, CH:] for j in H]
    ps = [_dot_nt(lhs[j].astype(BF16), sts[j].astype(BF16)) for j in H]
    pv = [_dot(jnp.concatenate([a_kk[j], a_rk[j]], axis=0).astype(BF16), v[j].astype(BF16)) for j in H]
    rhs = [ps[j][:CH] + pv[j][:CH] for j in H]
    o0 = [ps[j][CH:] + pv[j][CH:] for j in H]
    u = _solve_all(tuple(a_kb), tuple(rhs), cm)
    o = [o0[j] - _pdot(_dot, a_rb[j], u[j], P_APPLY) for j in H]
    st2 = [sts[j] * ec[j] + _dot_tn(jnp.concatenate([v[j], u[j]], axis=0).astype(BF16),
                                    jnp.concatenate([khe[j], -bhe[j]], axis=0).astype(BF16)) for j in H]
    return jnp.concatenate(o, axis=1), tuple(st2)


def _chunk_consts():
    t = np.arange(CH)[:, None]
    s = np.arange(CH)[None, :]
    mats = [(t >= s), (t > s), (t == s)]
    for lvl in range(6):
        sz = 1 << lvl
        mats.append(((t // sz) % 2 == 1) & ((s // sz) == (t // sz) - 1))
    mats.append(np.zeros((CH, CH), bool))
    return np.stack(mats).astype(np.float32)


def _adamw(w, g, m, v):
    m = ADAM_B1 * m + (1.0 - ADAM_B1) * g
    v = ADAM_B2 * v + (1.0 - ADAM_B2) * (g * g)
    m_hat = m / (1.0 - ADAM_B1 ** ADAM_STEP)
    v_hat = v / (1.0 - ADAM_B2 ** ADAM_STEP)
    delta = -ADAM_LR * (m_hat / (jnp.sqrt(v_hat) + ADAM_EPS) + ADAM_WD * w)
    return delta, m, v


CT = 128
RB = 64
WIN = RB + 32


def _conv_fwd(pm, ck, T, S):
    nb = T // S

    def body(val_ref, gate_ref, ck_ref, out_ref, ubuf):
        ubuf[0:32, :] = jnp.zeros((32, CT), F32)
        ubuf[32:, :] = val_ref[...] * _sigmoid(gate_ref[...])

        def blk(rb, carry):
            base = pl.multiple_of(rb * RB, RB)
            win = ubuf[pl.ds(base, WIN), :]
            acc = jnp.zeros((RB, CT), F32)
            for j in range(CW):
                acc = acc + ck_ref[j:j + 1, :] * pltpu.roll(win, (WIN - (2 + j)) % WIN, 0)[0:RB, :]
            out_ref[pl.ds(base, RB), :] = acc
            return carry

        lax.fori_loop(0, S // RB, blk, 0)

    return pl.pallas_call(
        body, name="conv_fwd", grid=(D // CT, nb),
        in_specs=[pl.BlockSpec((S, CT), lambda ct, b: (b, ct)),
                  pl.BlockSpec((S, CT), lambda ct, b: (b, D // CT + ct)),
                  pl.BlockSpec((32, CT), lambda ct, b: (0, ct))],
        out_specs=pl.BlockSpec((S, CT), lambda ct, b: (b, ct)),
        out_shape=jax.ShapeDtypeStruct((T, D), F32),
        scratch_shapes=[pltpu.VMEM((S + 32, CT), F32)],
        compiler_params=_cparams(("parallel", "arbitrary")),
    )(pm, pm, ck)


def _conv_bwd(pm, duc, ck, T, S):
    nb = T // S

    def body(val_ref, gate_ref, duc_ref, ck_ref, dval_ref, dgate_ref, dck_ref, ubuf, dbuf, acc):
        b = pl.program_id(1)
        ubuf[0:32, :] = jnp.zeros((32, CT), F32)
        ubuf[32:, :] = val_ref[...] * _sigmoid(gate_ref[...])
        dbuf[0:S, :] = duc_ref[...]
        dbuf[S:, :] = jnp.zeros((32, CT), F32)
        acc[...] = jnp.zeros_like(acc)

        def blk(rb, carry):
            base = pl.multiple_of(rb * RB, RB)
            uwin = ubuf[pl.ds(base, WIN), :]
            dwin = dbuf[pl.ds(base, WIN), :]
            dblk = dwin[0:RB, :]
            du = jnp.zeros((RB, CT), F32)
            for j in range(CW):
                du = du + ck_ref[j:j + 1, :] * pltpu.roll(dwin, (WIN - (CW - 1 - j)) % WIN, 0)[0:RB, :]
                ush = pltpu.roll(uwin, (WIN - (2 + j)) % WIN, 0)[0:RB, :]
                acc[j] += jnp.sum((dblk * ush).reshape(RB // 8, 8, CT), axis=0)
            val = val_ref[pl.ds(base, RB), :]
            sg = _sigmoid(gate_ref[pl.ds(base, RB), :])
            dval_ref[pl.ds(base, RB), :] = (du * sg).astype(BF16)
            dgate_ref[pl.ds(base, RB), :] = (du * val * sg * (1.0 - sg)).astype(BF16)
            return carry

        lax.fori_loop(0, S // RB, blk, 0)

        @pl.when(b == 0)
        def _():
            dck_ref[...] = jnp.zeros_like(dck_ref)
        for j in range(CW):
            dck_ref[j:j + 1, :] += jnp.sum(acc[j], axis=0, keepdims=True)

    return pl.pallas_call(
        body, name="conv_bwd", grid=(D // CT, nb),
        in_specs=[pl.BlockSpec((S, CT), lambda ct, b: (b, ct)),
                  pl.BlockSpec((S, CT), lambda ct, b: (b, D // CT + ct)),
                  pl.BlockSpec((S, CT), lambda ct, b: (b, ct)),
                  pl.BlockSpec((32, CT), lambda ct, b: (0, ct))],
        out_specs=[pl.BlockSpec((S, CT), lambda ct, b: (b, ct)),
                   pl.BlockSpec((S, CT), lambda ct, b: (b, ct)),
                   pl.BlockSpec((32, CT), lambda ct, b: (0, ct))],
        out_shape=[jax.ShapeDtypeStruct((T, D), BF16), jax.ShapeDtypeStruct((T, D), BF16),
                   jax.ShapeDtypeStruct((32, D), F32)],
        scratch_shapes=[pltpu.VMEM((S + 32, CT), F32), pltpu.VMEM((S + 32, CT), F32), pltpu.VMEM((32, 8, CT), F32)],
        compiler_params=_cparams(("parallel", "arbitrary")),
    )(pm, pm, duc, ck)


def _scan_fwd(ins, cm, nb, S):
    nc = S // CH
    nch = nb * NH
    blk = pl.BlockSpec((nb, CH, D), lambda i: (0, i, 0))
    hblk = pl.BlockSpec((nb, NH, None, HN, HN), lambda i: (0, 0, i, 0, 0))

    def body(r_ref, lw_ref, k_ref, v_ref, kk_ref, b_ref, cm_ref, o_ref, hs_ref, st):
        @pl.when(pl.program_id(0) == 0)
        def _():
            st[...] = jnp.zeros_like(st)
        s0 = [st[j] for j in range(nch)]
        for j in range(nch):
            hs_ref[j // NH, j % NH] = s0[j]
        vals = [jnp.concatenate([ref[q] for q in range(nb)], axis=1) for ref in (r_ref, lw_ref, k_ref, v_ref, kk_ref, b_ref)]
        o, s1 = _chunk(s0, *vals, cm_ref[...])
        for q in range(nb):
            o_ref[q] = o[:, q * D:(q + 1) * D]
        for j in range(nch):
            st[j] = s1[j]

    o, hs = pl.pallas_call(
        body, name="scan_fwd", grid=(nc,),
        in_specs=[blk] * 6 + [pl.BlockSpec(cm.shape, lambda i: (0, 0, 0))],
        out_specs=[blk, hblk],
        out_shape=[jax.ShapeDtypeStruct((nb, S, D), F32), jax.ShapeDtypeStruct((nb, NH, nc, HN, HN), F32)],
        scratch_shapes=[pltpu.VMEM((nch, HN, HN), F32)],
        compiler_params=_cparams(("arbitrary",)),
    )(*[a.reshape(nb, S, D) for a in ins], cm)
    return o.reshape(nb * S, D), hs


def _scan_bwd(ins, hs, do, cm, nb, S):
    nc = S // CH
    nch = nb * NH
    blk = pl.BlockSpec((nb, CH, D), lambda i: (0, nc - 1 - i, 0))
    hblk = pl.BlockSpec((nb, NH, None, HN, HN), lambda i: (0, 0, nc - 1 - i, 0, 0))

    def body(r_ref, lw_ref, k_ref, v_ref, kk_ref, b_ref, hs_ref, do_ref, cm_ref,
             dr_ref, dlw_ref, dk_ref, dv_ref, dkk_ref, db_ref, dst):
        @pl.when(pl.program_id(0) == 0)
        def _():
            dst[...] = jnp.zeros_like(dst)
        cmv = cm_ref[...]
        side = lambda ref: jnp.concatenate([ref[q] for q in range(nb)], axis=1)
        f = lambda s0, r, lw, k, v, kk, b: _chunk(s0, r, lw, k, v, kk, b, cmv)
        _, vjp = jax.vjp(f, [hs_ref[j // NH, j % NH] for j in range(nch)],
                         *[side(ref) for ref in (r_ref, lw_ref, k_ref, v_ref, kk_ref, b_ref)])
        ds0, *grads = vjp((side(do_ref), tuple(dst[j] for j in range(nch))))
        for j in range(nch):
            dst[j] = ds0[j]
        for ref, g in zip((dr_ref, dlw_ref, dk_ref, dv_ref, dkk_ref, db_ref), grads):
            for q in range(nb):
                ref[q] = g[:, q * D:(q + 1) * D]

    outs = pl.pallas_call(
        body, name="scan_bwd", grid=(nc,),
        in_specs=[blk] * 6 + [hblk, blk, pl.BlockSpec(cm.shape, lambda i: (0, 0, 0))],
        out_specs=[blk] * 6,
        out_shape=[jax.ShapeDtypeStruct((nb, S, D), F32)] * 6,
        scratch_shapes=[pltpu.VMEM((nch, HN, HN), F32)],
        compiler_params=_cparams(("arbitrary",)),
    )(*[a.reshape(nb, S, D) for a in ins], hs, do.reshape(nb, S, D), cm)
    return [a.reshape(nb * S, D) for a in outs]


def _ew(fn, name, ins, n_out, tm, out_dtype=F32):
    R, W = ins[0].shape[-2:]
    tm = min(tm, R)
    if R % tm:
        tm = R // 2
    assert R % tm == 0 and (tm % 16 == 0 or tm == R), (name, R, tm)

    def body(*refs):
        vals = fn(*[r[...] for r in refs[:len(ins)]])
        for r, v in zip(refs[len(ins):], vals):
            r[...] = v.astype(r.dtype)

    def spec(a):
        if a.ndim == 3:
            return pl.BlockSpec((a.shape[0], tm, W), lambda i: (0, i, 0))
        return pl.BlockSpec((tm, W), lambda i: (i, 0))

    return pl.pallas_call(
        body, name=name, grid=(R // tm,), in_specs=[spec(a) for a in ins],
        out_specs=[pl.BlockSpec((tm, W), lambda i: (i, 0))] * n_out,
        out_shape=[jax.ShapeDtypeStruct((R, W), out_dtype)] * n_out,
        compiler_params=_cparams(("parallel",)),
    )(*ins)


def _sum_slots(r):
    s = r[0]
    for j in range(1, r.shape[0]):
        s = s + r[j]
    return s


def _place():
    x, y, c = lax.axis_index("x"), lax.axis_index("y"), lax.axis_index("c")
    return x, y, c


def _flip(v, d):
    return 1 - v if d else v


CHIP_PEERS = ((1, 0), (0, 1), (1, 1))
DEV_PEERS = tuple((dx, dy, dc) for dx in (0, 1) for dy in (0, 1) for dc in (0, 1))[1:]


def _comm_call(name, ins, out_shapes, plan, n_rem, n_fwd=0):
    n_in = len(ins)

    def body(*refs):
        in_refs, out_refs = refs[:n_in], refs[n_in:n_in + len(out_shapes)]
        send_sems, recv_sems, loc_sems = refs[n_in + len(out_shapes):]
        loc, rem, *rest = plan(in_refs, out_refs, _place())
        fwd = rest[0] if rest else []
        assert len(rem) == n_rem and len(fwd) == n_fwd and len(loc) <= 2 * n_in, (name, len(loc), len(rem), len(fwd))

        def remote(i, s, d, peer):
            return pltpu.make_async_remote_copy(src_ref=s, dst_ref=d, send_sem=send_sems.at[i], recv_sem=recv_sems.at[i],
                                                device_id=peer, device_id_type=MESH)

        copies = [pltpu.make_async_copy(s, d, loc_sems.at[i]) for i, (s, d) in enumerate(loc)]
        rcopies = [remote(i, s, d, peer) for i, (s, d, peer) in enumerate(rem)]
        for cp in copies + rcopies:
            cp.start()
        landed = set()
        fcopies = []
        for i, (s, d, peer, k) in enumerate(fwd):
            if k not in landed:
                rcopies[k].wait_recv()
                landed.add(k)
            fcopies.append(remote(n_rem + i, s, d, peer))
            fcopies[-1].start()
        for k, cp in enumerate(rcopies):
            if k not in landed:
                cp.wait_recv()
        for cp in rcopies + fcopies:
            cp.wait_send()
        for cp in fcopies:
            cp.wait_recv()
        for cp in copies:
            cp.wait()

    return pl.pallas_call(
        body, name=name, in_specs=[ANY] * n_in, out_specs=[ANY] * len(out_shapes), out_shape=out_shapes,
        scratch_shapes=[pltpu.SemaphoreType.DMA((n_rem + n_fwd,)), pltpu.SemaphoreType.DMA((n_rem + n_fwd,)),
                        pltpu.SemaphoreType.DMA((2 * n_in,))],
        compiler_params=pltpu.CompilerParams(has_side_effects=True),
    )(*ins)


def _gather_plan(n_big, in_refs, out_refs, place):
    x, y, c = place
    chip, dev = 2 * x + y, 4 * x + 2 * y + c
    sib = (x, y, 1 - c)
    loc = [(in_refs[0], out_refs[0].at[dev])] + [(s, d.at[chip]) for s, d in zip(in_refs[1 + n_big:], out_refs[1 + n_big:])]
    rem = [(in_refs[0], out_refs[0].at[dev], (_flip(x, dx), _flip(y, dy), _flip(c, dc))) for dx, dy, dc in DEV_PEERS]
    fwd = []
    for s, d in zip(in_refs[1:1 + n_big], out_refs[1:1 + n_big]):
        for dx, dy in CHIP_PEERS:
            px, py = _flip(x, dx), _flip(y, dy)
            fwd.append((d.at[2 * px + py, c], d.at[2 * px + py, c], sib, len(rem)))
            rem.append((s.at[c], d.at[chip, c], (px, py, c)))
    for s, d in zip(in_refs[1 + n_big:], out_refs[1 + n_big:]):
        rem += [(s, d.at[chip], (_flip(x, dx), _flip(y, dy), c)) for dx, dy in CHIP_PEERS]
    return loc, rem, fwd


def _join_plan(in_refs, out_refs, place):
    x, y, c = place
    return [], [(s, d, (x, y, 1 - c)) for s, d in zip(in_refs, out_refs)]


def _scatter_plan(n_all, in_refs, out_refs, place):
    x, y, c = place
    chip, dev = 2 * x + y, 4 * x + 2 * y + c
    loc, rem = [], []
    for s, d in zip(in_refs[:n_all], out_refs[:n_all]):
        loc.append((s.at[dev], d.at[dev]))
        for dx, dy, dc in DEV_PEERS:
            px, py, pc = _flip(x, dx), _flip(y, dy), _flip(c, dc)
            rem.append((s.at[4 * px + 2 * py + pc], d.at[dev], (px, py, pc)))
    for s, d in zip(in_refs[n_all:], out_refs[n_all:]):
        for dx, dy in CHIP_PEERS:
            px, py = _flip(x, dx), _flip(y, dy)
            rem.append((s.at[2 * px + py], d.at[chip], (px, py, c)))
    return loc, rem


def _bshape(a, nb):
    return a.reshape(nb, 1, a.shape[-1])


def _with_prev(cur, before, tiles_per_seq):
    first = pl.program_id(0) % tiles_per_seq == 0
    row0 = jnp.where(first, 0.0, before[before.shape[0] - 1:, :])
    rid = lax.broadcasted_iota(jnp.int32, cur.shape, 0)
    return jnp.where(rid == 0, row0, pltpu.roll(cur, 1, 0))


def _with_next(cur, after, tiles_per_seq):
    last = pl.program_id(0) % tiles_per_seq == tiles_per_seq - 1
    n = cur.shape[0]
    row_n = jnp.where(last, 0.0, after[0:1, :])
    rid = lax.broadcasted_iota(jnp.int32, cur.shape, 0)
    return jnp.where(rid == n - 1, row_n, pltpu.roll(cur, n - 1, 0))


def _local_step(x2d, tgt, mod, wmain, wlora, late_w, ck, w2, a2, small, nb, S, grads_hook):
    T = nb * S
    shift, scale, gate = (_bshape(mod[:, i * D:(i + 1) * D], nb) for i in range(3))
    G = jnp.asarray(np.arange(128)[:, None] == np.arange(D)[None, :] // HN, dtype=BF16)
    cm = jnp.asarray(_chunk_consts())
    ckp = jnp.pad(ck, ((0, 1), (0, 0)))
    zpad = jnp.zeros((64, D), F32)
    w2p = jnp.concatenate([w2, zpad], axis=0)
    a2p = jnp.concatenate([zpad, a2], axis=0)
    mu = small["rwkv_mu"]
    mu_r, mu_k, mu_v, mu_l = mu[:, 0:D], mu[:, D:2 * D], mu[:, 2 * D:3 * D], mu[:, 3 * D:]
    g4 = [mu_r, mu_k, mu_v, mu_l, small["rwkv_w0"], w2p, small["rwkv_a0"], a2p, small["rwkv_k_k"], small["rwkv_k_a"], G]
    g5 = [small["rwkv_gn_g"], small["rwkv_gn_b"], small["rwkv_r_k"], G]
    g3 = [small["conv_b"], small["conv_ln_g"], small["conv_ln_b"]]

    (h,), _, _ = _rows(lambda r, b, g: ([_s1(r[0], g[0], b[0], b[1])], [], []), "pre_fwd", T, S, 256,
                       [(x2d, D, 0)], [scale, shift], [small["norm_g"]], [(D, BF16)], [], [])
    skip = (DMAIN, lambda g, t: pl.multiple_of(g * t + jnp.where(g * t >= 6 * D, LORA, 0), LORA))
    if len(late_w) == 3:
        pm = _matmul(h, wmain, "nt", "proj_main", min(T, 1024), 1024, D, b_rows=skip)
        wco, wro, wo = late_w
    else:
        pm, *landed = _matmul(h, wmain, "nt", "proj_main", min(T, 1024), 1024, D, ride=late_w[0], b_rows=skip)
        wco, wro, wo = late_w[1](landed)
    plo = _matmul(h, wlora, "nt", "proj_lora", 512, LORA, D)
    uc = _conv_fwd(pm, ckp, T, S)
    (uo,), _, _ = _rows(lambda r, b, g: ([_s3(r[0], r[1], *g)], [], []), "conv_post_fwd", T, S, 256,
                        [(uc, D, 0), (pm, D, 2)], [], g3, [(D, BF16)], [], [])
    yc = _matmul(uo, wco, "nn", "conv_out", min(T, 1024), 1024, D)
    rows4 = [(pm, D, 3), (pm, D, 4), (pm, D, 5), (plo, LORA, 0),
             (pm, D, 3, "prev"), (pm, D, 4, "prev"), (pm, D, 5, "prev"), (plo, LORA, 0, "prev")]
    tps4 = S // 128

    def shifted4(r, tps=tps4):
        return list(r[:4]) + [_with_prev(r[i], r[4 + i], tps) for i in range(4)]

    sc_in, _, _ = _rows(lambda r, b, g: (list(_s4(*shifted4(r, S // 256), *g)), [], []), "rwkv_pre_fwd", T, S, 256,
                        rows4, [], g4, [(D, F32)] * 6, [], [])
    o, hs = _scan_fwd(sc_in, cm, nb, S)
    rows5 = [(o, D, 0), (sc_in[0], D, 0), (sc_in[2], D, 0), (sc_in[3], D, 0), (pm, D, 6)]
    (o2,), _, _ = _rows(lambda r, b, g: ([_s5(*r, *g)], [], []), "rwkv_post_fwd", T, S, 256,
                        rows5, [], g5, [(D, BF16)], [], [])
    yr = _matmul(o2, wro, "nn", "rwkv_out", min(T, 1024), 1024, D)
    rows6 = [(yc, D, 0), (yr, D, 0), (pm, D, 7), (pm, D, 8)]
    (m,), _, _ = _rows(lambda r, b, g: ([_s6(*r)], [], []), "merge_fwd", T, S, 256, rows6, [], [], [(D, BF16)], [], [])
    out = _matmul(m, wo, "nn", "out_proj", min(T, 1024), 1024, D)

    def head(r, b, g):
        loss, (dx, dout, dgate, dfg) = jax.value_and_grad(_s7, argnums=(0, 1, 3, 4))(r[0], r[1], r[2], b[0], g[0])
        return [dx, dout], [dgate], [dfg, jnp.full((1, 128), loss, F32)]

    (dx_res, dout), (dgate,), (d_final_g, loss_v) = _rows(
        head, "head", T, S, 256, [(x2d, D, 0), (out, D, 0), (tgt, D, 0)], [gate], [small["final_g"]],
        [(D, F32), (D, BF16)], [D], [(1, D), (1, 128)])

    d_wo = _matmul(m, dout, "tn", "d_w_out", 512, 1024, T)
    dm = _matmul(dout, wo, "nt", "d_merge", min(T, 1024), 1024, D)

    def merge_bwd(r, b, g):
        _, vjp = jax.vjp(_s6, *r[:4])
        dyc, dyr, dgc, dgr = vjp(r[4])
        return [dyc, dyr, dgc, dgr], [], []

    (dyc, dyr, dgc, dgr), _, _ = _rows(merge_bwd, "merge_bwd", T, S, 256, rows6 + [(dm, D, 0)], [], [],
                                       [(D, BF16), (D, BF16), (D, BF16), (D, BF16)], [], [])
    d_wco = _matmul(uo, dyc, "tn", "d_w_conv_out", 512, 1024, T)
    d_wro = _matmul(o2, dyr, "tn", "d_w_rwkv_out", 512, 1024, T)
    duo = _matmul(dyc, wco, "nt", "d_conv_act", min(T, 1024), 1024, D)
    do2 = _matmul(dyr, wro, "nt", "d_rwkv_act", min(T, 1024), 1024, D)

    def conv_post_bwd(r, b, g):
        _, vjp = jax.vjp(_s3, r[0], r[1], *g)
        duc, dog, dcb, dlg, dlb = vjp(r[2])
        return [duc, dog], [], [dcb, dlg, dlb]

    (duc, dcog), _, (d_cb, d_lg, d_lb) = _rows(conv_post_bwd, "conv_post_bwd", T, S, 256,
                                               [(uc, D, 0), (pm, D, 2), (duo, D, 0)], [], g3,
                                               [(D, F32), (D, BF16)], [], [(1, D)] * 3)
    dval, dgt, d_ckp = _conv_bwd(pm, duc, ckp, T, S)

    def rwkv_post_bwd(r, b, g):
        _, vjp = jax.vjp(lambda *z: _s5(*z, g[3]), *r[:5], *g[:3])
        res = vjp(r[5])
        return list(res[:5]), [], list(res[5:8])

    (do, dr_b, dk_b, dv_b, drog), _, (d_gg, d_gb, d_rk) = _rows(
        rwkv_post_bwd, "rwkv_post_bwd", T, S, 256, rows5 + [(do2, D, 0)], [], g5,
        [(D, F32)] * 4 + [(D, BF16)], [], [(1, D)] * 3)
    dsc = _scan_bwd(sc_in, hs, do, cm, nb, S)

    def rwkv_pre_bwd(r, b, g):
        _, vjp = jax.vjp(lambda *z: _s4(*z, g[10]), *shifted4(r), *g[:10])
        ct = (r[8] + r[14], r[9], r[10] + r[15], r[11] + r[16], r[12], r[13])
        res = vjp(ct)
        return list(res[:8]), [], list(res[8:18])

    rows4b = rows4 + [(a, D, 0) for a in dsc] + [(dr_b, D, 0), (dk_b, D, 0), (dv_b, D, 0)]
    gshapes = [(1, D), (1, D), (1, D), (1, LORA), (1, D), (LORA, D), (1, D), (LORA, D), (1, D), (1, D)]
    dts, _, gts = _rows(rwkv_pre_bwd, "rwkv_pre_bwd", T, S, 128, rows4b, [], g4,
                        [(D, BF16)] * 3 + [(LORA, BF16)] + [(D, BF16)] * 3 + [(LORA, BF16)], [], gshapes)
    dr0, dk0, dv0, dl0, dpr, dpk, dpv, dpl = dts
    d_mu_r, d_mu_k, d_mu_v, d_mu_l, d_w0, d_w2p, d_a0, d_a2p, d_kk, d_ka = gts

    def assemble(r, b, g):
        r = [z.astype(F32) for z in r]
        sh = [_with_next(r[10 + i], r[14 + i], tps4) for i in range(4)]
        main = jnp.concatenate([r[0], r[1], r[2], r[3] + sh[0], r[4] + sh[1], r[5] + sh[2], r[6], r[7], r[8]], axis=1)
        return [main, r[9] + sh[3]], [], []

    rows_a = [(dval, D, 0), (dgt, D, 0), (dcog, D, 0), (dr0, D, 0), (dk0, D, 0), (dv0, D, 0), (drog, D, 0), (dgc, D, 0),
              (dgr, D, 0), (dl0, LORA, 0), (dpr, D, 0), (dpk, D, 0), (dpv, D, 0), (dpl, LORA, 0),
              (dpr, D, 0, "next"), (dpk, D, 0, "next"), (dpv, D, 0, "next"), (dpl, LORA, 0, "next")]
    (dpm, dplo), _, _ = _rows(assemble, "assemble_dp", T, S, 128, rows_a, [], [], [(DMAIN, BF16), (LORA, BF16)], [], [])
    RW = D // NCHIP
    c_i = lax.axis_index("c")
    halved = [g.reshape(NCHIP, 2, RW // 2, D).transpose(1, 0, 2, 3).reshape(2, NCHIP * RW // 2, D) for g in (d_wco, d_wro, d_wo)]
    halved += [g.reshape(-1, NCHIP, 2, RW // 2).transpose(2, 1, 0, 3).reshape(2, -1, RW // 2)
               for g in (d_ckp[:CW], d_w2p[:64], d_a2p[64:])]
    h_keep, h_send = (lax.dynamic_slice_in_dim(h, k * (D // 2), D // 2, axis=1) for k in (c_i, 1 - c_i))
    tk = T
    send = [_matmul(h_send, dpm, "tn", "d_w_main_send", D // 2, 1024, tk, out_t=True)[0],
            _matmul(h_send, dplo, "tn", "d_w_lora_send", D // 2, LORA, tk, out_t=True)[0]]
    keep = [None, _matmul(h_keep, dplo, "tn", "d_w_lora_keep", D // 2, LORA, tk, out_t=True)[0]]

    def to_sibling(in_refs, out_refs, place):
        x, y, c = place
        return [], [(s if i < 2 else s.at[1 - c], d, (x, y, 1 - c)) for i, (s, d) in enumerate(zip(in_refs, out_refs))]

    got_shapes = [jax.ShapeDtypeStruct(t.shape, F32) for t in send] + [jax.ShapeDtypeStruct(t.shape[1:], F32) for t in halved]
    d_w_keep, *got_h = _matmul(h_keep, dpm, "tn", "d_w_main_keep", D // 2, 1024, tk, out_t=True,
                               ride=(send + halved, got_shapes, to_sibling, len(got_shapes)))
    keep[0] = d_w_keep[0]

    def own_half_plus(both, q):
        return [jnp.where(lax.axis_index("c") == 0, both[0], both[1]) + q]

    chip_part = [_ew(own_half_plus, "chip_sum_%d" % (2 + i), [g, got_h[2 + i]], 1, 1024, BF16)[0]
                 for i, g in enumerate(halved)]

    def sum_body(p_ref, q_ref, *rest):
        rest[-1][...] = (p_ref[...] + q_ref[...]).astype(BF16)

    blk = pl.BlockSpec((1024, D // 2), lambda i: (i, 0))
    d_win_h = pl.pallas_call(
        sum_body, name="chip_sum_w_in", grid=(DMAIN // 1024,), in_specs=[blk, blk],
        out_specs=pl.BlockSpec((pl.Element(1024), pl.Element(D // 2)), lambda i: (skip[1](i, 1024), 0)),
        out_shape=jax.ShapeDtypeStruct((DMAIN + LORA, D // 2), BF16), compiler_params=_cparams(("parallel",)),
    )(keep[0], got_h[0])
    lora_blk = pl.BlockSpec((LORA, D // 2), lambda i: (0, 0))
    d_win_h = pl.pallas_call(
        sum_body, name="chip_sum_w_lora", grid=(1,), in_specs=[lora_blk, lora_blk, ANY],
        out_specs=pl.BlockSpec((LORA, D // 2), lambda i: (6 * D // LORA, 0)),
        out_shape=jax.ShapeDtypeStruct((DMAIN + LORA, D // 2), BF16), input_output_aliases={2: 0},
        compiler_params=_cparams(("arbitrary",)),
    )(keep[1], got_h[1], d_win_h)
    chip_part = [d_win_h] + chip_part
    dh_m, *got_big = _matmul(dpm, wmain, "nn", "d_h_main", 512, 1024, 3072, ride=grads_hook(chip_part), b_rows=skip)
    dh_l = _matmul(dplo, wlora, "nn", "d_h_lora", 512, 1024, LORA)

    def pre_bwd(r, b, g):
        _, vjp = jax.vjp(_s1, r[0], g[0], b[0], b[1])
        dx, dg, dscale, dshift = vjp(r[1] + r[2])
        return [dx + r[3]], [dscale, dshift], [dg]

    (gx,), (dscale, dshift), (d_ng,) = _rows(pre_bwd, "pre_bwd", T, S, 256,
                                             [(x2d, D, 0), (dh_m, D, 0), (dh_l, D, 0), (dx_res, D, 0)],
                                             [scale, shift], [small["norm_g"]], [(D, F32)], [D, D], [(1, D)])
    dmod = jnp.concatenate([dshift, dscale, dgate], axis=-1).reshape(nb, 3 * D)
    d_small = {"norm_g": d_ng, "conv_b": d_cb, "conv_ln_g": d_lg, "conv_ln_b": d_lb,
               "rwkv_mu": jnp.concatenate([d_mu_r, d_mu_k, d_mu_v, d_mu_l], axis=1),
               "rwkv_w0": d_w0, "rwkv_a0": d_a0, "rwkv_k_k": d_kk, "rwkv_k_a": d_ka, "rwkv_r_k": d_rk,
               "rwkv_gn_g": d_gg, "rwkv_gn_b": d_gb, "final_g": d_final_g}
    return loss_v[0, 0], gx, dmod, got_big, d_small


def _step(a):
    nb, S, _ = a["x"].shape
    T = nb * S
    x_i, y_i, c_i = _place()
    chip = 2 * x_i + y_i
    w_in_t, m_w_in_t, v_w_in_t = (jnp.transpose(a[p + "w_in"][0]) for p in ("", "m_", "v_"))
    WS = w_in_t.shape[0]
    small_w = {n: a[n].reshape(1, sz) for n, sz in SMALL}

    def halves(t):
        return t.reshape(2, t.shape[0] // 2, t.shape[1])

    g_ins = [a["c"], halves(w_in_t.astype(BF16)), a["conv_k"][0], a["rwkv_w2"][0], a["rwkv_a2"][0]]
    g_out = [jax.ShapeDtypeStruct((NDEV,) + g_ins[0].shape, F32)]
    g_out += [jax.ShapeDtypeStruct((NCHIP,) + t.shape, t.dtype) for t in g_ins[1:]]
    c_all, win_g, ck_g, w2_g, a2_g = _comm_call(
        "gather_weights", g_ins, g_out, functools.partial(_gather_plan, 1), 7 + 3 * 4, 3)
    c_all = c_all.reshape(NDEV * nb, D)
    win_t = lax.dynamic_update_index_in_dim(win_g, g_ins[1], chip, 0).reshape(NCHIP * WS, D)
    late = [a[n][0].astype(BF16) for n in ("w_conv_out", "w_rwkv_out", "w_out")]

    def late_plan(in_refs, out_refs, place):
        x, y, c = place
        return [], [(s, d.at[2 * x + y], (_flip(x, dx), _flip(y, dy), c))
                    for s, d in zip(in_refs, out_refs) for dx, dy in CHIP_PEERS]

    def late_finish(landed):
        return [lax.dynamic_update_index_in_dim(g, own, chip, 0).reshape(D, D) for g, own in zip(landed, late)]

    late_w = ((late, [jax.ShapeDtypeStruct((NCHIP,) + t.shape, BF16) for t in late], late_plan, 9), late_finish)
    wmain = win_t
    wlora = win_t[6 * D:6 * D + LORA]
    ck = jnp.concatenate([ck_g[j] for j in range(NCHIP)], axis=1)
    w2 = jnp.concatenate([w2_g[j] for j in range(NCHIP)], axis=1)
    a2 = jnp.concatenate([a2_g[j] for j in range(NCHIP)], axis=1)

    ada_w = a["ada_w"][0]
    MW = ada_w.shape[1]
    ada_b_loc = lax.dynamic_slice(a["ada_b"], (0, chip * MW), (1, MW))

    def mod_body(c_ref, w_ref, b_ref, o_ref):
        o_ref[...] = _dot(_silu(c_ref[...]), w_ref[...], HI) + b_ref[...]

    modp = pl.pallas_call(mod_body, name="ada_mod", out_shape=jax.ShapeDtypeStruct((NDEV * nb, MW), F32),
                          compiler_params=_cparams())(c_all, ada_w, ada_b_loc)
    (mod_g,) = _comm_call("scatter_mod", [modp.reshape(NDEV, nb, MW)],
                          [jax.ShapeDtypeStruct((NDEV, nb, MW), F32)],
                          functools.partial(_scatter_plan, 1), 7)
    mod = mod_g.reshape(NCHIP, 2, nb, MW)
    mod = mod[:, 0].transpose(1, 0, 2).reshape(nb, NCHIP * MW)

    RW = D // NCHIP
    sh_s = []

    def grads_hook(chip_part):
        sh_s.append(chip_part[0].reshape(NCHIP, WS, D // 2))
        sh_s.extend(t.reshape(NCHIP, RW // 2, D) for t in chip_part[1:4])
        sh_s.extend(t.reshape(NCHIP, -1, RW // 2) for t in chip_part[4:])
        return (sh_s, [jax.ShapeDtypeStruct(t.shape, t.dtype) for t in sh_s], functools.partial(_scatter_plan, 0),
                3 * len(sh_s))

    loss_p, gx, dmod, got_big, d_small = _local_step(
        a["x"].reshape(T, D), a["loss_target"].reshape(T, D), mod, wmain, wlora, late_w, ck, w2, a2, small_w, nb, S,
        grads_hook)
    loss = lax.psum(loss_p, ("x", "y", "c"))

    d_small["ada_b"] = _colsum(dmod)
    small_vec = jnp.concatenate([d_small[n] for n, _ in SMALL], axis=1)
    dmod_s = dmod.reshape(nb, NCHIP, MW).transpose(1, 0, 2)
    dmod_s = jnp.repeat(dmod_s, 2, axis=0)
    small_s = jnp.broadcast_to(small_vec[None], (NDEV, 1, NSMALL))
    got = _comm_call("scatter_small", [dmod_s, small_s], [jax.ShapeDtypeStruct(t.shape, F32) for t in (dmod_s, small_s)],
                     functools.partial(_scatter_plan, 2), 14)
    dmod_all, small_all = got[0].reshape(NDEV * nb, MW), got[1].reshape(NDEV, NSMALL)

    def shard_sum(recv, sent):
        chip_i = 2 * lax.axis_index("x") + lax.axis_index("y")
        s = None
        for j in range(NCHIP):
            t = jnp.where(chip_i == j, sent[j], recv[j]).astype(F32)
            s = t if s is None else s + t
        return [s]

    fin = [_ew(shard_sum, "shard_sum_%d" % i, [t, sh_s[i]], 1, 128)[0] for i, t in enumerate(got_big)]
    oth = _comm_call("join_halves", fin, [jax.ShapeDtypeStruct(t.shape, F32) for t in fin], _join_plan, len(fin))

    outs = {}

    def upd_halves(name, mine, other):
        shp = a[name].shape
        R, W = 2 * mine.shape[0], mine.shape[1]
        tm = 128
        nh = R // 2 // tm

        def body(w_ref, m_ref, v_ref, f_ref, o_ref, g_ref, d_ref, m2_ref, v2_ref):
            g = jnp.where(pl.program_id(0) // nh == lax.axis_index("c"), f_ref[...], o_ref[...])
            g_ref[...] = g
            d_ref[...], m2_ref[...], v2_ref[...] = _adamw(w_ref[...], g, m_ref[...], v_ref[...])

        full = pl.BlockSpec((None, tm, W), lambda i: (0, i, 0))
        half = pl.BlockSpec((tm, W), lambda i: (i % nh, 0))
        assert shp == (1, R, W)
        outs[name] = pl.pallas_call(
            body, name="adamw_" + name, grid=(R // tm,), in_specs=[full] * 3 + [half] * 2, out_specs=[full] * 4,
            out_shape=[jax.ShapeDtypeStruct(shp, F32)] * 4, compiler_params=_cparams(("parallel",)),
        )(*[a[p + name] for p in ("", "m_", "v_")], mine, other)

    for name, f, o in zip(("w_conv_out", "w_rwkv_out", "w_out"), fin[1:4], oth[1:4]):
        upd_halves(name, f, o)

    def upd_col_halves(name, wmv, mine, other, tw):
        R, W = wmv[0].shape

        def body(w_ref, m_ref, v_ref, f_ref, o_ref, g_ref, d_ref, m2_ref, v2_ref):
            first = lax.axis_index("c") == 0
            g = jnp.concatenate([jnp.where(first, f_ref[...], o_ref[...]), jnp.where(first, o_ref[...], f_ref[...])],
                                axis=1)
            g_ref[...] = g
            d_ref[...], m2_ref[...], v2_ref[...] = _adamw(w_ref[...], g, m_ref[...], v_ref[...])

        full = pl.BlockSpec((tw, W), lambda i: (i, 0))
        half = pl.BlockSpec((tw, W // 2), lambda i: (i, 0))
        return pl.pallas_call(
            body, name="adamw_" + name, grid=(R // tw,), in_specs=[full] * 3 + [half] * 2, out_specs=[full] * 4,
            out_shape=[jax.ShapeDtypeStruct((R, W), F32)] * 4, compiler_params=_cparams(("parallel",)),
        )(*wmv, mine, other)

    res = upd_col_halves("w_in", (w_in_t, m_w_in_t, v_w_in_t), fin[0], oth[0], WS // 4)
    outs["w_in"] = [jnp.transpose(r)[None] for r in res]
    for name, f, o in zip(("conv_k", "rwkv_w2", "rwkv_a2"), fin[4:], oth[4:]):
        res = upd_col_halves(name, [a[p + name][0] for p in ("", "m_", "v_")], f, o, f.shape[0])
        outs[name] = [r[None] for r in res]

    def adaw_body(c_ref, dm_ref, w_ref, m_ref, v_ref, g_ref, d_ref, m2_ref, v2_ref):
        g = _dot_tn(_silu(c_ref[...]), dm_ref[...], HI)
        g_ref[...] = g
        d_ref[...], m2_ref[...], v2_ref[...] = _adamw(w_ref[...], g, m_ref[...], v_ref[...])

    res = pl.pallas_call(adaw_body, name="adamw_ada_w", out_shape=[jax.ShapeDtypeStruct((D, MW), F32)] * 4,
                         compiler_params=_cparams())(c_all, dmod_all, ada_w, a["m_ada_w"][0], a["v_ada_w"][0])
    outs["ada_w"] = [r.reshape(a["ada_w"].shape) for r in res]

    def small_body(gs_ref, *refs):
        g_all = _sum_slots(gs_ref[...])
        ins, out_refs = refs[:3 * len(SMALL)], refs[3 * len(SMALL):]
        off = 0
        for i, (_, sz) in enumerate(SMALL):
            w_ref, m_ref, v_ref = ins[3 * i:3 * i + 3]
            g = g_all[:, off:off + sz]
            res4 = (g, *_adamw(w_ref[...], g, m_ref[...], v_ref[...]))
            for r, val in zip(out_refs[4 * i:4 * i + 4], res4):
                r[...] = val
            off += sz

    small_ins = [a[p + n].reshape(1, sz) for n, sz in SMALL for p in ("", "m_", "v_")]
    res = pl.pallas_call(
        small_body, name="adamw_small", compiler_params=_cparams(),
        out_shape=[jax.ShapeDtypeStruct((1, sz), F32) for _, sz in SMALL for _ in range(4)],
    )(small_all.reshape(NDEV, 1, NSMALL), *small_ins)
    for i, (n, _) in enumerate(SMALL):
        outs[n] = [r.reshape(a[n].shape) for r in res[4 * i:4 * i + 4]]

    return (loss, gx.reshape(nb, S, D), *[outs[n][0] for n in WEIGHTS], *[outs[n][1] for n in WEIGHTS],
            *[outs[n][2] for n in WEIGHTS], *[outs[n][3] for n in WEIGHTS])


def _colsum(dmod):
    def body(d_ref, o_ref):
        o_ref[...] = jnp.sum(d_ref[...], axis=0, keepdims=True)
    return pl.pallas_call(body, name="ada_b_rowsum", out_shape=jax.ShapeDtypeStruct((1, dmod.shape[1]), F32),
                          compiler_params=_cparams())(dmod)


def kernel(x, c, ada_w, ada_b, norm_g, w_in, conv_k, conv_b, conv_ln_g, conv_ln_b, w_conv_out, rwkv_mu, rwkv_w0, rwkv_w2, rwkv_a0, rwkv_a2, rwkv_k_k, rwkv_k_a, rwkv_r_k, rwkv_gn_g, rwkv_gn_b, w_rwkv_out, w_out, final_g, loss_target, m_ada_w, m_ada_b, m_norm_g, m_w_in, m_conv_k, m_conv_b, m_conv_ln_g, m_conv_ln_b, m_w_conv_out, m_rwkv_mu, m_rwkv_w0, m_rwkv_w2, m_rwkv_a0, m_rwkv_a2, m_rwkv_k_k, m_rwkv_k_a, m_rwkv_r_k, m_rwkv_gn_g, m_rwkv_gn_b, m_w_rwkv_out, m_w_out, m_final_g, v_ada_w, v_ada_b, v_norm_g, v_w_in, v_conv_k, v_conv_b, v_conv_ln_g, v_conv_ln_b, v_w_conv_out, v_rwkv_mu, v_rwkv_w0, v_rwkv_w2, v_rwkv_a0, v_rwkv_a2, v_rwkv_k_k, v_rwkv_k_a, v_rwkv_r_k, v_rwkv_gn_g, v_rwkv_gn_b, v_w_rwkv_out, v_w_out, v_final_g):
    return _step(dict(locals()))
```

```python
import functools

import numpy as np
import jax
import jax.numpy as jnp
from jax import lax
from jax.experimental import pallas as pl
from jax.experimental.pallas import tpu as pltpu

F32 = jnp.float32
BF16 = jnp.bfloat16
HI = lax.Precision.HIGHEST
MESH = pl.DeviceIdType.MESH
ANY = pl.BlockSpec(memory_space=pl.ANY)

D = 1024
NH = 16
HN = 64
LORA = 128
DMAIN = 9 * D
CH = 64
CW = 31
NCHIP = 4
NDEV = 8
VMEM_LIMIT = 56 * 1024 * 1024

RMS_EPS = 1e-6
LN_EPS = 1e-5
GN_EPS = 64e-5
L2_EPS = 1e-12
ADAM_LR = 0.001
ADAM_B1 = 0.9
ADAM_B2 = 0.999
ADAM_EPS = 1e-08
ADAM_WD = 0.01
ADAM_STEP = 10

SMALL = (("ada_b", 3072), ("norm_g", 1024), ("conv_b", 1024), ("conv_ln_g", 1024), ("conv_ln_b", 1024),
         ("rwkv_mu", 3200), ("rwkv_w0", 1024), ("rwkv_a0", 1024), ("rwkv_k_k", 1024), ("rwkv_k_a", 1024),
         ("rwkv_r_k", 1024), ("rwkv_gn_g", 1024), ("rwkv_gn_b", 1024), ("final_g", 1024))
NSMALL = sum(n for _, n in SMALL)

WEIGHTS = ['ada_w', 'ada_b', 'norm_g', 'w_in', 'conv_k', 'conv_b', 'conv_ln_g', 'conv_ln_b', 'w_conv_out', 'rwkv_mu',
           'rwkv_w0', 'rwkv_w2', 'rwkv_a0', 'rwkv_a2', 'rwkv_k_k', 'rwkv_k_a', 'rwkv_r_k', 'rwkv_gn_g', 'rwkv_gn_b',
           'w_rwkv_out', 'w_out', 'final_g']


def _cparams(sem=None, **kw):
    if sem is not None:
        kw["dimension_semantics"] = sem
    return pltpu.CompilerParams(vmem_limit_bytes=VMEM_LIMIT, **kw)


def _dot(a, b, prec=None):
    return jnp.dot(a, b, preferred_element_type=F32, precision=prec)


def _dot_nt(a, b, prec=None):
    return lax.dot_general(a, b, (((1,), (1,)), ((), ())), preferred_element_type=F32, precision=prec)


def _dot_tn(a, b, prec=None):
    return lax.dot_general(a, b, (((0,), (0,)), ((), ())), preferred_element_type=F32, precision=prec)


def _pdot(f, a, b, p):
    if p == "hi":
        return f(a, b, HI)
    ah, bh = a.astype(BF16), b.astype(BF16)
    if p == "bf":
        return f(ah, bh)
    al, bl = (a - ah.astype(F32)).astype(BF16), (b - bh.astype(F32)).astype(BF16)
    return f(ah, bh) + (f(ah, bl) + f(al, bh))


P_SCORE = "b3"
P_INV = "bf"
P_APPLY = "bf"


def _sigmoid(z):
    return 1.0 / (1.0 + jnp.exp(-z))


def _silu(z):
    return z * _sigmoid(z)


def _matmul(a, b, mode, name, tm, tn, tk, ride=None, out_t=False, b_rows=None):
    if mode == "nn":
        (M, K), N = a.shape, b.shape[1]
        a_spec = pl.BlockSpec((tm, tk), lambda j, i, k: (i, k))
        b_spec = pl.BlockSpec((tk, tn), lambda j, i, k: (k, j))
        if b_rows is not None:
            assert b_rows[0] == K
            b_spec = pl.BlockSpec((pl.Element(tk), pl.Element(tn)), lambda j, i, k: (b_rows[1](k, tk), j * tn))
        f = _dot
    elif mode == "nt":
        (M, K), N = a.shape, b.shape[0]
        a_spec = pl.BlockSpec((tm, tk), lambda j, i, k: (i, k))
        b_spec = pl.BlockSpec((tn, tk), lambda j, i, k: (j, k))
        if b_rows is not None:
            N = b_rows[0]
            b_spec = pl.BlockSpec((pl.Element(tn), pl.Element(tk)), lambda j, i, k: (b_rows[1](j, tn), k * tk))
        f = _dot_nt
    else:
        (K, M), N = a.shape, b.shape[1]
        a_spec = pl.BlockSpec((tk, tm), lambda j, i, k: (k, i))
        b_spec = pl.BlockSpec((tk, tn), lambda j, i, k: (k, j))
        f = _dot_tn
    assert M % tm == 0 and N % tn == 0 and K % tk == 0, (name, M, N, K)

    grid = (N // tn, M // tm, K // tk)
    o_spec = pl.BlockSpec((tm, tn), lambda j, i, k: (i, j))
    o_shape = jax.ShapeDtypeStruct((M, N), F32)

    scratch = []
    if out_t:
        o_spec = pl.BlockSpec((None, tn, tm), lambda j, i, k: (i, j, 0))
        o_shape = jax.ShapeDtypeStruct((M // tm, N, tm), F32)
        scratch = [pltpu.VMEM((tm, tn), F32)]

    def step(a_ref, b_ref, o_ref, *acc):
        acc_ref = acc[0] if out_t else o_ref

        @pl.when(pl.program_id(2) == 0)
        def _():
            acc_ref[...] = jnp.zeros_like(acc_ref)
        acc_ref[...] += f(a_ref[...], b_ref[...])
        if out_t:
            @pl.when(pl.program_id(2) == grid[2] - 1)
            def _():
                o_ref[...] = acc_ref[...].T

    if ride is None:
        return pl.pallas_call(
            step, name=name, grid=grid, in_specs=[a_spec, b_spec], out_specs=o_spec, out_shape=o_shape,
            scratch_shapes=scratch, compiler_params=_cparams(("parallel", "parallel", "arbitrary")),
        )(a, b)

    r_ins, r_shapes, plan, n_rem = ride
    n_ri, n_ro = len(r_ins), len(r_shapes)

    def body(a_ref, b_ref, *rest):
        r_in, o_ref, r_out = rest[:n_ri], rest[n_ri], rest[n_ri + 1:n_ri + 1 + n_ro]
        send_sems, recv_sems, *acc = rest[n_ri + 1 + n_ro:]
        loc, rem = plan(r_in, r_out, _place())
        assert not loc and len(rem) == n_rem, (name, len(loc), len(rem))
        copies = [pltpu.make_async_remote_copy(src_ref=s, dst_ref=d, send_sem=send_sems.at[i], recv_sem=recv_sems.at[i],
                                               device_id=peer, device_id_type=MESH) for i, (s, d, peer) in enumerate(rem)]
        pid = [pl.program_id(ax) for ax in range(3)]

        @pl.when((pid[0] == 0) & (pid[1] == 0) & (pid[2] == 0))
        def _():
            for cp in copies:
                cp.start()

        step(a_ref, b_ref, o_ref, *acc)

        @pl.when((pid[0] == grid[0] - 1) & (pid[1] == grid[1] - 1) & (pid[2] == grid[2] - 1))
        def _():
            for cp in copies:
                cp.wait_send()
            for cp in copies:
                cp.wait_recv()

    return pl.pallas_call(
        body, name=name, grid=grid, in_specs=[a_spec, b_spec] + [ANY] * n_ri, out_specs=[o_spec] + [ANY] * n_ro,
        out_shape=[o_shape] + list(r_shapes),
        scratch_shapes=[pltpu.SemaphoreType.DMA((n_rem,)), pltpu.SemaphoreType.DMA((n_rem,))] + scratch,
        compiler_params=_cparams(("arbitrary", "arbitrary", "arbitrary"), has_side_effects=True),
    )(a, b, *r_ins)


def _rows(fn, name, T, S, tm, rows, bpars, gpars, outs, baccs, gaccs):
    nb = T // S
    tps = S // tm
    n_r, n_b, n_g, n_o, n_ba, n_ga = len(rows), len(bpars), len(gpars), len(outs), len(baccs), len(gaccs)

    def body(*refs):
        r_refs = refs[:n_r]
        b_refs = refs[n_r:n_r + n_b]
        g_refs = refs[n_r + n_b:n_r + n_b + n_g]
        o_refs = refs[n_r + n_b + n_g:n_r + n_b + n_g + n_o]
        ba_refs = refs[n_r + n_b + n_g + n_o:n_r + n_b + n_g + n_o + n_ba]
        ga_refs = refs[n_r + n_b + n_g + n_o + n_ba:]
        i = pl.program_id(0)
        o_vals, ba_vals, ga_vals = fn([r[...] for r in r_refs], [r[...] for r in b_refs], [r[...] for r in g_refs])
        for r, v in zip(o_refs, o_vals):
            r[...] = v.astype(r.dtype)
        if n_ba:
            @pl.when(i % tps == 0)
            def _():
                for r in ba_refs:
                    r[...] = jnp.zeros_like(r)
            for r, v in zip(ba_refs, ba_vals):
                r[...] += v.reshape(r.shape)
        if n_ga:
            @pl.when(i == 0)
            def _():
                for r in ga_refs:
                    r[...] = jnp.zeros_like(r)
            for r, v in zip(ga_refs, ga_vals):
                r[...] += v.reshape(r.shape)

    def row_spec(arr, w, cb, kind="tile"):
        hr = 8 * (4 // arr.dtype.itemsize)
        if kind == "prev":
            return pl.BlockSpec((hr, w), lambda i: (jnp.maximum(i * (tm // hr) - 1, 0), cb))
        if kind == "next":
            return pl.BlockSpec((hr, w), lambda i: (jnp.minimum((i + 1) * (tm // hr), T // hr - 1), cb))
        return pl.BlockSpec((tm, w), lambda i: (i, cb))

    in_specs = [row_spec(*r) for r in rows]
    in_specs += [pl.BlockSpec((None, 1, p.shape[-1]), lambda i: (i // tps, 0, 0)) for p in bpars]
    in_specs += [pl.BlockSpec(p.shape, lambda i: (0, 0)) for p in gpars]
    out_specs = [pl.BlockSpec((tm, w), lambda i: (i, 0)) for w, _ in outs]
    out_specs += [pl.BlockSpec((None, 1, w), lambda i: (i // tps, 0, 0)) for w in baccs]
    out_specs += [pl.BlockSpec(s, lambda i: (0, 0)) for s in gaccs]
    out_shape = [jax.ShapeDtypeStruct((T, w), dt) for w, dt in outs]
    out_shape += [jax.ShapeDtypeStruct((nb, 1, w), F32) for w in baccs]
    out_shape += [jax.ShapeDtypeStruct(s, F32) for s in gaccs]
    res = pl.pallas_call(
        body, name=name, grid=(T // tm,), in_specs=in_specs, out_specs=out_specs, out_shape=out_shape,
        compiler_params=_cparams(("arbitrary",)),
    )(*[r[0] for r in rows], *bpars, *gpars)
    return res[:n_o], res[n_o:n_o + n_ba], res[n_o + n_ba:]


@jax.custom_vjp
def _gsum(z, G):
    zh = z.astype(BF16)
    zl = (z - zh.astype(F32)).astype(BF16)
    r = _dot_nt(zh, G) + _dot_nt(zl, G)
    rh = r.astype(BF16)
    rl = (r - rh.astype(F32)).astype(BF16)
    return _dot(rh, G) + _dot(rl, G)


def _dot3(x, w):
    xh = x.astype(BF16).astype(F32)
    wh = w.astype(BF16).astype(F32)
    xc = jnp.concatenate([xh, xh, x - xh], axis=1).astype(BF16)
    wc = jnp.concatenate([wh, w - wh, wh], axis=0).astype(BF16)
    return _dot(xc, wc)


_gsum.defvjp(lambda z, G: (_gsum(z, G), G), lambda G, ct: (_gsum(ct, G), jnp.zeros_like(G)))


def _s1(x, g, scale, shift):
    y = x * lax.rsqrt(jnp.mean(x * x, axis=-1, keepdims=True) + RMS_EPS)
    return (y * g) * (1.0 + scale) + shift


def _s3(uc, og, cb, lg, lb):
    u = uc + cb
    mu = jnp.mean(u, axis=-1, keepdims=True)
    d = u - mu
    var = jnp.mean(d * d, axis=-1, keepdims=True)
    y = d * lax.rsqrt(var + LN_EPS) * lg + lb
    return _silu(y) * _silu(og)


def _s4(r0, k0, v0, l0, pr, pk, pv, plo, mu_r, mu_k, mu_v, mu_l, w0, w2p, a0, a2p, k_k, k_a, G):
    r = r0 + mu_r * (pr - r0)
    k = k0 + mu_k * (pk - k0)
    v = v0 + mu_v * (pv - v0)
    lo = l0 + mu_l * (plo - l0)
    w_pre = w0 + _dot3(jnp.tanh(lo), w2p)
    lw = -np.float32(np.exp(-0.5)) * _sigmoid(w_pre)
    a = _sigmoid(a0 + _dot3(lo, a2p))
    kkr = k * k_k
    ss = _gsum(kkr * kkr, G)
    kk = kkr / jnp.maximum(jnp.sqrt(ss), L2_EPS)
    k2 = k * (1.0 + (a - 1.0) * k_a)
    return r, lw, k2, v, kk, kk * a


def _s5(o, r, k2, v, og, gg, gb, rk, G):
    mu = _gsum(o, G) * (1.0 / HN)
    d = o - mu
    var = _gsum(d * d, G) * (1.0 / HN)
    y = d * lax.rsqrt(var + GN_EPS) * gg + gb
    bonus = _gsum(r * k2 * rk, G)
    return (y + bonus * v) * _silu(og)


def _s6(yc, yr, gc, gr):
    return _sigmoid(gc) * yc + _sigmoid(gr) * yr


def _s7(x, out, tgt, gate, fg):
    x2 = x + gate * out
    y = x2 * lax.rsqrt(jnp.mean(x2 * x2, axis=-1, keepdims=True) + RMS_EPS) * fg
    e = y - tgt
    return 0.5 * jnp.sum(jnp.mean(e * e, axis=-1))


def _solve_all_fwd(a_kbs, rhss, cm):
    H = range(len(a_kbs))
    xi = [cm[2] - cm[3] * a_kbs[j] for j in H]
    for lvl in range(1, 6):
        t = [_pdot(_dot, xi[j], cm[3 + lvl] * a_kbs[j], P_INV) for j in H]
        xi = [xi[j] - _pdot(_dot, t[j], xi[j], P_INV) for j in H]
    u = tuple(_pdot(_dot, xi[j], rhss[j], P_APPLY) for j in H)
    return u, (xi, u, cm)


def _solve_all_bwd(res, dus):
    xi, u, cm = res
    H = range(len(u))
    g = tuple(_pdot(_dot_tn, xi[j], dus[j], P_APPLY) for j in H)
    da = tuple(-(cm[1] * _pdot(_dot_nt, g[j], u[j], P_APPLY)) for j in H)
    return da, g, jnp.zeros_like(cm)


@jax.custom_vjp
def _solve_all(a_kbs, rhss, cm):
    return _solve_all_fwd(a_kbs, rhss, cm)[0]


_solve_all.defvjp(_solve_all_fwd, _solve_all_bwd)


def _chunk(sts, r, lw, k, v, kk, b, cm):
    cum = _dot(cm[0], lw, HI)
    ein = jnp.exp(-cum)
    rt = r * jnp.exp(cum)
    kkt = kk * jnp.exp(cum - lw)
    kh = k * ein
    bh = b * ein
    ec = jnp.exp(jnp.sum(lw, axis=0, keepdims=True))
    khe = kh * ec
    bhe = bh * ec
    H = range(len(sts))
    tri, strict, eye = cm[0], cm[1], cm[2]
    rt, kkt, kh, bh, v, khe, bhe, ec = ([a[:, j * HN:(j + 1) * HN] for j in H] for a in (rt, kkt, kh, bh, v, khe, bhe, ec))
    lhs = [jnp.concatenate([kkt[j], rt[j]], axis=0) for j in H]
    rhs_s = [jnp.concatenate([bh[j], kh[j]], axis=0) for j in H]
    lh = [a.astype(BF16).astype(F32) for a in lhs]
    rh = [a.astype(BF16).astype(F32) for a in rhs_s]
    lc = [jnp.concatenate([lh[j], lh[j], lhs[j] - lh[j]], axis=1).astype(BF16) for j in H]
    rc = [jnp.concatenate([rh[j], rhs_s[j] - rh[j], rh[j]], axis=1).astype(BF16) for j in H]
    sc = [_dot_nt(lc[j], rc[j]) for j in H]
    a_kb = [strict * sc[j][:CH, :CH] for j in H]
    a_kk = [strict * sc[j][:CH, CH:] for j in H]
    a_rb = [tri * sc[j][CH:, :CH] for j in H]
    a_rk = [tri * sc[j][CH:, CH:] for j in H]
    ps = [_dot_nt(lhs[j].astype(BF16), sts[j].astype(BF16)) for j in H]
    pv = [_dot(jnp.concatenate([a_kk[j], a_rk[j]], axis=0).astype(BF16), v[j].astype(BF16)) for j in H]
    rhs = [ps[j][:CH] + pv[j][:CH] for j in H]
    o0 = [ps[j][CH:] + pv[j][CH:] for j in H]
    u = _solve_all(tuple(a_kb), tuple(rhs), cm)
    o = [o0[j] - _pdot(_dot, a_rb[j], u[j], P_APPLY) for j in H]
    st2 = [sts[j] * ec[j] + _dot_tn(jnp.concatenate([v[j], u[j]], axis=0).astype(BF16),
                                    jnp.concatenate([khe[j], -bhe[j]], axis=0).astype(BF16)) for j in H]
    return jnp.concatenate(o, axis=1), tuple(st2)


def _chunk_consts():
    t = np.arange(CH)[:, None]
    s = np.arange(CH)[None, :]
    mats = [(t >= s), (t > s), (t == s)]
    for lvl in range(6):
        sz = 1 << lvl
        mats.append(((t // sz) % 2 == 1) & ((s // sz) == (t // sz) - 1))
    mats.append(np.zeros((CH, CH), bool))
    return np.stack(mats).astype(np.float32)


def _adamw(w, g, m, v):
    m = ADAM_B1 * m + (1.0 - ADAM_B1) * g
    v = ADAM_B2 * v + (1.0 - ADAM_B2) * (g * g)
    m_hat = m / (1.0 - ADAM_B1 ** ADAM_STEP)
    v_hat = v / (1.0 - ADAM_B2 ** ADAM_STEP)
    delta = -ADAM_LR * (m_hat / (jnp.sqrt(v_hat) + ADAM_EPS) + ADAM_WD * w)
    return delta, m, v


CT = 128
RB = 64
WIN = RB + 32


def _conv_fwd(pm, ck, T, S):
    nb = T // S

    def body(val_ref, gate_ref, ck_ref, out_ref, ubuf):
        ubuf[0:32, :] = jnp.zeros((32, CT), F32)
        ubuf[32:, :] = val_ref[...] * _sigmoid(gate_ref[...])

        def blk(rb, carry):
            base = pl.multiple_of(rb * RB, RB)
            win = ubuf[pl.ds(base, WIN), :]
            acc = jnp.zeros((RB, CT), F32)
            for j in range(CW):
                acc = acc + ck_ref[j:j + 1, :] * pltpu.roll(win, (WIN - (2 + j)) % WIN, 0)[0:RB, :]
            out_ref[pl.ds(base, RB), :] = acc
            return carry

        lax.fori_loop(0, S // RB, blk, 0)

    return pl.pallas_call(
        body, name="conv_fwd", grid=(D // CT, nb),
        in_specs=[pl.BlockSpec((S, CT), lambda ct, b: (b, ct)),
                  pl.BlockSpec((S, CT), lambda ct, b: (b, D // CT + ct)),
                  pl.BlockSpec((32, CT), lambda ct, b: (0, ct))],
        out_specs=pl.BlockSpec((S, CT), lambda ct, b: (b, ct)),
        out_shape=jax.ShapeDtypeStruct((T, D), F32),
        scratch_shapes=[pltpu.VMEM((S + 32, CT), F32)],
        compiler_params=_cparams(("parallel", "arbitrary")),
    )(pm, pm, ck)


def _conv_bwd(pm, duc, ck, T, S):
    nb = T // S

    def body(val_ref, gate_ref, duc_ref, ck_ref, dval_ref, dgate_ref, dck_ref, ubuf, dbuf, acc):
        b = pl.program_id(1)
        ubuf[0:32, :] = jnp.zeros((32, CT), F32)
        ubuf[32:, :] = val_ref[...] * _sigmoid(gate_ref[...])
        dbuf[0:S, :] = duc_ref[...]
        dbuf[S:, :] = jnp.zeros((32, CT), F32)
        acc[...] = jnp.zeros_like(acc)

        def blk(rb, carry):
            base = pl.multiple_of(rb * RB, RB)
            uwin = ubuf[pl.ds(base, WIN), :]
            dwin = dbuf[pl.ds(base, WIN), :]
            dblk = dwin[0:RB, :]
            du = jnp.zeros((RB, CT), F32)
            for j in range(CW):
                du = du + ck_ref[j:j + 1, :] * pltpu.roll(dwin, (WIN - (CW - 1 - j)) % WIN, 0)[0:RB, :]
                ush = pltpu.roll(uwin, (WIN - (2 + j)) % WIN, 0)[0:RB, :]
                acc[j] += jnp.sum((dblk * ush).reshape(RB // 8, 8, CT), axis=0)
            val = val_ref[pl.ds(base, RB), :]
            sg = _sigmoid(gate_ref[pl.ds(base, RB), :])
            dval_ref[pl.ds(base, RB), :] = (du * sg).astype(BF16)
            dgate_ref[pl.ds(base, RB), :] = (du * val * sg * (1.0 - sg)).astype(BF16)
            return carry

        lax.fori_loop(0, S // RB, blk, 0)

        @pl.when(b == 0)
        def _():
            dck_ref[...] = jnp.zeros_like(dck_ref)
        for j in range(CW):
            dck_ref[j:j + 1, :] += jnp.sum(acc[j], axis=0, keepdims=True)

    return pl.pallas_call(
        body, name="conv_bwd", grid=(D // CT, nb),
        in_specs=[pl.BlockSpec((S, CT), lambda ct, b: (b, ct)),
                  pl.BlockSpec((S, CT), lambda ct, b: (b, D // CT + ct)),
                  pl.BlockSpec((S, CT), lambda ct, b: (b, ct)),
                  pl.BlockSpec((32, CT), lambda ct, b: (0, ct))],
        out_specs=[pl.BlockSpec((S, CT), lambda ct, b: (b, ct)),
                   pl.BlockSpec((S, CT), lambda ct, b: (b, ct)),
                   pl.BlockSpec((32, CT), lambda ct, b: (0, ct))],
        out_shape=[jax.ShapeDtypeStruct((T, D), BF16), jax.ShapeDtypeStruct((T, D), BF16),
                   jax.ShapeDtypeStruct((32, D), F32)],
        scratch_shapes=[pltpu.VMEM((S + 32, CT), F32), pltpu.VMEM((S + 32, CT), F32), pltpu.VMEM((32, 8, CT), F32)],
        compiler_params=_cparams(("parallel", "arbitrary")),
    )(pm, pm, duc, ck)


def _scan_fwd(ins, cm, nb, S):
    nc = S // CH
    nch = nb * NH
    blk = pl.BlockSpec((nb, CH, D), lambda i: (0, i, 0))
    hblk = pl.BlockSpec((nb, NH, None, HN, HN), lambda i: (0, 0, i, 0, 0))

    def body(r_ref, lw_ref, k_ref, v_ref, kk_ref, b_ref, cm_ref, o_ref, hs_ref, st):
        @pl.when(pl.program_id(0) == 0)
        def _():
            st[...] = jnp.zeros_like(st)
        s0 = [st[j] for j in range(nch)]
        for j in range(nch):
            hs_ref[j // NH, j % NH] = s0[j]
        vals = [jnp.concatenate([ref[q] for q in range(nb)], axis=1) for ref in (r_ref, lw_ref, k_ref, v_ref, kk_ref, b_ref)]
        o, s1 = _chunk(s0, *vals, cm_ref[...])
        for q in range(nb):
            o_ref[q] = o[:, q * D:(q + 1) * D]
        for j in range(nch):
            st[j] = s1[j]

    o, hs = pl.pallas_call(
        body, name="scan_fwd", grid=(nc,),
        in_specs=[blk] * 6 + [pl.BlockSpec(cm.shape, lambda i: (0, 0, 0))],
        out_specs=[blk, hblk],
        out_shape=[jax.ShapeDtypeStruct((nb, S, D), F32), jax.ShapeDtypeStruct((nb, NH, nc, HN, HN), F32)],
        scratch_shapes=[pltpu.VMEM((nch, HN, HN), F32)],
        compiler_params=_cparams(("arbitrary",)),
    )(*[a.reshape(nb, S, D) for a in ins], cm)
    return o.reshape(nb * S, D), hs


def _scan_bwd(ins, hs, do, cm, nb, S):
    nc = S // CH
    nch = nb * NH
    blk = pl.BlockSpec((nb, CH, D), lambda i: (0, nc - 1 - i, 0))
    hblk = pl.BlockSpec((nb, NH, None, HN, HN), lambda i: (0, 0, nc - 1 - i, 0, 0))

    def body(r_ref, lw_ref, k_ref, v_ref, kk_ref, b_ref, hs_ref, do_ref, cm_ref,
             dr_ref, dlw_ref, dk_ref, dv_ref, dkk_ref, db_ref, dst):
        @pl.when(pl.program_id(0) == 0)
        def _():
            dst[...] = jnp.zeros_like(dst)
        cmv = cm_ref[...]
        side = lambda ref: jnp.concatenate([ref[q] for q in range(nb)], axis=1)
        f = lambda s0, r, lw, k, v, kk, b: _chunk(s0, r, lw, k, v, kk, b, cmv)
        _, vjp = jax.vjp(f, [hs_ref[j // NH, j % NH] for j in range(nch)],
                         *[side(ref) for ref in (r_ref, lw_ref, k_ref, v_ref, kk_ref, b_ref)])
        ds0, *grads = vjp((side(do_ref), tuple(dst[j] for j in range(nch))))
        for j in range(nch):
            dst[j] = ds0[j]
        for ref, g in zip((dr_ref, dlw_ref, dk_ref, dv_ref, dkk_ref, db_ref), grads):
            for q in range(nb):
                ref[q] = g[:, q * D:(q + 1) * D]

    outs = pl.pallas_call(
        body, name="scan_bwd", grid=(nc,),
        in_specs=[blk] * 6 + [hblk, blk, pl.BlockSpec(cm.shape, lambda i: (0, 0, 0))],
        out_specs=[blk] * 6,
        out_shape=[jax.ShapeDtypeStruct((nb, S, D), F32)] * 6,
        scratch_shapes=[pltpu.VMEM((nch, HN, HN), F32)],
        compiler_params=_cparams(("arbitrary",)),
    )(*[a.reshape(nb, S, D) for a in ins], hs, do.reshape(nb, S, D), cm)
    return [a.reshape(nb * S, D) for a in outs]


def _ew(fn, name, ins, n_out, tm, out_dtype=F32):
    R, W = ins[0].shape[-2:]
    tm = min(tm, R)
    if R % tm:
        tm = R // 2
    assert R % tm == 0 and (tm % 16 == 0 or tm == R), (name, R, tm)

    def body(*refs):
        vals = fn(*[r[...] for r in refs[:len(ins)]])
        for r, v in zip(refs[len(ins):], vals):
            r[...] = v.astype(r.dtype)

    def spec(a):
        if a.ndim == 3:
            return pl.BlockSpec((a.shape[0], tm, W), lambda i: (0, i, 0))
        return pl.BlockSpec((tm, W), lambda i: (i, 0))

    return pl.pallas_call(
        body, name=name, grid=(R // tm,), in_specs=[spec(a) for a in ins],
        out_specs=[pl.BlockSpec((tm, W), lambda i: (i, 0))] * n_out,
        out_shape=[jax.ShapeDtypeStruct((R, W), out_dtype)] * n_out,
        compiler_params=_cparams(("parallel",)),
    )(*ins)


def _sum_slots(r):
    s = r[0]
    for j in range(1, r.shape[0]):
        s = s + r[j]
    return s


def _place():
    x, y, c = lax.axis_index("x"), lax.axis_index("y"), lax.axis_index("c")
    return x, y, c


def _flip(v, d):
    return 1 - v if d else v


CHIP_PEERS = ((1, 0), (0, 1), (1, 1))
DEV_PEERS = tuple((dx, dy, dc) for dx in (0, 1) for dy in (0, 1) for dc in (0, 1))[1:]


def _comm_call(name, ins, out_shapes, plan, n_rem, n_fwd=0):
    n_in = len(ins)

    def body(*refs):
        in_refs, out_refs = refs[:n_in], refs[n_in:n_in + len(out_shapes)]
        send_sems, recv_sems, loc_sems = refs[n_in + len(out_shapes):]
        loc, rem, *rest = plan(in_refs, out_refs, _place())
        fwd = rest[0] if rest else []
        assert len(rem) == n_rem and len(fwd) == n_fwd and len(loc) <= 2 * n_in, (name, len(loc), len(rem), len(fwd))

        def remote(i, s, d, peer):
            return pltpu.make_async_remote_copy(src_ref=s, dst_ref=d, send_sem=send_sems.at[i], recv_sem=recv_sems.at[i],
                                                device_id=peer, device_id_type=MESH)

        copies = [pltpu.make_async_copy(s, d, loc_sems.at[i]) for i, (s, d) in enumerate(loc)]
        rcopies = [remote(i, s, d, peer) for i, (s, d, peer) in enumerate(rem)]
        for cp in copies + rcopies:
            cp.start()
        landed = set()
        fcopies = []
        for i, (s, d, peer, k) in enumerate(fwd):
            if k not in landed:
                rcopies[k].wait_recv()
                landed.add(k)
            fcopies.append(remote(n_rem + i, s, d, peer))
            fcopies[-1].start()
        for k, cp in enumerate(rcopies):
            if k not in landed:
                cp.wait_recv()
        for cp in rcopies + fcopies:
            cp.wait_send()
        for cp in fcopies:
            cp.wait_recv()
        for cp in copies:
            cp.wait()

    return pl.pallas_call(
        body, name=name, in_specs=[ANY] * n_in, out_specs=[ANY] * len(out_shapes), out_shape=out_shapes,
        scratch_shapes=[pltpu.SemaphoreType.DMA((n_rem + n_fwd,)), pltpu.SemaphoreType.DMA((n_rem + n_fwd,)),
                        pltpu.SemaphoreType.DMA((2 * n_in,))],
        compiler_params=pltpu.CompilerParams(has_side_effects=True),
    )(*ins)


def _gather_plan(n_big, in_refs, out_refs, place):
    x, y, c = place
    chip, dev = 2 * x + y, 4 * x + 2 * y + c
    sib = (x, y, 1 - c)
    loc = [(in_refs[0], out_refs[0].at[dev])] + [(s, d.at[chip]) for s, d in zip(in_refs[1 + n_big:], out_refs[1 + n_big:])]
    rem = [(in_refs[0], out_refs[0].at[dev], (_flip(x, dx), _flip(y, dy), _flip(c, dc))) for dx, dy, dc in DEV_PEERS]
    fwd = []
    for s, d in zip(in_refs[1:1 + n_big], out_refs[1:1 + n_big]):
        for dx, dy in CHIP_PEERS:
            px, py = _flip(x, dx), _flip(y, dy)
            fwd.append((d.at[2 * px + py, c], d.at[2 * px + py, c], sib, len(rem)))
            rem.append((s.at[c], d.at[chip, c], (px, py, c)))
    for s, d in zip(in_refs[1 + n_big:], out_refs[1 + n_big:]):
        rem += [(s, d.at[chip], (_flip(x, dx), _flip(y, dy), c)) for dx, dy in CHIP_PEERS]
    return loc, rem, fwd


def _join_plan(in_refs, out_refs, place):
    x, y, c = place
    return [], [(s, d, (x, y, 1 - c)) for s, d in zip(in_refs, out_refs)]


def _scatter_plan(n_all, in_refs, out_refs, place):
    x, y, c = place
    chip, dev = 2 * x + y, 4 * x + 2 * y + c
    loc, rem = [], []
    for s, d in zip(in_refs[:n_all], out_refs[:n_all]):
        loc.append((s.at[dev], d.at[dev]))
        for dx, dy, dc in DEV_PEERS:
            px, py, pc = _flip(x, dx), _flip(y, dy), _flip(c, dc)
            rem.append((s.at[4 * px + 2 * py + pc], d.at[dev], (px, py, pc)))
    for s, d in zip(in_refs[n_all:], out_refs[n_all:]):
        for dx, dy in CHIP_PEERS:
            px, py = _flip(x, dx), _flip(y, dy)
            rem.append((s.at[2 * px + py], d.at[chip], (px, py, c)))
    return loc, rem


def _bshape(a, nb):
    return a.reshape(nb, 1, a.shape[-1])


def _with_prev(cur, before, tiles_per_seq):
    first = pl.program_id(0) % tiles_per_seq == 0
    row0 = jnp.where(first, 0.0, before[before.shape[0] - 1:, :])
    rid = lax.broadcasted_iota(jnp.int32, cur.shape, 0)
    return jnp.where(rid == 0, row0, pltpu.roll(cur, 1, 0))


def _with_next(cur, after, tiles_per_seq):
    last = pl.program_id(0) % tiles_per_seq == tiles_per_seq - 1
    n = cur.shape[0]
    row_n = jnp.where(last, 0.0, after[0:1, :])
    rid = lax.broadcasted_iota(jnp.int32, cur.shape, 0)
    return jnp.where(rid == n - 1, row_n, pltpu.roll(cur, n - 1, 0))


def _local_step(x2d, tgt, mod, wmain, wlora, late_w, ck, w2, a2, small, nb, S, grads_hook):
    T = nb * S
    shift, scale, gate = (_bshape(mod[:, i * D:(i + 1) * D], nb) for i in range(3))
    G = jnp.asarray(np.arange(128)[:, None] == np.arange(D)[None, :] // HN, dtype=BF16)
    cm = jnp.asarray(_chunk_consts())
    ckp = jnp.pad(ck, ((0, 1), (0, 0)))
    zpad = jnp.zeros((64, D), F32)
    w2p = jnp.concatenate([w2, zpad], axis=0)
    a2p = jnp.concatenate([zpad, a2], axis=0)
    mu = small["rwkv_mu"]
    mu_r, mu_k, mu_v, mu_l = mu[:, 0:D], mu[:, D:2 * D], mu[:, 2 * D:3 * D], mu[:, 3 * D:]
    g4 = [mu_r, mu_k, mu_v, mu_l, small["rwkv_w0"], w2p, small["rwkv_a0"], a2p, small["rwkv_k_k"], small["rwkv_k_a"], G]
    g5 = [small["rwkv_gn_g"], small["rwkv_gn_b"], small["rwkv_r_k"], G]
    g3 = [small["conv_b"], small["conv_ln_g"], small["conv_ln_b"]]

    (h,), _, _ = _rows(lambda r, b, g: ([_s1(r[0], g[0], b[0], b[1])], [], []), "pre_fwd", T, S, 256,
                       [(x2d, D, 0)], [scale, shift], [small["norm_g"]], [(D, BF16)], [], [])
    skip = (DMAIN, lambda g, t: pl.multiple_of(g * t + jnp.where(g * t >= 6 * D, LORA, 0), LORA))
    if len(late_w) == 3:
        pm = _matmul(h, wmain, "nt", "proj_main", min(T, 1024), 1024, D, b_rows=skip)
        wco, wro, wo = late_w
    else:
        pm, *landed = _matmul(h, wmain, "nt", "proj_main", min(T, 1024), 1024, D, ride=late_w[0], b_rows=skip)
        wco, wro, wo = late_w[1](landed)
    plo = _matmul(h, wlora, "nt", "proj_lora", 512, LORA, D)
    uc = _conv_fwd(pm, ckp, T, S)
    (uo,), _, _ = _rows(lambda r, b, g: ([_s3(r[0], r[1], *g)], [], []), "conv_post_fwd", T, S, 256,
                        [(uc, D, 0), (pm, D, 2)], [], g3, [(D, BF16)], [], [])
    yc = _matmul(uo, wco, "nn", "conv_out", min(T, 1024), 1024, D)
    rows4 = [(pm, D, 3), (pm, D, 4), (pm, D, 5), (plo, LORA, 0),
             (pm, D, 3, "prev"), (pm, D, 4, "prev"), (pm, D, 5, "prev"), (plo, LORA, 0, "prev")]
    tps4 = S // 128

    def shifted4(r, tps=tps4):
        return list(r[:4]) + [_with_prev(r[i], r[4 + i], tps) for i in range(4)]

    sc_in, _, _ = _rows(lambda r, b, g: (list(_s4(*shifted4(r, S // 256), *g)), [], []), "rwkv_pre_fwd", T, S, 256,
                        rows4, [], g4, [(D, F32)] * 6, [], [])
    o, hs = _scan_fwd(sc_in, cm, nb, S)
    rows5 = [(o, D, 0), (sc_in[0], D, 0), (sc_in[2], D, 0), (sc_in[3], D, 0), (pm, D, 6)]
    (o2,), _, _ = _rows(lambda r, b, g: ([_s5(*r, *g)], [], []), "rwkv_post_fwd", T, S, 256,
                        rows5, [], g5, [(D, BF16)], [], [])
    yr = _matmul(o2, wro, "nn", "rwkv_out", min(T, 1024), 1024, D)
    rows6 = [(yc, D, 0), (yr, D, 0), (pm, D, 7), (pm, D, 8)]
    def tail(r, b, g):
        yc_, yr_, gc_, gr_, x_, tgt_ = r
        m_ = _s6(yc_, yr_, gc_, gr_)
        out_ = _dot(m_.astype(BF16), g[1])
        loss, (dx, dout_, dgate, dfg) = jax.value_and_grad(_s7, argnums=(0, 1, 3, 4))(x_, out_, tgt_, b[0], g[0])
        dout_b = dout_.astype(BF16)
        _, vjp = jax.vjp(_s6, yc_, yr_, gc_, gr_)
        dyc_, dyr_, dgc_, dgr_ = vjp(_dot_nt(dout_b, g[1]))
        return [dx, dout_b, m_, dyc_, dyr_, dgc_, dgr_], [dgate], [dfg, jnp.full((1, 128), loss, F32)]

    (dx_res, dout, m, dyc, dyr, dgc, dgr), (dgate,), (d_final_g, loss_v) = _rows(
        tail, "tail", T, S, 256, rows6 + [(x2d, D, 0), (tgt, D, 0)], [gate], [small["final_g"], wo],
        [(D, F32)] + [(D, BF16)] * 6, [D], [(1, D), (1, 128)])

    d_wo = _matmul(m, dout, "tn", "d_w_out", 512, 1024, T)
    d_wco = _matmul(uo, dyc, "tn", "d_w_conv_out", 512, 1024, T)
    d_wro = _matmul(o2, dyr, "tn", "d_w_rwkv_out", 512, 1024, T)
    duo = _matmul(dyc, wco, "nt", "d_conv_act", min(T, 1024), 1024, D)
    do2 = _matmul(dyr, wro, "nt", "d_rwkv_act", min(T, 1024), 1024, D)

    def conv_post_bwd(r, b, g):
        _, vjp = jax.vjp(_s3, r[0], r[1], *g)
        duc, dog, dcb, dlg, dlb = vjp(r[2])
        return [duc, dog], [], [dcb, dlg, dlb]

    (duc, dcog), _, (d_cb, d_lg, d_lb) = _rows(conv_post_bwd, "conv_post_bwd", T, S, 256,
                                               [(uc, D, 0), (pm, D, 2), (duo, D, 0)], [], g3,
                                               [(D, F32), (D, BF16)], [], [(1, D)] * 3)
    dval, dgt, d_ckp = _conv_bwd(pm, duc, ckp, T, S)

    def rwkv_post_bwd(r, b, g):
        _, vjp = jax.vjp(lambda *z: _s5(*z, g[3]), *r[:5], *g[:3])
        res = vjp(r[5])
        return list(res[:5]), [], list(res[5:8])

    (do, dr_b, dk_b, dv_b, drog), _, (d_gg, d_gb, d_rk) = _rows(
        rwkv_post_bwd, "rwkv_post_bwd", T, S, 256, rows5 + [(do2, D, 0)], [], g5,
        [(D, F32)] * 4 + [(D, BF16)], [], [(1, D)] * 3)
    dsc = _scan_bwd(sc_in, hs, do, cm, nb, S)

    def rwkv_pre_bwd(r, b, g):
        _, vjp = jax.vjp(lambda *z: _s4(*z, g[10]), *shifted4(r), *g[:10])
        ct = (r[8] + r[14], r[9], r[10] + r[15], r[11] + r[16], r[12], r[13])
        res = vjp(ct)
        return list(res[:8]), [], list(res[8:18])

    rows4b = rows4 + [(a, D, 0) for a in dsc] + [(dr_b, D, 0), (dk_b, D, 0), (dv_b, D, 0)]
    gshapes = [(1, D), (1, D), (1, D), (1, LORA), (1, D), (LORA, D), (1, D), (LORA, D), (1, D), (1, D)]
    dts, _, gts = _rows(rwkv_pre_bwd, "rwkv_pre_bwd", T, S, 128, rows4b, [], g4,
                        [(D, BF16)] * 3 + [(LORA, BF16)] + [(D, BF16)] * 3 + [(LORA, BF16)], [], gshapes)
    dr0, dk0, dv0, dl0, dpr, dpk, dpv, dpl = dts
    d_mu_r, d_mu_k, d_mu_v, d_mu_l, d_w0, d_w2p, d_a0, d_a2p, d_kk, d_ka = gts

    def assemble(r, b, g):
        r = [z.astype(F32) for z in r]
        sh = [_with_next(r[10 + i], r[14 + i], tps4) for i in range(4)]
        main = jnp.concatenate([r[0], r[1], r[2], r[3] + sh[0], r[4] + sh[1], r[5] + sh[2], r[6], r[7], r[8]], axis=1)
        return [main, r[9] + sh[3]], [], []

    rows_a = [(dval, D, 0), (dgt, D, 0), (dcog, D, 0), (dr0, D, 0), (dk0, D, 0), (dv0, D, 0), (drog, D, 0), (dgc, D, 0),
              (dgr, D, 0), (dl0, LORA, 0), (dpr, D, 0), (dpk, D, 0), (dpv, D, 0), (dpl, LORA, 0),
              (dpr, D, 0, "next"), (dpk, D, 0, "next"), (dpv, D, 0, "next"), (dpl, LORA, 0, "next")]
    (dpm, dplo), _, _ = _rows(assemble, "assemble_dp", T, S, 128, rows_a, [], [], [(DMAIN, BF16), (LORA, BF16)], [], [])
    RW = D // NCHIP
    c_i = lax.axis_index("c")
    halved = [g.reshape(NCHIP, 2, RW // 2, D).transpose(1, 0, 2, 3).reshape(2, NCHIP * RW // 2, D) for g in (d_wco, d_wro, d_wo)]
    halved += [g.reshape(-1, NCHIP, 2, RW // 2).transpose(2, 1, 0, 3).reshape(2, -1, RW // 2)
               for g in (d_ckp[:CW], d_w2p[:64], d_a2p[64:])]
    h_keep, h_send = (lax.dynamic_slice_in_dim(h, k * (D // 2), D // 2, axis=1) for k in (c_i, 1 - c_i))
    tk = T
    send = [_matmul(h_send, dpm, "tn", "d_w_main_send", D // 2, 1024, tk, out_t=True)[0],
            _matmul(h_send, dplo, "tn", "d_w_lora_send", D // 2, LORA, tk, out_t=True)[0]]
    keep = [None, _matmul(h_keep, dplo, "tn", "d_w_lora_keep", D // 2, LORA, tk, out_t=True)[0]]

    def to_sibling(in_refs, out_refs, place):
        x, y, c = place
        return [], [(s if i < 2 else s.at[1 - c], d, (x, y, 1 - c)) for i, (s, d) in enumerate(zip(in_refs, out_refs))]

    got_shapes = [jax.ShapeDtypeStruct(t.shape, F32) for t in send] + [jax.ShapeDtypeStruct(t.shape[1:], F32) for t in halved]
    d_w_keep, *got_h = _matmul(h_keep, dpm, "tn", "d_w_main_keep", D // 2, 1024, tk, out_t=True,
                               ride=(send + halved, got_shapes, to_sibling, len(got_shapes)))
    keep[0] = d_w_keep[0]

    def own_half_plus(both, q):
        return [jnp.where(lax.axis_index("c") == 0, both[0], both[1]) + q]

    chip_part = [_ew(own_half_plus, "chip_sum_%d" % (2 + i), [g, got_h[2 + i]], 1, 1024, BF16)[0]
                 for i, g in enumerate(halved)]

    def sum_body(p_ref, q_ref, *rest):
        rest[-1][...] = (p_ref[...] + q_ref[...]).astype(BF16)

    blk = pl.BlockSpec((1024, D // 2), lambda i: (i, 0))
    d_win_h = pl.pallas_call(
        sum_body, name="chip_sum_w_in", grid=(DMAIN // 1024,), in_specs=[blk, blk],
        out_specs=pl.BlockSpec((pl.Element(1024), pl.Element(D // 2)), lambda i: (skip[1](i, 1024), 0)),
        out_shape=jax.ShapeDtypeStruct((DMAIN + LORA, D // 2), BF16), compiler_params=_cparams(("parallel",)),
    )(keep[0], got_h[0])
    lora_blk = pl.BlockSpec((LORA, D // 2), lambda i: (0, 0))
    d_win_h = pl.pallas_call(
        sum_body, name="chip_sum_w_lora", grid=(1,), in_specs=[lora_blk, lora_blk, ANY],
        out_specs=pl.BlockSpec((LORA, D // 2), lambda i: (6 * D // LORA, 0)),
        out_shape=jax.ShapeDtypeStruct((DMAIN + LORA, D // 2), BF16), input_output_aliases={2: 0},
        compiler_params=_cparams(("arbitrary",)),
    )(keep[1], got_h[1], d_win_h)
    chip_part = [d_win_h] + chip_part
    dh_m, *got_big = _matmul(dpm, wmain, "nn", "d_h_main", 512, 1024, 3072, ride=grads_hook(chip_part), b_rows=skip)
    dh_l = _matmul(dplo, wlora, "nn", "d_h_lora", 512, 1024, LORA)

    def pre_bwd(r, b, g):
        _, vjp = jax.vjp(_s1, r[0], g[0], b[0], b[1])
        dx, dg, dscale, dshift = vjp(r[1] + r[2])
        return [dx + r[3]], [dscale, dshift], [dg]

    (gx,), (dscale, dshift), (d_ng,) = _rows(pre_bwd, "pre_bwd", T, S, 256,
                                             [(x2d, D, 0), (dh_m, D, 0), (dh_l, D, 0), (dx_res, D, 0)],
                                             [scale, shift], [small["norm_g"]], [(D, F32)], [D, D], [(1, D)])
    dmod = jnp.concatenate([dshift, dscale, dgate], axis=-1).reshape(nb, 3 * D)
    d_small = {"norm_g": d_ng, "conv_b": d_cb, "conv_ln_g": d_lg, "conv_ln_b": d_lb,
               "rwkv_mu": jnp.concatenate([d_mu_r, d_mu_k, d_mu_v, d_mu_l], axis=1),
               "rwkv_w0": d_w0, "rwkv_a0": d_a0, "rwkv_k_k": d_kk, "rwkv_k_a": d_ka, "rwkv_r_k": d_rk,
               "rwkv_gn_g": d_gg, "rwkv_gn_b": d_gb, "final_g": d_final_g}
    return loss_v[0, 0], gx, dmod, got_big, d_small


def _step(a):
    nb, S, _ = a["x"].shape
    T = nb * S
    x_i, y_i, c_i = _place()
    chip = 2 * x_i + y_i
    w_in_t, m_w_in_t, v_w_in_t = (jnp.transpose(a[p + "w_in"][0]) for p in ("", "m_", "v_"))
    WS = w_in_t.shape[0]
    small_w = {n: a[n].reshape(1, sz) for n, sz in SMALL}

    def halves(t):
        return t.reshape(2, t.shape[0] // 2, t.shape[1])

    g_ins = [a["c"], halves(w_in_t.astype(BF16)), a["conv_k"][0], a["rwkv_w2"][0], a["rwkv_a2"][0]]
    g_out = [jax.ShapeDtypeStruct((NDEV,) + g_ins[0].shape, F32)]
    g_out += [jax.ShapeDtypeStruct((NCHIP,) + t.shape, t.dtype) for t in g_ins[1:]]
    c_all, win_g, ck_g, w2_g, a2_g = _comm_call(
        "gather_weights", g_ins, g_out, functools.partial(_gather_plan, 1), 7 + 3 * 4, 3)
    c_all = c_all.reshape(NDEV * nb, D)
    win_t = lax.dynamic_update_index_in_dim(win_g, g_ins[1], chip, 0).reshape(NCHIP * WS, D)
    late = [a[n][0].astype(BF16) for n in ("w_conv_out", "w_rwkv_out", "w_out")]

    def late_plan(in_refs, out_refs, place):
        x, y, c = place
        return [], [(s, d.at[2 * x + y], (_flip(x, dx), _flip(y, dy), c))
                    for s, d in zip(in_refs, out_refs) for dx, dy in CHIP_PEERS]

    def late_finish(landed):
        return [lax.dynamic_update_index_in_dim(g, own, chip, 0).reshape(D, D) for g, own in zip(landed, late)]

    late_w = ((late, [jax.ShapeDtypeStruct((NCHIP,) + t.shape, BF16) for t in late], late_plan, 9), late_finish)
    wmain = win_t
    wlora = win_t[6 * D:6 * D + LORA]
    ck = jnp.concatenate([ck_g[j] for j in range(NCHIP)], axis=1)
    w2 = jnp.concatenate([w2_g[j] for j in range(NCHIP)], axis=1)
    a2 = jnp.concatenate([a2_g[j] for j in range(NCHIP)], axis=1)

    ada_w = a["ada_w"][0]
    MW = ada_w.shape[1]
    ada_b_loc = lax.dynamic_slice(a["ada_b"], (0, chip * MW), (1, MW))

    def mod_body(c_ref, w_ref, b_ref, o_ref):
        o_ref[...] = _dot(_silu(c_ref[...]), w_ref[...], HI) + b_ref[...]

    modp = pl.pallas_call(mod_body, name="ada_mod", out_shape=jax.ShapeDtypeStruct((NDEV * nb, MW), F32),
                          compiler_params=_cparams())(c_all, ada_w, ada_b_loc)
    (mod_g,) = _comm_call("scatter_mod", [modp.reshape(NDEV, nb, MW)],
                          [jax.ShapeDtypeStruct((NDEV, nb, MW), F32)],
                          functools.partial(_scatter_plan, 1), 7)
    mod = mod_g.reshape(NCHIP, 2, nb, MW)
    mod = mod[:, 0].transpose(1, 0, 2).reshape(nb, NCHIP * MW)

    RW = D // NCHIP
    sh_s = []

    def grads_hook(chip_part):
        sh_s.append(chip_part[0].reshape(NCHIP, WS, D // 2))
        sh_s.extend(t.reshape(NCHIP, RW // 2, D) for t in chip_part[1:4])
        sh_s.extend(t.reshape(NCHIP, -1, RW // 2) for t in chip_part[4:])
        return (sh_s, [jax.ShapeDtypeStruct(t.shape, t.dtype) for t in sh_s], functools.partial(_scatter_plan, 0),
                3 * len(sh_s))

    loss_p, gx, dmod, got_big, d_small = _local_step(
        a["x"].reshape(T, D), a["loss_target"].reshape(T, D), mod, wmain, wlora, late_w, ck, w2, a2, small_w, nb, S,
        grads_hook)
    loss = lax.psum(loss_p, ("x", "y", "c"))

    d_small["ada_b"] = _colsum(dmod)
    small_vec = jnp.concatenate([d_small[n] for n, _ in SMALL], axis=1)
    dmod_s = dmod.reshape(nb, NCHIP, MW).transpose(1, 0, 2)
    dmod_s = jnp.repeat(dmod_s, 2, axis=0)
    small_s = jnp.broadcast_to(small_vec[None], (NDEV, 1, NSMALL))
    got = _comm_call("scatter_small", [dmod_s, small_s], [jax.ShapeDtypeStruct(t.shape, F32) for t in (dmod_s, small_s)],
                     functools.partial(_scatter_plan, 2), 14)
    dmod_all, small_all = got[0].reshape(NDEV * nb, MW), got[1].reshape(NDEV, NSMALL)

    def shard_sum(recv, sent):
        chip_i = 2 * lax.axis_index("x") + lax.axis_index("y")
        s = None
        for j in range(NCHIP):
            t = jnp.where(chip_i == j, sent[j], recv[j]).astype(F32)
            s = t if s is None else s + t
        return [s]

    fin = [_ew(shard_sum, "shard_sum_%d" % i, [t, sh_s[i]], 1, 128)[0] for i, t in enumerate(got_big)]
    oth = _comm_call("join_halves", fin, [jax.ShapeDtypeStruct(t.shape, F32) for t in fin], _join_plan, len(fin))

    outs = {}

    def upd_halves(name, mine, other):
        shp = a[name].shape
        R, W = 2 * mine.shape[0], mine.shape[1]
        tm = 128
        nh = R // 2 // tm

        def body(w_ref, m_ref, v_ref, f_ref, o_ref, g_ref, d_ref, m2_ref, v2_ref):
            g = jnp.where(pl.program_id(0) // nh == lax.axis_index("c"), f_ref[...], o_ref[...])
            g_ref[...] = g
            d_ref[...], m2_ref[...], v2_ref[...] = _adamw(w_ref[...], g, m_ref[...], v_ref[...])

        full = pl.BlockSpec((None, tm, W), lambda i: (0, i, 0))
        half = pl.BlockSpec((tm, W), lambda i: (i % nh, 0))
        assert shp == (1, R, W)
        outs[name] = pl.pallas_call(
            body, name="adamw_" + name, grid=(R // tm,), in_specs=[full] * 3 + [half] * 2, out_specs=[full] * 4,
            out_shape=[jax.ShapeDtypeStruct(shp, F32)] * 4, compiler_params=_cparams(("parallel",)),
        )(*[a[p + name] for p in ("", "m_", "v_")], mine, other)

    for name, f, o in zip(("w_conv_out", "w_rwkv_out", "w_out"), fin[1:4], oth[1:4]):
        upd_halves(name, f, o)

    def upd_col_halves(name, wmv, mine, other, tw):
        R, W = wmv[0].shape

        def body(w_ref, m_ref, v_ref, f_ref, o_ref, g_ref, d_ref, m2_ref, v2_ref):
            first = lax.axis_index("c") == 0
            g = jnp.concatenate([jnp.where(first, f_ref[...], o_ref[...]), jnp.where(first, o_ref[...], f_ref[...])],
                                axis=1)
            g_ref[...] = g
            d_ref[...], m2_ref[...], v2_ref[...] = _adamw(w_ref[...], g, m_ref[...], v_ref[...])

        full = pl.BlockSpec((tw, W), lambda i: (i, 0))
        half = pl.BlockSpec((tw, W // 2), lambda i: (i, 0))
        return pl.pallas_call(
            body, name="adamw_" + name, grid=(R // tw,), in_specs=[full] * 3 + [half] * 2, out_specs=[full] * 4,
            out_shape=[jax.ShapeDtypeStruct((R, W), F32)] * 4, compiler_params=_cparams(("parallel",)),
        )(*wmv, mine, other)

    res = upd_col_halves("w_in", (w_in_t, m_w_in_t, v_w_in_t), fin[0], oth[0], WS // 4)
    outs["w_in"] = [jnp.transpose(r)[None] for r in res]
    for name, f, o in zip(("conv_k", "rwkv_w2", "rwkv_a2"), fin[4:], oth[4:]):
        res = upd_col_halves(name, [a[p + name][0] for p in ("", "m_", "v_")], f, o, f.shape[0])
        outs[name] = [r[None] for r in res]

    def adaw_body(c_ref, dm_ref, w_ref, m_ref, v_ref, g_ref, d_ref, m2_ref, v2_ref):
        g = _dot_tn(_silu(c_ref[...]), dm_ref[...], HI)
        g_ref[...] = g
        d_ref[...], m2_ref[...], v2_ref[...] = _adamw(w_ref[...], g, m_ref[...], v_ref[...])

    res = pl.pallas_call(adaw_body, name="adamw_ada_w", out_shape=[jax.ShapeDtypeStruct((D, MW), F32)] * 4,
                         compiler_params=_cparams())(c_all, dmod_all, ada_w, a["m_ada_w"][0], a["v_ada_w"][0])
    outs["ada_w"] = [r.reshape(a["ada_w"].shape) for r in res]

    def small_body(gs_ref, *refs):
        g_all = _sum_slots(gs_ref[...])
        ins, out_refs = refs[:3 * len(SMALL)], refs[3 * len(SMALL):]
        off = 0
        for i, (_, sz) in enumerate(SMALL):
            w_ref, m_ref, v_ref = ins[3 * i:3 * i + 3]
            g = g_all[:, off:off + sz]
            res4 = (g, *_adamw(w_ref[...], g, m_ref[...], v_ref[...]))
            for r, val in zip(out_refs[4 * i:4 * i + 4], res4):
                r[...] = val
            off += sz

    small_ins = [a[p + n].reshape(1, sz) for n, sz in SMALL for p in ("", "m_", "v_")]
    res = pl.pallas_call(
        small_body, name="adamw_small", compiler_params=_cparams(),
        out_shape=[jax.ShapeDtypeStruct((1, sz), F32) for _, sz in SMALL for _ in range(4)],
    )(small_all.reshape(NDEV, 1, NSMALL), *small_ins)
    for i, (n, _) in enumerate(SMALL):
        outs[n] = [r.reshape(a[n].shape) for r in res[4 * i:4 * i + 4]]

    return (loss, gx.reshape(nb, S, D), *[outs[n][0] for n in WEIGHTS], *[outs[n][1] for n in WEIGHTS],
            *[outs[n][2] for n in WEIGHTS], *[outs[n][3] for n in WEIGHTS])


def _colsum(dmod):
    def body(d_ref, o_ref):
        o_ref[...] = jnp.sum(d_ref[...], axis=0, keepdims=True)
    return pl.pallas_call(body, name="ada_b_rowsum", out_shape=jax.ShapeDtypeStruct((1, dmod.shape[1]), F32),
                          compiler_params=_cparams())(dmod)


def kernel(x, c, ada_w, ada_b, norm_g, w_in, conv_k, conv_b, conv_ln_g, conv_ln_b, w_conv_out, rwkv_mu, rwkv_w0, rwkv_w2, rwkv_a0, rwkv_a2, rwkv_k_k, rwkv_k_a, rwkv_r_k, rwkv_gn_g, rwkv_gn_b, w_rwkv_out, w_out, final_g, loss_target, m_ada_w, m_ada_b, m_norm_g, m_w_in, m_conv_k, m_conv_b, m_conv_ln_g, m_conv_ln_b, m_w_conv_out, m_rwkv_mu, m_rwkv_w0, m_rwkv_w2, m_rwkv_a0, m_rwkv_a2, m_rwkv_k_k, m_rwkv_k_a, m_rwkv_r_k, m_rwkv_gn_g, m_rwkv_gn_b, m_w_rwkv_out, m_w_out, m_final_g, v_ada_w, v_ada_b, v_norm_g, v_w_in, v_conv_k, v_conv_b, v_conv_ln_g, v_conv_ln_b, v_w_conv_out, v_rwkv_mu, v_rwkv_w0, v_rwkv_w2, v_rwkv_a0, v_rwkv_a2, v_rwkv_k_k, v_rwkv_k_a, v_rwkv_r_k, v_rwkv_gn_g, v_rwkv_gn_b, v_w_rwkv_out, v_w_out, v_final_g):
    return _step(dict(locals()))
```

```python
import functools

import numpy as np
import jax
import jax.numpy as jnp
from jax import lax
from jax.experimental import pallas as pl
from jax.experimental.pallas import tpu as pltpu

F32 = jnp.float32
BF16 = jnp.bfloat16
HI = lax.Precision.HIGHEST
MESH = pl.DeviceIdType.MESH
ANY = pl.BlockSpec(memory_space=pl.ANY)

D = 1024
NH = 16
HN = 64
LORA = 128
DMAIN = 9 * D
CH = 64
CW = 31
NCHIP = 4
NDEV = 8
VMEM_LIMIT = 56 * 1024 * 1024

RMS_EPS = 1e-6
LN_EPS = 1e-5
GN_EPS = 64e-5
L2_EPS = 1e-12
ADAM_LR = 0.001
ADAM_B1 = 0.9
ADAM_B2 = 0.999
ADAM_EPS = 1e-08
ADAM_WD = 0.01
ADAM_STEP = 10

SMALL = (("ada_b", 3072), ("norm_g", 1024), ("conv_b", 1024), ("conv_ln_g", 1024), ("conv_ln_b", 1024),
         ("rwkv_mu", 3200), ("rwkv_w0", 1024), ("rwkv_a0", 1024), ("rwkv_k_k", 1024), ("rwkv_k_a", 1024),
         ("rwkv_r_k", 1024), ("rwkv_gn_g", 1024), ("rwkv_gn_b", 1024), ("final_g", 1024))
NSMALL = sum(n for _, n in SMALL)

WEIGHTS = ['ada_w', 'ada_b', 'norm_g', 'w_in', 'conv_k', 'conv_b', 'conv_ln_g', 'conv_ln_b', 'w_conv_out', 'rwkv_mu',
           'rwkv_w0', 'rwkv_w2', 'rwkv_a0', 'rwkv_a2', 'rwkv_k_k', 'rwkv_k_a', 'rwkv_r_k', 'rwkv_gn_g', 'rwkv_gn_b',
           'w_rwkv_out', 'w_out', 'final_g']


def _cparams(sem=None, **kw):
    if sem is not None:
        kw["dimension_semantics"] = sem
    return pltpu.CompilerParams(vmem_limit_bytes=VMEM_LIMIT, **kw)


def _dot(a, b, prec=None):
    return jnp.dot(a, b, preferred_element_type=F32, precision=prec)


def _dot_nt(a, b, prec=None):
    return lax.dot_general(a, b, (((1,), (1,)), ((), ())), preferred_element_type=F32, precision=prec)


def _dot_tn(a, b, prec=None):
    return lax.dot_general(a, b, (((0,), (0,)), ((), ())), preferred_element_type=F32, precision=prec)


def _pdot(f, a, b, p):
    if p == "hi":
        return f(a, b, HI)
    ah, bh = a.astype(BF16), b.astype(BF16)
    if p == "bf":
        return f(ah, bh)
    al, bl = (a - ah.astype(F32)).astype(BF16), (b - bh.astype(F32)).astype(BF16)
    return f(ah, bh) + (f(ah, bl) + f(al, bh))


P_SCORE = "b3"
P_INV = "bf"
P_APPLY = "bf"


def _sigmoid(z):
    return 1.0 / (1.0 + jnp.exp(-z))


def _silu(z):
    return z * _sigmoid(z)


def _matmul(a, b, mode, name, tm, tn, tk, ride=None, out_t=False, b_rows=None):
    if mode == "nn":
        (M, K), N = a.shape, b.shape[1]
        a_spec = pl.BlockSpec((tm, tk), lambda j, i, k: (i, k))
        b_spec = pl.BlockSpec((tk, tn), lambda j, i, k: (k, j))
        if b_rows is not None:
            assert b_rows[0] == K
            b_spec = pl.BlockSpec((pl.Element(tk), pl.Element(tn)), lambda j, i, k: (b_rows[1](k, tk), j * tn))
        f = _dot
    elif mode == "nt":
        (M, K), N = a.shape, b.shape[0]
        a_spec = pl.BlockSpec((tm, tk), lambda j, i, k: (i, k))
        b_spec = pl.BlockSpec((tn, tk), lambda j, i, k: (j, k))
        if b_rows is not None:
            N = b_rows[0]
            b_spec = pl.BlockSpec((pl.Element(tn), pl.Element(tk)), lambda j, i, k: (b_rows[1](j, tn), k * tk))
        f = _dot_nt
    else:
        (K, M), N = a.shape, b.shape[1]
        a_spec = pl.BlockSpec((tk, tm), lambda j, i, k: (k, i))
        b_spec = pl.BlockSpec((tk, tn), lambda j, i, k: (k, j))
        f = _dot_tn
    assert M % tm == 0 and N % tn == 0 and K % tk == 0, (name, M, N, K)

    grid = (N // tn, M // tm, K // tk)
    o_spec = pl.BlockSpec((tm, tn), lambda j, i, k: (i, j))
    o_shape = jax.ShapeDtypeStruct((M, N), F32)

    scratch = []
    if out_t:
        o_spec = pl.BlockSpec((None, tn, tm), lambda j, i, k: (i, j, 0))
        o_shape = jax.ShapeDtypeStruct((M // tm, N, tm), F32)
        scratch = [pltpu.VMEM((tm, tn), F32)]

    def step(a_ref, b_ref, o_ref, *acc):
        acc_ref = acc[0] if out_t else o_ref

        @pl.when(pl.program_id(2) == 0)
        def _():
            acc_ref[...] = jnp.zeros_like(acc_ref)
        acc_ref[...] += f(a_ref[...], b_ref[...])
        if out_t:
            @pl.when(pl.program_id(2) == grid[2] - 1)
            def _():
                o_ref[...] = acc_ref[...].T

    if ride is None:
        return pl.pallas_call(
            step, name=name, grid=grid, in_specs=[a_spec, b_spec], out_specs=o_spec, out_shape=o_shape,
            scratch_shapes=scratch, compiler_params=_cparams(("parallel", "parallel", "arbitrary")),
        )(a, b)

    r_ins, r_shapes, plan, n_rem = ride
    n_ri, n_ro = len(r_ins), len(r_shapes)

    def body(a_ref, b_ref, *rest):
        r_in, o_ref, r_out = rest[:n_ri], rest[n_ri], rest[n_ri + 1:n_ri + 1 + n_ro]
        send_sems, recv_sems, *acc = rest[n_ri + 1 + n_ro:]
        loc, rem = plan(r_in, r_out, _place())
        assert not loc and len(rem) == n_rem, (name, len(loc), len(rem))
        copies = [pltpu.make_async_remote_copy(src_ref=s, dst_ref=d, send_sem=send_sems.at[i], recv_sem=recv_sems.at[i],
                                               device_id=peer, device_id_type=MESH) for i, (s, d, peer) in enumerate(rem)]
        pid = [pl.program_id(ax) for ax in range(3)]

        @pl.when((pid[0] == 0) & (pid[1] == 0) & (pid[2] == 0))
        def _():
            for cp in copies:
                cp.start()

        step(a_ref, b_ref, o_ref, *acc)

        @pl.when((pid[0] == grid[0] - 1) & (pid[1] == grid[1] - 1) & (pid[2] == grid[2] - 1))
        def _():
            for cp in copies:
                cp.wait_send()
            for cp in copies:
                cp.wait_recv()

    return pl.pallas_call(
        body, name=name, grid=grid, in_specs=[a_spec, b_spec] + [ANY] * n_ri, out_specs=[o_spec] + [ANY] * n_ro,
        out_shape=[o_shape] + list(r_shapes),
        scratch_shapes=[pltpu.SemaphoreType.DMA((n_rem,)), pltpu.SemaphoreType.DMA((n_rem,))] + scratch,
        compiler_params=_cparams(("arbitrary", "arbitrary", "arbitrary"), has_side_effects=True),
    )(a, b, *r_ins)


def _rows(fn, name, T, S, tm, rows, bpars, gpars, outs, baccs, gaccs):
    nb = T // S
    tps = S // tm
    n_r, n_b, n_g, n_o, n_ba, n_ga = len(rows), len(bpars), len(gpars), len(outs), len(baccs), len(gaccs)

    def body(*refs):
        r_refs = refs[:n_r]
        b_refs = refs[n_r:n_r + n_b]
        g_refs = refs[n_r + n_b:n_r + n_b + n_g]
        o_refs = refs[n_r + n_b + n_g:n_r + n_b + n_g + n_o]
        ba_refs = refs[n_r + n_b + n_g + n_o:n_r + n_b + n_g + n_o + n_ba]
        ga_refs = refs[n_r + n_b + n_g + n_o + n_ba:]
        i = pl.program_id(0)
        o_vals, ba_vals, ga_vals = fn([r[...] for r in r_refs], [r[...] for r in b_refs], [r[...] for r in g_refs])
        for r, v in zip(o_refs, o_vals):
            r[...] = v.astype(r.dtype)
        if n_ba:
            @pl.when(i % tps == 0)
            def _():
                for r in ba_refs:
                    r[...] = jnp.zeros_like(r)
            for r, v in zip(ba_refs, ba_vals):
                r[...] += v.reshape(r.shape)
        if n_ga:
            @pl.when(i == 0)
            def _():
                for r in ga_refs:
                    r[...] = jnp.zeros_like(r)
            for r, v in zip(ga_refs, ga_vals):
                r[...] += v.reshape(r.shape)

    def row_spec(arr, w, cb, kind="tile"):
        hr = 8 * (4 // arr.dtype.itemsize)
        if kind == "prev":
            return pl.BlockSpec((hr, w), lambda i: (jnp.maximum(i * (tm // hr) - 1, 0), cb))
        if kind == "next":
            return pl.BlockSpec((hr, w), lambda i: (jnp.minimum((i + 1) * (tm // hr), T // hr - 1), cb))
        return pl.BlockSpec((tm, w), lambda i: (i, cb))

    in_specs = [row_spec(*r) for r in rows]
    in_specs += [pl.BlockSpec((None, 1, p.shape[-1]), lambda i: (i // tps, 0, 0)) for p in bpars]
    in_specs += [pl.BlockSpec(p.shape, lambda i: (0, 0)) for p in gpars]
    out_specs = [pl.BlockSpec((tm, w), lambda i: (i, 0)) for w, _ in outs]
    out_specs += [pl.BlockSpec((None, 1, w), lambda i: (i // tps, 0, 0)) for w in baccs]
    out_specs += [pl.BlockSpec(s, lambda i: (0, 0)) for s in gaccs]
    out_shape = [jax.ShapeDtypeStruct((T, w), dt) for w, dt in outs]
    out_shape += [jax.ShapeDtypeStruct((nb, 1, w), F32) for w in baccs]
    out_shape += [jax.ShapeDtypeStruct(s, F32) for s in gaccs]
    res = pl.pallas_call(
        body, name=name, grid=(T // tm,), in_specs=in_specs, out_specs=out_specs, out_shape=out_shape,
        compiler_params=_cparams(("arbitrary",)),
    )(*[r[0] for r in rows], *bpars, *gpars)
    return res[:n_o], res[n_o:n_o + n_ba], res[n_o + n_ba:]


@jax.custom_vjp
def _gsum(z, G):
    zh = z.astype(BF16)
    zl = (z - zh.astype(F32)).astype(BF16)
    r = _dot_nt(zh, G) + _dot_nt(zl, G)
    rh = r.astype(BF16)
    rl = (r - rh.astype(F32)).astype(BF16)
    return _dot(rh, G) + _dot(rl, G)


def _dot3(x, w):
    xh = x.astype(BF16).astype(F32)
    wh = w.astype(BF16).astype(F32)
    xc = jnp.concatenate([xh, xh, x - xh], axis=1).astype(BF16)
    wc = jnp.concatenate([wh, w - wh, wh], axis=0).astype(BF16)
    return _dot(xc, wc)


_gsum.defvjp(lambda z, G: (_gsum(z, G), G), lambda G, ct: (_gsum(ct, G), jnp.zeros_like(G)))


def _s1(x, g, scale, shift):
    y = x * lax.rsqrt(jnp.mean(x * x, axis=-1, keepdims=True) + RMS_EPS)
    return (y * g) * (1.0 + scale) + shift


def _s3(uc, og, cb, lg, lb):
    u = uc + cb
    mu = jnp.mean(u, axis=-1, keepdims=True)
    d = u - mu
    var = jnp.mean(d * d, axis=-1, keepdims=True)
    y = d * lax.rsqrt(var + LN_EPS) * lg + lb
    return _silu(y) * _silu(og)


def _s4(r0, k0, v0, l0, pr, pk, pv, plo, mu_r, mu_k, mu_v, mu_l, w0, w2p, a0, a2p, k_k, k_a, G):
    r = r0 + mu_r * (pr - r0)
    k = k0 + mu_k * (pk - k0)
    v = v0 + mu_v * (pv - v0)
    lo = l0 + mu_l * (plo - l0)
    w_pre = w0 + _dot3(jnp.tanh(lo), w2p)
    lw = -np.float32(np.exp(-0.5)) * _sigmoid(w_pre)
    a = _sigmoid(a0 + _dot3(lo, a2p))
    kkr = k * k_k
    ss = _gsum(kkr * kkr, G)
    kk = kkr / jnp.maximum(jnp.sqrt(ss), L2_EPS)
    k2 = k * (1.0 + (a - 1.0) * k_a)
    return r, lw, k2, v, kk, kk * a


def _s5(o, r, k2, v, og, gg, gb, rk, G):
    mu = _gsum(o, G) * (1.0 / HN)
    d = o - mu
    var = _gsum(d * d, G) * (1.0 / HN)
    y = d * lax.rsqrt(var + GN_EPS) * gg + gb
    bonus = _gsum(r * k2 * rk, G)
    return (y + bonus * v) * _silu(og)


def _s6(yc, yr, gc, gr):
    return _sigmoid(gc) * yc + _sigmoid(gr) * yr


def _s7(x, out, tgt, gate, fg):
    x2 = x + gate * out
    y = x2 * lax.rsqrt(jnp.mean(x2 * x2, axis=-1, keepdims=True) + RMS_EPS) * fg
    e = y - tgt
    return 0.5 * jnp.sum(jnp.mean(e * e, axis=-1))


def _solve_all_fwd(a_kbs, rhss, cm):
    H = range(len(a_kbs))
    xi = [cm[2] - cm[3] * a_kbs[j] for j in H]
    for lvl in range(1, 6):
        t = [_pdot(_dot, xi[j], cm[3 + lvl] * a_kbs[j], P_INV) for j in H]
        xi = [xi[j] - _pdot(_dot, t[j], xi[j], P_INV) for j in H]
    u = tuple(_pdot(_dot, xi[j], rhss[j], P_APPLY) for j in H)
    return u, (xi, u, cm)


def _solve_all_bwd(res, dus):
    xi, u, cm = res
    H = range(len(u))
    g = tuple(_pdot(_dot_tn, xi[j], dus[j], P_APPLY) for j in H)
    da = tuple(-(cm[1] * _pdot(_dot_nt, g[j], u[j], P_APPLY)) for j in H)
    return da, g, jnp.zeros_like(cm)


@jax.custom_vjp
def _solve_all(a_kbs, rhss, cm):
    return _solve_all_fwd(a_kbs, rhss, cm)[0]


_solve_all.defvjp(_solve_all_fwd, _solve_all_bwd)


def _chunk(sts, r, lw, k, v, kk, b, cm):
    cum = _dot(cm[0], lw, HI)
    ein = jnp.exp(-cum)
    rt = r * jnp.exp(cum)
    kkt = kk * jnp.exp(cum - lw)
    kh = k * ein
    bh = b * ein
    ec = jnp.exp(jnp.sum(lw, axis=0, keepdims=True))
    khe = kh * ec
    bhe = bh * ec
    H = range(len(sts))
    tri, strict, eye = cm[0], cm[1], cm[2]
    rt, kkt, kh, bh, v, khe, bhe, ec = ([a[:, j * HN:(j + 1) * HN] for j in H] for a in (rt, kkt, kh, bh, v, khe, bhe, ec))
    lhs = [jnp.concatenate([kkt[j], rt[j]], axis=0) for j in H]
    rhs_s = [jnp.concatenate([bh[j], kh[j]], axis=0) for j in H]
    lh = [a.astype(BF16).astype(F32) for a in lhs]
    rh = [a.astype(BF16).astype(F32) for a in rhs_s]
    lc = [jnp.concatenate([lh[j], lh[j], lhs[j] - lh[j]], axis=1).astype(BF16) for j in H]
    rc = [jnp.concatenate([rh[j], rhs_s[j] - rh[j], rh[j]], axis=1).astype(BF16) for j in H]
    sc = [_dot_nt(lc[j], rc[j]) for j in H]
    a_kb = [strict * sc[j][:CH, :CH] for j in H]
    a_kk = [strict * sc[j][:CH, CH:] for j in H]
    a_rb = [tri * sc[j][CH:, :CH] for j in H]
    a_rk = [tri * sc[j][CH:, CH:] for j in H]
    ps = [_dot_nt(lhs[j].astype(BF16), sts[j].astype(BF16)) for j in H]
    pv = [_dot(jnp.concatenate([a_kk[j], a_rk[j]], axis=0).astype(BF16), v[j].astype(BF16)) for j in H]
    rhs = [ps[j][:CH] + pv[j][:CH] for j in H]
    o0 = [ps[j][CH:] + pv[j][CH:] for j in H]
    u = _solve_all(tuple(a_kb), tuple(rhs), cm)
    o = [o0[j] - _pdot(_dot, a_rb[j], u[j], P_APPLY) for j in H]
    st2 = [sts[j] * ec[j] + _dot_tn(jnp.concatenate([v[j], u[j]], axis=0).astype(BF16),
                                    jnp.concatenate([khe[j], -bhe[j]], axis=0).astype(BF16)) for j in H]
    return jnp.concatenate(o, axis=1), tuple(st2)


def _chunk_consts():
    t = np.arange(CH)[:, None]
    s = np.arange(CH)[None, :]
    mats = [(t >= s), (t > s), (t == s)]
    for lvl in range(6):
        sz = 1 << lvl
        mats.append(((t // sz) % 2 == 1) & ((s // sz) == (t // sz) - 1))
    mats.append(np.zeros((CH, CH), bool))
    return np.stack(mats).astype(np.float32)


def _adamw(w, g, m, v):
    m = ADAM_B1 * m + (1.0 - ADAM_B1) * g
    v = ADAM_B2 * v + (1.0 - ADAM_B2) * (g * g)
    m_hat = m / (1.0 - ADAM_B1 ** ADAM_STEP)
    v_hat = v / (1.0 - ADAM_B2 ** ADAM_STEP)
    delta = -ADAM_LR * (m_hat / (jnp.sqrt(v_hat) + ADAM_EPS) + ADAM_WD * w)
    return delta, m, v


CT = 128
RB = 64
WIN = RB + 32


def _conv_fwd(pm, ck, T, S):
    nb = T // S

    def body(val_ref, gate_ref, ck_ref, out_ref, ubuf):
        ubuf[0:32, :] = jnp.zeros((32, CT), F32)
        ubuf[32:, :] = val_ref[...] * _sigmoid(gate_ref[...])

        def blk(rb, carry):
            base = pl.multiple_of(rb * RB, RB)
            win = ubuf[pl.ds(base, WIN), :]
            acc = jnp.zeros((RB, CT), F32)
            for j in range(CW):
                acc = acc + ck_ref[j:j + 1, :] * pltpu.roll(win, (WIN - (2 + j)) % WIN, 0)[0:RB, :]
            out_ref[pl.ds(base, RB), :] = acc
            return carry

        lax.fori_loop(0, S // RB, blk, 0)

    return pl.pallas_call(
        body, name="conv_fwd", grid=(D // CT, nb),
        in_specs=[pl.BlockSpec((S, CT), lambda ct, b: (b, ct)),
                  pl.BlockSpec((S, CT), lambda ct, b: (b, D // CT + ct)),
                  pl.BlockSpec((32, CT), lambda ct, b: (0, ct))],
        out_specs=pl.BlockSpec((S, CT), lambda ct, b: (b, ct)),
        out_shape=jax.ShapeDtypeStruct((T, D), F32),
        scratch_shapes=[pltpu.VMEM((S + 32, CT), F32)],
        compiler_params=_cparams(("parallel", "arbitrary")),
    )(pm, pm, ck)


def _conv_bwd(pm, duc, ck, T, S):
    nb = T // S

    def body(val_ref, gate_ref, duc_ref, ck_ref, dval_ref, dgate_ref, dck_ref, ubuf, dbuf, acc):
        b = pl.program_id(1)
        ubuf[0:32, :] = jnp.zeros((32, CT), F32)
        ubuf[32:, :] = val_ref[...] * _sigmoid(gate_ref[...])
        dbuf[0:S, :] = duc_ref[...]
        dbuf[S:, :] = jnp.zeros((32, CT), F32)
        acc[...] = jnp.zeros_like(acc)

        def blk(rb, carry):
            base = pl.multiple_of(rb * RB, RB)
            uwin = ubuf[pl.ds(base, WIN), :]
            dwin = dbuf[pl.ds(base, WIN), :]
            dblk = dwin[0:RB, :]
            du = jnp.zeros((RB, CT), F32)
            for j in range(CW):
                du = du + ck_ref[j:j + 1, :] * pltpu.roll(dwin, (WIN - (CW - 1 - j)) % WIN, 0)[0:RB, :]
                ush = pltpu.roll(uwin, (WIN - (2 + j)) % WIN, 0)[0:RB, :]
                acc[j] += jnp.sum((dblk * ush).reshape(RB // 8, 8, CT), axis=0)
            val = val_ref[pl.ds(base, RB), :]
            sg = _sigmoid(gate_ref[pl.ds(base, RB), :])
            dval_ref[pl.ds(base, RB), :] = (du * sg).astype(BF16)
            dgate_ref[pl.ds(base, RB), :] = (du * val * sg * (1.0 - sg)).astype(BF16)
            return carry

        lax.fori_loop(0, S // RB, blk, 0)

        @pl.when(b == 0)
        def _():
            dck_ref[...] = jnp.zeros_like(dck_ref)
        for j in range(CW):
            dck_ref[j:j + 1, :] += jnp.sum(acc[j], axis=0, keepdims=True)

    return pl.pallas_call(
        body, name="conv_bwd", grid=(D // CT, nb),
        in_specs=[pl.BlockSpec((S, CT), lambda ct, b: (b, ct)),
                  pl.BlockSpec((S, CT), lambda ct, b: (b, D // CT + ct)),
                  pl.BlockSpec((S, CT), lambda ct, b: (b, ct)),
                  pl.BlockSpec((32, CT), lambda ct, b: (0, ct))],
        out_specs=[pl.BlockSpec((S, CT), lambda ct, b: (b, ct)),
                   pl.BlockSpec((S, CT), lambda ct, b: (b, ct)),
                   pl.BlockSpec((32, CT), lambda ct, b: (0, ct))],
        out_shape=[jax.ShapeDtypeStruct((T, D), BF16), jax.ShapeDtypeStruct((T, D), BF16),
                   jax.ShapeDtypeStruct((32, D), F32)],
        scratch_shapes=[pltpu.VMEM((S + 32, CT), F32), pltpu.VMEM((S + 32, CT), F32), pltpu.VMEM((32, 8, CT), F32)],
        compiler_params=_cparams(("parallel", "arbitrary")),
    )(pm, pm, duc, ck)


def _scan_fwd(ins, cm, nb, S):
    nc = S // CH
    nch = nb * NH
    blk = pl.BlockSpec((nb, CH, D), lambda i: (0, i, 0))
    hblk = pl.BlockSpec((nb, NH, None, HN, HN), lambda i: (0, 0, i, 0, 0))

    def body(r_ref, lw_ref, k_ref, v_ref, kk_ref, b_ref, cm_ref, o_ref, hs_ref, st):
        @pl.when(pl.program_id(0) == 0)
        def _():
            st[...] = jnp.zeros_like(st)
        s0 = [st[j] for j in range(nch)]
        for j in range(nch):
            hs_ref[j // NH, j % NH] = s0[j]
        vals = [jnp.concatenate([ref[q] for q in range(nb)], axis=1) for ref in (r_ref, lw_ref, k_ref, v_ref, kk_ref, b_ref)]
        o, s1 = _chunk(s0, *vals, cm_ref[...])
        for q in range(nb):
            o_ref[q] = o[:, q * D:(q + 1) * D]
        for j in range(nch):
            st[j] = s1[j]

    o, hs = pl.pallas_call(
        body, name="scan_fwd", grid=(nc,),
        in_specs=[blk] * 6 + [pl.BlockSpec(cm.shape, lambda i: (0, 0, 0))],
        out_specs=[blk, hblk],
        out_shape=[jax.ShapeDtypeStruct((nb, S, D), F32), jax.ShapeDtypeStruct((nb, NH, nc, HN, HN), F32)],
        scratch_shapes=[pltpu.VMEM((nch, HN, HN), F32)],
        compiler_params=_cparams(("arbitrary",)),
    )(*[a.reshape(nb, S, D) for a in ins], cm)
    return o.reshape(nb * S, D), hs


def _scan_bwd(ins, hs, do, cm, nb, S):
    nc = S // CH
    nch = nb * NH
    blk = pl.BlockSpec((nb, CH, D), lambda i: (0, nc - 1 - i, 0))
    hblk = pl.BlockSpec((nb, NH, None, HN, HN), lambda i: (0, 0, nc - 1 - i, 0, 0))

    def body(r_ref, lw_ref, k_ref, v_ref, kk_ref, b_ref, hs_ref, do_ref, cm_ref,
             dr_ref, dlw_ref, dk_ref, dv_ref, dkk_ref, db_ref, dst):
        @pl.when(pl.program_id(0) == 0)
        def _():
            dst[...] = jnp.zeros_like(dst)
        cmv = cm_ref[...]
        side = lambda ref: jnp.concatenate([ref[q] for q in range(nb)], axis=1)
        f = lambda s0, r, lw, k, v, kk, b: _chunk(s0, r, lw, k, v, kk, b, cmv)
        _, vjp = jax.vjp(f, [hs_ref[j // NH, j % NH] for j in range(nch)],
                         *[side(ref) for ref in (r_ref, lw_ref, k_ref, v_ref, kk_ref, b_ref)])
        ds0, *grads = vjp((side(do_ref), tuple(dst[j] for j in range(nch))))
        for j in range(nch):
            dst[j] = ds0[j]
        for ref, g in zip((dr_ref, dlw_ref, dk_ref, dv_ref, dkk_ref, db_ref), grads):
            for q in range(nb):
                ref[q] = g[:, q * D:(q + 1) * D]

    outs = pl.pallas_call(
        body, name="scan_bwd", grid=(nc,),
        in_specs=[blk] * 6 + [hblk, blk, pl.BlockSpec(cm.shape, lambda i: (0, 0, 0))],
        out_specs=[blk] * 6,
        out_shape=[jax.ShapeDtypeStruct((nb, S, D), F32)] * 6,
        scratch_shapes=[pltpu.VMEM((nch, HN, HN), F32)],
        compiler_params=_cparams(("arbitrary",)),
    )(*[a.reshape(nb, S, D) for a in ins], hs, do.reshape(nb, S, D), cm)
    return [a.reshape(nb * S, D) for a in outs]


def _ew(fn, name, ins, n_out, tm, out_dtype=F32):
    R, W = ins[0].shape[-2:]
    tm = min(tm, R)
    if R % tm:
        tm = R // 2
    assert R % tm == 0 and (tm % 16 == 0 or tm == R), (name, R, tm)

    def body(*refs):
        vals = fn(*[r[...] for r in refs[:len(ins)]])
        for r, v in zip(refs[len(ins):], vals):
            r[...] = v.astype(r.dtype)

    def spec(a):
        if a.ndim == 3:
            return pl.BlockSpec((a.shape[0], tm, W), lambda i: (0, i, 0))
        return pl.BlockSpec((tm, W), lambda i: (i, 0))

    return pl.pallas_call(
        body, name=name, grid=(R // tm,), in_specs=[spec(a) for a in ins],
        out_specs=[pl.BlockSpec((tm, W), lambda i: (i, 0))] * n_out,
        out_shape=[jax.ShapeDtypeStruct((R, W), out_dtype)] * n_out,
        compiler_params=_cparams(("parallel",)),
    )(*ins)


def _sum_slots(r):
    s = r[0]
    for j in range(1, r.shape[0]):
        s = s + r[j]
    return s


def _place():
    x, y, c = lax.axis_index("x"), lax.axis_index("y"), lax.axis_index("c")
    return x, y, c


def _flip(v, d):
    return 1 - v if d else v


CHIP_PEERS = ((1, 0), (0, 1), (1, 1))
DEV_PEERS = tuple((dx, dy, dc) for dx in (0, 1) for dy in (0, 1) for dc in (0, 1))[1:]


def _comm_call(name, ins, out_shapes, plan, n_rem, n_fwd=0):
    n_in = len(ins)

    def body(*refs):
        in_refs, out_refs = refs[:n_in], refs[n_in:n_in + len(out_shapes)]
        send_sems, recv_sems, loc_sems = refs[n_in + len(out_shapes):]
        loc, rem, *rest = plan(in_refs, out_refs, _place())
        fwd = rest[0] if rest else []
        assert len(rem) == n_rem and len(fwd) == n_fwd and len(loc) <= 2 * n_in, (name, len(loc), len(rem), len(fwd))

        def remote(i, s, d, peer):
            return pltpu.make_async_remote_copy(src_ref=s, dst_ref=d, send_sem=send_sems.at[i], recv_sem=recv_sems.at[i],
                                                device_id=peer, device_id_type=MESH)

        copies = [pltpu.make_async_copy(s, d, loc_sems.at[i]) for i, (s, d) in enumerate(loc)]
        rcopies = [remote(i, s, d, peer) for i, (s, d, peer) in enumerate(rem)]
        for cp in copies + rcopies:
            cp.start()
        landed = set()
        fcopies = []
        for i, (s, d, peer, k) in enumerate(fwd):
            if k not in landed:
                rcopies[k].wait_recv()
                landed.add(k)
            fcopies.append(remote(n_rem + i, s, d, peer))
            fcopies[-1].start()
        for k, cp in enumerate(rcopies):
            if k not in landed:
                cp.wait_recv()
        for cp in rcopies + fcopies:
            cp.wait_send()
        for cp in fcopies:
            cp.wait_recv()
        for cp in copies:
            cp.wait()

    return pl.pallas_call(
        body, name=name, in_specs=[ANY] * n_in, out_specs=[ANY] * len(out_shapes), out_shape=out_shapes,
        scratch_shapes=[pltpu.SemaphoreType.DMA((n_rem + n_fwd,)), pltpu.SemaphoreType.DMA((n_rem + n_fwd,)),
                        pltpu.SemaphoreType.DMA((2 * n_in,))],
        compiler_params=pltpu.CompilerParams(has_side_effects=True),
    )(*ins)


def _gather_plan(n_big, in_refs, out_refs, place):
    x, y, c = place
    chip, dev = 2 * x + y, 4 * x + 2 * y + c
    sib = (x, y, 1 - c)
    loc = [(in_refs[0], out_refs[0].at[dev])] + [(s, d.at[chip]) for s, d in zip(in_refs[1 + n_big:], out_refs[1 + n_big:])]
    rem = [(in_refs[0], out_refs[0].at[dev], (_flip(x, dx), _flip(y, dy), _flip(c, dc))) for dx, dy, dc in DEV_PEERS]
    fwd = []
    for s, d in zip(in_refs[1:1 + n_big], out_refs[1:1 + n_big]):
        for dx, dy in CHIP_PEERS:
            px, py = _flip(x, dx), _flip(y, dy)
            fwd.append((d.at[2 * px + py, c], d.at[2 * px + py, c], sib, len(rem)))
            rem.append((s.at[c], d.at[chip, c], (px, py, c)))
    for s, d in zip(in_refs[1 + n_big:], out_refs[1 + n_big:]):
        rem += [(s, d.at[chip], (_flip(x, dx), _flip(y, dy), c)) for dx, dy in CHIP_PEERS]
    return loc, rem, fwd


def _join_plan(in_refs, out_refs, place):
    x, y, c = place
    return [], [(s, d, (x, y, 1 - c)) for s, d in zip(in_refs, out_refs)]


def _scatter_plan(n_all, in_refs, out_refs, place):
    x, y, c = place
    chip, dev = 2 * x + y, 4 * x + 2 * y + c
    loc, rem = [], []
    for s, d in zip(in_refs[:n_all], out_refs[:n_all]):
        loc.append((s.at[dev], d.at[dev]))
        for dx, dy, dc in DEV_PEERS:
            px, py, pc = _flip(x, dx), _flip(y, dy), _flip(c, dc)
            rem.append((s.at[4 * px + 2 * py + pc], d.at[dev], (px, py, pc)))
    for s, d in zip(in_refs[n_all:], out_refs[n_all:]):
        for dx, dy in CHIP_PEERS:
            px, py = _flip(x, dx), _flip(y, dy)
            rem.append((s.at[2 * px + py], d.at[chip], (px, py, c)))
    return loc, rem


def _bshape(a, nb):
    return a.reshape(nb, 1, a.shape[-1])


def _with_prev(cur, before, tiles_per_seq):
    first = pl.program_id(0) % tiles_per_seq == 0
    row0 = jnp.where(first, 0.0, before[before.shape[0] - 1:, :])
    rid = lax.broadcasted_iota(jnp.int32, cur.shape, 0)
    return jnp.where(rid == 0, row0, pltpu.roll(cur, 1, 0))


def _with_next(cur, after, tiles_per_seq):
    last = pl.program_id(0) % tiles_per_seq == tiles_per_seq - 1
    n = cur.shape[0]
    row_n = jnp.where(last, 0.0, after[0:1, :])
    rid = lax.broadcasted_iota(jnp.int32, cur.shape, 0)
    return jnp.where(rid == n - 1, row_n, pltpu.roll(cur, n - 1, 0))


def _local_step(x2d, tgt, mod, wmain, wlora, late_w, ck, w2, a2, small, nb, S, grads_hook):
    T = nb * S
    shift, scale, gate = (_bshape(mod[:, i * D:(i + 1) * D], nb) for i in range(3))
    G = jnp.asarray(np.arange(128)[:, None] == np.arange(D)[None, :] // HN, dtype=BF16)
    cm = jnp.asarray(_chunk_consts())
    ckp = jnp.pad(ck, ((0, 1), (0, 0)))
    zpad = jnp.zeros((64, D), F32)
    w2p = jnp.concatenate([w2, zpad], axis=0)
    a2p = jnp.concatenate([zpad, a2], axis=0)
    mu = small["rwkv_mu"]
    mu_r, mu_k, mu_v, mu_l = mu[:, 0:D], mu[:, D:2 * D], mu[:, 2 * D:3 * D], mu[:, 3 * D:]
    g4 = [mu_r, mu_k, mu_v, mu_l, small["rwkv_w0"], w2p, small["rwkv_a0"], a2p, small["rwkv_k_k"], small["rwkv_k_a"], G]
    g5 = [small["rwkv_gn_g"], small["rwkv_gn_b"], small["rwkv_r_k"], G]
    g3 = [small["conv_b"], small["conv_ln_g"], small["conv_ln_b"]]

    (h,), _, _ = _rows(lambda r, b, g: ([_s1(r[0], g[0], b[0], b[1])], [], []), "pre_fwd", T, S, 256,
                       [(x2d, D, 0)], [scale, shift], [small["norm_g"]], [(D, BF16)], [], [])
    skip = (DMAIN, lambda g, t: pl.multiple_of(g * t + jnp.where(g * t >= 6 * D, LORA, 0), LORA))
    if len(late_w) == 3:
        pm = _matmul(h, wmain, "nt", "proj_main", min(T, 1024), 1024, D, b_rows=skip)
        wco, wro, wo = late_w
    else:
        pm, *landed = _matmul(h, wmain, "nt", "proj_main", min(T, 1024), 1024, D, ride=late_w[0], b_rows=skip)
        wco, wro, wo = late_w[1](landed)
    plo = _matmul(h, wlora, "nt", "proj_lora", 512, LORA, D)
    uc = _conv_fwd(pm, ckp, T, S)
    def conv_post(r, b, g):
        u = _s3(r[0], r[1], *g[:3])
        return [u, _dot(u.astype(BF16), g[3])], [], []

    (uo, yc), _, _ = _rows(conv_post, "conv_post_fwd", T, S, 256, [(uc, D, 0), (pm, D, 2)], [], g3 + [wco],
                           [(D, BF16), (D, F32)], [], [])
    rows4 = [(pm, D, 3), (pm, D, 4), (pm, D, 5), (plo, LORA, 0),
             (pm, D, 3, "prev"), (pm, D, 4, "prev"), (pm, D, 5, "prev"), (plo, LORA, 0, "prev")]
    tps4 = S // 128

    def shifted4(r, tps=tps4):
        return list(r[:4]) + [_with_prev(r[i], r[4 + i], tps) for i in range(4)]

    sc_in, _, _ = _rows(lambda r, b, g: (list(_s4(*shifted4(r, S // 256), *g)), [], []), "rwkv_pre_fwd", T, S, 256,
                        rows4, [], g4, [(D, F32)] * 6, [], [])
    o, hs = _scan_fwd(sc_in, cm, nb, S)
    rows5 = [(o, D, 0), (sc_in[0], D, 0), (sc_in[2], D, 0), (sc_in[3], D, 0), (pm, D, 6)]
    def rwkv_post(r, b, g):
        u = _s5(*r, *g[:4])
        return [u, _dot(u.astype(BF16), g[4])], [], []

    (o2, yr), _, _ = _rows(rwkv_post, "rwkv_post_fwd", T, S, 256, rows5, [], g5 + [wro], [(D, BF16), (D, F32)], [], [])
    rows6 = [(yc, D, 0), (yr, D, 0), (pm, D, 7), (pm, D, 8)]
    def tail(r, b, g):
        yc_, yr_, gc_, gr_, x_, tgt_ = r
        m_ = _s6(yc_, yr_, gc_, gr_)
        out_ = _dot(m_.astype(BF16), g[1])
        loss, (dx, dout_, dgate, dfg) = jax.value_and_grad(_s7, argnums=(0, 1, 3, 4))(x_, out_, tgt_, b[0], g[0])
        dout_b = dout_.astype(BF16)
        _, vjp = jax.vjp(_s6, yc_, yr_, gc_, gr_)
        dyc_, dyr_, dgc_, dgr_ = vjp(_dot_nt(dout_b, g[1]))
        return [dx, dout_b, m_, dyc_, dyr_, dgc_, dgr_], [dgate], [dfg, jnp.full((1, 128), loss, F32)]

    (dx_res, dout, m, dyc, dyr, dgc, dgr), (dgate,), (d_final_g, loss_v) = _rows(
        tail, "tail", T, S, 256, rows6 + [(x2d, D, 0), (tgt, D, 0)], [gate], [small["final_g"], wo],
        [(D, F32)] + [(D, BF16)] * 6, [D], [(1, D), (1, 128)])

    d_wo = _matmul(m, dout, "tn", "d_w_out", 512, 1024, T)
    d_wco = _matmul(uo, dyc, "tn", "d_w_conv_out", 512, 1024, T)
    d_wro = _matmul(o2, dyr, "tn", "d_w_rwkv_out", 512, 1024, T)
    def conv_post_bwd(r, b, g):
        _, vjp = jax.vjp(_s3, r[0], r[1], *g[:3])
        duc, dog, dcb, dlg, dlb = vjp(_dot_nt(r[2], g[3]))
        return [duc, dog], [], [dcb, dlg, dlb]

    (duc, dcog), _, (d_cb, d_lg, d_lb) = _rows(conv_post_bwd, "conv_post_bwd", T, S, 256,
                                               [(uc, D, 0), (pm, D, 2), (dyc, D, 0)], [], g3 + [wco],
                                               [(D, F32), (D, BF16)], [], [(1, D)] * 3)
    dval, dgt, d_ckp = _conv_bwd(pm, duc, ckp, T, S)

    def rwkv_post_bwd(r, b, g):
        _, vjp = jax.vjp(lambda *z: _s5(*z, g[3]), *r[:5], *g[:3])
        res = vjp(_dot_nt(r[5], g[4]))
        return list(res[:5]), [], list(res[5:8])

    (do, dr_b, dk_b, dv_b, drog), _, (d_gg, d_gb, d_rk) = _rows(
        rwkv_post_bwd, "rwkv_post_bwd", T, S, 256, rows5 + [(dyr, D, 0)], [], g5 + [wro],
        [(D, F32)] * 4 + [(D, BF16)], [], [(1, D)] * 3)
    dsc = _scan_bwd(sc_in, hs, do, cm, nb, S)

    def rwkv_pre_bwd(r, b, g):
        _, vjp = jax.vjp(lambda *z: _s4(*z, g[10]), *shifted4(r), *g[:10])
        ct = (r[8] + r[14], r[9], r[10] + r[15], r[11] + r[16], r[12], r[13])
        res = vjp(ct)
        return list(res[:8]), [], list(res[8:18])

    rows4b = rows4 + [(a, D, 0) for a in dsc] + [(dr_b, D, 0), (dk_b, D, 0), (dv_b, D, 0)]
    gshapes = [(1, D), (1, D), (1, D), (1, LORA), (1, D), (LORA, D), (1, D), (LORA, D), (1, D), (1, D)]
    dts, _, gts = _rows(rwkv_pre_bwd, "rwkv_pre_bwd", T, S, 128, rows4b, [], g4,
                        [(D, BF16)] * 3 + [(LORA, BF16)] + [(D, BF16)] * 3 + [(LORA, BF16)], [], gshapes)
    dr0, dk0, dv0, dl0, dpr, dpk, dpv, dpl = dts
    d_mu_r, d_mu_k, d_mu_v, d_mu_l, d_w0, d_w2p, d_a0, d_a2p, d_kk, d_ka = gts

    def assemble(r, b, g):
        r = [z.astype(F32) for z in r]
        sh = [_with_next(r[10 + i], r[14 + i], tps4) for i in range(4)]
        main = jnp.concatenate([r[0], r[1], r[2], r[3] + sh[0], r[4] + sh[1], r[5] + sh[2], r[6], r[7], r[8]], axis=1)
        return [main, r[9] + sh[3]], [], []

    rows_a = [(dval, D, 0), (dgt, D, 0), (dcog, D, 0), (dr0, D, 0), (dk0, D, 0), (dv0, D, 0), (drog, D, 0), (dgc, D, 0),
              (dgr, D, 0), (dl0, LORA, 0), (dpr, D, 0), (dpk, D, 0), (dpv, D, 0), (dpl, LORA, 0),
              (dpr, D, 0, "next"), (dpk, D, 0, "next"), (dpv, D, 0, "next"), (dpl, LORA, 0, "next")]
    (dpm, dplo), _, _ = _rows(assemble, "assemble_dp", T, S, 128, rows_a, [], [], [(DMAIN, BF16), (LORA, BF16)], [], [])
    RW = D // NCHIP
    c_i = lax.axis_index("c")
    halved = [g.reshape(NCHIP, 2, RW // 2, D).transpose(1, 0, 2, 3).reshape(2, NCHIP * RW // 2, D) for g in (d_wco, d_wro, d_wo)]
    halved += [g.reshape(-1, NCHIP, 2, RW // 2).transpose(2, 1, 0, 3).reshape(2, -1, RW // 2)
               for g in (d_ckp[:CW], d_w2p[:64], d_a2p[64:])]
    h_keep, h_send = (lax.dynamic_slice_in_dim(h, k * (D // 2), D // 2, axis=1) for k in (c_i, 1 - c_i))
    tk = T
    send = [_matmul(h_send, dpm, "tn", "d_w_main_send", D // 2, 1024, tk, out_t=True)[0],
            _matmul(h_send, dplo, "tn", "d_w_lora_send", D // 2, LORA, tk, out_t=True)[0]]
    keep = [None, _matmul(h_keep, dplo, "tn", "d_w_lora_keep", D // 2, LORA, tk, out_t=True)[0]]

    def to_sibling(in_refs, out_refs, place):
        x, y, c = place
        return [], [(s if i < 2 else s.at[1 - c], d, (x, y, 1 - c)) for i, (s, d) in enumerate(zip(in_refs, out_refs))]

    got_shapes = [jax.ShapeDtypeStruct(t.shape, F32) for t in send] + [jax.ShapeDtypeStruct(t.shape[1:], F32) for t in halved]
    d_w_keep, *got_h = _matmul(h_keep, dpm, "tn", "d_w_main_keep", D // 2, 1024, tk, out_t=True,
                               ride=(send + halved, got_shapes, to_sibling, len(got_shapes)))
    keep[0] = d_w_keep[0]

    def own_half_plus(both, q):
        return [jnp.where(lax.axis_index("c") == 0, both[0], both[1]) + q]

    chip_part = [_ew(own_half_plus, "chip_sum_%d" % (2 + i), [g, got_h[2 + i]], 1, 1024, BF16)[0]
                 for i, g in enumerate(halved)]

    def sum_body(p_ref, q_ref, *rest):
        rest[-1][...] = (p_ref[...] + q_ref[...]).astype(BF16)

    blk = pl.BlockSpec((1024, D // 2), lambda i: (i, 0))
    d_win_h = pl.pallas_call(
        sum_body, name="chip_sum_w_in", grid=(DMAIN // 1024,), in_specs=[blk, blk],
        out_specs=pl.BlockSpec((pl.Element(1024), pl.Element(D // 2)), lambda i: (skip[1](i, 1024), 0)),
        out_shape=jax.ShapeDtypeStruct((DMAIN + LORA, D // 2), BF16), compiler_params=_cparams(("parallel",)),
    )(keep[0], got_h[0])
    lora_blk = pl.BlockSpec((LORA, D // 2), lambda i: (0, 0))
    d_win_h = pl.pallas_call(
        sum_body, name="chip_sum_w_lora", grid=(1,), in_specs=[lora_blk, lora_blk, ANY],
        out_specs=pl.BlockSpec((LORA, D // 2), lambda i: (6 * D // LORA, 0)),
        out_shape=jax.ShapeDtypeStruct((DMAIN + LORA, D // 2), BF16), input_output_aliases={2: 0},
        compiler_params=_cparams(("arbitrary",)),
    )(keep[1], got_h[1], d_win_h)
    chip_part = [d_win_h] + chip_part
    dh_m, *got_big = _matmul(dpm, wmain, "nn", "d_h_main", 512, 1024, 3072, ride=grads_hook(chip_part), b_rows=skip)
    dh_l = _matmul(dplo, wlora, "nn", "d_h_lora", 512, 1024, LORA)

    def pre_bwd(r, b, g):
        _, vjp = jax.vjp(_s1, r[0], g[0], b[0], b[1])
        dx, dg, dscale, dshift = vjp(r[1] + r[2])
        return [dx + r[3]], [dscale, dshift], [dg]

    (gx,), (dscale, dshift), (d_ng,) = _rows(pre_bwd, "pre_bwd", T, S, 256,
                                             [(x2d, D, 0), (dh_m, D, 0), (dh_l, D, 0), (dx_res, D, 0)],
                                             [scale, shift], [small["norm_g"]], [(D, F32)], [D, D], [(1, D)])
    dmod = jnp.concatenate([dshift, dscale, dgate], axis=-1).reshape(nb, 3 * D)
    d_small = {"norm_g": d_ng, "conv_b": d_cb, "conv_ln_g": d_lg, "conv_ln_b": d_lb,
               "rwkv_mu": jnp.concatenate([d_mu_r, d_mu_k, d_mu_v, d_mu_l], axis=1),
               "rwkv_w0": d_w0, "rwkv_a0": d_a0, "rwkv_k_k": d_kk, "rwkv_k_a": d_ka, "rwkv_r_k": d_rk,
               "rwkv_gn_g": d_gg, "rwkv_gn_b": d_gb, "final_g": d_final_g}
    return loss_v[0, 0], gx, dmod, got_big, d_small


def _step(a):
    nb, S, _ = a["x"].shape
    T = nb * S
    x_i, y_i, c_i = _place()
    chip = 2 * x_i + y_i
    w_in_t, m_w_in_t, v_w_in_t = (jnp.transpose(a[p + "w_in"][0]) for p in ("", "m_", "v_"))
    WS = w_in_t.shape[0]
    small_w = {n: a[n].reshape(1, sz) for n, sz in SMALL}

    def halves(t):
        return t.reshape(2, t.shape[0] // 2, t.shape[1])

    g_ins = [a["c"], halves(w_in_t.astype(BF16)), a["conv_k"][0], a["rwkv_w2"][0], a["rwkv_a2"][0]]
    g_out = [jax.ShapeDtypeStruct((NDEV,) + g_ins[0].shape, F32)]
    g_out += [jax.ShapeDtypeStruct((NCHIP,) + t.shape, t.dtype) for t in g_ins[1:]]
    c_all, win_g, ck_g, w2_g, a2_g = _comm_call(
        "gather_weights", g_ins, g_out, functools.partial(_gather_plan, 1), 7 + 3 * 4, 3)
    c_all = c_all.reshape(NDEV * nb, D)
    win_t = lax.dynamic_update_index_in_dim(win_g, g_ins[1], chip, 0).reshape(NCHIP * WS, D)
    late = [a[n][0].astype(BF16) for n in ("w_conv_out", "w_rwkv_out", "w_out")]

    def late_plan(in_refs, out_refs, place):
        x, y, c = place
        return [], [(s, d.at[2 * x + y], (_flip(x, dx), _flip(y, dy), c))
                    for s, d in zip(in_refs, out_refs) for dx, dy in CHIP_PEERS]

    def late_finish(landed):
        return [lax.dynamic_update_index_in_dim(g, own, chip, 0).reshape(D, D) for g, own in zip(landed, late)]

    late_w = ((late, [jax.ShapeDtypeStruct((NCHIP,) + t.shape, BF16) for t in late], late_plan, 9), late_finish)
    wmain = win_t
    wlora = win_t[6 * D:6 * D + LORA]
    ck = jnp.concatenate([ck_g[j] for j in range(NCHIP)], axis=1)
    w2 = jnp.concatenate([w2_g[j] for j in range(NCHIP)], axis=1)
    a2 = jnp.concatenate([a2_g[j] for j in range(NCHIP)], axis=1)

    ada_w = a["ada_w"][0]
    MW = ada_w.shape[1]
    ada_b_loc = lax.dynamic_slice(a["ada_b"], (0, chip * MW), (1, MW))

    def mod_body(c_ref, w_ref, b_ref, o_ref):
        o_ref[...] = _dot(_silu(c_ref[...]), w_ref[...], HI) + b_ref[...]

    modp = pl.pallas_call(mod_body, name="ada_mod", out_shape=jax.ShapeDtypeStruct((NDEV * nb, MW), F32),
                          compiler_params=_cparams())(c_all, ada_w, ada_b_loc)
    (mod_g,) = _comm_call("scatter_mod", [modp.reshape(NDEV, nb, MW)],
                          [jax.ShapeDtypeStruct((NDEV, nb, MW), F32)],
                          functools.partial(_scatter_plan, 1), 7)
    mod = mod_g.reshape(NCHIP, 2, nb, MW)
    mod = mod[:, 0].transpose(1, 0, 2).reshape(nb, NCHIP * MW)

    RW = D // NCHIP
    sh_s = []

    def grads_hook(chip_part):
        sh_s.append(chip_part[0].reshape(NCHIP, WS, D // 2))
        sh_s.extend(t.reshape(NCHIP, RW // 2, D) for t in chip_part[1:4])
        sh_s.extend(t.reshape(NCHIP, -1, RW // 2) for t in chip_part[4:])
        return (sh_s, [jax.ShapeDtypeStruct(t.shape, t.dtype) for t in sh_s], functools.partial(_scatter_plan, 0),
                3 * len(sh_s))

    loss_p, gx, dmod, got_big, d_small = _local_step(
        a["x"].reshape(T, D), a["loss_target"].reshape(T, D), mod, wmain, wlora, late_w, ck, w2, a2, small_w, nb, S,
        grads_hook)
    loss = lax.psum(loss_p, ("x", "y", "c"))

    d_small["ada_b"] = _colsum(dmod)
    small_vec = jnp.concatenate([d_small[n] for n, _ in SMALL], axis=1)
    dmod_s = dmod.reshape(nb, NCHIP, MW).transpose(1, 0, 2)
    dmod_s = jnp.repeat(dmod_s, 2, axis=0)
    small_s = jnp.broadcast_to(small_vec[None], (NDEV, 1, NSMALL))
    got = _comm_call("scatter_small", [dmod_s, small_s], [jax.ShapeDtypeStruct(t.shape, F32) for t in (dmod_s, small_s)],
                     functools.partial(_scatter_plan, 2), 14)
    dmod_all, small_all = got[0].reshape(NDEV * nb, MW), got[1].reshape(NDEV, NSMALL)

    def shard_sum(recv, sent):
        chip_i = 2 * lax.axis_index("x") + lax.axis_index("y")
        s = None
        for j in range(NCHIP):
            t = jnp.where(chip_i == j, sent[j], recv[j]).astype(F32)
            s = t if s is None else s + t
        return [s]

    fin = [_ew(shard_sum, "shard_sum_%d" % i, [t, sh_s[i]], 1, 128)[0] for i, t in enumerate(got_big)]
    oth = _comm_call("join_halves", fin, [jax.ShapeDtypeStruct(t.shape, F32) for t in fin], _join_plan, len(fin))

    outs = {}

    def upd_halves(name, mine, other):
        shp = a[name].shape
        R, W = 2 * mine.shape[0], mine.shape[1]
        tm = 128
        nh = R // 2 // tm

        def body(w_ref, m_ref, v_ref, f_ref, o_ref, g_ref, d_ref, m2_ref, v2_ref):
            g = jnp.where(pl.program_id(0) // nh == lax.axis_index("c"), f_ref[...], o_ref[...])
            g_ref[...] = g
            d_ref[...], m2_ref[...], v2_ref[...] = _adamw(w_ref[...], g, m_ref[...], v_ref[...])

        full = pl.BlockSpec((None, tm, W), lambda i: (0, i, 0))
        half = pl.BlockSpec((tm, W), lambda i: (i % nh, 0))
        assert shp == (1, R, W)
        outs[name] = pl.pallas_call(
            body, name="adamw_" + name, grid=(R // tm,), in_specs=[full] * 3 + [half] * 2, out_specs=[full] * 4,
            out_shape=[jax.ShapeDtypeStruct(shp, F32)] * 4, compiler_params=_cparams(("parallel",)),
        )(*[a[p + name] for p in ("", "m_", "v_")], mine, other)

    for name, f, o in zip(("w_conv_out", "w_rwkv_out", "w_out"), fin[1:4], oth[1:4]):
        upd_halves(name, f, o)

    def upd_col_halves(name, wmv, mine, other, tw):
        R, W = wmv[0].shape

        def body(w_ref, m_ref, v_ref, f_ref, o_ref, g_ref, d_ref, m2_ref, v2_ref):
            first = lax.axis_index("c") == 0
            g = jnp.concatenate([jnp.where(first, f_ref[...], o_ref[...]), jnp.where(first, o_ref[...], f_ref[...])],
                                axis=1)
            g_ref[...] = g
            d_ref[...], m2_ref[...], v2_ref[...] = _adamw(w_ref[...], g, m_ref[...], v_ref[...])

        full = pl.BlockSpec((tw, W), lambda i: (i, 0))
        half = pl.BlockSpec((tw, W // 2), lambda i: (i, 0))
        return pl.pallas_call(
            body, name="adamw_" + name, grid=(R // tw,), in_specs=[full] * 3 + [half] * 2, out_specs=[full] * 4,
            out_shape=[jax.ShapeDtypeStruct((R, W), F32)] * 4, compiler_params=_cparams(("parallel",)),
        )(*wmv, mine, other)

    res = upd_col_halves("w_in", (w_in_t, m_w_in_t, v_w_in_t), fin[0], oth[0], WS // 4)
    outs["w_in"] = [jnp.transpose(r)[None] for r in res]
    for name, f, o in zip(("conv_k", "rwkv_w2", "rwkv_a2"), fin[4:], oth[4:]):
        res = upd_col_halves(name, [a[p + name][0] for p in ("", "m_", "v_")], f, o, f.shape[0])
        outs[name] = [r[None] for r in res]

    def adaw_body(c_ref, dm_ref, w_ref, m_ref, v_ref, g_ref, d_ref, m2_ref, v2_ref):
        g = _dot_tn(_silu(c_ref[...]), dm_ref[...], HI)
        g_ref[...] = g
        d_ref[...], m2_ref[...], v2_ref[...] = _adamw(w_ref[...], g, m_ref[...], v_ref[...])

    res = pl.pallas_call(adaw_body, name="adamw_ada_w", out_shape=[jax.ShapeDtypeStruct((D, MW), F32)] * 4,
                         compiler_params=_cparams())(c_all, dmod_all, ada_w, a["m_ada_w"][0], a["v_ada_w"][0])
    outs["ada_w"] = [r.reshape(a["ada_w"].shape) for r in res]

    def small_body(gs_ref, *refs):
        g_all = _sum_slots(gs_ref[...])
        ins, out_refs = refs[:3 * len(SMALL)], refs[3 * len(SMALL):]
        off = 0
        for i, (_, sz) in enumerate(SMALL):
            w_ref, m_ref, v_ref = ins[3 * i:3 * i + 3]
            g = g_all[:, off:off + sz]
            res4 = (g, *_adamw(w_ref[...], g, m_ref[...], v_ref[...]))
            for r, val in zip(out_refs[4 * i:4 * i + 4], res4):
                r[...] = val
            off += sz

    small_ins = [a[p + n].reshape(1, sz) for n, sz in SMALL for p in ("", "m_", "v_")]
    res = pl.pallas_call(
        small_body, name="adamw_small", compiler_params=_cparams(),
        out_shape=[jax.ShapeDtypeStruct((1, sz), F32) for _, sz in SMALL for _ in range(4)],
    )(small_all.reshape(NDEV, 1, NSMALL), *small_ins)
    for i, (n, _) in enumerate(SMALL):
        outs[n] = [r.reshape(a[n].shape) for r in res[4 * i:4 * i + 4]]

    return (loss, gx.reshape(nb, S, D), *[outs[n][0] for n in WEIGHTS], *[outs[n][1] for n in WEIGHTS],
            *[outs[n][2] for n in WEIGHTS], *[outs[n][3] for n in WEIGHTS])


def _colsum(dmod):
    def body(d_ref, o_ref):
        o_ref[...] = jnp.sum(d_ref[...], axis=0, keepdims=True)
    return pl.pallas_call(body, name="ada_b_rowsum", out_shape=jax.ShapeDtypeStruct((1, dmod.shape[1]), F32),
                          compiler_params=_cparams())(dmod)


def kernel(x, c, ada_w, ada_b, norm_g, w_in, conv_k, conv_b, conv_ln_g, conv_ln_b, w_conv_out, rwkv_mu, rwkv_w0, rwkv_w2, rwkv_a0, rwkv_a2, rwkv_k_k, rwkv_k_a, rwkv_r_k, rwkv_gn_g, rwkv_gn_b, w_rwkv_out, w_out, final_g, loss_target, m_ada_w, m_ada_b, m_norm_g, m_w_in, m_conv_k, m_conv_b, m_conv_ln_g, m_conv_ln_b, m_w_conv_out, m_rwkv_mu, m_rwkv_w0, m_rwkv_w2, m_rwkv_a0, m_rwkv_a2, m_rwkv_k_k, m_rwkv_k_a, m_rwkv_r_k, m_rwkv_gn_g, m_rwkv_gn_b, m_w_rwkv_out, m_w_out, m_final_g, v_ada_w, v_ada_b, v_norm_g, v_w_in, v_conv_k, v_conv_b, v_conv_ln_g, v_conv_ln_b, v_w_conv_out, v_rwkv_mu, v_rwkv_w0, v_rwkv_w2, v_rwkv_a0, v_rwkv_a2, v_rwkv_k_k, v_rwkv_k_a, v_rwkv_r_k, v_rwkv_gn_g, v_rwkv_gn_b, v_w_rwkv_out, v_w_out, v_final_g):
    return _step(dict(locals()))
```
